```python
import jax, jax.numpy as jnp
from jax import lax
import numpy as np

D_MODEL = 1024
BATCH = 8
SEQ = 2048
DEPTH = 1

D_CONV = D_MODEL // 2
CONV_WIDTH = 31
M_INNER = D_MODEL // 2
M_HEADS = 4
M_HEAD_DIM = M_INNER // M_HEADS
QK_CONV_WIDTH = 4
CHUNK = 64
N_GROUPS = 4
E_PER_GROUP = 8
N_EXPERTS = N_GROUPS * E_PER_GROUP
TOP_K = 2
D_EXPERT = D_MODEL // 2
EXPERT_ROWS = 128
RMS_EPS = 1e-6
LN_EPS = 1e-5
IN_WIDTHS = (D_CONV, D_CONV, M_INNER, M_INNER, M_INNER, M_INNER, M_HEADS, M_HEADS, D_MODEL, D_MODEL)
D_IN = 2 * D_CONV + 4 * M_INNER + 2 * M_HEADS + 2 * D_MODEL

kernel_name = "hybrid_conformer_mlstm_hmoe_adaln"


def rmsnorm(x, g):
    xf = x.astype(jnp.float32)
    y = xf * lax.rsqrt(jnp.mean(xf * xf, axis=-1, keepdims=True) + RMS_EPS)
    return (y * g.astype(jnp.float32)).astype(x.dtype)


def layernorm(x, g, b):
    xf = x.astype(jnp.float32)
    mu = jnp.mean(xf, axis=-1, keepdims=True)
    var = jnp.mean(jnp.square(xf - mu), axis=-1, keepdims=True)
    y = (xf - mu) * lax.rsqrt(var + LN_EPS)
    return (y * g.astype(jnp.float32) + b.astype(jnp.float32)).astype(x.dtype)


def modulate(h, shift, scale):
    return h * (1 + scale[:, None, :]) + shift[:, None, :]


def causal_dwconv(u, w, b):
    width = w.shape[0]
    y = lax.conv_general_dilated(
        u, w.astype(u.dtype)[:, None, :], window_strides=(1,), padding=[(width - 1, 0)],
        dimension_numbers=("NWC", "WIO", "NWC"), feature_group_count=u.shape[-1])
    return y + b.astype(u.dtype)


def mlstm_chunkwise(q, k, v, i_pre, f_pre):
    B, S, NH, DH = q.shape
    NC = S // CHUNK
    q = q * (DH ** -0.5)
    lf = jax.nn.log_sigmoid(f_pre)
    li = i_pre

    def to_chunks(a):
        return a.reshape(B, NC, CHUNK, NH, DH).transpose(1, 0, 3, 2, 4)

    def gates_to_chunks(a):
        return a.reshape(B, NC, CHUNK, NH).transpose(1, 0, 3, 2)

    causal = jnp.tril(jnp.ones((CHUNK, CHUNK), dtype=bool))

    def step(carry, xs):
        C, n, m = carry
        qc, kc, vc, lic, lfc = xs
        bcum = jnp.cumsum(lfc, axis=-1)
        dmat = bcum[..., :, None] - bcum[..., None, :] + lic[..., None, :]
        dmat = jnp.where(causal, dmat, -jnp.inf)
        inter = bcum + m[..., None]
        m_t = jnp.maximum(jnp.max(dmat, axis=-1), inter)
        wts = jnp.exp(dmat - m_t[..., None])
        s_inter = jnp.exp(inter - m_t)
        qk = jnp.einsum("bhtd,bhsd->bhts", qc, kc) * wts
        num = jnp.einsum("bhts,bhsd->bhtd", qk, vc) + s_inter[..., None] * jnp.einsum("bhtd,bhde->bhte", qc, C)
        den = jnp.sum(qk, axis=-1) + s_inter * jnp.einsum("bhtd,bhd->bht", qc, n)
        h = num / jnp.maximum(jnp.abs(den), jnp.exp(-m_t))[..., None]
        b_last = bcum[..., -1]
        a = b_last[..., None] - bcum + lic
        m_new = jnp.maximum(b_last + m, jnp.max(a, axis=-1))
        wk = jnp.exp(a - m_new[..., None])
        sc = jnp.exp(b_last + m - m_new)
        C_new = sc[..., None, None] * C + jnp.einsum("bhs,bhsd,bhse->bhde", wk, kc, vc)
        n_new = sc[..., None] * n + jnp.einsum("bhs,bhsd->bhd", wk, kc)
        return (C_new, n_new, m_new), h

    init = (jnp.zeros((B, NH, DH, DH), jnp.float32),
            jnp.zeros((B, NH, DH), jnp.float32),
            jnp.zeros((B, NH), jnp.float32))
    xs = (to_chunks(q), to_chunks(k), to_chunks(v), gates_to_chunks(li), gates_to_chunks(lf))
    _, h = lax.scan(step, init, xs)
    return h.transpose(1, 0, 3, 2, 4).reshape(B, S, NH, DH)


def hybrid_mixer(h, w_in, b_if, conv_dw_w, conv_dw_b, conv_ln_g, conv_ln_b, w_conv_out,
                 qk_conv_w, qk_conv_b, m_norm_g, w_m_out, w_out):
    B, S, _ = h.shape
    z = h @ w_in
    offs = np.cumsum(IN_WIDTHS)[:-1].tolist()
    glu_a, glu_b, q, k, v, o, i_pre, f_pre, g_a, g_b = jnp.split(z, offs, axis=-1)

    u = glu_a * jax.nn.sigmoid(glu_b)
    u = causal_dwconv(u, conv_dw_w, conv_dw_b)
    u = jax.nn.silu(layernorm(u, conv_ln_g, conv_ln_b))
    y_a = u @ w_conv_out

    qk = jax.nn.silu(causal_dwconv(jnp.concatenate([q, k], axis=-1), qk_conv_w, qk_conv_b))
    q, k = jnp.split(qk, 2, axis=-1)
    heads = lambda a: a.astype(jnp.float32).reshape(B, S, M_HEADS, M_HEAD_DIM)
    bif = b_if.astype(jnp.float32)
    i_g = i_pre.astype(jnp.float32) + bif[:M_HEADS]
    f_g = f_pre.astype(jnp.float32) + bif[M_HEADS:]
    hm = mlstm_chunkwise(heads(q), heads(k), heads(v), i_g, f_g)
    mu = jnp.mean(hm, axis=-1, keepdims=True)
    var = jnp.mean(jnp.square(hm - mu), axis=-1, keepdims=True)
    hm = (hm - mu) * lax.rsqrt(var + LN_EPS) * m_norm_g.astype(jnp.float32).reshape(M_HEADS, M_HEAD_DIM)
    hm = hm * jax.nn.sigmoid(heads(o))
    y_b = hm.reshape(B, S, M_INNER).astype(h.dtype) @ w_m_out

    merged = jax.nn.sigmoid(g_a) * y_a + jax.nn.sigmoid(g_b) * y_b
    return merged @ w_out


def hierarchical_moe(h, w_rg, b_rg, w_re, b_re, w_e_gate, w_e_up, w_e_down):
    T, D = h.shape
    hf = h.astype(jnp.float32)
    p_grp = jax.nn.softmax(hf @ w_rg.astype(jnp.float32) + b_rg.astype(jnp.float32), axis=-1)
    g_sel = jnp.argmax(p_grp, axis=-1)
    p_g = jnp.take_along_axis(p_grp, g_sel[:, None], axis=-1)
    le = (hf @ w_re.astype(jnp.float32) + b_re.astype(jnp.float32)).reshape(T, N_GROUPS, E_PER_GROUP)
    le_sel = jnp.take_along_axis(le, g_sel[:, None, None], axis=1)[:, 0]
    top_p, top_e = lax.top_k(jax.nn.softmax(le_sel, axis=-1), TOP_K)
    w_tok = p_g * top_p / jnp.sum(top_p, axis=-1, keepdims=True)
    e_idx = g_sel[:, None] * E_PER_GROUP + top_e

    A = T * TOP_K
    e_flat = e_idx.reshape(A).astype(jnp.int32)
    w_flat = w_tok.reshape(A)
    tok = jnp.arange(A, dtype=jnp.int32) // TOP_K
    order = jnp.argsort(e_flat)
    e_sorted = e_flat[order]
    tok_sorted = tok[order]
    counts = jnp.bincount(e_flat, length=N_EXPERTS)
    padded = (counts + EXPERT_ROWS - 1) // EXPERT_ROWS * EXPERT_ROWS
    starts = jnp.cumsum(counts) - counts
    pends = jnp.cumsum(padded)
    pstarts = pends - padded
    dest = pstarts[e_sorted] + jnp.arange(A, dtype=jnp.int32) - starts[e_sorted]
    n_groups_rows = -(-(A + N_EXPERTS * (EXPERT_ROWS - 1)) // EXPERT_ROWS)
    P = n_groups_rows * EXPERT_ROWS
    xs = jnp.zeros((P, D), h.dtype).at[dest].set(h[tok_sorted])
    grp_e = jnp.minimum(jnp.searchsorted(pends, jnp.arange(n_groups_rows) * EXPERT_ROWS, side="right"),
                        N_EXPERTS - 1)

    def expert_rows(args):
        xb, e = args
        return (jax.nn.silu(xb @ w_e_gate[e]) * (xb @ w_e_up[e])) @ w_e_down[e]

    ys = lax.map(expert_rows, (xs.reshape(n_groups_rows, EXPERT_ROWS, D), grp_e)).reshape(P, D)
    contrib = ys[dest].astype(jnp.float32) * w_flat[order][:, None]
    return jax.ops.segment_sum(contrib, tok_sorted, num_segments=T).astype(h.dtype)


def setup_inputs(seed: int = 0) -> dict:
    key = jax.random.key(seed)
    ks = jax.random.split(key, 32)
    L, D = DEPTH, D_MODEL

    def nrm(k, shape, scale):
        return jax.random.normal(k, shape, jnp.float32) * scale

    b_if = jnp.concatenate([
        nrm(ks[6], (L, M_HEADS), 0.1),
        jnp.linspace(3.0, 6.0, M_HEADS, dtype=jnp.float32)[None, :] + nrm(ks[7], (L, M_HEADS), 0.1)], axis=-1)
    return {
        "x": nrm(ks[0], (BATCH, SEQ, D), 1.0),
        "c": nrm(ks[1], (BATCH, D), 1.0),
        "w_ada": nrm(ks[2], (L, D, 6 * D), 0.5 * D ** -0.5),
        "b_ada": nrm(ks[3], (L, 6 * D), 0.02),
        "g_norm1": 1.0 + nrm(ks[4], (L, D), 0.02),
        "w_in": nrm(ks[5], (L, D, D_IN), D ** -0.5),
        "b_if": b_if,
        "conv_dw_w": nrm(ks[8], (L, CONV_WIDTH, D_CONV), CONV_WIDTH ** -0.5),
        "conv_dw_b": nrm(ks[9], (L, D_CONV), 0.02),
        "conv_ln_g": 1.0 + nrm(ks[10], (L, D_CONV), 0.02),
        "conv_ln_b": nrm(ks[11], (L, D_CONV), 0.02),
        "w_conv_out": nrm(ks[12], (L, D_CONV, D), D_CONV ** -0.5),
        "qk_conv_w": nrm(ks[13], (L, QK_CONV_WIDTH, 2 * M_INNER), QK_CONV_WIDTH ** -0.5),
        "qk_conv_b": nrm(ks[14], (L, 2 * M_INNER), 0.02),
        "m_norm_g": 1.0 + nrm(ks[15], (L, M_INNER), 0.02),
        "w_m_out": nrm(ks[16], (L, M_INNER, D), M_INNER ** -0.5),
        "w_out": nrm(ks[17], (L, D, D), D ** -0.5),
        "g_norm2": 1.0 + nrm(ks[18], (L, D), 0.02),
        "w_rg": nrm(ks[19], (L, D, N_GROUPS), D ** -0.5),
        "b_rg": nrm(ks[20], (L, N_GROUPS), 0.01),
        "w_re": nrm(ks[21], (L, D, N_EXPERTS), D ** -0.5),
        "b_re": nrm(ks[22], (L, N_EXPERTS), 0.01),
        "w_e_gate": nrm(ks[23], (L, N_EXPERTS, D, D_EXPERT), D ** -0.5),
        "w_e_up": nrm(ks[24], (L, N_EXPERTS, D, D_EXPERT), D ** -0.5),
        "w_e_down": nrm(ks[25], (L, N_EXPERTS, D_EXPERT, D), D_EXPERT ** -0.5),
        "g_final": 1.0 + nrm(ks[26], (D,), 0.02),
    }


def reference(x, c, w_ada, b_ada, g_norm1, w_in, b_if, conv_dw_w, conv_dw_b, conv_ln_g, conv_ln_b,
              w_conv_out, qk_conv_w, qk_conv_b, m_norm_g, w_m_out, w_out, g_norm2, w_rg, b_rg,
              w_re, b_re, w_e_gate, w_e_up, w_e_down, g_final):
    B, S, D = x.shape
    for l in range(DEPTH):
        mod = jax.nn.silu(c) @ w_ada[l] + b_ada[l]
        sh1, sc1, gt1, sh2, sc2, gt2 = jnp.split(mod, 6, axis=-1)
        h = modulate(rmsnorm(x, g_norm1[l]), sh1, sc1)
        y = hybrid_mixer(h, w_in[l], b_if[l], conv_dw_w[l], conv_dw_b[l], conv_ln_g[l], conv_ln_b[l],
                         w_conv_out[l], qk_conv_w[l], qk_conv_b[l], m_norm_g[l], w_m_out[l], w_out[l])
        x = x + gt1[:, None, :] * y
        h = modulate(rmsnorm(x, g_norm2[l]), sh2, sc2)
        y = hierarchical_moe(h.reshape(B * S, D), w_rg[l], b_rg[l], w_re[l], b_re[l],
                             w_e_gate[l], w_e_up[l], w_e_down[l]).reshape(B, S, D)
        x = x + gt2[:, None, :] * y
    return rmsnorm(x, g_final)
```

```python
import functools

import jax
import jax.numpy as jnp
from jax import lax
from jax.experimental import pallas as pl
from jax.experimental.pallas import tpu as pltpu

F32 = jnp.float32
BF16 = jnp.bfloat16
I32 = jnp.int32

M_HEADS = 4
CONV_WIDTH = 31
QK_CONV_WIDTH = 4
N_GROUPS = 4
E_PER_GROUP = 8
N_EXPERTS = N_GROUPS * E_PER_GROUP
TOP_K = 2
RMS_EPS = 1e-6
LN_EPS = 1e-5

LANES = 128
SUBLANES = 8
VMEM_LIMIT = 56 * 1024 * 1024

ADA_TN = 1024
INPROJ_TM = 256
CONV_TS = 256
CONV_HALO = 32
CONV_RC = 32
MLSTM_L = 128
MERGE_TM = 256
EXPERT_TM = 256
DISPATCH_TM = 256
COMBINE_TM = 256


def _sigmoid(v):
    return 1.0 / (1.0 + jnp.exp(-v))


def _log_sigmoid(v):
    return -(jnp.maximum(-v, 0.0) + jnp.log1p(jnp.exp(-jnp.abs(v))))


def _params(*sem):
    return pltpu.CompilerParams(dimension_semantics=sem, vmem_limit_bytes=VMEM_LIMIT)


def _ada_kernel(c_ref, w_ref, b_ref, o_ref):
    c = c_ref[...]
    s = c * _sigmoid(c)
    o_ref[...] = jnp.dot(s, w_ref[...], preferred_element_type=F32,
                         precision=lax.Precision.HIGHEST) + b_ref[...]


def _ada(c, w_ada, b_ada):
    B, D = c.shape
    N = w_ada.shape[1]
    return pl.pallas_call(
        _ada_kernel,
        out_shape=jax.ShapeDtypeStruct((B, N), F32),
        grid=(N // ADA_TN,),
        in_specs=[pl.BlockSpec((B, D), lambda j: (0, 0)),
                  pl.BlockSpec((D, ADA_TN), lambda j: (0, j)),
                  pl.BlockSpec((1, ADA_TN), lambda j: (0, j))],
        out_specs=pl.BlockSpec((B, ADA_TN), lambda j: (0, j)),
        compiler_params=_params("arbitrary"),
        name="ada",
    )(c, w_ada, b_ada.reshape(1, N))


def _inproj_kernel(x_ref, mod_ref, g_ref, wm_ref, wif_ref, wift_ref,
                   u_ref, qk_ref, v_ref, o_ref, sga_ref, sgb_ref, ifc_ref, ifr_ref):
    x = x_ref[...]
    shift = mod_ref[0, 0:1, :]
    scale = mod_ref[0, 1:2, :]
    ms = jnp.mean(x * x, axis=-1, keepdims=True)
    h = x * lax.rsqrt(ms + RMS_EPS) * g_ref[...]
    h = h * (1.0 + scale) + shift
    hb = h.astype(BF16)
    dc = u_ref.shape[1]
    d = sga_ref.shape[1]

    def seg(lo, hi):
        return jnp.dot(hb, wm_ref[:, lo:hi], preferred_element_type=F32)

    u_ref[...] = seg(0, dc) * _sigmoid(seg(dc, 2 * dc))
    qk_ref[...] = seg(2 * dc, 4 * dc)
    v_ref[...] = seg(4 * dc, 5 * dc)
    o_ref[...] = seg(5 * dc, 6 * dc)
    sga_ref[...] = _sigmoid(seg(6 * dc, 6 * dc + d))
    sgb_ref[...] = _sigmoid(seg(6 * dc + d, 6 * dc + 2 * d))
    ifc_ref[...] = jnp.dot(hb, wif_ref[...], preferred_element_type=F32)
    ifr_ref[...] = lax.dot_general(wift_ref[...], hb, (((1,), (1,)), ((), ())),
                                   preferred_element_type=F32)


def _inproj(x2, mod3, g1, w_main, w_if, w_ift, seq):
    T, D = x2.shape
    tm = INPROJ_TM
    dc = D // 2
    per_b = seq // tm
    row = lambda i: (i, 0)
    const = lambda i: (0, 0)
    return pl.pallas_call(
        _inproj_kernel,
        out_shape=[jax.ShapeDtypeStruct((T, dc), F32),
                   jax.ShapeDtypeStruct((T, 2 * dc), F32),
                   jax.ShapeDtypeStruct((T, dc), F32),
                   jax.ShapeDtypeStruct((T, dc), F32),
                   jax.ShapeDtypeStruct((T, D), F32),
                   jax.ShapeDtypeStruct((T, D), F32),
                   jax.ShapeDtypeStruct((T, LANES), F32),
                   jax.ShapeDtypeStruct((SUBLANES, T), F32)],
        grid=(T // tm,),
        in_specs=[pl.BlockSpec((tm, D), row),
                  pl.BlockSpec((1, 6, D), lambda i: (i // per_b, 0, 0)),
                  pl.BlockSpec((1, D), const),
                  pl.BlockSpec(w_main.shape, const),
                  pl.BlockSpec(w_if.shape, const),
                  pl.BlockSpec(w_ift.shape, const)],
        out_specs=[pl.BlockSpec((tm, dc), row),
                   pl.BlockSpec((tm, 2 * dc), row),
                   pl.BlockSpec((tm, dc), row),
                   pl.BlockSpec((tm, dc), row),
                   pl.BlockSpec((tm, D), row),
                   pl.BlockSpec((tm, D), row),
                   pl.BlockSpec((tm, LANES), row),
                   pl.BlockSpec((SUBLANES, tm), lambda i: (0, i))],
        compiler_params=_params("arbitrary"),
        name="inproj",
    )(x2, mod3, g1, w_main, w_if, w_ift)


def _conv_kernel(u_ref, w_ref, b_ref, lg_ref, lb_ref, wo_ref, y_ref, ubuf, cbuf):
    ts = u_ref.shape[1]
    halo = CONV_HALO

    @pl.when(pl.program_id(1) == 0)
    def _():
        ubuf[0:halo, :] = jnp.zeros((halo, ubuf.shape[1]), F32)

    ubuf[halo:halo + ts, :] = u_ref[0]
    off = halo - (CONV_WIDTH - 1)
    for r0 in range(0, ts, CONV_RC):
        acc = jnp.broadcast_to(b_ref[...], (CONV_RC, ubuf.shape[1]))
        for k in range(CONV_WIDTH):
            acc = acc + w_ref[k:k + 1, :] * ubuf[off + k + r0:off + k + r0 + CONV_RC, :]
        cbuf[r0:r0 + CONV_RC, :] = acc
    ubuf[0:halo, :] = ubuf[ts:ts + halo, :]

    a = cbuf[...]
    mu = jnp.mean(a, axis=-1, keepdims=True)
    ac = a - mu
    var = jnp.mean(ac * ac, axis=-1, keepdims=True)
    z = ac * lax.rsqrt(var + LN_EPS) * lg_ref[...] + lb_ref[...]
    z = z * _sigmoid(z)
    y_ref[0] = jnp.dot(z.astype(BF16), wo_ref[...], preferred_element_type=F32)


def _conv_branch(u3, w, b, lg, lb, wo):
    B, S, C = u3.shape
    D = wo.shape[1]
    ts = CONV_TS
    const = lambda bi, si: (0, 0)
    return pl.pallas_call(
        _conv_kernel,
        out_shape=jax.ShapeDtypeStruct((B, S, D), F32),
        grid=(B, S // ts),
        in_specs=[pl.BlockSpec((1, ts, C), lambda bi, si: (bi, si, 0)),
                  pl.BlockSpec(w.shape, const),
                  pl.BlockSpec((1, C), const),
                  pl.BlockSpec((1, C), const),
                  pl.BlockSpec((1, C), const),
                  pl.BlockSpec(wo.shape, const)],
        out_specs=pl.BlockSpec((1, ts, D), lambda bi, si: (bi, si, 0)),
        scratch_shapes=[pltpu.VMEM((ts + CONV_HALO, C), F32),
                        pltpu.VMEM((ts, C), F32)],
        compiler_params=_params("arbitrary", "arbitrary"),
        name="conv",
    )(u3, w, b, lg, lb, wo)


def _mlstm_kernel(qk_ref, v_ref, o_ref, ifc_ref, ifr_ref, cw_ref, cb_ref, bifc_ref, bifr_ref,
                  ng_ref, wo_ref, y_ref, qkbuf, cn_ref, m_ref, hbuf):
    L = qk_ref.shape[1]
    mi = v_ref.shape[2]
    dh = mi // M_HEADS
    halo = SUBLANES

    @pl.when(pl.program_id(1) == 0)
    def _():
        qkbuf[0:halo, :] = jnp.zeros((halo, qkbuf.shape[1]), F32)
        cn_ref[...] = jnp.zeros(cn_ref.shape, F32)
        m_ref[...] = jnp.zeros(m_ref.shape, F32)

    qkbuf[halo:halo + L, :] = qk_ref[0]
    off = halo - (QK_CONV_WIDTH - 1)
    y = jnp.broadcast_to(cb_ref[...], (L, qkbuf.shape[1]))
    for k in range(QK_CONV_WIDTH):
        y = y + cw_ref[k:k + 1, :] * qkbuf[off + k:off + k + L, :]
    y = y * _sigmoid(y)
    qkbuf[0:halo, :] = qkbuf[L:L + halo, :]

    ifr = ifr_ref[...] + bifr_ref[...]
    ifc = ifc_ref[...] + bifc_ref[...]
    lfr = _log_sigmoid(ifr)
    lfc = _log_sigmoid(ifc)
    rows = lax.broadcasted_iota(I32, (L, L), 0)
    cols = lax.broadcasted_iota(I32, (L, L), 1)
    causal = cols <= rows
    lower = causal.astype(F32)
    upper = (rows <= cols).astype(F32)
    bcum_c = jnp.dot(lower, lfc, preferred_element_type=F32, precision=lax.Precision.HIGHEST)
    bcum_r = jnp.dot(lfr, upper, preferred_element_type=F32, precision=lax.Precision.HIGHEST)

    lane = lax.broadcasted_iota(I32, (L, dh), 1)
    ones_col = jnp.where(lane == 0, 1.0, 0.0).astype(F32)
    vv = v_ref[0]
    oo = o_ref[0]
    scale = dh ** -0.5
    for hd in range(M_HEADS):
        q = y[:, hd * dh:(hd + 1) * dh] * scale
        kk = y[:, mi + hd * dh:mi + (hd + 1) * dh]
        v = vv[:, hd * dh:(hd + 1) * dh]
        kt = kk.T
        bc = bcum_c[:, M_HEADS + hd:M_HEADS + hd + 1]
        br = bcum_r[M_HEADS + hd:M_HEADS + hd + 1, :]
        li = ifr[hd:hd + 1, :]
        m_prev = m_ref[hd, 0:1, 0:1]
        dmat = jnp.where(causal, bc - br + li, -jnp.inf)
        inter = bc + m_prev
        m_t = jnp.maximum(jnp.max(dmat, axis=-1, keepdims=True), inter)
        wts = jnp.exp(dmat - m_t)
        s_inter = jnp.exp(inter - m_t)
        qb = q.astype(BF16)
        s_mat = jnp.dot(qb, kt.astype(BF16), preferred_element_type=F32) * wts
        cn = cn_ref[hd]
        qcn = jnp.dot(qb, cn.astype(BF16), preferred_element_type=F32)
        num = jnp.dot(s_mat.astype(BF16), v.astype(BF16), preferred_element_type=F32) \
            + s_inter * qcn[:, 0:dh]
        den = jnp.sum(s_mat, axis=-1, keepdims=True) + s_inter * qcn[:, dh:dh + 1]
        hh = num / jnp.maximum(jnp.abs(den), jnp.exp(-m_t))
        b_last = br[:, L - 1:L]
        a = b_last - br + li
        m_new = jnp.maximum(b_last + m_prev, jnp.max(a, axis=-1, keepdims=True))
        wk = jnp.exp(a - m_new)
        sc = jnp.exp(b_last + m_prev - m_new)
        v_ext = jnp.concatenate([v, ones_col], axis=1)
        cn_ref[hd] = sc * cn + jnp.dot((kt * wk).astype(BF16), v_ext.astype(BF16),
                                       preferred_element_type=F32)
        m_ref[hd] = jnp.broadcast_to(m_new, m_ref.shape[1:])
        mu = jnp.mean(hh, axis=-1, keepdims=True)
        hc = hh - mu
        var = jnp.mean(hc * hc, axis=-1, keepdims=True)
        hn = hc * lax.rsqrt(var + LN_EPS) * ng_ref[:, hd * dh:(hd + 1) * dh]
        hbuf[:, hd * dh:(hd + 1) * dh] = hn * _sigmoid(oo[:, hd * dh:(hd + 1) * dh])
    y_ref[0] = jnp.dot(hbuf[...].astype(BF16), wo_ref[...], preferred_element_type=F32)


def _mlstm_branch(qk3, v3, o3, ifc, ifr, cw, cb, bifc, bifr, ng, wo):
    B, S, C2 = qk3.shape
    mi = v3.shape[2]
    dh = mi // M_HEADS
    D = wo.shape[1]
    L = MLSTM_L
    nc = S // L
    const = lambda bi, ci: (0, 0)
    tile = lambda bi, ci: (bi, ci, 0)
    return pl.pallas_call(
        _mlstm_kernel,
        out_shape=jax.ShapeDtypeStruct((B, S, D), F32),
        grid=(B, nc),
        in_specs=[pl.BlockSpec((1, L, C2), tile),
                  pl.BlockSpec((1, L, mi), tile),
                  pl.BlockSpec((1, L, mi), tile),
                  pl.BlockSpec((L, LANES), lambda bi, ci: (bi * nc + ci, 0)),
                  pl.BlockSpec((SUBLANES, L), lambda bi, ci: (0, bi * nc + ci)),
                  pl.BlockSpec(cw.shape, const),
                  pl.BlockSpec((1, C2), const),
                  pl.BlockSpec((1, LANES), const),
                  pl.BlockSpec((SUBLANES, 1), const),
                  pl.BlockSpec((1, mi), const),
                  pl.BlockSpec(wo.shape, const)],
        out_specs=pl.BlockSpec((1, L, D), tile),
        scratch_shapes=[pltpu.VMEM((L + SUBLANES, C2), F32),
                        pltpu.VMEM((M_HEADS, dh, 2 * dh), F32),
                        pltpu.VMEM((M_HEADS, SUBLANES, LANES), F32),
                        pltpu.VMEM((L, mi), F32)],
        compiler_params=_params("arbitrary", "arbitrary"),
        name="mlstm",
    )(qk3, v3, o3, ifc, ifr, cw, cb, bifc, bifr, ng, wo)


def _merge_kernel(x_ref, ya_ref, yb_ref, sga_ref, sgb_ref, mod_ref, g2_ref, wo_ref, wr_ref, br_ref,
                  x1_ref, h2_ref, ri_ref, rf_ref, cnt_ref, run_ref):
    tm = x_ref.shape[0]

    @pl.when(pl.program_id(0) == 0)
    def _():
        run_ref[...] = jnp.zeros(run_ref.shape, F32)

    gate1 = mod_ref[0, 2:3, :]
    shift2 = mod_ref[0, 3:4, :]
    scale2 = mod_ref[0, 4:5, :]
    merged = sga_ref[...] * ya_ref[...] + sgb_ref[...] * yb_ref[...]
    mix = jnp.dot(merged.astype(BF16), wo_ref[...], preferred_element_type=F32)
    x1 = x_ref[...] + gate1 * mix
    x1_ref[...] = x1
    ms = jnp.mean(x1 * x1, axis=-1, keepdims=True)
    h2 = x1 * lax.rsqrt(ms + RMS_EPS) * g2_ref[...]
    h2 = h2 * (1.0 + scale2) + shift2
    h2_ref[...] = h2

    logits = jnp.dot(h2.astype(BF16), wr_ref[...], preferred_element_type=F32) + br_ref[...]
    lane = lax.broadcasted_iota(I32, (tm, LANES), 1).astype(F32)
    neg = -jnp.inf

    def first_argmax(vals):
        mx = jnp.max(vals, axis=-1, keepdims=True)
        idx = jnp.min(jnp.where(vals == mx, lane, float(LANES)), axis=-1, keepdims=True)
        return mx, idx

    lg = jnp.where(lane < N_GROUPS, logits, neg)
    gmax, gsel = first_argmax(lg)
    p_g = 1.0 / jnp.sum(jnp.exp(lg - gmax), axis=-1, keepdims=True)
    lo = N_GROUPS + gsel * E_PER_GROUP
    le = jnp.where((lane >= lo) & (lane < lo + E_PER_GROUP), logits, neg)
    l1, i1 = first_argmax(le)
    l2, i2 = first_argmax(jnp.where(lane == i1, neg, le))
    r = jnp.exp(l2 - l1)
    w1 = p_g / (1.0 + r)
    w2 = p_g * r / (1.0 + r)
    e1 = i1 - N_GROUPS
    e2 = i2 - N_GROUPS

    onehot = jnp.where((lane == e1) | (lane == e2), 1.0, 0.0)
    rows = lax.broadcasted_iota(I32, (tm, tm), 0)
    cols = lax.broadcasted_iota(I32, (tm, tm), 1)
    strict = jnp.where(cols < rows, 1.0, 0.0).astype(BF16)
    run = run_ref[0:1, :]
    before = jnp.dot(strict, onehot.astype(BF16), preferred_element_type=F32) + run
    rank1 = jnp.sum(jnp.where(lane == e1, before, 0.0), axis=-1, keepdims=True)
    rank2 = jnp.sum(jnp.where(lane == e2, before, 0.0), axis=-1, keepdims=True)
    run_new = run + jnp.sum(onehot, axis=0, keepdims=True)
    run_ref[...] = jnp.broadcast_to(run_new, run_ref.shape)
    cnt_ref[...] = jnp.broadcast_to(run_new, cnt_ref.shape).astype(I32)

    ri_ref[...] = jnp.where(lane == 0, e1, jnp.where(lane == 1, e2, jnp.where(
        lane == 2, rank1, jnp.where(lane == 3, rank2, 0.0)))).astype(I32)
    rf_ref[...] = jnp.where(lane == 0, w1, jnp.where(lane == 1, w2, 0.0))


def _merge(x2, ya, yb, sga, sgb, mod3, g2, wo, wr, br, seq):
    T, D = x2.shape
    tm = MERGE_TM
    per_b = seq // tm
    row = lambda i: (i, 0)
    const = lambda i: (0, 0)
    return pl.pallas_call(
        _merge_kernel,
        out_shape=[jax.ShapeDtypeStruct((T, D), F32),
                   jax.ShapeDtypeStruct((T, D), F32),
                   jax.ShapeDtypeStruct((T, LANES), I32),
                   jax.ShapeDtypeStruct((T, LANES), F32),
                   jax.ShapeDtypeStruct((SUBLANES, LANES), I32)],
        grid=(T // tm,),
        in_specs=[pl.BlockSpec((tm, D), row),
                  pl.BlockSpec((tm, D), row),
                  pl.BlockSpec((tm, D), row),
                  pl.BlockSpec((tm, D), row),
                  pl.BlockSpec((tm, D), row),
                  pl.BlockSpec((1, 6, D), lambda i: (i // per_b, 0, 0)),
                  pl.BlockSpec((1, D), const),
                  pl.BlockSpec(wo.shape, const),
                  pl.BlockSpec(wr.shape, const),
                  pl.BlockSpec((1, LANES), const)],
        out_specs=[pl.BlockSpec((tm, D), row),
                   pl.BlockSpec((tm, D), row),
                   pl.BlockSpec((tm, LANES), row),
                   pl.BlockSpec((tm, LANES), row),
                   pl.BlockSpec((SUBLANES, LANES), const)],
        scratch_shapes=[pltpu.VMEM((SUBLANES, LANES), F32)],
        compiler_params=_params("arbitrary"),
        name="merge",
    )(x2, ya, yb, sga, sgb, mod3, g2, wo, wr, br)


def _dispatch_kernel(dest_ref, zpos_ref, zlen_ref, nt_ref, h_ref, xs_ref, zbuf, sem, zsem):
    td = h_ref.shape[0]
    i = pl.program_id(0)

    @pl.when(i == 0)
    def _():
        zbuf[...] = jnp.zeros(zbuf.shape, F32)

        def fill(e, c, wait):
            pos = zpos_ref[e]
            n = zlen_ref[e]
            head = jnp.minimum((-pos) & (SUBLANES - 1), n)

            def row(r, cc):
                cp = pltpu.make_async_copy(zbuf.at[0], xs_ref.at[pos + r], zsem)
                cp.wait() if wait else cp.start()
                return cc

            lax.fori_loop(0, head, row, 0)
            rest = n - head
            off = pos + head
            bit = zbuf.shape[0] // 2
            while bit >= SUBLANES:
                cp = pltpu.make_async_copy(zbuf.at[pl.ds(0, bit)],
                                           xs_ref.at[pl.ds(pl.multiple_of(off, SUBLANES), bit)], zsem)

                @pl.when((rest & bit) != 0)
                def _():
                    cp.wait() if wait else cp.start()

                off = off + (rest & bit)
                bit //= 2
            return c

        tz = zbuf.shape[0]

        def tail(j, c, wait):
            cp = pltpu.make_async_copy(zbuf, xs_ref.at[pl.ds(pl.multiple_of(j * tz, tz), tz)], zsem)
            cp.wait() if wait else cp.start()
            return c

        n_all = xs_ref.shape[0] // tz
        lax.fori_loop(0, N_EXPERTS, functools.partial(fill, wait=False), 0)
        lax.fori_loop(nt_ref[0], n_all, functools.partial(tail, wait=False), 0)
        lax.fori_loop(0, N_EXPERTS, functools.partial(fill, wait=True), 0)
        lax.fori_loop(nt_ref[0], n_all, functools.partial(tail, wait=True), 0)

    def issue(r, c):
        base = (i * td + r) * TOP_K
        for k in range(TOP_K):
            pltpu.make_async_copy(h_ref.at[r], xs_ref.at[dest_ref[base + k]], sem).start()
        return c

    lax.fori_loop(0, td, issue, 0)
    for k in range(TOP_K):
        pltpu.make_async_copy(h_ref, xs_ref.at[pl.ds(0, td)], sem).wait()


def _dispatch(dest_flat, zpos, zlen, n_tiles, h2, n_rows):
    T, D = h2.shape
    td = DISPATCH_TM
    return pl.pallas_call(
        _dispatch_kernel,
        out_shape=jax.ShapeDtypeStruct((n_rows, D), F32),
        grid_spec=pltpu.PrefetchScalarGridSpec(
            num_scalar_prefetch=4,
            grid=(T // td,),
            in_specs=[pl.BlockSpec((td, D), lambda i, d, zp, zl, nt: (i, 0))],
            out_specs=pl.BlockSpec(memory_space=pl.ANY),
            scratch_shapes=[pltpu.VMEM((EXPERT_TM, D), F32),
                            pltpu.SemaphoreType.DMA,
                            pltpu.SemaphoreType.DMA]),
        compiler_params=_params("arbitrary"),
        name="dispatch",
    )(dest_flat, zpos, zlen, n_tiles, h2)


def _expert_kernel(te_ref, tb_ref, first_ref, nt_ref, xs_ref, wg_ref, wu_ref, wd_ref, ys_ref,
                   wgb, wub, wdb):
    j = pl.program_id(0)

    @pl.when(j < nt_ref[0])
    def _():
        @pl.when(first_ref[j] == 1)
        def _():
            wgb[...] = wg_ref[0].astype(BF16)
            wub[...] = wu_ref[0].astype(BF16)
            wdb[...] = wd_ref[0].astype(BF16)

        xb = xs_ref[...].astype(BF16)
        g = jnp.dot(xb, wgb[...], preferred_element_type=F32)
        u = jnp.dot(xb, wub[...], preferred_element_type=F32)
        a = (g * _sigmoid(g)) * u
        ys_ref[...] = jnp.dot(a.astype(BF16), wdb[...], preferred_element_type=F32)

    @pl.when(j >= nt_ref[0])
    def _():
        ys_ref[...] = jnp.zeros(ys_ref.shape, F32)


def _experts(tile_e, tile_b, tile_first, n_tiles, xs, wg, wu, wd, max_tiles):
    P, D = xs.shape
    de = wg.shape[2]
    tm = EXPERT_TM
    wmap = lambda j, te, tb, tf, nt: (te[j], 0, 0)
    rmap = lambda j, te, tb, tf, nt: (tb[j], 0)
    return pl.pallas_call(
        _expert_kernel,
        out_shape=jax.ShapeDtypeStruct((P, D), F32),
        grid_spec=pltpu.PrefetchScalarGridSpec(
            num_scalar_prefetch=4,
            grid=(max_tiles,),
            in_specs=[pl.BlockSpec((tm, D), rmap),
                      pl.BlockSpec((1, D, de), wmap),
                      pl.BlockSpec((1, D, de), wmap),
                      pl.BlockSpec((1, de, D), wmap)],
            out_specs=pl.BlockSpec((tm, D), lambda j, te, tb, tf, nt: (j, 0)),
            scratch_shapes=[pltpu.VMEM((D, de), BF16),
                            pltpu.VMEM((D, de), BF16),
                            pltpu.VMEM((de, D), BF16)]),
        compiler_params=_params("arbitrary"),
        name="experts",
    )(tile_e, tile_b, tile_first, n_tiles, xs, wg, wu, wd)


def _combine_kernel(dest_ref, x1_ref, rf_ref, mod_ref, gf_ref, ys_ref, out_ref, y0, y1, sem, *,
                    final_norm):
    tc = x1_ref.shape[0]
    i = pl.program_id(0)

    def issue(r, c):
        base = (i * tc + r) * TOP_K
        pltpu.make_async_copy(ys_ref.at[dest_ref[base]], y0.at[r], sem).start()
        pltpu.make_async_copy(ys_ref.at[dest_ref[base + 1]], y1.at[r], sem).start()
        return c

    lax.fori_loop(0, tc, issue, 0)
    pltpu.make_async_copy(ys_ref.at[pl.ds(0, tc)], y0, sem).wait()
    pltpu.make_async_copy(ys_ref.at[pl.ds(0, tc)], y1, sem).wait()

    gate2 = mod_ref[0, 5:6, :]
    w = rf_ref[...]
    moe = w[:, 0:1] * y0[...] + w[:, 1:2] * y1[...]
    x2 = x1_ref[...] + gate2 * moe
    if final_norm:
        ms = jnp.mean(x2 * x2, axis=-1, keepdims=True)
        x2 = x2 * lax.rsqrt(ms + RMS_EPS) * gf_ref[...]
    out_ref[...] = x2


def _combine(dest_flat, x1, rf, mod3, gf, ys, seq, final_norm):
    T, D = x1.shape
    tc = COMBINE_TM
    per_b = seq // tc
    return pl.pallas_call(
        functools.partial(_combine_kernel, final_norm=final_norm),
        out_shape=jax.ShapeDtypeStruct((T, D), F32),
        grid_spec=pltpu.PrefetchScalarGridSpec(
            num_scalar_prefetch=1,
            grid=(T // tc,),
            in_specs=[pl.BlockSpec((tc, D), lambda i, d: (i, 0)),
                      pl.BlockSpec((tc, LANES), lambda i, d: (i, 0)),
                      pl.BlockSpec((1, 6, D), lambda i, d: (i // per_b, 0, 0)),
                      pl.BlockSpec((1, D), lambda i, d: (0, 0)),
                      pl.BlockSpec(memory_space=pl.ANY)],
            out_specs=pl.BlockSpec((tc, D), lambda i, d: (i, 0)),
            scratch_shapes=[pltpu.VMEM((tc, D), F32),
                            pltpu.VMEM((tc, D), F32),
                            pltpu.SemaphoreType.DMA]),
        compiler_params=_params("arbitrary"),
        name="combine",
    )(dest_flat, x1, rf, mod3, gf, ys)


def _layer(x2, c, seq, w_ada, b_ada, g_norm1, w_in, b_if, conv_dw_w, conv_dw_b, conv_ln_g, conv_ln_b,
           w_conv_out, qk_conv_w, qk_conv_b, m_norm_g, w_m_out, w_out, g_norm2, w_rg, b_rg,
           w_re, b_re, w_e_gate, w_e_up, w_e_down):
    T, D = x2.shape
    B = T // seq
    dc = D // 2
    nif = 2 * M_HEADS

    mod3 = _ada(c, w_ada, b_ada).reshape(B, 6, D)

    if_lo = 6 * dc
    w_main = jnp.concatenate([w_in[:, :if_lo], w_in[:, if_lo + nif:]], axis=1).astype(BF16)
    w_if = w_in[:, if_lo:if_lo + nif]
    w_if_pad = jnp.pad(w_if, ((0, 0), (0, LANES - nif))).astype(BF16)
    w_ift = w_if.T.astype(BF16)
    u, qk, v, o, sga, sgb, ifc, ifr = _inproj(x2, mod3, g_norm1.reshape(1, D), w_main, w_if_pad, w_ift, seq)

    ya = _conv_branch(u.reshape(B, seq, dc), conv_dw_w, conv_dw_b.reshape(1, dc),
                      conv_ln_g.reshape(1, dc), conv_ln_b.reshape(1, dc), w_conv_out.astype(BF16))
    bifc = jnp.pad(b_if, (0, LANES - nif)).reshape(1, LANES)
    bifr = b_if.reshape(nif, 1)
    yb = _mlstm_branch(qk.reshape(B, seq, 2 * dc), v.reshape(B, seq, dc), o.reshape(B, seq, dc),
                       ifc, ifr, qk_conv_w, qk_conv_b.reshape(1, 2 * dc), bifc, bifr,
                       m_norm_g.reshape(1, dc), w_m_out.astype(BF16))

    n_r = N_GROUPS + N_EXPERTS
    w_r = jnp.pad(jnp.concatenate([w_rg, w_re], axis=1), ((0, 0), (0, LANES - n_r))).astype(BF16)
    b_r = jnp.pad(jnp.concatenate([b_rg, b_re]), (0, LANES - n_r)).reshape(1, LANES)
    x1, h2, ri, rf, cnt = _merge(x2, ya.reshape(T, D), yb.reshape(T, D), sga, sgb, mod3,
                                 g_norm2.reshape(1, D), w_out.astype(BF16), w_r, b_r, seq)

    tm = EXPERT_TM
    counts = cnt[0, :N_EXPERTS]
    tiles_e = (counts + tm - 1) // tm
    tile_end = jnp.cumsum(tiles_e)
    base = (tile_end - tiles_e) * tm
    n_tiles = tile_end[-1]
    max_tiles = (T * TOP_K) // tm + N_EXPERTS
    n_rows = max_tiles * tm
    jt = jnp.arange(max_tiles, dtype=I32)
    jc = jnp.minimum(jt, n_tiles - 1)
    tile_e = jnp.sum(jc[:, None] >= tile_end[None, :], axis=1).astype(I32)
    tile_first = ((jt < n_tiles) & (jc == (tile_end - tiles_e)[tile_e])).astype(I32)
    dest = (base[ri[:, 0:TOP_K]] + ri[:, TOP_K:2 * TOP_K]).reshape(T * TOP_K).astype(I32)
    zpos = (base + counts).astype(I32)
    zlen = (tiles_e * tm - counts).astype(I32)

    nt = n_tiles.reshape(1).astype(I32)
    xs = _dispatch(dest, zpos, zlen, nt, h2, n_rows)
    ys = _experts(tile_e, jc.astype(I32), tile_first, nt, xs, w_e_gate, w_e_up, w_e_down, max_tiles)
    return dest, x1, rf, mod3, ys


def kernel(x, c, w_ada, b_ada, g_norm1, w_in, b_if, conv_dw_w, conv_dw_b, conv_ln_g, conv_ln_b,
           w_conv_out, qk_conv_w, qk_conv_b, m_norm_g, w_m_out, w_out, g_norm2, w_rg, b_rg,
           w_re, b_re, w_e_gate, w_e_up, w_e_down, g_final):
    B, S, D = x.shape
    depth = w_ada.shape[0]
    x2 = x.reshape(B * S, D)
    for l in range(depth):
        dest, x1, rf, mod3, ys = _layer(
            x2, c, S, w_ada[l], b_ada[l], g_norm1[l], w_in[l], b_if[l], conv_dw_w[l], conv_dw_b[l],
            conv_ln_g[l], conv_ln_b[l], w_conv_out[l], qk_conv_w[l], qk_conv_b[l], m_norm_g[l],
            w_m_out[l], w_out[l], g_norm2[l], w_rg[l], b_rg[l], w_re[l], b_re[l],
            w_e_gate[l], w_e_up[l], w_e_down[l])
        x2 = _combine(dest, x1, rf, mod3, g_final.reshape(1, D), ys, S, final_norm=l == depth - 1)
    return x2.reshape(B, S, D)
```

```python
import functools

import jax
import jax.numpy as jnp
from jax import lax
from jax.experimental import pallas as pl
from jax.experimental.pallas import tpu as pltpu

F32 = jnp.float32
BF16 = jnp.bfloat16
I32 = jnp.int32

M_HEADS = 4
CONV_WIDTH = 31
QK_CONV_WIDTH = 4
N_GROUPS = 4
E_PER_GROUP = 8
N_EXPERTS = N_GROUPS * E_PER_GROUP
TOP_K = 2
RMS_EPS = 1e-6
LN_EPS = 1e-5

LANES = 128
SUBLANES = 8
VMEM_LIMIT = 56 * 1024 * 1024

ADA_TN = 1024
INPROJ_TM = 256
CONV_TS = 256
CONV_HALO = 32
CONV_RC = 32
MLSTM_L = 128
MERGE_TM = 256
EXPERT_TM = 256
COMBINE_TM = 256
RANK_BITS = 16
RANK_RADIX = 1 << RANK_BITS
assert EXPERT_TM & (EXPERT_TM - 1) == 0


def _sigmoid(v):
    return 1.0 / (1.0 + jnp.exp(-v))


def _log_sigmoid(v):
    return -(jnp.maximum(-v, 0.0) + jnp.log1p(jnp.exp(-jnp.abs(v))))


def _params(*sem):
    return pltpu.CompilerParams(dimension_semantics=sem, vmem_limit_bytes=VMEM_LIMIT)


def _ada_kernel(c_ref, w_ref, b_ref, o_ref):
    c = c_ref[...]
    s = c * _sigmoid(c)
    o_ref[...] = jnp.dot(s, w_ref[...], preferred_element_type=F32,
                         precision=lax.Precision.HIGHEST) + b_ref[...]


def _ada(c, w_ada, b_ada):
    B, D = c.shape
    N = w_ada.shape[1]
    return pl.pallas_call(
        _ada_kernel,
        out_shape=jax.ShapeDtypeStruct((B, N), F32),
        grid=(N // ADA_TN,),
        in_specs=[pl.BlockSpec((B, D), lambda j: (0, 0)),
                  pl.BlockSpec((D, ADA_TN), lambda j: (0, j)),
                  pl.BlockSpec((1, ADA_TN), lambda j: (0, j))],
        out_specs=pl.BlockSpec((B, ADA_TN), lambda j: (0, j)),
        compiler_params=_params("arbitrary"),
        name="ada",
    )(c, w_ada, b_ada.reshape(1, N))


def _inproj_kernel(x_ref, mod_ref, g_ref, wm_ref, wif_ref, wift_ref,
                   u_ref, qk_ref, v_ref, o_ref, sga_ref, sgb_ref, ifc_ref, ifr_ref):
    x = x_ref[...]
    shift = mod_ref[0, 0:1, :]
    scale = mod_ref[0, 1:2, :]
    ms = jnp.mean(x * x, axis=-1, keepdims=True)
    h = x * lax.rsqrt(ms + RMS_EPS) * g_ref[...]
    h = h * (1.0 + scale) + shift
    hb = h.astype(BF16)
    dc = u_ref.shape[1]
    d = sga_ref.shape[1]

    def seg(lo, hi):
        return jnp.dot(hb, wm_ref[:, lo:hi], preferred_element_type=F32)

    u_ref[...] = seg(0, dc) * _sigmoid(seg(dc, 2 * dc))
    qk_ref[...] = seg(2 * dc, 4 * dc)
    v_ref[...] = seg(4 * dc, 5 * dc)
    o_ref[...] = seg(5 * dc, 6 * dc)
    sga_ref[...] = _sigmoid(seg(6 * dc, 6 * dc + d))
    sgb_ref[...] = _sigmoid(seg(6 * dc + d, 6 * dc + 2 * d))
    ifc_ref[...] = jnp.dot(hb, wif_ref[...], preferred_element_type=F32)
    ifr_ref[...] = lax.dot_general(wift_ref[...], hb, (((1,), (1,)), ((), ())),
                                   preferred_element_type=F32)


def _inproj(x2, mod3, g1, w_main, w_if, w_ift, seq):
    T, D = x2.shape
    tm = INPROJ_TM
    dc = D // 2
    per_b = seq // tm
    row = lambda i: (i, 0)
    const = lambda i: (0, 0)
    return pl.pallas_call(
        _inproj_kernel,
        out_shape=[jax.ShapeDtypeStruct((T, dc), F32),
                   jax.ShapeDtypeStruct((T, 2 * dc), F32),
                   jax.ShapeDtypeStruct((T, dc), F32),
                   jax.ShapeDtypeStruct((T, dc), F32),
                   jax.ShapeDtypeStruct((T, D), F32),
                   jax.ShapeDtypeStruct((T, D), F32),
                   jax.ShapeDtypeStruct((T, LANES), F32),
                   jax.ShapeDtypeStruct((SUBLANES, T), F32)],
        grid=(T // tm,),
        in_specs=[pl.BlockSpec((tm, D), row),
                  pl.BlockSpec((1, 6, D), lambda i: (i // per_b, 0, 0)),
                  pl.BlockSpec((1, D), const),
                  pl.BlockSpec(w_main.shape, const),
                  pl.BlockSpec(w_if.shape, const),
                  pl.BlockSpec(w_ift.shape, const)],
        out_specs=[pl.BlockSpec((tm, dc), row),
                   pl.BlockSpec((tm, 2 * dc), row),
                   pl.BlockSpec((tm, dc), row),
                   pl.BlockSpec((tm, dc), row),
                   pl.BlockSpec((tm, D), row),
                   pl.BlockSpec((tm, D), row),
                   pl.BlockSpec((tm, LANES), row),
                   pl.BlockSpec((SUBLANES, tm), lambda i: (0, i))],
        compiler_params=_params("arbitrary"),
        name="inproj",
    )(x2, mod3, g1, w_main, w_if, w_ift)


def _conv_kernel(u_ref, w_ref, b_ref, lg_ref, lb_ref, wo_ref, y_ref, ubuf, cbuf):
    ts = u_ref.shape[1]
    halo = CONV_HALO

    @pl.when(pl.program_id(1) == 0)
    def _():
        ubuf[0:halo, :] = jnp.zeros((halo, ubuf.shape[1]), F32)

    ubuf[halo:halo + ts, :] = u_ref[0]
    off = halo - (CONV_WIDTH - 1)
    for r0 in range(0, ts, CONV_RC):
        acc = jnp.broadcast_to(b_ref[...], (CONV_RC, ubuf.shape[1]))
        for k in range(CONV_WIDTH):
            acc = acc + w_ref[k:k + 1, :] * ubuf[off + k + r0:off + k + r0 + CONV_RC, :]
        cbuf[r0:r0 + CONV_RC, :] = acc
    ubuf[0:halo, :] = ubuf[ts:ts + halo, :]

    a = cbuf[...]
    mu = jnp.mean(a, axis=-1, keepdims=True)
    ac = a - mu
    var = jnp.mean(ac * ac, axis=-1, keepdims=True)
    z = ac * lax.rsqrt(var + LN_EPS) * lg_ref[...] + lb_ref[...]
    z = z * _sigmoid(z)
    y_ref[0] = jnp.dot(z.astype(BF16), wo_ref[...], preferred_element_type=F32)


def _conv_branch(u3, w, b, lg, lb, wo):
    B, S, C = u3.shape
    D = wo.shape[1]
    ts = CONV_TS
    const = lambda bi, si: (0, 0)
    return pl.pallas_call(
        _conv_kernel,
        out_shape=jax.ShapeDtypeStruct((B, S, D), F32),
        grid=(B, S // ts),
        in_specs=[pl.BlockSpec((1, ts, C), lambda bi, si: (bi, si, 0)),
                  pl.BlockSpec(w.shape, const),
                  pl.BlockSpec((1, C), const),
                  pl.BlockSpec((1, C), const),
                  pl.BlockSpec((1, C), const),
                  pl.BlockSpec(wo.shape, const)],
        out_specs=pl.BlockSpec((1, ts, D), lambda bi, si: (bi, si, 0)),
        scratch_shapes=[pltpu.VMEM((ts + CONV_HALO, C), F32),
                        pltpu.VMEM((ts, C), F32)],
        compiler_params=_params("arbitrary", "arbitrary"),
        name="conv",
    )(u3, w, b, lg, lb, wo)


def _mlstm_kernel(qk_ref, v_ref, o_ref, ifc_ref, ifr_ref, cw_ref, cb_ref, bifc_ref, bifr_ref,
                  ng_ref, wo_ref, y_ref, qkbuf, cn_ref, m_ref, hbuf):
    L = qk_ref.shape[1]
    mi = v_ref.shape[2]
    dh = mi // M_HEADS
    halo = SUBLANES

    @pl.when(pl.program_id(1) == 0)
    def _():
        qkbuf[0:halo, :] = jnp.zeros((halo, qkbuf.shape[1]), F32)
        cn_ref[...] = jnp.zeros(cn_ref.shape, F32)
        m_ref[...] = jnp.zeros(m_ref.shape, F32)

    qkbuf[halo:halo + L, :] = qk_ref[0]
    off = halo - (QK_CONV_WIDTH - 1)
    y = jnp.broadcast_to(cb_ref[...], (L, qkbuf.shape[1]))
    for k in range(QK_CONV_WIDTH):
        y = y + cw_ref[k:k + 1, :] * qkbuf[off + k:off + k + L, :]
    y = y * _sigmoid(y)
    qkbuf[0:halo, :] = qkbuf[L:L + halo, :]

    ifr = ifr_ref[...] + bifr_ref[...]
    ifc = ifc_ref[...] + bifc_ref[...]
    lfr = _log_sigmoid(ifr)
    lfc = _log_sigmoid(ifc)
    rows = lax.broadcasted_iota(I32, (L, L), 0)
    cols = lax.broadcasted_iota(I32, (L, L), 1)
    causal = cols <= rows
    lower = causal.astype(F32)
    upper = (rows <= cols).astype(F32)
    bcum_c = jnp.dot(lower, lfc, preferred_element_type=F32, precision=lax.Precision.HIGHEST)
    bcum_r = jnp.dot(lfr, upper, preferred_element_type=F32, precision=lax.Precision.HIGHEST)

    lane = lax.broadcasted_iota(I32, (L, dh), 1)
    ones_col = jnp.where(lane == 0, 1.0, 0.0).astype(F32)
    vv = v_ref[0]
    oo = o_ref[0]
    scale = dh ** -0.5
    for hd in range(M_HEADS):
        q = y[:, hd * dh:(hd + 1) * dh] * scale
        kk = y[:, mi + hd * dh:mi + (hd + 1) * dh]
        v = vv[:, hd * dh:(hd + 1) * dh]
        kt = kk.T
        bc = bcum_c[:, M_HEADS + hd:M_HEADS + hd + 1]
        br = bcum_r[M_HEADS + hd:M_HEADS + hd + 1, :]
        li = ifr[hd:hd + 1, :]
        m_prev = m_ref[hd, 0:1, 0:1]
        dmat = jnp.where(causal, bc - br + li, -jnp.inf)
        inter = bc + m_prev
        m_t = jnp.maximum(jnp.max(dmat, axis=-1, keepdims=True), inter)
        wts = jnp.exp(dmat - m_t)
        s_inter = jnp.exp(inter - m_t)
        qb = q.astype(BF16)
        s_mat = jnp.dot(qb, kt.astype(BF16), preferred_element_type=F32) * wts
        cn = cn_ref[hd]
        qcn = jnp.dot(qb, cn.astype(BF16), preferred_element_type=F32)
        num = jnp.dot(s_mat.astype(BF16), v.astype(BF16), preferred_element_type=F32) \
            + s_inter * qcn[:, 0:dh]
        den = jnp.sum(s_mat, axis=-1, keepdims=True) + s_inter * qcn[:, dh:dh + 1]
        hh = num / jnp.maximum(jnp.abs(den), jnp.exp(-m_t))
        b_last = br[:, L - 1:L]
        a = b_last - br + li
        m_new = jnp.maximum(b_last + m_prev, jnp.max(a, axis=-1, keepdims=True))
        wk = jnp.exp(a - m_new)
        sc = jnp.exp(b_last + m_prev - m_new)
        v_ext = jnp.concatenate([v, ones_col], axis=1)
        cn_ref[hd] = sc * cn + jnp.dot((kt * wk).astype(BF16), v_ext.astype(BF16),
                                       preferred_element_type=F32)
        m_ref[hd] = jnp.broadcast_to(m_new, m_ref.shape[1:])
        mu = jnp.mean(hh, axis=-1, keepdims=True)
        hc = hh - mu
        var = jnp.mean(hc * hc, axis=-1, keepdims=True)
        hn = hc * lax.rsqrt(var + LN_EPS) * ng_ref[:, hd * dh:(hd + 1) * dh]
        hbuf[:, hd * dh:(hd + 1) * dh] = hn * _sigmoid(oo[:, hd * dh:(hd + 1) * dh])
    y_ref[0] = jnp.dot(hbuf[...].astype(BF16), wo_ref[...], preferred_element_type=F32)


def _mlstm_branch(qk3, v3, o3, ifc, ifr, cw, cb, bifc, bifr, ng, wo):
    B, S, C2 = qk3.shape
    mi = v3.shape[2]
    dh = mi // M_HEADS
    D = wo.shape[1]
    L = MLSTM_L
    nc = S // L
    const = lambda bi, ci: (0, 0)
    tile = lambda bi, ci: (bi, ci, 0)
    return pl.pallas_call(
        _mlstm_kernel,
        out_shape=jax.ShapeDtypeStruct((B, S, D), F32),
        grid=(B, nc),
        in_specs=[pl.BlockSpec((1, L, C2), tile),
                  pl.BlockSpec((1, L, mi), tile),
                  pl.BlockSpec((1, L, mi), tile),
                  pl.BlockSpec((L, LANES), lambda bi, ci: (bi * nc + ci, 0)),
                  pl.BlockSpec((SUBLANES, L), lambda bi, ci: (0, bi * nc + ci)),
                  pl.BlockSpec(cw.shape, const),
                  pl.BlockSpec((1, C2), const),
                  pl.BlockSpec((1, LANES), const),
                  pl.BlockSpec((SUBLANES, 1), const),
                  pl.BlockSpec((1, mi), const),
                  pl.BlockSpec(wo.shape, const)],
        out_specs=pl.BlockSpec((1, L, D), tile),
        scratch_shapes=[pltpu.VMEM((L + SUBLANES, C2), F32),
                        pltpu.VMEM((M_HEADS, dh, 2 * dh), F32),
                        pltpu.VMEM((M_HEADS, SUBLANES, LANES), F32),
                        pltpu.VMEM((L, mi), F32)],
        compiler_params=_params("arbitrary", "arbitrary"),
        name="mlstm",
    )(qk3, v3, o3, ifc, ifr, cw, cb, bifc, bifr, ng, wo)


def _merge_kernel(x_ref, ya_ref, yb_ref, sga_ref, sgb_ref, mod_ref, g2_ref, wo_ref, wr_ref, br_ref,
                  x1_ref, h2_ref, ri_ref, rf_ref, cnt_ref, run_ref):
    tm = x_ref.shape[0]

    @pl.when(pl.program_id(0) == 0)
    def _():
        run_ref[...] = jnp.zeros(run_ref.shape, F32)

    gate1 = mod_ref[0, 2:3, :]
    shift2 = mod_ref[0, 3:4, :]
    scale2 = mod_ref[0, 4:5, :]
    merged = sga_ref[...] * ya_ref[...] + sgb_ref[...] * yb_ref[...]
    mix = jnp.dot(merged.astype(BF16), wo_ref[...], preferred_element_type=F32)
    x1 = x_ref[...] + gate1 * mix
    x1_ref[...] = x1
    ms = jnp.mean(x1 * x1, axis=-1, keepdims=True)
    h2 = x1 * lax.rsqrt(ms + RMS_EPS) * g2_ref[...]
    h2 = h2 * (1.0 + scale2) + shift2
    h2_ref[...] = h2.reshape(h2_ref.shape)

    logits = jnp.dot(h2.astype(BF16), wr_ref[...], preferred_element_type=F32) + br_ref[...]
    lane = lax.broadcasted_iota(I32, (tm, LANES), 1).astype(F32)
    neg = -jnp.inf

    def first_argmax(vals):
        mx = jnp.max(vals, axis=-1, keepdims=True)
        idx = jnp.min(jnp.where(vals == mx, lane, float(LANES)), axis=-1, keepdims=True)
        return mx, idx

    lg = jnp.where(lane < N_GROUPS, logits, neg)
    gmax, gsel = first_argmax(lg)
    p_g = 1.0 / jnp.sum(jnp.exp(lg - gmax), axis=-1, keepdims=True)
    lo = N_GROUPS + gsel * E_PER_GROUP
    le = jnp.where((lane >= lo) & (lane < lo + E_PER_GROUP), logits, neg)
    l1, i1 = first_argmax(le)
    l2, i2 = first_argmax(jnp.where(lane == i1, neg, le))
    r = jnp.exp(l2 - l1)
    w1 = p_g / (1.0 + r)
    w2 = p_g * r / (1.0 + r)
    e1 = i1 - N_GROUPS
    e2 = i2 - N_GROUPS

    onehot = jnp.where((lane == e1) | (lane == e2), 1.0, 0.0)
    rows = lax.broadcasted_iota(I32, (tm, tm), 0)
    cols = lax.broadcasted_iota(I32, (tm, tm), 1)
    strict = jnp.where(cols < rows, 1.0, 0.0).astype(BF16)
    run = run_ref[0:1, :]
    before = jnp.dot(strict, onehot.astype(BF16), preferred_element_type=F32) + run
    rank1 = jnp.sum(jnp.where(lane == e1, before, 0.0), axis=-1, keepdims=True)
    rank2 = jnp.sum(jnp.where(lane == e2, before, 0.0), axis=-1, keepdims=True)
    run_new = run + jnp.sum(onehot, axis=0, keepdims=True)
    run_ref[...] = jnp.broadcast_to(run_new, run_ref.shape)
    cnt_ref[...] = jnp.broadcast_to(run_new, cnt_ref.shape).astype(I32)

    ri_ref[...] = jnp.where(lane == 0, e1 * float(RANK_RADIX) + rank1,
                            jnp.where(lane == 1, e2 * float(RANK_RADIX) + rank2, 0.0)).astype(I32)
    rf_ref[...] = jnp.where(lane == 0, w1, jnp.where(lane == 1, w2, 0.0))


def _merge(x2, ya, yb, sga, sgb, mod3, g2, wo, wr, br, seq):
    T, D = x2.shape
    tm = MERGE_TM
    per_b = seq // tm
    row = lambda i: (i, 0)
    const = lambda i: (0, 0)
    return pl.pallas_call(
        _merge_kernel,
        out_shape=[jax.ShapeDtypeStruct((T, D), F32),
                   jax.ShapeDtypeStruct((T, D // LANES, LANES), F32),
                   jax.ShapeDtypeStruct((T, LANES), I32),
                   jax.ShapeDtypeStruct((T, LANES), F32),
                   jax.ShapeDtypeStruct((SUBLANES, LANES), I32)],
        grid=(T // tm,),
        in_specs=[pl.BlockSpec((tm, D), row),
                  pl.BlockSpec((tm, D), row),
                  pl.BlockSpec((tm, D), row),
                  pl.BlockSpec((tm, D), row),
                  pl.BlockSpec((tm, D), row),
                  pl.BlockSpec((1, 6, D), lambda i: (i // per_b, 0, 0)),
                  pl.BlockSpec((1, D), const),
                  pl.BlockSpec(wo.shape, const),
                  pl.BlockSpec(wr.shape, const),
                  pl.BlockSpec((1, LANES), const)],
        out_specs=[pl.BlockSpec((tm, D), row),
                   pl.BlockSpec((tm, D // LANES, LANES), lambda i: (i, 0, 0)),
                   pl.BlockSpec((tm, LANES), row),
                   pl.BlockSpec((tm, LANES), row),
                   pl.BlockSpec((SUBLANES, LANES), const)],
        scratch_shapes=[pltpu.VMEM((SUBLANES, LANES), F32)],
        compiler_params=_params("arbitrary"),
        name="merge",
    )(x2, ya, yb, sga, sgb, mod3, g2, wo, wr, br)


def _invert_kernel(code_ref, cnt_ref, tok_ref, dst_ref, base_ref):
    n_assign = code_ref.shape[0]
    n_tok = n_assign // TOP_K
    tm = EXPERT_TM
    n_rows = tok_ref.shape[0]

    def pad_row(p, c):
        tok_ref[p] = 0
        dst_ref[p] = n_assign + (p & (tm - 1))
        return c

    def first_row(e, acc):
        base_ref[e] = acc
        end = acc + (cnt_ref[e] + tm - 1) // tm * tm
        lax.fori_loop(acc + cnt_ref[e], end, pad_row, 0)
        return end

    n_live = lax.fori_loop(0, N_EXPERTS, first_row, 0)

    def idle_tile(t, c):
        return lax.fori_loop(0, tm, lambda r, cc: pad_row(t * tm + r, cc), c, unroll=8)

    lax.fori_loop(n_live // tm, n_rows // tm, idle_tile, 0)

    def place(t, c):
        for k in range(TOP_K):
            code = code_ref[t * TOP_K + k]
            p = base_ref[lax.shift_right_logical(code, RANK_BITS)] + (code & (RANK_RADIX - 1))
            tok_ref[p] = t
            dst_ref[p] = k * n_tok + t
        return c

    lax.fori_loop(0, n_tok, place, 0, unroll=4)


def _invert(code, counts, n_rows):
    smem = pl.BlockSpec(memory_space=pltpu.SMEM)
    return pl.pallas_call(
        _invert_kernel,
        out_shape=[jax.ShapeDtypeStruct((n_rows,), I32), jax.ShapeDtypeStruct((n_rows,), I32)],
        in_specs=[smem, smem],
        out_specs=[smem, smem],
        scratch_shapes=[pltpu.SMEM((N_EXPERTS,), I32)],
        name="invert",
    )(code, counts)


def _expert_kernel(te_ref, first_ref, nt_ref, tok_ref, dst_ref, h_ref, wg_ref, wu_ref, wd_ref, yk_ref,
                   wgb, wub, wdb, xbuf0, xbuf1, obuf0, obuf1, gsem, ssem):
    j = pl.program_id(0)
    nt = nt_ref[0]
    xbuf = (xbuf0, xbuf1)
    obuf = (obuf0, obuf1)
    tm = xbuf0.shape[0]
    n_tok = h_ref.shape[0]
    n_assign = n_tok * TOP_K

    def gather_start(tile, slot):
        for r in range(tm):
            pltpu.make_async_copy(h_ref.at[tok_ref[tile * tm + r]], xbuf[slot].at[r],
                                  gsem.at[slot]).start()

    def gather_wait(slot):
        pltpu.make_async_copy(h_ref.at[pl.ds(0, tm)], xbuf[slot], gsem.at[slot]).wait()

    def scatter_start(tile, slot, live):
        for r in range(tm):
            dst = dst_ref[tile * tm + r]
            if live is not True:
                dst = jnp.where(live, dst, n_assign + r)
            pltpu.make_async_copy(obuf[slot].at[r], yk_ref.at[dst], ssem.at[slot]).start()

    def scatter_wait(slot):
        pltpu.make_async_copy(obuf[slot], yk_ref.at[pl.ds(0, tm)], ssem.at[slot]).wait()

    @pl.when(j == 0)
    def _():
        obuf1[...] = jnp.zeros(obuf1.shape, F32)
        gather_start(0, 0)

    @pl.when((j < nt) & (first_ref[j] == 1))
    def _():
        wgb[...] = wg_ref[0].astype(BF16)
        wub[...] = wu_ref[0].astype(BF16)
        wdb[...] = wd_ref[0].astype(BF16)

    def live_step(slot):
        other = 1 - slot
        gather_wait(slot)
        if slot == 0:
            @pl.when(j > 0)
            def _():
                scatter_wait(slot)
        else:
            scatter_wait(slot)
        gather_start(j + 1, other)
        scatter_start(jnp.maximum(j - 1, 0), other, (j > 0) if slot == 0 else True)
        xb = xbuf[slot][...].reshape(tm, wgb.shape[0]).astype(BF16)
        g = jnp.dot(xb, wgb[...], preferred_element_type=F32)
        u = jnp.dot(xb, wub[...], preferred_element_type=F32)
        act = (g * _sigmoid(g)) * u
        y = jnp.dot(act.astype(BF16), wdb[...], preferred_element_type=F32)
        obuf[slot][...] = y.reshape(obuf[slot].shape)

    def drain(last):
        gather_wait(1 - last)
        scatter_wait(1 - last)
        scatter_start(nt - 1, last, True)
        scatter_wait(last)

    for parity in range(2):
        pl.when((j < nt) & (j % 2 == parity))(functools.partial(live_step, parity))
        pl.when((j == nt) & ((nt - 1) % 2 == parity))(functools.partial(drain, parity))


def _experts(tile_e, tile_first, n_tiles, inv_tok, inv_dst, h2, wg, wu, wd, max_tiles):
    T, rs, rl = h2.shape
    D = rs * rl
    de = wg.shape[2]
    tm = EXPERT_TM
    wmap = lambda j, te, tf, nt, it, idst: (te[j], 0, 0)
    return pl.pallas_call(
        _expert_kernel,
        out_shape=jax.ShapeDtypeStruct((T * TOP_K + tm, rs, rl), F32),
        grid_spec=pltpu.PrefetchScalarGridSpec(
            num_scalar_prefetch=5,
            grid=(max_tiles + 1,),
            in_specs=[pl.BlockSpec(memory_space=pl.ANY),
                      pl.BlockSpec((1, D, de), wmap),
                      pl.BlockSpec((1, D, de), wmap),
                      pl.BlockSpec((1, de, D), wmap)],
            out_specs=pl.BlockSpec(memory_space=pl.ANY),
            scratch_shapes=[pltpu.VMEM((D, de), BF16),
                            pltpu.VMEM((D, de), BF16),
                            pltpu.VMEM((de, D), BF16),
                            pltpu.VMEM((tm, rs, rl), F32),
                            pltpu.VMEM((tm, rs, rl), F32),
                            pltpu.VMEM((tm, rs, rl), F32),
                            pltpu.VMEM((tm, rs, rl), F32),
                            pltpu.SemaphoreType.DMA((2,)),
                            pltpu.SemaphoreType.DMA((2,))]),
        compiler_params=_params("arbitrary"),
        name="experts",
    )(tile_e, tile_first, n_tiles, inv_tok, inv_dst, h2, wg, wu, wd)


def _combine_kernel(x1_ref, rf_ref, mod_ref, gf_ref, y0_ref, y1_ref, out_ref, *, final_norm):
    gate2 = mod_ref[0, 5:6, :]
    w = rf_ref[...]
    y0 = y0_ref[...].reshape(x1_ref.shape)
    y1 = y1_ref[...].reshape(x1_ref.shape)
    moe = w[:, 0:1] * y0 + w[:, 1:2] * y1
    x2 = x1_ref[...] + gate2 * moe
    if final_norm:
        ms = jnp.mean(x2 * x2, axis=-1, keepdims=True)
        x2 = x2 * lax.rsqrt(ms + RMS_EPS) * gf_ref[...]
    out_ref[...] = x2


def _combine(x1, rf, mod3, gf, yk, seq, final_norm):
    T, D = x1.shape
    tc = COMBINE_TM
    per_b = seq // tc
    n_blk = T // tc
    return pl.pallas_call(
        functools.partial(_combine_kernel, final_norm=final_norm),
        out_shape=jax.ShapeDtypeStruct((T, D), F32),
        grid=(n_blk,),
        in_specs=[pl.BlockSpec((tc, D), lambda i: (i, 0)),
                  pl.BlockSpec((tc, LANES), lambda i: (i, 0)),
                  pl.BlockSpec((1, 6, D), lambda i: (i // per_b, 0, 0)),
                  pl.BlockSpec((1, D), lambda i: (0, 0)),
                  pl.BlockSpec((tc,) + yk.shape[1:], lambda i: (i, 0, 0)),
                  pl.BlockSpec((tc,) + yk.shape[1:], lambda i: (n_blk + i, 0, 0))],
        out_specs=pl.BlockSpec((tc, D), lambda i: (i, 0)),
        compiler_params=_params("arbitrary"),
        name="combine",
    )(x1, rf, mod3, gf, yk, yk)


def _layer(x2, c, seq, w_ada, b_ada, g_norm1, w_in, b_if, conv_dw_w, conv_dw_b, conv_ln_g, conv_ln_b,
           w_conv_out, qk_conv_w, qk_conv_b, m_norm_g, w_m_out, w_out, g_norm2, w_rg, b_rg,
           w_re, b_re, w_e_gate, w_e_up, w_e_down):
    T, D = x2.shape
    B = T // seq
    dc = D // 2
    nif = 2 * M_HEADS

    mod3 = _ada(c, w_ada, b_ada).reshape(B, 6, D)

    if_lo = 6 * dc
    w_main = jnp.concatenate([w_in[:, :if_lo], w_in[:, if_lo + nif:]], axis=1).astype(BF16)
    w_if = w_in[:, if_lo:if_lo + nif]
    w_if_pad = jnp.pad(w_if, ((0, 0), (0, LANES - nif))).astype(BF16)
    w_ift = w_if.T.astype(BF16)
    u, qk, v, o, sga, sgb, ifc, ifr = _inproj(x2, mod3, g_norm1.reshape(1, D), w_main, w_if_pad, w_ift, seq)

    ya = _conv_branch(u.reshape(B, seq, dc), conv_dw_w, conv_dw_b.reshape(1, dc),
                      conv_ln_g.reshape(1, dc), conv_ln_b.reshape(1, dc), w_conv_out.astype(BF16))
    bifc = jnp.pad(b_if, (0, LANES - nif)).reshape(1, LANES)
    bifr = b_if.reshape(nif, 1)
    yb = _mlstm_branch(qk.reshape(B, seq, 2 * dc), v.reshape(B, seq, dc), o.reshape(B, seq, dc),
                       ifc, ifr, qk_conv_w, qk_conv_b.reshape(1, 2 * dc), bifc, bifr,
                       m_norm_g.reshape(1, dc), w_m_out.astype(BF16))

    n_r = N_GROUPS + N_EXPERTS
    w_r = jnp.pad(jnp.concatenate([w_rg, w_re], axis=1), ((0, 0), (0, LANES - n_r))).astype(BF16)
    b_r = jnp.pad(jnp.concatenate([b_rg, b_re]), (0, LANES - n_r)).reshape(1, LANES)
    x1, h2, ri, rf, cnt = _merge(x2, ya.reshape(T, D), yb.reshape(T, D), sga, sgb, mod3,
                                 g_norm2.reshape(1, D), w_out.astype(BF16), w_r, b_r, seq)

    tm = EXPERT_TM
    counts = cnt[0, :N_EXPERTS]
    tiles_e = (counts + tm - 1) // tm
    tile_end = jnp.cumsum(tiles_e)
    base = (tile_end - tiles_e) * tm
    n_tiles = tile_end[-1]
    max_tiles = (T * TOP_K) // tm + N_EXPERTS
    jt = jnp.arange(max_tiles + 1, dtype=I32)
    jc = jnp.minimum(jt, n_tiles - 1)
    tile_e = jnp.sum(jc[:, None] >= tile_end[None, :], axis=1).astype(I32)
    tile_first = ((jt < n_tiles) & (jc == (tile_end - tiles_e)[tile_e])).astype(I32)

    code = ri[:, 0:TOP_K].reshape(T * TOP_K)
    inv_tok, inv_dst = _invert(code, counts, (max_tiles + 1) * tm)
    yk = _experts(tile_e, tile_first, n_tiles.reshape(1).astype(I32), inv_tok, inv_dst, h2,
                  w_e_gate, w_e_up, w_e_down, max_tiles)
    return x1, rf, mod3, yk


def kernel(x, c, w_ada, b_ada, g_norm1, w_in, b_if, conv_dw_w, conv_dw_b, conv_ln_g, conv_ln_b,
           w_conv_out, qk_conv_w, qk_conv_b, m_norm_g, w_m_out, w_out, g_norm2, w_rg, b_rg,
           w_re, b_re, w_e_gate, w_e_up, w_e_down, g_final):
    B, S, D = x.shape
    depth = w_ada.shape[0]
    x2 = x.reshape(B * S, D)
    for l in range(depth):
        x1, rf, mod3, yk = _layer(
            x2, c, S, w_ada[l], b_ada[l], g_norm1[l], w_in[l], b_if[l], conv_dw_w[l], conv_dw_b[l],
            conv_ln_g[l], conv_ln_b[l], w_conv_out[l], qk_conv_w[l], qk_conv_b[l], m_norm_g[l],
            w_m_out[l], w_out[l], g_norm2[l], w_rg[l], b_rg[l], w_re[l], b_re[l],
            w_e_gate[l], w_e_up[l], w_e_down[l])
        x2 = _combine(x1, rf, mod3, g_final.reshape(1, D), yk, S, final_norm=l == depth - 1)
    return x2.reshape(B, S, D)
```

```python
import functools

import jax
import jax.numpy as jnp
from jax import lax
from jax.experimental import pallas as pl
from jax.experimental.pallas import tpu as pltpu

F32 = jnp.float32
BF16 = jnp.bfloat16
I32 = jnp.int32

M_HEADS = 4
CONV_WIDTH = 31
QK_CONV_WIDTH = 4
N_GROUPS = 4
E_PER_GROUP = 8
N_EXPERTS = N_GROUPS * E_PER_GROUP
TOP_K = 2
RMS_EPS = 1e-6
LN_EPS = 1e-5

LANES = 128
SUBLANES = 8
VMEM_LIMIT = 56 * 1024 * 1024

ADA_TN = 1024
INPROJ_TM = 256
CONV_TS = 256
CONV_HALO = 32
CONV_RC = 32
MLSTM_L = 128
MERGE_TM = 256
EXPERT_TM = 256
COMBINE_TM = 256
RANK_BITS = 16
RANK_RADIX = 1 << RANK_BITS
assert EXPERT_TM & (EXPERT_TM - 1) == 0


def _sigmoid(v):
    return 1.0 / (1.0 + jnp.exp(-v))


def _log_sigmoid(v):
    return -(jnp.maximum(-v, 0.0) + jnp.log1p(jnp.exp(-jnp.abs(v))))


def _params(*sem):
    return pltpu.CompilerParams(dimension_semantics=sem, vmem_limit_bytes=VMEM_LIMIT)


def _ada_kernel(c_ref, w_ref, b_ref, o_ref):
    c = c_ref[...]
    s = c * _sigmoid(c)
    o_ref[...] = jnp.dot(s, w_ref[...], preferred_element_type=F32,
                         precision=lax.Precision.HIGHEST) + b_ref[...]


def _ada(c, w_ada, b_ada):
    B, D = c.shape
    N = w_ada.shape[1]
    return pl.pallas_call(
        _ada_kernel,
        out_shape=jax.ShapeDtypeStruct((B, N), F32),
        grid=(N // ADA_TN,),
        in_specs=[pl.BlockSpec((B, D), lambda j: (0, 0)),
                  pl.BlockSpec((D, ADA_TN), lambda j: (0, j)),
                  pl.BlockSpec((1, ADA_TN), lambda j: (0, j))],
        out_specs=pl.BlockSpec((B, ADA_TN), lambda j: (0, j)),
        compiler_params=_params("arbitrary"),
        name="ada",
    )(c, w_ada, b_ada.reshape(1, N))


def _inproj_kernel(x_ref, mod_ref, g_ref, wm_ref, wif_ref, wift_ref,
                   u_ref, qk_ref, v_ref, o_ref, sga_ref, sgb_ref, ifc_ref, ifr_ref):
    x = x_ref[...]
    shift = mod_ref[0, 0:1, :]
    scale = mod_ref[0, 1:2, :]
    ms = jnp.mean(x * x, axis=-1, keepdims=True)
    h = x * lax.rsqrt(ms + RMS_EPS) * g_ref[...]
    h = h * (1.0 + scale) + shift
    hb = h.astype(BF16)
    dc = u_ref.shape[1]
    d = sga_ref.shape[1]

    def seg(lo, hi):
        return jnp.dot(hb, wm_ref[:, lo:hi], preferred_element_type=F32)

    u_ref[...] = seg(0, dc) * _sigmoid(seg(dc, 2 * dc))
    qk_ref[...] = seg(2 * dc, 4 * dc)
    v_ref[...] = seg(4 * dc, 5 * dc)
    o_ref[...] = seg(5 * dc, 6 * dc)
    sga_ref[...] = _sigmoid(seg(6 * dc, 6 * dc + d))
    sgb_ref[...] = _sigmoid(seg(6 * dc + d, 6 * dc + 2 * d))
    ifc_ref[...] = jnp.dot(hb, wif_ref[...], preferred_element_type=F32)
    ifr_ref[...] = lax.dot_general(wift_ref[...], hb, (((1,), (1,)), ((), ())),
                                   preferred_element_type=F32)


def _inproj(x2, mod3, g1, w_main, w_if, w_ift, seq):
    T, D = x2.shape
    tm = INPROJ_TM
    dc = D // 2
    per_b = seq // tm
    row = lambda i: (i, 0)
    const = lambda i: (0, 0)
    return pl.pallas_call(
        _inproj_kernel,
        out_shape=[jax.ShapeDtypeStruct((T, dc), F32),
                   jax.ShapeDtypeStruct((T, 2 * dc), F32),
                   jax.ShapeDtypeStruct((T, dc), F32),
                   jax.ShapeDtypeStruct((T, dc), F32),
                   jax.ShapeDtypeStruct((T, D), F32),
                   jax.ShapeDtypeStruct((T, D), F32),
                   jax.ShapeDtypeStruct((T, LANES), F32),
                   jax.ShapeDtypeStruct((SUBLANES, T), F32)],
        grid=(T // tm,),
        in_specs=[pl.BlockSpec((tm, D), row),
                  pl.BlockSpec((1, 6, D), lambda i: (i // per_b, 0, 0)),
                  pl.BlockSpec((1, D), const),
                  pl.BlockSpec(w_main.shape, const),
                  pl.BlockSpec(w_if.shape, const),
                  pl.BlockSpec(w_ift.shape, const)],
        out_specs=[pl.BlockSpec((tm, dc), row),
                   pl.BlockSpec((tm, 2 * dc), row),
                   pl.BlockSpec((tm, dc), row),
                   pl.BlockSpec((tm, dc), row),
                   pl.BlockSpec((tm, D), row),
                   pl.BlockSpec((tm, D), row),
                   pl.BlockSpec((tm, LANES), row),
                   pl.BlockSpec((SUBLANES, tm), lambda i: (0, i))],
        compiler_params=_params("arbitrary"),
        name="inproj",
    )(x2, mod3, g1, w_main, w_if, w_ift)


def _conv_kernel(u_ref, w_ref, b_ref, lg_ref, lb_ref, wo_ref, y_ref, ubuf, sbuf, cbuf):
    ts = u_ref.shape[1]
    halo = CONV_HALO

    @pl.when(pl.program_id(1) == 0)
    def _():
        ubuf[0:halo, :] = jnp.zeros((halo, ubuf.shape[1]), F32)

    ubuf[halo:halo + ts, :] = u_ref[0]
    ns = sbuf.shape[1]
    for r in range(1, SUBLANES):
        sbuf[r - 1] = ubuf[r:r + ns, :]
    off = halo - (CONV_WIDTH - 1)
    for r0 in range(0, ts, CONV_RC):
        acc = jnp.broadcast_to(b_ref[...], (CONV_RC, ubuf.shape[1]))
        for k in range(CONV_WIDTH):
            r = (off + k) % SUBLANES
            lo = off + k - r + r0
            win = ubuf[lo:lo + CONV_RC, :] if r == 0 else sbuf[r - 1, lo:lo + CONV_RC, :]
            acc = acc + w_ref[k:k + 1, :] * win
        cbuf[r0:r0 + CONV_RC, :] = acc
    ubuf[0:halo, :] = ubuf[ts:ts + halo, :]

    a = cbuf[...]
    mu = jnp.mean(a, axis=-1, keepdims=True)
    ac = a - mu
    var = jnp.mean(ac * ac, axis=-1, keepdims=True)
    z = ac * lax.rsqrt(var + LN_EPS) * lg_ref[...] + lb_ref[...]
    z = z * _sigmoid(z)
    y_ref[0] = jnp.dot(z.astype(BF16), wo_ref[...], preferred_element_type=F32)


def _conv_branch(u3, w, b, lg, lb, wo):
    B, S, C = u3.shape
    D = wo.shape[1]
    ts = CONV_TS
    const = lambda bi, si: (0, 0)
    return pl.pallas_call(
        _conv_kernel,
        out_shape=jax.ShapeDtypeStruct((B, S, D), F32),
        grid=(B, S // ts),
        in_specs=[pl.BlockSpec((1, ts, C), lambda bi, si: (bi, si, 0)),
                  pl.BlockSpec(w.shape, const),
                  pl.BlockSpec((1, C), const),
                  pl.BlockSpec((1, C), const),
                  pl.BlockSpec((1, C), const),
                  pl.BlockSpec(wo.shape, const)],
        out_specs=pl.BlockSpec((1, ts, D), lambda bi, si: (bi, si, 0)),
        scratch_shapes=[pltpu.VMEM((ts + CONV_HALO, C), F32),
                        pltpu.VMEM((SUBLANES - 1, ts + CONV_HALO - SUBLANES, C), F32),
                        pltpu.VMEM((ts, C), F32)],
        compiler_params=_params("arbitrary", "arbitrary"),
        name="conv",
    )(u3, w, b, lg, lb, wo)


def _mlstm_kernel(qk_ref, v_ref, o_ref, ifc_ref, ifr_ref, cw_ref, cb_ref, bifc_ref, bifr_ref,
                  ng_ref, wo_ref, y_ref, qkbuf, cn_ref, m_ref, hbuf):
    L = qk_ref.shape[1]
    mi = v_ref.shape[2]
    dh = mi // M_HEADS
    halo = SUBLANES

    @pl.when(pl.program_id(1) == 0)
    def _():
        qkbuf[0:halo, :] = jnp.zeros((halo, qkbuf.shape[1]), F32)
        cn_ref[...] = jnp.zeros(cn_ref.shape, F32)
        m_ref[...] = jnp.zeros(m_ref.shape, F32)

    qkbuf[halo:halo + L, :] = qk_ref[0]
    off = halo - (QK_CONV_WIDTH - 1)
    y = jnp.broadcast_to(cb_ref[...], (L, qkbuf.shape[1]))
    for k in range(QK_CONV_WIDTH):
        y = y + cw_ref[k:k + 1, :] * qkbuf[off + k:off + k + L, :]
    y = y * _sigmoid(y)
    qkbuf[0:halo, :] = qkbuf[L:L + halo, :]

    ifr = ifr_ref[...] + bifr_ref[...]
    ifc = ifc_ref[...] + bifc_ref[...]
    lfr = _log_sigmoid(ifr)
    lfc = _log_sigmoid(ifc)
    rows = lax.broadcasted_iota(I32, (L, L), 0)
    cols = lax.broadcasted_iota(I32, (L, L), 1)
    causal = cols <= rows
    lower = causal.astype(F32)
    upper = (rows <= cols).astype(F32)
    bcum_c = jnp.dot(lower, lfc, preferred_element_type=F32, precision=lax.Precision.HIGHEST)
    bcum_r = jnp.dot(lfr, upper, preferred_element_type=F32, precision=lax.Precision.HIGHEST)

    lane = lax.broadcasted_iota(I32, (L, dh), 1)
    ones_col = jnp.where(lane == 0, 1.0, 0.0).astype(F32)
    vv = v_ref[0]
    oo = o_ref[0]
    scale = dh ** -0.5
    for hd in range(M_HEADS):
        q = y[:, hd * dh:(hd + 1) * dh] * scale
        kk = y[:, mi + hd * dh:mi + (hd + 1) * dh]
        v = vv[:, hd * dh:(hd + 1) * dh]
        kt = kk.T
        bc = bcum_c[:, M_HEADS + hd:M_HEADS + hd + 1]
        br = bcum_r[M_HEADS + hd:M_HEADS + hd + 1, :]
        li = ifr[hd:hd + 1, :]
        m_prev = m_ref[hd, 0:1, 0:1]
        dmat = jnp.where(causal, bc - br + li, -jnp.inf)
        inter = bc + m_prev
        m_t = jnp.maximum(jnp.max(dmat, axis=-1, keepdims=True), inter)
        wts = jnp.exp(dmat - m_t)
        s_inter = jnp.exp(inter - m_t)
        qb = q.astype(BF16)
        s_mat = jnp.dot(qb, kt.astype(BF16), preferred_element_type=F32) * wts
        cn = cn_ref[hd]
        qcn = jnp.dot(qb, cn.astype(BF16), preferred_element_type=F32)
        num = jnp.dot(s_mat.astype(BF16), v.astype(BF16), preferred_element_type=F32) \
            + s_inter * qcn[:, 0:dh]
        den = jnp.sum(s_mat, axis=-1, keepdims=True) + s_inter * qcn[:, dh:dh + 1]
        hh = num / jnp.maximum(jnp.abs(den), jnp.exp(-m_t))
        b_last = br[:, L - 1:L]
        a = b_last - br + li
        m_new = jnp.maximum(b_last + m_prev, jnp.max(a, axis=-1, keepdims=True))
        wk = jnp.exp(a - m_new)
        sc = jnp.exp(b_last + m_prev - m_new)
        v_ext = jnp.concatenate([v, ones_col], axis=1)
        cn_ref[hd] = sc * cn + jnp.dot((kt * wk).astype(BF16), v_ext.astype(BF16),
                                       preferred_element_type=F32)
        m_ref[hd] = jnp.broadcast_to(m_new, m_ref.shape[1:])
        mu = jnp.mean(hh, axis=-1, keepdims=True)
        hc = hh - mu
        var = jnp.mean(hc * hc, axis=-1, keepdims=True)
        hn = hc * lax.rsqrt(var + LN_EPS) * ng_ref[:, hd * dh:(hd + 1) * dh]
        hbuf[:, hd * dh:(hd + 1) * dh] = hn * _sigmoid(oo[:, hd * dh:(hd + 1) * dh])
    y_ref[0] = jnp.dot(hbuf[...].astype(BF16), wo_ref[...], preferred_element_type=F32)


def _mlstm_branch(qk3, v3, o3, ifc, ifr, cw, cb, bifc, bifr, ng, wo):
    B, S, C2 = qk3.shape
    mi = v3.shape[2]
    dh = mi // M_HEADS
    D = wo.shape[1]
    L = MLSTM_L
    nc = S // L
    const = lambda bi, ci: (0, 0)
    tile = lambda bi, ci: (bi, ci, 0)
    return pl.pallas_call(
        _mlstm_kernel,
        out_shape=jax.ShapeDtypeStruct((B, S, D), F32),
        grid=(B, nc),
        in_specs=[pl.BlockSpec((1, L, C2), tile),
                  pl.BlockSpec((1, L, mi), tile),
                  pl.BlockSpec((1, L, mi), tile),
                  pl.BlockSpec((L, LANES), lambda bi, ci: (bi * nc + ci, 0)),
                  pl.BlockSpec((SUBLANES, L), lambda bi, ci: (0, bi * nc + ci)),
                  pl.BlockSpec(cw.shape, const),
                  pl.BlockSpec((1, C2), const),
                  pl.BlockSpec((1, LANES), const),
                  pl.BlockSpec((SUBLANES, 1), const),
                  pl.BlockSpec((1, mi), const),
                  pl.BlockSpec(wo.shape, const)],
        out_specs=pl.BlockSpec((1, L, D), tile),
        scratch_shapes=[pltpu.VMEM((L + SUBLANES, C2), F32),
                        pltpu.VMEM((M_HEADS, dh, 2 * dh), F32),
                        pltpu.VMEM((M_HEADS, SUBLANES, LANES), F32),
                        pltpu.VMEM((L, mi), F32)],
        compiler_params=_params("arbitrary", "arbitrary"),
        name="mlstm",
    )(qk3, v3, o3, ifc, ifr, cw, cb, bifc, bifr, ng, wo)


def _merge_kernel(x_ref, ya_ref, yb_ref, sga_ref, sgb_ref, mod_ref, g2_ref, wo_ref, wr_ref, br_ref,
                  x1_ref, h2_ref, ri_ref, rf_ref, cnt_ref, run_ref):
    tm = x_ref.shape[0]

    @pl.when(pl.program_id(0) == 0)
    def _():
        run_ref[...] = jnp.zeros(run_ref.shape, F32)

    gate1 = mod_ref[0, 2:3, :]
    shift2 = mod_ref[0, 3:4, :]
    scale2 = mod_ref[0, 4:5, :]
    merged = sga_ref[...] * ya_ref[...] + sgb_ref[...] * yb_ref[...]
    mix = jnp.dot(merged.astype(BF16), wo_ref[...], preferred_element_type=F32)
    x1 = x_ref[...] + gate1 * mix
    x1_ref[...] = x1
    ms = jnp.mean(x1 * x1, axis=-1, keepdims=True)
    h2 = x1 * lax.rsqrt(ms + RMS_EPS) * g2_ref[...]
    h2 = h2 * (1.0 + scale2) + shift2
    h2_ref[...] = h2.reshape(h2_ref.shape)

    logits = jnp.dot(h2.astype(BF16), wr_ref[...], preferred_element_type=F32) + br_ref[...]
    lane = lax.broadcasted_iota(I32, (tm, LANES), 1).astype(F32)
    neg = -jnp.inf

    def first_argmax(vals):
        mx = jnp.max(vals, axis=-1, keepdims=True)
        idx = jnp.min(jnp.where(vals == mx, lane, float(LANES)), axis=-1, keepdims=True)
        return mx, idx

    lg = jnp.where(lane < N_GROUPS, logits, neg)
    gmax, gsel = first_argmax(lg)
    p_g = 1.0 / jnp.sum(jnp.exp(lg - gmax), axis=-1, keepdims=True)
    lo = N_GROUPS + gsel * E_PER_GROUP
    le = jnp.where((lane >= lo) & (lane < lo + E_PER_GROUP), logits, neg)
    l1, i1 = first_argmax(le)
    l2, i2 = first_argmax(jnp.where(lane == i1, neg, le))
    r = jnp.exp(l2 - l1)
    w1 = p_g / (1.0 + r)
    w2 = p_g * r / (1.0 + r)
    e1 = i1 - N_GROUPS
    e2 = i2 - N_GROUPS

    onehot = jnp.where((lane == e1) | (lane == e2), 1.0, 0.0)
    rows = lax.broadcasted_iota(I32, (tm, tm), 0)
    cols = lax.broadcasted_iota(I32, (tm, tm), 1)
    strict = jnp.where(cols < rows, 1.0, 0.0).astype(BF16)
    run = run_ref[0:1, :]
    before = jnp.dot(strict, onehot.astype(BF16), preferred_element_type=F32) + run
    rank1 = jnp.sum(jnp.where(lane == e1, before, 0.0), axis=-1, keepdims=True)
    rank2 = jnp.sum(jnp.where(lane == e2, before, 0.0), axis=-1, keepdims=True)
    run_new = run + jnp.sum(onehot, axis=0, keepdims=True)
    run_ref[...] = jnp.broadcast_to(run_new, run_ref.shape)
    cnt_ref[...] = jnp.broadcast_to(run_new, cnt_ref.shape).astype(I32)

    ri_ref[...] = jnp.where(lane == 0, e1 * float(RANK_RADIX) + rank1,
                            jnp.where(lane == 1, e2 * float(RANK_RADIX) + rank2, 0.0)).astype(I32)
    rf_ref[...] = jnp.where(lane == 0, w1, jnp.where(lane == 1, w2, 0.0))


def _merge(x2, ya, yb, sga, sgb, mod3, g2, wo, wr, br, seq):
    T, D = x2.shape
    tm = MERGE_TM
    per_b = seq // tm
    row = lambda i: (i, 0)
    const = lambda i: (0, 0)
    return pl.pallas_call(
        _merge_kernel,
        out_shape=[jax.ShapeDtypeStruct((T, D), F32),
                   jax.ShapeDtypeStruct((T, D // LANES, LANES), F32),
                   jax.ShapeDtypeStruct((T, LANES), I32),
                   jax.ShapeDtypeStruct((T, LANES), F32),
                   jax.ShapeDtypeStruct((SUBLANES, LANES), I32)],
        grid=(T // tm,),
        in_specs=[pl.BlockSpec((tm, D), row),
                  pl.BlockSpec((tm, D), row),
                  pl.BlockSpec((tm, D), row),
                  pl.BlockSpec((tm, D), row),
                  pl.BlockSpec((tm, D), row),
                  pl.BlockSpec((1, 6, D), lambda i: (i // per_b, 0, 0)),
                  pl.BlockSpec((1, D), const),
                  pl.BlockSpec(wo.shape, const),
                  pl.BlockSpec(wr.shape, const),
                  pl.BlockSpec((1, LANES), const)],
        out_specs=[pl.BlockSpec((tm, D), row),
                   pl.BlockSpec((tm, D // LANES, LANES), lambda i: (i, 0, 0)),
                   pl.BlockSpec((tm, LANES), row),
                   pl.BlockSpec((tm, LANES), row),
                   pl.BlockSpec((SUBLANES, LANES), const)],
        scratch_shapes=[pltpu.VMEM((SUBLANES, LANES), F32)],
        compiler_params=_params("arbitrary"),
        name="merge",
    )(x2, ya, yb, sga, sgb, mod3, g2, wo, wr, br)


def _invert_kernel(code_ref, cnt_ref, tok_ref, dst_ref, base_ref):
    n_assign = code_ref.shape[0]
    n_tok = n_assign // TOP_K
    tm = EXPERT_TM
    n_rows = tok_ref.shape[0]

    def pad_row(p, c):
        tok_ref[p] = 0
        dst_ref[p] = n_assign + (p & (tm - 1))
        return c

    def first_row(e, acc):
        base_ref[e] = acc
        end = acc + (cnt_ref[e] + tm - 1) // tm * tm
        lax.fori_loop(acc + cnt_ref[e], end, pad_row, 0)
        return end

    n_live = lax.fori_loop(0, N_EXPERTS, first_row, 0)

    def idle_tile(t, c):
        return lax.fori_loop(0, tm, lambda r, cc: pad_row(t * tm + r, cc), c, unroll=8)

    lax.fori_loop(n_live // tm, n_rows // tm, idle_tile, 0)

    def place(t, c):
        for k in range(TOP_K):
            code = code_ref[t * TOP_K + k]
            p = base_ref[lax.shift_right_logical(code, RANK_BITS)] + (code & (RANK_RADIX - 1))
            tok_ref[p] = t
            dst_ref[p] = k * n_tok + t
        return c

    lax.fori_loop(0, n_tok, place, 0, unroll=4)


def _invert(code, counts, n_rows):
    smem = pl.BlockSpec(memory_space=pltpu.SMEM)
    return pl.pallas_call(
        _invert_kernel,
        out_shape=[jax.ShapeDtypeStruct((n_rows,), I32), jax.ShapeDtypeStruct((n_rows,), I32)],
        in_specs=[smem, smem],
        out_specs=[smem, smem],
        scratch_shapes=[pltpu.SMEM((N_EXPERTS,), I32)],
        name="invert",
    )(code, counts)


def _expert_kernel(te_ref, first_ref, nt_ref, tok_ref, dst_ref, h_ref, wg_ref, wu_ref, wd_ref, yk_ref,
                   wgb, wub, wdb, xbuf0, xbuf1, obuf0, obuf1, gsem, ssem):
    j = pl.program_id(0)
    nt = nt_ref[0]
    xbuf = (xbuf0, xbuf1)
    obuf = (obuf0, obuf1)
    tm = xbuf0.shape[0]
    n_tok = h_ref.shape[0]
    n_assign = n_tok * TOP_K

    def gather_start(tile, slot):
        for r in range(tm):
            pltpu.make_async_copy(h_ref.at[tok_ref[tile * tm + r]], xbuf[slot].at[r],
                                  gsem.at[slot]).start()

    def gather_wait(slot):
        pltpu.make_async_copy(h_ref.at[pl.ds(0, tm)], xbuf[slot], gsem.at[slot]).wait()

    def scatter_start(tile, slot, live):
        for r in range(tm):
            dst = dst_ref[tile * tm + r]
            if live is not True:
                dst = jnp.where(live, dst, n_assign + r)
            pltpu.make_async_copy(obuf[slot].at[r], yk_ref.at[dst], ssem.at[slot]).start(priority=1)

    def scatter_wait(slot):
        pltpu.make_async_copy(obuf[slot], yk_ref.at[pl.ds(0, tm)], ssem.at[slot]).wait()

    @pl.when(j == 0)
    def _():
        obuf1[...] = jnp.zeros(obuf1.shape, F32)
        gather_start(0, 0)

    @pl.when((j < nt) & (first_ref[j] == 1))
    def _():
        wgb[...] = wg_ref[0].astype(BF16)
        wub[...] = wu_ref[0].astype(BF16)
        wdb[...] = wd_ref[0].astype(BF16)

    def live_step(slot):
        other = 1 - slot
        gather_wait(slot)
        if slot == 0:
            @pl.when(j > 0)
            def _():
                scatter_wait(slot)
        else:
            scatter_wait(slot)
        gather_start(j + 1, other)
        scatter_start(jnp.maximum(j - 1, 0), other, (j > 0) if slot == 0 else True)
        xb = xbuf[slot][...].reshape(tm, wgb.shape[0]).astype(BF16)
        g = jnp.dot(xb, wgb[...], preferred_element_type=F32)
        u = jnp.dot(xb, wub[...], preferred_element_type=F32)
        act = (g * _sigmoid(g)) * u
        y = jnp.dot(act.astype(BF16), wdb[...], preferred_element_type=F32)
        obuf[slot][...] = y.reshape(obuf[slot].shape)

    def drain(last):
        gather_wait(1 - last)
        scatter_wait(1 - last)
        scatter_start(nt - 1, last, True)
        scatter_wait(last)

    for parity in range(2):
        pl.when((j < nt) & (j % 2 == parity))(functools.partial(live_step, parity))
        pl.when((j == nt) & ((nt - 1) % 2 == parity))(functools.partial(drain, parity))


def _experts(tile_e, tile_first, n_tiles, inv_tok, inv_dst, h2, wg, wu, wd, max_tiles):
    T, rs, rl = h2.shape
    D = rs * rl
    de = wg.shape[2]
    tm = EXPERT_TM
    wmap = lambda j, te, tf, nt, it, idst: (te[j], 0, 0)
    return pl.pallas_call(
        _expert_kernel,
        out_shape=jax.ShapeDtypeStruct((T * TOP_K + tm, rs, rl), F32),
        grid_spec=pltpu.PrefetchScalarGridSpec(
            num_scalar_prefetch=5,
            grid=(max_tiles + 1,),
            in_specs=[pl.BlockSpec(memory_space=pl.ANY),
                      pl.BlockSpec((1, D, de), wmap),
                      pl.BlockSpec((1, D, de), wmap),
                      pl.BlockSpec((1, de, D), wmap)],
            out_specs=pl.BlockSpec(memory_space=pl.ANY),
            scratch_shapes=[pltpu.VMEM((D, de), BF16),
                            pltpu.VMEM((D, de), BF16),
                            pltpu.VMEM((de, D), BF16),
                            pltpu.VMEM((tm, rs, rl), F32),
                            pltpu.VMEM((tm, rs, rl), F32),
                            pltpu.VMEM((tm, rs, rl), F32),
                            pltpu.VMEM((tm, rs, rl), F32),
                            pltpu.SemaphoreType.DMA((2,)),
                            pltpu.SemaphoreType.DMA((2,))]),
        compiler_params=_params("arbitrary"),
        name="experts",
    )(tile_e, tile_first, n_tiles, inv_tok, inv_dst, h2, wg, wu, wd)


def _combine_kernel(x1_ref, rf_ref, mod_ref, gf_ref, y0_ref, y1_ref, out_ref, *, final_norm):
    gate2 = mod_ref[0, 5:6, :]
    w = rf_ref[...]
    y0 = y0_ref[...].reshape(x1_ref.shape)
    y1 = y1_ref[...].reshape(x1_ref.shape)
    moe = w[:, 0:1] * y0 + w[:, 1:2] * y1
    x2 = x1_ref[...] + gate2 * moe
    if final_norm:
        ms = jnp.mean(x2 * x2, axis=-1, keepdims=True)
        x2 = x2 * lax.rsqrt(ms + RMS_EPS) * gf_ref[...]
    out_ref[...] = x2


def _combine(x1, rf, mod3, gf, yk, seq, final_norm):
    T, D = x1.shape
    tc = COMBINE_TM
    per_b = seq // tc
    n_blk = T // tc
    return pl.pallas_call(
        functools.partial(_combine_kernel, final_norm=final_norm),
        out_shape=jax.ShapeDtypeStruct((T, D), F32),
        grid=(n_blk,),
        in_specs=[pl.BlockSpec((tc, D), lambda i: (i, 0)),
                  pl.BlockSpec((tc, LANES), lambda i: (i, 0)),
                  pl.BlockSpec((1, 6, D), lambda i: (i // per_b, 0, 0)),
                  pl.BlockSpec((1, D), lambda i: (0, 0)),
                  pl.BlockSpec((tc,) + yk.shape[1:], lambda i: (i, 0, 0)),
                  pl.BlockSpec((tc,) + yk.shape[1:], lambda i: (n_blk + i, 0, 0))],
        out_specs=pl.BlockSpec((tc, D), lambda i: (i, 0)),
        compiler_params=_params("arbitrary"),
        name="combine",
    )(x1, rf, mod3, gf, yk, yk)


def _layer(x2, c, seq, w_ada, b_ada, g_norm1, w_in, b_if, conv_dw_w, conv_dw_b, conv_ln_g, conv_ln_b,
           w_conv_out, qk_conv_w, qk_conv_b, m_norm_g, w_m_out, w_out, g_norm2, w_rg, b_rg,
           w_re, b_re, w_e_gate, w_e_up, w_e_down):
    T, D = x2.shape
    B = T // seq
    dc = D // 2
    nif = 2 * M_HEADS

    mod3 = _ada(c, w_ada, b_ada).reshape(B, 6, D)

    if_lo = 6 * dc
    w_main = jnp.concatenate([w_in[:, :if_lo], w_in[:, if_lo + nif:]], axis=1).astype(BF16)
    w_if = w_in[:, if_lo:if_lo + nif]
    w_if_pad = jnp.pad(w_if, ((0, 0), (0, LANES - nif))).astype(BF16)
    w_ift = w_if.T.astype(BF16)
    u, qk, v, o, sga, sgb, ifc, ifr = _inproj(x2, mod3, g_norm1.reshape(1, D), w_main, w_if_pad, w_ift, seq)

    ya = _conv_branch(u.reshape(B, seq, dc), conv_dw_w, conv_dw_b.reshape(1, dc),
                      conv_ln_g.reshape(1, dc), conv_ln_b.reshape(1, dc), w_conv_out.astype(BF16))
    bifc = jnp.pad(b_if, (0, LANES - nif)).reshape(1, LANES)
    bifr = b_if.reshape(nif, 1)
    yb = _mlstm_branch(qk.reshape(B, seq, 2 * dc), v.reshape(B, seq, dc), o.reshape(B, seq, dc),
                       ifc, ifr, qk_conv_w, qk_conv_b.reshape(1, 2 * dc), bifc, bifr,
                       m_norm_g.reshape(1, dc), w_m_out.astype(BF16))

    n_r = N_GROUPS + N_EXPERTS
    w_r = jnp.pad(jnp.concatenate([w_rg, w_re], axis=1), ((0, 0), (0, LANES - n_r))).astype(BF16)
    b_r = jnp.pad(jnp.concatenate([b_rg, b_re]), (0, LANES - n_r)).reshape(1, LANES)
    x1, h2, ri, rf, cnt = _merge(x2, ya.reshape(T, D), yb.reshape(T, D), sga, sgb, mod3,
                                 g_norm2.reshape(1, D), w_out.astype(BF16), w_r, b_r, seq)

    tm = EXPERT_TM
    counts = cnt[0, :N_EXPERTS]
    tiles_e = (counts + tm - 1) // tm
    tile_end = jnp.cumsum(tiles_e)
    base = (tile_end - tiles_e) * tm
    n_tiles = tile_end[-1]
    max_tiles = (T * TOP_K) // tm + N_EXPERTS
    jt = jnp.arange(max_tiles + 1, dtype=I32)
    jc = jnp.minimum(jt, n_tiles - 1)
    tile_e = jnp.sum(jc[:, None] >= tile_end[None, :], axis=1).astype(I32)
    tile_first = ((jt < n_tiles) & (jc == (tile_end - tiles_e)[tile_e])).astype(I32)

    code = ri[:, 0:TOP_K].reshape(T * TOP_K)
    inv_tok, inv_dst = _invert(code, counts, (max_tiles + 1) * tm)
    yk = _experts(tile_e, tile_first, n_tiles.reshape(1).astype(I32), inv_tok, inv_dst, h2,
                  w_e_gate, w_e_up, w_e_down, max_tiles)
    return x1, rf, mod3, yk


def kernel(x, c, w_ada, b_ada, g_norm1, w_in, b_if, conv_dw_w, conv_dw_b, conv_ln_g, conv_ln_b,
           w_conv_out, qk_conv_w, qk_conv_b, m_norm_g, w_m_out, w_out, g_norm2, w_rg, b_rg,
           w_re, b_re, w_e_gate, w_e_up, w_e_down, g_final):
    B, S, D = x.shape
    depth = w_ada.shape[0]
    x2 = x.reshape(B * S, D)
    for l in range(depth):
        x1, rf, mod3, yk = _layer(
            x2, c, S, w_ada[l], b_ada[l], g_norm1[l], w_in[l], b_if[l], conv_dw_w[l], conv_dw_b[l],
            conv_ln_g[l], conv_ln_b[l], w_conv_out[l], qk_conv_w[l], qk_conv_b[l], m_norm_g[l],
            w_m_out[l], w_out[l], g_norm2[l], w_rg[l], b_rg[l], w_re[l], b_re[l],
            w_e_gate[l], w_e_up[l], w_e_down[l])
        x2 = _combine(x1, rf, mod3, g_final.reshape(1, D), yk, S, final_norm=l == depth - 1)
    return x2.reshape(B, S, D)
```

```python
import functools

import jax
import jax.numpy as jnp
from jax import lax
from jax.experimental import pallas as pl
from jax.experimental.pallas import tpu as pltpu

F32 = jnp.float32
BF16 = jnp.bfloat16
I32 = jnp.int32

M_HEADS = 4
CONV_WIDTH = 31
QK_CONV_WIDTH = 4
N_GROUPS = 4
E_PER_GROUP = 8
N_EXPERTS = N_GROUPS * E_PER_GROUP
TOP_K = 2
RMS_EPS = 1e-6
LN_EPS = 1e-5

LANES = 128
SUBLANES = 8
VMEM_LIMIT = 56 * 1024 * 1024

ADA_TN = 1024
INPROJ_TM = 256
CONV_TS = 256
CONV_HALO = 32
CONV_RC = 32
MLSTM_L = 128
MERGE_TM = 256
EXPERT_TM = 256
COMBINE_TM = 256
RANK_BITS = 16
RANK_RADIX = 1 << RANK_BITS
assert EXPERT_TM & (EXPERT_TM - 1) == 0


def _sigmoid(v):
    return 1.0 / (1.0 + jnp.exp(-v))


def _log_sigmoid(v):
    return -(jnp.maximum(-v, 0.0) + jnp.log1p(jnp.exp(-jnp.abs(v))))


def _params(*sem):
    return pltpu.CompilerParams(dimension_semantics=sem, vmem_limit_bytes=VMEM_LIMIT)


def _ada_kernel(c_ref, w_ref, b_ref, o_ref):
    c = c_ref[...]
    s = c * _sigmoid(c)
    o_ref[...] = jnp.dot(s, w_ref[...], preferred_element_type=F32,
                         precision=lax.Precision.HIGHEST) + b_ref[...]


def _ada(c, w_ada, b_ada):
    B, D = c.shape
    N = w_ada.shape[1]
    return pl.pallas_call(
        _ada_kernel,
        out_shape=jax.ShapeDtypeStruct((B, N), F32),
        grid=(N // ADA_TN,),
        in_specs=[pl.BlockSpec((B, D), lambda j: (0, 0)),
                  pl.BlockSpec((D, ADA_TN), lambda j: (0, j)),
                  pl.BlockSpec((1, ADA_TN), lambda j: (0, j))],
        out_specs=pl.BlockSpec((B, ADA_TN), lambda j: (0, j)),
        compiler_params=_params("arbitrary"),
        name="ada",
    )(c, w_ada, b_ada.reshape(1, N))


def _inproj_kernel(x_ref, mod_ref, g_ref, wm_ref, wif_ref, wift_ref,
                   u_ref, qk_ref, v_ref, o_ref, sga_ref, sgb_ref, ifc_ref, ifr_ref):
    x = x_ref[...]
    shift = mod_ref[0, 0:1, :]
    scale = mod_ref[0, 1:2, :]
    ms = jnp.mean(x * x, axis=-1, keepdims=True)
    h = x * lax.rsqrt(ms + RMS_EPS) * g_ref[...]
    h = h * (1.0 + scale) + shift
    hb = h.astype(BF16)
    dc = u_ref.shape[1]
    d = sga_ref.shape[1]

    def seg(lo, hi):
        return jnp.dot(hb, wm_ref[:, lo:hi], preferred_element_type=F32)

    u_ref[...] = seg(0, dc) * _sigmoid(seg(dc, 2 * dc))
    qk_ref[...] = seg(2 * dc, 4 * dc)
    v_ref[...] = seg(4 * dc, 5 * dc)
    o_ref[...] = seg(5 * dc, 6 * dc)
    sga_ref[...] = _sigmoid(seg(6 * dc, 6 * dc + d))
    sgb_ref[...] = _sigmoid(seg(6 * dc + d, 6 * dc + 2 * d))
    ifc_ref[...] = jnp.dot(hb, wif_ref[...], preferred_element_type=F32)
    ifr_ref[...] = lax.dot_general(wift_ref[...], hb, (((1,), (1,)), ((), ())),
                                   preferred_element_type=F32)


def _inproj(x2, mod3, g1, w_main, w_if, w_ift, seq):
    T, D = x2.shape
    tm = INPROJ_TM
    dc = D // 2
    per_b = seq // tm
    row = lambda i: (i, 0)
    const = lambda i: (0, 0)
    return pl.pallas_call(
        _inproj_kernel,
        out_shape=[jax.ShapeDtypeStruct((T, dc), F32),
                   jax.ShapeDtypeStruct((T, 2 * dc), F32),
                   jax.ShapeDtypeStruct((T, dc), F32),
                   jax.ShapeDtypeStruct((T, dc), F32),
                   jax.ShapeDtypeStruct((T, D), F32),
                   jax.ShapeDtypeStruct((T, D), F32),
                   jax.ShapeDtypeStruct((T, LANES), F32),
                   jax.ShapeDtypeStruct((SUBLANES, T), F32)],
        grid=(T // tm,),
        in_specs=[pl.BlockSpec((tm, D), row),
                  pl.BlockSpec((1, 6, D), lambda i: (i // per_b, 0, 0)),
                  pl.BlockSpec((1, D), const),
                  pl.BlockSpec(w_main.shape, const),
                  pl.BlockSpec(w_if.shape, const),
                  pl.BlockSpec(w_ift.shape, const)],
        out_specs=[pl.BlockSpec((tm, dc), row),
                   pl.BlockSpec((tm, 2 * dc), row),
                   pl.BlockSpec((tm, dc), row),
                   pl.BlockSpec((tm, dc), row),
                   pl.BlockSpec((tm, D), row),
                   pl.BlockSpec((tm, D), row),
                   pl.BlockSpec((tm, LANES), row),
                   pl.BlockSpec((SUBLANES, tm), lambda i: (0, i))],
        compiler_params=_params("arbitrary"),
        name="inproj",
    )(x2, mod3, g1, w_main, w_if, w_ift)


def _conv_kernel(u_ref, w_ref, b_ref, lg_ref, lb_ref, wo_ref, y_ref, ubuf, sbuf, cbuf):
    ts = u_ref.shape[1]
    halo = CONV_HALO

    @pl.when(pl.program_id(1) == 0)
    def _():
        ubuf[0:halo, :] = jnp.zeros((halo, ubuf.shape[1]), F32)

    ubuf[halo:halo + ts, :] = u_ref[0]
    ns = sbuf.shape[1]
    for r in range(1, SUBLANES):
        sbuf[r - 1] = ubuf[r:r + ns, :]
    off = halo - (CONV_WIDTH - 1)
    for r0 in range(0, ts, CONV_RC):
        acc = jnp.broadcast_to(b_ref[...], (CONV_RC, ubuf.shape[1]))
        for k in range(CONV_WIDTH):
            r = (off + k) % SUBLANES
            lo = off + k - r + r0
            win = ubuf[lo:lo + CONV_RC, :] if r == 0 else sbuf[r - 1, lo:lo + CONV_RC, :]
            acc = acc + w_ref[k:k + 1, :] * win
        cbuf[r0:r0 + CONV_RC, :] = acc
    ubuf[0:halo, :] = ubuf[ts:ts + halo, :]

    a = cbuf[...]
    mu = jnp.mean(a, axis=-1, keepdims=True)
    ac = a - mu
    var = jnp.mean(ac * ac, axis=-1, keepdims=True)
    z = ac * lax.rsqrt(var + LN_EPS) * lg_ref[...] + lb_ref[...]
    z = z * _sigmoid(z)
    y_ref[0] = jnp.dot(z.astype(BF16), wo_ref[...], preferred_element_type=F32)


def _conv_branch(u3, w, b, lg, lb, wo):
    B, S, C = u3.shape
    D = wo.shape[1]
    ts = CONV_TS
    const = lambda bi, si: (0, 0)
    return pl.pallas_call(
        _conv_kernel,
        out_shape=jax.ShapeDtypeStruct((B, S, D), F32),
        grid=(B, S // ts),
        in_specs=[pl.BlockSpec((1, ts, C), lambda bi, si: (bi, si, 0)),
                  pl.BlockSpec(w.shape, const),
                  pl.BlockSpec((1, C), const),
                  pl.BlockSpec((1, C), const),
                  pl.BlockSpec((1, C), const),
                  pl.BlockSpec(wo.shape, const)],
        out_specs=pl.BlockSpec((1, ts, D), lambda bi, si: (bi, si, 0)),
        scratch_shapes=[pltpu.VMEM((ts + CONV_HALO, C), F32),
                        pltpu.VMEM((SUBLANES - 1, ts + CONV_HALO - SUBLANES, C), F32),
                        pltpu.VMEM((ts, C), F32)],
        compiler_params=_params("arbitrary", "arbitrary"),
        name="conv",
    )(u3, w, b, lg, lb, wo)


def _mlstm_kernel(qk_ref, v_ref, o_ref, ifc_ref, ifr_ref, cw_ref, cb_ref, bifc_ref, bifr_ref,
                  ng_ref, wo_ref, y_ref, qkbuf, cn_ref, m_ref, hbuf):
    L = qk_ref.shape[1]
    mi = v_ref.shape[2]
    dh = mi // M_HEADS
    halo = SUBLANES

    @pl.when(pl.program_id(1) == 0)
    def _():
        qkbuf[0:halo, :] = jnp.zeros((halo, qkbuf.shape[1]), F32)
        cn_ref[...] = jnp.zeros(cn_ref.shape, F32)
        m_ref[...] = jnp.zeros(m_ref.shape, F32)

    qkbuf[halo:halo + L, :] = qk_ref[0]
    off = halo - (QK_CONV_WIDTH - 1)
    y = jnp.broadcast_to(cb_ref[...], (L, qkbuf.shape[1]))
    for k in range(QK_CONV_WIDTH):
        y = y + cw_ref[k:k + 1, :] * qkbuf[off + k:off + k + L, :]
    y = y * _sigmoid(y)
    qkbuf[0:halo, :] = qkbuf[L:L + halo, :]

    ifr = ifr_ref[...] + bifr_ref[...]
    ifc = ifc_ref[...] + bifc_ref[...]
    lfr = _log_sigmoid(ifr)
    lfc = _log_sigmoid(ifc)
    rows = lax.broadcasted_iota(I32, (L, L), 0)
    cols = lax.broadcasted_iota(I32, (L, L), 1)
    causal = cols <= rows
    lower = causal.astype(F32)
    upper = (rows <= cols).astype(F32)
    bcum_c = jnp.dot(lower, lfc, preferred_element_type=F32, precision=lax.Precision.HIGHEST)
    bcum_r = jnp.dot(lfr, upper, preferred_element_type=F32, precision=lax.Precision.HIGHEST)

    lane = lax.broadcasted_iota(I32, (L, dh), 1)
    ones_col = jnp.where(lane == 0, 1.0, 0.0).astype(F32)
    vv = v_ref[0]
    oo = o_ref[0]
    scale = dh ** -0.5
    for hd in range(M_HEADS):
        q = y[:, hd * dh:(hd + 1) * dh] * scale
        kk = y[:, mi + hd * dh:mi + (hd + 1) * dh]
        v = vv[:, hd * dh:(hd + 1) * dh]
        kt = kk.T
        bc = bcum_c[:, M_HEADS + hd:M_HEADS + hd + 1]
        br = bcum_r[M_HEADS + hd:M_HEADS + hd + 1, :]
        li = ifr[hd:hd + 1, :]
        m_prev = m_ref[hd, 0:1, 0:1]
        dmat = jnp.where(causal, bc - br + li, -jnp.inf)
        inter = bc + m_prev
        m_t = jnp.maximum(jnp.max(dmat, axis=-1, keepdims=True), inter)
        wts = jnp.exp(dmat - m_t)
        s_inter = jnp.exp(inter - m_t)
        qb = q.astype(BF16)
        s_mat = jnp.dot(qb, kt.astype(BF16), preferred_element_type=F32) * wts
        cn = cn_ref[hd]
        qcn = jnp.dot(qb, cn.astype(BF16), preferred_element_type=F32)
        num = jnp.dot(s_mat.astype(BF16), v.astype(BF16), preferred_element_type=F32) \
            + s_inter * qcn[:, 0:dh]
        den = jnp.sum(s_mat, axis=-1, keepdims=True) + s_inter * qcn[:, dh:dh + 1]
        hh = num / jnp.maximum(jnp.abs(den), jnp.exp(-m_t))
        b_last = br[:, L - 1:L]
        a = b_last - br + li
        m_new = jnp.maximum(b_last + m_prev, jnp.max(a, axis=-1, keepdims=True))
        wk = jnp.exp(a - m_new)
        sc = jnp.exp(b_last + m_prev - m_new)
        v_ext = jnp.concatenate([v, ones_col], axis=1)
        cn_ref[hd] = sc * cn + jnp.dot((kt * wk).astype(BF16), v_ext.astype(BF16),
                                       preferred_element_type=F32)
        m_ref[hd] = jnp.broadcast_to(m_new, m_ref.shape[1:])
        mu = jnp.mean(hh, axis=-1, keepdims=True)
        hc = hh - mu
        var = jnp.mean(hc * hc, axis=-1, keepdims=True)
        hn = hc * lax.rsqrt(var + LN_EPS) * ng_ref[:, hd * dh:(hd + 1) * dh]
        hbuf[:, hd * dh:(hd + 1) * dh] = hn * _sigmoid(oo[:, hd * dh:(hd + 1) * dh])
    y_ref[0] = jnp.dot(hbuf[...].astype(BF16), wo_ref[...], preferred_element_type=F32)


def _mlstm_branch(qk3, v3, o3, ifc, ifr, cw, cb, bifc, bifr, ng, wo):
    B, S, C2 = qk3.shape
    mi = v3.shape[2]
    dh = mi // M_HEADS
    D = wo.shape[1]
    L = MLSTM_L
    nc = S // L
    const = lambda bi, ci: (0, 0)
    tile = lambda bi, ci: (bi, ci, 0)
    return pl.pallas_call(
        _mlstm_kernel,
        out_shape=jax.ShapeDtypeStruct((B, S, D), F32),
        grid=(B, nc),
        in_specs=[pl.BlockSpec((1, L, C2), tile),
                  pl.BlockSpec((1, L, mi), tile),
                  pl.BlockSpec((1, L, mi), tile),
                  pl.BlockSpec((L, LANES), lambda bi, ci: (bi * nc + ci, 0)),
                  pl.BlockSpec((SUBLANES, L), lambda bi, ci: (0, bi * nc + ci)),
                  pl.BlockSpec(cw.shape, const),
                  pl.BlockSpec((1, C2), const),
                  pl.BlockSpec((1, LANES), const),
                  pl.BlockSpec((SUBLANES, 1), const),
                  pl.BlockSpec((1, mi), const),
                  pl.BlockSpec(wo.shape, const)],
        out_specs=pl.BlockSpec((1, L, D), tile),
        scratch_shapes=[pltpu.VMEM((L + SUBLANES, C2), F32),
                        pltpu.VMEM((M_HEADS, dh, 2 * dh), F32),
                        pltpu.VMEM((M_HEADS, SUBLANES, LANES), F32),
                        pltpu.VMEM((L, mi), F32)],
        compiler_params=_params("arbitrary", "arbitrary"),
        name="mlstm",
    )(qk3, v3, o3, ifc, ifr, cw, cb, bifc, bifr, ng, wo)


def _merge_kernel(x_ref, ya_ref, yb_ref, sga_ref, sgb_ref, mod_ref, g2_ref, wo_ref, wr_ref, br_ref,
                  x1_ref, h2_ref, ri_ref, rf_ref, cnt_ref, run_ref):
    tm = x_ref.shape[0]

    @pl.when(pl.program_id(0) == 0)
    def _():
        run_ref[...] = jnp.zeros(run_ref.shape, F32)

    gate1 = mod_ref[0, 2:3, :]
    shift2 = mod_ref[0, 3:4, :]
    scale2 = mod_ref[0, 4:5, :]
    merged = sga_ref[...] * ya_ref[...] + sgb_ref[...] * yb_ref[...]
    mix = jnp.dot(merged.astype(BF16), wo_ref[...], preferred_element_type=F32)
    x1 = x_ref[...] + gate1 * mix
    x1_ref[...] = x1
    ms = jnp.mean(x1 * x1, axis=-1, keepdims=True)
    h2 = x1 * lax.rsqrt(ms + RMS_EPS) * g2_ref[...]
    h2 = h2 * (1.0 + scale2) + shift2
    h2_ref[...] = h2.reshape(h2_ref.shape)

    logits = jnp.dot(h2.astype(BF16), wr_ref[...], preferred_element_type=F32) + br_ref[...]
    lane = lax.broadcasted_iota(I32, (tm, LANES), 1).astype(F32)
    neg = -jnp.inf

    def first_argmax(vals):
        mx = jnp.max(vals, axis=-1, keepdims=True)
        idx = jnp.min(jnp.where(vals == mx, lane, float(LANES)), axis=-1, keepdims=True)
        return mx, idx

    lg = jnp.where(lane < N_GROUPS, logits, neg)
    gmax, gsel = first_argmax(lg)
    p_g = 1.0 / jnp.sum(jnp.exp(lg - gmax), axis=-1, keepdims=True)
    lo = N_GROUPS + gsel * E_PER_GROUP
    le = jnp.where((lane >= lo) & (lane < lo + E_PER_GROUP), logits, neg)
    l1, i1 = first_argmax(le)
    l2, i2 = first_argmax(jnp.where(lane == i1, neg, le))
    r = jnp.exp(l2 - l1)
    w1 = p_g / (1.0 + r)
    w2 = p_g * r / (1.0 + r)
    e1 = i1 - N_GROUPS
    e2 = i2 - N_GROUPS

    onehot = jnp.where((lane == e1) | (lane == e2), 1.0, 0.0)
    rows = lax.broadcasted_iota(I32, (tm, tm), 0)
    cols = lax.broadcasted_iota(I32, (tm, tm), 1)
    strict = jnp.where(cols < rows, 1.0, 0.0).astype(BF16)
    run = run_ref[0:1, :]
    before = jnp.dot(strict, onehot.astype(BF16), preferred_element_type=F32) + run
    rank1 = jnp.sum(jnp.where(lane == e1, before, 0.0), axis=-1, keepdims=True)
    rank2 = jnp.sum(jnp.where(lane == e2, before, 0.0), axis=-1, keepdims=True)
    run_new = run + jnp.sum(onehot, axis=0, keepdims=True)
    run_ref[...] = jnp.broadcast_to(run_new, run_ref.shape)
    cnt_ref[...] = jnp.broadcast_to(run_new, cnt_ref.shape).astype(I32)

    ri_ref[...] = jnp.where(lane == 0, e1 * float(RANK_RADIX) + rank1,
                            jnp.where(lane == 1, e2 * float(RANK_RADIX) + rank2, 0.0)).astype(I32)
    rf_ref[...] = jnp.where(lane == 0, w1, jnp.where(lane == 1, w2, 0.0))


def _merge(x2, ya, yb, sga, sgb, mod3, g2, wo, wr, br, seq):
    T, D = x2.shape
    tm = MERGE_TM
    per_b = seq // tm
    row = lambda i: (i, 0)
    const = lambda i: (0, 0)
    return pl.pallas_call(
        _merge_kernel,
        out_shape=[jax.ShapeDtypeStruct((T, D), F32),
                   jax.ShapeDtypeStruct((T, D // LANES, LANES), F32),
                   jax.ShapeDtypeStruct((T, LANES), I32),
                   jax.ShapeDtypeStruct((T, LANES), F32),
                   jax.ShapeDtypeStruct((SUBLANES, LANES), I32)],
        grid=(T // tm,),
        in_specs=[pl.BlockSpec((tm, D), row),
                  pl.BlockSpec((tm, D), row),
                  pl.BlockSpec((tm, D), row),
                  pl.BlockSpec((tm, D), row),
                  pl.BlockSpec((tm, D), row),
                  pl.BlockSpec((1, 6, D), lambda i: (i // per_b, 0, 0)),
                  pl.BlockSpec((1, D), const),
                  pl.BlockSpec(wo.shape, const),
                  pl.BlockSpec(wr.shape, const),
                  pl.BlockSpec((1, LANES), const)],
        out_specs=[pl.BlockSpec((tm, D), row),
                   pl.BlockSpec((tm, D // LANES, LANES), lambda i: (i, 0, 0)),
                   pl.BlockSpec((tm, LANES), row),
                   pl.BlockSpec((tm, LANES), row),
                   pl.BlockSpec((SUBLANES, LANES), const)],
        scratch_shapes=[pltpu.VMEM((SUBLANES, LANES), F32)],
        compiler_params=_params("arbitrary"),
        name="merge",
    )(x2, ya, yb, sga, sgb, mod3, g2, wo, wr, br)


def _invert_kernel(code_ref, cnt_ref, tok_ref, base_ref):
    n_assign = code_ref.shape[0]
    n_tok = n_assign // TOP_K
    tm = EXPERT_TM
    n_rows = tok_ref.shape[0]

    def pad_row(p, c):
        tok_ref[p] = 0
        return c

    def first_row(e, acc):
        base_ref[e] = acc
        end = acc + (cnt_ref[e] + tm - 1) // tm * tm
        lax.fori_loop(acc + cnt_ref[e], end, pad_row, 0)
        return end

    n_live = lax.fori_loop(0, N_EXPERTS, first_row, 0)

    def idle_tile(t, c):
        return lax.fori_loop(0, tm, lambda r, cc: pad_row(t * tm + r, cc), c, unroll=8)

    lax.fori_loop(n_live // tm, n_rows // tm, idle_tile, 0)

    def place(t, c):
        for k in range(TOP_K):
            code = code_ref[t * TOP_K + k]
            tok_ref[_code_row(base_ref, code)] = t
        return c

    lax.fori_loop(0, n_tok, place, 0, unroll=4)


def _code_row(base_ref, code):
    return base_ref[lax.shift_right_logical(code, RANK_BITS)] + (code & (RANK_RADIX - 1))


def _invert(code, counts, n_rows):
    smem = pl.BlockSpec(memory_space=pltpu.SMEM)
    return pl.pallas_call(
        _invert_kernel,
        out_shape=[jax.ShapeDtypeStruct((n_rows,), I32), jax.ShapeDtypeStruct((N_EXPERTS,), I32)],
        in_specs=[smem, smem],
        out_specs=[smem, smem],
        name="invert",
    )(code, counts)


def _expert_kernel(te_ref, first_ref, nt_ref, tok_ref, h_ref, wg_ref, wu_ref, wd_ref, ys_ref,
                   wgb, wub, wdb, xbuf0, xbuf1, gsem):
    j = pl.program_id(0)
    nt = nt_ref[0]
    xbuf = (xbuf0, xbuf1)
    tm = xbuf0.shape[0]

    def gather_start(tile, slot):
        for r in range(tm):
            pltpu.make_async_copy(h_ref.at[tok_ref[tile * tm + r]], xbuf[slot].at[r],
                                  gsem.at[slot]).start()

    def gather_wait(slot):
        pltpu.make_async_copy(h_ref.at[pl.ds(0, tm)], xbuf[slot], gsem.at[slot]).wait()

    @pl.when(j == 0)
    def _():
        gather_start(0, 0)

    @pl.when((j < nt) & (first_ref[j] == 1))
    def _():
        wgb[...] = wg_ref[0].astype(BF16)
        wub[...] = wu_ref[0].astype(BF16)
        wdb[...] = wd_ref[0].astype(BF16)

    def live_step(slot):
        gather_wait(slot)
        gather_start(j + 1, 1 - slot)
        xb = xbuf[slot][...].reshape(tm, wgb.shape[0]).astype(BF16)
        g = jnp.dot(xb, wgb[...], preferred_element_type=F32)
        u = jnp.dot(xb, wub[...], preferred_element_type=F32)
        act = (g * _sigmoid(g)) * u
        ys_ref[...] = jnp.dot(act.astype(BF16), wdb[...], preferred_element_type=F32)

    for parity in range(2):
        pl.when((j < nt) & (j % 2 == parity))(functools.partial(live_step, parity))
        pl.when((j == nt) & (nt % 2 == parity))(functools.partial(gather_wait, parity))

    @pl.when(j >= nt)
    def _():
        ys_ref[...] = jnp.zeros(ys_ref.shape, F32)


def _experts(tile_e, tile_first, n_tiles, inv_tok, h2, wg, wu, wd, max_tiles):
    T, rs, rl = h2.shape
    D = rs * rl
    de = wg.shape[2]
    tm = EXPERT_TM
    wmap = lambda j, te, tf, nt, it: (te[j], 0, 0)
    return pl.pallas_call(
        _expert_kernel,
        out_shape=jax.ShapeDtypeStruct(((max_tiles + 1) * tm, D), F32),
        grid_spec=pltpu.PrefetchScalarGridSpec(
            num_scalar_prefetch=4,
            grid=(max_tiles + 1,),
            in_specs=[pl.BlockSpec(memory_space=pl.ANY),
                      pl.BlockSpec((1, D, de), wmap),
                      pl.BlockSpec((1, D, de), wmap),
                      pl.BlockSpec((1, de, D), wmap)],
            out_specs=pl.BlockSpec((tm, D), lambda j, te, tf, nt, it: (j, 0)),
            scratch_shapes=[pltpu.VMEM((D, de), BF16),
                            pltpu.VMEM((D, de), BF16),
                            pltpu.VMEM((de, D), BF16),
                            pltpu.VMEM((tm, rs, rl), F32),
                            pltpu.VMEM((tm, rs, rl), F32),
                            pltpu.SemaphoreType.DMA((2,))]),
        compiler_params=_params("arbitrary"),
        name="experts",
    )(tile_e, tile_first, n_tiles, inv_tok, h2, wg, wu, wd)


def _combine_kernel(code_ref, base_ref, x1_ref, rf_ref, mod_ref, gf_ref, ys_ref, out_ref, y0, y1, sem,
                    *, final_norm):
    tc = x1_ref.shape[0]
    i = pl.program_id(0)

    def issue(r, c):
        a = (i * tc + r) * TOP_K
        pltpu.make_async_copy(ys_ref.at[_code_row(base_ref, code_ref[a])], y0.at[r], sem).start()
        pltpu.make_async_copy(ys_ref.at[_code_row(base_ref, code_ref[a + 1])], y1.at[r], sem).start()
        return c

    lax.fori_loop(0, tc, issue, 0, unroll=8)
    pltpu.make_async_copy(ys_ref.at[pl.ds(0, tc)], y0, sem).wait()
    pltpu.make_async_copy(ys_ref.at[pl.ds(0, tc)], y1, sem).wait()

    gate2 = mod_ref[0, 5:6, :]
    w = rf_ref[...]
    moe = w[:, 0:1] * y0[...] + w[:, 1:2] * y1[...]
    x2 = x1_ref[...] + gate2 * moe
    if final_norm:
        ms = jnp.mean(x2 * x2, axis=-1, keepdims=True)
        x2 = x2 * lax.rsqrt(ms + RMS_EPS) * gf_ref[...]
    out_ref[...] = x2


def _combine(code, base, x1, rf, mod3, gf, ys, seq, final_norm):
    T, D = x1.shape
    tc = COMBINE_TM
    per_b = seq // tc
    return pl.pallas_call(
        functools.partial(_combine_kernel, final_norm=final_norm),
        out_shape=jax.ShapeDtypeStruct((T, D), F32),
        grid_spec=pltpu.PrefetchScalarGridSpec(
            num_scalar_prefetch=2,
            grid=(T // tc,),
            in_specs=[pl.BlockSpec((tc, D), lambda i, cd, bs: (i, 0)),
                      pl.BlockSpec((tc, LANES), lambda i, cd, bs: (i, 0)),
                      pl.BlockSpec((1, 6, D), lambda i, cd, bs: (i // per_b, 0, 0)),
                      pl.BlockSpec((1, D), lambda i, cd, bs: (0, 0)),
                      pl.BlockSpec(memory_space=pl.ANY)],
            out_specs=pl.BlockSpec((tc, D), lambda i, cd, bs: (i, 0)),
            scratch_shapes=[pltpu.VMEM((tc, D), F32),
                            pltpu.VMEM((tc, D), F32),
                            pltpu.SemaphoreType.DMA]),
        compiler_params=_params("arbitrary"),
        name="combine",
    )(code, base, x1, rf, mod3, gf, ys)


def _layer(x2, c, seq, w_ada, b_ada, g_norm1, w_in, b_if, conv_dw_w, conv_dw_b, conv_ln_g, conv_ln_b,
           w_conv_out, qk_conv_w, qk_conv_b, m_norm_g, w_m_out, w_out, g_norm2, w_rg, b_rg,
           w_re, b_re, w_e_gate, w_e_up, w_e_down):
    T, D = x2.shape
    B = T // seq
    dc = D // 2
    nif = 2 * M_HEADS

    mod3 = _ada(c, w_ada, b_ada).reshape(B, 6, D)

    if_lo = 6 * dc
    w_main = jnp.concatenate([w_in[:, :if_lo], w_in[:, if_lo + nif:]], axis=1).astype(BF16)
    w_if = w_in[:, if_lo:if_lo + nif]
    w_if_pad = jnp.pad(w_if, ((0, 0), (0, LANES - nif))).astype(BF16)
    w_ift = w_if.T.astype(BF16)
    u, qk, v, o, sga, sgb, ifc, ifr = _inproj(x2, mod3, g_norm1.reshape(1, D), w_main, w_if_pad, w_ift, seq)

    ya = _conv_branch(u.reshape(B, seq, dc), conv_dw_w, conv_dw_b.reshape(1, dc),
                      conv_ln_g.reshape(1, dc), conv_ln_b.reshape(1, dc), w_conv_out.astype(BF16))
    bifc = jnp.pad(b_if, (0, LANES - nif)).reshape(1, LANES)
    bifr = b_if.reshape(nif, 1)
    yb = _mlstm_branch(qk.reshape(B, seq, 2 * dc), v.reshape(B, seq, dc), o.reshape(B, seq, dc),
                       ifc, ifr, qk_conv_w, qk_conv_b.reshape(1, 2 * dc), bifc, bifr,
                       m_norm_g.reshape(1, dc), w_m_out.astype(BF16))

    n_r = N_GROUPS + N_EXPERTS
    w_r = jnp.pad(jnp.concatenate([w_rg, w_re], axis=1), ((0, 0), (0, LANES - n_r))).astype(BF16)
    b_r = jnp.pad(jnp.concatenate([b_rg, b_re]), (0, LANES - n_r)).reshape(1, LANES)
    x1, h2, ri, rf, cnt = _merge(x2, ya.reshape(T, D), yb.reshape(T, D), sga, sgb, mod3,
                                 g_norm2.reshape(1, D), w_out.astype(BF16), w_r, b_r, seq)

    tm = EXPERT_TM
    counts = cnt[0, :N_EXPERTS]
    tiles_e = (counts + tm - 1) // tm
    tile_end = jnp.cumsum(tiles_e)
    n_tiles = tile_end[-1]
    max_tiles = (T * TOP_K) // tm + N_EXPERTS
    jt = jnp.arange(max_tiles + 1, dtype=I32)
    jc = jnp.minimum(jt, n_tiles - 1)
    tile_e = jnp.sum(jc[:, None] >= tile_end[None, :], axis=1).astype(I32)
    tile_first = ((jt < n_tiles) & (jc == (tile_end - tiles_e)[tile_e])).astype(I32)

    code = ri[:, 0:TOP_K].reshape(T * TOP_K)
    inv_tok, base = _invert(code, counts, (max_tiles + 1) * tm)
    ys = _experts(tile_e, tile_first, n_tiles.reshape(1).astype(I32), inv_tok, h2,
                  w_e_gate, w_e_up, w_e_down, max_tiles)
    return code, base, x1, rf, mod3, ys


def kernel(x, c, w_ada, b_ada, g_norm1, w_in, b_if, conv_dw_w, conv_dw_b, conv_ln_g, conv_ln_b,
           w_conv_out, qk_conv_w, qk_conv_b, m_norm_g, w_m_out, w_out, g_norm2, w_rg, b_rg,
           w_re, b_re, w_e_gate, w_e_up, w_e_down, g_final):
    B, S, D = x.shape
    depth = w_ada.shape[0]
    x2 = x.reshape(B * S, D)
    for l in range(depth):
        code, base, x1, rf, mod3, ys = _layer(
            x2, c, S, w_ada[l], b_ada[l], g_norm1[l], w_in[l], b_if[l], conv_dw_w[l], conv_dw_b[l],
            conv_ln_g[l], conv_ln_b[l], w_conv_out[l], qk_conv_w[l], qk_conv_b[l], m_norm_g[l],
            w_m_out[l], w_out[l], g_norm2[l], w_rg[l], b_rg[l], w_re[l], b_re[l],
            w_e_gate[l], w_e_up[l], w_e_down[l])
        x2 = _combine(code, base, x1, rf, mod3, g_final.reshape(1, D), ys, S,
                      final_norm=l == depth - 1)
    return x2.reshape(B, S, D)
```

```python
import functools

import jax
import jax.numpy as jnp
from jax import lax
from jax.experimental import pallas as pl
from jax.experimental.pallas import tpu as pltpu

F32 = jnp.float32
BF16 = jnp.bfloat16
I32 = jnp.int32

M_HEADS = 4
CONV_WIDTH = 31
QK_CONV_WIDTH = 4
N_GROUPS = 4
E_PER_GROUP = 8
N_EXPERTS = N_GROUPS * E_PER_GROUP
TOP_K = 2
RMS_EPS = 1e-6
LN_EPS = 1e-5

LANES = 128
SUBLANES = 8
VMEM_LIMIT = 56 * 1024 * 1024

ADA_TN = 1024
INPROJ_TM = 256
CONV_TS = 256
CONV_HALO = 32
CONV_RC = 32
MLSTM_L = 128
MERGE_TM = 256
EXPERT_TM = 256
DISPATCH_TM = 256
COMBINE_TM = 256
RANK_BITS = 16
RANK_RADIX = 1 << RANK_BITS
assert EXPERT_TM & (EXPERT_TM - 1) == 0


def _sigmoid(v):
    return 1.0 / (1.0 + jnp.exp(-v))


def _log_sigmoid(v):
    return -(jnp.maximum(-v, 0.0) + jnp.log1p(jnp.exp(-jnp.abs(v))))


def _params(*sem):
    return pltpu.CompilerParams(dimension_semantics=sem, vmem_limit_bytes=VMEM_LIMIT)


def _ada_kernel(c_ref, w_ref, b_ref, o_ref):
    c = c_ref[...]
    s = c * _sigmoid(c)
    o_ref[...] = jnp.dot(s, w_ref[...], preferred_element_type=F32,
                         precision=lax.Precision.HIGHEST) + b_ref[...]


def _ada(c, w_ada, b_ada):
    B, D = c.shape
    N = w_ada.shape[1]
    return pl.pallas_call(
        _ada_kernel,
        out_shape=jax.ShapeDtypeStruct((B, N), F32),
        grid=(N // ADA_TN,),
        in_specs=[pl.BlockSpec((B, D), lambda j: (0, 0)),
                  pl.BlockSpec((D, ADA_TN), lambda j: (0, j)),
                  pl.BlockSpec((1, ADA_TN), lambda j: (0, j))],
        out_specs=pl.BlockSpec((B, ADA_TN), lambda j: (0, j)),
        compiler_params=_params("arbitrary"),
        name="ada",
    )(c, w_ada, b_ada.reshape(1, N))


def _inproj_kernel(x_ref, mod_ref, g_ref, wm_ref, wif_ref, wift_ref,
                   u_ref, qk_ref, v_ref, o_ref, sga_ref, sgb_ref, ifc_ref, ifr_ref):
    x = x_ref[...]
    shift = mod_ref[0, 0:1, :]
    scale = mod_ref[0, 1:2, :]
    ms = jnp.mean(x * x, axis=-1, keepdims=True)
    h = x * lax.rsqrt(ms + RMS_EPS) * g_ref[...]
    h = h * (1.0 + scale) + shift
    hb = h.astype(BF16)
    dc = u_ref.shape[1]
    d = sga_ref.shape[1]

    def seg(lo, hi):
        return jnp.dot(hb, wm_ref[:, lo:hi], preferred_element_type=F32)

    u_ref[...] = seg(0, dc) * _sigmoid(seg(dc, 2 * dc))
    qk_ref[...] = seg(2 * dc, 4 * dc)
    v_ref[...] = seg(4 * dc, 5 * dc)
    o_ref[...] = seg(5 * dc, 6 * dc)
    sga_ref[...] = _sigmoid(seg(6 * dc, 6 * dc + d))
    sgb_ref[...] = _sigmoid(seg(6 * dc + d, 6 * dc + 2 * d))
    ifc_ref[...] = jnp.dot(hb, wif_ref[...], preferred_element_type=F32)
    ifr_ref[...] = lax.dot_general(wift_ref[...], hb, (((1,), (1,)), ((), ())),
                                   preferred_element_type=F32)


def _inproj(x2, mod3, g1, w_main, w_if, w_ift, seq):
    T, D = x2.shape
    tm = INPROJ_TM
    dc = D // 2
    per_b = seq // tm
    row = lambda i: (i, 0)
    const = lambda i: (0, 0)
    return pl.pallas_call(
        _inproj_kernel,
        out_shape=[jax.ShapeDtypeStruct((T, dc), F32),
                   jax.ShapeDtypeStruct((T, 2 * dc), F32),
                   jax.ShapeDtypeStruct((T, dc), F32),
                   jax.ShapeDtypeStruct((T, dc), F32),
                   jax.ShapeDtypeStruct((T, D), F32),
                   jax.ShapeDtypeStruct((T, D), F32),
                   jax.ShapeDtypeStruct((T, LANES), F32),
                   jax.ShapeDtypeStruct((SUBLANES, T), F32)],
        grid=(T // tm,),
        in_specs=[pl.BlockSpec((tm, D), row),
                  pl.BlockSpec((1, 6, D), lambda i: (i // per_b, 0, 0)),
                  pl.BlockSpec((1, D), const),
                  pl.BlockSpec(w_main.shape, const),
                  pl.BlockSpec(w_if.shape, const),
                  pl.BlockSpec(w_ift.shape, const)],
        out_specs=[pl.BlockSpec((tm, dc), row),
                   pl.BlockSpec((tm, 2 * dc), row),
                   pl.BlockSpec((tm, dc), row),
                   pl.BlockSpec((tm, dc), row),
                   pl.BlockSpec((tm, D), row),
                   pl.BlockSpec((tm, D), row),
                   pl.BlockSpec((tm, LANES), row),
                   pl.BlockSpec((SUBLANES, tm), lambda i: (0, i))],
        compiler_params=_params("arbitrary"),
        name="inproj",
    )(x2, mod3, g1, w_main, w_if, w_ift)


def _conv_kernel(u_ref, w_ref, b_ref, lg_ref, lb_ref, wo_ref, y_ref, ubuf, sbuf, cbuf):
    ts = u_ref.shape[1]
    halo = CONV_HALO

    @pl.when(pl.program_id(1) == 0)
    def _():
        ubuf[0:halo, :] = jnp.zeros((halo, ubuf.shape[1]), F32)

    ubuf[halo:halo + ts, :] = u_ref[0]
    ns = sbuf.shape[1]
    for r in range(1, SUBLANES):
        sbuf[r - 1] = ubuf[r:r + ns, :]
    off = halo - (CONV_WIDTH - 1)
    for r0 in range(0, ts, CONV_RC):
        acc = jnp.broadcast_to(b_ref[...], (CONV_RC, ubuf.shape[1]))
        for k in range(CONV_WIDTH):
            r = (off + k) % SUBLANES
            lo = off + k - r + r0
            win = ubuf[lo:lo + CONV_RC, :] if r == 0 else sbuf[r - 1, lo:lo + CONV_RC, :]
            acc = acc + w_ref[k:k + 1, :] * win
        cbuf[r0:r0 + CONV_RC, :] = acc
    ubuf[0:halo, :] = ubuf[ts:ts + halo, :]

    a = cbuf[...]
    mu = jnp.mean(a, axis=-1, keepdims=True)
    ac = a - mu
    var = jnp.mean(ac * ac, axis=-1, keepdims=True)
    z = ac * lax.rsqrt(var + LN_EPS) * lg_ref[...] + lb_ref[...]
    z = z * _sigmoid(z)
    y_ref[0] = jnp.dot(z.astype(BF16), wo_ref[...], preferred_element_type=F32)


def _conv_branch(u3, w, b, lg, lb, wo):
    B, S, C = u3.shape
    D = wo.shape[1]
    ts = CONV_TS
    const = lambda bi, si: (0, 0)
    return pl.pallas_call(
        _conv_kernel,
        out_shape=jax.ShapeDtypeStruct((B, S, D), F32),
        grid=(B, S // ts),
        in_specs=[pl.BlockSpec((1, ts, C), lambda bi, si: (bi, si, 0)),
                  pl.BlockSpec(w.shape, const),
                  pl.BlockSpec((1, C), const),
                  pl.BlockSpec((1, C), const),
                  pl.BlockSpec((1, C), const),
                  pl.BlockSpec(wo.shape, const)],
        out_specs=pl.BlockSpec((1, ts, D), lambda bi, si: (bi, si, 0)),
        scratch_shapes=[pltpu.VMEM((ts + CONV_HALO, C), F32),
                        pltpu.VMEM((SUBLANES - 1, ts + CONV_HALO - SUBLANES, C), F32),
                        pltpu.VMEM((ts, C), F32)],
        compiler_params=_params("arbitrary", "arbitrary"),
        name="conv",
    )(u3, w, b, lg, lb, wo)


def _mlstm_kernel(qk_ref, v_ref, o_ref, ifc_ref, ifr_ref, cw_ref, cb_ref, bifc_ref, bifr_ref,
                  ng_ref, wo_ref, y_ref, qkbuf, cn_ref, m_ref, hbuf):
    L = qk_ref.shape[1]
    mi = v_ref.shape[2]
    dh = mi // M_HEADS
    halo = SUBLANES

    @pl.when(pl.program_id(1) == 0)
    def _():
        qkbuf[0:halo, :] = jnp.zeros((halo, qkbuf.shape[1]), F32)
        cn_ref[...] = jnp.zeros(cn_ref.shape, F32)
        m_ref[...] = jnp.zeros(m_ref.shape, F32)

    qkbuf[halo:halo + L, :] = qk_ref[0]
    off = halo - (QK_CONV_WIDTH - 1)
    y = jnp.broadcast_to(cb_ref[...], (L, qkbuf.shape[1]))
    for k in range(QK_CONV_WIDTH):
        y = y + cw_ref[k:k + 1, :] * qkbuf[off + k:off + k + L, :]
    y = y * _sigmoid(y)
    qkbuf[0:halo, :] = qkbuf[L:L + halo, :]

    ifr = ifr_ref[...] + bifr_ref[...]
    ifc = ifc_ref[...] + bifc_ref[...]
    lfr = _log_sigmoid(ifr)
    lfc = _log_sigmoid(ifc)
    rows = lax.broadcasted_iota(I32, (L, L), 0)
    cols = lax.broadcasted_iota(I32, (L, L), 1)
    causal = cols <= rows
    lower = causal.astype(F32)
    upper = (rows <= cols).astype(F32)
    bcum_c = jnp.dot(lower, lfc, preferred_element_type=F32, precision=lax.Precision.HIGHEST)
    bcum_r = jnp.dot(lfr, upper, preferred_element_type=F32, precision=lax.Precision.HIGHEST)

    lane = lax.broadcasted_iota(I32, (L, dh), 1)
    ones_col = jnp.where(lane == 0, 1.0, 0.0).astype(F32)
    vv = v_ref[0]
    oo = o_ref[0]
    scale = dh ** -0.5
    for hd in range(M_HEADS):
        q = y[:, hd * dh:(hd + 1) * dh] * scale
        kk = y[:, mi + hd * dh:mi + (hd + 1) * dh]
        v = vv[:, hd * dh:(hd + 1) * dh]
        kt = kk.T
        bc = bcum_c[:, M_HEADS + hd:M_HEADS + hd + 1]
        br = bcum_r[M_HEADS + hd:M_HEADS + hd + 1, :]
        li = ifr[hd:hd + 1, :]
        m_prev = m_ref[hd, 0:1, 0:1]
        dmat = jnp.where(causal, bc - br + li, -jnp.inf)
        inter = bc + m_prev
        m_t = jnp.maximum(jnp.max(dmat, axis=-1, keepdims=True), inter)
        wts = jnp.exp(dmat - m_t)
        s_inter = jnp.exp(inter - m_t)
        qb = q.astype(BF16)
        s_mat = jnp.dot(qb, kt.astype(BF16), preferred_element_type=F32) * wts
        cn = cn_ref[hd]
        qcn = jnp.dot(qb, cn.astype(BF16), preferred_element_type=F32)
        num = jnp.dot(s_mat.astype(BF16), v.astype(BF16), preferred_element_type=F32) \
            + s_inter * qcn[:, 0:dh]
        den = jnp.sum(s_mat, axis=-1, keepdims=True) + s_inter * qcn[:, dh:dh + 1]
        hh = num / jnp.maximum(jnp.abs(den), jnp.exp(-m_t))
        b_last = br[:, L - 1:L]
        a = b_last - br + li
        m_new = jnp.maximum(b_last + m_prev, jnp.max(a, axis=-1, keepdims=True))
        wk = jnp.exp(a - m_new)
        sc = jnp.exp(b_last + m_prev - m_new)
        v_ext = jnp.concatenate([v, ones_col], axis=1)
        cn_ref[hd] = sc * cn + jnp.dot((kt * wk).astype(BF16), v_ext.astype(BF16),
                                       preferred_element_type=F32)
        m_ref[hd] = jnp.broadcast_to(m_new, m_ref.shape[1:])
        mu = jnp.mean(hh, axis=-1, keepdims=True)
        hc = hh - mu
        var = jnp.mean(hc * hc, axis=-1, keepdims=True)
        hn = hc * lax.rsqrt(var + LN_EPS) * ng_ref[:, hd * dh:(hd + 1) * dh]
        hbuf[:, hd * dh:(hd + 1) * dh] = hn * _sigmoid(oo[:, hd * dh:(hd + 1) * dh])
    y_ref[0] = jnp.dot(hbuf[...].astype(BF16), wo_ref[...], preferred_element_type=F32)


def _mlstm_branch(qk3, v3, o3, ifc, ifr, cw, cb, bifc, bifr, ng, wo):
    B, S, C2 = qk3.shape
    mi = v3.shape[2]
    dh = mi // M_HEADS
    D = wo.shape[1]
    L = MLSTM_L
    nc = S // L
    const = lambda bi, ci: (0, 0)
    tile = lambda bi, ci: (bi, ci, 0)
    return pl.pallas_call(
        _mlstm_kernel,
        out_shape=jax.ShapeDtypeStruct((B, S, D), F32),
        grid=(B, nc),
        in_specs=[pl.BlockSpec((1, L, C2), tile),
                  pl.BlockSpec((1, L, mi), tile),
                  pl.BlockSpec((1, L, mi), tile),
                  pl.BlockSpec((L, LANES), lambda bi, ci: (bi * nc + ci, 0)),
                  pl.BlockSpec((SUBLANES, L), lambda bi, ci: (0, bi * nc + ci)),
                  pl.BlockSpec(cw.shape, const),
                  pl.BlockSpec((1, C2), const),
                  pl.BlockSpec((1, LANES), const),
                  pl.BlockSpec((SUBLANES, 1), const),
                  pl.BlockSpec((1, mi), const),
                  pl.BlockSpec(wo.shape, const)],
        out_specs=pl.BlockSpec((1, L, D), tile),
        scratch_shapes=[pltpu.VMEM((L + SUBLANES, C2), F32),
                        pltpu.VMEM((M_HEADS, dh, 2 * dh), F32),
                        pltpu.VMEM((M_HEADS, SUBLANES, LANES), F32),
                        pltpu.VMEM((L, mi), F32)],
        compiler_params=_params("arbitrary", "arbitrary"),
        name="mlstm",
    )(qk3, v3, o3, ifc, ifr, cw, cb, bifc, bifr, ng, wo)


def _merge_kernel(x_ref, ya_ref, yb_ref, sga_ref, sgb_ref, mod_ref, g2_ref, wo_ref, wr_ref, br_ref,
                  x1_ref, h2_ref, ri_ref, rf_ref, cnt_ref, run_ref):
    tm = x_ref.shape[0]

    @pl.when(pl.program_id(0) == 0)
    def _():
        run_ref[...] = jnp.zeros(run_ref.shape, F32)

    gate1 = mod_ref[0, 2:3, :]
    shift2 = mod_ref[0, 3:4, :]
    scale2 = mod_ref[0, 4:5, :]
    merged = sga_ref[...] * ya_ref[...] + sgb_ref[...] * yb_ref[...]
    mix = jnp.dot(merged.astype(BF16), wo_ref[...], preferred_element_type=F32)
    x1 = x_ref[...] + gate1 * mix
    x1_ref[...] = x1
    ms = jnp.mean(x1 * x1, axis=-1, keepdims=True)
    h2 = x1 * lax.rsqrt(ms + RMS_EPS) * g2_ref[...]
    h2 = h2 * (1.0 + scale2) + shift2
    h2_ref[...] = h2

    logits = jnp.dot(h2.astype(BF16), wr_ref[...], preferred_element_type=F32) + br_ref[...]
    lane = lax.broadcasted_iota(I32, (tm, LANES), 1).astype(F32)
    neg = -jnp.inf

    def first_argmax(vals):
        mx = jnp.max(vals, axis=-1, keepdims=True)
        idx = jnp.min(jnp.where(vals == mx, lane, float(LANES)), axis=-1, keepdims=True)
        return mx, idx

    lg = jnp.where(lane < N_GROUPS, logits, neg)
    gmax, gsel = first_argmax(lg)
    p_g = 1.0 / jnp.sum(jnp.exp(lg - gmax), axis=-1, keepdims=True)
    lo = N_GROUPS + gsel * E_PER_GROUP
    le = jnp.where((lane >= lo) & (lane < lo + E_PER_GROUP), logits, neg)
    l1, i1 = first_argmax(le)
    l2, i2 = first_argmax(jnp.where(lane == i1, neg, le))
    r = jnp.exp(l2 - l1)
    w1 = p_g / (1.0 + r)
    w2 = p_g * r / (1.0 + r)
    e1 = i1 - N_GROUPS
    e2 = i2 - N_GROUPS

    onehot = jnp.where((lane == e1) | (lane == e2), 1.0, 0.0)
    rows = lax.broadcasted_iota(I32, (tm, tm), 0)
    cols = lax.broadcasted_iota(I32, (tm, tm), 1)
    strict = jnp.where(cols < rows, 1.0, 0.0).astype(BF16)
    run = run_ref[0:1, :]
    before = jnp.dot(strict, onehot.astype(BF16), preferred_element_type=F32) + run
    rank1 = jnp.sum(jnp.where(lane == e1, before, 0.0), axis=-1, keepdims=True)
    rank2 = jnp.sum(jnp.where(lane == e2, before, 0.0), axis=-1, keepdims=True)
    run_new = run + jnp.sum(onehot, axis=0, keepdims=True)
    run_ref[...] = jnp.broadcast_to(run_new, run_ref.shape)
    cnt_ref[...] = jnp.broadcast_to(run_new, cnt_ref.shape).astype(I32)

    ri_ref[...] = jnp.where(lane == 0, e1 * float(RANK_RADIX) + rank1,
                            jnp.where(lane == 1, e2 * float(RANK_RADIX) + rank2, 0.0)).astype(I32)
    rf_ref[...] = jnp.where(lane == 0, w1, jnp.where(lane == 1, w2, 0.0))


def _merge(x2, ya, yb, sga, sgb, mod3, g2, wo, wr, br, seq):
    T, D = x2.shape
    tm = MERGE_TM
    per_b = seq // tm
    row = lambda i: (i, 0)
    const = lambda i: (0, 0)
    return pl.pallas_call(
        _merge_kernel,
        out_shape=[jax.ShapeDtypeStruct((T, D), F32),
                   jax.ShapeDtypeStruct((T, D), F32),
                   jax.ShapeDtypeStruct((T, LANES), I32),
                   jax.ShapeDtypeStruct((T, LANES), F32),
                   jax.ShapeDtypeStruct((SUBLANES, LANES), I32)],
        grid=(T // tm,),
        in_specs=[pl.BlockSpec((tm, D), row),
                  pl.BlockSpec((tm, D), row),
                  pl.BlockSpec((tm, D), row),
                  pl.BlockSpec((tm, D), row),
                  pl.BlockSpec((tm, D), row),
                  pl.BlockSpec((1, 6, D), lambda i: (i // per_b, 0, 0)),
                  pl.BlockSpec((1, D), const),
                  pl.BlockSpec(wo.shape, const),
                  pl.BlockSpec(wr.shape, const),
                  pl.BlockSpec((1, LANES), const)],
        out_specs=[pl.BlockSpec((tm, D), row),
                   pl.BlockSpec((tm, D), row),
                   pl.BlockSpec((tm, LANES), row),
                   pl.BlockSpec((tm, LANES), row),
                   pl.BlockSpec((SUBLANES, LANES), const)],
        scratch_shapes=[pltpu.VMEM((SUBLANES, LANES), F32)],
        compiler_params=_params("arbitrary"),
        name="merge",
    )(x2, ya, yb, sga, sgb, mod3, g2, wo, wr, br)


def _code_row(base_ref, code):
    return base_ref[lax.shift_right_logical(code, RANK_BITS)] + (code & (RANK_RADIX - 1))


def _dispatch_kernel(code_ref, cnt_ref, h_ref, xs_ref, dst_ref, base_ref, zbuf, sem, zsem):
    td = h_ref.shape[0]
    i = pl.program_id(0)
    n_assign = code_ref.shape[0]
    n_tok = n_assign // TOP_K
    tm = zbuf.shape[0]
    n_rows = xs_ref.shape[0]

    @pl.when(i == 0)
    def _():
        zbuf[...] = jnp.zeros(zbuf.shape, F32)

        def pad_dst(p, c):
            dst_ref[p] = n_assign + (p & (tm - 1))
            return c

        def expert(e, acc, wait):
            pos = acc + cnt_ref[e]
            end = acc + (cnt_ref[e] + tm - 1) // tm * tm
            n = end - pos
            if not wait:
                base_ref[e] = acc
                lax.fori_loop(pos, end, pad_dst, 0)
            head = jnp.minimum((-pos) & (SUBLANES - 1), n)

            def row(r, cc):
                cp = pltpu.make_async_copy(zbuf.at[0], xs_ref.at[pos + r], zsem)
                cp.wait() if wait else cp.start()
                return cc

            lax.fori_loop(0, head, row, 0)
            rest = n - head
            off = pos + head
            bit = tm // 2
            while bit >= SUBLANES:
                cp = pltpu.make_async_copy(zbuf.at[pl.ds(0, bit)],
                                           xs_ref.at[pl.ds(pl.multiple_of(off, SUBLANES), bit)], zsem)

                @pl.when((rest & bit) != 0)
                def _():
                    cp.wait() if wait else cp.start()

                off = off + (rest & bit)
                bit //= 2
            return end

        def idle_tile(t, c, wait):
            cp = pltpu.make_async_copy(zbuf, xs_ref.at[pl.ds(pl.multiple_of(t * tm, tm), tm)], zsem)
            cp.wait() if wait else cp.start()
            if not wait:
                lax.fori_loop(0, tm, lambda r, cc: pad_dst(t * tm + r, cc), 0, unroll=8)
            return c

        n_live = lax.fori_loop(0, N_EXPERTS, functools.partial(expert, wait=False), 0)
        lax.fori_loop(n_live // tm, n_rows // tm, functools.partial(idle_tile, wait=False), 0)
        lax.fori_loop(0, N_EXPERTS, functools.partial(expert, wait=True), 0)
        lax.fori_loop(n_live // tm, n_rows // tm, functools.partial(idle_tile, wait=True), 0)

    def issue(r, c):
        t = i * td + r
        for k in range(TOP_K):
            p = _code_row(base_ref, code_ref[t * TOP_K + k])
            pltpu.make_async_copy(h_ref.at[r], xs_ref.at[p], sem).start(priority=k % 2)
            dst_ref[p] = k * n_tok + t
        return c

    lax.fori_loop(0, td, issue, 0, unroll=8)
    for k in range(TOP_K):
        pltpu.make_async_copy(h_ref, xs_ref.at[pl.ds(0, td)], sem).wait()


def _dispatch(code, counts, h2, n_rows):
    T, D = h2.shape
    td = DISPATCH_TM
    return pl.pallas_call(
        _dispatch_kernel,
        out_shape=[jax.ShapeDtypeStruct((n_rows, D), F32), jax.ShapeDtypeStruct((n_rows,), I32)],
        grid_spec=pltpu.PrefetchScalarGridSpec(
            num_scalar_prefetch=2,
            grid=(T // td,),
            in_specs=[pl.BlockSpec((td, D), lambda i, cd, cn: (i, 0))],
            out_specs=[pl.BlockSpec(memory_space=pl.ANY), pl.BlockSpec(memory_space=pltpu.SMEM)],
            scratch_shapes=[pltpu.SMEM((N_EXPERTS,), I32),
                            pltpu.VMEM((EXPERT_TM, D), F32),
                            pltpu.SemaphoreType.DMA,
                            pltpu.SemaphoreType.DMA]),
        compiler_params=_params("arbitrary"),
        name="dispatch",
    )(code, counts, h2)


def _expert_kernel(te_ref, tb_ref, first_ref, nt_ref, dst_ref, xs_ref, wg_ref, wu_ref, wd_ref, yk_ref,
                   wgb, wub, wdb, obuf0, obuf1, ssem):
    j = pl.program_id(0)
    nt = nt_ref[0]
    obuf = (obuf0, obuf1)
    tm = obuf0.shape[0]
    n_assign = yk_ref.shape[0] - tm

    def scatter_start(tile, slot, live):
        for r in range(tm):
            dst = dst_ref[tile * tm + r]
            if live is not True:
                dst = jnp.where(live, dst, n_assign + r)
            pltpu.make_async_copy(obuf[slot].at[r], yk_ref.at[dst], ssem.at[slot]).start(priority=r % 2)

    def scatter_wait(slot):
        pltpu.make_async_copy(obuf[slot], yk_ref.at[pl.ds(0, tm)], ssem.at[slot]).wait()

    @pl.when(j == 0)
    def _():
        obuf1[...] = jnp.zeros(obuf1.shape, F32)

    @pl.when((j < nt) & (first_ref[j] == 1))
    def _():
        wgb[...] = wg_ref[0].astype(BF16)
        wub[...] = wu_ref[0].astype(BF16)
        wdb[...] = wd_ref[0].astype(BF16)

    def live_step(slot):
        if slot == 0:
            @pl.when(j > 0)
            def _():
                scatter_wait(slot)
        else:
            scatter_wait(slot)
        scatter_start(jnp.maximum(j - 1, 0), 1 - slot, (j > 0) if slot == 0 else True)
        xb = xs_ref[...].astype(BF16)
        g = jnp.dot(xb, wgb[...], preferred_element_type=F32)
        u = jnp.dot(xb, wub[...], preferred_element_type=F32)
        act = (g * _sigmoid(g)) * u
        obuf[slot][...] = jnp.dot(act.astype(BF16), wdb[...], preferred_element_type=F32)

    def drain(last):
        scatter_wait(1 - last)
        scatter_start(nt - 1, last, True)
        scatter_wait(last)

    for parity in range(2):
        pl.when((j < nt) & (j % 2 == parity))(functools.partial(live_step, parity))
        pl.when((j == nt) & ((nt - 1) % 2 == parity))(functools.partial(drain, parity))


def _experts(tile_e, tile_b, tile_first, n_tiles, dst, xs, wg, wu, wd, n_tok, max_tiles):
    P, D = xs.shape
    de = wg.shape[2]
    tm = EXPERT_TM
    wmap = lambda j, te, tb, tf, nt, ds: (te[j], 0, 0)
    return pl.pallas_call(
        _expert_kernel,
        out_shape=jax.ShapeDtypeStruct((n_tok * TOP_K + tm, D), F32),
        grid_spec=pltpu.PrefetchScalarGridSpec(
            num_scalar_prefetch=5,
            grid=(max_tiles + 1,),
            in_specs=[pl.BlockSpec((tm, D), lambda j, te, tb, tf, nt, ds: (tb[j], 0)),
                      pl.BlockSpec((1, D, de), wmap),
                      pl.BlockSpec((1, D, de), wmap),
                      pl.BlockSpec((1, de, D), wmap)],
            out_specs=pl.BlockSpec(memory_space=pl.ANY),
            scratch_shapes=[pltpu.VMEM((D, de), BF16),
                            pltpu.VMEM((D, de), BF16),
                            pltpu.VMEM((de, D), BF16),
                            pltpu.VMEM((tm, D), F32),
                            pltpu.VMEM((tm, D), F32),
                            pltpu.SemaphoreType.DMA((2,))]),
        compiler_params=_params("arbitrary"),
        name="experts",
    )(tile_e, tile_b, tile_first, n_tiles, dst, xs, wg, wu, wd)


def _combine_kernel(x1_ref, rf_ref, mod_ref, gf_ref, y0_ref, y1_ref, out_ref, *, final_norm):
    gate2 = mod_ref[0, 5:6, :]
    w = rf_ref[...]
    moe = w[:, 0:1] * y0_ref[...] + w[:, 1:2] * y1_ref[...]
    x2 = x1_ref[...] + gate2 * moe
    if final_norm:
        ms = jnp.mean(x2 * x2, axis=-1, keepdims=True)
        x2 = x2 * lax.rsqrt(ms + RMS_EPS) * gf_ref[...]
    out_ref[...] = x2


def _combine(x1, rf, mod3, gf, yk, seq, final_norm):
    T, D = x1.shape
    tc = COMBINE_TM
    per_b = seq // tc
    n_blk = T // tc
    return pl.pallas_call(
        functools.partial(_combine_kernel, final_norm=final_norm),
        out_shape=jax.ShapeDtypeStruct((T, D), F32),
        grid=(n_blk,),
        in_specs=[pl.BlockSpec((tc, D), lambda i: (i, 0)),
                  pl.BlockSpec((tc, LANES), lambda i: (i, 0)),
                  pl.BlockSpec((1, 6, D), lambda i: (i // per_b, 0, 0)),
                  pl.BlockSpec((1, D), lambda i: (0, 0)),
                  pl.BlockSpec((tc, D), lambda i: (i, 0)),
                  pl.BlockSpec((tc, D), lambda i: (n_blk + i, 0))],
        out_specs=pl.BlockSpec((tc, D), lambda i: (i, 0)),
        compiler_params=_params("arbitrary"),
        name="combine",
    )(x1, rf, mod3, gf, yk, yk)


def _layer(x2, c, seq, w_ada, b_ada, g_norm1, w_in, b_if, conv_dw_w, conv_dw_b, conv_ln_g, conv_ln_b,
           w_conv_out, qk_conv_w, qk_conv_b, m_norm_g, w_m_out, w_out, g_norm2, w_rg, b_rg,
           w_re, b_re, w_e_gate, w_e_up, w_e_down):
    T, D = x2.shape
    B = T // seq
    dc = D // 2
    nif = 2 * M_HEADS

    mod3 = _ada(c, w_ada, b_ada).reshape(B, 6, D)

    if_lo = 6 * dc
    w_main = jnp.concatenate([w_in[:, :if_lo], w_in[:, if_lo + nif:]], axis=1).astype(BF16)
    w_if = w_in[:, if_lo:if_lo + nif]
    w_if_pad = jnp.pad(w_if, ((0, 0), (0, LANES - nif))).astype(BF16)
    w_ift = w_if.T.astype(BF16)
    u, qk, v, o, sga, sgb, ifc, ifr = _inproj(x2, mod3, g_norm1.reshape(1, D), w_main, w_if_pad, w_ift, seq)

    ya = _conv_branch(u.reshape(B, seq, dc), conv_dw_w, conv_dw_b.reshape(1, dc),
                      conv_ln_g.reshape(1, dc), conv_ln_b.reshape(1, dc), w_conv_out.astype(BF16))
    bifc = jnp.pad(b_if, (0, LANES - nif)).reshape(1, LANES)
    bifr = b_if.reshape(nif, 1)
    yb = _mlstm_branch(qk.reshape(B, seq, 2 * dc), v.reshape(B, seq, dc), o.reshape(B, seq, dc),
                       ifc, ifr, qk_conv_w, qk_conv_b.reshape(1, 2 * dc), bifc, bifr,
                       m_norm_g.reshape(1, dc), w_m_out.astype(BF16))

    n_r = N_GROUPS + N_EXPERTS
    w_r = jnp.pad(jnp.concatenate([w_rg, w_re], axis=1), ((0, 0), (0, LANES - n_r))).astype(BF16)
    b_r = jnp.pad(jnp.concatenate([b_rg, b_re]), (0, LANES - n_r)).reshape(1, LANES)
    x1, h2, ri, rf, cnt = _merge(x2, ya.reshape(T, D), yb.reshape(T, D), sga, sgb, mod3,
                                 g_norm2.reshape(1, D), w_out.astype(BF16), w_r, b_r, seq)

    tm = EXPERT_TM
    counts = cnt[0, :N_EXPERTS]
    tiles_e = (counts + tm - 1) // tm
    tile_end = jnp.cumsum(tiles_e)
    n_tiles = tile_end[-1]
    max_tiles = (T * TOP_K) // tm + N_EXPERTS
    jt = jnp.arange(max_tiles + 1, dtype=I32)
    jc = jnp.minimum(jt, n_tiles - 1)
    tile_e = jnp.sum(jc[:, None] >= tile_end[None, :], axis=1).astype(I32)
    tile_first = ((jt < n_tiles) & (jc == (tile_end - tiles_e)[tile_e])).astype(I32)

    code = ri[:, 0:TOP_K].reshape(T * TOP_K)
    xs, dst = _dispatch(code, counts, h2, max_tiles * tm)
    yk = _experts(tile_e, jc, tile_first, n_tiles.reshape(1).astype(I32), dst, xs,
                  w_e_gate, w_e_up, w_e_down, T, max_tiles)
    return x1, rf, mod3, yk


def kernel(x, c, w_ada, b_ada, g_norm1, w_in, b_if, conv_dw_w, conv_dw_b, conv_ln_g, conv_ln_b,
           w_conv_out, qk_conv_w, qk_conv_b, m_norm_g, w_m_out, w_out, g_norm2, w_rg, b_rg,
           w_re, b_re, w_e_gate, w_e_up, w_e_down, g_final):
    B, S, D = x.shape
    depth = w_ada.shape[0]
    x2 = x.reshape(B * S, D)
    for l in range(depth):
        x1, rf, mod3, yk = _layer(
            x2, c, S, w_ada[l], b_ada[l], g_norm1[l], w_in[l], b_if[l], conv_dw_w[l], conv_dw_b[l],
            conv_ln_g[l], conv_ln_b[l], w_conv_out[l], qk_conv_w[l], qk_conv_b[l], m_norm_g[l],
            w_m_out[l], w_out[l], g_norm2[l], w_rg[l], b_rg[l], w_re[l], b_re[l],
            w_e_gate[l], w_e_up[l], w_e_down[l])
        x2 = _combine(x1, rf, mod3, g_final.reshape(1, D), yk, S, final_norm=l == depth - 1)
    return x2.reshape(B, S, D)
```

```python
import functools

import jax
import jax.numpy as jnp
from jax import lax
from jax.experimental import pallas as pl
from jax.experimental.pallas import tpu as pltpu

F32 = jnp.float32
BF16 = jnp.bfloat16
I32 = jnp.int32

M_HEADS = 4
CONV_WIDTH = 31
QK_CONV_WIDTH = 4
N_GROUPS = 4
E_PER_GROUP = 8
N_EXPERTS = N_GROUPS * E_PER_GROUP
TOP_K = 2
RMS_EPS = 1e-6
LN_EPS = 1e-5

LANES = 128
SUBLANES = 8
VMEM_LIMIT = 56 * 1024 * 1024

ADA_TN = 1024
INPROJ_TM = 256
CONV_TS = 256
CONV_HALO = 32
CONV_RC = 32
MLSTM_L = 128
MLSTM_STEP_CHUNKS = 2
MERGE_TM = 256
EXPERT_TM = 256
DISPATCH_TM = 256
COMBINE_TM = 256
RANK_BITS = 16
RANK_RADIX = 1 << RANK_BITS
assert EXPERT_TM & (EXPERT_TM - 1) == 0


def _sigmoid(v):
    return 1.0 / (1.0 + jnp.exp(-v))


def _log_sigmoid(v):
    return -(jnp.maximum(-v, 0.0) + jnp.log1p(jnp.exp(-jnp.abs(v))))


def _split_bf16(v):
    hi = v.astype(BF16)
    r1 = v - hi.astype(F32)
    mid = r1.astype(BF16)
    lo = (r1 - mid.astype(F32)).astype(BF16)
    return hi, mid, lo


def _params(*sem):
    return pltpu.CompilerParams(dimension_semantics=sem, vmem_limit_bytes=VMEM_LIMIT)


def _ada_kernel(c_ref, w_ref, b_ref, o_ref):
    c = c_ref[...]
    s = c * _sigmoid(c)
    o_ref[...] = jnp.dot(s, w_ref[...], preferred_element_type=F32,
                         precision=lax.Precision.HIGHEST) + b_ref[...]


def _ada(c, w_ada, b_ada):
    B, D = c.shape
    N = w_ada.shape[1]
    return pl.pallas_call(
        _ada_kernel,
        out_shape=jax.ShapeDtypeStruct((B, N), F32),
        grid=(N // ADA_TN,),
        in_specs=[pl.BlockSpec((B, D), lambda j: (0, 0)),
                  pl.BlockSpec((D, ADA_TN), lambda j: (0, j)),
                  pl.BlockSpec((1, ADA_TN), lambda j: (0, j))],
        out_specs=pl.BlockSpec((B, ADA_TN), lambda j: (0, j)),
        compiler_params=_params("arbitrary"),
        name="ada",
    )(c, w_ada, b_ada.reshape(1, N))


def _modulated_rmsnorm(x, g, shift, scale):
    ms = jnp.mean(x * x, axis=-1, keepdims=True)
    return (x * lax.rsqrt(ms + RMS_EPS) * g) * (1.0 + scale) + shift


def _inproj_kernel(x_ref, mod_ref, g_ref, wm_ref, wif_ref, wift_ref, cw_ref, cb_ref,
                   u_ref, q_ref, k_ref, v_ref, so_ref, ifc_ref, ifr_ref, qkbuf, *, per_b):
    tm = x_ref.shape[0]
    hb = _modulated_rmsnorm(x_ref[...], g_ref[...], mod_ref[0, 0:1, :], mod_ref[0, 1:2, :]).astype(BF16)
    dc = u_ref.shape[1]

    def seg(lo, hi):
        return jnp.dot(hb, wm_ref[:, lo:hi], preferred_element_type=F32)

    u_ref[...] = seg(0, dc) * _sigmoid(seg(dc, 2 * dc))

    halo = SUBLANES

    @pl.when(pl.program_id(0) % per_b == 0)
    def _():
        qkbuf[0:halo, :] = jnp.zeros((halo, qkbuf.shape[1]), F32)

    qkbuf[halo:halo + tm, :] = seg(2 * dc, 4 * dc)
    off = halo - (QK_CONV_WIDTH - 1)
    y = jnp.broadcast_to(cb_ref[...], (tm, qkbuf.shape[1]))
    for k in range(QK_CONV_WIDTH):
        y = y + cw_ref[k:k + 1, :] * qkbuf[off + k:off + k + tm, :]
    y = y * _sigmoid(y)
    qkbuf[0:halo, :] = qkbuf[tm:tm + halo, :]
    q_ref[...] = (y[:, 0:dc] * (dc // M_HEADS) ** -0.5).astype(BF16)
    k_ref[...] = y[:, dc:2 * dc]
    v_ref[...] = seg(4 * dc, 5 * dc).astype(BF16)
    so_ref[...] = _sigmoid(seg(5 * dc, 6 * dc))
    ifc_ref[...] = jnp.dot(hb, wif_ref[...], preferred_element_type=F32)
    ifr_ref[...] = lax.dot_general(wift_ref[...], hb, (((1,), (1,)), ((), ())),
                                   preferred_element_type=F32)


def _inproj(x2, mod3, g1, w_main, w_if, w_ift, cw, cb, seq):
    T, D = x2.shape
    tm = INPROJ_TM
    dc = D // 2
    per_b = seq // tm
    row = lambda i: (i, 0)
    const = lambda i: (0, 0)
    return pl.pallas_call(
        functools.partial(_inproj_kernel, per_b=per_b),
        out_shape=[jax.ShapeDtypeStruct((T, dc), F32),
                   jax.ShapeDtypeStruct((T, dc), BF16),
                   jax.ShapeDtypeStruct((T, dc), F32),
                   jax.ShapeDtypeStruct((T, dc), BF16),
                   jax.ShapeDtypeStruct((T, dc), F32),
                   jax.ShapeDtypeStruct((T, LANES), F32),
                   jax.ShapeDtypeStruct((SUBLANES, T), F32)],
        grid=(T // tm,),
        in_specs=[pl.BlockSpec((tm, D), row),
                  pl.BlockSpec((1, 6, D), lambda i: (i // per_b, 0, 0)),
                  pl.BlockSpec((1, D), const),
                  pl.BlockSpec(w_main.shape, const),
                  pl.BlockSpec(w_if.shape, const),
                  pl.BlockSpec(w_ift.shape, const),
                  pl.BlockSpec(cw.shape, const),
                  pl.BlockSpec((1, 2 * dc), const)],
        out_specs=[pl.BlockSpec((tm, dc), row),
                   pl.BlockSpec((tm, dc), row),
                   pl.BlockSpec((tm, dc), row),
                   pl.BlockSpec((tm, dc), row),
                   pl.BlockSpec((tm, dc), row),
                   pl.BlockSpec((tm, LANES), row),
                   pl.BlockSpec((SUBLANES, tm), lambda i: (0, i))],
        scratch_shapes=[pltpu.VMEM((tm + SUBLANES, 2 * dc), F32)],
        compiler_params=_params("arbitrary"),
        name="inproj",
    )(x2, mod3, g1, w_main, w_if, w_ift, cw, cb)


def _conv_kernel(u_ref, w_ref, b_ref, lg_ref, lb_ref, z_ref, ubuf, sbuf, cbuf):
    ts = u_ref.shape[1]
    halo = CONV_HALO

    @pl.when(pl.program_id(1) == 0)
    def _():
        ubuf[0:halo, :] = jnp.zeros((halo, ubuf.shape[1]), F32)

    ubuf[halo:halo + ts, :] = u_ref[0]
    ns = sbuf.shape[1]
    for r in range(1, SUBLANES):
        sbuf[r - 1] = ubuf[r:r + ns, :]
    off = halo - (CONV_WIDTH - 1)
    for r0 in range(0, ts, CONV_RC):
        acc = jnp.broadcast_to(b_ref[...], (CONV_RC, ubuf.shape[1]))
        for k in range(CONV_WIDTH):
            r = (off + k) % SUBLANES
            lo = off + k - r + r0
            win = ubuf[lo:lo + CONV_RC, :] if r == 0 else sbuf[r - 1, lo:lo + CONV_RC, :]
            acc = acc + w_ref[k:k + 1, :] * win
        cbuf[r0:r0 + CONV_RC, :] = acc
    ubuf[0:halo, :] = ubuf[ts:ts + halo, :]

    a = cbuf[...]
    mu = jnp.mean(a, axis=-1, keepdims=True)
    ac = a - mu
    var = jnp.mean(ac * ac, axis=-1, keepdims=True)
    z = ac * lax.rsqrt(var + LN_EPS) * lg_ref[...] + lb_ref[...]
    z = z * _sigmoid(z)
    z_ref[0] = z.astype(BF16)


def _conv_branch(u3, w, b, lg, lb):
    B, S, C = u3.shape
    ts = CONV_TS
    const = lambda bi, si: (0, 0)
    return pl.pallas_call(
        _conv_kernel,
        out_shape=jax.ShapeDtypeStruct((B, S, C), BF16),
        grid=(B, S // ts),
        in_specs=[pl.BlockSpec((1, ts, C), lambda bi, si: (bi, si, 0)),
                  pl.BlockSpec(w.shape, const),
                  pl.BlockSpec((1, C), const),
                  pl.BlockSpec((1, C), const),
                  pl.BlockSpec((1, C), const)],
        out_specs=pl.BlockSpec((1, ts, C), lambda bi, si: (bi, si, 0)),
        scratch_shapes=[pltpu.VMEM((ts + CONV_HALO, C), F32),
                        pltpu.VMEM((SUBLANES - 1, ts + CONV_HALO - SUBLANES, C), F32),
                        pltpu.VMEM((ts, C), F32)],
        compiler_params=_params("arbitrary", "arbitrary"),
        name="conv",
    )(u3, w, b, lg, lb)


def _mlstm_kernel(q_ref, k_ref, v_ref, so_ref, ifc_ref, ifr_ref, bifc_ref, bifr_ref, ng_ref, hb_ref,
                  cn_ref, m_ref):
    L = MLSTM_L
    n_chunks = q_ref.shape[1] // L
    mi = v_ref.shape[2]
    dh = mi // M_HEADS

    @pl.when(pl.program_id(1) == 0)
    def _():
        cn_ref[...] = jnp.zeros(cn_ref.shape, F32)
        m_ref[...] = jnp.zeros(m_ref.shape, F32)

    ifr_all = ifr_ref[...] + bifr_ref[...]
    lfr_all = _log_sigmoid(ifr_all)
    lfc_all = _log_sigmoid(ifc_ref[...] + bifc_ref[...])
    rows = lax.broadcasted_iota(I32, (L, L), 0)
    cols = lax.broadcasted_iota(I32, (L, L), 1)
    causal = cols <= rows
    lower = jnp.where(causal, 1.0, 0.0).astype(BF16)
    upper = jnp.where(rows <= cols, 1.0, 0.0).astype(BF16)
    lane = lax.broadcasted_iota(I32, (L, dh), 1)
    ones_col = jnp.where(lane == 0, 1.0, 0.0).astype(BF16)
    lfc_parts = _split_bf16(lfc_all)
    lfr_parts = _split_bf16(lfr_all)

    for ch in range(n_chunks):
        r0 = ch * L
        ifr = ifr_all[:, r0:r0 + L]
        bcum_c = sum(jnp.dot(lower, p[r0:r0 + L, :], preferred_element_type=F32) for p in lfc_parts)
        bcum_r = sum(jnp.dot(p[:, r0:r0 + L], upper, preferred_element_type=F32) for p in lfr_parts)
        for hd in range(M_HEADS):
            c0 = hd * dh
            qb = q_ref[0, r0:r0 + L, c0:c0 + dh]
            vb = v_ref[0, r0:r0 + L, c0:c0 + dh]
            kt = k_ref[0, r0:r0 + L, c0:c0 + dh].T
            bc = bcum_c[:, M_HEADS + hd:M_HEADS + hd + 1]
            br = bcum_r[M_HEADS + hd:M_HEADS + hd + 1, :]
            li = ifr[hd:hd + 1, :]
            m_prev = m_ref[hd, 0:1, 0:1]
            dmat = jnp.where(causal, bc - br + li, -jnp.inf)
            inter = bc + m_prev
            m_t = jnp.maximum(jnp.max(dmat, axis=-1, keepdims=True), inter)
            wts = jnp.exp(dmat - m_t)
            s_inter = jnp.exp(inter - m_t)
            s_mat = jnp.dot(qb, kt.astype(BF16), preferred_element_type=F32) * wts
            cn = cn_ref[hd]
            qcn = jnp.dot(qb, cn.astype(BF16), preferred_element_type=F32)
            num = jnp.dot(s_mat.astype(BF16), vb, preferred_element_type=F32) + s_inter * qcn[:, 0:dh]
            den = jnp.sum(s_mat, axis=-1, keepdims=True) + s_inter * qcn[:, dh:dh + 1]
            hh = num / jnp.maximum(jnp.abs(den), jnp.exp(-m_t))
            b_last = br[:, L - 1:L]
            a = b_last - br + li
            m_new = jnp.maximum(b_last + m_prev, jnp.max(a, axis=-1, keepdims=True))
            wk = jnp.exp(a - m_new)
            sc = jnp.exp(b_last + m_prev - m_new)
            v_ext = jnp.concatenate([vb, ones_col], axis=1)
            cn_ref[hd] = sc * cn + jnp.dot((kt * wk).astype(BF16), v_ext, preferred_element_type=F32)
            m_ref[hd] = jnp.broadcast_to(m_new, m_ref.shape[1:])
            mu = jnp.mean(hh, axis=-1, keepdims=True)
            hc = hh - mu
            var = jnp.mean(hc * hc, axis=-1, keepdims=True)
            hn = hc * lax.rsqrt(var + LN_EPS) * ng_ref[:, c0:c0 + dh]
            hb_ref[0, r0:r0 + L, c0:c0 + dh] = (hn * so_ref[0, r0:r0 + L, c0:c0 + dh]).astype(BF16)


def _mlstm_branch(q3, k3, v3, so3, ifc, ifr, bifc, bifr, ng):
    B, S, mi = q3.shape
    dh = mi // M_HEADS
    rows = MLSTM_L * MLSTM_STEP_CHUNKS
    ns = S // rows
    const = lambda bi, ci: (0, 0)
    tile = lambda bi, ci: (bi, ci, 0)
    return pl.pallas_call(
        _mlstm_kernel,
        out_shape=jax.ShapeDtypeStruct((B, S, mi), BF16),
        grid=(B, ns),
        in_specs=[pl.BlockSpec((1, rows, mi), tile),
                  pl.BlockSpec((1, rows, mi), tile),
                  pl.BlockSpec((1, rows, mi), tile),
                  pl.BlockSpec((1, rows, mi), tile),
                  pl.BlockSpec((rows, LANES), lambda bi, ci: (bi * ns + ci, 0)),
                  pl.BlockSpec((SUBLANES, rows), lambda bi, ci: (0, bi * ns + ci)),
                  pl.BlockSpec((1, LANES), const),
                  pl.BlockSpec((SUBLANES, 1), const),
                  pl.BlockSpec((1, mi), const)],
        out_specs=pl.BlockSpec((1, rows, mi), tile),
        scratch_shapes=[pltpu.VMEM((M_HEADS, dh, 2 * dh), F32),
                        pltpu.VMEM((M_HEADS, SUBLANES, LANES), F32)],
        compiler_params=_params("arbitrary", "arbitrary"),
        name="mlstm",
    )(q3, k3, v3, so3, ifc, ifr, bifc, bifr, ng)


def _merge_kernel(x_ref, za_ref, hb_ref, mod_ref, g1_ref, g2_ref, wg_ref, wa_ref, wb_ref, wo_ref,
                  wr_ref, br_ref, x1_ref, h2_ref, ri_ref, rf_ref, cnt_ref, run_ref):
    tm, d = x_ref.shape

    @pl.when(pl.program_id(0) == 0)
    def _():
        run_ref[...] = jnp.zeros(run_ref.shape, F32)

    x = x_ref[...]
    h1 = _modulated_rmsnorm(x, g1_ref[...], mod_ref[0, 0:1, :], mod_ref[0, 1:2, :]).astype(BF16)
    gate_a = jnp.dot(h1, wg_ref[:, 0:d], preferred_element_type=F32)
    gate_b = jnp.dot(h1, wg_ref[:, d:2 * d], preferred_element_type=F32)
    ya = jnp.dot(za_ref[...], wa_ref[...], preferred_element_type=F32)
    yb = jnp.dot(hb_ref[...], wb_ref[...], preferred_element_type=F32)
    merged = _sigmoid(gate_a) * ya + _sigmoid(gate_b) * yb
    mix = jnp.dot(merged.astype(BF16), wo_ref[...], preferred_element_type=F32)
    x1 = x + mod_ref[0, 2:3, :] * mix
    x1_ref[...] = x1
    h2 = _modulated_rmsnorm(x1, g2_ref[...], mod_ref[0, 3:4, :], mod_ref[0, 4:5, :])
    h2_ref[...] = h2

    logits = jnp.dot(h2.astype(BF16), wr_ref[...], preferred_element_type=F32) + br_ref[...]
    lane = lax.broadcasted_iota(I32, (tm, LANES), 1).astype(F32)
    neg = -jnp.inf

    def first_argmax(vals):
        mx = jnp.max(vals, axis=-1, keepdims=True)
        idx = jnp.min(jnp.where(vals == mx, lane, float(LANES)), axis=-1, keepdims=True)
        return mx, idx

    lg = jnp.where(lane < N_GROUPS, logits, neg)
    gmax, gsel = first_argmax(lg)
    p_g = 1.0 / jnp.sum(jnp.exp(lg - gmax), axis=-1, keepdims=True)
    lo = N_GROUPS + gsel * E_PER_GROUP
    le = jnp.where((lane >= lo) & (lane < lo + E_PER_GROUP), logits, neg)
    l1, i1 = first_argmax(le)
    l2, i2 = first_argmax(jnp.where(lane == i1, neg, le))
    r = jnp.exp(l2 - l1)
    w1 = p_g / (1.0 + r)
    w2 = p_g * r / (1.0 + r)
    e1 = i1 - N_GROUPS
    e2 = i2 - N_GROUPS

    onehot = jnp.where((lane == e1) | (lane == e2), 1.0, 0.0)
    rows = lax.broadcasted_iota(I32, (tm, tm), 0)
    cols = lax.broadcasted_iota(I32, (tm, tm), 1)
    strict = jnp.where(cols < rows, 1.0, 0.0).astype(BF16)
    run = run_ref[0:1, :]
    before = jnp.dot(strict, onehot.astype(BF16), preferred_element_type=F32) + run
    rank1 = jnp.sum(jnp.where(lane == e1, before, 0.0), axis=-1, keepdims=True)
    rank2 = jnp.sum(jnp.where(lane == e2, before, 0.0), axis=-1, keepdims=True)
    run_new = run + jnp.sum(onehot, axis=0, keepdims=True)
    run_ref[...] = jnp.broadcast_to(run_new, run_ref.shape)
    cnt_ref[...] = jnp.broadcast_to(run_new, cnt_ref.shape).astype(I32)

    ri_ref[...] = jnp.where(lane == 0, e1 * float(RANK_RADIX) + rank1,
                            jnp.where(lane == 1, e2 * float(RANK_RADIX) + rank2, 0.0)).astype(I32)
    rf_ref[...] = jnp.where(lane == 0, w1, jnp.where(lane == 1, w2, 0.0))


def _merge(x2, za, hb, mod3, g1, g2, wg, wa, wb, wo, wr, br, seq):
    T, D = x2.shape
    dc = za.shape[1]
    tm = MERGE_TM
    per_b = seq // tm
    row = lambda i: (i, 0)
    const = lambda i: (0, 0)
    return pl.pallas_call(
        _merge_kernel,
        out_shape=[jax.ShapeDtypeStruct((T, D), F32),
                   jax.ShapeDtypeStruct((T, D), F32),
                   jax.ShapeDtypeStruct((T, LANES), I32),
                   jax.ShapeDtypeStruct((T, LANES), F32),
                   jax.ShapeDtypeStruct((SUBLANES, LANES), I32)],
        grid=(T // tm,),
        in_specs=[pl.BlockSpec((tm, D), row),
                  pl.BlockSpec((tm, dc), row),
                  pl.BlockSpec((tm, dc), row),
                  pl.BlockSpec((1, 6, D), lambda i: (i // per_b, 0, 0)),
                  pl.BlockSpec((1, D), const),
                  pl.BlockSpec((1, D), const),
                  pl.BlockSpec(wg.shape, const),
                  pl.BlockSpec(wa.shape, const),
                  pl.BlockSpec(wb.shape, const),
                  pl.BlockSpec(wo.shape, const),
                  pl.BlockSpec(wr.shape, const),
                  pl.BlockSpec((1, LANES), const)],
        out_specs=[pl.BlockSpec((tm, D), row),
                   pl.BlockSpec((tm, D), row),
                   pl.BlockSpec((tm, LANES), row),
                   pl.BlockSpec((tm, LANES), row),
                   pl.BlockSpec((SUBLANES, LANES), const)],
        scratch_shapes=[pltpu.VMEM((SUBLANES, LANES), F32)],
        compiler_params=_params("arbitrary"),
        name="merge",
    )(x2, za, hb, mod3, g1, g2, wg, wa, wb, wo, wr, br)


def _code_row(base_ref, code):
    return base_ref[lax.shift_right_logical(code, RANK_BITS)] + (code & (RANK_RADIX - 1))


def _dispatch_kernel(code_ref, cnt_ref, h_ref, xs_ref, dst_ref, base_ref, zbuf, sem, zsem):
    td = h_ref.shape[0]
    i = pl.program_id(0)
    n_assign = code_ref.shape[0]
    n_tok = n_assign // TOP_K
    tm = zbuf.shape[0]
    n_rows = xs_ref.shape[0]

    @pl.when(i == 0)
    def _():
        zbuf[...] = jnp.zeros(zbuf.shape, F32)

        def pad_dst(p, c):
            dst_ref[p] = n_assign + (p & (tm - 1))
            return c

        def expert(e, acc, wait):
            pos = acc + cnt_ref[e]
            end = acc + (cnt_ref[e] + tm - 1) // tm * tm
            n = end - pos
            if not wait:
                base_ref[e] = acc
                lax.fori_loop(pos, end, pad_dst, 0)
            head = jnp.minimum((-pos) & (SUBLANES - 1), n)

            def row(r, cc):
                cp = pltpu.make_async_copy(zbuf.at[0], xs_ref.at[pos + r], zsem)
                cp.wait() if wait else cp.start()
                return cc

            lax.fori_loop(0, head, row, 0)
            rest = n - head
            off = pos + head
            bit = tm // 2
            while bit >= SUBLANES:
                cp = pltpu.make_async_copy(zbuf.at[pl.ds(0, bit)],
                                           xs_ref.at[pl.ds(pl.multiple_of(off, SUBLANES), bit)], zsem)

                @pl.when((rest & bit) != 0)
                def _():
                    cp.wait() if wait else cp.start()

                off = off + (rest & bit)
                bit //= 2
            return end

        def idle_tile(t, c, wait):
            cp = pltpu.make_async_copy(zbuf, xs_ref.at[pl.ds(pl.multiple_of(t * tm, tm), tm)], zsem)
            cp.wait() if wait else cp.start()
            if not wait:
                lax.fori_loop(0, tm, lambda r, cc: pad_dst(t * tm + r, cc), 0, unroll=8)
            return c

        n_live = lax.fori_loop(0, N_EXPERTS, functools.partial(expert, wait=False), 0)
        lax.fori_loop(n_live // tm, n_rows // tm, functools.partial(idle_tile, wait=False), 0)
        lax.fori_loop(0, N_EXPERTS, functools.partial(expert, wait=True), 0)
        lax.fori_loop(n_live // tm, n_rows // tm, functools.partial(idle_tile, wait=True), 0)

    def issue(r, c):
        t = i * td + r
        for k in range(TOP_K):
            p = _code_row(base_ref, code_ref[t * TOP_K + k])
            pltpu.make_async_copy(h_ref.at[r], xs_ref.at[p], sem).start(priority=k % 2)
            dst_ref[p] = k * n_tok + t
        return c

    lax.fori_loop(0, td, issue, 0, unroll=8)
    for k in range(TOP_K):
        pltpu.make_async_copy(h_ref, xs_ref.at[pl.ds(0, td)], sem).wait()


def _dispatch(code, counts, h2, n_rows):
    T, D = h2.shape
    td = DISPATCH_TM
    return pl.pallas_call(
        _dispatch_kernel,
        out_shape=[jax.ShapeDtypeStruct((n_rows, D), F32), jax.ShapeDtypeStruct((n_rows,), I32)],
        grid_spec=pltpu.PrefetchScalarGridSpec(
            num_scalar_prefetch=2,
            grid=(T // td,),
            in_specs=[pl.BlockSpec((td, D), lambda i, cd, cn: (i, 0))],
            out_specs=[pl.BlockSpec(memory_space=pl.ANY), pl.BlockSpec(memory_space=pltpu.SMEM)],
            scratch_shapes=[pltpu.SMEM((N_EXPERTS,), I32),
                            pltpu.VMEM((EXPERT_TM, D), F32),
                            pltpu.SemaphoreType.DMA,
                            pltpu.SemaphoreType.DMA]),
        compiler_params=_params("arbitrary"),
        name="dispatch",
    )(code, counts, h2)


def _expert_kernel(te_ref, tb_ref, first_ref, nt_ref, dst_ref, xs_ref, wg_ref, wu_ref, wd_ref, yk_ref,
                   wgb, wub, wdb, obuf0, obuf1, ssem):
    j = pl.program_id(0)
    nt = nt_ref[0]
    obuf = (obuf0, obuf1)
    tm = obuf0.shape[0]
    n_assign = yk_ref.shape[0] - tm

    def scatter_start(tile, slot, live):
        for r in range(tm):
            dst = dst_ref[tile * tm + r]
            if live is not True:
                dst = jnp.where(live, dst, n_assign + r)
            pltpu.make_async_copy(obuf[slot].at[r], yk_ref.at[dst], ssem.at[slot]).start(priority=r % 2)

    def scatter_wait(slot):
        pltpu.make_async_copy(obuf[slot], yk_ref.at[pl.ds(0, tm)], ssem.at[slot]).wait()

    @pl.when(j == 0)
    def _():
        obuf1[...] = jnp.zeros(obuf1.shape, F32)

    @pl.when((j < nt) & (first_ref[j] == 1))
    def _():
        wgb[...] = wg_ref[0].astype(BF16)
        wub[...] = wu_ref[0].astype(BF16)
        wdb[...] = wd_ref[0].astype(BF16)

    def live_step(slot):
        if slot == 0:
            @pl.when(j > 0)
            def _():
                scatter_wait(slot)
        else:
            scatter_wait(slot)
        scatter_start(jnp.maximum(j - 1, 0), 1 - slot, (j > 0) if slot == 0 else True)
        xb = xs_ref[...].astype(BF16)
        g = jnp.dot(xb, wgb[...], preferred_element_type=F32)
        u = jnp.dot(xb, wub[...], preferred_element_type=F32)
        act = (g * _sigmoid(g)) * u
        obuf[slot][...] = jnp.dot(act.astype(BF16), wdb[...], preferred_element_type=F32)

    def drain(last):
        scatter_wait(1 - last)
        scatter_start(nt - 1, last, True)
        scatter_wait(last)

    for parity in range(2):
        pl.when((j < nt) & (j % 2 == parity))(functools.partial(live_step, parity))
        pl.when((j == nt) & ((nt - 1) % 2 == parity))(functools.partial(drain, parity))


def _experts(tile_e, tile_b, tile_first, n_tiles, dst, xs, wg, wu, wd, n_tok, max_tiles):
    P, D = xs.shape
    de = wg.shape[2]
    tm = EXPERT_TM
    wmap = lambda j, te, tb, tf, nt, ds: (te[j], 0, 0)
    return pl.pallas_call(
        _expert_kernel,
        out_shape=jax.ShapeDtypeStruct((n_tok * TOP_K + tm, D), F32),
        grid_spec=pltpu.PrefetchScalarGridSpec(
            num_scalar_prefetch=5,
            grid=(max_tiles + 1,),
            in_specs=[pl.BlockSpec((tm, D), lambda j, te, tb, tf, nt, ds: (tb[j], 0)),
                      pl.BlockSpec((1, D, de), wmap),
                      pl.BlockSpec((1, D, de), wmap),
                      pl.BlockSpec((1, de, D), wmap)],
            out_specs=pl.BlockSpec(memory_space=pl.ANY),
            scratch_shapes=[pltpu.VMEM((D, de), BF16),
                            pltpu.VMEM((D, de), BF16),
                            pltpu.VMEM((de, D), BF16),
                            pltpu.VMEM((tm, D), F32),
                            pltpu.VMEM((tm, D), F32),
                            pltpu.SemaphoreType.DMA((2,))]),
        compiler_params=_params("arbitrary"),
        name="experts",
    )(tile_e, tile_b, tile_first, n_tiles, dst, xs, wg, wu, wd)


def _combine_kernel(x1_ref, rf_ref, mod_ref, gf_ref, y0_ref, y1_ref, out_ref, *, final_norm):
    gate2 = mod_ref[0, 5:6, :]
    w = rf_ref[...]
    moe = w[:, 0:1] * y0_ref[...] + w[:, 1:2] * y1_ref[...]
    x2 = x1_ref[...] + gate2 * moe
    if final_norm:
        ms = jnp.mean(x2 * x2, axis=-1, keepdims=True)
        x2 = x2 * lax.rsqrt(ms + RMS_EPS) * gf_ref[...]
    out_ref[...] = x2


def _combine(x1, rf, mod3, gf, yk, seq, final_norm):
    T, D = x1.shape
    tc = COMBINE_TM
    per_b = seq // tc
    n_blk = T // tc
    return pl.pallas_call(
        functools.partial(_combine_kernel, final_norm=final_norm),
        out_shape=jax.ShapeDtypeStruct((T, D), F32),
        grid=(n_blk,),
        in_specs=[pl.BlockSpec((tc, D), lambda i: (i, 0)),
                  pl.BlockSpec((tc, LANES), lambda i: (i, 0)),
                  pl.BlockSpec((1, 6, D), lambda i: (i // per_b, 0, 0)),
                  pl.BlockSpec((1, D), lambda i: (0, 0)),
                  pl.BlockSpec((tc, D), lambda i: (i, 0)),
                  pl.BlockSpec((tc, D), lambda i: (n_blk + i, 0))],
        out_specs=pl.BlockSpec((tc, D), lambda i: (i, 0)),
        compiler_params=_params("arbitrary"),
        name="combine",
    )(x1, rf, mod3, gf, yk, yk)


def _layer(x2, c, seq, w_ada, b_ada, g_norm1, w_in, b_if, conv_dw_w, conv_dw_b, conv_ln_g, conv_ln_b,
           w_conv_out, qk_conv_w, qk_conv_b, m_norm_g, w_m_out, w_out, g_norm2, w_rg, b_rg,
           w_re, b_re, w_e_gate, w_e_up, w_e_down):
    T, D = x2.shape
    B = T // seq
    dc = D // 2
    nif = 2 * M_HEADS

    mod3 = _ada(c, w_ada, b_ada).reshape(B, 6, D)

    if_lo = 6 * dc
    g1 = g_norm1.reshape(1, D)
    w_main = w_in[:, :if_lo].astype(BF16)
    w_gates = w_in[:, if_lo + nif:].astype(BF16)
    w_if = w_in[:, if_lo:if_lo + nif]
    w_if_pad = jnp.pad(w_if, ((0, 0), (0, LANES - nif))).astype(BF16)
    w_ift = w_if.T.astype(BF16)
    u, q, k, v, so, ifc, ifr = _inproj(x2, mod3, g1, w_main, w_if_pad, w_ift,
                                       qk_conv_w, qk_conv_b.reshape(1, 2 * dc), seq)

    za = _conv_branch(u.reshape(B, seq, dc), conv_dw_w, conv_dw_b.reshape(1, dc),
                      conv_ln_g.reshape(1, dc), conv_ln_b.reshape(1, dc))
    bifc = jnp.pad(b_if, (0, LANES - nif)).reshape(1, LANES)
    bifr = b_if.reshape(nif, 1)
    b3 = lambda a: a.reshape(B, seq, dc)
    hb = _mlstm_branch(b3(q), b3(k), b3(v), b3(so), ifc, ifr, bifc, bifr, m_norm_g.reshape(1, dc))

    n_r = N_GROUPS + N_EXPERTS
    w_r = jnp.pad(jnp.concatenate([w_rg, w_re], axis=1), ((0, 0), (0, LANES - n_r))).astype(BF16)
    b_r = jnp.pad(jnp.concatenate([b_rg, b_re]), (0, LANES - n_r)).reshape(1, LANES)
    x1, h2, ri, rf, cnt = _merge(x2, za.reshape(T, dc), hb.reshape(T, dc), mod3, g1,
                                 g_norm2.reshape(1, D), w_gates, w_conv_out.astype(BF16),
                                 w_m_out.astype(BF16), w_out.astype(BF16), w_r, b_r, seq)

    tm = EXPERT_TM
    counts = cnt[0, :N_EXPERTS]
    tiles_e = (counts + tm - 1) // tm
    tile_end = jnp.cumsum(tiles_e)
    n_tiles = tile_end[-1]
    max_tiles = (T * TOP_K) // tm + N_EXPERTS
    jt = jnp.arange(max_tiles + 1, dtype=I32)
    jc = jnp.minimum(jt, n_tiles - 1)
    tile_e = jnp.sum(jc[:, None] >= tile_end[None, :], axis=1).astype(I32)
    tile_first = ((jt < n_tiles) & (jc == (tile_end - tiles_e)[tile_e])).astype(I32)

    code = ri[:, 0:TOP_K].reshape(T * TOP_K)
    xs, dst = _dispatch(code, counts, h2, max_tiles * tm)
    yk = _experts(tile_e, jc, tile_first, n_tiles.reshape(1).astype(I32), dst, xs,
                  w_e_gate, w_e_up, w_e_down, T, max_tiles)
    return x1, rf, mod3, yk


def kernel(x, c, w_ada, b_ada, g_norm1, w_in, b_if, conv_dw_w, conv_dw_b, conv_ln_g, conv_ln_b,
           w_conv_out, qk_conv_w, qk_conv_b, m_norm_g, w_m_out, w_out, g_norm2, w_rg, b_rg,
           w_re, b_re, w_e_gate, w_e_up, w_e_down, g_final):
    B, S, D = x.shape
    depth = w_ada.shape[0]
    x2 = x.reshape(B * S, D)
    for l in range(depth):
        x1, rf, mod3, yk = _layer(
            x2, c, S, w_ada[l], b_ada[l], g_norm1[l], w_in[l], b_if[l], conv_dw_w[l], conv_dw_b[l],
            conv_ln_g[l], conv_ln_b[l], w_conv_out[l], qk_conv_w[l], qk_conv_b[l], m_norm_g[l],
            w_m_out[l], w_out[l], g_norm2[l], w_rg[l], b_rg[l], w_re[l], b_re[l],
            w_e_gate[l], w_e_up[l], w_e_down[l])
        x2 = _combine(x1, rf, mod3, g_final.reshape(1, D), yk, S, final_norm=l == depth - 1)
    return x2.reshape(B, S, D)
```

```python
import dataclasses
import functools

import jax
import jax.numpy as jnp
from jax import lax
from jax.experimental import pallas as pl
from jax.experimental.pallas import tpu as pltpu
from jax.experimental.pallas import tpu_sc as plsc

F32 = jnp.float32
BF16 = jnp.bfloat16
I32 = jnp.int32

M_HEADS = 4
CONV_WIDTH = 31
QK_CONV_WIDTH = 4
N_GROUPS = 4
E_PER_GROUP = 8
N_EXPERTS = N_GROUPS * E_PER_GROUP
TOP_K = 2
RMS_EPS = 1e-6
LN_EPS = 1e-5

LANES = 128
SUBLANES = 8
VMEM_LIMIT = 56 * 1024 * 1024

ADA_TN = 1024
INPROJ_TM = 256
CONV_TS = 256
CONV_HALO = 32
CONV_RC = 32
MLSTM_L = 128
MERGE_TM = 256
EXPERT_TM = 256
SC_WINDOW = 32
COMBINE_TM = 256
RANK_BITS = 16
RANK_RADIX = 1 << RANK_BITS
assert EXPERT_TM & (EXPERT_TM - 1) == 0


def _sigmoid(v):
    return 1.0 / (1.0 + jnp.exp(-v))


def _log_sigmoid(v):
    return -(jnp.maximum(-v, 0.0) + jnp.log1p(jnp.exp(-jnp.abs(v))))


def _params(*sem):
    return pltpu.CompilerParams(dimension_semantics=sem, vmem_limit_bytes=VMEM_LIMIT)


def _ada_kernel(c_ref, w_ref, b_ref, o_ref):
    c = c_ref[...]
    s = c * _sigmoid(c)
    o_ref[...] = jnp.dot(s, w_ref[...], preferred_element_type=F32,
                         precision=lax.Precision.HIGHEST) + b_ref[...]


def _ada(c, w_ada, b_ada):
    B, D = c.shape
    N = w_ada.shape[1]
    return pl.pallas_call(
        _ada_kernel,
        out_shape=jax.ShapeDtypeStruct((B, N), F32),
        grid=(N // ADA_TN,),
        in_specs=[pl.BlockSpec((B, D), lambda j: (0, 0)),
                  pl.BlockSpec((D, ADA_TN), lambda j: (0, j)),
                  pl.BlockSpec((1, ADA_TN), lambda j: (0, j))],
        out_specs=pl.BlockSpec((B, ADA_TN), lambda j: (0, j)),
        compiler_params=_params("arbitrary"),
        name="ada",
    )(c, w_ada, b_ada.reshape(1, N))


def _inproj_kernel(x_ref, mod_ref, g_ref, wm_ref, wif_ref, wift_ref,
                   u_ref, qk_ref, v_ref, o_ref, sga_ref, sgb_ref, ifc_ref, ifr_ref):
    x = x_ref[...]
    shift = mod_ref[0, 0:1, :]
    scale = mod_ref[0, 1:2, :]
    ms = jnp.mean(x * x, axis=-1, keepdims=True)
    h = x * lax.rsqrt(ms + RMS_EPS) * g_ref[...]
    h = h * (1.0 + scale) + shift
    hb = h.astype(BF16)
    dc = u_ref.shape[1]
    d = sga_ref.shape[1]

    def seg(lo, hi):
        return jnp.dot(hb, wm_ref[:, lo:hi], preferred_element_type=F32)

    u_ref[...] = seg(0, dc) * _sigmoid(seg(dc, 2 * dc))
    qk_ref[...] = seg(2 * dc, 4 * dc)
    v_ref[...] = seg(4 * dc, 5 * dc)
    o_ref[...] = seg(5 * dc, 6 * dc)
    sga_ref[...] = _sigmoid(seg(6 * dc, 6 * dc + d))
    sgb_ref[...] = _sigmoid(seg(6 * dc + d, 6 * dc + 2 * d))
    ifc_ref[...] = jnp.dot(hb, wif_ref[...], preferred_element_type=F32)
    ifr_ref[...] = lax.dot_general(wift_ref[...], hb, (((1,), (1,)), ((), ())),
                                   preferred_element_type=F32)


def _inproj(x2, mod3, g1, w_main, w_if, w_ift, seq):
    T, D = x2.shape
    tm = INPROJ_TM
    dc = D // 2
    per_b = seq // tm
    row = lambda i: (i, 0)
    const = lambda i: (0, 0)
    return pl.pallas_call(
        _inproj_kernel,
        out_shape=[jax.ShapeDtypeStruct((T, dc), F32),
                   jax.ShapeDtypeStruct((T, 2 * dc), F32),
                   jax.ShapeDtypeStruct((T, dc), F32),
                   jax.ShapeDtypeStruct((T, dc), F32),
                   jax.ShapeDtypeStruct((T, D), F32),
                   jax.ShapeDtypeStruct((T, D), F32),
                   jax.ShapeDtypeStruct((T, LANES), F32),
                   jax.ShapeDtypeStruct((SUBLANES, T), F32)],
        grid=(T // tm,),
        in_specs=[pl.BlockSpec((tm, D), row),
                  pl.BlockSpec((1, 6, D), lambda i: (i // per_b, 0, 0)),
                  pl.BlockSpec((1, D), const),
                  pl.BlockSpec(w_main.shape, const),
                  pl.BlockSpec(w_if.shape, const),
                  pl.BlockSpec(w_ift.shape, const)],
        out_specs=[pl.BlockSpec((tm, dc), row),
                   pl.BlockSpec((tm, 2 * dc), row),
                   pl.BlockSpec((tm, dc), row),
                   pl.BlockSpec((tm, dc), row),
                   pl.BlockSpec((tm, D), row),
                   pl.BlockSpec((tm, D), row),
                   pl.BlockSpec((tm, LANES), row),
                   pl.BlockSpec((SUBLANES, tm), lambda i: (0, i))],
        compiler_params=_params("arbitrary"),
        name="inproj",
    )(x2, mod3, g1, w_main, w_if, w_ift)


def _conv_kernel(u_ref, w_ref, b_ref, lg_ref, lb_ref, wo_ref, y_ref, ubuf, sbuf, cbuf):
    ts = u_ref.shape[1]
    halo = CONV_HALO

    @pl.when(pl.program_id(1) == 0)
    def _():
        ubuf[0:halo, :] = jnp.zeros((halo, ubuf.shape[1]), F32)

    ubuf[halo:halo + ts, :] = u_ref[0]
    ns = sbuf.shape[1]
    for r in range(1, SUBLANES):
        sbuf[r - 1] = ubuf[r:r + ns, :]
    off = halo - (CONV_WIDTH - 1)
    for r0 in range(0, ts, CONV_RC):
        acc = jnp.broadcast_to(b_ref[...], (CONV_RC, ubuf.shape[1]))
        for k in range(CONV_WIDTH):
            r = (off + k) % SUBLANES
            lo = off + k - r + r0
            win = ubuf[lo:lo + CONV_RC, :] if r == 0 else sbuf[r - 1, lo:lo + CONV_RC, :]
            acc = acc + w_ref[k:k + 1, :] * win
        cbuf[r0:r0 + CONV_RC, :] = acc
    ubuf[0:halo, :] = ubuf[ts:ts + halo, :]

    a = cbuf[...]
    mu = jnp.mean(a, axis=-1, keepdims=True)
    ac = a - mu
    var = jnp.mean(ac * ac, axis=-1, keepdims=True)
    z = ac * lax.rsqrt(var + LN_EPS) * lg_ref[...] + lb_ref[...]
    z = z * _sigmoid(z)
    y_ref[0] = jnp.dot(z.astype(BF16), wo_ref[...], preferred_element_type=F32)


def _conv_branch(u3, w, b, lg, lb, wo):
    B, S, C = u3.shape
    D = wo.shape[1]
    ts = CONV_TS
    const = lambda bi, si: (0, 0)
    return pl.pallas_call(
        _conv_kernel,
        out_shape=jax.ShapeDtypeStruct((B, S, D), F32),
        grid=(B, S // ts),
        in_specs=[pl.BlockSpec((1, ts, C), lambda bi, si: (bi, si, 0)),
                  pl.BlockSpec(w.shape, const),
                  pl.BlockSpec((1, C), const),
                  pl.BlockSpec((1, C), const),
                  pl.BlockSpec((1, C), const),
                  pl.BlockSpec(wo.shape, const)],
        out_specs=pl.BlockSpec((1, ts, D), lambda bi, si: (bi, si, 0)),
        scratch_shapes=[pltpu.VMEM((ts + CONV_HALO, C), F32),
                        pltpu.VMEM((SUBLANES - 1, ts + CONV_HALO - SUBLANES, C), F32),
                        pltpu.VMEM((ts, C), F32)],
        compiler_params=_params("arbitrary", "arbitrary"),
        name="conv",
    )(u3, w, b, lg, lb, wo)


def _mlstm_kernel(qk_ref, v_ref, o_ref, ifc_ref, ifr_ref, cw_ref, cb_ref, bifc_ref, bifr_ref,
                  ng_ref, wo_ref, y_ref, qkbuf, cn_ref, m_ref, hbuf):
    L = qk_ref.shape[1]
    mi = v_ref.shape[2]
    dh = mi // M_HEADS
    halo = SUBLANES

    @pl.when(pl.program_id(1) == 0)
    def _():
        qkbuf[0:halo, :] = jnp.zeros((halo, qkbuf.shape[1]), F32)
        cn_ref[...] = jnp.zeros(cn_ref.shape, F32)
        m_ref[...] = jnp.zeros(m_ref.shape, F32)

    qkbuf[halo:halo + L, :] = qk_ref[0]
    off = halo - (QK_CONV_WIDTH - 1)
    y = jnp.broadcast_to(cb_ref[...], (L, qkbuf.shape[1]))
    for k in range(QK_CONV_WIDTH):
        y = y + cw_ref[k:k + 1, :] * qkbuf[off + k:off + k + L, :]
    y = y * _sigmoid(y)
    qkbuf[0:halo, :] = qkbuf[L:L + halo, :]

    ifr = ifr_ref[...] + bifr_ref[...]
    ifc = ifc_ref[...] + bifc_ref[...]
    lfr = _log_sigmoid(ifr)
    lfc = _log_sigmoid(ifc)
    rows = lax.broadcasted_iota(I32, (L, L), 0)
    cols = lax.broadcasted_iota(I32, (L, L), 1)
    causal = cols <= rows
    lower = causal.astype(F32)
    upper = (rows <= cols).astype(F32)
    bcum_c = jnp.dot(lower, lfc, preferred_element_type=F32, precision=lax.Precision.HIGHEST)
    bcum_r = jnp.dot(lfr, upper, preferred_element_type=F32, precision=lax.Precision.HIGHEST)

    lane = lax.broadcasted_iota(I32, (L, dh), 1)
    ones_col = jnp.where(lane == 0, 1.0, 0.0).astype(F32)
    vv = v_ref[0]
    oo = o_ref[0]
    scale = dh ** -0.5
    for hd in range(M_HEADS):
        q = y[:, hd * dh:(hd + 1) * dh] * scale
        kk = y[:, mi + hd * dh:mi + (hd + 1) * dh]
        v = vv[:, hd * dh:(hd + 1) * dh]
        kt = kk.T
        bc = bcum_c[:, M_HEADS + hd:M_HEADS + hd + 1]
        br = bcum_r[M_HEADS + hd:M_HEADS + hd + 1, :]
        li = ifr[hd:hd + 1, :]
        m_prev = m_ref[hd, 0:1, 0:1]
        dmat = jnp.where(causal, bc - br + li, -jnp.inf)
        inter = bc + m_prev
        m_t = jnp.maximum(jnp.max(dmat, axis=-1, keepdims=True), inter)
        wts = jnp.exp(dmat - m_t)
        s_inter = jnp.exp(inter - m_t)
        qb = q.astype(BF16)
        s_mat = jnp.dot(qb, kt.astype(BF16), preferred_element_type=F32) * wts
        cn = cn_ref[hd]
        qcn = jnp.dot(qb, cn.astype(BF16), preferred_element_type=F32)
        num = jnp.dot(s_mat.astype(BF16), v.astype(BF16), preferred_element_type=F32) \
            + s_inter * qcn[:, 0:dh]
        den = jnp.sum(s_mat, axis=-1, keepdims=True) + s_inter * qcn[:, dh:dh + 1]
        hh = num / jnp.maximum(jnp.abs(den), jnp.exp(-m_t))
        b_last = br[:, L - 1:L]
        a = b_last - br + li
        m_new = jnp.maximum(b_last + m_prev, jnp.max(a, axis=-1, keepdims=True))
        wk = jnp.exp(a - m_new)
        sc = jnp.exp(b_last + m_prev - m_new)
        v_ext = jnp.concatenate([v, ones_col], axis=1)
        cn_ref[hd] = sc * cn + jnp.dot((kt * wk).astype(BF16), v_ext.astype(BF16),
                                       preferred_element_type=F32)
        m_ref[hd] = jnp.broadcast_to(m_new, m_ref.shape[1:])
        mu = jnp.mean(hh, axis=-1, keepdims=True)
        hc = hh - mu
        var = jnp.mean(hc * hc, axis=-1, keepdims=True)
        hn = hc * lax.rsqrt(var + LN_EPS) * ng_ref[:, hd * dh:(hd + 1) * dh]
        hbuf[:, hd * dh:(hd + 1) * dh] = hn * _sigmoid(oo[:, hd * dh:(hd + 1) * dh])
    y_ref[0] = jnp.dot(hbuf[...].astype(BF16), wo_ref[...], preferred_element_type=F32)


def _mlstm_branch(qk3, v3, o3, ifc, ifr, cw, cb, bifc, bifr, ng, wo):
    B, S, C2 = qk3.shape
    mi = v3.shape[2]
    dh = mi // M_HEADS
    D = wo.shape[1]
    L = MLSTM_L
    nc = S // L
    const = lambda bi, ci: (0, 0)
    tile = lambda bi, ci: (bi, ci, 0)
    return pl.pallas_call(
        _mlstm_kernel,
        out_shape=jax.ShapeDtypeStruct((B, S, D), F32),
        grid=(B, nc),
        in_specs=[pl.BlockSpec((1, L, C2), tile),
                  pl.BlockSpec((1, L, mi), tile),
                  pl.BlockSpec((1, L, mi), tile),
                  pl.BlockSpec((L, LANES), lambda bi, ci: (bi * nc + ci, 0)),
                  pl.BlockSpec((SUBLANES, L), lambda bi, ci: (0, bi * nc + ci)),
                  pl.BlockSpec(cw.shape, const),
                  pl.BlockSpec((1, C2), const),
                  pl.BlockSpec((1, LANES), const),
                  pl.BlockSpec((SUBLANES, 1), const),
                  pl.BlockSpec((1, mi), const),
                  pl.BlockSpec(wo.shape, const)],
        out_specs=pl.BlockSpec((1, L, D), tile),
        scratch_shapes=[pltpu.VMEM((L + SUBLANES, C2), F32),
                        pltpu.VMEM((M_HEADS, dh, 2 * dh), F32),
                        pltpu.VMEM((M_HEADS, SUBLANES, LANES), F32),
                        pltpu.VMEM((L, mi), F32)],
        compiler_params=_params("arbitrary", "arbitrary"),
        name="mlstm",
    )(qk3, v3, o3, ifc, ifr, cw, cb, bifc, bifr, ng, wo)


def _merge_kernel(x_ref, ya_ref, yb_ref, sga_ref, sgb_ref, mod_ref, g2_ref, wo_ref, wr_ref, br_ref,
                  x1_ref, h2_ref, ri_ref, rf_ref, cnt_ref, run_ref):
    tm = x_ref.shape[0]

    @pl.when(pl.program_id(0) == 0)
    def _():
        run_ref[...] = jnp.zeros(run_ref.shape, F32)

    gate1 = mod_ref[0, 2:3, :]
    shift2 = mod_ref[0, 3:4, :]
    scale2 = mod_ref[0, 4:5, :]
    merged = sga_ref[...] * ya_ref[...] + sgb_ref[...] * yb_ref[...]
    mix = jnp.dot(merged.astype(BF16), wo_ref[...], preferred_element_type=F32)
    x1 = x_ref[...] + gate1 * mix
    x1_ref[...] = x1
    ms = jnp.mean(x1 * x1, axis=-1, keepdims=True)
    h2 = x1 * lax.rsqrt(ms + RMS_EPS) * g2_ref[...]
    h2 = h2 * (1.0 + scale2) + shift2
    h2_ref[...] = h2

    logits = jnp.dot(h2.astype(BF16), wr_ref[...], preferred_element_type=F32) + br_ref[...]
    lane = lax.broadcasted_iota(I32, (tm, LANES), 1).astype(F32)
    neg = -jnp.inf

    def first_argmax(vals):
        mx = jnp.max(vals, axis=-1, keepdims=True)
        idx = jnp.min(jnp.where(vals == mx, lane, float(LANES)), axis=-1, keepdims=True)
        return mx, idx

    lg = jnp.where(lane < N_GROUPS, logits, neg)
    gmax, gsel = first_argmax(lg)
    p_g = 1.0 / jnp.sum(jnp.exp(lg - gmax), axis=-1, keepdims=True)
    lo = N_GROUPS + gsel * E_PER_GROUP
    le = jnp.where((lane >= lo) & (lane < lo + E_PER_GROUP), logits, neg)
    l1, i1 = first_argmax(le)
    l2, i2 = first_argmax(jnp.where(lane == i1, neg, le))
    r = jnp.exp(l2 - l1)
    w1 = p_g / (1.0 + r)
    w2 = p_g * r / (1.0 + r)
    e1 = i1 - N_GROUPS
    e2 = i2 - N_GROUPS

    onehot = jnp.where((lane == e1) | (lane == e2), 1.0, 0.0)
    rows = lax.broadcasted_iota(I32, (tm, tm), 0)
    cols = lax.broadcasted_iota(I32, (tm, tm), 1)
    strict = jnp.where(cols < rows, 1.0, 0.0).astype(BF16)
    run = run_ref[0:1, :]
    before = jnp.dot(strict, onehot.astype(BF16), preferred_element_type=F32) + run
    rank1 = jnp.sum(jnp.where(lane == e1, before, 0.0), axis=-1, keepdims=True)
    rank2 = jnp.sum(jnp.where(lane == e2, before, 0.0), axis=-1, keepdims=True)
    run_new = run + jnp.sum(onehot, axis=0, keepdims=True)
    run_ref[...] = jnp.broadcast_to(run_new, run_ref.shape)
    cnt_ref[...] = jnp.broadcast_to(run_new, cnt_ref.shape).astype(I32)

    ri_ref[...] = jnp.where(lane == 0, e1 * float(RANK_RADIX) + rank1,
                            jnp.where(lane == 1, e2 * float(RANK_RADIX) + rank2, 0.0)).astype(I32)
    rf_ref[...] = jnp.where(lane == 0, w1, jnp.where(lane == 1, w2, 0.0))


def _merge(x2, ya, yb, sga, sgb, mod3, g2, wo, wr, br, seq):
    T, D = x2.shape
    tm = MERGE_TM
    per_b = seq // tm
    row = lambda i: (i, 0)
    const = lambda i: (0, 0)
    return pl.pallas_call(
        _merge_kernel,
        out_shape=[jax.ShapeDtypeStruct((T, D), F32),
                   jax.ShapeDtypeStruct((T, D), F32),
                   jax.ShapeDtypeStruct((T, LANES), I32),
                   jax.ShapeDtypeStruct((T, LANES), F32),
                   jax.ShapeDtypeStruct((SUBLANES, LANES), I32)],
        grid=(T // tm,),
        in_specs=[pl.BlockSpec((tm, D), row),
                  pl.BlockSpec((tm, D), row),
                  pl.BlockSpec((tm, D), row),
                  pl.BlockSpec((tm, D), row),
                  pl.BlockSpec((tm, D), row),
                  pl.BlockSpec((1, 6, D), lambda i: (i // per_b, 0, 0)),
                  pl.BlockSpec((1, D), const),
                  pl.BlockSpec(wo.shape, const),
                  pl.BlockSpec(wr.shape, const),
                  pl.BlockSpec((1, LANES), const)],
        out_specs=[pl.BlockSpec((tm, D), row),
                   pl.BlockSpec((tm, D), row),
                   pl.BlockSpec((tm, LANES), row),
                   pl.BlockSpec((tm, LANES), row),
                   pl.BlockSpec((SUBLANES, LANES), const)],
        scratch_shapes=[pltpu.VMEM((SUBLANES, LANES), F32)],
        compiler_params=_params("arbitrary"),
        name="merge",
    )(x2, ya, yb, sga, sgb, mod3, g2, wo, wr, br)


def _sc_workers():
    info = plsc.get_sparse_core_info()
    mesh = plsc.VectorSubcoreMesh(core_axis_name="core", subcore_axis_name="subcore")
    params = pltpu.CompilerParams()
    if "needs_layout_passes" in pltpu.CompilerParams.__dataclass_fields__:
        params = dataclasses.replace(params, needs_layout_passes=False)
    return info, mesh, params


def _rows_from_codes(code_v, base_v, idx_v, lanes):
    for j in range(code_v.shape[0] // lanes):
        c = code_v[pl.ds(j * lanes, lanes)]
        expert = lax.shift_right_logical(c, RANK_BITS)
        idx_v[pl.ds(j * lanes, lanes)] = plsc.load_gather(base_v, [expert]) + (c & (RANK_RADIX - 1))


def _two_slot_loop(n_chunks, start, finish):
    start(0, 0)

    @pl.loop(0, n_chunks, step=2)
    def _(c):
        start(c + 1, 1)
        finish(c, 0)

        @pl.when(c + 2 < n_chunks)
        def _():
            start(c + 2, 0)

        finish(c + 1, 1)


def _sc_dispatch(h2, code0, code1, base, n_rows):
    T, D = h2.shape
    info, mesh, params = _sc_workers()
    n_workers = info.num_cores * info.num_subcores
    w = SC_WINDOW
    per_w = T // n_workers
    n_chunks = per_w // w
    assert per_w * n_workers == T and n_chunks * w == per_w and n_chunks % 2 == 0

    @functools.partial(
        pl.kernel, out_type=jax.ShapeDtypeStruct((n_rows, D), h2.dtype), mesh=mesh, compiler_params=params,
        scratch_types=[pltpu.VMEM((N_EXPERTS,), I32), pltpu.VMEM((w,), I32), pltpu.VMEM((w,), I32),
                       pltpu.VMEM((w,), I32), pltpu.VMEM((w, D), h2.dtype), pltpu.VMEM((w, D), h2.dtype),
                       pltpu.SemaphoreType.DMA, pltpu.SemaphoreType.DMA])
    def scatter(h_hbm, c0_hbm, c1_hbm, b_hbm, xs_hbm, base_v, code_v, i0_v, i1_v, rows0, rows1, sem0, sem1):
        wid = lax.axis_index("subcore") * info.num_cores + lax.axis_index("core")
        w0 = wid * per_w
        pltpu.sync_copy(b_hbm, base_v)
        rows = (rows0, rows1)
        sems = (sem0, sem1)

        def start(c, slot):
            pltpu.async_copy(h_hbm.at[pl.ds(w0 + c * w, w)], rows[slot], sems[slot])

        def finish(c, slot):
            pltpu.sync_copy(c0_hbm.at[pl.ds(w0 + c * w, w)], code_v)
            _rows_from_codes(code_v, base_v, i0_v, info.num_lanes)
            pltpu.sync_copy(c1_hbm.at[pl.ds(w0 + c * w, w)], code_v)
            _rows_from_codes(code_v, base_v, i1_v, info.num_lanes)
            pltpu.make_async_copy(h_hbm.at[pl.ds(w0 + c * w, w)], rows[slot], sems[slot]).wait()
            pltpu.sync_copy(rows[slot], xs_hbm.at[i0_v])
            pltpu.sync_copy(rows[slot], xs_hbm.at[i1_v])

        _two_slot_loop(n_chunks, start, finish)

    return scatter(h2, code0, code1, base)


def _sc_collect(ys, codes, base):
    n = codes.shape[0]
    D = ys.shape[1]
    info, mesh, params = _sc_workers()
    n_workers = info.num_cores * info.num_subcores
    w = SC_WINDOW
    per_w = n // n_workers
    n_chunks = per_w // w
    assert per_w * n_workers == n and n_chunks * w == per_w and n_chunks % 2 == 0

    @functools.partial(
        pl.kernel, out_type=jax.ShapeDtypeStruct((n, D), ys.dtype), mesh=mesh, compiler_params=params,
        scratch_types=[pltpu.VMEM((N_EXPERTS,), I32), pltpu.VMEM((w,), I32), pltpu.VMEM((w,), I32),
                       pltpu.VMEM((w,), I32), pltpu.VMEM((w, D), ys.dtype), pltpu.VMEM((w, D), ys.dtype),
                       pltpu.SemaphoreType.DMA, pltpu.SemaphoreType.DMA])
    def gather(ys_hbm, c_hbm, b_hbm, yk_hbm, base_v, code_v, i0_v, i1_v, rows0, rows1, sem0, sem1):
        wid = lax.axis_index("subcore") * info.num_cores + lax.axis_index("core")
        w0 = wid * per_w
        pltpu.sync_copy(b_hbm, base_v)
        idx = (i0_v, i1_v)
        rows = (rows0, rows1)
        sems = (sem0, sem1)

        def start(c, slot):
            pltpu.sync_copy(c_hbm.at[pl.ds(w0 + c * w, w)], code_v)
            _rows_from_codes(code_v, base_v, idx[slot], info.num_lanes)
            pltpu.async_copy(ys_hbm.at[idx[slot]], rows[slot], sems[slot])

        def finish(c, slot):
            pltpu.make_async_copy(ys_hbm.at[idx[slot]], rows[slot], sems[slot]).wait()
            pltpu.sync_copy(rows[slot], yk_hbm.at[pl.ds(w0 + c * w, w)])

        _two_slot_loop(n_chunks, start, finish)

    return gather(ys, codes, base)


def _expert_kernel(te_ref, tb_ref, first_ref, nt_ref, xs_ref, wg_ref, wu_ref, wd_ref, ys_ref,
                   wgb, wub, wdb):
    j = pl.program_id(0)

    @pl.when(j < nt_ref[0])
    def _():
        @pl.when(first_ref[j] == 1)
        def _():
            wgb[...] = wg_ref[0].astype(BF16)
            wub[...] = wu_ref[0].astype(BF16)
            wdb[...] = wd_ref[0].astype(BF16)

        xb = xs_ref[...].astype(BF16)
        g = jnp.dot(xb, wgb[...], preferred_element_type=F32)
        u = jnp.dot(xb, wub[...], preferred_element_type=F32)
        act = (g * _sigmoid(g)) * u
        ys_ref[...] = jnp.dot(act.astype(BF16), wdb[...], preferred_element_type=F32)


def _experts(tile_e, tile_b, tile_first, n_tiles, xs, wg, wu, wd, max_tiles):
    P, D = xs.shape
    de = wg.shape[2]
    tm = EXPERT_TM
    wmap = lambda j, te, tb, tf, nt: (te[j], 0, 0)
    rmap = lambda j, te, tb, tf, nt: (tb[j], 0)
    return pl.pallas_call(
        _expert_kernel,
        out_shape=jax.ShapeDtypeStruct((P, D), F32),
        grid_spec=pltpu.PrefetchScalarGridSpec(
            num_scalar_prefetch=4,
            grid=(max_tiles,),
            in_specs=[pl.BlockSpec((tm, D), rmap),
                      pl.BlockSpec((1, D, de), wmap),
                      pl.BlockSpec((1, D, de), wmap),
                      pl.BlockSpec((1, de, D), wmap)],
            out_specs=pl.BlockSpec((tm, D), rmap),
            scratch_shapes=[pltpu.VMEM((D, de), BF16),
                            pltpu.VMEM((D, de), BF16),
                            pltpu.VMEM((de, D), BF16)]),
        compiler_params=_params("arbitrary"),
        name="experts",
    )(tile_e, tile_b, tile_first, n_tiles, xs, wg, wu, wd)


def _combine_kernel(x1_ref, rf_ref, mod_ref, gf_ref, y0_ref, y1_ref, out_ref, *, final_norm):
    gate2 = mod_ref[0, 5:6, :]
    w = rf_ref[...]
    moe = w[:, 0:1] * y0_ref[...] + w[:, 1:2] * y1_ref[...]
    x2 = x1_ref[...] + gate2 * moe
    if final_norm:
        ms = jnp.mean(x2 * x2, axis=-1, keepdims=True)
        x2 = x2 * lax.rsqrt(ms + RMS_EPS) * gf_ref[...]
    out_ref[...] = x2


def _combine(x1, rf, mod3, gf, yk, seq, final_norm):
    T, D = x1.shape
    tc = COMBINE_TM
    per_b = seq // tc
    n_blk = T // tc
    return pl.pallas_call(
        functools.partial(_combine_kernel, final_norm=final_norm),
        out_shape=jax.ShapeDtypeStruct((T, D), F32),
        grid=(n_blk,),
        in_specs=[pl.BlockSpec((tc, D), lambda i: (i, 0)),
                  pl.BlockSpec((tc, LANES), lambda i: (i, 0)),
                  pl.BlockSpec((1, 6, D), lambda i: (i // per_b, 0, 0)),
                  pl.BlockSpec((1, D), lambda i: (0, 0)),
                  pl.BlockSpec((tc, D), lambda i: (i, 0)),
                  pl.BlockSpec((tc, D), lambda i: (n_blk + i, 0))],
        out_specs=pl.BlockSpec((tc, D), lambda i: (i, 0)),
        compiler_params=_params("arbitrary"),
        name="combine",
    )(x1, rf, mod3, gf, yk, yk)


def _layer(x2, c, seq, w_ada, b_ada, g_norm1, w_in, b_if, conv_dw_w, conv_dw_b, conv_ln_g, conv_ln_b,
           w_conv_out, qk_conv_w, qk_conv_b, m_norm_g, w_m_out, w_out, g_norm2, w_rg, b_rg,
           w_re, b_re, w_e_gate, w_e_up, w_e_down):
    T, D = x2.shape
    B = T // seq
    dc = D // 2
    nif = 2 * M_HEADS

    mod3 = _ada(c, w_ada, b_ada).reshape(B, 6, D)

    if_lo = 6 * dc
    w_main = jnp.concatenate([w_in[:, :if_lo], w_in[:, if_lo + nif:]], axis=1).astype(BF16)
    w_if = w_in[:, if_lo:if_lo + nif]
    w_if_pad = jnp.pad(w_if, ((0, 0), (0, LANES - nif))).astype(BF16)
    w_ift = w_if.T.astype(BF16)
    u, qk, v, o, sga, sgb, ifc, ifr = _inproj(x2, mod3, g_norm1.reshape(1, D), w_main, w_if_pad, w_ift, seq)

    ya = _conv_branch(u.reshape(B, seq, dc), conv_dw_w, conv_dw_b.reshape(1, dc),
                      conv_ln_g.reshape(1, dc), conv_ln_b.reshape(1, dc), w_conv_out.astype(BF16))
    bifc = jnp.pad(b_if, (0, LANES - nif)).reshape(1, LANES)
    bifr = b_if.reshape(nif, 1)
    yb = _mlstm_branch(qk.reshape(B, seq, 2 * dc), v.reshape(B, seq, dc), o.reshape(B, seq, dc),
                       ifc, ifr, qk_conv_w, qk_conv_b.reshape(1, 2 * dc), bifc, bifr,
                       m_norm_g.reshape(1, dc), w_m_out.astype(BF16))

    n_r = N_GROUPS + N_EXPERTS
    w_r = jnp.pad(jnp.concatenate([w_rg, w_re], axis=1), ((0, 0), (0, LANES - n_r))).astype(BF16)
    b_r = jnp.pad(jnp.concatenate([b_rg, b_re]), (0, LANES - n_r)).reshape(1, LANES)
    x1, h2, ri, rf, cnt = _merge(x2, ya.reshape(T, D), yb.reshape(T, D), sga, sgb, mod3,
                                 g_norm2.reshape(1, D), w_out.astype(BF16), w_r, b_r, seq)

    tm = EXPERT_TM
    counts = cnt[0, :N_EXPERTS]
    tiles_e = (counts + tm - 1) // tm
    tile_end = jnp.cumsum(tiles_e)
    n_tiles = tile_end[-1]
    max_tiles = (T * TOP_K) // tm + N_EXPERTS
    jt = jnp.arange(max_tiles, dtype=I32)
    jc = jnp.minimum(jt, n_tiles - 1)
    tile_e = jnp.sum(jc[:, None] >= tile_end[None, :], axis=1).astype(I32)
    tile_first = ((jt < n_tiles) & (jc == (tile_end - tiles_e)[tile_e])).astype(I32)
    base = ((tile_end - tiles_e) * tm).astype(I32)

    code0 = ri[:, 0]
    code1 = ri[:, 1]
    xs = _sc_dispatch(h2, code0, code1, base, max_tiles * tm)
    ys = _experts(tile_e, jc, tile_first, n_tiles.reshape(1).astype(I32), xs,
                  w_e_gate, w_e_up, w_e_down, max_tiles)
    yk = _sc_collect(ys, jnp.concatenate([code0, code1]), base)
    return x1, rf, mod3, yk


def kernel(x, c, w_ada, b_ada, g_norm1, w_in, b_if, conv_dw_w, conv_dw_b, conv_ln_g, conv_ln_b,
           w_conv_out, qk_conv_w, qk_conv_b, m_norm_g, w_m_out, w_out, g_norm2, w_rg, b_rg,
           w_re, b_re, w_e_gate, w_e_up, w_e_down, g_final):
    B, S, D = x.shape
    depth = w_ada.shape[0]
    x2 = x.reshape(B * S, D)
    for l in range(depth):
        x1, rf, mod3, yk = _layer(
            x2, c, S, w_ada[l], b_ada[l], g_norm1[l], w_in[l], b_if[l], conv_dw_w[l], conv_dw_b[l],
            conv_ln_g[l], conv_ln_b[l], w_conv_out[l], qk_conv_w[l], qk_conv_b[l], m_norm_g[l],
            w_m_out[l], w_out[l], g_norm2[l], w_rg[l], b_rg[l], w_re[l], b_re[l],
            w_e_gate[l], w_e_up[l], w_e_down[l])
        x2 = _combine(x1, rf, mod3, g_final.reshape(1, D), yk, S, final_norm=l == depth - 1)
    return x2.reshape(B, S, D)
```

```python
import dataclasses
import functools

import jax
import jax.numpy as jnp
from jax import lax
from jax.experimental import pallas as pl
from jax.experimental.pallas import tpu as pltpu
from jax.experimental.pallas import tpu_sc as plsc

F32 = jnp.float32
BF16 = jnp.bfloat16
I32 = jnp.int32

M_HEADS = 4
CONV_WIDTH = 31
QK_CONV_WIDTH = 4
N_GROUPS = 4
E_PER_GROUP = 8
N_EXPERTS = N_GROUPS * E_PER_GROUP
TOP_K = 2
RMS_EPS = 1e-6
LN_EPS = 1e-5

LANES = 128
SUBLANES = 8
VMEM_LIMIT = 56 * 1024 * 1024

ADA_TN = 1024
INPROJ_TM = 256
CONV_TS = 256
CONV_HALO = 32
CONV_RC = 32
MLSTM_L = 128
MERGE_TM = 256
EXPERT_TM = 512
SC_WINDOW = 32
COMBINE_TM = 256
COLLECT_PARTS = 4
RANK_BITS = 16
RANK_RADIX = 1 << RANK_BITS
assert EXPERT_TM & (EXPERT_TM - 1) == 0


def _sigmoid(v):
    return 1.0 / (1.0 + jnp.exp(-v))


def _log_sigmoid(v):
    return -(jnp.maximum(-v, 0.0) + jnp.log1p(jnp.exp(-jnp.abs(v))))


def _params(*sem):
    return pltpu.CompilerParams(dimension_semantics=sem, vmem_limit_bytes=VMEM_LIMIT)


def _ada_kernel(c_ref, w_ref, b_ref, o_ref):
    c = c_ref[...]
    s = c * _sigmoid(c)
    o_ref[...] = jnp.dot(s, w_ref[...], preferred_element_type=F32,
                         precision=lax.Precision.HIGHEST) + b_ref[...]


def _ada(c, w_ada, b_ada):
    B, D = c.shape
    N = w_ada.shape[1]
    return pl.pallas_call(
        _ada_kernel,
        out_shape=jax.ShapeDtypeStruct((B, N), F32),
        grid=(N // ADA_TN,),
        in_specs=[pl.BlockSpec((B, D), lambda j: (0, 0)),
                  pl.BlockSpec((D, ADA_TN), lambda j: (0, j)),
                  pl.BlockSpec((1, ADA_TN), lambda j: (0, j))],
        out_specs=pl.BlockSpec((B, ADA_TN), lambda j: (0, j)),
        compiler_params=_params("arbitrary"),
        name="ada",
    )(c, w_ada, b_ada.reshape(1, N))


def _inproj_kernel(x_ref, mod_ref, g_ref, wm_ref, wif_ref, wift_ref,
                   u_ref, qk_ref, v_ref, o_ref, sga_ref, sgb_ref, ifc_ref, ifr_ref):
    x = x_ref[...]
    shift = mod_ref[0, 0:1, :]
    scale = mod_ref[0, 1:2, :]
    ms = jnp.mean(x * x, axis=-1, keepdims=True)
    h = x * lax.rsqrt(ms + RMS_EPS) * g_ref[...]
    h = h * (1.0 + scale) + shift
    hb = h.astype(BF16)
    dc = u_ref.shape[1]
    d = sga_ref.shape[1]

    def seg(lo, hi):
        return jnp.dot(hb, wm_ref[:, lo:hi], preferred_element_type=F32)

    u_ref[...] = seg(0, dc) * _sigmoid(seg(dc, 2 * dc))
    qk_ref[...] = seg(2 * dc, 4 * dc)
    v_ref[...] = seg(4 * dc, 5 * dc)
    o_ref[...] = seg(5 * dc, 6 * dc)
    sga_ref[...] = _sigmoid(seg(6 * dc, 6 * dc + d))
    sgb_ref[...] = _sigmoid(seg(6 * dc + d, 6 * dc + 2 * d))
    ifc_ref[...] = jnp.dot(hb, wif_ref[...], preferred_element_type=F32)
    ifr_ref[...] = lax.dot_general(wift_ref[...], hb, (((1,), (1,)), ((), ())),
                                   preferred_element_type=F32)


def _inproj(x2, mod3, g1, w_main, w_if, w_ift, seq):
    T, D = x2.shape
    tm = INPROJ_TM
    dc = D // 2
    per_b = seq // tm
    row = lambda i: (i, 0)
    const = lambda i: (0, 0)
    return pl.pallas_call(
        _inproj_kernel,
        out_shape=[jax.ShapeDtypeStruct((T, dc), F32),
                   jax.ShapeDtypeStruct((T, 2 * dc), F32),
                   jax.ShapeDtypeStruct((T, dc), F32),
                   jax.ShapeDtypeStruct((T, dc), F32),
                   jax.ShapeDtypeStruct((T, D), F32),
                   jax.ShapeDtypeStruct((T, D), F32),
                   jax.ShapeDtypeStruct((T, LANES), F32),
                   jax.ShapeDtypeStruct((SUBLANES, T), F32)],
        grid=(T // tm,),
        in_specs=[pl.BlockSpec((tm, D), row),
                  pl.BlockSpec((1, 6, D), lambda i: (i // per_b, 0, 0)),
                  pl.BlockSpec((1, D), const),
                  pl.BlockSpec(w_main.shape, const),
                  pl.BlockSpec(w_if.shape, const),
                  pl.BlockSpec(w_ift.shape, const)],
        out_specs=[pl.BlockSpec((tm, dc), row),
                   pl.BlockSpec((tm, 2 * dc), row),
                   pl.BlockSpec((tm, dc), row),
                   pl.BlockSpec((tm, dc), row),
                   pl.BlockSpec((tm, D), row),
                   pl.BlockSpec((tm, D), row),
                   pl.BlockSpec((tm, LANES), row),
                   pl.BlockSpec((SUBLANES, tm), lambda i: (0, i))],
        compiler_params=_params("arbitrary"),
        name="inproj",
    )(x2, mod3, g1, w_main, w_if, w_ift)


def _conv_kernel(u_ref, w_ref, b_ref, lg_ref, lb_ref, wo_ref, y_ref, ubuf, sbuf, cbuf):
    ts = u_ref.shape[1]
    halo = CONV_HALO

    @pl.when(pl.program_id(1) == 0)
    def _():
        ubuf[0:halo, :] = jnp.zeros((halo, ubuf.shape[1]), F32)

    ubuf[halo:halo + ts, :] = u_ref[0]
    ns = sbuf.shape[1]
    for r in range(1, SUBLANES):
        sbuf[r - 1] = ubuf[r:r + ns, :]
    off = halo - (CONV_WIDTH - 1)
    for r0 in range(0, ts, CONV_RC):
        acc = jnp.broadcast_to(b_ref[...], (CONV_RC, ubuf.shape[1]))
        for k in range(CONV_WIDTH):
            r = (off + k) % SUBLANES
            lo = off + k - r + r0
            win = ubuf[lo:lo + CONV_RC, :] if r == 0 else sbuf[r - 1, lo:lo + CONV_RC, :]
            acc = acc + w_ref[k:k + 1, :] * win
        cbuf[r0:r0 + CONV_RC, :] = acc
    ubuf[0:halo, :] = ubuf[ts:ts + halo, :]

    a = cbuf[...]
    mu = jnp.mean(a, axis=-1, keepdims=True)
    ac = a - mu
    var = jnp.mean(ac * ac, axis=-1, keepdims=True)
    z = ac * lax.rsqrt(var + LN_EPS) * lg_ref[...] + lb_ref[...]
    z = z * _sigmoid(z)
    y_ref[0] = jnp.dot(z.astype(BF16), wo_ref[...], preferred_element_type=F32)


def _conv_branch(u3, w, b, lg, lb, wo):
    B, S, C = u3.shape
    D = wo.shape[1]
    ts = CONV_TS
    const = lambda bi, si: (0, 0)
    return pl.pallas_call(
        _conv_kernel,
        out_shape=jax.ShapeDtypeStruct((B, S, D), F32),
        grid=(B, S // ts),
        in_specs=[pl.BlockSpec((1, ts, C), lambda bi, si: (bi, si, 0)),
                  pl.BlockSpec(w.shape, const),
                  pl.BlockSpec((1, C), const),
                  pl.BlockSpec((1, C), const),
                  pl.BlockSpec((1, C), const),
                  pl.BlockSpec(wo.shape, const)],
        out_specs=pl.BlockSpec((1, ts, D), lambda bi, si: (bi, si, 0)),
        scratch_shapes=[pltpu.VMEM((ts + CONV_HALO, C), F32),
                        pltpu.VMEM((SUBLANES - 1, ts + CONV_HALO - SUBLANES, C), F32),
                        pltpu.VMEM((ts, C), F32)],
        compiler_params=_params("arbitrary", "arbitrary"),
        name="conv",
    )(u3, w, b, lg, lb, wo)


def _mlstm_kernel(qk_ref, v_ref, o_ref, ifc_ref, ifr_ref, cw_ref, cb_ref, bifc_ref, bifr_ref,
                  ng_ref, wo_ref, y_ref, qkbuf, cn_ref, m_ref, hbuf):
    L = qk_ref.shape[1]
    mi = v_ref.shape[2]
    dh = mi // M_HEADS
    halo = SUBLANES

    @pl.when(pl.program_id(1) == 0)
    def _():
        qkbuf[0:halo, :] = jnp.zeros((halo, qkbuf.shape[1]), F32)
        cn_ref[...] = jnp.zeros(cn_ref.shape, F32)
        m_ref[...] = jnp.zeros(m_ref.shape, F32)

    qkbuf[halo:halo + L, :] = qk_ref[0]
    off = halo - (QK_CONV_WIDTH - 1)
    y = jnp.broadcast_to(cb_ref[...], (L, qkbuf.shape[1]))
    for k in range(QK_CONV_WIDTH):
        y = y + cw_ref[k:k + 1, :] * qkbuf[off + k:off + k + L, :]
    y = y * _sigmoid(y)
    qkbuf[0:halo, :] = qkbuf[L:L + halo, :]

    ifr = ifr_ref[...] + bifr_ref[...]
    ifc = ifc_ref[...] + bifc_ref[...]
    lfr = _log_sigmoid(ifr)
    lfc = _log_sigmoid(ifc)
    rows = lax.broadcasted_iota(I32, (L, L), 0)
    cols = lax.broadcasted_iota(I32, (L, L), 1)
    causal = cols <= rows
    lower = causal.astype(F32)
    upper = (rows <= cols).astype(F32)
    bcum_c = jnp.dot(lower, lfc, preferred_element_type=F32, precision=lax.Precision.HIGHEST)
    bcum_r = jnp.dot(lfr, upper, preferred_element_type=F32, precision=lax.Precision.HIGHEST)

    lane = lax.broadcasted_iota(I32, (L, dh), 1)
    ones_col = jnp.where(lane == 0, 1.0, 0.0).astype(F32)
    vv = v_ref[0]
    oo = o_ref[0]
    scale = dh ** -0.5
    for hd in range(M_HEADS):
        q = y[:, hd * dh:(hd + 1) * dh] * scale
        kk = y[:, mi + hd * dh:mi + (hd + 1) * dh]
        v = vv[:, hd * dh:(hd + 1) * dh]
        kt = kk.T
        bc = bcum_c[:, M_HEADS + hd:M_HEADS + hd + 1]
        br = bcum_r[M_HEADS + hd:M_HEADS + hd + 1, :]
        li = ifr[hd:hd + 1, :]
        m_prev = m_ref[hd, 0:1, 0:1]
        dmat = jnp.where(causal, bc - br + li, -jnp.inf)
        inter = bc + m_prev
        m_t = jnp.maximum(jnp.max(dmat, axis=-1, keepdims=True), inter)
        wts = jnp.exp(dmat - m_t)
        s_inter = jnp.exp(inter - m_t)
        qb = q.astype(BF16)
        s_mat = jnp.dot(qb, kt.astype(BF16), preferred_element_type=F32) * wts
        cn = cn_ref[hd]
        qcn = jnp.dot(qb, cn.astype(BF16), preferred_element_type=F32)
        num = jnp.dot(s_mat.astype(BF16), v.astype(BF16), preferred_element_type=F32) \
            + s_inter * qcn[:, 0:dh]
        den = jnp.sum(s_mat, axis=-1, keepdims=True) + s_inter * qcn[:, dh:dh + 1]
        hh = num / jnp.maximum(jnp.abs(den), jnp.exp(-m_t))
        b_last = br[:, L - 1:L]
        a = b_last - br + li
        m_new = jnp.maximum(b_last + m_prev, jnp.max(a, axis=-1, keepdims=True))
        wk = jnp.exp(a - m_new)
        sc = jnp.exp(b_last + m_prev - m_new)
        v_ext = jnp.concatenate([v, ones_col], axis=1)
        cn_ref[hd] = sc * cn + jnp.dot((kt * wk).astype(BF16), v_ext.astype(BF16),
                                       preferred_element_type=F32)
        m_ref[hd] = jnp.broadcast_to(m_new, m_ref.shape[1:])
        mu = jnp.mean(hh, axis=-1, keepdims=True)
        hc = hh - mu
        var = jnp.mean(hc * hc, axis=-1, keepdims=True)
        hn = hc * lax.rsqrt(var + LN_EPS) * ng_ref[:, hd * dh:(hd + 1) * dh]
        hbuf[:, hd * dh:(hd + 1) * dh] = hn * _sigmoid(oo[:, hd * dh:(hd + 1) * dh])
    y_ref[0] = jnp.dot(hbuf[...].astype(BF16), wo_ref[...], preferred_element_type=F32)


def _mlstm_branch(qk3, v3, o3, ifc, ifr, cw, cb, bifc, bifr, ng, wo):
    B, S, C2 = qk3.shape
    mi = v3.shape[2]
    dh = mi // M_HEADS
    D = wo.shape[1]
    L = MLSTM_L
    nc = S // L
    const = lambda bi, ci: (0, 0)
    tile = lambda bi, ci: (bi, ci, 0)
    return pl.pallas_call(
        _mlstm_kernel,
        out_shape=jax.ShapeDtypeStruct((B, S, D), F32),
        grid=(B, nc),
        in_specs=[pl.BlockSpec((1, L, C2), tile),
                  pl.BlockSpec((1, L, mi), tile),
                  pl.BlockSpec((1, L, mi), tile),
                  pl.BlockSpec((L, LANES), lambda bi, ci: (bi * nc + ci, 0)),
                  pl.BlockSpec((SUBLANES, L), lambda bi, ci: (0, bi * nc + ci)),
                  pl.BlockSpec(cw.shape, const),
                  pl.BlockSpec((1, C2), const),
                  pl.BlockSpec((1, LANES), const),
                  pl.BlockSpec((SUBLANES, 1), const),
                  pl.BlockSpec((1, mi), const),
                  pl.BlockSpec(wo.shape, const)],
        out_specs=pl.BlockSpec((1, L, D), tile),
        scratch_shapes=[pltpu.VMEM((L + SUBLANES, C2), F32),
                        pltpu.VMEM((M_HEADS, dh, 2 * dh), F32),
                        pltpu.VMEM((M_HEADS, SUBLANES, LANES), F32),
                        pltpu.VMEM((L, mi), F32)],
        compiler_params=_params("arbitrary", "arbitrary"),
        name="mlstm",
    )(qk3, v3, o3, ifc, ifr, cw, cb, bifc, bifr, ng, wo)


def _merge_kernel(x_ref, ya_ref, yb_ref, sga_ref, sgb_ref, mod_ref, g2_ref, wo_ref, wr_ref, br_ref,
                  x1_ref, h2_ref, ri_ref, rf_ref, cnt_ref, run_ref):
    tm = x_ref.shape[0]

    @pl.when(pl.program_id(0) == 0)
    def _():
        run_ref[...] = jnp.zeros(run_ref.shape, F32)

    gate1 = mod_ref[0, 2:3, :]
    shift2 = mod_ref[0, 3:4, :]
    scale2 = mod_ref[0, 4:5, :]
    merged = sga_ref[...] * ya_ref[...] + sgb_ref[...] * yb_ref[...]
    mix = jnp.dot(merged.astype(BF16), wo_ref[...], preferred_element_type=F32)
    x1 = x_ref[...] + gate1 * mix
    x1_ref[...] = x1
    ms = jnp.mean(x1 * x1, axis=-1, keepdims=True)
    h2 = x1 * lax.rsqrt(ms + RMS_EPS) * g2_ref[...]
    h2 = h2 * (1.0 + scale2) + shift2
    h2_ref[...] = h2

    logits = jnp.dot(h2.astype(BF16), wr_ref[...], preferred_element_type=F32) + br_ref[...]
    lane = lax.broadcasted_iota(I32, (tm, LANES), 1).astype(F32)
    neg = -jnp.inf

    def first_argmax(vals):
        mx = jnp.max(vals, axis=-1, keepdims=True)
        idx = jnp.min(jnp.where(vals == mx, lane, float(LANES)), axis=-1, keepdims=True)
        return mx, idx

    lg = jnp.where(lane < N_GROUPS, logits, neg)
    gmax, gsel = first_argmax(lg)
    p_g = 1.0 / jnp.sum(jnp.exp(lg - gmax), axis=-1, keepdims=True)
    lo = N_GROUPS + gsel * E_PER_GROUP
    le = jnp.where((lane >= lo) & (lane < lo + E_PER_GROUP), logits, neg)
    l1, i1 = first_argmax(le)
    l2, i2 = first_argmax(jnp.where(lane == i1, neg, le))
    r = jnp.exp(l2 - l1)
    w1 = p_g / (1.0 + r)
    w2 = p_g * r / (1.0 + r)
    e1 = i1 - N_GROUPS
    e2 = i2 - N_GROUPS

    onehot = jnp.where((lane == e1) | (lane == e2), 1.0, 0.0)
    rows = lax.broadcasted_iota(I32, (tm, tm), 0)
    cols = lax.broadcasted_iota(I32, (tm, tm), 1)
    strict = jnp.where(cols < rows, 1.0, 0.0).astype(BF16)
    run = run_ref[0:1, :]
    before = jnp.dot(strict, onehot.astype(BF16), preferred_element_type=F32) + run
    rank1 = jnp.sum(jnp.where(lane == e1, before, 0.0), axis=-1, keepdims=True)
    rank2 = jnp.sum(jnp.where(lane == e2, before, 0.0), axis=-1, keepdims=True)
    run_new = run + jnp.sum(onehot, axis=0, keepdims=True)
    run_ref[...] = jnp.broadcast_to(run_new, run_ref.shape)
    cnt_ref[...] = jnp.broadcast_to(run_new, cnt_ref.shape).astype(I32)

    ri_ref[...] = jnp.where(lane == 0, e1 * float(RANK_RADIX) + rank1,
                            jnp.where(lane == 1, e2 * float(RANK_RADIX) + rank2, 0.0)).astype(I32)
    rf_ref[...] = jnp.where(lane == 0, w1, jnp.where(lane == 1, w2, 0.0))


def _merge(x2, ya, yb, sga, sgb, mod3, g2, wo, wr, br, seq):
    T, D = x2.shape
    tm = MERGE_TM
    per_b = seq // tm
    row = lambda i: (i, 0)
    const = lambda i: (0, 0)
    return pl.pallas_call(
        _merge_kernel,
        out_shape=[jax.ShapeDtypeStruct((T, D), F32),
                   jax.ShapeDtypeStruct((T, D), F32),
                   jax.ShapeDtypeStruct((T, LANES), I32),
                   jax.ShapeDtypeStruct((T, LANES), F32),
                   jax.ShapeDtypeStruct((SUBLANES, LANES), I32)],
        grid=(T // tm,),
        in_specs=[pl.BlockSpec((tm, D), row),
                  pl.BlockSpec((tm, D), row),
                  pl.BlockSpec((tm, D), row),
                  pl.BlockSpec((tm, D), row),
                  pl.BlockSpec((tm, D), row),
                  pl.BlockSpec((1, 6, D), lambda i: (i // per_b, 0, 0)),
                  pl.BlockSpec((1, D), const),
                  pl.BlockSpec(wo.shape, const),
                  pl.BlockSpec(wr.shape, const),
                  pl.BlockSpec((1, LANES), const)],
        out_specs=[pl.BlockSpec((tm, D), row),
                   pl.BlockSpec((tm, D), row),
                   pl.BlockSpec((tm, LANES), row),
                   pl.BlockSpec((tm, LANES), row),
                   pl.BlockSpec((SUBLANES, LANES), const)],
        scratch_shapes=[pltpu.VMEM((SUBLANES, LANES), F32)],
        compiler_params=_params("arbitrary"),
        name="merge",
    )(x2, ya, yb, sga, sgb, mod3, g2, wo, wr, br)


def _sc_workers():
    info = plsc.get_sparse_core_info()
    mesh = plsc.VectorSubcoreMesh(core_axis_name="core", subcore_axis_name="subcore")
    params = pltpu.CompilerParams()
    if "needs_layout_passes" in pltpu.CompilerParams.__dataclass_fields__:
        params = dataclasses.replace(params, needs_layout_passes=False)
    return info, mesh, params


def _rows_from_codes(code_v, base_v, idx_v, lanes):
    for j in range(code_v.shape[0] // lanes):
        c = code_v[pl.ds(j * lanes, lanes)]
        expert = lax.shift_right_logical(c, RANK_BITS)
        idx_v[pl.ds(j * lanes, lanes)] = plsc.load_gather(base_v, [expert]) + (c & (RANK_RADIX - 1))


def _two_slot_loop(n_chunks, start, finish):
    start(0, 0)

    @pl.loop(0, n_chunks, step=2)
    def _(c):
        start(c + 1, 1)
        finish(c, 0)

        @pl.when(c + 2 < n_chunks)
        def _():
            start(c + 2, 0)

        finish(c + 1, 1)


def _sc_dispatch(h2, code0, code1, base, n_rows):
    T, D = h2.shape
    info, mesh, params = _sc_workers()
    n_workers = info.num_cores * info.num_subcores
    w = SC_WINDOW
    per_w = T // n_workers
    n_chunks = per_w // w
    assert per_w * n_workers == T and n_chunks * w == per_w and n_chunks % 2 == 0

    @functools.partial(
        pl.kernel, out_type=jax.ShapeDtypeStruct((n_rows, D), h2.dtype), mesh=mesh, compiler_params=params,
        scratch_types=[pltpu.VMEM((N_EXPERTS,), I32), pltpu.VMEM((w,), I32), pltpu.VMEM((w,), I32),
                       pltpu.VMEM((w,), I32), pltpu.VMEM((w, D), h2.dtype), pltpu.VMEM((w, D), h2.dtype),
                       pltpu.SemaphoreType.DMA, pltpu.SemaphoreType.DMA])
    def scatter(h_hbm, c0_hbm, c1_hbm, b_hbm, xs_hbm, base_v, code_v, i0_v, i1_v, rows0, rows1, sem0, sem1):
        wid = lax.axis_index("subcore") * info.num_cores + lax.axis_index("core")
        w0 = wid * per_w
        pltpu.sync_copy(b_hbm, base_v)
        rows = (rows0, rows1)
        sems = (sem0, sem1)

        def start(c, slot):
            pltpu.async_copy(h_hbm.at[pl.ds(w0 + c * w, w)], rows[slot], sems[slot])

        def finish(c, slot):
            pltpu.sync_copy(c0_hbm.at[pl.ds(w0 + c * w, w)], code_v)
            _rows_from_codes(code_v, base_v, i0_v, info.num_lanes)
            pltpu.sync_copy(c1_hbm.at[pl.ds(w0 + c * w, w)], code_v)
            _rows_from_codes(code_v, base_v, i1_v, info.num_lanes)
            pltpu.make_async_copy(h_hbm.at[pl.ds(w0 + c * w, w)], rows[slot], sems[slot]).wait()
            pltpu.sync_copy(rows[slot], xs_hbm.at[i0_v])
            pltpu.sync_copy(rows[slot], xs_hbm.at[i1_v])

        _two_slot_loop(n_chunks, start, finish)

    return scatter(h2, code0, code1, base)


def _sc_collect(ys, codes, base):
    n = codes.shape[0]
    D = ys.shape[1]
    info, mesh, params = _sc_workers()
    n_workers = info.num_cores * info.num_subcores
    w = SC_WINDOW
    per_w = n // n_workers
    n_chunks = per_w // w
    assert per_w * n_workers == n and n_chunks * w == per_w and n_chunks % 2 == 0

    @functools.partial(
        pl.kernel, out_type=jax.ShapeDtypeStruct((n, D), ys.dtype), mesh=mesh, compiler_params=params,
        scratch_types=[pltpu.VMEM((N_EXPERTS,), I32), pltpu.VMEM((w,), I32), pltpu.VMEM((w,), I32),
                       pltpu.VMEM((w,), I32), pltpu.VMEM((w, D), ys.dtype), pltpu.VMEM((w, D), ys.dtype),
                       pltpu.SemaphoreType.DMA, pltpu.SemaphoreType.DMA])
    def gather(ys_hbm, c_hbm, b_hbm, yk_hbm, base_v, code_v, i0_v, i1_v, rows0, rows1, sem0, sem1):
        wid = lax.axis_index("subcore") * info.num_cores + lax.axis_index("core")
        w0 = wid * per_w
        pltpu.sync_copy(b_hbm, base_v)
        idx = (i0_v, i1_v)
        rows = (rows0, rows1)
        sems = (sem0, sem1)

        def start(c, slot):
            pltpu.sync_copy(c_hbm.at[pl.ds(w0 + c * w, w)], code_v)
            _rows_from_codes(code_v, base_v, idx[slot], info.num_lanes)
            pltpu.async_copy(ys_hbm.at[idx[slot]], rows[slot], sems[slot])

        def finish(c, slot):
            pltpu.make_async_copy(ys_hbm.at[idx[slot]], rows[slot], sems[slot]).wait()
            pltpu.sync_copy(rows[slot], yk_hbm.at[pl.ds(w0 + c * w, w)])

        _two_slot_loop(n_chunks, start, finish)

    return gather(ys, codes, base)


def _expert_kernel(te_ref, tb_ref, first_ref, nt_ref, xs_ref, wg_ref, wu_ref, wd_ref, ys_ref,
                   wgb, wub, wdb):
    j = pl.program_id(0)

    @pl.when(j < nt_ref[0])
    def _():
        @pl.when(first_ref[j] == 1)
        def _():
            wgb[...] = wg_ref[0].astype(BF16)
            wub[...] = wu_ref[0].astype(BF16)
            wdb[...] = wd_ref[0].astype(BF16)

        xb = xs_ref[...].astype(BF16)
        g = jnp.dot(xb, wgb[...], preferred_element_type=F32)
        u = jnp.dot(xb, wub[...], preferred_element_type=F32)
        act = (g * _sigmoid(g)) * u
        ys_ref[...] = jnp.dot(act.astype(BF16), wdb[...], preferred_element_type=F32)


def _experts(tile_e, tile_b, tile_first, n_tiles, xs, wg, wu, wd, max_tiles):
    P, D = xs.shape
    de = wg.shape[2]
    tm = EXPERT_TM
    wmap = lambda j, te, tb, tf, nt: (te[j], 0, 0)
    rmap = lambda j, te, tb, tf, nt: (tb[j], 0)
    return pl.pallas_call(
        _expert_kernel,
        out_shape=jax.ShapeDtypeStruct((P, D), F32),
        grid_spec=pltpu.PrefetchScalarGridSpec(
            num_scalar_prefetch=4,
            grid=(max_tiles,),
            in_specs=[pl.BlockSpec((tm, D), rmap),
                      pl.BlockSpec((1, D, de), wmap),
                      pl.BlockSpec((1, D, de), wmap),
                      pl.BlockSpec((1, de, D), wmap)],
            out_specs=pl.BlockSpec((tm, D), rmap),
            scratch_shapes=[pltpu.VMEM((D, de), BF16),
                            pltpu.VMEM((D, de), BF16),
                            pltpu.VMEM((de, D), BF16)]),
        compiler_params=_params("arbitrary"),
        name="experts",
    )(tile_e, tile_b, tile_first, n_tiles, xs, wg, wu, wd)


def _combine_kernel(x1_ref, rf_ref, mod_ref, gf_ref, y0_ref, y1_ref, *rest, final_norm):
    out_ref = rest[-1]
    gate2 = mod_ref[0, 5:6, :]
    w = rf_ref[...]
    moe = w[:, 0:1] * y0_ref[...] + w[:, 1:2] * y1_ref[...]
    x2 = x1_ref[...] + gate2 * moe
    if final_norm:
        ms = jnp.mean(x2 * x2, axis=-1, keepdims=True)
        x2 = x2 * lax.rsqrt(ms + RMS_EPS) * gf_ref[...]
    out_ref[...] = x2


def _combine(x1, rf, mod3, gf, yk, seq, final_norm, part, n_parts, prev_out):
    T, D = x1.shape
    tc = COMBINE_TM
    per_b = seq // tc
    n_blk = T // tc // n_parts
    b0 = part * n_blk
    row = lambda i: (b0 + i, 0)
    in_specs = [pl.BlockSpec((tc, D), row),
                pl.BlockSpec((tc, LANES), row),
                pl.BlockSpec((1, 6, D), lambda i: ((b0 + i) // per_b, 0, 0)),
                pl.BlockSpec((1, D), lambda i: (0, 0)),
                pl.BlockSpec((tc, D), lambda i: (i, 0)),
                pl.BlockSpec((tc, D), lambda i: (n_blk + i, 0))]
    args = [x1, rf, mod3, gf, yk, yk]
    aliases = {}
    if prev_out is not None:
        in_specs.append(pl.BlockSpec(memory_space=pl.ANY))
        args.append(prev_out)
        aliases = {len(args) - 1: 0}
    return pl.pallas_call(
        functools.partial(_combine_kernel, final_norm=final_norm),
        out_shape=jax.ShapeDtypeStruct((T, D), F32),
        grid=(n_blk,),
        in_specs=in_specs,
        out_specs=pl.BlockSpec((tc, D), row),
        input_output_aliases=aliases,
        compiler_params=_params("arbitrary"),
        name="combine",
    )(*args)


def _layer(x2, c, seq, w_ada, b_ada, g_norm1, w_in, b_if, conv_dw_w, conv_dw_b, conv_ln_g, conv_ln_b,
           w_conv_out, qk_conv_w, qk_conv_b, m_norm_g, w_m_out, w_out, g_norm2, w_rg, b_rg,
           w_re, b_re, w_e_gate, w_e_up, w_e_down):
    T, D = x2.shape
    B = T // seq
    dc = D // 2
    nif = 2 * M_HEADS

    mod3 = _ada(c, w_ada, b_ada).reshape(B, 6, D)

    if_lo = 6 * dc
    w_main = jnp.concatenate([w_in[:, :if_lo], w_in[:, if_lo + nif:]], axis=1).astype(BF16)
    w_if = w_in[:, if_lo:if_lo + nif]
    w_if_pad = jnp.pad(w_if, ((0, 0), (0, LANES - nif))).astype(BF16)
    w_ift = w_if.T.astype(BF16)
    u, qk, v, o, sga, sgb, ifc, ifr = _inproj(x2, mod3, g_norm1.reshape(1, D), w_main, w_if_pad, w_ift, seq)

    ya = _conv_branch(u.reshape(B, seq, dc), conv_dw_w, conv_dw_b.reshape(1, dc),
                      conv_ln_g.reshape(1, dc), conv_ln_b.reshape(1, dc), w_conv_out.astype(BF16))
    bifc = jnp.pad(b_if, (0, LANES - nif)).reshape(1, LANES)
    bifr = b_if.reshape(nif, 1)
    yb = _mlstm_branch(qk.reshape(B, seq, 2 * dc), v.reshape(B, seq, dc), o.reshape(B, seq, dc),
                       ifc, ifr, qk_conv_w, qk_conv_b.reshape(1, 2 * dc), bifc, bifr,
                       m_norm_g.reshape(1, dc), w_m_out.astype(BF16))

    n_r = N_GROUPS + N_EXPERTS
    w_r = jnp.pad(jnp.concatenate([w_rg, w_re], axis=1), ((0, 0), (0, LANES - n_r))).astype(BF16)
    b_r = jnp.pad(jnp.concatenate([b_rg, b_re]), (0, LANES - n_r)).reshape(1, LANES)
    x1, h2, ri, rf, cnt = _merge(x2, ya.reshape(T, D), yb.reshape(T, D), sga, sgb, mod3,
                                 g_norm2.reshape(1, D), w_out.astype(BF16), w_r, b_r, seq)

    tm = EXPERT_TM
    counts = cnt[0, :N_EXPERTS]
    tiles_e = (counts + tm - 1) // tm
    tile_end = jnp.cumsum(tiles_e)
    n_tiles = tile_end[-1]
    max_tiles = (T * TOP_K) // tm + N_EXPERTS
    jt = jnp.arange(max_tiles, dtype=I32)
    jc = jnp.minimum(jt, n_tiles - 1)
    tile_e = jnp.sum(jc[:, None] >= tile_end[None, :], axis=1).astype(I32)
    tile_first = ((jt < n_tiles) & (jc == (tile_end - tiles_e)[tile_e])).astype(I32)
    base = ((tile_end - tiles_e) * tm).astype(I32)

    code0 = ri[:, 0]
    code1 = ri[:, 1]
    xs = _sc_dispatch(h2, code0, code1, base, max_tiles * tm)
    ys = _experts(tile_e, jc, tile_first, n_tiles.reshape(1).astype(I32), xs,
                  w_e_gate, w_e_up, w_e_down, max_tiles)
    return x1, rf, mod3, ys, code0, code1, base


def kernel(x, c, w_ada, b_ada, g_norm1, w_in, b_if, conv_dw_w, conv_dw_b, conv_ln_g, conv_ln_b,
           w_conv_out, qk_conv_w, qk_conv_b, m_norm_g, w_m_out, w_out, g_norm2, w_rg, b_rg,
           w_re, b_re, w_e_gate, w_e_up, w_e_down, g_final):
    B, S, D = x.shape
    depth = w_ada.shape[0]
    x2 = x.reshape(B * S, D)
    for l in range(depth):
        x1, rf, mod3, ys, code0, code1, base = _layer(
            x2, c, S, w_ada[l], b_ada[l], g_norm1[l], w_in[l], b_if[l], conv_dw_w[l], conv_dw_b[l],
            conv_ln_g[l], conv_ln_b[l], w_conv_out[l], qk_conv_w[l], qk_conv_b[l], m_norm_g[l],
            w_m_out[l], w_out[l], g_norm2[l], w_rg[l], b_rg[l], w_re[l], b_re[l],
            w_e_gate[l], w_e_up[l], w_e_down[l])
        x2 = None
        tp = B * S // COLLECT_PARTS
        for p in range(COLLECT_PARTS):
            codes = jnp.concatenate([code0[p * tp:(p + 1) * tp], code1[p * tp:(p + 1) * tp]])
            yk = _sc_collect(ys, codes, base)
            x2 = _combine(x1, rf, mod3, g_final.reshape(1, D), yk, S, l == depth - 1,
                          p, COLLECT_PARTS, x2)
    return x2.reshape(B, S, D)
```

```python
import dataclasses
import functools

import jax
import jax.numpy as jnp
from jax import lax
from jax.experimental import pallas as pl
from jax.experimental.pallas import tpu as pltpu
from jax.experimental.pallas import tpu_sc as plsc

F32 = jnp.float32
BF16 = jnp.bfloat16
I32 = jnp.int32

M_HEADS = 4
CONV_WIDTH = 31
QK_CONV_WIDTH = 4
N_GROUPS = 4
E_PER_GROUP = 8
N_EXPERTS = N_GROUPS * E_PER_GROUP
TOP_K = 2
RMS_EPS = 1e-6
LN_EPS = 1e-5

LANES = 128
SUBLANES = 8
VMEM_LIMIT = 56 * 1024 * 1024

ADA_TN = 1024
INPROJ_TM = 256
CONV_TS = 256
CONV_HALO = 32
CONV_RC = 32
MLSTM_L = 128
MLSTM_SEQS = 2
MERGE_TM = 256
EXPERT_TM = 512
SC_WINDOW_BYTES = 128 * 1024
COMBINE_TM = 256
RANK_BITS = 16
RANK_RADIX = 1 << RANK_BITS
assert EXPERT_TM & (EXPERT_TM - 1) == 0


def _sigmoid(v):
    return 1.0 / (1.0 + jnp.exp(-v))


def _log_sigmoid(v):
    return -(jnp.maximum(-v, 0.0) + jnp.log1p(jnp.exp(-jnp.abs(v))))


def _pack_bf16_pairs(v):
    n = v.shape[1] // 2
    bits = lax.bitcast_convert_type(v.astype(BF16).astype(F32), jnp.uint32)
    word = bits[:, n:] | (bits[:, :n] >> 16)
    return lax.bitcast_convert_type(word, I32)


def _unpack_bf16_pairs(w):
    bits = lax.bitcast_convert_type(w, jnp.uint32)
    lo = lax.bitcast_convert_type(bits << 16, F32)
    hi = lax.bitcast_convert_type(bits & jnp.uint32(0xFFFF0000), F32)
    return jnp.concatenate([lo, hi], axis=1).astype(BF16)


def _params(*sem):
    return pltpu.CompilerParams(dimension_semantics=sem, vmem_limit_bytes=VMEM_LIMIT)


def _ada_kernel(c_ref, w_ref, b_ref, o_ref):
    c = c_ref[...]
    s = c * _sigmoid(c)
    o_ref[...] = jnp.dot(s, w_ref[...], preferred_element_type=F32,
                         precision=lax.Precision.HIGHEST) + b_ref[...]


def _ada(c, w_ada, b_ada):
    B, D = c.shape
    N = w_ada.shape[1]
    return pl.pallas_call(
        _ada_kernel,
        out_shape=jax.ShapeDtypeStruct((B, N), F32),
        grid=(N // ADA_TN,),
        in_specs=[pl.BlockSpec((B, D), lambda j: (0, 0)),
                  pl.BlockSpec((D, ADA_TN), lambda j: (0, j)),
                  pl.BlockSpec((1, ADA_TN), lambda j: (0, j))],
        out_specs=pl.BlockSpec((B, ADA_TN), lambda j: (0, j)),
        compiler_params=_params("arbitrary"),
        name="ada",
    )(c, w_ada, b_ada.reshape(1, N))


def _inproj_kernel(x_ref, mod_ref, g_ref, wm_ref, wif_ref, wift_ref,
                   u_ref, qk_ref, v_ref, o_ref, sga_ref, sgb_ref, ifc_ref, ifr_ref):
    x = x_ref[...]
    shift = mod_ref[0, 0:1, :]
    scale = mod_ref[0, 1:2, :]
    ms = jnp.mean(x * x, axis=-1, keepdims=True)
    h = x * lax.rsqrt(ms + RMS_EPS) * g_ref[...]
    h = h * (1.0 + scale) + shift
    hb = h.astype(BF16)
    dc = u_ref.shape[1]
    d = sga_ref.shape[1]

    def seg(lo, hi):
        return jnp.dot(hb, wm_ref[:, lo:hi], preferred_element_type=F32)

    u_ref[...] = seg(0, dc) * _sigmoid(seg(dc, 2 * dc))
    qk_ref[...] = seg(2 * dc, 4 * dc)
    v_ref[...] = seg(4 * dc, 5 * dc)
    o_ref[...] = seg(5 * dc, 6 * dc)
    sga_ref[...] = _sigmoid(seg(6 * dc, 6 * dc + d))
    sgb_ref[...] = _sigmoid(seg(6 * dc + d, 6 * dc + 2 * d))
    ifc_ref[...] = jnp.dot(hb, wif_ref[...], preferred_element_type=F32)
    ifr_ref[0] = lax.dot_general(wift_ref[...], hb, (((1,), (1,)), ((), ())),
                                 preferred_element_type=F32)


def _inproj(x2, mod3, g1, w_main, w_if, w_ift, seq):
    T, D = x2.shape
    tm = INPROJ_TM
    dc = D // 2
    per_b = seq // tm
    row = lambda i: (i, 0)
    const = lambda i: (0, 0)
    return pl.pallas_call(
        _inproj_kernel,
        out_shape=[jax.ShapeDtypeStruct((T, dc), F32),
                   jax.ShapeDtypeStruct((T, 2 * dc), F32),
                   jax.ShapeDtypeStruct((T, dc), F32),
                   jax.ShapeDtypeStruct((T, dc), F32),
                   jax.ShapeDtypeStruct((T, D), F32),
                   jax.ShapeDtypeStruct((T, D), F32),
                   jax.ShapeDtypeStruct((T, LANES), F32),
                   jax.ShapeDtypeStruct((T // seq, SUBLANES, seq), F32)],
        grid=(T // tm,),
        in_specs=[pl.BlockSpec((tm, D), row),
                  pl.BlockSpec((1, 6, D), lambda i: (i // per_b, 0, 0)),
                  pl.BlockSpec((1, D), const),
                  pl.BlockSpec(w_main.shape, const),
                  pl.BlockSpec(w_if.shape, const),
                  pl.BlockSpec(w_ift.shape, const)],
        out_specs=[pl.BlockSpec((tm, dc), row),
                   pl.BlockSpec((tm, 2 * dc), row),
                   pl.BlockSpec((tm, dc), row),
                   pl.BlockSpec((tm, dc), row),
                   pl.BlockSpec((tm, D), row),
                   pl.BlockSpec((tm, D), row),
                   pl.BlockSpec((tm, LANES), row),
                   pl.BlockSpec((1, SUBLANES, tm), lambda i: (i // per_b, 0, i % per_b))],
        compiler_params=_params("arbitrary"),
        name="inproj",
    )(x2, mod3, g1, w_main, w_if, w_ift)


def _conv_kernel(u_ref, w_ref, b_ref, lg_ref, lb_ref, wo_ref, y_ref, ubuf, sbuf, cbuf):
    ts = u_ref.shape[1]
    halo = CONV_HALO

    @pl.when(pl.program_id(1) == 0)
    def _():
        ubuf[0:halo, :] = jnp.zeros((halo, ubuf.shape[1]), F32)

    ubuf[halo:halo + ts, :] = u_ref[0]
    ns = sbuf.shape[1]
    for r in range(1, SUBLANES):
        sbuf[r - 1] = ubuf[r:r + ns, :]
    off = halo - (CONV_WIDTH - 1)
    for r0 in range(0, ts, CONV_RC):
        acc = jnp.broadcast_to(b_ref[...], (CONV_RC, ubuf.shape[1]))
        for k in range(CONV_WIDTH):
            r = (off + k) % SUBLANES
            lo = off + k - r + r0
            win = ubuf[lo:lo + CONV_RC, :] if r == 0 else sbuf[r - 1, lo:lo + CONV_RC, :]
            acc = acc + w_ref[k:k + 1, :] * win
        cbuf[r0:r0 + CONV_RC, :] = acc
    ubuf[0:halo, :] = ubuf[ts:ts + halo, :]

    a = cbuf[...]
    mu = jnp.mean(a, axis=-1, keepdims=True)
    ac = a - mu
    var = jnp.mean(ac * ac, axis=-1, keepdims=True)
    z = ac * lax.rsqrt(var + LN_EPS) * lg_ref[...] + lb_ref[...]
    z = z * _sigmoid(z)
    y_ref[0] = jnp.dot(z.astype(BF16), wo_ref[...], preferred_element_type=F32)


def _conv_branch(u3, w, b, lg, lb, wo):
    B, S, C = u3.shape
    D = wo.shape[1]
    ts = CONV_TS
    const = lambda bi, si: (0, 0)
    return pl.pallas_call(
        _conv_kernel,
        out_shape=jax.ShapeDtypeStruct((B, S, D), F32),
        grid=(B, S // ts),
        in_specs=[pl.BlockSpec((1, ts, C), lambda bi, si: (bi, si, 0)),
                  pl.BlockSpec(w.shape, const),
                  pl.BlockSpec((1, C), const),
                  pl.BlockSpec((1, C), const),
                  pl.BlockSpec((1, C), const),
                  pl.BlockSpec(wo.shape, const)],
        out_specs=pl.BlockSpec((1, ts, D), lambda bi, si: (bi, si, 0)),
        scratch_shapes=[pltpu.VMEM((ts + CONV_HALO, C), F32),
                        pltpu.VMEM((SUBLANES - 1, ts + CONV_HALO - SUBLANES, C), F32),
                        pltpu.VMEM((ts, C), F32)],
        compiler_params=_params("arbitrary", "arbitrary"),
        name="conv",
    )(u3, w, b, lg, lb, wo)


def _mlstm_kernel(qk_ref, v_ref, o_ref, ifc_ref, ifr_ref, cw_ref, cb_ref, bifc_ref, bifr_ref,
                  ng_ref, wo_ref, y_ref, qkbuf, cn_ref, m_ref, hbuf):
    @pl.when(pl.program_id(1) == 0)
    def _():
        qkbuf[:, 0:SUBLANES, :] = jnp.zeros((qkbuf.shape[0], SUBLANES, qkbuf.shape[2]), F32)
        cn_ref[...] = jnp.zeros(cn_ref.shape, F32)
        m_ref[...] = jnp.zeros(m_ref.shape, F32)

    for b in range(qk_ref.shape[0]):
        _mlstm_chunk(qk_ref.at[b], v_ref.at[b], o_ref.at[b], ifc_ref.at[b], ifr_ref.at[b], cw_ref, cb_ref,
                     bifc_ref, bifr_ref, ng_ref, wo_ref, y_ref.at[b], qkbuf.at[b], cn_ref.at[b],
                     m_ref.at[b], hbuf.at[b])


def _mlstm_chunk(qk_ref, v_ref, o_ref, ifc_ref, ifr_ref, cw_ref, cb_ref, bifc_ref, bifr_ref,
                 ng_ref, wo_ref, y_ref, qkbuf, cn_ref, m_ref, hbuf):
    L = qk_ref.shape[0]
    mi = v_ref.shape[1]
    dh = mi // M_HEADS
    halo = SUBLANES

    qkbuf[halo:halo + L, :] = qk_ref[...]
    off = halo - (QK_CONV_WIDTH - 1)
    y = jnp.broadcast_to(cb_ref[...], (L, qkbuf.shape[1]))
    for k in range(QK_CONV_WIDTH):
        y = y + cw_ref[k:k + 1, :] * qkbuf[off + k:off + k + L, :]
    y = y * _sigmoid(y)
    qkbuf[0:halo, :] = qkbuf[L:L + halo, :]

    ifr = ifr_ref[...] + bifr_ref[...]
    ifc = ifc_ref[...] + bifc_ref[...]
    lfr = _log_sigmoid(ifr)
    lfc = _log_sigmoid(ifc)
    rows = lax.broadcasted_iota(I32, (L, L), 0)
    cols = lax.broadcasted_iota(I32, (L, L), 1)
    causal = cols <= rows
    lower = causal.astype(F32)
    upper = (rows <= cols).astype(F32)
    bcum_c = jnp.dot(lower, lfc, preferred_element_type=F32, precision=lax.Precision.HIGHEST)
    bcum_r = jnp.dot(lfr, upper, preferred_element_type=F32, precision=lax.Precision.HIGHEST)

    lane = lax.broadcasted_iota(I32, (L, dh), 1)
    ones_col = jnp.where(lane == 0, 1.0, 0.0).astype(F32)
    vv = v_ref[...]
    oo = o_ref[...]
    scale = dh ** -0.5
    for hd in range(M_HEADS):
        q = y[:, hd * dh:(hd + 1) * dh] * scale
        kk = y[:, mi + hd * dh:mi + (hd + 1) * dh]
        v = vv[:, hd * dh:(hd + 1) * dh]
        kt = kk.T
        bc = bcum_c[:, M_HEADS + hd:M_HEADS + hd + 1]
        br = bcum_r[M_HEADS + hd:M_HEADS + hd + 1, :]
        li = ifr[hd:hd + 1, :]
        m_prev = m_ref[hd, 0:1, 0:1]
        dmat = jnp.where(causal, bc - br + li, -jnp.inf)
        inter = bc + m_prev
        m_t = jnp.maximum(jnp.max(dmat, axis=-1, keepdims=True), inter)
        wts = jnp.exp(dmat - m_t)
        s_inter = jnp.exp(inter - m_t)
        qb = q.astype(BF16)
        s_mat = jnp.dot(qb, kt.astype(BF16), preferred_element_type=F32) * wts
        cn = cn_ref[hd]
        qcn = jnp.dot(qb, cn.astype(BF16), preferred_element_type=F32)
        num = jnp.dot(s_mat.astype(BF16), v.astype(BF16), preferred_element_type=F32) \
            + s_inter * qcn[:, 0:dh]
        den = jnp.sum(s_mat, axis=-1, keepdims=True) + s_inter * qcn[:, dh:dh + 1]
        hh = num / jnp.maximum(jnp.abs(den), jnp.exp(-m_t))
        b_last = br[:, L - 1:L]
        a = b_last - br + li
        m_new = jnp.maximum(b_last + m_prev, jnp.max(a, axis=-1, keepdims=True))
        wk = jnp.exp(a - m_new)
        sc = jnp.exp(b_last + m_prev - m_new)
        v_ext = jnp.concatenate([v, ones_col], axis=1)
        cn_ref[hd] = sc * cn + jnp.dot((kt * wk).astype(BF16), v_ext.astype(BF16),
                                       preferred_element_type=F32)
        m_ref[hd] = jnp.broadcast_to(m_new, m_ref.shape[1:])
        mu = jnp.mean(hh, axis=-1, keepdims=True)
        hc = hh - mu
        var = jnp.mean(hc * hc, axis=-1, keepdims=True)
        hn = hc * lax.rsqrt(var + LN_EPS) * ng_ref[:, hd * dh:(hd + 1) * dh]
        hbuf[:, hd * dh:(hd + 1) * dh] = hn * _sigmoid(oo[:, hd * dh:(hd + 1) * dh])
    y_ref[...] = jnp.dot(hbuf[...].astype(BF16), wo_ref[...], preferred_element_type=F32)


def _mlstm_branch(qk3, v3, o3, ifc3, ifr3, cw, cb, bifc, bifr, ng, wo):
    B, S, C2 = qk3.shape
    mi = v3.shape[2]
    dh = mi // M_HEADS
    D = wo.shape[1]
    L = MLSTM_L
    nb = MLSTM_SEQS
    const = lambda bi, ci: (0, 0)
    tile = lambda bi, ci: (bi, ci, 0)
    return pl.pallas_call(
        _mlstm_kernel,
        out_shape=jax.ShapeDtypeStruct((B, S, D), F32),
        grid=(B // nb, S // L),
        in_specs=[pl.BlockSpec((nb, L, C2), tile),
                  pl.BlockSpec((nb, L, mi), tile),
                  pl.BlockSpec((nb, L, mi), tile),
                  pl.BlockSpec((nb, L, LANES), tile),
                  pl.BlockSpec((nb, SUBLANES, L), lambda bi, ci: (bi, 0, ci)),
                  pl.BlockSpec(cw.shape, const),
                  pl.BlockSpec((1, C2), const),
                  pl.BlockSpec((1, LANES), const),
                  pl.BlockSpec((SUBLANES, 1), const),
                  pl.BlockSpec((1, mi), const),
                  pl.BlockSpec(wo.shape, const)],
        out_specs=pl.BlockSpec((nb, L, D), tile),
        scratch_shapes=[pltpu.VMEM((nb, L + SUBLANES, C2), F32),
                        pltpu.VMEM((nb, M_HEADS, dh, 2 * dh), F32),
                        pltpu.VMEM((nb, M_HEADS, SUBLANES, LANES), F32),
                        pltpu.VMEM((nb, L, mi), F32)],
        compiler_params=_params("arbitrary", "arbitrary"),
        name="mlstm",
    )(qk3, v3, o3, ifc3, ifr3, cw, cb, bifc, bifr, ng, wo)


def _merge_kernel(x_ref, ya_ref, yb_ref, sga_ref, sgb_ref, mod_ref, g2_ref, wo_ref, wr_ref, br_ref,
                  x1_ref, h2_ref, ri_ref, rf_ref, cnt_ref, run_ref):
    tm = x_ref.shape[0]

    @pl.when(pl.program_id(0) == 0)
    def _():
        run_ref[...] = jnp.zeros(run_ref.shape, F32)

    gate1 = mod_ref[0, 2:3, :]
    shift2 = mod_ref[0, 3:4, :]
    scale2 = mod_ref[0, 4:5, :]
    merged = sga_ref[...] * ya_ref[...] + sgb_ref[...] * yb_ref[...]
    mix = jnp.dot(merged.astype(BF16), wo_ref[...], preferred_element_type=F32)
    x1 = x_ref[...] + gate1 * mix
    x1_ref[...] = x1
    ms = jnp.mean(x1 * x1, axis=-1, keepdims=True)
    h2 = x1 * lax.rsqrt(ms + RMS_EPS) * g2_ref[...]
    h2 = h2 * (1.0 + scale2) + shift2
    h2_ref[...] = _pack_bf16_pairs(h2)

    logits = jnp.dot(h2.astype(BF16), wr_ref[...], preferred_element_type=F32) + br_ref[...]
    lane = lax.broadcasted_iota(I32, (tm, LANES), 1).astype(F32)
    neg = -jnp.inf

    def first_argmax(vals):
        mx = jnp.max(vals, axis=-1, keepdims=True)
        idx = jnp.min(jnp.where(vals == mx, lane, float(LANES)), axis=-1, keepdims=True)
        return mx, idx

    lg = jnp.where(lane < N_GROUPS, logits, neg)
    gmax, gsel = first_argmax(lg)
    p_g = 1.0 / jnp.sum(jnp.exp(lg - gmax), axis=-1, keepdims=True)
    lo = N_GROUPS + gsel * E_PER_GROUP
    le = jnp.where((lane >= lo) & (lane < lo + E_PER_GROUP), logits, neg)
    l1, i1 = first_argmax(le)
    l2, i2 = first_argmax(jnp.where(lane == i1, neg, le))
    r = jnp.exp(l2 - l1)
    w1 = p_g / (1.0 + r)
    w2 = p_g * r / (1.0 + r)
    e1 = i1 - N_GROUPS
    e2 = i2 - N_GROUPS

    onehot = jnp.where((lane == e1) | (lane == e2), 1.0, 0.0)
    rows = lax.broadcasted_iota(I32, (tm, tm), 0)
    cols = lax.broadcasted_iota(I32, (tm, tm), 1)
    strict = jnp.where(cols < rows, 1.0, 0.0).astype(BF16)
    run = run_ref[0:1, :]
    before = jnp.dot(strict, onehot.astype(BF16), preferred_element_type=F32) + run
    rank1 = jnp.sum(jnp.where(lane == e1, before, 0.0), axis=-1, keepdims=True)
    rank2 = jnp.sum(jnp.where(lane == e2, before, 0.0), axis=-1, keepdims=True)
    run_new = run + jnp.sum(onehot, axis=0, keepdims=True)
    run_ref[...] = jnp.broadcast_to(run_new, run_ref.shape)
    cnt_ref[...] = jnp.broadcast_to(run_new, cnt_ref.shape).astype(I32)

    ri_ref[...] = jnp.where(lane == 0, e1 * float(RANK_RADIX) + rank1,
                            jnp.where(lane == 1, e2 * float(RANK_RADIX) + rank2, 0.0)).astype(I32)
    rf_ref[...] = jnp.where(lane == 0, w1, jnp.where(lane == 1, w2, 0.0))


def _merge(x2, ya, yb, sga, sgb, mod3, g2, wo, wr, br, seq):
    T, D = x2.shape
    tm = MERGE_TM
    per_b = seq // tm
    row = lambda i: (i, 0)
    const = lambda i: (0, 0)
    return pl.pallas_call(
        _merge_kernel,
        out_shape=[jax.ShapeDtypeStruct((T, D), F32),
                   jax.ShapeDtypeStruct((T, D // 2), I32),
                   jax.ShapeDtypeStruct((T, LANES), I32),
                   jax.ShapeDtypeStruct((T, LANES), F32),
                   jax.ShapeDtypeStruct((SUBLANES, LANES), I32)],
        grid=(T // tm,),
        in_specs=[pl.BlockSpec((tm, D), row),
                  pl.BlockSpec((tm, D), row),
                  pl.BlockSpec((tm, D), row),
                  pl.BlockSpec((tm, D), row),
                  pl.BlockSpec((tm, D), row),
                  pl.BlockSpec((1, 6, D), lambda i: (i // per_b, 0, 0)),
                  pl.BlockSpec((1, D), const),
                  pl.BlockSpec(wo.shape, const),
                  pl.BlockSpec(wr.shape, const),
                  pl.BlockSpec((1, LANES), const)],
        out_specs=[pl.BlockSpec((tm, D), row),
                   pl.BlockSpec((tm, D // 2), row),
                   pl.BlockSpec((tm, LANES), row),
                   pl.BlockSpec((tm, LANES), row),
                   pl.BlockSpec((SUBLANES, LANES), const)],
        scratch_shapes=[pltpu.VMEM((SUBLANES, LANES), F32)],
        compiler_params=_params("arbitrary"),
        name="merge",
    )(x2, ya, yb, sga, sgb, mod3, g2, wo, wr, br)


def _sc_workers():
    info = plsc.get_sparse_core_info()
    mesh = plsc.VectorSubcoreMesh(core_axis_name="core", subcore_axis_name="subcore")
    params = pltpu.CompilerParams()
    if "needs_layout_passes" in pltpu.CompilerParams.__dataclass_fields__:
        params = dataclasses.replace(params, needs_layout_passes=False)
    return info, mesh, params


def _rows_from_codes(code_v, base_v, idx_v, lanes):
    for j in range(code_v.shape[0] // lanes):
        c = code_v[pl.ds(j * lanes, lanes)]
        expert = lax.shift_right_logical(c, RANK_BITS)
        idx_v[pl.ds(j * lanes, lanes)] = plsc.load_gather(base_v, [expert]) + (c & (RANK_RADIX - 1))


def _two_slot_loop(n_chunks, start, finish):
    start(0, 0)

    @pl.loop(0, n_chunks, step=2)
    def _(c):
        start(c + 1, 1)
        finish(c, 0)

        @pl.when(c + 2 < n_chunks)
        def _():
            start(c + 2, 0)

        finish(c + 1, 1)


def _sc_dispatch(h2, code0, code1, base, n_rows):
    T, D = h2.shape
    info, mesh, params = _sc_workers()
    n_workers = info.num_cores * info.num_subcores
    w = SC_WINDOW_BYTES // (D * h2.dtype.itemsize)
    per_w = T // n_workers
    n_chunks = per_w // w
    assert per_w * n_workers == T and n_chunks * w == per_w and n_chunks % 2 == 0

    @functools.partial(
        pl.kernel, out_type=jax.ShapeDtypeStruct((n_rows, D), h2.dtype), mesh=mesh, compiler_params=params,
        scratch_types=[pltpu.VMEM((N_EXPERTS,), I32), pltpu.VMEM((w,), I32), pltpu.VMEM((w,), I32),
                       pltpu.VMEM((w,), I32), pltpu.VMEM((w, D), h2.dtype), pltpu.VMEM((w, D), h2.dtype),
                       pltpu.SemaphoreType.DMA, pltpu.SemaphoreType.DMA])
    def scatter(h_hbm, c0_hbm, c1_hbm, b_hbm, xs_hbm, base_v, code_v, i0_v, i1_v, rows0, rows1, sem0, sem1):
        wid = lax.axis_index("subcore") * info.num_cores + lax.axis_index("core")
        w0 = wid * per_w
        pltpu.sync_copy(b_hbm, base_v)
        rows = (rows0, rows1)
        sems = (sem0, sem1)

        def start(c, slot):
            pltpu.async_copy(h_hbm.at[pl.ds(w0 + c * w, w)], rows[slot], sems[slot])

        def finish(c, slot):
            pltpu.sync_copy(c0_hbm.at[pl.ds(w0 + c * w, w)], code_v)
            _rows_from_codes(code_v, base_v, i0_v, info.num_lanes)
            pltpu.sync_copy(c1_hbm.at[pl.ds(w0 + c * w, w)], code_v)
            _rows_from_codes(code_v, base_v, i1_v, info.num_lanes)
            pltpu.make_async_copy(h_hbm.at[pl.ds(w0 + c * w, w)], rows[slot], sems[slot]).wait()
            pltpu.sync_copy(rows[slot], xs_hbm.at[i0_v])
            pltpu.sync_copy(rows[slot], xs_hbm.at[i1_v])

        _two_slot_loop(n_chunks, start, finish)

    return scatter(h2, code0, code1, base)


def _sc_collect(ys, codes, base):
    n = codes.shape[0]
    D = ys.shape[1]
    info, mesh, params = _sc_workers()
    n_workers = info.num_cores * info.num_subcores
    w = SC_WINDOW_BYTES // (D * ys.dtype.itemsize)
    per_w = n // n_workers
    n_chunks = per_w // w
    assert per_w * n_workers == n and n_chunks * w == per_w and n_chunks % 2 == 0

    @functools.partial(
        pl.kernel, out_type=jax.ShapeDtypeStruct((n, D), ys.dtype), mesh=mesh, compiler_params=params,
        scratch_types=[pltpu.VMEM((N_EXPERTS,), I32), pltpu.VMEM((w,), I32), pltpu.VMEM((w,), I32),
                       pltpu.VMEM((w,), I32), pltpu.VMEM((w, D), ys.dtype), pltpu.VMEM((w, D), ys.dtype),
                       pltpu.SemaphoreType.DMA, pltpu.SemaphoreType.DMA])
    def gather(ys_hbm, c_hbm, b_hbm, yk_hbm, base_v, code_v, i0_v, i1_v, rows0, rows1, sem0, sem1):
        wid = lax.axis_index("subcore") * info.num_cores + lax.axis_index("core")
        w0 = wid * per_w
        pltpu.sync_copy(b_hbm, base_v)
        idx = (i0_v, i1_v)
        rows = (rows0, rows1)
        sems = (sem0, sem1)

        def start(c, slot):
            pltpu.sync_copy(c_hbm.at[pl.ds(w0 + c * w, w)], code_v)
            _rows_from_codes(code_v, base_v, idx[slot], info.num_lanes)
            pltpu.async_copy(ys_hbm.at[idx[slot]], rows[slot], sems[slot])

        def finish(c, slot):
            pltpu.make_async_copy(ys_hbm.at[idx[slot]], rows[slot], sems[slot]).wait()
            pltpu.sync_copy(rows[slot], yk_hbm.at[pl.ds(w0 + c * w, w)])

        _two_slot_loop(n_chunks, start, finish)

    return gather(ys, codes, base)


def _expert_kernel(te_ref, tb_ref, first_ref, nt_ref, xs_ref, wg_ref, wu_ref, wd_ref, ys_ref,
                   wgb, wub, wdb):
    j = pl.program_id(0)

    @pl.when(j < nt_ref[0])
    def _():
        @pl.when(first_ref[j] == 1)
        def _():
            wgb[...] = wg_ref[0].astype(BF16)
            wub[...] = wu_ref[0].astype(BF16)
            wdb[...] = wd_ref[0].astype(BF16)

        xb = _unpack_bf16_pairs(xs_ref[...])
        g = jnp.dot(xb, wgb[...], preferred_element_type=F32)
        u = jnp.dot(xb, wub[...], preferred_element_type=F32)
        act = (g * _sigmoid(g)) * u
        ys_ref[...] = jnp.dot(act.astype(BF16), wdb[...], preferred_element_type=F32)


def _experts(tile_e, tile_b, tile_first, n_tiles, xs, wg, wu, wd, max_tiles):
    P = xs.shape[0]
    D, de = wg.shape[1:]
    tm = EXPERT_TM
    wmap = lambda j, te, tb, tf, nt: (te[j], 0, 0)
    rmap = lambda j, te, tb, tf, nt: (tb[j], 0)
    return pl.pallas_call(
        _expert_kernel,
        out_shape=jax.ShapeDtypeStruct((P, D), F32),
        grid_spec=pltpu.PrefetchScalarGridSpec(
            num_scalar_prefetch=4,
            grid=(max_tiles,),
            in_specs=[pl.BlockSpec((tm, xs.shape[1]), rmap),
                      pl.BlockSpec((1, D, de), wmap),
                      pl.BlockSpec((1, D, de), wmap),
                      pl.BlockSpec((1, de, D), wmap)],
            out_specs=pl.BlockSpec((tm, D), rmap),
            scratch_shapes=[pltpu.VMEM((D, de), BF16),
                            pltpu.VMEM((D, de), BF16),
                            pltpu.VMEM((de, D), BF16)]),
        compiler_params=_params("arbitrary"),
        name="experts",
    )(tile_e, tile_b, tile_first, n_tiles, xs, wg, wu, wd)


def _combine_kernel(x1_ref, rf_ref, mod_ref, gf_ref, y0_ref, y1_ref, out_ref, *, final_norm):
    gate2 = mod_ref[0, 5:6, :]
    w = rf_ref[...]
    moe = w[:, 0:1] * y0_ref[...] + w[:, 1:2] * y1_ref[...]
    x2 = x1_ref[...] + gate2 * moe
    if final_norm:
        ms = jnp.mean(x2 * x2, axis=-1, keepdims=True)
        x2 = x2 * lax.rsqrt(ms + RMS_EPS) * gf_ref[...]
    out_ref[...] = x2


def _combine(x1, rf, mod3, gf, yk, seq, final_norm):
    T, D = x1.shape
    tc = COMBINE_TM
    per_b = seq // tc
    n_blk = T // tc
    return pl.pallas_call(
        functools.partial(_combine_kernel, final_norm=final_norm),
        out_shape=jax.ShapeDtypeStruct((T, D), F32),
        grid=(n_blk,),
        in_specs=[pl.BlockSpec((tc, D), lambda i: (i, 0)),
                  pl.BlockSpec((tc, LANES), lambda i: (i, 0)),
                  pl.BlockSpec((1, 6, D), lambda i: (i // per_b, 0, 0)),
                  pl.BlockSpec((1, D), lambda i: (0, 0)),
                  pl.BlockSpec((tc, D), lambda i: (i, 0)),
                  pl.BlockSpec((tc, D), lambda i: (n_blk + i, 0))],
        out_specs=pl.BlockSpec((tc, D), lambda i: (i, 0)),
        compiler_params=_params("arbitrary"),
        name="combine",
    )(x1, rf, mod3, gf, yk, yk)


def _layer(x2, c, seq, w_ada, b_ada, g_norm1, w_in, b_if, conv_dw_w, conv_dw_b, conv_ln_g, conv_ln_b,
           w_conv_out, qk_conv_w, qk_conv_b, m_norm_g, w_m_out, w_out, g_norm2, w_rg, b_rg,
           w_re, b_re, w_e_gate, w_e_up, w_e_down):
    T, D = x2.shape
    B = T // seq
    dc = D // 2
    nif = 2 * M_HEADS

    mod3 = _ada(c, w_ada, b_ada).reshape(B, 6, D)

    if_lo = 6 * dc
    w_main = jnp.concatenate([w_in[:, :if_lo], w_in[:, if_lo + nif:]], axis=1).astype(BF16)
    w_if = w_in[:, if_lo:if_lo + nif]
    w_if_pad = jnp.pad(w_if, ((0, 0), (0, LANES - nif))).astype(BF16)
    w_ift = w_if.T.astype(BF16)
    u, qk, v, o, sga, sgb, ifc, ifr = _inproj(x2, mod3, g_norm1.reshape(1, D), w_main, w_if_pad, w_ift, seq)

    ya = _conv_branch(u.reshape(B, seq, dc), conv_dw_w, conv_dw_b.reshape(1, dc),
                      conv_ln_g.reshape(1, dc), conv_ln_b.reshape(1, dc), w_conv_out.astype(BF16))
    bifc = jnp.pad(b_if, (0, LANES - nif)).reshape(1, LANES)
    bifr = b_if.reshape(nif, 1)
    yb = _mlstm_branch(qk.reshape(B, seq, 2 * dc), v.reshape(B, seq, dc), o.reshape(B, seq, dc),
                       ifc.reshape(B, seq, LANES), ifr, qk_conv_w, qk_conv_b.reshape(1, 2 * dc), bifc, bifr,
                       m_norm_g.reshape(1, dc), w_m_out.astype(BF16))

    n_r = N_GROUPS + N_EXPERTS
    w_r = jnp.pad(jnp.concatenate([w_rg, w_re], axis=1), ((0, 0), (0, LANES - n_r))).astype(BF16)
    b_r = jnp.pad(jnp.concatenate([b_rg, b_re]), (0, LANES - n_r)).reshape(1, LANES)
    x1, h2, ri, rf, cnt = _merge(x2, ya.reshape(T, D), yb.reshape(T, D), sga, sgb, mod3,
                                 g_norm2.reshape(1, D), w_out.astype(BF16), w_r, b_r, seq)

    tm = EXPERT_TM
    counts = cnt[0, :N_EXPERTS]
    tiles_e = (counts + tm - 1) // tm
    tile_end = jnp.cumsum(tiles_e)
    n_tiles = tile_end[-1]
    max_tiles = (T * TOP_K) // tm + N_EXPERTS
    jt = jnp.arange(max_tiles, dtype=I32)
    jc = jnp.minimum(jt, n_tiles - 1)
    tile_e = jnp.sum(jc[:, None] >= tile_end[None, :], axis=1).astype(I32)
    tile_first = ((jt < n_tiles) & (jc == (tile_end - tiles_e)[tile_e])).astype(I32)
    base = ((tile_end - tiles_e) * tm).astype(I32)

    code0 = ri[:, 0]
    code1 = ri[:, 1]
    xs = _sc_dispatch(h2, code0, code1, base, max_tiles * tm)
    ys = _experts(tile_e, jc, tile_first, n_tiles.reshape(1).astype(I32), xs,
                  w_e_gate, w_e_up, w_e_down, max_tiles)
    return x1, rf, mod3, ys, code0, code1, base


def kernel(x, c, w_ada, b_ada, g_norm1, w_in, b_if, conv_dw_w, conv_dw_b, conv_ln_g, conv_ln_b,
           w_conv_out, qk_conv_w, qk_conv_b, m_norm_g, w_m_out, w_out, g_norm2, w_rg, b_rg,
           w_re, b_re, w_e_gate, w_e_up, w_e_down, g_final):
    B, S, D = x.shape
    depth = w_ada.shape[0]
    x2 = x.reshape(B * S, D)
    for l in range(depth):
        x1, rf, mod3, ys, code0, code1, base = _layer(
            x2, c, S, w_ada[l], b_ada[l], g_norm1[l], w_in[l], b_if[l], conv_dw_w[l], conv_dw_b[l],
            conv_ln_g[l], conv_ln_b[l], w_conv_out[l], qk_conv_w[l], qk_conv_b[l], m_norm_g[l],
            w_m_out[l], w_out[l], g_norm2[l], w_rg[l], b_rg[l], w_re[l], b_re[l],
            w_e_gate[l], w_e_up[l], w_e_down[l])
        yk = _sc_collect(ys, jnp.concatenate([code0, code1]), base)
        x2 = _combine(x1, rf, mod3, g_final.reshape(1, D), yk, S, final_norm=l == depth - 1)
    return x2.reshape(B, S, D)
```

```python
import dataclasses
import functools

import jax
import jax.numpy as jnp
from jax import lax
from jax.experimental import pallas as pl
from jax.experimental.pallas import tpu as pltpu
from jax.experimental.pallas import tpu_sc as plsc

F32 = jnp.float32
BF16 = jnp.bfloat16
I32 = jnp.int32

M_HEADS = 4
CONV_WIDTH = 31
QK_CONV_WIDTH = 4
N_GROUPS = 4
E_PER_GROUP = 8
N_EXPERTS = N_GROUPS * E_PER_GROUP
TOP_K = 2
RMS_EPS = 1e-6
LN_EPS = 1e-5

LANES = 128
SUBLANES = 8
VMEM_LIMIT = 56 * 1024 * 1024

ADA_TN = 1024
INPROJ_TM = 256
CONV_TS = 256
CONV_HALO = 32
CONV_RC = 32
MLSTM_L = 128
MLSTM_SEQS = 2
MERGE_TM = 256
EXPERT_TM = 512
SC_WINDOW_BYTES = 128 * 1024
COMBINE_TM = 256
RANK_BITS = 16
RANK_RADIX = 1 << RANK_BITS
assert EXPERT_TM & (EXPERT_TM - 1) == 0


def _sigmoid(v):
    return 1.0 / (1.0 + jnp.exp(-v))


def _log_sigmoid(v):
    return -(jnp.maximum(-v, 0.0) + jnp.log1p(jnp.exp(-jnp.abs(v))))


def _pack_bf16_pairs(v):
    n = v.shape[1] // 2
    bits = lax.bitcast_convert_type(v.astype(BF16).astype(F32), jnp.uint32)
    word = bits[:, n:] | (bits[:, :n] >> 16)
    return lax.bitcast_convert_type(word, I32)


def _unpack_bf16_pairs(w):
    bits = lax.bitcast_convert_type(w, jnp.uint32)
    lo = lax.bitcast_convert_type(bits << 16, F32)
    hi = lax.bitcast_convert_type(bits & jnp.uint32(0xFFFF0000), F32)
    return jnp.concatenate([lo, hi], axis=1).astype(BF16)


def _params(*sem):
    return pltpu.CompilerParams(dimension_semantics=sem, vmem_limit_bytes=VMEM_LIMIT)


def _ada_kernel(c_ref, w_ref, b_ref, o_ref):
    c = c_ref[...]
    s = c * _sigmoid(c)
    o_ref[...] = jnp.dot(s, w_ref[...], preferred_element_type=F32,
                         precision=lax.Precision.HIGHEST) + b_ref[...]


def _ada(c, w_ada, b_ada):
    B, D = c.shape
    N = w_ada.shape[1]
    return pl.pallas_call(
        _ada_kernel,
        out_shape=jax.ShapeDtypeStruct((B, N), F32),
        grid=(N // ADA_TN,),
        in_specs=[pl.BlockSpec((B, D), lambda j: (0, 0)),
                  pl.BlockSpec((D, ADA_TN), lambda j: (0, j)),
                  pl.BlockSpec((1, ADA_TN), lambda j: (0, j))],
        out_specs=pl.BlockSpec((B, ADA_TN), lambda j: (0, j)),
        compiler_params=_params("arbitrary"),
        name="ada",
    )(c, w_ada, b_ada.reshape(1, N))


def _inproj_kernel(x_ref, mod_ref, g_ref, wm_ref, wgt_ref, wif_ref, wift_ref,
                   u_ref, qk_ref, v_ref, o_ref, sga_ref, sgb_ref, ifc_ref, ifr_ref):
    x = x_ref[...]
    shift = mod_ref[0, 0:1, :]
    scale = mod_ref[0, 1:2, :]
    ms = jnp.mean(x * x, axis=-1, keepdims=True)
    h = x * lax.rsqrt(ms + RMS_EPS) * g_ref[...]
    h = h * (1.0 + scale) + shift
    hb = h.astype(BF16)
    dc = u_ref.shape[1]
    d = sga_ref.shape[1]

    def seg(lo, hi):
        return jnp.dot(hb, wm_ref[:, lo:hi], preferred_element_type=F32)

    u_ref[...] = seg(0, dc) * _sigmoid(seg(dc, 2 * dc))
    qk_ref[...] = seg(2 * dc, 4 * dc)
    v_ref[...] = seg(4 * dc, 5 * dc)
    o_ref[...] = seg(5 * dc, 6 * dc)
    sga_ref[...] = _sigmoid(jnp.dot(hb, wgt_ref[:, 0:d], preferred_element_type=F32))
    sgb_ref[...] = _sigmoid(jnp.dot(hb, wgt_ref[:, d:2 * d], preferred_element_type=F32))
    ifc_ref[...] = jnp.dot(hb, wif_ref[...], preferred_element_type=F32)
    ifr_ref[0] = lax.dot_general(wift_ref[...], hb, (((1,), (1,)), ((), ())),
                                 preferred_element_type=F32)


def _inproj(x2, mod3, g1, w_main, w_gates, w_if, w_ift, seq):
    T, D = x2.shape
    tm = INPROJ_TM
    dc = D // 2
    per_b = seq // tm
    row = lambda i: (i, 0)
    const = lambda i: (0, 0)
    return pl.pallas_call(
        _inproj_kernel,
        out_shape=[jax.ShapeDtypeStruct((T, dc), F32),
                   jax.ShapeDtypeStruct((T, 2 * dc), F32),
                   jax.ShapeDtypeStruct((T, dc), F32),
                   jax.ShapeDtypeStruct((T, dc), F32),
                   jax.ShapeDtypeStruct((T, D), F32),
                   jax.ShapeDtypeStruct((T, D), F32),
                   jax.ShapeDtypeStruct((T, LANES), F32),
                   jax.ShapeDtypeStruct((T // seq, SUBLANES, seq), F32)],
        grid=(T // tm,),
        in_specs=[pl.BlockSpec((tm, D), row),
                  pl.BlockSpec((1, 6, D), lambda i: (i // per_b, 0, 0)),
                  pl.BlockSpec((1, D), const),
                  pl.BlockSpec(w_main.shape, const),
                  pl.BlockSpec(w_gates.shape, const),
                  pl.BlockSpec(w_if.shape, const),
                  pl.BlockSpec(w_ift.shape, const)],
        out_specs=[pl.BlockSpec((tm, dc), row),
                   pl.BlockSpec((tm, 2 * dc), row),
                   pl.BlockSpec((tm, dc), row),
                   pl.BlockSpec((tm, dc), row),
                   pl.BlockSpec((tm, D), row),
                   pl.BlockSpec((tm, D), row),
                   pl.BlockSpec((tm, LANES), row),
                   pl.BlockSpec((1, SUBLANES, tm), lambda i: (i // per_b, 0, i % per_b))],
        compiler_params=_params("arbitrary"),
        name="inproj",
    )(x2, mod3, g1, w_main, w_gates, w_if, w_ift)


def _conv_kernel(u_ref, w_ref, b_ref, lg_ref, lb_ref, wo_ref, y_ref, ubuf, sbuf, cbuf):
    ts = u_ref.shape[1]
    halo = CONV_HALO

    @pl.when(pl.program_id(1) == 0)
    def _():
        ubuf[0:halo, :] = jnp.zeros((halo, ubuf.shape[1]), F32)

    ubuf[halo:halo + ts, :] = u_ref[0]
    ns = sbuf.shape[1]
    for r in range(1, SUBLANES):
        sbuf[r - 1] = ubuf[r:r + ns, :]
    off = halo - (CONV_WIDTH - 1)
    for r0 in range(0, ts, CONV_RC):
        acc = jnp.broadcast_to(b_ref[...], (CONV_RC, ubuf.shape[1]))
        for k in range(CONV_WIDTH):
            r = (off + k) % SUBLANES
            lo = off + k - r + r0
            win = ubuf[lo:lo + CONV_RC, :] if r == 0 else sbuf[r - 1, lo:lo + CONV_RC, :]
            acc = acc + w_ref[k:k + 1, :] * win
        cbuf[r0:r0 + CONV_RC, :] = acc
    ubuf[0:halo, :] = ubuf[ts:ts + halo, :]

    a = cbuf[...]
    mu = jnp.mean(a, axis=-1, keepdims=True)
    ac = a - mu
    var = jnp.mean(ac * ac, axis=-1, keepdims=True)
    z = ac * lax.rsqrt(var + LN_EPS) * lg_ref[...] + lb_ref[...]
    z = z * _sigmoid(z)
    y_ref[0] = jnp.dot(z.astype(BF16), wo_ref[...], preferred_element_type=F32)


def _conv_branch(u3, w, b, lg, lb, wo):
    B, S, C = u3.shape
    D = wo.shape[1]
    ts = CONV_TS
    const = lambda bi, si: (0, 0)
    return pl.pallas_call(
        _conv_kernel,
        out_shape=jax.ShapeDtypeStruct((B, S, D), F32),
        grid=(B, S // ts),
        in_specs=[pl.BlockSpec((1, ts, C), lambda bi, si: (bi, si, 0)),
                  pl.BlockSpec(w.shape, const),
                  pl.BlockSpec((1, C), const),
                  pl.BlockSpec((1, C), const),
                  pl.BlockSpec((1, C), const),
                  pl.BlockSpec(wo.shape, const)],
        out_specs=pl.BlockSpec((1, ts, D), lambda bi, si: (bi, si, 0)),
        scratch_shapes=[pltpu.VMEM((ts + CONV_HALO, C), F32),
                        pltpu.VMEM((SUBLANES - 1, ts + CONV_HALO - SUBLANES, C), F32),
                        pltpu.VMEM((ts, C), F32)],
        compiler_params=_params("arbitrary", "arbitrary"),
        name="conv",
    )(u3, w, b, lg, lb, wo)


def _mlstm_kernel(qk_ref, v_ref, o_ref, ifc_ref, ifr_ref, cw_ref, cb_ref, bifc_ref, bifr_ref,
                  ng_ref, wo_ref, y_ref, qkbuf, cn_ref, m_ref, hbuf):
    @pl.when(pl.program_id(1) == 0)
    def _():
        qkbuf[:, 0:SUBLANES, :] = jnp.zeros((qkbuf.shape[0], SUBLANES, qkbuf.shape[2]), F32)
        cn_ref[...] = jnp.zeros(cn_ref.shape, F32)
        m_ref[...] = jnp.zeros(m_ref.shape, F32)

    for b in range(qk_ref.shape[0]):
        _mlstm_chunk(qk_ref.at[b], v_ref.at[b], o_ref.at[b], ifc_ref.at[b], ifr_ref.at[b], cw_ref, cb_ref,
                     bifc_ref, bifr_ref, ng_ref, wo_ref, y_ref.at[b], qkbuf.at[b], cn_ref.at[b],
                     m_ref.at[b], hbuf.at[b])


def _mlstm_chunk(qk_ref, v_ref, o_ref, ifc_ref, ifr_ref, cw_ref, cb_ref, bifc_ref, bifr_ref,
                 ng_ref, wo_ref, y_ref, qkbuf, cn_ref, m_ref, hbuf):
    L = qk_ref.shape[0]
    mi = v_ref.shape[1]
    dh = mi // M_HEADS
    halo = SUBLANES

    qkbuf[halo:halo + L, :] = qk_ref[...]
    off = halo - (QK_CONV_WIDTH - 1)
    y = jnp.broadcast_to(cb_ref[...], (L, qkbuf.shape[1]))
    for k in range(QK_CONV_WIDTH):
        y = y + cw_ref[k:k + 1, :] * qkbuf[off + k:off + k + L, :]
    y = y * _sigmoid(y)
    qkbuf[0:halo, :] = qkbuf[L:L + halo, :]

    ifr = ifr_ref[...] + bifr_ref[...]
    ifc = ifc_ref[...] + bifc_ref[...]
    lfr = _log_sigmoid(ifr)
    lfc = _log_sigmoid(ifc)
    rows = lax.broadcasted_iota(I32, (L, L), 0)
    cols = lax.broadcasted_iota(I32, (L, L), 1)
    causal = cols <= rows
    lower = causal.astype(F32)
    upper = (rows <= cols).astype(F32)
    bcum_c = jnp.dot(lower, lfc, preferred_element_type=F32, precision=lax.Precision.HIGHEST)
    bcum_r = jnp.dot(lfr, upper, preferred_element_type=F32, precision=lax.Precision.HIGHEST)

    lane = lax.broadcasted_iota(I32, (L, dh), 1)
    ones_col = jnp.where(lane == 0, 1.0, 0.0).astype(F32)
    vv = v_ref[...]
    oo = o_ref[...]
    scale = dh ** -0.5
    for hd in range(M_HEADS):
        q = y[:, hd * dh:(hd + 1) * dh] * scale
        kk = y[:, mi + hd * dh:mi + (hd + 1) * dh]
        v = vv[:, hd * dh:(hd + 1) * dh]
        kt = kk.T
        bc = bcum_c[:, M_HEADS + hd:M_HEADS + hd + 1]
        br = bcum_r[M_HEADS + hd:M_HEADS + hd + 1, :]
        li = ifr[hd:hd + 1, :]
        m_prev = m_ref[hd, 0:1, 0:1]
        dmat = jnp.where(causal, bc - br + li, -jnp.inf)
        inter = bc + m_prev
        m_t = jnp.maximum(jnp.max(dmat, axis=-1, keepdims=True), inter)
        wts = jnp.exp(dmat - m_t)
        s_inter = jnp.exp(inter - m_t)
        qb = q.astype(BF16)
        s_mat = jnp.dot(qb, kt.astype(BF16), preferred_element_type=F32) * wts
        cn = cn_ref[hd]
        qcn = jnp.dot(qb, cn.astype(BF16), preferred_element_type=F32)
        num = jnp.dot(s_mat.astype(BF16), v.astype(BF16), preferred_element_type=F32) \
            + s_inter * qcn[:, 0:dh]
        den = jnp.sum(s_mat, axis=-1, keepdims=True) + s_inter * qcn[:, dh:dh + 1]
        hh = num / jnp.maximum(jnp.abs(den), jnp.exp(-m_t))
        b_last = br[:, L - 1:L]
        a = b_last - br + li
        m_new = jnp.maximum(b_last + m_prev, jnp.max(a, axis=-1, keepdims=True))
        wk = jnp.exp(a - m_new)
        sc = jnp.exp(b_last + m_prev - m_new)
        v_ext = jnp.concatenate([v, ones_col], axis=1)
        cn_ref[hd] = sc * cn + jnp.dot((kt * wk).astype(BF16), v_ext.astype(BF16),
                                       preferred_element_type=F32)
        m_ref[hd] = jnp.broadcast_to(m_new, m_ref.shape[1:])
        mu = jnp.mean(hh, axis=-1, keepdims=True)
        hc = hh - mu
        var = jnp.mean(hc * hc, axis=-1, keepdims=True)
        hn = hc * lax.rsqrt(var + LN_EPS) * ng_ref[:, hd * dh:(hd + 1) * dh]
        hbuf[:, hd * dh:(hd + 1) * dh] = hn * _sigmoid(oo[:, hd * dh:(hd + 1) * dh])
    y_ref[...] = jnp.dot(hbuf[...].astype(BF16), wo_ref[...], preferred_element_type=F32)


def _mlstm_branch(qk3, v3, o3, ifc3, ifr3, cw, cb, bifc, bifr, ng, wo):
    B, S, C2 = qk3.shape
    mi = v3.shape[2]
    dh = mi // M_HEADS
    D = wo.shape[1]
    L = MLSTM_L
    nb = MLSTM_SEQS
    const = lambda bi, ci: (0, 0)
    tile = lambda bi, ci: (bi, ci, 0)
    return pl.pallas_call(
        _mlstm_kernel,
        out_shape=jax.ShapeDtypeStruct((B, S, D), F32),
        grid=(B // nb, S // L),
        in_specs=[pl.BlockSpec((nb, L, C2), tile),
                  pl.BlockSpec((nb, L, mi), tile),
                  pl.BlockSpec((nb, L, mi), tile),
                  pl.BlockSpec((nb, L, LANES), tile),
                  pl.BlockSpec((nb, SUBLANES, L), lambda bi, ci: (bi, 0, ci)),
                  pl.BlockSpec(cw.shape, const),
                  pl.BlockSpec((1, C2), const),
                  pl.BlockSpec((1, LANES), const),
                  pl.BlockSpec((SUBLANES, 1), const),
                  pl.BlockSpec((1, mi), const),
                  pl.BlockSpec(wo.shape, const)],
        out_specs=pl.BlockSpec((nb, L, D), tile),
        scratch_shapes=[pltpu.VMEM((nb, L + SUBLANES, C2), F32),
                        pltpu.VMEM((nb, M_HEADS, dh, 2 * dh), F32),
                        pltpu.VMEM((nb, M_HEADS, SUBLANES, LANES), F32),
                        pltpu.VMEM((nb, L, mi), F32)],
        compiler_params=_params("arbitrary", "arbitrary"),
        name="mlstm",
    )(qk3, v3, o3, ifc3, ifr3, cw, cb, bifc, bifr, ng, wo)


def _merge_kernel(x_ref, ya_ref, yb_ref, sga_ref, sgb_ref, mod_ref, g2_ref, wo_ref, wr_ref, br_ref,
                  x1_ref, h2_ref, ri_ref, rf_ref, cnt_ref, run_ref):
    tm = x_ref.shape[0]

    @pl.when(pl.program_id(0) == 0)
    def _():
        run_ref[...] = jnp.zeros(run_ref.shape, F32)

    gate1 = mod_ref[0, 2:3, :]
    shift2 = mod_ref[0, 3:4, :]
    scale2 = mod_ref[0, 4:5, :]
    merged = sga_ref[...] * ya_ref[...] + sgb_ref[...] * yb_ref[...]
    mix = jnp.dot(merged.astype(BF16), wo_ref[...], preferred_element_type=F32)
    x1 = x_ref[...] + gate1 * mix
    x1_ref[...] = x1
    ms = jnp.mean(x1 * x1, axis=-1, keepdims=True)
    h2 = x1 * lax.rsqrt(ms + RMS_EPS) * g2_ref[...]
    h2 = h2 * (1.0 + scale2) + shift2
    h2_ref[...] = _pack_bf16_pairs(h2)

    logits = jnp.dot(h2.astype(BF16), wr_ref[...], preferred_element_type=F32) + br_ref[...]
    lane = lax.broadcasted_iota(I32, (tm, LANES), 1).astype(F32)
    neg = -jnp.inf

    def first_argmax(vals):
        mx = jnp.max(vals, axis=-1, keepdims=True)
        idx = jnp.min(jnp.where(vals == mx, lane, float(LANES)), axis=-1, keepdims=True)
        return mx, idx

    lg = jnp.where(lane < N_GROUPS, logits, neg)
    gmax, gsel = first_argmax(lg)
    p_g = 1.0 / jnp.sum(jnp.exp(lg - gmax), axis=-1, keepdims=True)
    lo = N_GROUPS + gsel * E_PER_GROUP
    le = jnp.where((lane >= lo) & (lane < lo + E_PER_GROUP), logits, neg)
    l1, i1 = first_argmax(le)
    l2, i2 = first_argmax(jnp.where(lane == i1, neg, le))
    r = jnp.exp(l2 - l1)
    w1 = p_g / (1.0 + r)
    w2 = p_g * r / (1.0 + r)
    e1 = i1 - N_GROUPS
    e2 = i2 - N_GROUPS

    onehot = jnp.where((lane == e1) | (lane == e2), 1.0, 0.0)
    rows = lax.broadcasted_iota(I32, (tm, tm), 0)
    cols = lax.broadcasted_iota(I32, (tm, tm), 1)
    strict = jnp.where(cols < rows, 1.0, 0.0).astype(BF16)
    run = run_ref[0:1, :]
    before = jnp.dot(strict, onehot.astype(BF16), preferred_element_type=F32) + run
    rank1 = jnp.sum(jnp.where(lane == e1, before, 0.0), axis=-1, keepdims=True)
    rank2 = jnp.sum(jnp.where(lane == e2, before, 0.0), axis=-1, keepdims=True)
    run_new = run + jnp.sum(onehot, axis=0, keepdims=True)
    run_ref[...] = jnp.broadcast_to(run_new, run_ref.shape)
    cnt_ref[...] = jnp.broadcast_to(run_new, cnt_ref.shape).astype(I32)

    ri_ref[...] = jnp.where(lane == 0, e1 * float(RANK_RADIX) + rank1,
                            jnp.where(lane == 1, e2 * float(RANK_RADIX) + rank2, 0.0)).astype(I32)
    rf_ref[...] = jnp.where(lane == 0, w1, jnp.where(lane == 1, w2, 0.0))


def _merge(x2, ya, yb, sga, sgb, mod3, g2, wo, wr, br, seq):
    T, D = x2.shape
    tm = MERGE_TM
    per_b = seq // tm
    row = lambda i: (i, 0)
    const = lambda i: (0, 0)
    return pl.pallas_call(
        _merge_kernel,
        out_shape=[jax.ShapeDtypeStruct((T, D), F32),
                   jax.ShapeDtypeStruct((T, D // 2), I32),
                   jax.ShapeDtypeStruct((T, LANES), I32),
                   jax.ShapeDtypeStruct((T, LANES), F32),
                   jax.ShapeDtypeStruct((SUBLANES, LANES), I32)],
        grid=(T // tm,),
        in_specs=[pl.BlockSpec((tm, D), row),
                  pl.BlockSpec((tm, D), row),
                  pl.BlockSpec((tm, D), row),
                  pl.BlockSpec((tm, D), row),
                  pl.BlockSpec((tm, D), row),
                  pl.BlockSpec((1, 6, D), lambda i: (i // per_b, 0, 0)),
                  pl.BlockSpec((1, D), const),
                  pl.BlockSpec(wo.shape, const),
                  pl.BlockSpec(wr.shape, const),
                  pl.BlockSpec((1, LANES), const)],
        out_specs=[pl.BlockSpec((tm, D), row),
                   pl.BlockSpec((tm, D // 2), row),
                   pl.BlockSpec((tm, LANES), row),
                   pl.BlockSpec((tm, LANES), row),
                   pl.BlockSpec((SUBLANES, LANES), const)],
        scratch_shapes=[pltpu.VMEM((SUBLANES, LANES), F32)],
        compiler_params=_params("arbitrary"),
        name="merge",
    )(x2, ya, yb, sga, sgb, mod3, g2, wo, wr, br)


def _sc_workers():
    info = plsc.get_sparse_core_info()
    mesh = plsc.VectorSubcoreMesh(core_axis_name="core", subcore_axis_name="subcore")
    params = pltpu.CompilerParams()
    if "needs_layout_passes" in pltpu.CompilerParams.__dataclass_fields__:
        params = dataclasses.replace(params, needs_layout_passes=False)
    return info, mesh, params


def _rows_from_codes(code_v, base_v, idx_v, lanes):
    for j in range(code_v.shape[0] // lanes):
        c = code_v[pl.ds(j * lanes, lanes)]
        expert = lax.shift_right_logical(c, RANK_BITS)
        idx_v[pl.ds(j * lanes, lanes)] = plsc.load_gather(base_v, [expert]) + (c & (RANK_RADIX - 1))


def _two_slot_loop(n_chunks, start, finish):
    start(0, 0)

    @pl.loop(0, n_chunks, step=2)
    def _(c):
        start(c + 1, 1)
        finish(c, 0)

        @pl.when(c + 2 < n_chunks)
        def _():
            start(c + 2, 0)

        finish(c + 1, 1)


def _sc_dispatch(h2, code0, code1, base, n_rows):
    T, D = h2.shape
    info, mesh, params = _sc_workers()
    n_workers = info.num_cores * info.num_subcores
    w = SC_WINDOW_BYTES // (D * h2.dtype.itemsize)
    per_w = T // n_workers
    n_chunks = per_w // w
    assert per_w * n_workers == T and n_chunks * w == per_w and n_chunks % 2 == 0

    @functools.partial(
        pl.kernel, out_type=jax.ShapeDtypeStruct((n_rows, D), h2.dtype), mesh=mesh, compiler_params=params,
        scratch_types=[pltpu.VMEM((N_EXPERTS,), I32), pltpu.VMEM((w,), I32), pltpu.VMEM((w,), I32),
                       pltpu.VMEM((w,), I32), pltpu.VMEM((w, D), h2.dtype), pltpu.VMEM((w, D), h2.dtype),
                       pltpu.SemaphoreType.DMA, pltpu.SemaphoreType.DMA])
    def scatter(h_hbm, c0_hbm, c1_hbm, b_hbm, xs_hbm, base_v, code_v, i0_v, i1_v, rows0, rows1, sem0, sem1):
        wid = lax.axis_index("subcore") * info.num_cores + lax.axis_index("core")
        w0 = wid * per_w
        pltpu.sync_copy(b_hbm, base_v)
        rows = (rows0, rows1)
        sems = (sem0, sem1)

        def start(c, slot):
            pltpu.async_copy(h_hbm.at[pl.ds(w0 + c * w, w)], rows[slot], sems[slot])

        def finish(c, slot):
            pltpu.sync_copy(c0_hbm.at[pl.ds(w0 + c * w, w)], code_v)
            _rows_from_codes(code_v, base_v, i0_v, info.num_lanes)
            pltpu.sync_copy(c1_hbm.at[pl.ds(w0 + c * w, w)], code_v)
            _rows_from_codes(code_v, base_v, i1_v, info.num_lanes)
            pltpu.make_async_copy(h_hbm.at[pl.ds(w0 + c * w, w)], rows[slot], sems[slot]).wait()
            pltpu.sync_copy(rows[slot], xs_hbm.at[i0_v])
            pltpu.sync_copy(rows[slot], xs_hbm.at[i1_v])

        _two_slot_loop(n_chunks, start, finish)

    return scatter(h2, code0, code1, base)


def _sc_collect(ys, codes, base):
    n = codes.shape[0]
    D = ys.shape[1]
    info, mesh, params = _sc_workers()
    n_workers = info.num_cores * info.num_subcores
    w = SC_WINDOW_BYTES // (D * ys.dtype.itemsize)
    per_w = n // n_workers
    n_chunks = per_w // w
    assert per_w * n_workers == n and n_chunks * w == per_w and n_chunks % 2 == 0

    @functools.partial(
        pl.kernel, out_type=jax.ShapeDtypeStruct((n, D), ys.dtype), mesh=mesh, compiler_params=params,
        scratch_types=[pltpu.VMEM((N_EXPERTS,), I32), pltpu.VMEM((w,), I32), pltpu.VMEM((w,), I32),
                       pltpu.VMEM((w,), I32), pltpu.VMEM((w, D), ys.dtype), pltpu.VMEM((w, D), ys.dtype),
                       pltpu.SemaphoreType.DMA, pltpu.SemaphoreType.DMA])
    def gather(ys_hbm, c_hbm, b_hbm, yk_hbm, base_v, code_v, i0_v, i1_v, rows0, rows1, sem0, sem1):
        wid = lax.axis_index("subcore") * info.num_cores + lax.axis_index("core")
        w0 = wid * per_w
        pltpu.sync_copy(b_hbm, base_v)
        idx = (i0_v, i1_v)
        rows = (rows0, rows1)
        sems = (sem0, sem1)

        def start(c, slot):
            pltpu.sync_copy(c_hbm.at[pl.ds(w0 + c * w, w)], code_v)
            _rows_from_codes(code_v, base_v, idx[slot], info.num_lanes)
            pltpu.async_copy(ys_hbm.at[idx[slot]], rows[slot], sems[slot])

        def finish(c, slot):
            pltpu.make_async_copy(ys_hbm.at[idx[slot]], rows[slot], sems[slot]).wait()
            pltpu.sync_copy(rows[slot], yk_hbm.at[pl.ds(w0 + c * w, w)])

        _two_slot_loop(n_chunks, start, finish)

    return gather(ys, codes, base)


def _schedule_kernel(cnt_ref, te_ref, first_ref, base_ref, nt_ref):
    tm = EXPERT_TM

    def expert(e, t0):
        n = (cnt_ref[e] + tm - 1) // tm
        base_ref[e] = t0 * tm

        def tile(t, c):
            te_ref[t] = e
            first_ref[t] = jnp.where(t == t0, 1, 0)
            return c

        lax.fori_loop(t0, t0 + n, tile, 0)
        return t0 + n

    nt = lax.fori_loop(0, N_EXPERTS, expert, 0)
    nt_ref[0] = nt
    last = te_ref[nt - 1]

    def idle(t, c):
        te_ref[t] = last
        first_ref[t] = 0
        return c

    lax.fori_loop(nt, te_ref.shape[0], idle, 0)


def _schedule(counts, max_tiles):
    smem = pl.BlockSpec(memory_space=pltpu.SMEM)
    return pl.pallas_call(
        _schedule_kernel,
        out_shape=[jax.ShapeDtypeStruct((max_tiles,), I32),
                   jax.ShapeDtypeStruct((max_tiles,), I32),
                   jax.ShapeDtypeStruct((N_EXPERTS,), I32),
                   jax.ShapeDtypeStruct((1,), I32)],
        in_specs=[smem],
        out_specs=[smem, smem, smem, smem],
        name="schedule",
    )(counts)


def _expert_kernel(te_ref, first_ref, nt_ref, xs_ref, wg_ref, wu_ref, wd_ref, ys_ref,
                   wgb, wub, wdb):
    j = pl.program_id(0)

    @pl.when(j < nt_ref[0])
    def _():
        @pl.when(first_ref[j] == 1)
        def _():
            wgb[...] = wg_ref[0].astype(BF16)
            wub[...] = wu_ref[0].astype(BF16)
            wdb[...] = wd_ref[0].astype(BF16)

        xb = _unpack_bf16_pairs(xs_ref[...])
        g = jnp.dot(xb, wgb[...], preferred_element_type=F32)
        u = jnp.dot(xb, wub[...], preferred_element_type=F32)
        act = (g * _sigmoid(g)) * u
        ys_ref[...] = jnp.dot(act.astype(BF16), wdb[...], preferred_element_type=F32)


def _experts(tile_e, tile_first, n_tiles, xs, wg, wu, wd, max_tiles):
    P = xs.shape[0]
    D, de = wg.shape[1:]
    tm = EXPERT_TM
    wmap = lambda j, te, tf, nt: (te[j], 0, 0)
    rmap = lambda j, te, tf, nt: (j, 0)
    return pl.pallas_call(
        _expert_kernel,
        out_shape=jax.ShapeDtypeStruct((P, D), F32),
        grid_spec=pltpu.PrefetchScalarGridSpec(
            num_scalar_prefetch=3,
            grid=(max_tiles,),
            in_specs=[pl.BlockSpec((tm, xs.shape[1]), rmap),
                      pl.BlockSpec((1, D, de), wmap),
                      pl.BlockSpec((1, D, de), wmap),
                      pl.BlockSpec((1, de, D), wmap)],
            out_specs=pl.BlockSpec((tm, D), rmap),
            scratch_shapes=[pltpu.VMEM((D, de), BF16),
                            pltpu.VMEM((D, de), BF16),
                            pltpu.VMEM((de, D), BF16)]),
        compiler_params=_params("arbitrary"),
        name="experts",
    )(tile_e, tile_first, n_tiles, xs, wg, wu, wd)


def _combine_kernel(x1_ref, rf_ref, mod_ref, gf_ref, y0_ref, y1_ref, out_ref, *, final_norm):
    gate2 = mod_ref[0, 5:6, :]
    w = rf_ref[...]
    moe = w[:, 0:1] * y0_ref[...] + w[:, 1:2] * y1_ref[...]
    x2 = x1_ref[...] + gate2 * moe
    if final_norm:
        ms = jnp.mean(x2 * x2, axis=-1, keepdims=True)
        x2 = x2 * lax.rsqrt(ms + RMS_EPS) * gf_ref[...]
    out_ref[...] = x2


def _combine(x1, rf, mod3, gf, yk, seq, final_norm):
    T, D = x1.shape
    tc = COMBINE_TM
    per_b = seq // tc
    n_blk = T // tc
    return pl.pallas_call(
        functools.partial(_combine_kernel, final_norm=final_norm),
        out_shape=jax.ShapeDtypeStruct((T, D), F32),
        grid=(n_blk,),
        in_specs=[pl.BlockSpec((tc, D), lambda i: (i, 0)),
                  pl.BlockSpec((tc, LANES), lambda i: (i, 0)),
                  pl.BlockSpec((1, 6, D), lambda i: (i // per_b, 0, 0)),
                  pl.BlockSpec((1, D), lambda i: (0, 0)),
                  pl.BlockSpec((tc, D), lambda i: (i, 0)),
                  pl.BlockSpec((tc, D), lambda i: (n_blk + i, 0))],
        out_specs=pl.BlockSpec((tc, D), lambda i: (i, 0)),
        compiler_params=_params("arbitrary"),
        name="combine",
    )(x1, rf, mod3, gf, yk, yk)


def _layer(x2, c, seq, w_ada, b_ada, g_norm1, w_in, b_if, conv_dw_w, conv_dw_b, conv_ln_g, conv_ln_b,
           w_conv_out, qk_conv_w, qk_conv_b, m_norm_g, w_m_out, w_out, g_norm2, w_rg, b_rg,
           w_re, b_re, w_e_gate, w_e_up, w_e_down):
    T, D = x2.shape
    B = T // seq
    dc = D // 2
    nif = 2 * M_HEADS

    mod3 = _ada(c, w_ada, b_ada).reshape(B, 6, D)

    if_lo = 6 * dc
    w_main = w_in[:, :if_lo].astype(BF16)
    w_gates = w_in[:, if_lo + nif:].astype(BF16)
    w_if = w_in[:, if_lo:if_lo + nif]
    w_if_pad = jnp.pad(w_if, ((0, 0), (0, LANES - nif))).astype(BF16)
    w_ift = w_if.T.astype(BF16)
    u, qk, v, o, sga, sgb, ifc, ifr = _inproj(x2, mod3, g_norm1.reshape(1, D), w_main, w_gates,
                                              w_if_pad, w_ift, seq)

    ya = _conv_branch(u.reshape(B, seq, dc), conv_dw_w, conv_dw_b.reshape(1, dc),
                      conv_ln_g.reshape(1, dc), conv_ln_b.reshape(1, dc), w_conv_out.astype(BF16))
    bifc = jnp.pad(b_if, (0, LANES - nif)).reshape(1, LANES)
    bifr = b_if.reshape(nif, 1)
    yb = _mlstm_branch(qk.reshape(B, seq, 2 * dc), v.reshape(B, seq, dc), o.reshape(B, seq, dc),
                       ifc.reshape(B, seq, LANES), ifr, qk_conv_w, qk_conv_b.reshape(1, 2 * dc), bifc, bifr,
                       m_norm_g.reshape(1, dc), w_m_out.astype(BF16))

    n_r = N_GROUPS + N_EXPERTS
    w_r = jnp.pad(jnp.concatenate([w_rg, w_re], axis=1), ((0, 0), (0, LANES - n_r))).astype(BF16)
    b_r = jnp.pad(jnp.concatenate([b_rg, b_re]), (0, LANES - n_r)).reshape(1, LANES)
    x1, h2, ri, rf, cnt = _merge(x2, ya.reshape(T, D), yb.reshape(T, D), sga, sgb, mod3,
                                 g_norm2.reshape(1, D), w_out.astype(BF16), w_r, b_r, seq)

    tm = EXPERT_TM
    max_tiles = (T * TOP_K) // tm + N_EXPERTS
    tile_e, tile_first, base, n_tiles = _schedule(cnt[0, :N_EXPERTS], max_tiles)

    code0 = ri[:, 0]
    code1 = ri[:, 1]
    xs = _sc_dispatch(h2, code0, code1, base, max_tiles * tm)
    ys = _experts(tile_e, tile_first, n_tiles, xs, w_e_gate, w_e_up, w_e_down, max_tiles)
    return x1, rf, mod3, ys, code0, code1, base


def kernel(x, c, w_ada, b_ada, g_norm1, w_in, b_if, conv_dw_w, conv_dw_b, conv_ln_g, conv_ln_b,
           w_conv_out, qk_conv_w, qk_conv_b, m_norm_g, w_m_out, w_out, g_norm2, w_rg, b_rg,
           w_re, b_re, w_e_gate, w_e_up, w_e_down, g_final):
    B, S, D = x.shape
    depth = w_ada.shape[0]
    x2 = x.reshape(B * S, D)
    for l in range(depth):
        x1, rf, mod3, ys, code0, code1, base = _layer(
            x2, c, S, w_ada[l], b_ada[l], g_norm1[l], w_in[l], b_if[l], conv_dw_w[l], conv_dw_b[l],
            conv_ln_g[l], conv_ln_b[l], w_conv_out[l], qk_conv_w[l], qk_conv_b[l], m_norm_g[l],
            w_m_out[l], w_out[l], g_norm2[l], w_rg[l], b_rg[l], w_re[l], b_re[l],
            w_e_gate[l], w_e_up[l], w_e_down[l])
        yk = _sc_collect(ys, jnp.concatenate([code0, code1]), base)
        x2 = _combine(x1, rf, mod3, g_final.reshape(1, D), yk, S, final_norm=l == depth - 1)
    return x2.reshape(B, S, D)
```

```python
import dataclasses
import functools

import jax
import jax.numpy as jnp
from jax import lax
from jax.experimental import pallas as pl
from jax.experimental.pallas import tpu as pltpu
from jax.experimental.pallas import tpu_sc as plsc

F32 = jnp.float32
BF16 = jnp.bfloat16
I32 = jnp.int32

M_HEADS = 4
CONV_WIDTH = 31
QK_CONV_WIDTH = 4
N_GROUPS = 4
E_PER_GROUP = 8
N_EXPERTS = N_GROUPS * E_PER_GROUP
TOP_K = 2
RMS_EPS = 1e-6
LN_EPS = 1e-5

LANES = 128
SUBLANES = 8
VMEM_LIMIT = 56 * 1024 * 1024

ADA_TN = 1024
INPROJ_TM = 256
CONV_TS = 256
CONV_HALO = 32
CONV_RC = 32
MLSTM_L = 128
MLSTM_SEQS = 2
MERGE_TM = 256
EXPERT_TM = 512
SC_WINDOW_BYTES = 128 * 1024
COMBINE_TM = 256
RANK_BITS = 16
RANK_RADIX = 1 << RANK_BITS
assert EXPERT_TM & (EXPERT_TM - 1) == 0


def _sigmoid(v):
    return 1.0 / (1.0 + jnp.exp(-v))


def _log_sigmoid(v):
    return -(jnp.maximum(-v, 0.0) + jnp.log1p(jnp.exp(-jnp.abs(v))))


def _pack_bf16_pairs(v):
    n = v.shape[1] // 2
    bits = lax.bitcast_convert_type(v.astype(BF16).astype(F32), jnp.uint32)
    word = bits[:, n:] | (bits[:, :n] >> 16)
    return lax.bitcast_convert_type(word, I32)


def _unpack_bf16_pairs(w):
    bits = lax.bitcast_convert_type(w, jnp.uint32)
    lo = lax.bitcast_convert_type(bits << 16, F32)
    hi = lax.bitcast_convert_type(bits & jnp.uint32(0xFFFF0000), F32)
    return jnp.concatenate([lo, hi], axis=1).astype(BF16)


def _params(*sem):
    return pltpu.CompilerParams(dimension_semantics=sem, vmem_limit_bytes=VMEM_LIMIT)


def _ada_kernel(c_ref, w_ref, b_ref, o_ref):
    c = c_ref[...]
    s = c * _sigmoid(c)
    o_ref[...] = jnp.dot(s, w_ref[...], preferred_element_type=F32,
                         precision=lax.Precision.HIGHEST) + b_ref[...]


def _ada(c, w_ada, b_ada):
    B, D = c.shape
    N = w_ada.shape[1]
    return pl.pallas_call(
        _ada_kernel,
        out_shape=jax.ShapeDtypeStruct((B, N), F32),
        grid=(N // ADA_TN,),
        in_specs=[pl.BlockSpec((B, D), lambda j: (0, 0)),
                  pl.BlockSpec((D, ADA_TN), lambda j: (0, j)),
                  pl.BlockSpec((1, ADA_TN), lambda j: (0, j))],
        out_specs=pl.BlockSpec((B, ADA_TN), lambda j: (0, j)),
        compiler_params=_params("arbitrary"),
        name="ada",
    )(c, w_ada, b_ada.reshape(1, N))


def _inproj_kernel(x_ref, mod_ref, g_ref, wm_ref, wgt_ref, wif_ref, wift_ref,
                   u_ref, qk_ref, v_ref, o_ref, sga_ref, sgb_ref, ifc_ref, ifr_ref):
    x = x_ref[...]
    shift = mod_ref[0, 0:1, :]
    scale = mod_ref[0, 1:2, :]
    ms = jnp.mean(x * x, axis=-1, keepdims=True)
    h = x * lax.rsqrt(ms + RMS_EPS) * g_ref[...]
    h = h * (1.0 + scale) + shift
    hb = h.astype(BF16)
    dc = u_ref.shape[1]
    d = sga_ref.shape[1]

    def seg(lo, hi):
        return jnp.dot(hb, wm_ref[:, lo:hi], preferred_element_type=F32)

    u_ref[...] = seg(0, dc) * _sigmoid(seg(dc, 2 * dc))
    qk_ref[...] = seg(2 * dc, 4 * dc)
    v_ref[...] = seg(4 * dc, 5 * dc)
    o_ref[...] = seg(5 * dc, 6 * dc)
    sga_ref[...] = _sigmoid(jnp.dot(hb, wgt_ref[:, 0:d], preferred_element_type=F32))
    sgb_ref[...] = _sigmoid(jnp.dot(hb, wgt_ref[:, d:2 * d], preferred_element_type=F32))
    ifc_ref[...] = jnp.dot(hb, wif_ref[...], preferred_element_type=F32)
    ifr_ref[0] = lax.dot_general(wift_ref[...], hb, (((1,), (1,)), ((), ())),
                                 preferred_element_type=F32)


def _inproj(x2, mod3, g1, w_main, w_gates, w_if, w_ift, seq):
    T, D = x2.shape
    tm = INPROJ_TM
    dc = D // 2
    per_b = seq // tm
    row = lambda i: (i, 0)
    const = lambda i: (0, 0)
    return pl.pallas_call(
        _inproj_kernel,
        out_shape=[jax.ShapeDtypeStruct((T, dc), F32),
                   jax.ShapeDtypeStruct((T, 2 * dc), F32),
                   jax.ShapeDtypeStruct((T, dc), F32),
                   jax.ShapeDtypeStruct((T, dc), F32),
                   jax.ShapeDtypeStruct((T, D), F32),
                   jax.ShapeDtypeStruct((T, D), F32),
                   jax.ShapeDtypeStruct((T, LANES), F32),
                   jax.ShapeDtypeStruct((T // seq, SUBLANES, seq), F32)],
        grid=(T // tm,),
        in_specs=[pl.BlockSpec((tm, D), row),
                  pl.BlockSpec((1, 6, D), lambda i: (i // per_b, 0, 0)),
                  pl.BlockSpec((1, D), const),
                  pl.BlockSpec(w_main.shape, const),
                  pl.BlockSpec(w_gates.shape, const),
                  pl.BlockSpec(w_if.shape, const),
                  pl.BlockSpec(w_ift.shape, const)],
        out_specs=[pl.BlockSpec((tm, dc), row),
                   pl.BlockSpec((tm, 2 * dc), row),
                   pl.BlockSpec((tm, dc), row),
                   pl.BlockSpec((tm, dc), row),
                   pl.BlockSpec((tm, D), row),
                   pl.BlockSpec((tm, D), row),
                   pl.BlockSpec((tm, LANES), row),
                   pl.BlockSpec((1, SUBLANES, tm), lambda i: (i // per_b, 0, i % per_b))],
        compiler_params=_params("arbitrary"),
        name="inproj",
    )(x2, mod3, g1, w_main, w_gates, w_if, w_ift)


def _conv_kernel(u_ref, w_ref, b_ref, lg_ref, lb_ref, wo_ref, y_ref, ubuf, sbuf, cbuf):
    ts = u_ref.shape[1]
    halo = CONV_HALO

    @pl.when(pl.program_id(1) == 0)
    def _():
        ubuf[0:halo, :] = jnp.zeros((halo, ubuf.shape[1]), F32)

    ubuf[halo:halo + ts, :] = u_ref[0]
    ns = sbuf.shape[1]
    for r in range(1, SUBLANES):
        sbuf[r - 1] = ubuf[r:r + ns, :]
    off = halo - (CONV_WIDTH - 1)
    for r0 in range(0, ts, CONV_RC):
        acc = jnp.broadcast_to(b_ref[...], (CONV_RC, ubuf.shape[1]))
        for k in range(CONV_WIDTH):
            r = (off + k) % SUBLANES
            lo = off + k - r + r0
            win = ubuf[lo:lo + CONV_RC, :] if r == 0 else sbuf[r - 1, lo:lo + CONV_RC, :]
            acc = acc + w_ref[k:k + 1, :] * win
        cbuf[r0:r0 + CONV_RC, :] = acc
    ubuf[0:halo, :] = ubuf[ts:ts + halo, :]

    a = cbuf[...]
    mu = jnp.mean(a, axis=-1, keepdims=True)
    ac = a - mu
    var = jnp.mean(ac * ac, axis=-1, keepdims=True)
    z = ac * lax.rsqrt(var + LN_EPS) * lg_ref[...] + lb_ref[...]
    z = z * _sigmoid(z)
    y_ref[0] = jnp.dot(z.astype(BF16), wo_ref[...], preferred_element_type=F32)


def _conv_branch(u3, w, b, lg, lb, wo):
    B, S, C = u3.shape
    D = wo.shape[1]
    ts = CONV_TS
    const = lambda bi, si: (0, 0)
    return pl.pallas_call(
        _conv_kernel,
        out_shape=jax.ShapeDtypeStruct((B, S, D), F32),
        grid=(B, S // ts),
        in_specs=[pl.BlockSpec((1, ts, C), lambda bi, si: (bi, si, 0)),
                  pl.BlockSpec(w.shape, const),
                  pl.BlockSpec((1, C), const),
                  pl.BlockSpec((1, C), const),
                  pl.BlockSpec((1, C), const),
                  pl.BlockSpec(wo.shape, const)],
        out_specs=pl.BlockSpec((1, ts, D), lambda bi, si: (bi, si, 0)),
        scratch_shapes=[pltpu.VMEM((ts + CONV_HALO, C), F32),
                        pltpu.VMEM((SUBLANES - 1, ts + CONV_HALO - SUBLANES, C), F32),
                        pltpu.VMEM((ts, C), F32)],
        compiler_params=_params("arbitrary", "arbitrary"),
        name="conv",
    )(u3, w, b, lg, lb, wo)


def _mlstm_kernel(*refs, n_riders):
    (qk_ref, v_ref, o_ref, ifc_ref, ifr_ref, cw_ref, cb_ref, bifc_ref, bifr_ref, ng_ref, wo_ref) = refs[:11]
    rider_in = refs[11:11 + n_riders]
    y_ref = refs[11 + n_riders]
    rider_out = refs[12 + n_riders:12 + 2 * n_riders]
    qkbuf, cn_ref, m_ref, hbuf = refs[12 + 2 * n_riders:]

    for src, dst in zip(rider_in, rider_out):
        dst[...] = src[...].astype(BF16)

    @pl.when(pl.program_id(1) == 0)
    def _():
        qkbuf[:, 0:SUBLANES, :] = jnp.zeros((qkbuf.shape[0], SUBLANES, qkbuf.shape[2]), F32)
        cn_ref[...] = jnp.zeros(cn_ref.shape, F32)
        m_ref[...] = jnp.zeros(m_ref.shape, F32)

    for b in range(qk_ref.shape[0]):
        _mlstm_chunk(qk_ref.at[b], v_ref.at[b], o_ref.at[b], ifc_ref.at[b], ifr_ref.at[b], cw_ref, cb_ref,
                     bifc_ref, bifr_ref, ng_ref, wo_ref, y_ref.at[b], qkbuf.at[b], cn_ref.at[b],
                     m_ref.at[b], hbuf.at[b])


def _mlstm_chunk(qk_ref, v_ref, o_ref, ifc_ref, ifr_ref, cw_ref, cb_ref, bifc_ref, bifr_ref,
                 ng_ref, wo_ref, y_ref, qkbuf, cn_ref, m_ref, hbuf):
    L = qk_ref.shape[0]
    mi = v_ref.shape[1]
    dh = mi // M_HEADS
    halo = SUBLANES

    qkbuf[halo:halo + L, :] = qk_ref[...]
    off = halo - (QK_CONV_WIDTH - 1)
    y = jnp.broadcast_to(cb_ref[...], (L, qkbuf.shape[1]))
    for k in range(QK_CONV_WIDTH):
        y = y + cw_ref[k:k + 1, :] * qkbuf[off + k:off + k + L, :]
    y = y * _sigmoid(y)
    qkbuf[0:halo, :] = qkbuf[L:L + halo, :]

    ifr = ifr_ref[...] + bifr_ref[...]
    ifc = ifc_ref[...] + bifc_ref[...]
    lfr = _log_sigmoid(ifr)
    lfc = _log_sigmoid(ifc)
    rows = lax.broadcasted_iota(I32, (L, L), 0)
    cols = lax.broadcasted_iota(I32, (L, L), 1)
    causal = cols <= rows
    lower = causal.astype(F32)
    upper = (rows <= cols).astype(F32)
    bcum_c = jnp.dot(lower, lfc, preferred_element_type=F32, precision=lax.Precision.HIGHEST)
    bcum_r = jnp.dot(lfr, upper, preferred_element_type=F32, precision=lax.Precision.HIGHEST)

    lane = lax.broadcasted_iota(I32, (L, dh), 1)
    ones_col = jnp.where(lane == 0, 1.0, 0.0).astype(F32)
    vv = v_ref[...]
    oo = o_ref[...]
    scale = dh ** -0.5
    for hd in range(M_HEADS):
        q = y[:, hd * dh:(hd + 1) * dh] * scale
        kk = y[:, mi + hd * dh:mi + (hd + 1) * dh]
        v = vv[:, hd * dh:(hd + 1) * dh]
        kt = kk.T
        bc = bcum_c[:, M_HEADS + hd:M_HEADS + hd + 1]
        br = bcum_r[M_HEADS + hd:M_HEADS + hd + 1, :]
        li = ifr[hd:hd + 1, :]
        m_prev = m_ref[hd, 0:1, 0:1]
        dmat = jnp.where(causal, bc - br + li, -jnp.inf)
        inter = bc + m_prev
        m_t = jnp.maximum(jnp.max(dmat, axis=-1, keepdims=True), inter)
        wts = jnp.exp(dmat - m_t)
        s_inter = jnp.exp(inter - m_t)
        qb = q.astype(BF16)
        s_mat = jnp.dot(qb, kt.astype(BF16), preferred_element_type=F32) * wts
        cn = cn_ref[hd]
        qcn = jnp.dot(qb, cn.astype(BF16), preferred_element_type=F32)
        num = jnp.dot(s_mat.astype(BF16), v.astype(BF16), preferred_element_type=F32) \
            + s_inter * qcn[:, 0:dh]
        den = jnp.sum(s_mat, axis=-1, keepdims=True) + s_inter * qcn[:, dh:dh + 1]
        hh = num / jnp.maximum(jnp.abs(den), jnp.exp(-m_t))
        b_last = br[:, L - 1:L]
        a = b_last - br + li
        m_new = jnp.maximum(b_last + m_prev, jnp.max(a, axis=-1, keepdims=True))
        wk = jnp.exp(a - m_new)
        sc = jnp.exp(b_last + m_prev - m_new)
        v_ext = jnp.concatenate([v, ones_col], axis=1)
        cn_ref[hd] = sc * cn + jnp.dot((kt * wk).astype(BF16), v_ext.astype(BF16),
                                       preferred_element_type=F32)
        m_ref[hd] = jnp.broadcast_to(m_new, m_ref.shape[1:])
        mu = jnp.mean(hh, axis=-1, keepdims=True)
        hc = hh - mu
        var = jnp.mean(hc * hc, axis=-1, keepdims=True)
        hn = hc * lax.rsqrt(var + LN_EPS) * ng_ref[:, hd * dh:(hd + 1) * dh]
        hbuf[:, hd * dh:(hd + 1) * dh] = hn * _sigmoid(oo[:, hd * dh:(hd + 1) * dh])
    y_ref[...] = jnp.dot(hbuf[...].astype(BF16), wo_ref[...], preferred_element_type=F32)


def _mlstm_branch(qk3, v3, o3, ifc3, ifr3, cw, cb, bifc, bifr, ng, wo, riders):
    B, S, C2 = qk3.shape
    mi = v3.shape[2]
    dh = mi // M_HEADS
    D = wo.shape[1]
    L = MLSTM_L
    nb = MLSTM_SEQS
    ns = S // L
    n_steps = (B // nb) * ns
    const = lambda bi, ci: (0, 0)
    tile = lambda bi, ci: (bi, ci, 0)
    slab = lambda bi, ci: (bi * ns + ci, 0, 0)
    slabs = [r.reshape(n_steps, -1, r.shape[-1]) for r in riders]
    outs = pl.pallas_call(
        functools.partial(_mlstm_kernel, n_riders=len(riders)),
        out_shape=[jax.ShapeDtypeStruct((B, S, D), F32)]
        + [jax.ShapeDtypeStruct(s.shape, BF16) for s in slabs],
        grid=(B // nb, ns),
        in_specs=[pl.BlockSpec((nb, L, C2), tile),
                  pl.BlockSpec((nb, L, mi), tile),
                  pl.BlockSpec((nb, L, mi), tile),
                  pl.BlockSpec((nb, L, LANES), tile),
                  pl.BlockSpec((nb, SUBLANES, L), lambda bi, ci: (bi, 0, ci)),
                  pl.BlockSpec(cw.shape, const),
                  pl.BlockSpec((1, C2), const),
                  pl.BlockSpec((1, LANES), const),
                  pl.BlockSpec((SUBLANES, 1), const),
                  pl.BlockSpec((1, mi), const),
                  pl.BlockSpec(wo.shape, const)]
        + [pl.BlockSpec((1,) + s.shape[1:], slab) for s in slabs],
        out_specs=[pl.BlockSpec((nb, L, D), tile)]
        + [pl.BlockSpec((1,) + s.shape[1:], slab) for s in slabs],
        scratch_shapes=[pltpu.VMEM((nb, L + SUBLANES, C2), F32),
                        pltpu.VMEM((nb, M_HEADS, dh, 2 * dh), F32),
                        pltpu.VMEM((nb, M_HEADS, SUBLANES, LANES), F32),
                        pltpu.VMEM((nb, L, mi), F32)],
        compiler_params=_params("arbitrary", "arbitrary"),
        name="mlstm",
    )(qk3, v3, o3, ifc3, ifr3, cw, cb, bifc, bifr, ng, wo, *slabs)
    return outs[0], [o.reshape(r.shape) for o, r in zip(outs[1:], riders)]


def _merge_kernel(x_ref, ya_ref, yb_ref, sga_ref, sgb_ref, mod_ref, g2_ref, wo_ref, wr_ref, br_ref,
                  x1_ref, h2_ref, ri_ref, rf_ref, cnt_ref, run_ref):
    tm = x_ref.shape[0]

    @pl.when(pl.program_id(0) == 0)
    def _():
        run_ref[...] = jnp.zeros(run_ref.shape, F32)

    gate1 = mod_ref[0, 2:3, :]
    shift2 = mod_ref[0, 3:4, :]
    scale2 = mod_ref[0, 4:5, :]
    merged = sga_ref[...] * ya_ref[...] + sgb_ref[...] * yb_ref[...]
    mix = jnp.dot(merged.astype(BF16), wo_ref[...], preferred_element_type=F32)
    x1 = x_ref[...] + gate1 * mix
    x1_ref[...] = x1
    ms = jnp.mean(x1 * x1, axis=-1, keepdims=True)
    h2 = x1 * lax.rsqrt(ms + RMS_EPS) * g2_ref[...]
    h2 = h2 * (1.0 + scale2) + shift2
    h2_ref[...] = _pack_bf16_pairs(h2)

    logits = jnp.dot(h2.astype(BF16), wr_ref[...], preferred_element_type=F32) + br_ref[...]
    lane = lax.broadcasted_iota(I32, (tm, LANES), 1).astype(F32)
    neg = -jnp.inf

    def first_argmax(vals):
        mx = jnp.max(vals, axis=-1, keepdims=True)
        idx = jnp.min(jnp.where(vals == mx, lane, float(LANES)), axis=-1, keepdims=True)
        return mx, idx

    lg = jnp.where(lane < N_GROUPS, logits, neg)
    gmax, gsel = first_argmax(lg)
    p_g = 1.0 / jnp.sum(jnp.exp(lg - gmax), axis=-1, keepdims=True)
    lo = N_GROUPS + gsel * E_PER_GROUP
    le = jnp.where((lane >= lo) & (lane < lo + E_PER_GROUP), logits, neg)
    l1, i1 = first_argmax(le)
    l2, i2 = first_argmax(jnp.where(lane == i1, neg, le))
    r = jnp.exp(l2 - l1)
    w1 = p_g / (1.0 + r)
    w2 = p_g * r / (1.0 + r)
    e1 = i1 - N_GROUPS
    e2 = i2 - N_GROUPS

    onehot = jnp.where((lane == e1) | (lane == e2), 1.0, 0.0)
    rows = lax.broadcasted_iota(I32, (tm, tm), 0)
    cols = lax.broadcasted_iota(I32, (tm, tm), 1)
    strict = jnp.where(cols < rows, 1.0, 0.0).astype(BF16)
    run = run_ref[0:1, :]
    before = jnp.dot(strict, onehot.astype(BF16), preferred_element_type=F32) + run
    rank1 = jnp.sum(jnp.where(lane == e1, before, 0.0), axis=-1, keepdims=True)
    rank2 = jnp.sum(jnp.where(lane == e2, before, 0.0), axis=-1, keepdims=True)
    run_new = run + jnp.sum(onehot, axis=0, keepdims=True)
    run_ref[...] = jnp.broadcast_to(run_new, run_ref.shape)
    cnt_ref[...] = jnp.broadcast_to(run_new, cnt_ref.shape).astype(I32)

    ri_ref[...] = jnp.where(lane == 0, e1 * float(RANK_RADIX) + rank1,
                            jnp.where(lane == 1, e2 * float(RANK_RADIX) + rank2, 0.0)).astype(I32)
    rf_ref[...] = jnp.where(lane == 0, w1, jnp.where(lane == 1, w2, 0.0))


def _merge(x2, ya, yb, sga, sgb, mod3, g2, wo, wr, br, seq):
    T, D = x2.shape
    tm = MERGE_TM
    per_b = seq // tm
    row = lambda i: (i, 0)
    const = lambda i: (0, 0)
    return pl.pallas_call(
        _merge_kernel,
        out_shape=[jax.ShapeDtypeStruct((T, D), F32),
                   jax.ShapeDtypeStruct((T, D // 2), I32),
                   jax.ShapeDtypeStruct((T, LANES), I32),
                   jax.ShapeDtypeStruct((T, LANES), F32),
                   jax.ShapeDtypeStruct((SUBLANES, LANES), I32)],
        grid=(T // tm,),
        in_specs=[pl.BlockSpec((tm, D), row),
                  pl.BlockSpec((tm, D), row),
                  pl.BlockSpec((tm, D), row),
                  pl.BlockSpec((tm, D), row),
                  pl.BlockSpec((tm, D), row),
                  pl.BlockSpec((1, 6, D), lambda i: (i // per_b, 0, 0)),
                  pl.BlockSpec((1, D), const),
                  pl.BlockSpec(wo.shape, const),
                  pl.BlockSpec(wr.shape, const),
                  pl.BlockSpec((1, LANES), const)],
        out_specs=[pl.BlockSpec((tm, D), row),
                   pl.BlockSpec((tm, D // 2), row),
                   pl.BlockSpec((tm, LANES), row),
                   pl.BlockSpec((tm, LANES), row),
                   pl.BlockSpec((SUBLANES, LANES), const)],
        scratch_shapes=[pltpu.VMEM((SUBLANES, LANES), F32)],
        compiler_params=_params("arbitrary"),
        name="merge",
    )(x2, ya, yb, sga, sgb, mod3, g2, wo, wr, br)


def _sc_workers():
    info = plsc.get_sparse_core_info()
    mesh = plsc.VectorSubcoreMesh(core_axis_name="core", subcore_axis_name="subcore")
    params = pltpu.CompilerParams()
    if "needs_layout_passes" in pltpu.CompilerParams.__dataclass_fields__:
        params = dataclasses.replace(params, needs_layout_passes=False)
    return info, mesh, params


def _rows_from_codes(code_v, base_v, idx_v, lanes):
    for j in range(code_v.shape[0] // lanes):
        c = code_v[pl.ds(j * lanes, lanes)]
        expert = lax.shift_right_logical(c, RANK_BITS)
        idx_v[pl.ds(j * lanes, lanes)] = plsc.load_gather(base_v, [expert]) + (c & (RANK_RADIX - 1))


def _two_slot_loop(n_chunks, start, finish):
    start(0, 0)

    @pl.loop(0, n_chunks, step=2)
    def _(c):
        start(c + 1, 1)
        finish(c, 0)

        @pl.when(c + 2 < n_chunks)
        def _():
            start(c + 2, 0)

        finish(c + 1, 1)


def _sc_dispatch(h2, code0, code1, base, n_rows):
    T, D = h2.shape
    info, mesh, params = _sc_workers()
    n_workers = info.num_cores * info.num_subcores
    w = SC_WINDOW_BYTES // (D * h2.dtype.itemsize)
    per_w = T // n_workers
    n_chunks = per_w // w
    assert per_w * n_workers == T and n_chunks * w == per_w and n_chunks % 2 == 0

    @functools.partial(
        pl.kernel, out_type=jax.ShapeDtypeStruct((n_rows, D), h2.dtype), mesh=mesh, compiler_params=params,
        scratch_types=[pltpu.VMEM((N_EXPERTS,), I32), pltpu.VMEM((w,), I32), pltpu.VMEM((w,), I32),
                       pltpu.VMEM((w,), I32), pltpu.VMEM((w, D), h2.dtype), pltpu.VMEM((w, D), h2.dtype),
                       pltpu.SemaphoreType.DMA, pltpu.SemaphoreType.DMA])
    def scatter(h_hbm, c0_hbm, c1_hbm, b_hbm, xs_hbm, base_v, code_v, i0_v, i1_v, rows0, rows1, sem0, sem1):
        wid = lax.axis_index("subcore") * info.num_cores + lax.axis_index("core")
        w0 = wid * per_w
        pltpu.sync_copy(b_hbm, base_v)
        rows = (rows0, rows1)
        sems = (sem0, sem1)

        def start(c, slot):
            pltpu.async_copy(h_hbm.at[pl.ds(w0 + c * w, w)], rows[slot], sems[slot])

        def finish(c, slot):
            pltpu.sync_copy(c0_hbm.at[pl.ds(w0 + c * w, w)], code_v)
            _rows_from_codes(code_v, base_v, i0_v, info.num_lanes)
            pltpu.sync_copy(c1_hbm.at[pl.ds(w0 + c * w, w)], code_v)
            _rows_from_codes(code_v, base_v, i1_v, info.num_lanes)
            pltpu.make_async_copy(h_hbm.at[pl.ds(w0 + c * w, w)], rows[slot], sems[slot]).wait()
            pltpu.sync_copy(rows[slot], xs_hbm.at[i0_v])
            pltpu.sync_copy(rows[slot], xs_hbm.at[i1_v])

        _two_slot_loop(n_chunks, start, finish)

    return scatter(h2, code0, code1, base)


def _sc_collect(ys, codes, base):
    n = codes.shape[0]
    D = ys.shape[1]
    info, mesh, params = _sc_workers()
    n_workers = info.num_cores * info.num_subcores
    w = SC_WINDOW_BYTES // (D * ys.dtype.itemsize)
    per_w = n // n_workers
    n_chunks = per_w // w
    assert per_w * n_workers == n and n_chunks * w == per_w and n_chunks % 2 == 0

    @functools.partial(
        pl.kernel, out_type=jax.ShapeDtypeStruct((n, D), ys.dtype), mesh=mesh, compiler_params=params,
        scratch_types=[pltpu.VMEM((N_EXPERTS,), I32), pltpu.VMEM((w,), I32), pltpu.VMEM((w,), I32),
                       pltpu.VMEM((w,), I32), pltpu.VMEM((w, D), ys.dtype), pltpu.VMEM((w, D), ys.dtype),
                       pltpu.SemaphoreType.DMA, pltpu.SemaphoreType.DMA])
    def gather(ys_hbm, c_hbm, b_hbm, yk_hbm, base_v, code_v, i0_v, i1_v, rows0, rows1, sem0, sem1):
        wid = lax.axis_index("subcore") * info.num_cores + lax.axis_index("core")
        w0 = wid * per_w
        pltpu.sync_copy(b_hbm, base_v)
        idx = (i0_v, i1_v)
        rows = (rows0, rows1)
        sems = (sem0, sem1)

        def start(c, slot):
            pltpu.sync_copy(c_hbm.at[pl.ds(w0 + c * w, w)], code_v)
            _rows_from_codes(code_v, base_v, idx[slot], info.num_lanes)
            pltpu.async_copy(ys_hbm.at[idx[slot]], rows[slot], sems[slot])

        def finish(c, slot):
            pltpu.make_async_copy(ys_hbm.at[idx[slot]], rows[slot], sems[slot]).wait()
            pltpu.sync_copy(rows[slot], yk_hbm.at[pl.ds(w0 + c * w, w)])

        _two_slot_loop(n_chunks, start, finish)

    return gather(ys, codes, base)


def _schedule_kernel(cnt_ref, te_ref, tb_ref, base_ref, nt_ref):
    tm = EXPERT_TM

    def expert(e, t0):
        n = (cnt_ref[e] + tm - 1) // tm
        base_ref[e] = t0 * tm

        def tile(t, c):
            te_ref[t] = e
            tb_ref[t] = t
            return c

        lax.fori_loop(t0, t0 + n, tile, 0)
        return t0 + n

    nt = lax.fori_loop(0, N_EXPERTS, expert, 0)
    nt_ref[0] = nt
    last = te_ref[nt - 1]

    def idle(t, c):
        te_ref[t] = last
        tb_ref[t] = nt - 1
        return c

    lax.fori_loop(nt, te_ref.shape[0], idle, 0)


def _schedule(counts, max_tiles):
    smem = pl.BlockSpec(memory_space=pltpu.SMEM)
    return pl.pallas_call(
        _schedule_kernel,
        out_shape=[jax.ShapeDtypeStruct((max_tiles,), I32),
                   jax.ShapeDtypeStruct((max_tiles,), I32),
                   jax.ShapeDtypeStruct((N_EXPERTS,), I32),
                   jax.ShapeDtypeStruct((1,), I32)],
        in_specs=[smem],
        out_specs=[smem, smem, smem, smem],
        name="schedule",
    )(counts)


def _expert_kernel(te_ref, tb_ref, nt_ref, xs_ref, wg_ref, wu_ref, wd_ref, ys_ref):
    @pl.when(pl.program_id(0) < nt_ref[0])
    def _():
        xb = _unpack_bf16_pairs(xs_ref[...])
        g = jnp.dot(xb, wg_ref[0], preferred_element_type=F32)
        u = jnp.dot(xb, wu_ref[0], preferred_element_type=F32)
        act = (g * _sigmoid(g)) * u
        ys_ref[...] = jnp.dot(act.astype(BF16), wd_ref[0], preferred_element_type=F32)


def _experts(tile_e, tile_b, n_tiles, xs, wg, wu, wd, max_tiles):
    P = xs.shape[0]
    D, de = wg.shape[1:]
    tm = EXPERT_TM
    wmap = lambda j, te, tb, nt: (te[j], 0, 0)
    rmap = lambda j, te, tb, nt: (tb[j], 0)
    return pl.pallas_call(
        _expert_kernel,
        out_shape=jax.ShapeDtypeStruct((P, D), F32),
        grid_spec=pltpu.PrefetchScalarGridSpec(
            num_scalar_prefetch=3,
            grid=(max_tiles,),
            in_specs=[pl.BlockSpec((tm, xs.shape[1]), rmap),
                      pl.BlockSpec((1, D, de), wmap),
                      pl.BlockSpec((1, D, de), wmap),
                      pl.BlockSpec((1, de, D), wmap)],
            out_specs=pl.BlockSpec((tm, D), rmap)),
        compiler_params=_params("arbitrary"),
        name="experts",
    )(tile_e, tile_b, n_tiles, xs, wg, wu, wd)


def _combine_kernel(x1_ref, rf_ref, mod_ref, gf_ref, y0_ref, y1_ref, out_ref, *, final_norm):
    gate2 = mod_ref[0, 5:6, :]
    w = rf_ref[...]
    moe = w[:, 0:1] * y0_ref[...] + w[:, 1:2] * y1_ref[...]
    x2 = x1_ref[...] + gate2 * moe
    if final_norm:
        ms = jnp.mean(x2 * x2, axis=-1, keepdims=True)
        x2 = x2 * lax.rsqrt(ms + RMS_EPS) * gf_ref[...]
    out_ref[...] = x2


def _combine(x1, rf, mod3, gf, yk, seq, final_norm):
    T, D = x1.shape
    tc = COMBINE_TM
    per_b = seq // tc
    n_blk = T // tc
    return pl.pallas_call(
        functools.partial(_combine_kernel, final_norm=final_norm),
        out_shape=jax.ShapeDtypeStruct((T, D), F32),
        grid=(n_blk,),
        in_specs=[pl.BlockSpec((tc, D), lambda i: (i, 0)),
                  pl.BlockSpec((tc, LANES), lambda i: (i, 0)),
                  pl.BlockSpec((1, 6, D), lambda i: (i // per_b, 0, 0)),
                  pl.BlockSpec((1, D), lambda i: (0, 0)),
                  pl.BlockSpec((tc, D), lambda i: (i, 0)),
                  pl.BlockSpec((tc, D), lambda i: (n_blk + i, 0))],
        out_specs=pl.BlockSpec((tc, D), lambda i: (i, 0)),
        compiler_params=_params("arbitrary"),
        name="combine",
    )(x1, rf, mod3, gf, yk, yk)


def _layer(x2, c, seq, w_ada, b_ada, g_norm1, w_in, b_if, conv_dw_w, conv_dw_b, conv_ln_g, conv_ln_b,
           w_conv_out, qk_conv_w, qk_conv_b, m_norm_g, w_m_out, w_out, g_norm2, w_rg, b_rg,
           w_re, b_re, w_e_gate, w_e_up, w_e_down):
    T, D = x2.shape
    B = T // seq
    dc = D // 2
    nif = 2 * M_HEADS

    mod3 = _ada(c, w_ada, b_ada).reshape(B, 6, D)

    if_lo = 6 * dc
    w_main = w_in[:, :if_lo].astype(BF16)
    w_gates = w_in[:, if_lo + nif:].astype(BF16)
    w_if = w_in[:, if_lo:if_lo + nif]
    w_if_pad = jnp.pad(w_if, ((0, 0), (0, LANES - nif))).astype(BF16)
    w_ift = w_if.T.astype(BF16)
    u, qk, v, o, sga, sgb, ifc, ifr = _inproj(x2, mod3, g_norm1.reshape(1, D), w_main, w_gates,
                                              w_if_pad, w_ift, seq)

    ya = _conv_branch(u.reshape(B, seq, dc), conv_dw_w, conv_dw_b.reshape(1, dc),
                      conv_ln_g.reshape(1, dc), conv_ln_b.reshape(1, dc), w_conv_out.astype(BF16))
    bifc = jnp.pad(b_if, (0, LANES - nif)).reshape(1, LANES)
    bifr = b_if.reshape(nif, 1)
    yb, (wg_b, wu_b, wd_b) = _mlstm_branch(
        qk.reshape(B, seq, 2 * dc), v.reshape(B, seq, dc), o.reshape(B, seq, dc),
        ifc.reshape(B, seq, LANES), ifr, qk_conv_w, qk_conv_b.reshape(1, 2 * dc), bifc, bifr,
        m_norm_g.reshape(1, dc), w_m_out.astype(BF16), riders=(w_e_gate, w_e_up, w_e_down))

    n_r = N_GROUPS + N_EXPERTS
    w_r = jnp.pad(jnp.concatenate([w_rg, w_re], axis=1), ((0, 0), (0, LANES - n_r))).astype(BF16)
    b_r = jnp.pad(jnp.concatenate([b_rg, b_re]), (0, LANES - n_r)).reshape(1, LANES)
    x1, h2, ri, rf, cnt = _merge(x2, ya.reshape(T, D), yb.reshape(T, D), sga, sgb, mod3,
                                 g_norm2.reshape(1, D), w_out.astype(BF16), w_r, b_r, seq)

    tm = EXPERT_TM
    max_tiles = (T * TOP_K) // tm + N_EXPERTS
    tile_e, tile_b, base, n_tiles = _schedule(cnt[0, :N_EXPERTS], max_tiles)

    code0 = ri[:, 0]
    code1 = ri[:, 1]
    xs = _sc_dispatch(h2, code0, code1, base, max_tiles * tm)
    ys = _experts(tile_e, tile_b, n_tiles, xs, wg_b, wu_b, wd_b, max_tiles)
    return x1, rf, mod3, ys, code0, code1, base


def kernel(x, c, w_ada, b_ada, g_norm1, w_in, b_if, conv_dw_w, conv_dw_b, conv_ln_g, conv_ln_b,
           w_conv_out, qk_conv_w, qk_conv_b, m_norm_g, w_m_out, w_out, g_norm2, w_rg, b_rg,
           w_re, b_re, w_e_gate, w_e_up, w_e_down, g_final):
    B, S, D = x.shape
    depth = w_ada.shape[0]
    x2 = x.reshape(B * S, D)
    for l in range(depth):
        x1, rf, mod3, ys, code0, code1, base = _layer(
            x2, c, S, w_ada[l], b_ada[l], g_norm1[l], w_in[l], b_if[l], conv_dw_w[l], conv_dw_b[l],
            conv_ln_g[l], conv_ln_b[l], w_conv_out[l], qk_conv_w[l], qk_conv_b[l], m_norm_g[l],
            w_m_out[l], w_out[l], g_norm2[l], w_rg[l], b_rg[l], w_re[l], b_re[l],
            w_e_gate[l], w_e_up[l], w_e_down[l])
        yk = _sc_collect(ys, jnp.concatenate([code0, code1]), base)
        x2 = _combine(x1, rf, mod3, g_final.reshape(1, D), yk, S, final_norm=l == depth - 1)
    return x2.reshape(B, S, D)
```

```python
import dataclasses
import functools

import jax
import jax.numpy as jnp
from jax import lax
from jax.experimental import pallas as pl
from jax.experimental.pallas import tpu as pltpu
from jax.experimental.pallas import tpu_sc as plsc

F32 = jnp.float32
BF16 = jnp.bfloat16
I32 = jnp.int32

M_HEADS = 4
CONV_WIDTH = 31
QK_CONV_WIDTH = 4
N_GROUPS = 4
E_PER_GROUP = 8
N_EXPERTS = N_GROUPS * E_PER_GROUP
TOP_K = 2
RMS_EPS = 1e-6
LN_EPS = 1e-5

LANES = 128
SUBLANES = 8
VMEM_LIMIT = 56 * 1024 * 1024

ADA_TN = 1024
INPROJ_TM = 256
CONV_TS = 512
CONV_HALO = 32
CONV_RC = 32
MLSTM_L = 128
MLSTM_SEQS = 2
MERGE_TM = 256
EXPERT_TM = 512
SC_WINDOW_BYTES = 128 * 1024
COMBINE_TM = 256
RANK_BITS = 16
RANK_RADIX = 1 << RANK_BITS
assert EXPERT_TM & (EXPERT_TM - 1) == 0


def _sigmoid(v):
    return 1.0 / (1.0 + jnp.exp(-v))


def _log_sigmoid(v):
    return -(jnp.maximum(-v, 0.0) + jnp.log1p(jnp.exp(-jnp.abs(v))))


def _pack_bf16_pairs(v):
    n = v.shape[1] // 2
    bits = lax.bitcast_convert_type(v.astype(BF16).astype(F32), jnp.uint32)
    word = bits[:, n:] | (bits[:, :n] >> 16)
    return lax.bitcast_convert_type(word, I32)


def _unpack_bf16_pairs(w):
    bits = lax.bitcast_convert_type(w, jnp.uint32)
    lo = lax.bitcast_convert_type(bits << 16, F32)
    hi = lax.bitcast_convert_type(bits & jnp.uint32(0xFFFF0000), F32)
    return jnp.concatenate([lo, hi], axis=1).astype(BF16)


def _params(*sem):
    return pltpu.CompilerParams(dimension_semantics=sem, vmem_limit_bytes=VMEM_LIMIT)


def _ada_kernel(c_ref, w_ref, b_ref, o_ref):
    c = c_ref[...]
    s = c * _sigmoid(c)
    o_ref[...] = jnp.dot(s, w_ref[...], preferred_element_type=F32,
                         precision=lax.Precision.HIGHEST) + b_ref[...]


def _ada(c, w_ada, b_ada):
    B, D = c.shape
    N = w_ada.shape[1]
    return pl.pallas_call(
        _ada_kernel,
        out_shape=jax.ShapeDtypeStruct((B, N), F32),
        grid=(N // ADA_TN,),
        in_specs=[pl.BlockSpec((B, D), lambda j: (0, 0)),
                  pl.BlockSpec((D, ADA_TN), lambda j: (0, j)),
                  pl.BlockSpec((1, ADA_TN), lambda j: (0, j))],
        out_specs=pl.BlockSpec((B, ADA_TN), lambda j: (0, j)),
        compiler_params=_params("arbitrary"),
        name="ada",
    )(c, w_ada, b_ada.reshape(1, N))


def _inproj_kernel(x_ref, mod_ref, g_ref, wm_ref, wgt_ref, wif_ref, wift_ref,
                   u_ref, qk_ref, v_ref, o_ref, sga_ref, sgb_ref, ifc_ref, ifr_ref):
    x = x_ref[...]
    shift = mod_ref[0, 0:1, :]
    scale = mod_ref[0, 1:2, :]
    ms = jnp.mean(x * x, axis=-1, keepdims=True)
    h = x * lax.rsqrt(ms + RMS_EPS) * g_ref[...]
    h = h * (1.0 + scale) + shift
    hb = h.astype(BF16)
    dc = u_ref.shape[1]
    d = sga_ref.shape[1]

    def seg(lo, hi):
        return jnp.dot(hb, wm_ref[:, lo:hi], preferred_element_type=F32)

    u_ref[...] = seg(0, dc) * _sigmoid(seg(dc, 2 * dc))
    qk_ref[...] = seg(2 * dc, 4 * dc)
    v_ref[...] = seg(4 * dc, 5 * dc).astype(BF16)
    o_ref[...] = seg(5 * dc, 6 * dc)
    sga_ref[...] = _sigmoid(jnp.dot(hb, wgt_ref[:, 0:d], preferred_element_type=F32)).astype(BF16)
    sgb_ref[...] = _sigmoid(jnp.dot(hb, wgt_ref[:, d:2 * d], preferred_element_type=F32)).astype(BF16)
    ifc_ref[...] = jnp.dot(hb, wif_ref[...], preferred_element_type=F32)
    ifr_ref[0] = lax.dot_general(wift_ref[...], hb, (((1,), (1,)), ((), ())),
                                 preferred_element_type=F32)


def _inproj(x2, mod3, g1, w_main, w_gates, w_if, w_ift, seq):
    T, D = x2.shape
    tm = INPROJ_TM
    dc = D // 2
    per_b = seq // tm
    row = lambda i: (i, 0)
    const = lambda i: (0, 0)
    return pl.pallas_call(
        _inproj_kernel,
        out_shape=[jax.ShapeDtypeStruct((T, dc), F32),
                   jax.ShapeDtypeStruct((T, 2 * dc), F32),
                   jax.ShapeDtypeStruct((T, dc), BF16),
                   jax.ShapeDtypeStruct((T, dc), F32),
                   jax.ShapeDtypeStruct((T, D), BF16),
                   jax.ShapeDtypeStruct((T, D), BF16),
                   jax.ShapeDtypeStruct((T, LANES), F32),
                   jax.ShapeDtypeStruct((T // seq, SUBLANES, seq), F32)],
        grid=(T // tm,),
        in_specs=[pl.BlockSpec((tm, D), row),
                  pl.BlockSpec((1, 6, D), lambda i: (i // per_b, 0, 0)),
                  pl.BlockSpec((1, D), const),
                  pl.BlockSpec(w_main.shape, const),
                  pl.BlockSpec(w_gates.shape, const),
                  pl.BlockSpec(w_if.shape, const),
                  pl.BlockSpec(w_ift.shape, const)],
        out_specs=[pl.BlockSpec((tm, dc), row),
                   pl.BlockSpec((tm, 2 * dc), row),
                   pl.BlockSpec((tm, dc), row),
                   pl.BlockSpec((tm, dc), row),
                   pl.BlockSpec((tm, D), row),
                   pl.BlockSpec((tm, D), row),
                   pl.BlockSpec((tm, LANES), row),
                   pl.BlockSpec((1, SUBLANES, tm), lambda i: (i // per_b, 0, i % per_b))],
        compiler_params=_params("arbitrary"),
        name="inproj",
    )(x2, mod3, g1, w_main, w_gates, w_if, w_ift)


def _conv_kernel(u_ref, w_ref, b_ref, lg_ref, lb_ref, wo_ref, y_ref, ubuf, sbuf, cbuf):
    ts = u_ref.shape[1]
    halo = CONV_HALO

    @pl.when(pl.program_id(1) == 0)
    def _():
        ubuf[0:halo, :] = jnp.zeros((halo, ubuf.shape[1]), F32)

    ubuf[halo:halo + ts, :] = u_ref[0]
    ns = sbuf.shape[1]
    for r in range(1, SUBLANES):
        sbuf[r - 1] = ubuf[r:r + ns, :]
    off = halo - (CONV_WIDTH - 1)
    for r0 in range(0, ts, CONV_RC):
        acc = jnp.broadcast_to(b_ref[...], (CONV_RC, ubuf.shape[1]))
        for k in range(CONV_WIDTH):
            r = (off + k) % SUBLANES
            lo = off + k - r + r0
            win = ubuf[lo:lo + CONV_RC, :] if r == 0 else sbuf[r - 1, lo:lo + CONV_RC, :]
            acc = acc + w_ref[k:k + 1, :] * win
        cbuf[r0:r0 + CONV_RC, :] = acc
    ubuf[0:halo, :] = ubuf[ts:ts + halo, :]

    a = cbuf[...]
    mu = jnp.mean(a, axis=-1, keepdims=True)
    ac = a - mu
    var = jnp.mean(ac * ac, axis=-1, keepdims=True)
    z = ac * lax.rsqrt(var + LN_EPS) * lg_ref[...] + lb_ref[...]
    z = z * _sigmoid(z)
    y_ref[0] = jnp.dot(z.astype(BF16), wo_ref[...], preferred_element_type=F32).astype(BF16)


def _conv_branch(u3, w, b, lg, lb, wo):
    B, S, C = u3.shape
    D = wo.shape[1]
    ts = CONV_TS
    const = lambda bi, si: (0, 0)
    return pl.pallas_call(
        _conv_kernel,
        out_shape=jax.ShapeDtypeStruct((B, S, D), BF16),
        grid=(B, S // ts),
        in_specs=[pl.BlockSpec((1, ts, C), lambda bi, si: (bi, si, 0)),
                  pl.BlockSpec(w.shape, const),
                  pl.BlockSpec((1, C), const),
                  pl.BlockSpec((1, C), const),
                  pl.BlockSpec((1, C), const),
                  pl.BlockSpec(wo.shape, const)],
        out_specs=pl.BlockSpec((1, ts, D), lambda bi, si: (bi, si, 0)),
        scratch_shapes=[pltpu.VMEM((ts + CONV_HALO, C), F32),
                        pltpu.VMEM((SUBLANES - 1, ts + CONV_HALO - SUBLANES, C), F32),
                        pltpu.VMEM((ts, C), F32)],
        compiler_params=_params("arbitrary", "arbitrary"),
        name="conv",
    )(u3, w, b, lg, lb, wo)


def _mlstm_kernel(*refs, n_riders):
    (qk_ref, v_ref, o_ref, ifc_ref, ifr_ref, cw_ref, cb_ref, bifc_ref, bifr_ref, ng_ref, wo_ref) = refs[:11]
    rider_in = refs[11:11 + n_riders]
    y_ref = refs[11 + n_riders]
    rider_out = refs[12 + n_riders:12 + 2 * n_riders]
    qkbuf, cn_ref, m_ref, hbuf = refs[12 + 2 * n_riders:]

    for src, dst in zip(rider_in, rider_out):
        dst[...] = src[...].astype(BF16)

    @pl.when(pl.program_id(1) == 0)
    def _():
        qkbuf[:, 0:SUBLANES, :] = jnp.zeros((qkbuf.shape[0], SUBLANES, qkbuf.shape[2]), F32)
        cn_ref[...] = jnp.zeros(cn_ref.shape, F32)
        m_ref[...] = jnp.zeros(m_ref.shape, F32)

    for b in range(hbuf.shape[0]):
        _mlstm_chunk(qk_ref.at[b], v_ref.at[b], o_ref.at[b], ifc_ref.at[b], ifr_ref.at[b], cw_ref, cb_ref,
                     bifc_ref, bifr_ref, ng_ref, qkbuf.at[b], cn_ref.at[b], m_ref.at[b], hbuf.at[b])
        y = jnp.dot(hbuf[b].astype(BF16), wo_ref[...], preferred_element_type=F32)
        y_ref[b] = y.astype(BF16)


def _mlstm_chunk(qk_ref, v_ref, o_ref, ifc_ref, ifr_ref, cw_ref, cb_ref, bifc_ref, bifr_ref,
                 ng_ref, qkbuf, cn_ref, m_ref, hbuf):
    L = qk_ref.shape[0]
    mi = v_ref.shape[1]
    dh = mi // M_HEADS
    halo = SUBLANES

    qkbuf[halo:halo + L, :] = qk_ref[...]
    off = halo - (QK_CONV_WIDTH - 1)
    y = jnp.broadcast_to(cb_ref[...], (L, qkbuf.shape[1]))
    for k in range(QK_CONV_WIDTH):
        y = y + cw_ref[k:k + 1, :] * qkbuf[off + k:off + k + L, :]
    y = y * _sigmoid(y)
    qkbuf[0:halo, :] = qkbuf[L:L + halo, :]

    ifr = ifr_ref[...] + bifr_ref[...]
    ifc = ifc_ref[...] + bifc_ref[...]
    lfr = _log_sigmoid(ifr)
    lfc = _log_sigmoid(ifc)
    rows = lax.broadcasted_iota(I32, (L, L), 0)
    cols = lax.broadcasted_iota(I32, (L, L), 1)
    causal = cols <= rows
    lower = causal.astype(F32)
    upper = (rows <= cols).astype(F32)
    bcum_c = jnp.dot(lower, lfc, preferred_element_type=F32, precision=lax.Precision.HIGHEST)
    bcum_r = jnp.dot(lfr, upper, preferred_element_type=F32, precision=lax.Precision.HIGHEST)

    lane = lax.broadcasted_iota(I32, (L, dh), 1)
    ones_col = jnp.where(lane == 0, 1.0, 0.0).astype(BF16)
    vv = v_ref[...]
    oo = o_ref[...]
    scale = dh ** -0.5
    for hd in range(M_HEADS):
        q = y[:, hd * dh:(hd + 1) * dh] * scale
        kk = y[:, mi + hd * dh:mi + (hd + 1) * dh]
        v = vv[:, hd * dh:(hd + 1) * dh]
        kt = kk.T
        bc = bcum_c[:, M_HEADS + hd:M_HEADS + hd + 1]
        br = bcum_r[M_HEADS + hd:M_HEADS + hd + 1, :]
        li = ifr[hd:hd + 1, :]
        m_prev = m_ref[hd, 0:1, 0:1]
        dmat = jnp.where(causal, bc - br + li, -jnp.inf)
        inter = bc + m_prev
        m_t = jnp.maximum(jnp.max(dmat, axis=-1, keepdims=True), inter)
        wts = jnp.exp(dmat - m_t)
        s_inter = jnp.exp(inter - m_t)
        qb = q.astype(BF16)
        s_mat = jnp.dot(qb, kt.astype(BF16), preferred_element_type=F32) * wts
        cn = cn_ref[hd]
        qcn = jnp.dot(qb, cn.astype(BF16), preferred_element_type=F32)
        num = jnp.dot(s_mat.astype(BF16), v, preferred_element_type=F32) \
            + s_inter * qcn[:, 0:dh]
        den = jnp.sum(s_mat, axis=-1, keepdims=True) + s_inter * qcn[:, dh:dh + 1]
        hh = num / jnp.maximum(jnp.abs(den), jnp.exp(-m_t))
        b_last = br[:, L - 1:L]
        a = b_last - br + li
        m_new = jnp.maximum(b_last + m_prev, jnp.max(a, axis=-1, keepdims=True))
        wk = jnp.exp(a - m_new)
        sc = jnp.exp(b_last + m_prev - m_new)
        v_ext = jnp.concatenate([v, ones_col], axis=1)
        cn_ref[hd] = sc * cn + jnp.dot((kt * wk).astype(BF16), v_ext, preferred_element_type=F32)
        m_ref[hd] = jnp.broadcast_to(m_new, m_ref.shape[1:])
        mu = jnp.mean(hh, axis=-1, keepdims=True)
        hc = hh - mu
        var = jnp.mean(hc * hc, axis=-1, keepdims=True)
        hn = hc * lax.rsqrt(var + LN_EPS) * ng_ref[:, hd * dh:(hd + 1) * dh]
        hbuf[:, hd * dh:(hd + 1) * dh] = hn * _sigmoid(oo[:, hd * dh:(hd + 1) * dh])


def _mlstm_branch(qk3, v3, o3, ifc3, ifr3, cw, cb, bifc, bifr, ng, wo, riders):
    B, S, C2 = qk3.shape
    mi = v3.shape[2]
    dh = mi // M_HEADS
    D = wo.shape[1]
    L = MLSTM_L
    nb = MLSTM_SEQS
    ns = S // L
    n_steps = (B // nb) * ns
    const = lambda bi, ci: (0, 0)
    tile = lambda bi, ci: (bi, ci, 0)
    slab = lambda bi, ci: (bi * ns + ci, 0, 0)
    slabs = [r.reshape(n_steps, -1, r.shape[-1]) for r in riders]
    outs = pl.pallas_call(
        functools.partial(_mlstm_kernel, n_riders=len(riders)),
        out_shape=[jax.ShapeDtypeStruct((B, S, D), BF16)]
        + [jax.ShapeDtypeStruct(s.shape, BF16) for s in slabs],
        grid=(B // nb, ns),
        in_specs=[pl.BlockSpec((nb, L, C2), tile),
                  pl.BlockSpec((nb, L, mi), tile),
                  pl.BlockSpec((nb, L, mi), tile),
                  pl.BlockSpec((nb, L, LANES), tile),
                  pl.BlockSpec((nb, SUBLANES, L), lambda bi, ci: (bi, 0, ci)),
                  pl.BlockSpec(cw.shape, const),
                  pl.BlockSpec((1, C2), const),
                  pl.BlockSpec((1, LANES), const),
                  pl.BlockSpec((SUBLANES, 1), const),
                  pl.BlockSpec((1, mi), const),
                  pl.BlockSpec(wo.shape, const)]
        + [pl.BlockSpec((1,) + s.shape[1:], slab) for s in slabs],
        out_specs=[pl.BlockSpec((nb, L, D), tile)]
        + [pl.BlockSpec((1,) + s.shape[1:], slab) for s in slabs],
        scratch_shapes=[pltpu.VMEM((nb, L + SUBLANES, C2), F32),
                        pltpu.VMEM((nb, M_HEADS, dh, 2 * dh), F32),
                        pltpu.VMEM((nb, M_HEADS, SUBLANES, LANES), F32),
                        pltpu.VMEM((nb, L, mi), F32)],
        compiler_params=_params("arbitrary", "arbitrary"),
        name="mlstm",
    )(qk3, v3, o3, ifc3, ifr3, cw, cb, bifc, bifr, ng, wo, *slabs)
    return outs[0], [o.reshape(r.shape) for o, r in zip(outs[1:], riders)]


def _merge_kernel(x_ref, ya_ref, yb_ref, sga_ref, sgb_ref, mod_ref, g2_ref, wo_ref, wr_ref, br_ref,
                  x1_ref, h2_ref, ri_ref, rf_ref, cnt_ref, run_ref):
    tm = x_ref.shape[0]

    @pl.when(pl.program_id(0) == 0)
    def _():
        run_ref[...] = jnp.zeros(run_ref.shape, F32)

    gate1 = mod_ref[0, 2:3, :]
    shift2 = mod_ref[0, 3:4, :]
    scale2 = mod_ref[0, 4:5, :]
    merged = (sga_ref[...].astype(F32) * ya_ref[...].astype(F32)
              + sgb_ref[...].astype(F32) * yb_ref[...].astype(F32))
    mix = jnp.dot(merged.astype(BF16), wo_ref[...], preferred_element_type=F32)
    x1 = x_ref[...] + gate1 * mix
    x1_ref[...] = x1
    ms = jnp.mean(x1 * x1, axis=-1, keepdims=True)
    h2 = x1 * lax.rsqrt(ms + RMS_EPS) * g2_ref[...]
    h2 = h2 * (1.0 + scale2) + shift2
    h2_ref[...] = _pack_bf16_pairs(h2)

    logits = jnp.dot(h2.astype(BF16), wr_ref[...], preferred_element_type=F32) + br_ref[...]
    lane = lax.broadcasted_iota(I32, (tm, LANES), 1).astype(F32)
    neg = -jnp.inf

    def first_argmax(vals):
        mx = jnp.max(vals, axis=-1, keepdims=True)
        idx = jnp.min(jnp.where(vals == mx, lane, float(LANES)), axis=-1, keepdims=True)
        return mx, idx

    lg = jnp.where(lane < N_GROUPS, logits, neg)
    gmax, gsel = first_argmax(lg)
    p_g = 1.0 / jnp.sum(jnp.exp(lg - gmax), axis=-1, keepdims=True)
    lo = N_GROUPS + gsel * E_PER_GROUP
    le = jnp.where((lane >= lo) & (lane < lo + E_PER_GROUP), logits, neg)
    l1, i1 = first_argmax(le)
    l2, i2 = first_argmax(jnp.where(lane == i1, neg, le))
    r = jnp.exp(l2 - l1)
    w1 = p_g / (1.0 + r)
    w2 = p_g * r / (1.0 + r)
    e1 = i1 - N_GROUPS
    e2 = i2 - N_GROUPS

    onehot = jnp.where((lane == e1) | (lane == e2), 1.0, 0.0)
    rows = lax.broadcasted_iota(I32, (tm, tm), 0)
    cols = lax.broadcasted_iota(I32, (tm, tm), 1)
    strict = jnp.where(cols < rows, 1.0, 0.0).astype(BF16)
    run = run_ref[0:1, :]
    before = jnp.dot(strict, onehot.astype(BF16), preferred_element_type=F32) + run
    rank1 = jnp.sum(jnp.where(lane == e1, before, 0.0), axis=-1, keepdims=True)
    rank2 = jnp.sum(jnp.where(lane == e2, before, 0.0), axis=-1, keepdims=True)
    run_new = run + jnp.sum(onehot, axis=0, keepdims=True)
    run_ref[...] = jnp.broadcast_to(run_new, run_ref.shape)
    cnt_ref[...] = jnp.broadcast_to(run_new, cnt_ref.shape).astype(I32)

    ri_ref[...] = jnp.where(lane == 0, e1 * float(RANK_RADIX) + rank1,
                            jnp.where(lane == 1, e2 * float(RANK_RADIX) + rank2, 0.0)).astype(I32)
    rf_ref[...] = jnp.where(lane == 0, w1, jnp.where(lane == 1, w2, 0.0))


def _merge(x2, ya, yb, sga, sgb, mod3, g2, wo, wr, br, seq):
    T, D = x2.shape
    tm = MERGE_TM
    per_b = seq // tm
    row = lambda i: (i, 0)
    const = lambda i: (0, 0)
    return pl.pallas_call(
        _merge_kernel,
        out_shape=[jax.ShapeDtypeStruct((T, D), F32),
                   jax.ShapeDtypeStruct((T, D // 2), I32),
                   jax.ShapeDtypeStruct((T, LANES), I32),
                   jax.ShapeDtypeStruct((T, LANES), F32),
                   jax.ShapeDtypeStruct((SUBLANES, LANES), I32)],
        grid=(T // tm,),
        in_specs=[pl.BlockSpec((tm, D), row),
                  pl.BlockSpec((tm, D), row),
                  pl.BlockSpec((tm, D), row),
                  pl.BlockSpec((tm, D), row),
                  pl.BlockSpec((tm, D), row),
                  pl.BlockSpec((1, 6, D), lambda i: (i // per_b, 0, 0)),
                  pl.BlockSpec((1, D), const),
                  pl.BlockSpec(wo.shape, const),
                  pl.BlockSpec(wr.shape, const),
                  pl.BlockSpec((1, LANES), const)],
        out_specs=[pl.BlockSpec((tm, D), row),
                   pl.BlockSpec((tm, D // 2), row),
                   pl.BlockSpec((tm, LANES), row),
                   pl.BlockSpec((tm, LANES), row),
                   pl.BlockSpec((SUBLANES, LANES), const)],
        scratch_shapes=[pltpu.VMEM((SUBLANES, LANES), F32)],
        compiler_params=_params("arbitrary"),
        name="merge",
    )(x2, ya, yb, sga, sgb, mod3, g2, wo, wr, br)


def _sc_workers():
    info = plsc.get_sparse_core_info()
    mesh = plsc.VectorSubcoreMesh(core_axis_name="core", subcore_axis_name="subcore")
    params = pltpu.CompilerParams()
    if "needs_layout_passes" in pltpu.CompilerParams.__dataclass_fields__:
        params = dataclasses.replace(params, needs_layout_passes=False)
    return info, mesh, params


def _rows_from_codes(code_v, base_v, idx_v, lanes):
    for j in range(code_v.shape[0] // lanes):
        c = code_v[pl.ds(j * lanes, lanes)]
        expert = lax.shift_right_logical(c, RANK_BITS)
        idx_v[pl.ds(j * lanes, lanes)] = plsc.load_gather(base_v, [expert]) + (c & (RANK_RADIX - 1))


def _two_slot_loop(n_chunks, start, finish):
    start(0, 0)

    @pl.loop(0, n_chunks, step=2)
    def _(c):
        start(c + 1, 1)
        finish(c, 0)

        @pl.when(c + 2 < n_chunks)
        def _():
            start(c + 2, 0)

        finish(c + 1, 1)


def _sc_dispatch(h2, code0, code1, base, n_rows):
    T, D = h2.shape
    info, mesh, params = _sc_workers()
    n_workers = info.num_cores * info.num_subcores
    w = SC_WINDOW_BYTES // (D * h2.dtype.itemsize)
    per_w = T // n_workers
    n_chunks = per_w // w
    assert per_w * n_workers == T and n_chunks * w == per_w and n_chunks % 2 == 0

    @functools.partial(
        pl.kernel, out_type=jax.ShapeDtypeStruct((n_rows, D), h2.dtype), mesh=mesh, compiler_params=params,
        scratch_types=[pltpu.VMEM((N_EXPERTS,), I32), pltpu.VMEM((w,), I32), pltpu.VMEM((w,), I32),
                       pltpu.VMEM((w,), I32), pltpu.VMEM((w, D), h2.dtype), pltpu.VMEM((w, D), h2.dtype),
                       pltpu.SemaphoreType.DMA, pltpu.SemaphoreType.DMA])
    def scatter(h_hbm, c0_hbm, c1_hbm, b_hbm, xs_hbm, base_v, code_v, i0_v, i1_v, rows0, rows1, sem0, sem1):
        wid = lax.axis_index("subcore") * info.num_cores + lax.axis_index("core")
        w0 = wid * per_w
        pltpu.sync_copy(b_hbm, base_v)
        rows = (rows0, rows1)
        sems = (sem0, sem1)

        def start(c, slot):
            pltpu.async_copy(h_hbm.at[pl.ds(w0 + c * w, w)], rows[slot], sems[slot])

        def finish(c, slot):
            pltpu.sync_copy(c0_hbm.at[pl.ds(w0 + c * w, w)], code_v)
            _rows_from_codes(code_v, base_v, i0_v, info.num_lanes)
            pltpu.sync_copy(c1_hbm.at[pl.ds(w0 + c * w, w)], code_v)
            _rows_from_codes(code_v, base_v, i1_v, info.num_lanes)
            pltpu.make_async_copy(h_hbm.at[pl.ds(w0 + c * w, w)], rows[slot], sems[slot]).wait()
            pltpu.sync_copy(rows[slot], xs_hbm.at[i0_v])
            pltpu.sync_copy(rows[slot], xs_hbm.at[i1_v])

        _two_slot_loop(n_chunks, start, finish)

    return scatter(h2, code0, code1, base)


def _sc_collect(ys, codes, base):
    n = codes.shape[0]
    D = ys.shape[1]
    info, mesh, params = _sc_workers()
    n_workers = info.num_cores * info.num_subcores
    w = SC_WINDOW_BYTES // (D * ys.dtype.itemsize)
    per_w = n // n_workers
    n_chunks = per_w // w
    assert per_w * n_workers == n and n_chunks * w == per_w and n_chunks % 2 == 0

    @functools.partial(
        pl.kernel, out_type=jax.ShapeDtypeStruct((n, D), ys.dtype), mesh=mesh, compiler_params=params,
        scratch_types=[pltpu.VMEM((N_EXPERTS,), I32), pltpu.VMEM((w,), I32), pltpu.VMEM((w,), I32),
                       pltpu.VMEM((w,), I32), pltpu.VMEM((w, D), ys.dtype), pltpu.VMEM((w, D), ys.dtype),
                       pltpu.SemaphoreType.DMA, pltpu.SemaphoreType.DMA])
    def gather(ys_hbm, c_hbm, b_hbm, yk_hbm, base_v, code_v, i0_v, i1_v, rows0, rows1, sem0, sem1):
        wid = lax.axis_index("subcore") * info.num_cores + lax.axis_index("core")
        w0 = wid * per_w
        pltpu.sync_copy(b_hbm, base_v)
        idx = (i0_v, i1_v)
        rows = (rows0, rows1)
        sems = (sem0, sem1)

        def start(c, slot):
            pltpu.sync_copy(c_hbm.at[pl.ds(w0 + c * w, w)], code_v)
            _rows_from_codes(code_v, base_v, idx[slot], info.num_lanes)
            pltpu.async_copy(ys_hbm.at[idx[slot]], rows[slot], sems[slot])

        def finish(c, slot):
            pltpu.make_async_copy(ys_hbm.at[idx[slot]], rows[slot], sems[slot]).wait()
            pltpu.sync_copy(rows[slot], yk_hbm.at[pl.ds(w0 + c * w, w)])

        _two_slot_loop(n_chunks, start, finish)

    return gather(ys, codes, base)


def _schedule_kernel(cnt_ref, te_ref, tb_ref, base_ref, nt_ref):
    tm = EXPERT_TM

    def expert(e, t0):
        n = (cnt_ref[e] + tm - 1) // tm
        base_ref[e] = t0 * tm

        def tile(t, c):
            te_ref[t] = e
            tb_ref[t] = t
            return c

        lax.fori_loop(t0, t0 + n, tile, 0)
        return t0 + n

    nt = lax.fori_loop(0, N_EXPERTS, expert, 0)
    nt_ref[0] = nt
    last = te_ref[nt - 1]

    def idle(t, c):
        te_ref[t] = last
        tb_ref[t] = nt - 1
        return c

    lax.fori_loop(nt, te_ref.shape[0], idle, 0)


def _schedule(counts, max_tiles):
    smem = pl.BlockSpec(memory_space=pltpu.SMEM)
    return pl.pallas_call(
        _schedule_kernel,
        out_shape=[jax.ShapeDtypeStruct((max_tiles,), I32),
                   jax.ShapeDtypeStruct((max_tiles,), I32),
                   jax.ShapeDtypeStruct((N_EXPERTS,), I32),
                   jax.ShapeDtypeStruct((1,), I32)],
        in_specs=[smem],
        out_specs=[smem, smem, smem, smem],
        name="schedule",
    )(counts)


def _expert_kernel(te_ref, tb_ref, nt_ref, xs_ref, wg_ref, wu_ref, wd_ref, ys_ref):
    @pl.when(pl.program_id(0) < nt_ref[0])
    def _():
        xb = _unpack_bf16_pairs(xs_ref[...])
        g = jnp.dot(xb, wg_ref[0], preferred_element_type=F32)
        u = jnp.dot(xb, wu_ref[0], preferred_element_type=F32)
        act = (g * _sigmoid(g)) * u
        ys_ref[...] = jnp.dot(act.astype(BF16), wd_ref[0], preferred_element_type=F32)


def _experts(tile_e, tile_b, n_tiles, xs, wg, wu, wd, max_tiles):
    P = xs.shape[0]
    D, de = wg.shape[1:]
    tm = EXPERT_TM
    wmap = lambda j, te, tb, nt: (te[j], 0, 0)
    rmap = lambda j, te, tb, nt: (tb[j], 0)
    return pl.pallas_call(
        _expert_kernel,
        out_shape=jax.ShapeDtypeStruct((P, D), F32),
        grid_spec=pltpu.PrefetchScalarGridSpec(
            num_scalar_prefetch=3,
            grid=(max_tiles,),
            in_specs=[pl.BlockSpec((tm, xs.shape[1]), rmap),
                      pl.BlockSpec((1, D, de), wmap),
                      pl.BlockSpec((1, D, de), wmap),
                      pl.BlockSpec((1, de, D), wmap)],
            out_specs=pl.BlockSpec((tm, D), rmap)),
        compiler_params=_params("arbitrary"),
        name="experts",
    )(tile_e, tile_b, n_tiles, xs, wg, wu, wd)


def _combine_kernel(x1_ref, rf_ref, mod_ref, gf_ref, y0_ref, y1_ref, out_ref, *, final_norm):
    gate2 = mod_ref[0, 5:6, :]
    w = rf_ref[...]
    moe = w[:, 0:1] * y0_ref[...] + w[:, 1:2] * y1_ref[...]
    x2 = x1_ref[...] + gate2 * moe
    if final_norm:
        ms = jnp.mean(x2 * x2, axis=-1, keepdims=True)
        x2 = x2 * lax.rsqrt(ms + RMS_EPS) * gf_ref[...]
    out_ref[...] = x2


def _combine(x1, rf, mod3, gf, yk, seq, final_norm):
    T, D = x1.shape
    tc = COMBINE_TM
    per_b = seq // tc
    n_blk = T // tc
    return pl.pallas_call(
        functools.partial(_combine_kernel, final_norm=final_norm),
        out_shape=jax.ShapeDtypeStruct((T, D), F32),
        grid=(n_blk,),
        in_specs=[pl.BlockSpec((tc, D), lambda i: (i, 0)),
                  pl.BlockSpec((tc, LANES), lambda i: (i, 0)),
                  pl.BlockSpec((1, 6, D), lambda i: (i // per_b, 0, 0)),
                  pl.BlockSpec((1, D), lambda i: (0, 0)),
                  pl.BlockSpec((tc, D), lambda i: (i, 0)),
                  pl.BlockSpec((tc, D), lambda i: (n_blk + i, 0))],
        out_specs=pl.BlockSpec((tc, D), lambda i: (i, 0)),
        compiler_params=_params("arbitrary"),
        name="combine",
    )(x1, rf, mod3, gf, yk, yk)


def _layer(x2, c, seq, w_ada, b_ada, g_norm1, w_in, b_if, conv_dw_w, conv_dw_b, conv_ln_g, conv_ln_b,
           w_conv_out, qk_conv_w, qk_conv_b, m_norm_g, w_m_out, w_out, g_norm2, w_rg, b_rg,
           w_re, b_re, w_e_gate, w_e_up, w_e_down):
    T, D = x2.shape
    B = T // seq
    dc = D // 2
    nif = 2 * M_HEADS

    mod3 = _ada(c, w_ada, b_ada).reshape(B, 6, D)

    if_lo = 6 * dc
    w_main = w_in[:, :if_lo].astype(BF16)
    w_gates = w_in[:, if_lo + nif:].astype(BF16)
    w_if = w_in[:, if_lo:if_lo + nif]
    w_if_pad = jnp.pad(w_if, ((0, 0), (0, LANES - nif))).astype(BF16)
    w_ift = w_if.T.astype(BF16)
    u, qk, v, o, sga, sgb, ifc, ifr = _inproj(x2, mod3, g_norm1.reshape(1, D), w_main, w_gates,
                                              w_if_pad, w_ift, seq)

    ya = _conv_branch(u.reshape(B, seq, dc), conv_dw_w, conv_dw_b.reshape(1, dc),
                      conv_ln_g.reshape(1, dc), conv_ln_b.reshape(1, dc), w_conv_out.astype(BF16))
    bifc = jnp.pad(b_if, (0, LANES - nif)).reshape(1, LANES)
    bifr = b_if.reshape(nif, 1)
    yb, (wg_b, wu_b, wd_b) = _mlstm_branch(
        qk.reshape(B, seq, 2 * dc), v.reshape(B, seq, dc), o.reshape(B, seq, dc),
        ifc.reshape(B, seq, LANES), ifr, qk_conv_w, qk_conv_b.reshape(1, 2 * dc), bifc, bifr,
        m_norm_g.reshape(1, dc), w_m_out.astype(BF16), riders=(w_e_gate, w_e_up, w_e_down))

    n_r = N_GROUPS + N_EXPERTS
    w_r = jnp.pad(jnp.concatenate([w_rg, w_re], axis=1), ((0, 0), (0, LANES - n_r))).astype(BF16)
    b_r = jnp.pad(jnp.concatenate([b_rg, b_re]), (0, LANES - n_r)).reshape(1, LANES)
    x1, h2, ri, rf, cnt = _merge(x2, ya.reshape(T, D), yb.reshape(T, D), sga, sgb, mod3,
                                 g_norm2.reshape(1, D), w_out.astype(BF16), w_r, b_r, seq)

    tm = EXPERT_TM
    max_tiles = (T * TOP_K) // tm + N_EXPERTS
    tile_e, tile_b, base, n_tiles = _schedule(cnt[0, :N_EXPERTS], max_tiles)

    code0 = ri[:, 0]
    code1 = ri[:, 1]
    xs = _sc_dispatch(h2, code0, code1, base, max_tiles * tm)
    ys = _experts(tile_e, tile_b, n_tiles, xs, wg_b, wu_b, wd_b, max_tiles)
    return x1, rf, mod3, ys, code0, code1, base


def kernel(x, c, w_ada, b_ada, g_norm1, w_in, b_if, conv_dw_w, conv_dw_b, conv_ln_g, conv_ln_b,
           w_conv_out, qk_conv_w, qk_conv_b, m_norm_g, w_m_out, w_out, g_norm2, w_rg, b_rg,
           w_re, b_re, w_e_gate, w_e_up, w_e_down, g_final):
    B, S, D = x.shape
    depth = w_ada.shape[0]
    x2 = x.reshape(B * S, D)
    for l in range(depth):
        x1, rf, mod3, ys, code0, code1, base = _layer(
            x2, c, S, w_ada[l], b_ada[l], g_norm1[l], w_in[l], b_if[l], conv_dw_w[l], conv_dw_b[l],
            conv_ln_g[l], conv_ln_b[l], w_conv_out[l], qk_conv_w[l], qk_conv_b[l], m_norm_g[l],
            w_m_out[l], w_out[l], g_norm2[l], w_rg[l], b_rg[l], w_re[l], b_re[l],
            w_e_gate[l], w_e_up[l], w_e_down[l])
        yk = _sc_collect(ys, jnp.concatenate([code0, code1]), base)
        x2 = _combine(x1, rf, mod3, g_final.reshape(1, D), yk, S, final_norm=l == depth - 1)
    return x2.reshape(B, S, D)
```

```python
import dataclasses
import functools

import jax
import jax.numpy as jnp
from jax import lax
from jax.experimental import pallas as pl
from jax.experimental.pallas import tpu as pltpu
from jax.experimental.pallas import tpu_sc as plsc

F32 = jnp.float32
BF16 = jnp.bfloat16
I32 = jnp.int32

M_HEADS = 4
CONV_WIDTH = 31
QK_CONV_WIDTH = 4
N_GROUPS = 4
E_PER_GROUP = 8
N_EXPERTS = N_GROUPS * E_PER_GROUP
TOP_K = 2
RMS_EPS = 1e-6
LN_EPS = 1e-5

LANES = 128
SUBLANES = 8
VMEM_LIMIT = 56 * 1024 * 1024

ADA_TN = 1024
INPROJ_TM = 256
CONV_TS = 512
CONV_HALO = 32
CONV_RC = 32
MLSTM_L = 128
MLSTM_SEQS = 4
MERGE_TM = 256
EXPERT_TM = 512
SC_WINDOW_BYTES = 128 * 1024
COMBINE_TM = 256
RANK_BITS = 16
RANK_RADIX = 1 << RANK_BITS
assert EXPERT_TM & (EXPERT_TM - 1) == 0


def _sigmoid(v):
    return 1.0 / (1.0 + jnp.exp(-v))


def _log_sigmoid(v):
    return -(jnp.maximum(-v, 0.0) + jnp.log1p(jnp.exp(-jnp.abs(v))))


def _pack_bf16_pairs(v):
    n = v.shape[1] // 2
    bits = lax.bitcast_convert_type(v.astype(BF16).astype(F32), jnp.uint32)
    word = bits[:, n:] | (bits[:, :n] >> 16)
    return lax.bitcast_convert_type(word, I32)


def _unpack_bf16_pairs(w):
    bits = lax.bitcast_convert_type(w, jnp.uint32)
    lo = lax.bitcast_convert_type(bits << 16, F32)
    hi = lax.bitcast_convert_type(bits & jnp.uint32(0xFFFF0000), F32)
    return jnp.concatenate([lo, hi], axis=1).astype(BF16)


def _params(*sem):
    return pltpu.CompilerParams(dimension_semantics=sem, vmem_limit_bytes=VMEM_LIMIT)


def _ada_kernel(c_ref, w_ref, b_ref, o_ref):
    c = c_ref[...]
    s = c * _sigmoid(c)
    o_ref[...] = jnp.dot(s, w_ref[...], preferred_element_type=F32,
                         precision=lax.Precision.HIGHEST) + b_ref[...]


def _ada(c, w_ada, b_ada):
    B, D = c.shape
    N = w_ada.shape[1]
    return pl.pallas_call(
        _ada_kernel,
        out_shape=jax.ShapeDtypeStruct((B, N), F32),
        grid=(N // ADA_TN,),
        in_specs=[pl.BlockSpec((B, D), lambda j: (0, 0)),
                  pl.BlockSpec((D, ADA_TN), lambda j: (0, j)),
                  pl.BlockSpec((1, ADA_TN), lambda j: (0, j))],
        out_specs=pl.BlockSpec((B, ADA_TN), lambda j: (0, j)),
        compiler_params=_params("arbitrary"),
        name="ada",
    )(c, w_ada, b_ada.reshape(1, N))


def _inproj_kernel(x_ref, mod_ref, g_ref, wm_ref, wgt_ref, wif_ref, wift_ref,
                   u_ref, qk_ref, v_ref, o_ref, sga_ref, sgb_ref, ifc_ref, ifr_ref):
    x = x_ref[...]
    shift = mod_ref[0, 0:1, :]
    scale = mod_ref[0, 1:2, :]
    ms = jnp.mean(x * x, axis=-1, keepdims=True)
    h = x * lax.rsqrt(ms + RMS_EPS) * g_ref[...]
    h = h * (1.0 + scale) + shift
    hb = h.astype(BF16)
    dc = u_ref.shape[1]
    d = sga_ref.shape[1]

    def seg(lo, hi):
        return jnp.dot(hb, wm_ref[:, lo:hi], preferred_element_type=F32)

    u_ref[...] = seg(0, dc) * _sigmoid(seg(dc, 2 * dc))
    qk_ref[...] = seg(2 * dc, 4 * dc)
    v_ref[...] = seg(4 * dc, 5 * dc).astype(BF16)
    o_ref[...] = seg(5 * dc, 6 * dc)
    sga_ref[...] = _sigmoid(jnp.dot(hb, wgt_ref[:, 0:d], preferred_element_type=F32)).astype(BF16)
    sgb_ref[...] = _sigmoid(jnp.dot(hb, wgt_ref[:, d:2 * d], preferred_element_type=F32)).astype(BF16)
    ifc_ref[...] = jnp.dot(hb, wif_ref[...], preferred_element_type=F32)
    ifr_ref[0] = lax.dot_general(wift_ref[...], hb, (((1,), (1,)), ((), ())),
                                 preferred_element_type=F32)


def _inproj(x2, mod3, g1, w_main, w_gates, w_if, w_ift, seq):
    T, D = x2.shape
    tm = INPROJ_TM
    dc = D // 2
    per_b = seq // tm
    row = lambda i: (i, 0)
    const = lambda i: (0, 0)
    return pl.pallas_call(
        _inproj_kernel,
        out_shape=[jax.ShapeDtypeStruct((T, dc), F32),
                   jax.ShapeDtypeStruct((T, 2 * dc), F32),
                   jax.ShapeDtypeStruct((T, dc), BF16),
                   jax.ShapeDtypeStruct((T, dc), F32),
                   jax.ShapeDtypeStruct((T, D), BF16),
                   jax.ShapeDtypeStruct((T, D), BF16),
                   jax.ShapeDtypeStruct((T, LANES), F32),
                   jax.ShapeDtypeStruct((T // seq, SUBLANES, seq), F32)],
        grid=(T // tm,),
        in_specs=[pl.BlockSpec((tm, D), row),
                  pl.BlockSpec((1, 6, D), lambda i: (i // per_b, 0, 0)),
                  pl.BlockSpec((1, D), const),
                  pl.BlockSpec(w_main.shape, const),
                  pl.BlockSpec(w_gates.shape, const),
                  pl.BlockSpec(w_if.shape, const),
                  pl.BlockSpec(w_ift.shape, const)],
        out_specs=[pl.BlockSpec((tm, dc), row),
                   pl.BlockSpec((tm, 2 * dc), row),
                   pl.BlockSpec((tm, dc), row),
                   pl.BlockSpec((tm, dc), row),
                   pl.BlockSpec((tm, D), row),
                   pl.BlockSpec((tm, D), row),
                   pl.BlockSpec((tm, LANES), row),
                   pl.BlockSpec((1, SUBLANES, tm), lambda i: (i // per_b, 0, i % per_b))],
        compiler_params=_params("arbitrary"),
        name="inproj",
    )(x2, mod3, g1, w_main, w_gates, w_if, w_ift)


def _conv_kernel(u_ref, w_ref, b_ref, lg_ref, lb_ref, wo_ref, y_ref, ubuf, sbuf, cbuf):
    ts = u_ref.shape[1]
    halo = CONV_HALO

    @pl.when(pl.program_id(1) == 0)
    def _():
        ubuf[0:halo, :] = jnp.zeros((halo, ubuf.shape[1]), F32)

    ubuf[halo:halo + ts, :] = u_ref[0]
    ns = sbuf.shape[1]
    for r in range(1, SUBLANES):
        sbuf[r - 1] = ubuf[r:r + ns, :]
    off = halo - (CONV_WIDTH - 1)
    for r0 in range(0, ts, CONV_RC):
        acc = jnp.broadcast_to(b_ref[...], (CONV_RC, ubuf.shape[1]))
        for k in range(CONV_WIDTH):
            r = (off + k) % SUBLANES
            lo = off + k - r + r0
            win = ubuf[lo:lo + CONV_RC, :] if r == 0 else sbuf[r - 1, lo:lo + CONV_RC, :]
            acc = acc + w_ref[k:k + 1, :] * win
        cbuf[r0:r0 + CONV_RC, :] = acc
    ubuf[0:halo, :] = ubuf[ts:ts + halo, :]

    a = cbuf[...]
    mu = jnp.mean(a, axis=-1, keepdims=True)
    ac = a - mu
    var = jnp.mean(ac * ac, axis=-1, keepdims=True)
    z = ac * lax.rsqrt(var + LN_EPS) * lg_ref[...] + lb_ref[...]
    z = z * _sigmoid(z)
    y_ref[0] = jnp.dot(z.astype(BF16), wo_ref[...], preferred_element_type=F32).astype(BF16)


def _conv_branch(u3, w, b, lg, lb, wo):
    B, S, C = u3.shape
    D = wo.shape[1]
    ts = CONV_TS
    const = lambda bi, si: (0, 0)
    return pl.pallas_call(
        _conv_kernel,
        out_shape=jax.ShapeDtypeStruct((B, S, D), BF16),
        grid=(B, S // ts),
        in_specs=[pl.BlockSpec((1, ts, C), lambda bi, si: (bi, si, 0)),
                  pl.BlockSpec(w.shape, const),
                  pl.BlockSpec((1, C), const),
                  pl.BlockSpec((1, C), const),
                  pl.BlockSpec((1, C), const),
                  pl.BlockSpec(wo.shape, const)],
        out_specs=pl.BlockSpec((1, ts, D), lambda bi, si: (bi, si, 0)),
        scratch_shapes=[pltpu.VMEM((ts + CONV_HALO, C), F32),
                        pltpu.VMEM((SUBLANES - 1, ts + CONV_HALO - SUBLANES, C), F32),
                        pltpu.VMEM((ts, C), F32)],
        compiler_params=_params("arbitrary", "arbitrary"),
        name="conv",
    )(u3, w, b, lg, lb, wo)


def _mlstm_kernel(*refs, n_riders):
    (qk_ref, v_ref, o_ref, ifc_ref, ifr_ref, cw_ref, cb_ref, bifc_ref, bifr_ref, ng_ref, wo_ref) = refs[:11]
    rider_in = refs[11:11 + n_riders]
    y_ref = refs[11 + n_riders]
    rider_out = refs[12 + n_riders:12 + 2 * n_riders]
    qkbuf, cn_ref, m_ref, hbuf = refs[12 + 2 * n_riders:]

    for src, dst in zip(rider_in, rider_out):
        dst[...] = src[...].astype(BF16)

    @pl.when(pl.program_id(1) == 0)
    def _():
        qkbuf[:, 0:SUBLANES, :] = jnp.zeros((qkbuf.shape[0], SUBLANES, qkbuf.shape[2]), F32)
        cn_ref[...] = jnp.zeros(cn_ref.shape, F32)
        m_ref[...] = jnp.zeros(m_ref.shape, F32)

    nb, L, mi = hbuf.shape
    dh = mi // M_HEADS
    halo = SUBLANES
    off = halo - (QK_CONV_WIDTH - 1)
    rows = lax.broadcasted_iota(I32, (L, L), 0)
    cols = lax.broadcasted_iota(I32, (L, L), 1)
    causal = cols <= rows
    lower = causal.astype(F32)
    upper = (rows <= cols).astype(F32)
    lane = lax.broadcasted_iota(I32, (L, dh), 1)
    ones_col = jnp.where(lane == 0, 1.0, 0.0).astype(BF16)
    scale = dh ** -0.5

    seqs = []
    for b in range(nb):
        qkbuf[b, halo:halo + L, :] = qk_ref[b]
        y = jnp.broadcast_to(cb_ref[...], (L, qkbuf.shape[2]))
        for k in range(QK_CONV_WIDTH):
            y = y + cw_ref[k:k + 1, :] * qkbuf[b, off + k:off + k + L, :]
        y = y * _sigmoid(y)
        qkbuf[b, 0:halo, :] = qkbuf[b, L:L + halo, :]
        ifr = ifr_ref[b] + bifr_ref[...]
        ifc = ifc_ref[b] + bifc_ref[...]
        bcum_c = jnp.dot(lower, _log_sigmoid(ifc), preferred_element_type=F32,
                         precision=lax.Precision.HIGHEST)
        bcum_r = jnp.dot(_log_sigmoid(ifr), upper, preferred_element_type=F32,
                         precision=lax.Precision.HIGHEST)
        seqs.append((y, ifr, bcum_c, bcum_r))

    probs = [(b, hd) for b in range(nb) for hd in range(M_HEADS)]
    st = {}
    for p in probs:
        b, hd = p
        y, ifr, bcum_c, bcum_r = seqs[b]
        c0 = hd * dh
        qb = (y[:, c0:c0 + dh] * scale).astype(BF16)
        kt = y[:, mi + c0:mi + c0 + dh].T
        v = v_ref[b, :, c0:c0 + dh]
        bc = bcum_c[:, M_HEADS + hd:M_HEADS + hd + 1]
        br = bcum_r[M_HEADS + hd:M_HEADS + hd + 1, :]
        li = ifr[hd:hd + 1, :]
        m_prev = m_ref[b, hd, 0:1, 0:1]
        dmat = jnp.where(causal, bc - br + li, -jnp.inf)
        st[p] = dict(qb=qb, kt=kt, v=v, bc=bc, br=br, li=li, m_prev=m_prev, dmat=dmat)
    for p in probs:
        s = st[p]
        s["inter"] = s["bc"] + s["m_prev"]
        s["m_t"] = jnp.maximum(jnp.max(s["dmat"], axis=-1, keepdims=True), s["inter"])
    for p in probs:
        s = st[p]
        s["qk"] = jnp.dot(s["qb"], s["kt"].astype(BF16), preferred_element_type=F32)
    for p in probs:
        b, hd = p
        s = st[p]
        s["cn"] = cn_ref[b, hd]
        s["qcn"] = jnp.dot(s["qb"], s["cn"].astype(BF16), preferred_element_type=F32)
    for p in probs:
        s = st[p]
        s["wts"] = jnp.exp(s["dmat"] - s["m_t"])
        s["s_inter"] = jnp.exp(s["inter"] - s["m_t"])
    for p in probs:
        s = st[p]
        s["s_mat"] = s["qk"] * s["wts"]
    for p in probs:
        s = st[p]
        s["sv"] = jnp.dot(s["s_mat"].astype(BF16), s["v"], preferred_element_type=F32)
    for p in probs:
        s = st[p]
        s["rowsum"] = jnp.sum(s["s_mat"], axis=-1, keepdims=True)
    for p in probs:
        s = st[p]
        s["num"] = s["sv"] + s["s_inter"] * s["qcn"][:, 0:dh]
        s["den"] = s["rowsum"] + s["s_inter"] * s["qcn"][:, dh:dh + 1]
    for p in probs:
        b, hd = p
        s = st[p]
        b_last = s["br"][:, L - 1:L]
        a = b_last - s["br"] + s["li"]
        m_new = jnp.maximum(b_last + s["m_prev"], jnp.max(a, axis=-1, keepdims=True))
        wk = jnp.exp(a - m_new)
        sc = jnp.exp(b_last + s["m_prev"] - m_new)
        v_ext = jnp.concatenate([s["v"], ones_col], axis=1)
        cn_ref[b, hd] = sc * s["cn"] + jnp.dot((s["kt"] * wk).astype(BF16), v_ext, preferred_element_type=F32)
        m_ref[b, hd] = jnp.broadcast_to(m_new, m_ref.shape[2:])
    for p in probs:
        s = st[p]
        s["hh"] = s["num"] / jnp.maximum(jnp.abs(s["den"]), jnp.exp(-s["m_t"]))
        s["mu"] = jnp.mean(s["hh"], axis=-1, keepdims=True)
    for p in probs:
        s = st[p]
        s["hc"] = s["hh"] - s["mu"]
        s["var"] = jnp.mean(s["hc"] * s["hc"], axis=-1, keepdims=True)
    for p in probs:
        b, hd = p
        s = st[p]
        c0 = hd * dh
        hn = s["hc"] * lax.rsqrt(s["var"] + LN_EPS) * ng_ref[:, c0:c0 + dh]
        hbuf[b, :, c0:c0 + dh] = hn * _sigmoid(o_ref[b, :, c0:c0 + dh])
    for b in range(nb):
        y = jnp.dot(hbuf[b].astype(BF16), wo_ref[...], preferred_element_type=F32)
        y_ref[b] = y.astype(BF16)


def _mlstm_branch(qk3, v3, o3, ifc3, ifr3, cw, cb, bifc, bifr, ng, wo, riders):
    B, S, C2 = qk3.shape
    mi = v3.shape[2]
    dh = mi // M_HEADS
    D = wo.shape[1]
    L = MLSTM_L
    nb = MLSTM_SEQS
    ns = S // L
    n_steps = (B // nb) * ns
    const = lambda bi, ci: (0, 0)
    tile = lambda bi, ci: (bi, ci, 0)
    slab = lambda bi, ci: (bi * ns + ci, 0, 0)
    slabs = [r.reshape(n_steps, -1, r.shape[-1]) for r in riders]
    outs = pl.pallas_call(
        functools.partial(_mlstm_kernel, n_riders=len(riders)),
        out_shape=[jax.ShapeDtypeStruct((B, S, D), BF16)]
        + [jax.ShapeDtypeStruct(s.shape, BF16) for s in slabs],
        grid=(B // nb, ns),
        in_specs=[pl.BlockSpec((nb, L, C2), tile),
                  pl.BlockSpec((nb, L, mi), tile),
                  pl.BlockSpec((nb, L, mi), tile),
                  pl.BlockSpec((nb, L, LANES), tile),
                  pl.BlockSpec((nb, SUBLANES, L), lambda bi, ci: (bi, 0, ci)),
                  pl.BlockSpec(cw.shape, const),
                  pl.BlockSpec((1, C2), const),
                  pl.BlockSpec((1, LANES), const),
                  pl.BlockSpec((SUBLANES, 1), const),
                  pl.BlockSpec((1, mi), const),
                  pl.BlockSpec(wo.shape, const)]
        + [pl.BlockSpec((1,) + s.shape[1:], slab) for s in slabs],
        out_specs=[pl.BlockSpec((nb, L, D), tile)]
        + [pl.BlockSpec((1,) + s.shape[1:], slab) for s in slabs],
        scratch_shapes=[pltpu.VMEM((nb, L + SUBLANES, C2), F32),
                        pltpu.VMEM((nb, M_HEADS, dh, 2 * dh), F32),
                        pltpu.VMEM((nb, M_HEADS, SUBLANES, LANES), F32),
                        pltpu.VMEM((nb, L, mi), F32)],
        compiler_params=_params("arbitrary", "arbitrary"),
        name="mlstm",
    )(qk3, v3, o3, ifc3, ifr3, cw, cb, bifc, bifr, ng, wo, *slabs)
    return outs[0], [o.reshape(r.shape) for o, r in zip(outs[1:], riders)]


def _merge_kernel(x_ref, ya_ref, yb_ref, sga_ref, sgb_ref, mod_ref, g2_ref, wo_ref, wr_ref, br_ref,
                  x1_ref, h2_ref, ri_ref, rf_ref, cnt_ref, run_ref):
    tm = x_ref.shape[0]

    @pl.when(pl.program_id(0) == 0)
    def _():
        run_ref[...] = jnp.zeros(run_ref.shape, F32)

    gate1 = mod_ref[0, 2:3, :]
    shift2 = mod_ref[0, 3:4, :]
    scale2 = mod_ref[0, 4:5, :]
    merged = (sga_ref[...].astype(F32) * ya_ref[...].astype(F32)
              + sgb_ref[...].astype(F32) * yb_ref[...].astype(F32))
    mix = jnp.dot(merged.astype(BF16), wo_ref[...], preferred_element_type=F32)
    x1 = x_ref[...] + gate1 * mix
    x1_ref[...] = x1
    ms = jnp.mean(x1 * x1, axis=-1, keepdims=True)
    h2 = x1 * lax.rsqrt(ms + RMS_EPS) * g2_ref[...]
    h2 = h2 * (1.0 + scale2) + shift2
    h2_ref[...] = _pack_bf16_pairs(h2)

    logits = jnp.dot(h2.astype(BF16), wr_ref[...], preferred_element_type=F32) + br_ref[...]
    lane = lax.broadcasted_iota(I32, (tm, LANES), 1).astype(F32)
    neg = -jnp.inf

    def first_argmax(vals):
        mx = jnp.max(vals, axis=-1, keepdims=True)
        idx = jnp.min(jnp.where(vals == mx, lane, float(LANES)), axis=-1, keepdims=True)
        return mx, idx

    lg = jnp.where(lane < N_GROUPS, logits, neg)
    gmax, gsel = first_argmax(lg)
    p_g = 1.0 / jnp.sum(jnp.exp(lg - gmax), axis=-1, keepdims=True)
    lo = N_GROUPS + gsel * E_PER_GROUP
    le = jnp.where((lane >= lo) & (lane < lo + E_PER_GROUP), logits, neg)
    l1, i1 = first_argmax(le)
    l2, i2 = first_argmax(jnp.where(lane == i1, neg, le))
    r = jnp.exp(l2 - l1)
    w1 = p_g / (1.0 + r)
    w2 = p_g * r / (1.0 + r)
    e1 = i1 - N_GROUPS
    e2 = i2 - N_GROUPS

    onehot = jnp.where((lane == e1) | (lane == e2), 1.0, 0.0)
    rows = lax.broadcasted_iota(I32, (tm, tm), 0)
    cols = lax.broadcasted_iota(I32, (tm, tm), 1)
    strict = jnp.where(cols < rows, 1.0, 0.0).astype(BF16)
    run = run_ref[0:1, :]
    before = jnp.dot(strict, onehot.astype(BF16), preferred_element_type=F32) + run
    rank1 = jnp.sum(jnp.where(lane == e1, before, 0.0), axis=-1, keepdims=True)
    rank2 = jnp.sum(jnp.where(lane == e2, before, 0.0), axis=-1, keepdims=True)
    run_new = run + jnp.sum(onehot, axis=0, keepdims=True)
    run_ref[...] = jnp.broadcast_to(run_new, run_ref.shape)
    cnt_ref[...] = jnp.broadcast_to(run_new, cnt_ref.shape).astype(I32)

    ri_ref[...] = jnp.where(lane == 0, e1 * float(RANK_RADIX) + rank1,
                            jnp.where(lane == 1, e2 * float(RANK_RADIX) + rank2, 0.0)).astype(I32)
    rf_ref[...] = jnp.where(lane == 0, w1, jnp.where(lane == 1, w2, 0.0))


def _merge(x2, ya, yb, sga, sgb, mod3, g2, wo, wr, br, seq):
    T, D = x2.shape
    tm = MERGE_TM
    per_b = seq // tm
    row = lambda i: (i, 0)
    const = lambda i: (0, 0)
    return pl.pallas_call(
        _merge_kernel,
        out_shape=[jax.ShapeDtypeStruct((T, D), F32),
                   jax.ShapeDtypeStruct((T, D // 2), I32),
                   jax.ShapeDtypeStruct((T, LANES), I32),
                   jax.ShapeDtypeStruct((T, LANES), F32),
                   jax.ShapeDtypeStruct((SUBLANES, LANES), I32)],
        grid=(T // tm,),
        in_specs=[pl.BlockSpec((tm, D), row),
                  pl.BlockSpec((tm, D), row),
                  pl.BlockSpec((tm, D), row),
                  pl.BlockSpec((tm, D), row),
                  pl.BlockSpec((tm, D), row),
                  pl.BlockSpec((1, 6, D), lambda i: (i // per_b, 0, 0)),
                  pl.BlockSpec((1, D), const),
                  pl.BlockSpec(wo.shape, const),
                  pl.BlockSpec(wr.shape, const),
                  pl.BlockSpec((1, LANES), const)],
        out_specs=[pl.BlockSpec((tm, D), row),
                   pl.BlockSpec((tm, D // 2), row),
                   pl.BlockSpec((tm, LANES), row),
                   pl.BlockSpec((tm, LANES), row),
                   pl.BlockSpec((SUBLANES, LANES), const)],
        scratch_shapes=[pltpu.VMEM((SUBLANES, LANES), F32)],
        compiler_params=_params("arbitrary"),
        name="merge",
    )(x2, ya, yb, sga, sgb, mod3, g2, wo, wr, br)


def _sc_workers():
    info = plsc.get_sparse_core_info()
    mesh = plsc.VectorSubcoreMesh(core_axis_name="core", subcore_axis_name="subcore")
    params = pltpu.CompilerParams()
    if "needs_layout_passes" in pltpu.CompilerParams.__dataclass_fields__:
        params = dataclasses.replace(params, needs_layout_passes=False)
    return info, mesh, params


def _rows_from_codes(code_v, base_v, idx_v, lanes):
    for j in range(code_v.shape[0] // lanes):
        c = code_v[pl.ds(j * lanes, lanes)]
        expert = lax.shift_right_logical(c, RANK_BITS)
        idx_v[pl.ds(j * lanes, lanes)] = plsc.load_gather(base_v, [expert]) + (c & (RANK_RADIX - 1))


def _two_slot_loop(n_chunks, start, finish):
    start(0, 0)

    @pl.loop(0, n_chunks, step=2)
    def _(c):
        start(c + 1, 1)
        finish(c, 0)

        @pl.when(c + 2 < n_chunks)
        def _():
            start(c + 2, 0)

        finish(c + 1, 1)


def _sc_dispatch(h2, code0, code1, base, n_rows):
    T, D = h2.shape
    info, mesh, params = _sc_workers()
    n_workers = info.num_cores * info.num_subcores
    w = SC_WINDOW_BYTES // (D * h2.dtype.itemsize)
    per_w = T // n_workers
    n_chunks = per_w // w
    assert per_w * n_workers == T and n_chunks * w == per_w and n_chunks % 2 == 0

    @functools.partial(
        pl.kernel, out_type=jax.ShapeDtypeStruct((n_rows, D), h2.dtype), mesh=mesh, compiler_params=params,
        scratch_types=[pltpu.VMEM((N_EXPERTS,), I32), pltpu.VMEM((w,), I32), pltpu.VMEM((w,), I32),
                       pltpu.VMEM((w,), I32), pltpu.VMEM((w, D), h2.dtype), pltpu.VMEM((w, D), h2.dtype),
                       pltpu.SemaphoreType.DMA, pltpu.SemaphoreType.DMA])
    def scatter(h_hbm, c0_hbm, c1_hbm, b_hbm, xs_hbm, base_v, code_v, i0_v, i1_v, rows0, rows1, sem0, sem1):
        wid = lax.axis_index("subcore") * info.num_cores + lax.axis_index("core")
        w0 = wid * per_w
        pltpu.sync_copy(b_hbm, base_v)
        rows = (rows0, rows1)
        sems = (sem0, sem1)

        def start(c, slot):
            pltpu.async_copy(h_hbm.at[pl.ds(w0 + c * w, w)], rows[slot], sems[slot])

        def finish(c, slot):
            pltpu.sync_copy(c0_hbm.at[pl.ds(w0 + c * w, w)], code_v)
            _rows_from_codes(code_v, base_v, i0_v, info.num_lanes)
            pltpu.sync_copy(c1_hbm.at[pl.ds(w0 + c * w, w)], code_v)
            _rows_from_codes(code_v, base_v, i1_v, info.num_lanes)
            pltpu.make_async_copy(h_hbm.at[pl.ds(w0 + c * w, w)], rows[slot], sems[slot]).wait()
            pltpu.sync_copy(rows[slot], xs_hbm.at[i0_v])
            pltpu.sync_copy(rows[slot], xs_hbm.at[i1_v])

        _two_slot_loop(n_chunks, start, finish)

    return scatter(h2, code0, code1, base)


def _sc_collect(ys, codes, base):
    n = codes.shape[0]
    D = ys.shape[1]
    info, mesh, params = _sc_workers()
    n_workers = info.num_cores * info.num_subcores
    w = SC_WINDOW_BYTES // (D * ys.dtype.itemsize)
    per_w = n // n_workers
    n_chunks = per_w // w
    assert per_w * n_workers == n and n_chunks * w == per_w and n_chunks % 2 == 0

    @functools.partial(
        pl.kernel, out_type=jax.ShapeDtypeStruct((n, D), ys.dtype), mesh=mesh, compiler_params=params,
        scratch_types=[pltpu.VMEM((N_EXPERTS,), I32), pltpu.VMEM((w,), I32), pltpu.VMEM((w,), I32),
                       pltpu.VMEM((w,), I32), pltpu.VMEM((w, D), ys.dtype), pltpu.VMEM((w, D), ys.dtype),
                       pltpu.SemaphoreType.DMA, pltpu.SemaphoreType.DMA])
    def gather(ys_hbm, c_hbm, b_hbm, yk_hbm, base_v, code_v, i0_v, i1_v, rows0, rows1, sem0, sem1):
        wid = lax.axis_index("subcore") * info.num_cores + lax.axis_index("core")
        w0 = wid * per_w
        pltpu.sync_copy(b_hbm, base_v)
        idx = (i0_v, i1_v)
        rows = (rows0, rows1)
        sems = (sem0, sem1)

        def start(c, slot):
            pltpu.sync_copy(c_hbm.at[pl.ds(w0 + c * w, w)], code_v)
            _rows_from_codes(code_v, base_v, idx[slot], info.num_lanes)
            pltpu.async_copy(ys_hbm.at[idx[slot]], rows[slot], sems[slot])

        def finish(c, slot):
            pltpu.make_async_copy(ys_hbm.at[idx[slot]], rows[slot], sems[slot]).wait()
            pltpu.sync_copy(rows[slot], yk_hbm.at[pl.ds(w0 + c * w, w)])

        _two_slot_loop(n_chunks, start, finish)

    return gather(ys, codes, base)


def _schedule_kernel(cnt_ref, te_ref, tb_ref, base_ref, nt_ref):
    tm = EXPERT_TM

    def expert(e, t0):
        n = (cnt_ref[e] + tm - 1) // tm
        base_ref[e] = t0 * tm

        def tile(t, c):
            te_ref[t] = e
            tb_ref[t] = t
            return c

        lax.fori_loop(t0, t0 + n, tile, 0)
        return t0 + n

    nt = lax.fori_loop(0, N_EXPERTS, expert, 0)
    nt_ref[0] = nt
    last = te_ref[nt - 1]

    def idle(t, c):
        te_ref[t] = last
        tb_ref[t] = nt - 1
        return c

    lax.fori_loop(nt, te_ref.shape[0], idle, 0)


def _schedule(counts, max_tiles):
    smem = pl.BlockSpec(memory_space=pltpu.SMEM)
    return pl.pallas_call(
        _schedule_kernel,
        out_shape=[jax.ShapeDtypeStruct((max_tiles,), I32),
                   jax.ShapeDtypeStruct((max_tiles,), I32),
                   jax.ShapeDtypeStruct((N_EXPERTS,), I32),
                   jax.ShapeDtypeStruct((1,), I32)],
        in_specs=[smem],
        out_specs=[smem, smem, smem, smem],
        name="schedule",
    )(counts)


def _expert_kernel(te_ref, tb_ref, nt_ref, xs_ref, wg_ref, wu_ref, wd_ref, ys_ref):
    @pl.when(pl.program_id(0) < nt_ref[0])
    def _():
        xb = _unpack_bf16_pairs(xs_ref[...])
        g = jnp.dot(xb, wg_ref[0], preferred_element_type=F32)
        u = jnp.dot(xb, wu_ref[0], preferred_element_type=F32)
        act = (g * _sigmoid(g)) * u
        ys_ref[...] = jnp.dot(act.astype(BF16), wd_ref[0], preferred_element_type=F32)


def _experts(tile_e, tile_b, n_tiles, xs, wg, wu, wd, max_tiles):
    P = xs.shape[0]
    D, de = wg.shape[1:]
    tm = EXPERT_TM
    wmap = lambda j, te, tb, nt: (te[j], 0, 0)
    rmap = lambda j, te, tb, nt: (tb[j], 0)
    return pl.pallas_call(
        _expert_kernel,
        out_shape=jax.ShapeDtypeStruct((P, D), F32),
        grid_spec=pltpu.PrefetchScalarGridSpec(
            num_scalar_prefetch=3,
            grid=(max_tiles,),
            in_specs=[pl.BlockSpec((tm, xs.shape[1]), rmap),
                      pl.BlockSpec((1, D, de), wmap),
                      pl.BlockSpec((1, D, de), wmap),
                      pl.BlockSpec((1, de, D), wmap)],
            out_specs=pl.BlockSpec((tm, D), rmap)),
        compiler_params=_params("arbitrary"),
        name="experts",
    )(tile_e, tile_b, n_tiles, xs, wg, wu, wd)


def _combine_kernel(x1_ref, rf_ref, mod_ref, gf_ref, y0_ref, y1_ref, out_ref, *, final_norm):
    gate2 = mod_ref[0, 5:6, :]
    w = rf_ref[...]
    moe = w[:, 0:1] * y0_ref[...] + w[:, 1:2] * y1_ref[...]
    x2 = x1_ref[...] + gate2 * moe
    if final_norm:
        ms = jnp.mean(x2 * x2, axis=-1, keepdims=True)
        x2 = x2 * lax.rsqrt(ms + RMS_EPS) * gf_ref[...]
    out_ref[...] = x2


def _combine(x1, rf, mod3, gf, yk, seq, final_norm):
    T, D = x1.shape
    tc = COMBINE_TM
    per_b = seq // tc
    n_blk = T // tc
    return pl.pallas_call(
        functools.partial(_combine_kernel, final_norm=final_norm),
        out_shape=jax.ShapeDtypeStruct((T, D), F32),
        grid=(n_blk,),
        in_specs=[pl.BlockSpec((tc, D), lambda i: (i, 0)),
                  pl.BlockSpec((tc, LANES), lambda i: (i, 0)),
                  pl.BlockSpec((1, 6, D), lambda i: (i // per_b, 0, 0)),
                  pl.BlockSpec((1, D), lambda i: (0, 0)),
                  pl.BlockSpec((tc, D), lambda i: (i, 0)),
                  pl.BlockSpec((tc, D), lambda i: (n_blk + i, 0))],
        out_specs=pl.BlockSpec((tc, D), lambda i: (i, 0)),
        compiler_params=_params("arbitrary"),
        name="combine",
    )(x1, rf, mod3, gf, yk, yk)


def _layer(x2, c, seq, w_ada, b_ada, g_norm1, w_in, b_if, conv_dw_w, conv_dw_b, conv_ln_g, conv_ln_b,
           w_conv_out, qk_conv_w, qk_conv_b, m_norm_g, w_m_out, w_out, g_norm2, w_rg, b_rg,
           w_re, b_re, w_e_gate, w_e_up, w_e_down):
    T, D = x2.shape
    B = T // seq
    dc = D // 2
    nif = 2 * M_HEADS

    mod3 = _ada(c, w_ada, b_ada).reshape(B, 6, D)

    if_lo = 6 * dc
    w_main = w_in[:, :if_lo].astype(BF16)
    w_gates = w_in[:, if_lo + nif:].astype(BF16)
    w_if = w_in[:, if_lo:if_lo + nif]
    w_if_pad = jnp.pad(w_if, ((0, 0), (0, LANES - nif))).astype(BF16)
    w_ift = w_if.T.astype(BF16)
    u, qk, v, o, sga, sgb, ifc, ifr = _inproj(x2, mod3, g_norm1.reshape(1, D), w_main, w_gates,
                                              w_if_pad, w_ift, seq)

    ya = _conv_branch(u.reshape(B, seq, dc), conv_dw_w, conv_dw_b.reshape(1, dc),
                      conv_ln_g.reshape(1, dc), conv_ln_b.reshape(1, dc), w_conv_out.astype(BF16))
    bifc = jnp.pad(b_if, (0, LANES - nif)).reshape(1, LANES)
    bifr = b_if.reshape(nif, 1)
    yb, (wg_b, wu_b, wd_b) = _mlstm_branch(
        qk.reshape(B, seq, 2 * dc), v.reshape(B, seq, dc), o.reshape(B, seq, dc),
        ifc.reshape(B, seq, LANES), ifr, qk_conv_w, qk_conv_b.reshape(1, 2 * dc), bifc, bifr,
        m_norm_g.reshape(1, dc), w_m_out.astype(BF16), riders=(w_e_gate, w_e_up, w_e_down))

    n_r = N_GROUPS + N_EXPERTS
    w_r = jnp.pad(jnp.concatenate([w_rg, w_re], axis=1), ((0, 0), (0, LANES - n_r))).astype(BF16)
    b_r = jnp.pad(jnp.concatenate([b_rg, b_re]), (0, LANES - n_r)).reshape(1, LANES)
    x1, h2, ri, rf, cnt = _merge(x2, ya.reshape(T, D), yb.reshape(T, D), sga, sgb, mod3,
                                 g_norm2.reshape(1, D), w_out.astype(BF16), w_r, b_r, seq)

    tm = EXPERT_TM
    max_tiles = (T * TOP_K) // tm + N_EXPERTS
    tile_e, tile_b, base, n_tiles = _schedule(cnt[0, :N_EXPERTS], max_tiles)

    code0 = ri[:, 0]
    code1 = ri[:, 1]
    xs = _sc_dispatch(h2, code0, code1, base, max_tiles * tm)
    ys = _experts(tile_e, tile_b, n_tiles, xs, wg_b, wu_b, wd_b, max_tiles)
    return x1, rf, mod3, ys, code0, code1, base


def kernel(x, c, w_ada, b_ada, g_norm1, w_in, b_if, conv_dw_w, conv_dw_b, conv_ln_g, conv_ln_b,
           w_conv_out, qk_conv_w, qk_conv_b, m_norm_g, w_m_out, w_out, g_norm2, w_rg, b_rg,
           w_re, b_re, w_e_gate, w_e_up, w_e_down, g_final):
    B, S, D = x.shape
    depth = w_ada.shape[0]
    x2 = x.reshape(B * S, D)
    for l in range(depth):
        x1, rf, mod3, ys, code0, code1, base = _layer(
            x2, c, S, w_ada[l], b_ada[l], g_norm1[l], w_in[l], b_if[l], conv_dw_w[l], conv_dw_b[l],
            conv_ln_g[l], conv_ln_b[l], w_conv_out[l], qk_conv_w[l], qk_conv_b[l], m_norm_g[l],
            w_m_out[l], w_out[l], g_norm2[l], w_rg[l], b_rg[l], w_re[l], b_re[l],
            w_e_gate[l], w_e_up[l], w_e_down[l])
        yk = _sc_collect(ys, jnp.concatenate([code0, code1]), base)
        x2 = _combine(x1, rf, mod3, g_final.reshape(1, D), yk, S, final_norm=l == depth - 1)
    return x2.reshape(B, S, D)
```

```python
import dataclasses
import functools

import jax
import jax.numpy as jnp
from jax import lax
from jax.experimental import pallas as pl
from jax.experimental.pallas import tpu as pltpu
from jax.experimental.pallas import tpu_sc as plsc

F32 = jnp.float32
BF16 = jnp.bfloat16
I32 = jnp.int32

M_HEADS = 4
CONV_WIDTH = 31
QK_CONV_WIDTH = 4
N_GROUPS = 4
E_PER_GROUP = 8
N_EXPERTS = N_GROUPS * E_PER_GROUP
TOP_K = 2
RMS_EPS = 1e-6
LN_EPS = 1e-5

LANES = 128
SUBLANES = 8
VMEM_LIMIT = 56 * 1024 * 1024

ADA_TN = 1024
INPROJ_TM = 512
INPROJ_SUB = 256
CONV_TS = 512
CONV_HALO = 32
CONV_RC = 32
MLSTM_L = 128
MLSTM_SEQS = 4
MERGE_TM = 512
MERGE_SUB = 256
EXPERT_TM = 512
SC_WINDOW_BYTES = 128 * 1024
COMBINE_TM = 256
RANK_BITS = 16
RANK_RADIX = 1 << RANK_BITS
assert EXPERT_TM & (EXPERT_TM - 1) == 0


def _sigmoid(v):
    return 1.0 / (1.0 + jnp.exp(-v))


def _log_sigmoid(v):
    return -(jnp.maximum(-v, 0.0) + jnp.log1p(jnp.exp(-jnp.abs(v))))


def _pack_bf16_pairs(v):
    n = v.shape[1] // 2
    bits = lax.bitcast_convert_type(v.astype(BF16).astype(F32), jnp.uint32)
    word = bits[:, n:] | (bits[:, :n] >> 16)
    return lax.bitcast_convert_type(word, I32)


def _unpack_bf16_pairs(w):
    bits = lax.bitcast_convert_type(w, jnp.uint32)
    lo = lax.bitcast_convert_type(bits << 16, F32)
    hi = lax.bitcast_convert_type(bits & jnp.uint32(0xFFFF0000), F32)
    return jnp.concatenate([lo, hi], axis=1).astype(BF16)


def _params(*sem):
    return pltpu.CompilerParams(dimension_semantics=sem, vmem_limit_bytes=VMEM_LIMIT)


def _ada_kernel(c_ref, w_ref, b_ref, o_ref):
    c = c_ref[...]
    s = c * _sigmoid(c)
    o_ref[...] = jnp.dot(s, w_ref[...], preferred_element_type=F32,
                         precision=lax.Precision.HIGHEST) + b_ref[...]


def _ada(c, w_ada, b_ada):
    B, D = c.shape
    N = w_ada.shape[1]
    return pl.pallas_call(
        _ada_kernel,
        out_shape=jax.ShapeDtypeStruct((B, N), F32),
        grid=(N // ADA_TN,),
        in_specs=[pl.BlockSpec((B, D), lambda j: (0, 0)),
                  pl.BlockSpec((D, ADA_TN), lambda j: (0, j)),
                  pl.BlockSpec((1, ADA_TN), lambda j: (0, j))],
        out_specs=pl.BlockSpec((B, ADA_TN), lambda j: (0, j)),
        compiler_params=_params("arbitrary"),
        name="ada",
    )(c, w_ada, b_ada.reshape(1, N))


def _inproj_kernel(x_ref, mod_ref, g_ref, wm_ref, wgt_ref, wif_ref, wift_ref,
                   u_ref, qk_ref, v_ref, o_ref, sga_ref, sgb_ref, ifc_ref, ifr_ref):
    shift = mod_ref[0, 0:1, :]
    scale = mod_ref[0, 1:2, :]
    dc = u_ref.shape[1]
    d = sga_ref.shape[1]
    ts = INPROJ_SUB
    subs = [pl.ds(r0, ts) for r0 in range(0, x_ref.shape[0], ts)]

    hbs = []
    for sl in subs:
        x = x_ref[sl, :]
        ms = jnp.mean(x * x, axis=-1, keepdims=True)
        h = x * lax.rsqrt(ms + RMS_EPS) * g_ref[...]
        h = h * (1.0 + scale) + shift
        hbs.append(h.astype(BF16))

    for sl, hb in zip(subs, hbs):
        def seg(lo, hi):
            return jnp.dot(hb, wm_ref[:, lo:hi], preferred_element_type=F32)

        u_ref[sl, :] = seg(0, dc) * _sigmoid(seg(dc, 2 * dc))
        qk_ref[sl, :] = seg(2 * dc, 4 * dc)
        v_ref[sl, :] = seg(4 * dc, 5 * dc).astype(BF16)
        o_ref[sl, :] = seg(5 * dc, 6 * dc)
        sga_ref[sl, :] = _sigmoid(jnp.dot(hb, wgt_ref[:, 0:d], preferred_element_type=F32)).astype(BF16)
        sgb_ref[sl, :] = _sigmoid(jnp.dot(hb, wgt_ref[:, d:2 * d], preferred_element_type=F32)).astype(BF16)
        ifc_ref[sl, :] = jnp.dot(hb, wif_ref[...], preferred_element_type=F32)
        ifr_ref[0, :, sl] = lax.dot_general(wift_ref[...], hb, (((1,), (1,)), ((), ())),
                                            preferred_element_type=F32)


def _inproj(x2, mod3, g1, w_main, w_gates, w_if, w_ift, seq):
    T, D = x2.shape
    tm = INPROJ_TM
    dc = D // 2
    per_b = seq // tm
    row = lambda i: (i, 0)
    const = lambda i: (0, 0)
    return pl.pallas_call(
        _inproj_kernel,
        out_shape=[jax.ShapeDtypeStruct((T, dc), F32),
                   jax.ShapeDtypeStruct((T, 2 * dc), F32),
                   jax.ShapeDtypeStruct((T, dc), BF16),
                   jax.ShapeDtypeStruct((T, dc), F32),
                   jax.ShapeDtypeStruct((T, D), BF16),
                   jax.ShapeDtypeStruct((T, D), BF16),
                   jax.ShapeDtypeStruct((T, LANES), F32),
                   jax.ShapeDtypeStruct((T // seq, SUBLANES, seq), F32)],
        grid=(T // tm,),
        in_specs=[pl.BlockSpec((tm, D), row),
                  pl.BlockSpec((1, 6, D), lambda i: (i // per_b, 0, 0)),
                  pl.BlockSpec((1, D), const),
                  pl.BlockSpec(w_main.shape, const),
                  pl.BlockSpec(w_gates.shape, const),
                  pl.BlockSpec(w_if.shape, const),
                  pl.BlockSpec(w_ift.shape, const)],
        out_specs=[pl.BlockSpec((tm, dc), row),
                   pl.BlockSpec((tm, 2 * dc), row),
                   pl.BlockSpec((tm, dc), row),
                   pl.BlockSpec((tm, dc), row),
                   pl.BlockSpec((tm, D), row),
                   pl.BlockSpec((tm, D), row),
                   pl.BlockSpec((tm, LANES), row),
                   pl.BlockSpec((1, SUBLANES, tm), lambda i: (i // per_b, 0, i % per_b))],
        compiler_params=_params("arbitrary"),
        name="inproj",
    )(x2, mod3, g1, w_main, w_gates, w_if, w_ift)


def _conv_kernel(u_ref, w_ref, b_ref, lg_ref, lb_ref, wo_ref, y_ref, ubuf, sbuf, cbuf):
    ts = u_ref.shape[1]
    halo = CONV_HALO

    @pl.when(pl.program_id(1) == 0)
    def _():
        ubuf[0:halo, :] = jnp.zeros((halo, ubuf.shape[1]), F32)

    ubuf[halo:halo + ts, :] = u_ref[0]
    ns = sbuf.shape[1]
    for r in range(1, SUBLANES):
        sbuf[r - 1] = ubuf[r:r + ns, :]
    off = halo - (CONV_WIDTH - 1)
    for r0 in range(0, ts, CONV_RC):
        acc = jnp.broadcast_to(b_ref[...], (CONV_RC, ubuf.shape[1]))
        for k in range(CONV_WIDTH):
            r = (off + k) % SUBLANES
            lo = off + k - r + r0
            win = ubuf[lo:lo + CONV_RC, :] if r == 0 else sbuf[r - 1, lo:lo + CONV_RC, :]
            acc = acc + w_ref[k:k + 1, :] * win
        cbuf[r0:r0 + CONV_RC, :] = acc
    ubuf[0:halo, :] = ubuf[ts:ts + halo, :]

    a = cbuf[...]
    mu = jnp.mean(a, axis=-1, keepdims=True)
    ac = a - mu
    var = jnp.mean(ac * ac, axis=-1, keepdims=True)
    z = ac * lax.rsqrt(var + LN_EPS) * lg_ref[...] + lb_ref[...]
    z = z * _sigmoid(z)
    y_ref[0] = jnp.dot(z.astype(BF16), wo_ref[...], preferred_element_type=F32).astype(BF16)


def _conv_branch(u3, w, b, lg, lb, wo):
    B, S, C = u3.shape
    D = wo.shape[1]
    ts = CONV_TS
    const = lambda bi, si: (0, 0)
    return pl.pallas_call(
        _conv_kernel,
        out_shape=jax.ShapeDtypeStruct((B, S, D), BF16),
        grid=(B, S // ts),
        in_specs=[pl.BlockSpec((1, ts, C), lambda bi, si: (bi, si, 0)),
                  pl.BlockSpec(w.shape, const),
                  pl.BlockSpec((1, C), const),
                  pl.BlockSpec((1, C), const),
                  pl.BlockSpec((1, C), const),
                  pl.BlockSpec(wo.shape, const)],
        out_specs=pl.BlockSpec((1, ts, D), lambda bi, si: (bi, si, 0)),
        scratch_shapes=[pltpu.VMEM((ts + CONV_HALO, C), F32),
                        pltpu.VMEM((SUBLANES - 1, ts + CONV_HALO - SUBLANES, C), F32),
                        pltpu.VMEM((ts, C), F32)],
        compiler_params=_params("arbitrary", "arbitrary"),
        name="conv",
    )(u3, w, b, lg, lb, wo)


def _mlstm_kernel(*refs, n_riders):
    (qk_ref, v_ref, o_ref, ifc_ref, ifr_ref, cw_ref, cb_ref, bifc_ref, bifr_ref, ng_ref, wo_ref) = refs[:11]
    rider_in = refs[11:11 + n_riders]
    y_ref = refs[11 + n_riders]
    rider_out = refs[12 + n_riders:12 + 2 * n_riders]
    qkbuf, cn_ref, m_ref, hbuf = refs[12 + 2 * n_riders:]

    for src, dst in zip(rider_in, rider_out):
        dst[...] = src[...].astype(BF16)

    @pl.when(pl.program_id(1) == 0)
    def _():
        qkbuf[:, 0:SUBLANES, :] = jnp.zeros((qkbuf.shape[0], SUBLANES, qkbuf.shape[2]), F32)
        cn_ref[...] = jnp.zeros(cn_ref.shape, F32)
        m_ref[...] = jnp.zeros(m_ref.shape, F32)

    nb, L, mi = hbuf.shape
    dh = mi // M_HEADS
    halo = SUBLANES
    off = halo - (QK_CONV_WIDTH - 1)
    rows = lax.broadcasted_iota(I32, (L, L), 0)
    cols = lax.broadcasted_iota(I32, (L, L), 1)
    causal = cols <= rows
    lower = causal.astype(F32)
    upper = (rows <= cols).astype(F32)
    lane = lax.broadcasted_iota(I32, (L, dh), 1)
    ones_col = jnp.where(lane == 0, 1.0, 0.0).astype(BF16)
    scale = dh ** -0.5

    seqs = []
    for b in range(nb):
        qkbuf[b, halo:halo + L, :] = qk_ref[b]
        y = jnp.broadcast_to(cb_ref[...], (L, qkbuf.shape[2]))
        for k in range(QK_CONV_WIDTH):
            y = y + cw_ref[k:k + 1, :] * qkbuf[b, off + k:off + k + L, :]
        y = y * _sigmoid(y)
        qkbuf[b, 0:halo, :] = qkbuf[b, L:L + halo, :]
        ifr = ifr_ref[b] + bifr_ref[...]
        ifc = ifc_ref[b] + bifc_ref[...]
        bcum_c = jnp.dot(lower, _log_sigmoid(ifc), preferred_element_type=F32,
                         precision=lax.Precision.HIGHEST)
        bcum_r = jnp.dot(_log_sigmoid(ifr), upper, preferred_element_type=F32,
                         precision=lax.Precision.HIGHEST)
        seqs.append((y, ifr, bcum_c, bcum_r))

    probs = [(b, hd) for b in range(nb) for hd in range(M_HEADS)]
    st = {}
    for p in probs:
        b, hd = p
        y, ifr, bcum_c, bcum_r = seqs[b]
        c0 = hd * dh
        qb = (y[:, c0:c0 + dh] * scale).astype(BF16)
        kt = y[:, mi + c0:mi + c0 + dh].T
        v = v_ref[b, :, c0:c0 + dh]
        bc = bcum_c[:, M_HEADS + hd:M_HEADS + hd + 1]
        br = bcum_r[M_HEADS + hd:M_HEADS + hd + 1, :]
        li = ifr[hd:hd + 1, :]
        m_prev = m_ref[b, hd, 0:1, 0:1]
        dmat = jnp.where(causal, bc - br + li, -jnp.inf)
        st[p] = dict(qb=qb, kt=kt, v=v, bc=bc, br=br, li=li, m_prev=m_prev, dmat=dmat)
    for p in probs:
        s = st[p]
        s["inter"] = s["bc"] + s["m_prev"]
        s["m_t"] = jnp.maximum(jnp.max(s["dmat"], axis=-1, keepdims=True), s["inter"])
    for p in probs:
        s = st[p]
        s["qk"] = jnp.dot(s["qb"], s["kt"].astype(BF16), preferred_element_type=F32)
    for p in probs:
        b, hd = p
        s = st[p]
        s["cn"] = cn_ref[b, hd]
        s["qcn"] = jnp.dot(s["qb"], s["cn"].astype(BF16), preferred_element_type=F32)
    for p in probs:
        s = st[p]
        s["wts"] = jnp.exp(s["dmat"] - s["m_t"])
        s["s_inter"] = jnp.exp(s["inter"] - s["m_t"])
    for p in probs:
        s = st[p]
        s["s_mat"] = s["qk"] * s["wts"]
    for p in probs:
        s = st[p]
        s["sv"] = jnp.dot(s["s_mat"].astype(BF16), s["v"], preferred_element_type=F32)
    for p in probs:
        s = st[p]
        s["rowsum"] = jnp.sum(s["s_mat"], axis=-1, keepdims=True)
    for p in probs:
        s = st[p]
        s["num"] = s["sv"] + s["s_inter"] * s["qcn"][:, 0:dh]
        s["den"] = s["rowsum"] + s["s_inter"] * s["qcn"][:, dh:dh + 1]
    for p in probs:
        b, hd = p
        s = st[p]
        b_last = s["br"][:, L - 1:L]
        a = b_last - s["br"] + s["li"]
        m_new = jnp.maximum(b_last + s["m_prev"], jnp.max(a, axis=-1, keepdims=True))
        wk = jnp.exp(a - m_new)
        sc = jnp.exp(b_last + s["m_prev"] - m_new)
        v_ext = jnp.concatenate([s["v"], ones_col], axis=1)
        cn_ref[b, hd] = sc * s["cn"] + jnp.dot((s["kt"] * wk).astype(BF16), v_ext, preferred_element_type=F32)
        m_ref[b, hd] = jnp.broadcast_to(m_new, m_ref.shape[2:])
    for p in probs:
        s = st[p]
        s["hh"] = s["num"] / jnp.maximum(jnp.abs(s["den"]), jnp.exp(-s["m_t"]))
        s["mu"] = jnp.mean(s["hh"], axis=-1, keepdims=True)
    for p in probs:
        s = st[p]
        s["hc"] = s["hh"] - s["mu"]
        s["var"] = jnp.mean(s["hc"] * s["hc"], axis=-1, keepdims=True)
    for p in probs:
        b, hd = p
        s = st[p]
        c0 = hd * dh
        hn = s["hc"] * lax.rsqrt(s["var"] + LN_EPS) * ng_ref[:, c0:c0 + dh]
        hbuf[b, :, c0:c0 + dh] = hn * _sigmoid(o_ref[b, :, c0:c0 + dh])
    for b in range(nb):
        y = jnp.dot(hbuf[b].astype(BF16), wo_ref[...], preferred_element_type=F32)
        y_ref[b] = y.astype(BF16)


def _mlstm_branch(qk3, v3, o3, ifc3, ifr3, cw, cb, bifc, bifr, ng, wo, riders):
    B, S, C2 = qk3.shape
    mi = v3.shape[2]
    dh = mi // M_HEADS
    D = wo.shape[1]
    L = MLSTM_L
    nb = MLSTM_SEQS
    ns = S // L
    n_steps = (B // nb) * ns
    const = lambda bi, ci: (0, 0)
    tile = lambda bi, ci: (bi, ci, 0)
    slab = lambda bi, ci: (bi * ns + ci, 0, 0)
    slabs = [r.reshape(n_steps, -1, r.shape[-1]) for r in riders]
    outs = pl.pallas_call(
        functools.partial(_mlstm_kernel, n_riders=len(riders)),
        out_shape=[jax.ShapeDtypeStruct((B, S, D), BF16)]
        + [jax.ShapeDtypeStruct(s.shape, BF16) for s in slabs],
        grid=(B // nb, ns),
        in_specs=[pl.BlockSpec((nb, L, C2), tile),
                  pl.BlockSpec((nb, L, mi), tile),
                  pl.BlockSpec((nb, L, mi), tile),
                  pl.BlockSpec((nb, L, LANES), tile),
                  pl.BlockSpec((nb, SUBLANES, L), lambda bi, ci: (bi, 0, ci)),
                  pl.BlockSpec(cw.shape, const),
                  pl.BlockSpec((1, C2), const),
                  pl.BlockSpec((1, LANES), const),
                  pl.BlockSpec((SUBLANES, 1), const),
                  pl.BlockSpec((1, mi), const),
                  pl.BlockSpec(wo.shape, const)]
        + [pl.BlockSpec((1,) + s.shape[1:], slab) for s in slabs],
        out_specs=[pl.BlockSpec((nb, L, D), tile)]
        + [pl.BlockSpec((1,) + s.shape[1:], slab) for s in slabs],
        scratch_shapes=[pltpu.VMEM((nb, L + SUBLANES, C2), F32),
                        pltpu.VMEM((nb, M_HEADS, dh, 2 * dh), F32),
                        pltpu.VMEM((nb, M_HEADS, SUBLANES, LANES), F32),
                        pltpu.VMEM((nb, L, mi), F32)],
        compiler_params=_params("arbitrary", "arbitrary"),
        name="mlstm",
    )(qk3, v3, o3, ifc3, ifr3, cw, cb, bifc, bifr, ng, wo, *slabs)
    return outs[0], [o.reshape(r.shape) for o, r in zip(outs[1:], riders)]


def _merge_kernel(x_ref, ya_ref, yb_ref, sga_ref, sgb_ref, mod_ref, g2_ref, wo_ref, wr_ref, br_ref,
                  x1_ref, h2_ref, ri_ref, rf_ref, cnt_ref, run_ref):
    ts = MERGE_SUB
    subs = [pl.ds(r0, ts) for r0 in range(0, x_ref.shape[0], ts)]

    @pl.when(pl.program_id(0) == 0)
    def _():
        run_ref[...] = jnp.zeros(run_ref.shape, F32)

    gate1 = mod_ref[0, 2:3, :]
    shift2 = mod_ref[0, 3:4, :]
    scale2 = mod_ref[0, 4:5, :]
    lane = lax.broadcasted_iota(I32, (ts, LANES), 1).astype(F32)
    neg = -jnp.inf
    rows = lax.broadcasted_iota(I32, (ts, ts), 0)
    cols = lax.broadcasted_iota(I32, (ts, ts), 1)
    strict = jnp.where(cols < rows, 1.0, 0.0).astype(BF16)

    def first_argmax(vals):
        mx = jnp.max(vals, axis=-1, keepdims=True)
        idx = jnp.min(jnp.where(vals == mx, lane, float(LANES)), axis=-1, keepdims=True)
        return mx, idx

    h2s = []
    for sl in subs:
        merged = (sga_ref[sl, :].astype(F32) * ya_ref[sl, :].astype(F32)
                  + sgb_ref[sl, :].astype(F32) * yb_ref[sl, :].astype(F32))
        mix = jnp.dot(merged.astype(BF16), wo_ref[...], preferred_element_type=F32)
        x1 = x_ref[sl, :] + gate1 * mix
        x1_ref[sl, :] = x1
        ms = jnp.mean(x1 * x1, axis=-1, keepdims=True)
        h2 = x1 * lax.rsqrt(ms + RMS_EPS) * g2_ref[...]
        h2 = h2 * (1.0 + scale2) + shift2
        h2_ref[sl, :] = _pack_bf16_pairs(h2)
        h2s.append(h2.astype(BF16))

    run = run_ref[0:1, :]
    for sl, h2b in zip(subs, h2s):
        logits = jnp.dot(h2b, wr_ref[...], preferred_element_type=F32) + br_ref[...]
        lg = jnp.where(lane < N_GROUPS, logits, neg)
        gmax, gsel = first_argmax(lg)
        p_g = 1.0 / jnp.sum(jnp.exp(lg - gmax), axis=-1, keepdims=True)
        lo = N_GROUPS + gsel * E_PER_GROUP
        le = jnp.where((lane >= lo) & (lane < lo + E_PER_GROUP), logits, neg)
        l1, i1 = first_argmax(le)
        l2, i2 = first_argmax(jnp.where(lane == i1, neg, le))
        r = jnp.exp(l2 - l1)
        w1 = p_g / (1.0 + r)
        w2 = p_g * r / (1.0 + r)
        e1 = i1 - N_GROUPS
        e2 = i2 - N_GROUPS

        onehot = jnp.where((lane == e1) | (lane == e2), 1.0, 0.0)
        before = jnp.dot(strict, onehot.astype(BF16), preferred_element_type=F32) + run
        rank1 = jnp.sum(jnp.where(lane == e1, before, 0.0), axis=-1, keepdims=True)
        rank2 = jnp.sum(jnp.where(lane == e2, before, 0.0), axis=-1, keepdims=True)
        run = run + jnp.sum(onehot, axis=0, keepdims=True)

        ri_ref[sl, :] = jnp.where(lane == 0, e1 * float(RANK_RADIX) + rank1,
                                  jnp.where(lane == 1, e2 * float(RANK_RADIX) + rank2, 0.0)).astype(I32)
        rf_ref[sl, :] = jnp.where(lane == 0, w1, jnp.where(lane == 1, w2, 0.0))
    run_ref[...] = jnp.broadcast_to(run, run_ref.shape)
    cnt_ref[...] = jnp.broadcast_to(run, cnt_ref.shape).astype(I32)


def _merge(x2, ya, yb, sga, sgb, mod3, g2, wo, wr, br, seq):
    T, D = x2.shape
    tm = MERGE_TM
    per_b = seq // tm
    row = lambda i: (i, 0)
    const = lambda i: (0, 0)
    return pl.pallas_call(
        _merge_kernel,
        out_shape=[jax.ShapeDtypeStruct((T, D), F32),
                   jax.ShapeDtypeStruct((T, D // 2), I32),
                   jax.ShapeDtypeStruct((T, LANES), I32),
                   jax.ShapeDtypeStruct((T, LANES), F32),
                   jax.ShapeDtypeStruct((SUBLANES, LANES), I32)],
        grid=(T // tm,),
        in_specs=[pl.BlockSpec((tm, D), row),
                  pl.BlockSpec((tm, D), row),
                  pl.BlockSpec((tm, D), row),
                  pl.BlockSpec((tm, D), row),
                  pl.BlockSpec((tm, D), row),
                  pl.BlockSpec((1, 6, D), lambda i: (i // per_b, 0, 0)),
                  pl.BlockSpec((1, D), const),
                  pl.BlockSpec(wo.shape, const),
                  pl.BlockSpec(wr.shape, const),
                  pl.BlockSpec((1, LANES), const)],
        out_specs=[pl.BlockSpec((tm, D), row),
                   pl.BlockSpec((tm, D // 2), row),
                   pl.BlockSpec((tm, LANES), row),
                   pl.BlockSpec((tm, LANES), row),
                   pl.BlockSpec((SUBLANES, LANES), const)],
        scratch_shapes=[pltpu.VMEM((SUBLANES, LANES), F32)],
        compiler_params=_params("arbitrary"),
        name="merge",
    )(x2, ya, yb, sga, sgb, mod3, g2, wo, wr, br)


def _sc_workers():
    info = plsc.get_sparse_core_info()
    mesh = plsc.VectorSubcoreMesh(core_axis_name="core", subcore_axis_name="subcore")
    params = pltpu.CompilerParams()
    if "needs_layout_passes" in pltpu.CompilerParams.__dataclass_fields__:
        params = dataclasses.replace(params, needs_layout_passes=False)
    return info, mesh, params


def _rows_from_codes(code_v, base_v, idx_v, lanes):
    for j in range(code_v.shape[0] // lanes):
        c = code_v[pl.ds(j * lanes, lanes)]
        expert = lax.shift_right_logical(c, RANK_BITS)
        idx_v[pl.ds(j * lanes, lanes)] = plsc.load_gather(base_v, [expert]) + (c & (RANK_RADIX - 1))


def _two_slot_loop(n_chunks, start, finish):
    start(0, 0)

    @pl.loop(0, n_chunks, step=2)
    def _(c):
        start(c + 1, 1)
        finish(c, 0)

        @pl.when(c + 2 < n_chunks)
        def _():
            start(c + 2, 0)

        finish(c + 1, 1)


def _sc_dispatch(h2, code0, code1, base, n_rows):
    T, D = h2.shape
    info, mesh, params = _sc_workers()
    n_workers = info.num_cores * info.num_subcores
    w = SC_WINDOW_BYTES // (D * h2.dtype.itemsize)
    per_w = T // n_workers
    n_chunks = per_w // w
    assert per_w * n_workers == T and n_chunks * w == per_w and n_chunks % 2 == 0

    @functools.partial(
        pl.kernel, out_type=jax.ShapeDtypeStruct((n_rows, D), h2.dtype), mesh=mesh, compiler_params=params,
        scratch_types=[pltpu.VMEM((N_EXPERTS,), I32), pltpu.VMEM((w,), I32), pltpu.VMEM((w,), I32),
                       pltpu.VMEM((w,), I32), pltpu.VMEM((w, D), h2.dtype), pltpu.VMEM((w, D), h2.dtype),
                       pltpu.SemaphoreType.DMA, pltpu.SemaphoreType.DMA])
    def scatter(h_hbm, c0_hbm, c1_hbm, b_hbm, xs_hbm, base_v, code_v, i0_v, i1_v, rows0, rows1, sem0, sem1):
        wid = lax.axis_index("subcore") * info.num_cores + lax.axis_index("core")
        w0 = wid * per_w
        pltpu.sync_copy(b_hbm, base_v)
        rows = (rows0, rows1)
        sems = (sem0, sem1)

        def start(c, slot):
            pltpu.async_copy(h_hbm.at[pl.ds(w0 + c * w, w)], rows[slot], sems[slot])

        def finish(c, slot):
            pltpu.sync_copy(c0_hbm.at[pl.ds(w0 + c * w, w)], code_v)
            _rows_from_codes(code_v, base_v, i0_v, info.num_lanes)
            pltpu.sync_copy(c1_hbm.at[pl.ds(w0 + c * w, w)], code_v)
            _rows_from_codes(code_v, base_v, i1_v, info.num_lanes)
            pltpu.make_async_copy(h_hbm.at[pl.ds(w0 + c * w, w)], rows[slot], sems[slot]).wait()
            pltpu.sync_copy(rows[slot], xs_hbm.at[i0_v])
            pltpu.sync_copy(rows[slot], xs_hbm.at[i1_v])

        _two_slot_loop(n_chunks, start, finish)

    return scatter(h2, code0, code1, base)


def _sc_collect(ys, codes, base):
    n = codes.shape[0]
    D = ys.shape[1]
    info, mesh, params = _sc_workers()
    n_workers = info.num_cores * info.num_subcores
    w = SC_WINDOW_BYTES // (D * ys.dtype.itemsize)
    per_w = n // n_workers
    n_chunks = per_w // w
    assert per_w * n_workers == n and n_chunks * w == per_w and n_chunks % 2 == 0

    @functools.partial(
        pl.kernel, out_type=jax.ShapeDtypeStruct((n, D), ys.dtype), mesh=mesh, compiler_params=params,
        scratch_types=[pltpu.VMEM((N_EXPERTS,), I32), pltpu.VMEM((w,), I32), pltpu.VMEM((w,), I32),
                       pltpu.VMEM((w,), I32), pltpu.VMEM((w, D), ys.dtype), pltpu.VMEM((w, D), ys.dtype),
                       pltpu.SemaphoreType.DMA, pltpu.SemaphoreType.DMA])
    def gather(ys_hbm, c_hbm, b_hbm, yk_hbm, base_v, code_v, i0_v, i1_v, rows0, rows1, sem0, sem1):
        wid = lax.axis_index("subcore") * info.num_cores + lax.axis_index("core")
        w0 = wid * per_w
        pltpu.sync_copy(b_hbm, base_v)
        idx = (i0_v, i1_v)
        rows = (rows0, rows1)
        sems = (sem0, sem1)

        def start(c, slot):
            pltpu.sync_copy(c_hbm.at[pl.ds(w0 + c * w, w)], code_v)
            _rows_from_codes(code_v, base_v, idx[slot], info.num_lanes)
            pltpu.async_copy(ys_hbm.at[idx[slot]], rows[slot], sems[slot])

        def finish(c, slot):
            pltpu.make_async_copy(ys_hbm.at[idx[slot]], rows[slot], sems[slot]).wait()
            pltpu.sync_copy(rows[slot], yk_hbm.at[pl.ds(w0 + c * w, w)])

        _two_slot_loop(n_chunks, start, finish)

    return gather(ys, codes, base)


def _schedule_kernel(cnt_ref, te_ref, tb_ref, base_ref, nt_ref):
    tm = EXPERT_TM

    def expert(e, t0):
        n = (cnt_ref[e] + tm - 1) // tm
        base_ref[e] = t0 * tm

        def tile(t, c):
            te_ref[t] = e
            tb_ref[t] = t
            return c

        lax.fori_loop(t0, t0 + n, tile, 0)
        return t0 + n

    nt = lax.fori_loop(0, N_EXPERTS, expert, 0)
    nt_ref[0] = nt
    last = te_ref[nt - 1]

    def idle(t, c):
        te_ref[t] = last
        tb_ref[t] = nt - 1
        return c

    lax.fori_loop(nt, te_ref.shape[0], idle, 0)


def _schedule(counts, max_tiles):
    smem = pl.BlockSpec(memory_space=pltpu.SMEM)
    return pl.pallas_call(
        _schedule_kernel,
        out_shape=[jax.ShapeDtypeStruct((max_tiles,), I32),
                   jax.ShapeDtypeStruct((max_tiles,), I32),
                   jax.ShapeDtypeStruct((N_EXPERTS,), I32),
                   jax.ShapeDtypeStruct((1,), I32)],
        in_specs=[smem],
        out_specs=[smem, smem, smem, smem],
        name="schedule",
    )(counts)


def _expert_kernel(te_ref, tb_ref, nt_ref, xs_ref, wg_ref, wu_ref, wd_ref, ys_ref):
    @pl.when(pl.program_id(0) < nt_ref[0])
    def _():
        xb = _unpack_bf16_pairs(xs_ref[...])
        g = jnp.dot(xb, wg_ref[0], preferred_element_type=F32)
        u = jnp.dot(xb, wu_ref[0], preferred_element_type=F32)
        act = (g * _sigmoid(g)) * u
        ys_ref[...] = jnp.dot(act.astype(BF16), wd_ref[0], preferred_element_type=F32)


def _experts(tile_e, tile_b, n_tiles, xs, wg, wu, wd, max_tiles):
    P = xs.shape[0]
    D, de = wg.shape[1:]
    tm = EXPERT_TM
    wmap = lambda j, te, tb, nt: (te[j], 0, 0)
    rmap = lambda j, te, tb, nt: (tb[j], 0)
    return pl.pallas_call(
        _expert_kernel,
        out_shape=jax.ShapeDtypeStruct((P, D), F32),
        grid_spec=pltpu.PrefetchScalarGridSpec(
            num_scalar_prefetch=3,
            grid=(max_tiles,),
            in_specs=[pl.BlockSpec((tm, xs.shape[1]), rmap),
                      pl.BlockSpec((1, D, de), wmap),
                      pl.BlockSpec((1, D, de), wmap),
                      pl.BlockSpec((1, de, D), wmap)],
            out_specs=pl.BlockSpec((tm, D), rmap)),
        compiler_params=_params("arbitrary"),
        name="experts",
    )(tile_e, tile_b, n_tiles, xs, wg, wu, wd)


def _combine_kernel(x1_ref, rf_ref, mod_ref, gf_ref, y0_ref, y1_ref, out_ref, *, final_norm):
    gate2 = mod_ref[0, 5:6, :]
    w = rf_ref[...]
    moe = w[:, 0:1] * y0_ref[...] + w[:, 1:2] * y1_ref[...]
    x2 = x1_ref[...] + gate2 * moe
    if final_norm:
        ms = jnp.mean(x2 * x2, axis=-1, keepdims=True)
        x2 = x2 * lax.rsqrt(ms + RMS_EPS) * gf_ref[...]
    out_ref[...] = x2


def _combine(x1, rf, mod3, gf, yk, seq, final_norm):
    T, D = x1.shape
    tc = COMBINE_TM
    per_b = seq // tc
    n_blk = T // tc
    return pl.pallas_call(
        functools.partial(_combine_kernel, final_norm=final_norm),
        out_shape=jax.ShapeDtypeStruct((T, D), F32),
        grid=(n_blk,),
        in_specs=[pl.BlockSpec((tc, D), lambda i: (i, 0)),
                  pl.BlockSpec((tc, LANES), lambda i: (i, 0)),
                  pl.BlockSpec((1, 6, D), lambda i: (i // per_b, 0, 0)),
                  pl.BlockSpec((1, D), lambda i: (0, 0)),
                  pl.BlockSpec((tc, D), lambda i: (i, 0)),
                  pl.BlockSpec((tc, D), lambda i: (n_blk + i, 0))],
        out_specs=pl.BlockSpec((tc, D), lambda i: (i, 0)),
        compiler_params=_params("arbitrary"),
        name="combine",
    )(x1, rf, mod3, gf, yk, yk)


def _layer(x2, c, seq, w_ada, b_ada, g_norm1, w_in, b_if, conv_dw_w, conv_dw_b, conv_ln_g, conv_ln_b,
           w_conv_out, qk_conv_w, qk_conv_b, m_norm_g, w_m_out, w_out, g_norm2, w_rg, b_rg,
           w_re, b_re, w_e_gate, w_e_up, w_e_down):
    T, D = x2.shape
    B = T // seq
    dc = D // 2
    nif = 2 * M_HEADS

    mod3 = _ada(c, w_ada, b_ada).reshape(B, 6, D)

    if_lo = 6 * dc
    w_main = w_in[:, :if_lo].astype(BF16)
    w_gates = w_in[:, if_lo + nif:].astype(BF16)
    w_if = w_in[:, if_lo:if_lo + nif]
    w_if_pad = jnp.pad(w_if, ((0, 0), (0, LANES - nif))).astype(BF16)
    w_ift = w_if.T.astype(BF16)
    u, qk, v, o, sga, sgb, ifc, ifr = _inproj(x2, mod3, g_norm1.reshape(1, D), w_main, w_gates,
                                              w_if_pad, w_ift, seq)

    ya = _conv_branch(u.reshape(B, seq, dc), conv_dw_w, conv_dw_b.reshape(1, dc),
                      conv_ln_g.reshape(1, dc), conv_ln_b.reshape(1, dc), w_conv_out.astype(BF16))
    bifc = jnp.pad(b_if, (0, LANES - nif)).reshape(1, LANES)
    bifr = b_if.reshape(nif, 1)
    yb, (wg_b, wu_b, wd_b) = _mlstm_branch(
        qk.reshape(B, seq, 2 * dc), v.reshape(B, seq, dc), o.reshape(B, seq, dc),
        ifc.reshape(B, seq, LANES), ifr, qk_conv_w, qk_conv_b.reshape(1, 2 * dc), bifc, bifr,
        m_norm_g.reshape(1, dc), w_m_out.astype(BF16), riders=(w_e_gate, w_e_up, w_e_down))

    n_r = N_GROUPS + N_EXPERTS
    w_r = jnp.pad(jnp.concatenate([w_rg, w_re], axis=1), ((0, 0), (0, LANES - n_r))).astype(BF16)
    b_r = jnp.pad(jnp.concatenate([b_rg, b_re]), (0, LANES - n_r)).reshape(1, LANES)
    x1, h2, ri, rf, cnt = _merge(x2, ya.reshape(T, D), yb.reshape(T, D), sga, sgb, mod3,
                                 g_norm2.reshape(1, D), w_out.astype(BF16), w_r, b_r, seq)

    tm = EXPERT_TM
    max_tiles = (T * TOP_K) // tm + N_EXPERTS
    tile_e, tile_b, base, n_tiles = _schedule(cnt[0, :N_EXPERTS], max_tiles)

    code0 = ri[:, 0]
    code1 = ri[:, 1]
    xs = _sc_dispatch(h2, code0, code1, base, max_tiles * tm)
    ys = _experts(tile_e, tile_b, n_tiles, xs, wg_b, wu_b, wd_b, max_tiles)
    return x1, rf, mod3, ys, code0, code1, base


def kernel(x, c, w_ada, b_ada, g_norm1, w_in, b_if, conv_dw_w, conv_dw_b, conv_ln_g, conv_ln_b,
           w_conv_out, qk_conv_w, qk_conv_b, m_norm_g, w_m_out, w_out, g_norm2, w_rg, b_rg,
           w_re, b_re, w_e_gate, w_e_up, w_e_down, g_final):
    B, S, D = x.shape
    depth = w_ada.shape[0]
    x2 = x.reshape(B * S, D)
    for l in range(depth):
        x1, rf, mod3, ys, code0, code1, base = _layer(
            x2, c, S, w_ada[l], b_ada[l], g_norm1[l], w_in[l], b_if[l], conv_dw_w[l], conv_dw_b[l],
            conv_ln_g[l], conv_ln_b[l], w_conv_out[l], qk_conv_w[l], qk_conv_b[l], m_norm_g[l],
            w_m_out[l], w_out[l], g_norm2[l], w_rg[l], b_rg[l], w_re[l], b_re[l],
            w_e_gate[l], w_e_up[l], w_e_down[l])
        yk = _sc_collect(ys, jnp.concatenate([code0, code1]), base)
        x2 = _combine(x1, rf, mod3, g_final.reshape(1, D), yk, S, final_norm=l == depth - 1)
    return x2.reshape(B, S, D)
```

```python
import dataclasses
import functools

import jax
import jax.numpy as jnp
from jax import lax
from jax.experimental import pallas as pl
from jax.experimental.pallas import tpu as pltpu
from jax.experimental.pallas import tpu_sc as plsc

F32 = jnp.float32
BF16 = jnp.bfloat16
I32 = jnp.int32

M_HEADS = 4
CONV_WIDTH = 31
QK_CONV_WIDTH = 4
N_GROUPS = 4
E_PER_GROUP = 8
N_EXPERTS = N_GROUPS * E_PER_GROUP
TOP_K = 2
RMS_EPS = 1e-6
LN_EPS = 1e-5

LANES = 128
SUBLANES = 8
VMEM_LIMIT = 56 * 1024 * 1024

ADA_TN = 1024
INPROJ_TM = 512
INPROJ_SUB = 256
CONV_TS = 512
CONV_HALO = 32
CONV_RC = 32
MLSTM_L = 128
MLSTM_SEQS = 4
MERGE_TM = 512
MERGE_SUB = 256
EXPERT_TM = 384
SC_WINDOW_BYTES = 128 * 1024
COMBINE_TM = 256
RANK_BITS = 16
RANK_RADIX = 1 << RANK_BITS


def _sigmoid(v):
    return 1.0 / (1.0 + jnp.exp(-v))


def _log_sigmoid(v):
    return -(jnp.maximum(-v, 0.0) + jnp.log1p(jnp.exp(-jnp.abs(v))))


def _pack_bf16_pairs(v):
    n = v.shape[1] // 2
    bits = lax.bitcast_convert_type(v.astype(BF16).astype(F32), jnp.uint32)
    word = bits[:, n:] | (bits[:, :n] >> 16)
    return lax.bitcast_convert_type(word, I32)


def _unpack_bf16_pairs(w):
    bits = lax.bitcast_convert_type(w, jnp.uint32)
    lo = lax.bitcast_convert_type(bits << 16, F32)
    hi = lax.bitcast_convert_type(bits & jnp.uint32(0xFFFF0000), F32)
    return jnp.concatenate([lo, hi], axis=1).astype(BF16)


def _params(*sem):
    return pltpu.CompilerParams(dimension_semantics=sem, vmem_limit_bytes=VMEM_LIMIT)


def _ada_kernel(c_ref, w_ref, b_ref, o_ref):
    c = c_ref[...]
    s = c * _sigmoid(c)
    o_ref[...] = jnp.dot(s, w_ref[...], preferred_element_type=F32,
                         precision=lax.Precision.HIGHEST) + b_ref[...]


def _ada(c, w_ada, b_ada):
    B, D = c.shape
    N = w_ada.shape[1]
    return pl.pallas_call(
        _ada_kernel,
        out_shape=jax.ShapeDtypeStruct((B, N), F32),
        grid=(N // ADA_TN,),
        in_specs=[pl.BlockSpec((B, D), lambda j: (0, 0)),
                  pl.BlockSpec((D, ADA_TN), lambda j: (0, j)),
                  pl.BlockSpec((1, ADA_TN), lambda j: (0, j))],
        out_specs=pl.BlockSpec((B, ADA_TN), lambda j: (0, j)),
        compiler_params=_params("arbitrary"),
        name="ada",
    )(c, w_ada, b_ada.reshape(1, N))


def _inproj_kernel(x_ref, mod_ref, g_ref, wm_ref, wgt_ref, wif_ref, wift_ref,
                   u_ref, qk_ref, v_ref, o_ref, sga_ref, sgb_ref, ifc_ref, ifr_ref):
    shift = mod_ref[0, 0:1, :]
    scale = mod_ref[0, 1:2, :]
    dc = u_ref.shape[1]
    d = sga_ref.shape[1]
    ts = INPROJ_SUB
    subs = [pl.ds(r0, ts) for r0 in range(0, x_ref.shape[0], ts)]

    hbs = []
    for sl in subs:
        x = x_ref[sl, :]
        ms = jnp.mean(x * x, axis=-1, keepdims=True)
        h = x * lax.rsqrt(ms + RMS_EPS) * g_ref[...]
        h = h * (1.0 + scale) + shift
        hbs.append(h.astype(BF16))

    for sl, hb in zip(subs, hbs):
        def seg(lo, hi):
            return jnp.dot(hb, wm_ref[:, lo:hi], preferred_element_type=F32)

        u_ref[sl, :] = seg(0, dc) * _sigmoid(seg(dc, 2 * dc))
        qk_ref[sl, :] = seg(2 * dc, 4 * dc)
        v_ref[sl, :] = seg(4 * dc, 5 * dc).astype(BF16)
        o_ref[sl, :] = seg(5 * dc, 6 * dc)
        sga_ref[sl, :] = _sigmoid(jnp.dot(hb, wgt_ref[:, 0:d], preferred_element_type=F32)).astype(BF16)
        sgb_ref[sl, :] = _sigmoid(jnp.dot(hb, wgt_ref[:, d:2 * d], preferred_element_type=F32)).astype(BF16)
        ifc_ref[sl, :] = jnp.dot(hb, wif_ref[...], preferred_element_type=F32)
        ifr_ref[0, :, sl] = lax.dot_general(wift_ref[...], hb, (((1,), (1,)), ((), ())),
                                            preferred_element_type=F32)


def _inproj(x2, mod3, g1, w_main, w_gates, w_if, w_ift, seq):
    T, D = x2.shape
    tm = INPROJ_TM
    dc = D // 2
    per_b = seq // tm
    row = lambda i: (i, 0)
    const = lambda i: (0, 0)
    return pl.pallas_call(
        _inproj_kernel,
        out_shape=[jax.ShapeDtypeStruct((T, dc), F32),
                   jax.ShapeDtypeStruct((T, 2 * dc), F32),
                   jax.ShapeDtypeStruct((T, dc), BF16),
                   jax.ShapeDtypeStruct((T, dc), F32),
                   jax.ShapeDtypeStruct((T, D), BF16),
                   jax.ShapeDtypeStruct((T, D), BF16),
                   jax.ShapeDtypeStruct((T, LANES), F32),
                   jax.ShapeDtypeStruct((T // seq, SUBLANES, seq), F32)],
        grid=(T // tm,),
        in_specs=[pl.BlockSpec((tm, D), row),
                  pl.BlockSpec((1, 6, D), lambda i: (i // per_b, 0, 0)),
                  pl.BlockSpec((1, D), const),
                  pl.BlockSpec(w_main.shape, const),
                  pl.BlockSpec(w_gates.shape, const),
                  pl.BlockSpec(w_if.shape, const),
                  pl.BlockSpec(w_ift.shape, const)],
        out_specs=[pl.BlockSpec((tm, dc), row),
                   pl.BlockSpec((tm, 2 * dc), row),
                   pl.BlockSpec((tm, dc), row),
                   pl.BlockSpec((tm, dc), row),
                   pl.BlockSpec((tm, D), row),
                   pl.BlockSpec((tm, D), row),
                   pl.BlockSpec((tm, LANES), row),
                   pl.BlockSpec((1, SUBLANES, tm), lambda i: (i // per_b, 0, i % per_b))],
        compiler_params=_params("arbitrary"),
        name="inproj",
    )(x2, mod3, g1, w_main, w_gates, w_if, w_ift)


def _conv_kernel(u_ref, w_ref, b_ref, lg_ref, lb_ref, wo_ref, y_ref, ubuf, sbuf, cbuf):
    ts = u_ref.shape[1]
    halo = CONV_HALO

    @pl.when(pl.program_id(1) == 0)
    def _():
        ubuf[0:halo, :] = jnp.zeros((halo, ubuf.shape[1]), F32)

    ubuf[halo:halo + ts, :] = u_ref[0]
    ns = sbuf.shape[1]
    for r in range(1, SUBLANES):
        sbuf[r - 1] = ubuf[r:r + ns, :]
    off = halo - (CONV_WIDTH - 1)
    for r0 in range(0, ts, CONV_RC):
        acc = jnp.broadcast_to(b_ref[...], (CONV_RC, ubuf.shape[1]))
        for k in range(CONV_WIDTH):
            r = (off + k) % SUBLANES
            lo = off + k - r + r0
            win = ubuf[lo:lo + CONV_RC, :] if r == 0 else sbuf[r - 1, lo:lo + CONV_RC, :]
            acc = acc + w_ref[k:k + 1, :] * win
        cbuf[r0:r0 + CONV_RC, :] = acc
    ubuf[0:halo, :] = ubuf[ts:ts + halo, :]

    a = cbuf[...]
    mu = jnp.mean(a, axis=-1, keepdims=True)
    ac = a - mu
    var = jnp.mean(ac * ac, axis=-1, keepdims=True)
    z = ac * lax.rsqrt(var + LN_EPS) * lg_ref[...] + lb_ref[...]
    z = z * _sigmoid(z)
    y_ref[0] = jnp.dot(z.astype(BF16), wo_ref[...], preferred_element_type=F32).astype(BF16)


def _conv_branch(u3, w, b, lg, lb, wo):
    B, S, C = u3.shape
    D = wo.shape[1]
    ts = CONV_TS
    const = lambda bi, si: (0, 0)
    return pl.pallas_call(
        _conv_kernel,
        out_shape=jax.ShapeDtypeStruct((B, S, D), BF16),
        grid=(B, S // ts),
        in_specs=[pl.BlockSpec((1, ts, C), lambda bi, si: (bi, si, 0)),
                  pl.BlockSpec(w.shape, const),
                  pl.BlockSpec((1, C), const),
                  pl.BlockSpec((1, C), const),
                  pl.BlockSpec((1, C), const),
                  pl.BlockSpec(wo.shape, const)],
        out_specs=pl.BlockSpec((1, ts, D), lambda bi, si: (bi, si, 0)),
        scratch_shapes=[pltpu.VMEM((ts + CONV_HALO, C), F32),
                        pltpu.VMEM((SUBLANES - 1, ts + CONV_HALO - SUBLANES, C), F32),
                        pltpu.VMEM((ts, C), F32)],
        compiler_params=_params("arbitrary", "arbitrary"),
        name="conv",
    )(u3, w, b, lg, lb, wo)


def _mlstm_kernel(*refs, n_riders):
    (qk_ref, v_ref, o_ref, ifc_ref, ifr_ref, cw_ref, cb_ref, bifc_ref, bifr_ref, ng_ref, wo_ref) = refs[:11]
    rider_in = refs[11:11 + n_riders]
    y_ref = refs[11 + n_riders]
    rider_out = refs[12 + n_riders:12 + 2 * n_riders]
    qkbuf, cn_ref, m_ref, hbuf = refs[12 + 2 * n_riders:]

    for src, dst in zip(rider_in, rider_out):
        dst[...] = src[...].astype(BF16)

    @pl.when(pl.program_id(1) == 0)
    def _():
        qkbuf[:, 0:SUBLANES, :] = jnp.zeros((qkbuf.shape[0], SUBLANES, qkbuf.shape[2]), F32)
        cn_ref[...] = jnp.zeros(cn_ref.shape, F32)
        m_ref[...] = jnp.zeros(m_ref.shape, F32)

    nb, L, mi = hbuf.shape
    dh = mi // M_HEADS
    halo = SUBLANES
    off = halo - (QK_CONV_WIDTH - 1)
    rows = lax.broadcasted_iota(I32, (L, L), 0)
    cols = lax.broadcasted_iota(I32, (L, L), 1)
    causal = cols <= rows
    lower = causal.astype(F32)
    upper = (rows <= cols).astype(F32)
    lane = lax.broadcasted_iota(I32, (L, dh), 1)
    ones_col = jnp.where(lane == 0, 1.0, 0.0).astype(BF16)
    scale = dh ** -0.5

    seqs = []
    for b in range(nb):
        qkbuf[b, halo:halo + L, :] = qk_ref[b]
        y = jnp.broadcast_to(cb_ref[...], (L, qkbuf.shape[2]))
        for k in range(QK_CONV_WIDTH):
            y = y + cw_ref[k:k + 1, :] * qkbuf[b, off + k:off + k + L, :]
        y = y * _sigmoid(y)
        qkbuf[b, 0:halo, :] = qkbuf[b, L:L + halo, :]
        ifr = ifr_ref[b] + bifr_ref[...]
        ifc = ifc_ref[b] + bifc_ref[...]
        bcum_c = jnp.dot(lower, _log_sigmoid(ifc), preferred_element_type=F32,
                         precision=lax.Precision.HIGHEST)
        bcum_r = jnp.dot(_log_sigmoid(ifr), upper, preferred_element_type=F32,
                         precision=lax.Precision.HIGHEST)
        seqs.append((y, ifr, bcum_c, bcum_r))

    probs = [(b, hd) for b in range(nb) for hd in range(M_HEADS)]
    st = {}
    for p in probs:
        b, hd = p
        y, ifr, bcum_c, bcum_r = seqs[b]
        c0 = hd * dh
        qb = (y[:, c0:c0 + dh] * scale).astype(BF16)
        kt = y[:, mi + c0:mi + c0 + dh].T
        v = v_ref[b, :, c0:c0 + dh]
        bc = bcum_c[:, M_HEADS + hd:M_HEADS + hd + 1]
        br = bcum_r[M_HEADS + hd:M_HEADS + hd + 1, :]
        li = ifr[hd:hd + 1, :]
        m_prev = m_ref[b, hd, 0:1, 0:1]
        dmat = jnp.where(causal, bc - br + li, -jnp.inf)
        st[p] = dict(qb=qb, kt=kt, v=v, bc=bc, br=br, li=li, m_prev=m_prev, dmat=dmat)
    for p in probs:
        s = st[p]
        s["inter"] = s["bc"] + s["m_prev"]
        s["m_t"] = jnp.maximum(jnp.max(s["dmat"], axis=-1, keepdims=True), s["inter"])
    for p in probs:
        s = st[p]
        s["qk"] = jnp.dot(s["qb"], s["kt"].astype(BF16), preferred_element_type=F32)
    for p in probs:
        b, hd = p
        s = st[p]
        s["cn"] = cn_ref[b, hd]
        s["qcn"] = jnp.dot(s["qb"], s["cn"].astype(BF16), preferred_element_type=F32)
    for p in probs:
        s = st[p]
        s["wts"] = jnp.exp(s["dmat"] - s["m_t"])
        s["s_inter"] = jnp.exp(s["inter"] - s["m_t"])
    for p in probs:
        s = st[p]
        s["s_mat"] = s["qk"] * s["wts"]
    for p in probs:
        s = st[p]
        s["sv"] = jnp.dot(s["s_mat"].astype(BF16), s["v"], preferred_element_type=F32)
    for p in probs:
        s = st[p]
        s["rowsum"] = jnp.sum(s["s_mat"], axis=-1, keepdims=True)
    for p in probs:
        s = st[p]
        s["num"] = s["sv"] + s["s_inter"] * s["qcn"][:, 0:dh]
        s["den"] = s["rowsum"] + s["s_inter"] * s["qcn"][:, dh:dh + 1]
    for p in probs:
        b, hd = p
        s = st[p]
        b_last = s["br"][:, L - 1:L]
        a = b_last - s["br"] + s["li"]
        m_new = jnp.maximum(b_last + s["m_prev"], jnp.max(a, axis=-1, keepdims=True))
        wk = jnp.exp(a - m_new)
        sc = jnp.exp(b_last + s["m_prev"] - m_new)
        v_ext = jnp.concatenate([s["v"], ones_col], axis=1)
        cn_ref[b, hd] = sc * s["cn"] + jnp.dot((s["kt"] * wk).astype(BF16), v_ext, preferred_element_type=F32)
        m_ref[b, hd] = jnp.broadcast_to(m_new, m_ref.shape[2:])
    for p in probs:
        s = st[p]
        s["hh"] = s["num"] / jnp.maximum(jnp.abs(s["den"]), jnp.exp(-s["m_t"]))
        s["mu"] = jnp.mean(s["hh"], axis=-1, keepdims=True)
    for p in probs:
        s = st[p]
        s["hc"] = s["hh"] - s["mu"]
        s["var"] = jnp.mean(s["hc"] * s["hc"], axis=-1, keepdims=True)
    for p in probs:
        b, hd = p
        s = st[p]
        c0 = hd * dh
        hn = s["hc"] * lax.rsqrt(s["var"] + LN_EPS) * ng_ref[:, c0:c0 + dh]
        hbuf[b, :, c0:c0 + dh] = hn * _sigmoid(o_ref[b, :, c0:c0 + dh])
    for b in range(nb):
        y = jnp.dot(hbuf[b].astype(BF16), wo_ref[...], preferred_element_type=F32)
        y_ref[b] = y.astype(BF16)


def _mlstm_branch(qk3, v3, o3, ifc3, ifr3, cw, cb, bifc, bifr, ng, wo, riders):
    B, S, C2 = qk3.shape
    mi = v3.shape[2]
    dh = mi // M_HEADS
    D = wo.shape[1]
    L = MLSTM_L
    nb = MLSTM_SEQS
    ns = S // L
    n_steps = (B // nb) * ns
    const = lambda bi, ci: (0, 0)
    tile = lambda bi, ci: (bi, ci, 0)
    slab = lambda bi, ci: (bi * ns + ci, 0, 0)
    slabs = [r.reshape(n_steps, -1, r.shape[-1]) for r in riders]
    outs = pl.pallas_call(
        functools.partial(_mlstm_kernel, n_riders=len(riders)),
        out_shape=[jax.ShapeDtypeStruct((B, S, D), BF16)]
        + [jax.ShapeDtypeStruct(s.shape, BF16) for s in slabs],
        grid=(B // nb, ns),
        in_specs=[pl.BlockSpec((nb, L, C2), tile),
                  pl.BlockSpec((nb, L, mi), tile),
                  pl.BlockSpec((nb, L, mi), tile),
                  pl.BlockSpec((nb, L, LANES), tile),
                  pl.BlockSpec((nb, SUBLANES, L), lambda bi, ci: (bi, 0, ci)),
                  pl.BlockSpec(cw.shape, const),
                  pl.BlockSpec((1, C2), const),
                  pl.BlockSpec((1, LANES), const),
                  pl.BlockSpec((SUBLANES, 1), const),
                  pl.BlockSpec((1, mi), const),
                  pl.BlockSpec(wo.shape, const)]
        + [pl.BlockSpec((1,) + s.shape[1:], slab) for s in slabs],
        out_specs=[pl.BlockSpec((nb, L, D), tile)]
        + [pl.BlockSpec((1,) + s.shape[1:], slab) for s in slabs],
        scratch_shapes=[pltpu.VMEM((nb, L + SUBLANES, C2), F32),
                        pltpu.VMEM((nb, M_HEADS, dh, 2 * dh), F32),
                        pltpu.VMEM((nb, M_HEADS, SUBLANES, LANES), F32),
                        pltpu.VMEM((nb, L, mi), F32)],
        compiler_params=_params("arbitrary", "arbitrary"),
        name="mlstm",
    )(qk3, v3, o3, ifc3, ifr3, cw, cb, bifc, bifr, ng, wo, *slabs)
    return outs[0], [o.reshape(r.shape) for o, r in zip(outs[1:], riders)]


def _merge_kernel(x_ref, ya_ref, yb_ref, sga_ref, sgb_ref, mod_ref, g2_ref, wo_ref, wr_ref, br_ref,
                  x1_ref, h2_ref, ri_ref, rf_ref, cnt_ref, run_ref):
    ts = MERGE_SUB
    subs = [pl.ds(r0, ts) for r0 in range(0, x_ref.shape[0], ts)]

    @pl.when(pl.program_id(0) == 0)
    def _():
        run_ref[...] = jnp.zeros(run_ref.shape, F32)

    gate1 = mod_ref[0, 2:3, :]
    shift2 = mod_ref[0, 3:4, :]
    scale2 = mod_ref[0, 4:5, :]
    lane = lax.broadcasted_iota(I32, (ts, LANES), 1).astype(F32)
    neg = -jnp.inf
    rows = lax.broadcasted_iota(I32, (ts, ts), 0)
    cols = lax.broadcasted_iota(I32, (ts, ts), 1)
    strict = jnp.where(cols < rows, 1.0, 0.0).astype(BF16)

    def first_argmax(vals):
        mx = jnp.max(vals, axis=-1, keepdims=True)
        idx = jnp.min(jnp.where(vals == mx, lane, float(LANES)), axis=-1, keepdims=True)
        return mx, idx

    h2s = []
    for sl in subs:
        merged = (sga_ref[sl, :].astype(F32) * ya_ref[sl, :].astype(F32)
                  + sgb_ref[sl, :].astype(F32) * yb_ref[sl, :].astype(F32))
        mix = jnp.dot(merged.astype(BF16), wo_ref[...], preferred_element_type=F32)
        x1 = x_ref[sl, :] + gate1 * mix
        x1_ref[sl, :] = x1
        ms = jnp.mean(x1 * x1, axis=-1, keepdims=True)
        h2 = x1 * lax.rsqrt(ms + RMS_EPS) * g2_ref[...]
        h2 = h2 * (1.0 + scale2) + shift2
        h2_ref[sl, :] = _pack_bf16_pairs(h2)
        h2s.append(h2.astype(BF16))

    run = run_ref[0:1, :]
    for sl, h2b in zip(subs, h2s):
        logits = jnp.dot(h2b, wr_ref[...], preferred_element_type=F32) + br_ref[...]
        lg = jnp.where(lane < N_GROUPS, logits, neg)
        gmax, gsel = first_argmax(lg)
        p_g = 1.0 / jnp.sum(jnp.exp(lg - gmax), axis=-1, keepdims=True)
        lo = N_GROUPS + gsel * E_PER_GROUP
        le = jnp.where((lane >= lo) & (lane < lo + E_PER_GROUP), logits, neg)
        l1, i1 = first_argmax(le)
        l2, i2 = first_argmax(jnp.where(lane == i1, neg, le))
        r = jnp.exp(l2 - l1)
        w1 = p_g / (1.0 + r)
        w2 = p_g * r / (1.0 + r)
        e1 = i1 - N_GROUPS
        e2 = i2 - N_GROUPS

        onehot = jnp.where((lane == e1) | (lane == e2), 1.0, 0.0)
        before = jnp.dot(strict, onehot.astype(BF16), preferred_element_type=F32) + run
        rank1 = jnp.sum(jnp.where(lane == e1, before, 0.0), axis=-1, keepdims=True)
        rank2 = jnp.sum(jnp.where(lane == e2, before, 0.0), axis=-1, keepdims=True)
        run = run + jnp.sum(onehot, axis=0, keepdims=True)

        ri_ref[sl, :] = jnp.where(lane == 0, e1 * float(RANK_RADIX) + rank1,
                                  jnp.where(lane == 1, e2 * float(RANK_RADIX) + rank2, 0.0)).astype(I32)
        rf_ref[sl, :] = jnp.where(lane == 0, w1, jnp.where(lane == 1, w2, 0.0))
    run_ref[...] = jnp.broadcast_to(run, run_ref.shape)
    cnt_ref[...] = jnp.broadcast_to(run, cnt_ref.shape).astype(I32)


def _merge(x2, ya, yb, sga, sgb, mod3, g2, wo, wr, br, seq):
    T, D = x2.shape
    tm = MERGE_TM
    per_b = seq // tm
    row = lambda i: (i, 0)
    const = lambda i: (0, 0)
    return pl.pallas_call(
        _merge_kernel,
        out_shape=[jax.ShapeDtypeStruct((T, D), F32),
                   jax.ShapeDtypeStruct((T, D // 2), I32),
                   jax.ShapeDtypeStruct((T, LANES), I32),
                   jax.ShapeDtypeStruct((T, LANES), F32),
                   jax.ShapeDtypeStruct((SUBLANES, LANES), I32)],
        grid=(T // tm,),
        in_specs=[pl.BlockSpec((tm, D), row),
                  pl.BlockSpec((tm, D), row),
                  pl.BlockSpec((tm, D), row),
                  pl.BlockSpec((tm, D), row),
                  pl.BlockSpec((tm, D), row),
                  pl.BlockSpec((1, 6, D), lambda i: (i // per_b, 0, 0)),
                  pl.BlockSpec((1, D), const),
                  pl.BlockSpec(wo.shape, const),
                  pl.BlockSpec(wr.shape, const),
                  pl.BlockSpec((1, LANES), const)],
        out_specs=[pl.BlockSpec((tm, D), row),
                   pl.BlockSpec((tm, D // 2), row),
                   pl.BlockSpec((tm, LANES), row),
                   pl.BlockSpec((tm, LANES), row),
                   pl.BlockSpec((SUBLANES, LANES), const)],
        scratch_shapes=[pltpu.VMEM((SUBLANES, LANES), F32)],
        compiler_params=_params("arbitrary"),
        name="merge",
    )(x2, ya, yb, sga, sgb, mod3, g2, wo, wr, br)


def _sc_workers():
    info = plsc.get_sparse_core_info()
    mesh = plsc.VectorSubcoreMesh(core_axis_name="core", subcore_axis_name="subcore")
    params = pltpu.CompilerParams()
    if "needs_layout_passes" in pltpu.CompilerParams.__dataclass_fields__:
        params = dataclasses.replace(params, needs_layout_passes=False)
    return info, mesh, params


def _rows_from_codes(code_v, base_v, idx_v, lanes):
    for j in range(code_v.shape[0] // lanes):
        c = code_v[pl.ds(j * lanes, lanes)]
        expert = lax.shift_right_logical(c, RANK_BITS)
        idx_v[pl.ds(j * lanes, lanes)] = plsc.load_gather(base_v, [expert]) + (c & (RANK_RADIX - 1))


def _two_slot_loop(n_chunks, start, finish):
    start(0, 0)

    @pl.loop(0, n_chunks, step=2)
    def _(c):
        start(c + 1, 1)
        finish(c, 0)

        @pl.when(c + 2 < n_chunks)
        def _():
            start(c + 2, 0)

        finish(c + 1, 1)


def _sc_dispatch(h2, code0, code1, base, n_rows):
    T, D = h2.shape
    info, mesh, params = _sc_workers()
    n_workers = info.num_cores * info.num_subcores
    w = SC_WINDOW_BYTES // (D * h2.dtype.itemsize)
    per_w = T // n_workers
    n_chunks = per_w // w
    assert per_w * n_workers == T and n_chunks * w == per_w and n_chunks % 2 == 0

    @functools.partial(
        pl.kernel, out_type=jax.ShapeDtypeStruct((n_rows, D), h2.dtype), mesh=mesh, compiler_params=params,
        scratch_types=[pltpu.VMEM((N_EXPERTS,), I32), pltpu.VMEM((w,), I32), pltpu.VMEM((w,), I32),
                       pltpu.VMEM((w,), I32), pltpu.VMEM((w, D), h2.dtype), pltpu.VMEM((w, D), h2.dtype),
                       pltpu.SemaphoreType.DMA, pltpu.SemaphoreType.DMA])
    def scatter(h_hbm, c0_hbm, c1_hbm, b_hbm, xs_hbm, base_v, code_v, i0_v, i1_v, rows0, rows1, sem0, sem1):
        wid = lax.axis_index("subcore") * info.num_cores + lax.axis_index("core")
        w0 = wid * per_w
        pltpu.sync_copy(b_hbm, base_v)
        rows = (rows0, rows1)
        sems = (sem0, sem1)

        def start(c, slot):
            pltpu.async_copy(h_hbm.at[pl.ds(w0 + c * w, w)], rows[slot], sems[slot])

        def finish(c, slot):
            pltpu.sync_copy(c0_hbm.at[pl.ds(w0 + c * w, w)], code_v)
            _rows_from_codes(code_v, base_v, i0_v, info.num_lanes)
            pltpu.sync_copy(c1_hbm.at[pl.ds(w0 + c * w, w)], code_v)
            _rows_from_codes(code_v, base_v, i1_v, info.num_lanes)
            pltpu.make_async_copy(h_hbm.at[pl.ds(w0 + c * w, w)], rows[slot], sems[slot]).wait()
            pltpu.sync_copy(rows[slot], xs_hbm.at[i0_v])
            pltpu.sync_copy(rows[slot], xs_hbm.at[i1_v])

        _two_slot_loop(n_chunks, start, finish)

    return scatter(h2, code0, code1, base)


def _sc_collect(ys, codes, base):
    n = codes.shape[0]
    D = ys.shape[1]
    info, mesh, params = _sc_workers()
    n_workers = info.num_cores * info.num_subcores
    w = SC_WINDOW_BYTES // (D * ys.dtype.itemsize)
    per_w = n // n_workers
    n_chunks = per_w // w
    assert per_w * n_workers == n and n_chunks * w == per_w and n_chunks % 2 == 0

    @functools.partial(
        pl.kernel, out_type=jax.ShapeDtypeStruct((n, D), ys.dtype), mesh=mesh, compiler_params=params,
        scratch_types=[pltpu.VMEM((N_EXPERTS,), I32), pltpu.VMEM((w,), I32), pltpu.VMEM((w,), I32),
                       pltpu.VMEM((w,), I32), pltpu.VMEM((w, D), ys.dtype), pltpu.VMEM((w, D), ys.dtype),
                       pltpu.SemaphoreType.DMA, pltpu.SemaphoreType.DMA])
    def gather(ys_hbm, c_hbm, b_hbm, yk_hbm, base_v, code_v, i0_v, i1_v, rows0, rows1, sem0, sem1):
        wid = lax.axis_index("subcore") * info.num_cores + lax.axis_index("core")
        w0 = wid * per_w
        pltpu.sync_copy(b_hbm, base_v)
        idx = (i0_v, i1_v)
        rows = (rows0, rows1)
        sems = (sem0, sem1)

        def start(c, slot):
            pltpu.sync_copy(c_hbm.at[pl.ds(w0 + c * w, w)], code_v)
            _rows_from_codes(code_v, base_v, idx[slot], info.num_lanes)
            pltpu.async_copy(ys_hbm.at[idx[slot]], rows[slot], sems[slot])

        def finish(c, slot):
            pltpu.make_async_copy(ys_hbm.at[idx[slot]], rows[slot], sems[slot]).wait()
            pltpu.sync_copy(rows[slot], yk_hbm.at[pl.ds(w0 + c * w, w)])

        _two_slot_loop(n_chunks, start, finish)

    return gather(ys, codes, base)


def _schedule_kernel(cnt_ref, te_ref, tb_ref, base_ref, nt_ref):
    tm = EXPERT_TM

    def expert(e, t0):
        n = (cnt_ref[e] + tm - 1) // tm
        base_ref[e] = t0 * tm

        def tile(t, c):
            te_ref[t] = e
            tb_ref[t] = t
            return c

        lax.fori_loop(t0, t0 + n, tile, 0)
        return t0 + n

    nt = lax.fori_loop(0, N_EXPERTS, expert, 0)
    nt_ref[0] = nt
    last = te_ref[nt - 1]

    def idle(t, c):
        te_ref[t] = last
        tb_ref[t] = nt - 1
        return c

    lax.fori_loop(nt, te_ref.shape[0], idle, 0)


def _schedule(counts, max_tiles):
    smem = pl.BlockSpec(memory_space=pltpu.SMEM)
    return pl.pallas_call(
        _schedule_kernel,
        out_shape=[jax.ShapeDtypeStruct((max_tiles,), I32),
                   jax.ShapeDtypeStruct((max_tiles,), I32),
                   jax.ShapeDtypeStruct((N_EXPERTS,), I32),
                   jax.ShapeDtypeStruct((1,), I32)],
        in_specs=[smem],
        out_specs=[smem, smem, smem, smem],
        name="schedule",
    )(counts)


def _expert_kernel(te_ref, tb_ref, nt_ref, xs_ref, wg_ref, wu_ref, wd_ref, ys_ref):
    @pl.when(pl.program_id(0) < nt_ref[0])
    def _():
        xb = _unpack_bf16_pairs(xs_ref[...])
        g = jnp.dot(xb, wg_ref[0], preferred_element_type=F32)
        u = jnp.dot(xb, wu_ref[0], preferred_element_type=F32)
        act = (g * _sigmoid(g)) * u
        ys_ref[...] = jnp.dot(act.astype(BF16), wd_ref[0], preferred_element_type=F32)


def _experts(tile_e, tile_b, n_tiles, xs, wg, wu, wd, max_tiles):
    P = xs.shape[0]
    D, de = wg.shape[1:]
    tm = EXPERT_TM
    wmap = lambda j, te, tb, nt: (te[j], 0, 0)
    rmap = lambda j, te, tb, nt: (tb[j], 0)
    return pl.pallas_call(
        _expert_kernel,
        out_shape=jax.ShapeDtypeStruct((P, D), F32),
        grid_spec=pltpu.PrefetchScalarGridSpec(
            num_scalar_prefetch=3,
            grid=(max_tiles,),
            in_specs=[pl.BlockSpec((tm, xs.shape[1]), rmap),
                      pl.BlockSpec((1, D, de), wmap),
                      pl.BlockSpec((1, D, de), wmap),
                      pl.BlockSpec((1, de, D), wmap)],
            out_specs=pl.BlockSpec((tm, D), rmap)),
        compiler_params=_params("arbitrary"),
        name="experts",
    )(tile_e, tile_b, n_tiles, xs, wg, wu, wd)


def _combine_kernel(x1_ref, rf_ref, mod_ref, gf_ref, y0_ref, y1_ref, out_ref, *, final_norm):
    gate2 = mod_ref[0, 5:6, :]
    w = rf_ref[...]
    moe = w[:, 0:1] * y0_ref[...] + w[:, 1:2] * y1_ref[...]
    x2 = x1_ref[...] + gate2 * moe
    if final_norm:
        ms = jnp.mean(x2 * x2, axis=-1, keepdims=True)
        x2 = x2 * lax.rsqrt(ms + RMS_EPS) * gf_ref[...]
    out_ref[...] = x2


def _combine(x1, rf, mod3, gf, yk, seq, final_norm):
    T, D = x1.shape
    tc = COMBINE_TM
    per_b = seq // tc
    n_blk = T // tc
    return pl.pallas_call(
        functools.partial(_combine_kernel, final_norm=final_norm),
        out_shape=jax.ShapeDtypeStruct((T, D), F32),
        grid=(n_blk,),
        in_specs=[pl.BlockSpec((tc, D), lambda i: (i, 0)),
                  pl.BlockSpec((tc, LANES), lambda i: (i, 0)),
                  pl.BlockSpec((1, 6, D), lambda i: (i // per_b, 0, 0)),
                  pl.BlockSpec((1, D), lambda i: (0, 0)),
                  pl.BlockSpec((tc, D), lambda i: (i, 0)),
                  pl.BlockSpec((tc, D), lambda i: (n_blk + i, 0))],
        out_specs=pl.BlockSpec((tc, D), lambda i: (i, 0)),
        compiler_params=_params("arbitrary"),
        name="combine",
    )(x1, rf, mod3, gf, yk, yk)


def _layer(x2, c, seq, w_ada, b_ada, g_norm1, w_in, b_if, conv_dw_w, conv_dw_b, conv_ln_g, conv_ln_b,
           w_conv_out, qk_conv_w, qk_conv_b, m_norm_g, w_m_out, w_out, g_norm2, w_rg, b_rg,
           w_re, b_re, w_e_gate, w_e_up, w_e_down):
    T, D = x2.shape
    B = T // seq
    dc = D // 2
    nif = 2 * M_HEADS

    mod3 = _ada(c, w_ada, b_ada).reshape(B, 6, D)

    if_lo = 6 * dc
    w_main = w_in[:, :if_lo].astype(BF16)
    w_gates = w_in[:, if_lo + nif:].astype(BF16)
    w_if = w_in[:, if_lo:if_lo + nif]
    w_if_pad = jnp.pad(w_if, ((0, 0), (0, LANES - nif))).astype(BF16)
    w_ift = w_if.T.astype(BF16)
    u, qk, v, o, sga, sgb, ifc, ifr = _inproj(x2, mod3, g_norm1.reshape(1, D), w_main, w_gates,
                                              w_if_pad, w_ift, seq)

    ya = _conv_branch(u.reshape(B, seq, dc), conv_dw_w, conv_dw_b.reshape(1, dc),
                      conv_ln_g.reshape(1, dc), conv_ln_b.reshape(1, dc), w_conv_out.astype(BF16))
    bifc = jnp.pad(b_if, (0, LANES - nif)).reshape(1, LANES)
    bifr = b_if.reshape(nif, 1)
    yb, (wg_b, wu_b, wd_b) = _mlstm_branch(
        qk.reshape(B, seq, 2 * dc), v.reshape(B, seq, dc), o.reshape(B, seq, dc),
        ifc.reshape(B, seq, LANES), ifr, qk_conv_w, qk_conv_b.reshape(1, 2 * dc), bifc, bifr,
        m_norm_g.reshape(1, dc), w_m_out.astype(BF16), riders=(w_e_gate, w_e_up, w_e_down))

    n_r = N_GROUPS + N_EXPERTS
    w_r = jnp.pad(jnp.concatenate([w_rg, w_re], axis=1), ((0, 0), (0, LANES - n_r))).astype(BF16)
    b_r = jnp.pad(jnp.concatenate([b_rg, b_re]), (0, LANES - n_r)).reshape(1, LANES)
    x1, h2, ri, rf, cnt = _merge(x2, ya.reshape(T, D), yb.reshape(T, D), sga, sgb, mod3,
                                 g_norm2.reshape(1, D), w_out.astype(BF16), w_r, b_r, seq)

    tm = EXPERT_TM
    max_tiles = (T * TOP_K) // tm + N_EXPERTS
    tile_e, tile_b, base, n_tiles = _schedule(cnt[0, :N_EXPERTS], max_tiles)

    code0 = ri[:, 0]
    code1 = ri[:, 1]
    xs = _sc_dispatch(h2, code0, code1, base, max_tiles * tm)
    ys = _experts(tile_e, tile_b, n_tiles, xs, wg_b, wu_b, wd_b, max_tiles)
    return x1, rf, mod3, ys, code0, code1, base


def kernel(x, c, w_ada, b_ada, g_norm1, w_in, b_if, conv_dw_w, conv_dw_b, conv_ln_g, conv_ln_b,
           w_conv_out, qk_conv_w, qk_conv_b, m_norm_g, w_m_out, w_out, g_norm2, w_rg, b_rg,
           w_re, b_re, w_e_gate, w_e_up, w_e_down, g_final):
    B, S, D = x.shape
    depth = w_ada.shape[0]
    x2 = x.reshape(B * S, D)
    for l in range(depth):
        x1, rf, mod3, ys, code0, code1, base = _layer(
            x2, c, S, w_ada[l], b_ada[l], g_norm1[l], w_in[l], b_if[l], conv_dw_w[l], conv_dw_b[l],
            conv_ln_g[l], conv_ln_b[l], w_conv_out[l], qk_conv_w[l], qk_conv_b[l], m_norm_g[l],
            w_m_out[l], w_out[l], g_norm2[l], w_rg[l], b_rg[l], w_re[l], b_re[l],
            w_e_gate[l], w_e_up[l], w_e_down[l])
        yk = _sc_collect(ys, jnp.concatenate([code0, code1]), base)
        x2 = _combine(x1, rf, mod3, g_final.reshape(1, D), yk, S, final_norm=l == depth - 1)
    return x2.reshape(B, S, D)
```

```python
import dataclasses
import functools

import jax
import jax.numpy as jnp
from jax import lax
from jax.experimental import pallas as pl
from jax.experimental.pallas import tpu as pltpu
from jax.experimental.pallas import tpu_sc as plsc

F32 = jnp.float32
BF16 = jnp.bfloat16
I32 = jnp.int32

M_HEADS = 4
CONV_WIDTH = 31
QK_CONV_WIDTH = 4
N_GROUPS = 4
E_PER_GROUP = 8
N_EXPERTS = N_GROUPS * E_PER_GROUP
TOP_K = 2
RMS_EPS = 1e-6
LN_EPS = 1e-5

LANES = 128
SUBLANES = 8
VMEM_LIMIT = 56 * 1024 * 1024

ADA_TN = 768
INPROJ_TM = 512
INPROJ_SUB = 256
CONV_TS = 512
CONV_HALO = 32
CONV_RC = 32
MLSTM_L = 128
MLSTM_SEQS = 4
MERGE_TM = 512
MERGE_SUB = 256
EXPERT_TM = 512
SC_WINDOW_BYTES = 128 * 1024
COMBINE_TM = 256
RANK_BITS = 16
RANK_RADIX = 1 << RANK_BITS
assert EXPERT_TM & (EXPERT_TM - 1) == 0


def _sigmoid(v):
    return 1.0 / (1.0 + jnp.exp(-v))


def _log_sigmoid(v):
    return -(jnp.maximum(-v, 0.0) + jnp.log1p(jnp.exp(-jnp.abs(v))))


def _pack_bf16_pairs(v):
    n = v.shape[1] // 2
    bits = lax.bitcast_convert_type(v.astype(BF16).astype(F32), jnp.uint32)
    word = bits[:, n:] | (bits[:, :n] >> 16)
    return lax.bitcast_convert_type(word, I32)


def _unpack_bf16_pairs(w):
    bits = lax.bitcast_convert_type(w, jnp.uint32)
    lo = lax.bitcast_convert_type(bits << 16, F32)
    hi = lax.bitcast_convert_type(bits & jnp.uint32(0xFFFF0000), F32)
    return jnp.concatenate([lo, hi], axis=1).astype(BF16)


def _params(*sem):
    return pltpu.CompilerParams(dimension_semantics=sem, vmem_limit_bytes=VMEM_LIMIT)


def _ada_kernel(*refs, cuts):
    c_ref, w_ref, b_ref = refs[:3]
    srcs = refs[3:3 + len(cuts)]
    o_ref = refs[3 + len(cuts)]
    dsts = iter(refs[4 + len(cuts):])
    c = c_ref[...]
    s = c * _sigmoid(c)
    o_ref[...] = jnp.dot(s, w_ref[...], preferred_element_type=F32,
                         precision=lax.Precision.HIGHEST) + b_ref[...]
    for src, ranges in zip(srcs, cuts):
        for lo, hi in ranges:
            next(dsts)[...] = src[:, lo:hi].astype(BF16)


def _ada(c, w_ada, b_ada, riders):
    B, D = c.shape
    N = w_ada.shape[1]
    n_steps = N // ADA_TN
    slab = lambda j: (j, 0)
    outs = pl.pallas_call(
        functools.partial(_ada_kernel, cuts=tuple(tuple(r) for _, r in riders)),
        out_shape=[jax.ShapeDtypeStruct((B, N), F32)]
        + [jax.ShapeDtypeStruct((w.shape[0], hi - lo), BF16) for w, r in riders for lo, hi in r],
        grid=(n_steps,),
        in_specs=[pl.BlockSpec((B, D), lambda j: (0, 0)),
                  pl.BlockSpec((D, ADA_TN), lambda j: (0, j)),
                  pl.BlockSpec((1, ADA_TN), lambda j: (0, j))]
        + [pl.BlockSpec((w.shape[0] // n_steps, w.shape[1]), slab) for w, _ in riders],
        out_specs=[pl.BlockSpec((B, ADA_TN), lambda j: (0, j))]
        + [pl.BlockSpec((w.shape[0] // n_steps, hi - lo), slab) for w, r in riders for lo, hi in r],
        compiler_params=_params("arbitrary"),
        name="ada",
    )(c, w_ada, b_ada.reshape(1, N), *[w for w, _ in riders])
    return outs[0], outs[1:]


def _inproj_kernel(x_ref, mod_ref, g_ref, wm_ref, wgt_ref, wif_ref, wift_ref,
                   u_ref, qk_ref, v_ref, o_ref, sga_ref, sgb_ref, ifc_ref, ifr_ref):
    shift = mod_ref[0, 0:1, :]
    scale = mod_ref[0, 1:2, :]
    dc = u_ref.shape[1]
    d = sga_ref.shape[1]
    ts = INPROJ_SUB
    subs = [pl.ds(r0, ts) for r0 in range(0, x_ref.shape[0], ts)]

    hbs = []
    for sl in subs:
        x = x_ref[sl, :]
        ms = jnp.mean(x * x, axis=-1, keepdims=True)
        h = x * lax.rsqrt(ms + RMS_EPS) * g_ref[...]
        h = h * (1.0 + scale) + shift
        hbs.append(h.astype(BF16))

    for sl, hb in zip(subs, hbs):
        def seg(lo, hi):
            return jnp.dot(hb, wm_ref[:, lo:hi], preferred_element_type=F32)

        u_ref[sl, :] = seg(0, dc) * _sigmoid(seg(dc, 2 * dc))
        qk_ref[sl, :] = seg(2 * dc, 4 * dc)
        v_ref[sl, :] = seg(4 * dc, 5 * dc).astype(BF16)
        o_ref[sl, :] = seg(5 * dc, 6 * dc)
        sga_ref[sl, :] = _sigmoid(jnp.dot(hb, wgt_ref[:, 0:d], preferred_element_type=F32)).astype(BF16)
        sgb_ref[sl, :] = _sigmoid(jnp.dot(hb, wgt_ref[:, d:2 * d], preferred_element_type=F32)).astype(BF16)
        ifc_ref[sl, :] = jnp.dot(hb, wif_ref[...], preferred_element_type=F32)
        ifr_ref[0, :, sl] = lax.dot_general(wift_ref[...], hb, (((1,), (1,)), ((), ())),
                                            preferred_element_type=F32)


def _inproj(x2, mod3, g1, w_main, w_gates, w_if, w_ift, seq):
    T, D = x2.shape
    tm = INPROJ_TM
    dc = D // 2
    per_b = seq // tm
    row = lambda i: (i, 0)
    const = lambda i: (0, 0)
    return pl.pallas_call(
        _inproj_kernel,
        out_shape=[jax.ShapeDtypeStruct((T, dc), F32),
                   jax.ShapeDtypeStruct((T, 2 * dc), F32),
                   jax.ShapeDtypeStruct((T, dc), BF16),
                   jax.ShapeDtypeStruct((T, dc), F32),
                   jax.ShapeDtypeStruct((T, D), BF16),
                   jax.ShapeDtypeStruct((T, D), BF16),
                   jax.ShapeDtypeStruct((T, LANES), F32),
                   jax.ShapeDtypeStruct((T // seq, SUBLANES, seq), F32)],
        grid=(T // tm,),
        in_specs=[pl.BlockSpec((tm, D), row),
                  pl.BlockSpec((1, 6, D), lambda i: (i // per_b, 0, 0)),
                  pl.BlockSpec((1, D), const),
                  pl.BlockSpec(w_main.shape, const),
                  pl.BlockSpec(w_gates.shape, const),
                  pl.BlockSpec(w_if.shape, const),
                  pl.BlockSpec(w_ift.shape, const)],
        out_specs=[pl.BlockSpec((tm, dc), row),
                   pl.BlockSpec((tm, 2 * dc), row),
                   pl.BlockSpec((tm, dc), row),
                   pl.BlockSpec((tm, dc), row),
                   pl.BlockSpec((tm, D), row),
                   pl.BlockSpec((tm, D), row),
                   pl.BlockSpec((tm, LANES), row),
                   pl.BlockSpec((1, SUBLANES, tm), lambda i: (i // per_b, 0, i % per_b))],
        compiler_params=_params("arbitrary"),
        name="inproj",
    )(x2, mod3, g1, w_main, w_gates, w_if, w_ift)


def _conv_kernel(u_ref, w_ref, b_ref, lg_ref, lb_ref, wo_ref, y_ref, ubuf, sbuf, cbuf):
    ts = u_ref.shape[1]
    halo = CONV_HALO

    @pl.when(pl.program_id(1) == 0)
    def _():
        ubuf[0:halo, :] = jnp.zeros((halo, ubuf.shape[1]), F32)

    ubuf[halo:halo + ts, :] = u_ref[0]
    ns = sbuf.shape[1]
    for r in range(1, SUBLANES):
        sbuf[r - 1] = ubuf[r:r + ns, :]
    off = halo - (CONV_WIDTH - 1)
    for r0 in range(0, ts, CONV_RC):
        acc = jnp.broadcast_to(b_ref[...], (CONV_RC, ubuf.shape[1]))
        for k in range(CONV_WIDTH):
            r = (off + k) % SUBLANES
            lo = off + k - r + r0
            win = ubuf[lo:lo + CONV_RC, :] if r == 0 else sbuf[r - 1, lo:lo + CONV_RC, :]
            acc = acc + w_ref[k:k + 1, :] * win
        cbuf[r0:r0 + CONV_RC, :] = acc
    ubuf[0:halo, :] = ubuf[ts:ts + halo, :]

    a = cbuf[...]
    mu = jnp.mean(a, axis=-1, keepdims=True)
    ac = a - mu
    var = jnp.mean(ac * ac, axis=-1, keepdims=True)
    z = ac * lax.rsqrt(var + LN_EPS) * lg_ref[...] + lb_ref[...]
    z = z * _sigmoid(z)
    y_ref[0] = jnp.dot(z.astype(BF16), wo_ref[...], preferred_element_type=F32).astype(BF16)


def _conv_branch(u3, w, b, lg, lb, wo):
    B, S, C = u3.shape
    D = wo.shape[1]
    ts = CONV_TS
    const = lambda bi, si: (0, 0)
    return pl.pallas_call(
        _conv_kernel,
        out_shape=jax.ShapeDtypeStruct((B, S, D), BF16),
        grid=(B, S // ts),
        in_specs=[pl.BlockSpec((1, ts, C), lambda bi, si: (bi, si, 0)),
                  pl.BlockSpec(w.shape, const),
                  pl.BlockSpec((1, C), const),
                  pl.BlockSpec((1, C), const),
                  pl.BlockSpec((1, C), const),
                  pl.BlockSpec(wo.shape, const)],
        out_specs=pl.BlockSpec((1, ts, D), lambda bi, si: (bi, si, 0)),
        scratch_shapes=[pltpu.VMEM((ts + CONV_HALO, C), F32),
                        pltpu.VMEM((SUBLANES - 1, ts + CONV_HALO - SUBLANES, C), F32),
                        pltpu.VMEM((ts, C), F32)],
        compiler_params=_params("arbitrary", "arbitrary"),
        name="conv",
    )(u3, w, b, lg, lb, wo)


def _mlstm_kernel(*refs, n_riders):
    (qk_ref, v_ref, o_ref, ifc_ref, ifr_ref, cw_ref, cb_ref, bifc_ref, bifr_ref, ng_ref, wo_ref) = refs[:11]
    rider_in = refs[11:11 + n_riders]
    y_ref = refs[11 + n_riders]
    rider_out = refs[12 + n_riders:12 + 2 * n_riders]
    qkbuf, cn_ref, m_ref, hbuf = refs[12 + 2 * n_riders:]

    for src, dst in zip(rider_in, rider_out):
        dst[...] = src[...].astype(BF16)

    @pl.when(pl.program_id(1) == 0)
    def _():
        qkbuf[:, 0:SUBLANES, :] = jnp.zeros((qkbuf.shape[0], SUBLANES, qkbuf.shape[2]), F32)
        cn_ref[...] = jnp.zeros(cn_ref.shape, F32)
        m_ref[...] = jnp.zeros(m_ref.shape, F32)

    nb, L, mi = hbuf.shape
    dh = mi // M_HEADS
    halo = SUBLANES
    off = halo - (QK_CONV_WIDTH - 1)
    rows = lax.broadcasted_iota(I32, (L, L), 0)
    cols = lax.broadcasted_iota(I32, (L, L), 1)
    causal = cols <= rows
    lower = causal.astype(F32)
    upper = (rows <= cols).astype(F32)
    lane = lax.broadcasted_iota(I32, (L, dh), 1)
    ones_col = jnp.where(lane == 0, 1.0, 0.0).astype(BF16)
    scale = dh ** -0.5

    seqs = []
    for b in range(nb):
        qkbuf[b, halo:halo + L, :] = qk_ref[b]
        y = jnp.broadcast_to(cb_ref[...], (L, qkbuf.shape[2]))
        for k in range(QK_CONV_WIDTH):
            y = y + cw_ref[k:k + 1, :] * qkbuf[b, off + k:off + k + L, :]
        y = y * _sigmoid(y)
        qkbuf[b, 0:halo, :] = qkbuf[b, L:L + halo, :]
        ifr = ifr_ref[b] + bifr_ref[...]
        ifc = ifc_ref[b] + bifc_ref[...]
        bcum_c = jnp.dot(lower, _log_sigmoid(ifc), preferred_element_type=F32,
                         precision=lax.Precision.HIGHEST)
        bcum_r = jnp.dot(_log_sigmoid(ifr), upper, preferred_element_type=F32,
                         precision=lax.Precision.HIGHEST)
        seqs.append((y, ifr, bcum_c, bcum_r))

    probs = [(b, hd) for b in range(nb) for hd in range(M_HEADS)]
    st = {}
    for p in probs:
        b, hd = p
        y, ifr, bcum_c, bcum_r = seqs[b]
        c0 = hd * dh
        qb = (y[:, c0:c0 + dh] * scale).astype(BF16)
        kt = y[:, mi + c0:mi + c0 + dh].T
        v = v_ref[b, :, c0:c0 + dh]
        bc = bcum_c[:, M_HEADS + hd:M_HEADS + hd + 1]
        br = bcum_r[M_HEADS + hd:M_HEADS + hd + 1, :]
        li = ifr[hd:hd + 1, :]
        m_prev = m_ref[b, hd, 0:1, 0:1]
        dmat = jnp.where(causal, bc - br + li, -jnp.inf)
        st[p] = dict(qb=qb, kt=kt, v=v, bc=bc, br=br, li=li, m_prev=m_prev, dmat=dmat)
    for p in probs:
        s = st[p]
        s["inter"] = s["bc"] + s["m_prev"]
        s["m_t"] = jnp.maximum(jnp.max(s["dmat"], axis=-1, keepdims=True), s["inter"])
    for p in probs:
        s = st[p]
        s["qk"] = jnp.dot(s["qb"], s["kt"].astype(BF16), preferred_element_type=F32)
    for p in probs:
        b, hd = p
        s = st[p]
        s["cn"] = cn_ref[b, hd]
        s["qcn"] = jnp.dot(s["qb"], s["cn"].astype(BF16), preferred_element_type=F32)
    for p in probs:
        s = st[p]
        s["wts"] = jnp.exp(s["dmat"] - s["m_t"])
        s["s_inter"] = jnp.exp(s["inter"] - s["m_t"])
    for p in probs:
        s = st[p]
        s["s_mat"] = s["qk"] * s["wts"]
    for p in probs:
        s = st[p]
        s["sv"] = jnp.dot(s["s_mat"].astype(BF16), s["v"], preferred_element_type=F32)
    for p in probs:
        s = st[p]
        s["rowsum"] = jnp.sum(s["s_mat"], axis=-1, keepdims=True)
    for p in probs:
        s = st[p]
        s["num"] = s["sv"] + s["s_inter"] * s["qcn"][:, 0:dh]
        s["den"] = s["rowsum"] + s["s_inter"] * s["qcn"][:, dh:dh + 1]
    for p in probs:
        b, hd = p
        s = st[p]
        b_last = s["br"][:, L - 1:L]
        a = b_last - s["br"] + s["li"]
        m_new = jnp.maximum(b_last + s["m_prev"], jnp.max(a, axis=-1, keepdims=True))
        wk = jnp.exp(a - m_new)
        sc = jnp.exp(b_last + s["m_prev"] - m_new)
        v_ext = jnp.concatenate([s["v"], ones_col], axis=1)
        cn_ref[b, hd] = sc * s["cn"] + jnp.dot((s["kt"] * wk).astype(BF16), v_ext, preferred_element_type=F32)
        m_ref[b, hd] = jnp.broadcast_to(m_new, m_ref.shape[2:])
    for p in probs:
        s = st[p]
        s["hh"] = s["num"] / jnp.maximum(jnp.abs(s["den"]), jnp.exp(-s["m_t"]))
        s["mu"] = jnp.mean(s["hh"], axis=-1, keepdims=True)
    for p in probs:
        s = st[p]
        s["hc"] = s["hh"] - s["mu"]
        s["var"] = jnp.mean(s["hc"] * s["hc"], axis=-1, keepdims=True)
    for p in probs:
        b, hd = p
        s = st[p]
        c0 = hd * dh
        hn = s["hc"] * lax.rsqrt(s["var"] + LN_EPS) * ng_ref[:, c0:c0 + dh]
        hbuf[b, :, c0:c0 + dh] = hn * _sigmoid(o_ref[b, :, c0:c0 + dh])
    for b in range(nb):
        y = jnp.dot(hbuf[b].astype(BF16), wo_ref[...], preferred_element_type=F32)
        y_ref[b] = y.astype(BF16)


def _mlstm_branch(qk3, v3, o3, ifc3, ifr3, cw, cb, bifc, bifr, ng, wo, riders):
    B, S, C2 = qk3.shape
    mi = v3.shape[2]
    dh = mi // M_HEADS
    D = wo.shape[1]
    L = MLSTM_L
    nb = MLSTM_SEQS
    ns = S // L
    n_steps = (B // nb) * ns
    const = lambda bi, ci: (0, 0)
    tile = lambda bi, ci: (bi, ci, 0)
    slab = lambda bi, ci: (bi * ns + ci, 0, 0)
    slabs = [r.reshape(n_steps, -1, r.shape[-1]) for r in riders]
    outs = pl.pallas_call(
        functools.partial(_mlstm_kernel, n_riders=len(riders)),
        out_shape=[jax.ShapeDtypeStruct((B, S, D), BF16)]
        + [jax.ShapeDtypeStruct(s.shape, BF16) for s in slabs],
        grid=(B // nb, ns),
        in_specs=[pl.BlockSpec((nb, L, C2), tile),
                  pl.BlockSpec((nb, L, mi), tile),
                  pl.BlockSpec((nb, L, mi), tile),
                  pl.BlockSpec((nb, L, LANES), tile),
                  pl.BlockSpec((nb, SUBLANES, L), lambda bi, ci: (bi, 0, ci)),
                  pl.BlockSpec(cw.shape, const),
                  pl.BlockSpec((1, C2), const),
                  pl.BlockSpec((1, LANES), const),
                  pl.BlockSpec((SUBLANES, 1), const),
                  pl.BlockSpec((1, mi), const),
                  pl.BlockSpec(wo.shape, const)]
        + [pl.BlockSpec((1,) + s.shape[1:], slab) for s in slabs],
        out_specs=[pl.BlockSpec((nb, L, D), tile)]
        + [pl.BlockSpec((1,) + s.shape[1:], slab) for s in slabs],
        scratch_shapes=[pltpu.VMEM((nb, L + SUBLANES, C2), F32),
                        pltpu.VMEM((nb, M_HEADS, dh, 2 * dh), F32),
                        pltpu.VMEM((nb, M_HEADS, SUBLANES, LANES), F32),
                        pltpu.VMEM((nb, L, mi), F32)],
        compiler_params=_params("arbitrary", "arbitrary"),
        name="mlstm",
    )(qk3, v3, o3, ifc3, ifr3, cw, cb, bifc, bifr, ng, wo, *slabs)
    return outs[0], [o.reshape(r.shape) for o, r in zip(outs[1:], riders)]


def _merge_kernel(x_ref, ya_ref, yb_ref, sga_ref, sgb_ref, mod_ref, g2_ref, wo_ref, wr_ref, br_ref,
                  x1_ref, h2_ref, ri_ref, rf_ref, cnt_ref, run_ref):
    ts = MERGE_SUB
    subs = [pl.ds(r0, ts) for r0 in range(0, x_ref.shape[0], ts)]

    @pl.when(pl.program_id(0) == 0)
    def _():
        run_ref[...] = jnp.zeros(run_ref.shape, F32)

    gate1 = mod_ref[0, 2:3, :]
    shift2 = mod_ref[0, 3:4, :]
    scale2 = mod_ref[0, 4:5, :]
    lane = lax.broadcasted_iota(I32, (ts, LANES), 1).astype(F32)
    neg = -jnp.inf
    rows = lax.broadcasted_iota(I32, (ts, ts), 0)
    cols = lax.broadcasted_iota(I32, (ts, ts), 1)
    strict = jnp.where(cols < rows, 1.0, 0.0).astype(BF16)

    def first_argmax(vals):
        mx = jnp.max(vals, axis=-1, keepdims=True)
        idx = jnp.min(jnp.where(vals == mx, lane, float(LANES)), axis=-1, keepdims=True)
        return mx, idx

    h2s = []
    for sl in subs:
        merged = (sga_ref[sl, :].astype(F32) * ya_ref[sl, :].astype(F32)
                  + sgb_ref[sl, :].astype(F32) * yb_ref[sl, :].astype(F32))
        mix = jnp.dot(merged.astype(BF16), wo_ref[...], preferred_element_type=F32)
        x1 = x_ref[sl, :] + gate1 * mix
        x1_ref[sl, :] = x1
        ms = jnp.mean(x1 * x1, axis=-1, keepdims=True)
        h2 = x1 * lax.rsqrt(ms + RMS_EPS) * g2_ref[...]
        h2 = h2 * (1.0 + scale2) + shift2
        h2_ref[sl, :] = _pack_bf16_pairs(h2)
        h2s.append(h2.astype(BF16))

    run = run_ref[0:1, :]
    for sl, h2b in zip(subs, h2s):
        logits = jnp.dot(h2b, wr_ref[...], preferred_element_type=F32) + br_ref[...]
        lg = jnp.where(lane < N_GROUPS, logits, neg)
        gmax, gsel = first_argmax(lg)
        p_g = 1.0 / jnp.sum(jnp.exp(lg - gmax), axis=-1, keepdims=True)
        lo = N_GROUPS + gsel * E_PER_GROUP
        le = jnp.where((lane >= lo) & (lane < lo + E_PER_GROUP), logits, neg)
        l1, i1 = first_argmax(le)
        l2, i2 = first_argmax(jnp.where(lane == i1, neg, le))
        r = jnp.exp(l2 - l1)
        w1 = p_g / (1.0 + r)
        w2 = p_g * r / (1.0 + r)
        e1 = i1 - N_GROUPS
        e2 = i2 - N_GROUPS

        onehot = jnp.where((lane == e1) | (lane == e2), 1.0, 0.0)
        before = jnp.dot(strict, onehot.astype(BF16), preferred_element_type=F32) + run
        rank1 = jnp.sum(jnp.where(lane == e1, before, 0.0), axis=-1, keepdims=True)
        rank2 = jnp.sum(jnp.where(lane == e2, before, 0.0), axis=-1, keepdims=True)
        run = run + jnp.sum(onehot, axis=0, keepdims=True)

        codes = jnp.where(lane == 0, e1 * float(RANK_RADIX) + rank1,
                          jnp.where(lane == 1, e2 * float(RANK_RADIX) + rank2, 0.0))
        ri_ref[:, sl] = codes.T[0:SUBLANES, :].astype(I32)
        rf_ref[sl, :] = jnp.where(lane == 0, w1, jnp.where(lane == 1, w2, 0.0))
    run_ref[...] = jnp.broadcast_to(run, run_ref.shape)
    cnt_ref[...] = jnp.broadcast_to(run, cnt_ref.shape).astype(I32)


def _merge(x2, ya, yb, sga, sgb, mod3, g2, wo, wr, br, seq):
    T, D = x2.shape
    tm = MERGE_TM
    per_b = seq // tm
    row = lambda i: (i, 0)
    const = lambda i: (0, 0)
    return pl.pallas_call(
        _merge_kernel,
        out_shape=[jax.ShapeDtypeStruct((T, D), F32),
                   jax.ShapeDtypeStruct((T, D // 2), I32),
                   jax.ShapeDtypeStruct((SUBLANES, T), I32),
                   jax.ShapeDtypeStruct((T, LANES), F32),
                   jax.ShapeDtypeStruct((SUBLANES, LANES), I32)],
        grid=(T // tm,),
        in_specs=[pl.BlockSpec((tm, D), row),
                  pl.BlockSpec((tm, D), row),
                  pl.BlockSpec((tm, D), row),
                  pl.BlockSpec((tm, D), row),
                  pl.BlockSpec((tm, D), row),
                  pl.BlockSpec((1, 6, D), lambda i: (i // per_b, 0, 0)),
                  pl.BlockSpec((1, D), const),
                  pl.BlockSpec(wo.shape, const),
                  pl.BlockSpec(wr.shape, const),
                  pl.BlockSpec((1, LANES), const)],
        out_specs=[pl.BlockSpec((tm, D), row),
                   pl.BlockSpec((tm, D // 2), row),
                   pl.BlockSpec((SUBLANES, tm), lambda i: (0, i)),
                   pl.BlockSpec((tm, LANES), row),
                   pl.BlockSpec((SUBLANES, LANES), const)],
        scratch_shapes=[pltpu.VMEM((SUBLANES, LANES), F32)],
        compiler_params=_params("arbitrary"),
        name="merge",
    )(x2, ya, yb, sga, sgb, mod3, g2, wo, wr, br)


def _sc_workers():
    info = plsc.get_sparse_core_info()
    mesh = plsc.VectorSubcoreMesh(core_axis_name="core", subcore_axis_name="subcore")
    params = pltpu.CompilerParams()
    if "needs_layout_passes" in pltpu.CompilerParams.__dataclass_fields__:
        params = dataclasses.replace(params, needs_layout_passes=False)
    return info, mesh, params


def _rows_from_codes(code_v, base_v, idx_v, lanes):
    for j in range(code_v.shape[0] // lanes):
        c = code_v[pl.ds(j * lanes, lanes)]
        expert = lax.shift_right_logical(c, RANK_BITS)
        idx_v[pl.ds(j * lanes, lanes)] = plsc.load_gather(base_v, [expert]) + (c & (RANK_RADIX - 1))


def _two_slot_loop(n_chunks, start, finish):
    start(0, 0)

    @pl.loop(0, n_chunks, step=2)
    def _(c):
        start(c + 1, 1)
        finish(c, 0)

        @pl.when(c + 2 < n_chunks)
        def _():
            start(c + 2, 0)

        finish(c + 1, 1)


def _sc_dispatch(h2, code0, code1, base, n_rows):
    T, D = h2.shape
    info, mesh, params = _sc_workers()
    n_workers = info.num_cores * info.num_subcores
    w = SC_WINDOW_BYTES // (D * h2.dtype.itemsize)
    per_w = T // n_workers
    n_chunks = per_w // w
    assert per_w * n_workers == T and n_chunks * w == per_w and n_chunks % 2 == 0

    @functools.partial(
        pl.kernel, out_type=jax.ShapeDtypeStruct((n_rows, D), h2.dtype), mesh=mesh, compiler_params=params,
        scratch_types=[pltpu.VMEM((N_EXPERTS,), I32), pltpu.VMEM((w,), I32), pltpu.VMEM((w,), I32),
                       pltpu.VMEM((w,), I32), pltpu.VMEM((w, D), h2.dtype), pltpu.VMEM((w, D), h2.dtype),
                       pltpu.SemaphoreType.DMA, pltpu.SemaphoreType.DMA])
    def scatter(h_hbm, c0_hbm, c1_hbm, b_hbm, xs_hbm, base_v, code_v, i0_v, i1_v, rows0, rows1, sem0, sem1):
        wid = lax.axis_index("subcore") * info.num_cores + lax.axis_index("core")
        w0 = wid * per_w
        pltpu.sync_copy(b_hbm, base_v)
        rows = (rows0, rows1)
        sems = (sem0, sem1)

        def start(c, slot):
            pltpu.async_copy(h_hbm.at[pl.ds(w0 + c * w, w)], rows[slot], sems[slot])

        def finish(c, slot):
            pltpu.sync_copy(c0_hbm.at[pl.ds(w0 + c * w, w)], code_v)
            _rows_from_codes(code_v, base_v, i0_v, info.num_lanes)
            pltpu.sync_copy(c1_hbm.at[pl.ds(w0 + c * w, w)], code_v)
            _rows_from_codes(code_v, base_v, i1_v, info.num_lanes)
            pltpu.make_async_copy(h_hbm.at[pl.ds(w0 + c * w, w)], rows[slot], sems[slot]).wait()
            pltpu.sync_copy(rows[slot], xs_hbm.at[i0_v])
            pltpu.sync_copy(rows[slot], xs_hbm.at[i1_v])

        _two_slot_loop(n_chunks, start, finish)

    return scatter(h2, code0, code1, base)


def _sc_collect(ys, codes, base):
    n = codes.shape[0]
    D = ys.shape[1]
    info, mesh, params = _sc_workers()
    n_workers = info.num_cores * info.num_subcores
    w = SC_WINDOW_BYTES // (D * ys.dtype.itemsize)
    per_w = n // n_workers
    n_chunks = per_w // w
    assert per_w * n_workers == n and n_chunks * w == per_w and n_chunks % 2 == 0

    @functools.partial(
        pl.kernel, out_type=jax.ShapeDtypeStruct((n, D), ys.dtype), mesh=mesh, compiler_params=params,
        scratch_types=[pltpu.VMEM((N_EXPERTS,), I32), pltpu.VMEM((w,), I32), pltpu.VMEM((w,), I32),
                       pltpu.VMEM((w,), I32), pltpu.VMEM((w, D), ys.dtype), pltpu.VMEM((w, D), ys.dtype),
                       pltpu.SemaphoreType.DMA, pltpu.SemaphoreType.DMA])
    def gather(ys_hbm, c_hbm, b_hbm, yk_hbm, base_v, code_v, i0_v, i1_v, rows0, rows1, sem0, sem1):
        wid = lax.axis_index("subcore") * info.num_cores + lax.axis_index("core")
        w0 = wid * per_w
        pltpu.sync_copy(b_hbm, base_v)
        idx = (i0_v, i1_v)
        rows = (rows0, rows1)
        sems = (sem0, sem1)

        def start(c, slot):
            pltpu.sync_copy(c_hbm.at[pl.ds(w0 + c * w, w)], code_v)
            _rows_from_codes(code_v, base_v, idx[slot], info.num_lanes)
            pltpu.async_copy(ys_hbm.at[idx[slot]], rows[slot], sems[slot])

        def finish(c, slot):
            pltpu.make_async_copy(ys_hbm.at[idx[slot]], rows[slot], sems[slot]).wait()
            pltpu.sync_copy(rows[slot], yk_hbm.at[pl.ds(w0 + c * w, w)])

        _two_slot_loop(n_chunks, start, finish)

    return gather(ys, codes, base)


def _schedule_kernel(cnt_ref, te_ref, tb_ref, base_ref, nt_ref):
    tm = EXPERT_TM

    def expert(e, t0):
        n = (cnt_ref[e] + tm - 1) // tm
        base_ref[e] = t0 * tm

        def tile(t, c):
            te_ref[t] = e
            tb_ref[t] = t
            return c

        lax.fori_loop(t0, t0 + n, tile, 0)
        return t0 + n

    nt = lax.fori_loop(0, N_EXPERTS, expert, 0)
    nt_ref[0] = nt
    last = te_ref[nt - 1]

    def idle(t, c):
        te_ref[t] = last
        tb_ref[t] = nt - 1
        return c

    lax.fori_loop(nt, te_ref.shape[0], idle, 0)


def _schedule(counts, max_tiles):
    smem = pl.BlockSpec(memory_space=pltpu.SMEM)
    return pl.pallas_call(
        _schedule_kernel,
        out_shape=[jax.ShapeDtypeStruct((max_tiles,), I32),
                   jax.ShapeDtypeStruct((max_tiles,), I32),
                   jax.ShapeDtypeStruct((N_EXPERTS,), I32),
                   jax.ShapeDtypeStruct((1,), I32)],
        in_specs=[smem],
        out_specs=[smem, smem, smem, smem],
        name="schedule",
    )(counts)


def _expert_kernel(te_ref, tb_ref, nt_ref, xs_ref, wg_ref, wu_ref, wd_ref, ys_ref):
    @pl.when(pl.program_id(0) < nt_ref[0])
    def _():
        xb = _unpack_bf16_pairs(xs_ref[...])
        g = jnp.dot(xb, wg_ref[0], preferred_element_type=F32)
        u = jnp.dot(xb, wu_ref[0], preferred_element_type=F32)
        act = (g * _sigmoid(g)) * u
        ys_ref[...] = jnp.dot(act.astype(BF16), wd_ref[0], preferred_element_type=F32)


def _experts(tile_e, tile_b, n_tiles, xs, wg, wu, wd, max_tiles):
    P = xs.shape[0]
    D, de = wg.shape[1:]
    tm = EXPERT_TM
    wmap = lambda j, te, tb, nt: (te[j], 0, 0)
    rmap = lambda j, te, tb, nt: (tb[j], 0)
    return pl.pallas_call(
        _expert_kernel,
        out_shape=jax.ShapeDtypeStruct((P, D), F32),
        grid_spec=pltpu.PrefetchScalarGridSpec(
            num_scalar_prefetch=3,
            grid=(max_tiles,),
            in_specs=[pl.BlockSpec((tm, xs.shape[1]), rmap),
                      pl.BlockSpec((1, D, de), wmap),
                      pl.BlockSpec((1, D, de), wmap),
                      pl.BlockSpec((1, de, D), wmap)],
            out_specs=pl.BlockSpec((tm, D), rmap)),
        compiler_params=_params("arbitrary"),
        name="experts",
    )(tile_e, tile_b, n_tiles, xs, wg, wu, wd)


def _combine_kernel(x1_ref, rf_ref, mod_ref, gf_ref, y0_ref, y1_ref, out_ref, *, final_norm):
    gate2 = mod_ref[0, 5:6, :]
    w = rf_ref[...]
    moe = w[:, 0:1] * y0_ref[...] + w[:, 1:2] * y1_ref[...]
    x2 = x1_ref[...] + gate2 * moe
    if final_norm:
        ms = jnp.mean(x2 * x2, axis=-1, keepdims=True)
        x2 = x2 * lax.rsqrt(ms + RMS_EPS) * gf_ref[...]
    out_ref[...] = x2


def _combine(x1, rf, mod3, gf, yk, seq, final_norm):
    T, D = x1.shape
    tc = COMBINE_TM
    per_b = seq // tc
    n_blk = T // tc
    return pl.pallas_call(
        functools.partial(_combine_kernel, final_norm=final_norm),
        out_shape=jax.ShapeDtypeStruct((T, D), F32),
        grid=(n_blk,),
        in_specs=[pl.BlockSpec((tc, D), lambda i: (i, 0)),
                  pl.BlockSpec((tc, LANES), lambda i: (i, 0)),
                  pl.BlockSpec((1, 6, D), lambda i: (i // per_b, 0, 0)),
                  pl.BlockSpec((1, D), lambda i: (0, 0)),
                  pl.BlockSpec((tc, D), lambda i: (i, 0)),
                  pl.BlockSpec((tc, D), lambda i: (n_blk + i, 0))],
        out_specs=pl.BlockSpec((tc, D), lambda i: (i, 0)),
        compiler_params=_params("arbitrary"),
        name="combine",
    )(x1, rf, mod3, gf, yk, yk)


def _layer(x2, c, seq, w_ada, b_ada, g_norm1, w_in, b_if, conv_dw_w, conv_dw_b, conv_ln_g, conv_ln_b,
           w_conv_out, qk_conv_w, qk_conv_b, m_norm_g, w_m_out, w_out, g_norm2, w_rg, b_rg,
           w_re, b_re, w_e_gate, w_e_up, w_e_down):
    T, D = x2.shape
    B = T // seq
    dc = D // 2
    nif = 2 * M_HEADS

    if_lo = 6 * dc
    full = lambda w: (w, [(0, w.shape[1])])
    mod, (w_main, w_gates, w_out_b, w_conv_out_b, w_m_out_b) = _ada(
        c, w_ada, b_ada,
        [(w_in, [(0, if_lo), (if_lo + nif, w_in.shape[1])]), full(w_out), full(w_conv_out), full(w_m_out)])
    mod3 = mod.reshape(B, 6, D)
    w_if = w_in[:, if_lo:if_lo + nif]
    w_if_pad = jnp.pad(w_if, ((0, 0), (0, LANES - nif))).astype(BF16)
    w_ift = w_if.T.astype(BF16)
    u, qk, v, o, sga, sgb, ifc, ifr = _inproj(x2, mod3, g_norm1.reshape(1, D), w_main, w_gates,
                                              w_if_pad, w_ift, seq)

    ya = _conv_branch(u.reshape(B, seq, dc), conv_dw_w, conv_dw_b.reshape(1, dc),
                      conv_ln_g.reshape(1, dc), conv_ln_b.reshape(1, dc), w_conv_out_b)
    bifc = jnp.pad(b_if, (0, LANES - nif)).reshape(1, LANES)
    bifr = b_if.reshape(nif, 1)
    yb, (wg_b, wu_b, wd_b) = _mlstm_branch(
        qk.reshape(B, seq, 2 * dc), v.reshape(B, seq, dc), o.reshape(B, seq, dc),
        ifc.reshape(B, seq, LANES), ifr, qk_conv_w, qk_conv_b.reshape(1, 2 * dc), bifc, bifr,
        m_norm_g.reshape(1, dc), w_m_out_b, riders=(w_e_gate, w_e_up, w_e_down))

    n_r = N_GROUPS + N_EXPERTS
    w_r = jnp.pad(jnp.concatenate([w_rg, w_re], axis=1), ((0, 0), (0, LANES - n_r))).astype(BF16)
    b_r = jnp.pad(jnp.concatenate([b_rg, b_re]), (0, LANES - n_r)).reshape(1, LANES)
    x1, h2, ri, rf, cnt = _merge(x2, ya.reshape(T, D), yb.reshape(T, D), sga, sgb, mod3,
                                 g_norm2.reshape(1, D), w_out_b, w_r, b_r, seq)

    tm = EXPERT_TM
    max_tiles = (T * TOP_K) // tm + N_EXPERTS
    tile_e, tile_b, base, n_tiles = _schedule(cnt[0, :N_EXPERTS], max_tiles)

    code0 = ri[0]
    code1 = ri[1]
    xs = _sc_dispatch(h2, code0, code1, base, max_tiles * tm)
    ys = _experts(tile_e, tile_b, n_tiles, xs, wg_b, wu_b, wd_b, max_tiles)
    return x1, rf, mod3, ys, code0, code1, base


def kernel(x, c, w_ada, b_ada, g_norm1, w_in, b_if, conv_dw_w, conv_dw_b, conv_ln_g, conv_ln_b,
           w_conv_out, qk_conv_w, qk_conv_b, m_norm_g, w_m_out, w_out, g_norm2, w_rg, b_rg,
           w_re, b_re, w_e_gate, w_e_up, w_e_down, g_final):
    B, S, D = x.shape
    depth = w_ada.shape[0]
    x2 = x.reshape(B * S, D)
    for l in range(depth):
        x1, rf, mod3, ys, code0, code1, base = _layer(
            x2, c, S, w_ada[l], b_ada[l], g_norm1[l], w_in[l], b_if[l], conv_dw_w[l], conv_dw_b[l],
            conv_ln_g[l], conv_ln_b[l], w_conv_out[l], qk_conv_w[l], qk_conv_b[l], m_norm_g[l],
            w_m_out[l], w_out[l], g_norm2[l], w_rg[l], b_rg[l], w_re[l], b_re[l],
            w_e_gate[l], w_e_up[l], w_e_down[l])
        yk = _sc_collect(ys, jnp.concatenate([code0, code1]), base)
        x2 = _combine(x1, rf, mod3, g_final.reshape(1, D), yk, S, final_norm=l == depth - 1)
    return x2.reshape(B, S, D)
```

```python
import dataclasses
import functools

import jax
import jax.numpy as jnp
from jax import lax
from jax.experimental import pallas as pl
from jax.experimental.pallas import tpu as pltpu
from jax.experimental.pallas import tpu_sc as plsc

F32 = jnp.float32
BF16 = jnp.bfloat16
I32 = jnp.int32

M_HEADS = 4
CONV_WIDTH = 31
QK_CONV_WIDTH = 4
N_GROUPS = 4
E_PER_GROUP = 8
N_EXPERTS = N_GROUPS * E_PER_GROUP
TOP_K = 2
RMS_EPS = 1e-6
LN_EPS = 1e-5

LANES = 128
SUBLANES = 8
VMEM_LIMIT = 56 * 1024 * 1024

ADA_TN = 768
INPROJ_TM = 512
INPROJ_SUB = 256
CONV_TS = 512
CONV_HALO = 32
CONV_RC = 32
MLSTM_L = 128
MLSTM_SEQS = 4
MERGE_TM = 512
MERGE_SUB = 256
EXPERT_TM = 512
SC_WINDOW_BYTES = 128 * 1024
COMBINE_TM = 256
RANK_BITS = 16
RANK_RADIX = 1 << RANK_BITS
assert EXPERT_TM & (EXPERT_TM - 1) == 0


def _sigmoid(v):
    return 1.0 / (1.0 + jnp.exp(-v))


def _log_sigmoid(v):
    return -(jnp.maximum(-v, 0.0) + jnp.log1p(jnp.exp(-jnp.abs(v))))


def _pack_bf16_pairs(v):
    n = v.shape[1] // 2
    bits = lax.bitcast_convert_type(v.astype(BF16).astype(F32), jnp.uint32)
    word = bits[:, n:] | (bits[:, :n] >> 16)
    return lax.bitcast_convert_type(word, I32)


def _unpack_bf16_pairs(w):
    bits = lax.bitcast_convert_type(w, jnp.uint32)
    lo = lax.bitcast_convert_type(bits << 16, F32)
    hi = lax.bitcast_convert_type(bits & jnp.uint32(0xFFFF0000), F32)
    return jnp.concatenate([lo, hi], axis=1).astype(BF16)


def _params(*sem):
    return pltpu.CompilerParams(dimension_semantics=sem, vmem_limit_bytes=VMEM_LIMIT)


def _ada_kernel(*refs, cuts):
    c_ref, w_ref, b_ref = refs[:3]
    srcs = refs[3:3 + len(cuts)]
    o_ref = refs[3 + len(cuts)]
    dsts = iter(refs[4 + len(cuts):])
    c = c_ref[...]
    s = c * _sigmoid(c)
    o_ref[...] = jnp.dot(s, w_ref[...], preferred_element_type=F32,
                         precision=lax.Precision.HIGHEST) + b_ref[...]
    for src, ranges in zip(srcs, cuts):
        for lo, hi in ranges:
            next(dsts)[...] = src[:, lo:hi].astype(BF16)


def _ada(c, w_ada, b_ada, layer, riders):
    B, D = c.shape
    N = w_ada.shape[1]
    n_steps = N // ADA_TN
    slab = lambda j: (j, 0)
    outs = pl.pallas_call(
        functools.partial(_ada_kernel, cuts=tuple(tuple(r) for _, r in riders)),
        out_shape=[jax.ShapeDtypeStruct((B, N), F32)]
        + [jax.ShapeDtypeStruct((w.shape[1], hi - lo), BF16) for w, r in riders for lo, hi in r],
        grid=(n_steps,),
        in_specs=[pl.BlockSpec((B, D), lambda j: (0, 0)),
                  pl.BlockSpec((D, ADA_TN), lambda j: (0, j)),
                  pl.BlockSpec((1, ADA_TN), lambda j: (0, j))]
        + [pl.BlockSpec((None, w.shape[1] // n_steps, w.shape[2]), lambda j: (layer, j, 0))
           for w, _ in riders],
        out_specs=[pl.BlockSpec((B, ADA_TN), lambda j: (0, j))]
        + [pl.BlockSpec((w.shape[1] // n_steps, hi - lo), slab) for w, r in riders for lo, hi in r],
        compiler_params=_params("arbitrary"),
        name="ada",
    )(c, w_ada, b_ada.reshape(1, N), *[w for w, _ in riders])
    return outs[0], outs[1:]


def _inproj_kernel(x_ref, mod_ref, g_ref, wm_ref, wgt_ref, wif_ref, wift_ref,
                   u_ref, qk_ref, v_ref, o_ref, sga_ref, sgb_ref, ifc_ref, ifr_ref):
    shift = mod_ref[0, 0:1, :]
    scale = mod_ref[0, 1:2, :]
    dc = u_ref.shape[1]
    d = sga_ref.shape[1]
    ts = INPROJ_SUB
    subs = [pl.ds(r0, ts) for r0 in range(0, x_ref.shape[0], ts)]

    hbs = []
    for sl in subs:
        x = x_ref[sl, :]
        ms = jnp.mean(x * x, axis=-1, keepdims=True)
        h = x * lax.rsqrt(ms + RMS_EPS) * g_ref[...]
        h = h * (1.0 + scale) + shift
        hbs.append(h.astype(BF16))

    for sl, hb in zip(subs, hbs):
        def seg(lo, hi):
            return jnp.dot(hb, wm_ref[:, lo:hi], preferred_element_type=F32)

        u_ref[sl, :] = seg(0, dc) * _sigmoid(seg(dc, 2 * dc))
        qk_ref[sl, :] = seg(2 * dc, 4 * dc)
        v_ref[sl, :] = seg(4 * dc, 5 * dc).astype(BF16)
        o_ref[sl, :] = seg(5 * dc, 6 * dc)
        sga_ref[sl, :] = _sigmoid(jnp.dot(hb, wgt_ref[:, 0:d], preferred_element_type=F32)).astype(BF16)
        sgb_ref[sl, :] = _sigmoid(jnp.dot(hb, wgt_ref[:, d:2 * d], preferred_element_type=F32)).astype(BF16)
        ifc_ref[sl, :] = jnp.dot(hb, wif_ref[...], preferred_element_type=F32)
        ifr_ref[0, :, sl] = lax.dot_general(wift_ref[...], hb, (((1,), (1,)), ((), ())),
                                            preferred_element_type=F32)


def _inproj(x2, mod3, g1, w_main, w_gates, w_if, w_ift, seq):
    T, D = x2.shape
    tm = INPROJ_TM
    dc = D // 2
    per_b = seq // tm
    row = lambda i: (i, 0)
    const = lambda i: (0, 0)
    return pl.pallas_call(
        _inproj_kernel,
        out_shape=[jax.ShapeDtypeStruct((T, dc), F32),
                   jax.ShapeDtypeStruct((T, 2 * dc), F32),
                   jax.ShapeDtypeStruct((T, dc), BF16),
                   jax.ShapeDtypeStruct((T, dc), F32),
                   jax.ShapeDtypeStruct((T, D), BF16),
                   jax.ShapeDtypeStruct((T, D), BF16),
                   jax.ShapeDtypeStruct((T, LANES), F32),
                   jax.ShapeDtypeStruct((T // seq, SUBLANES, seq), F32)],
        grid=(T // tm,),
        in_specs=[pl.BlockSpec((tm, D), row),
                  pl.BlockSpec((1, 6, D), lambda i: (i // per_b, 0, 0)),
                  pl.BlockSpec((1, D), const),
                  pl.BlockSpec(w_main.shape, const),
                  pl.BlockSpec(w_gates.shape, const),
                  pl.BlockSpec(w_if.shape, const),
                  pl.BlockSpec(w_ift.shape, const)],
        out_specs=[pl.BlockSpec((tm, dc), row),
                   pl.BlockSpec((tm, 2 * dc), row),
                   pl.BlockSpec((tm, dc), row),
                   pl.BlockSpec((tm, dc), row),
                   pl.BlockSpec((tm, D), row),
                   pl.BlockSpec((tm, D), row),
                   pl.BlockSpec((tm, LANES), row),
                   pl.BlockSpec((1, SUBLANES, tm), lambda i: (i // per_b, 0, i % per_b))],
        compiler_params=_params("arbitrary"),
        name="inproj",
    )(x2, mod3, g1, w_main, w_gates, w_if, w_ift)


def _conv_kernel(u_ref, w_ref, b_ref, lg_ref, lb_ref, wo_ref, y_ref, ubuf, sbuf, cbuf):
    ts = u_ref.shape[1]
    halo = CONV_HALO

    @pl.when(pl.program_id(1) == 0)
    def _():
        ubuf[0:halo, :] = jnp.zeros((halo, ubuf.shape[1]), F32)

    ubuf[halo:halo + ts, :] = u_ref[0]
    ns = sbuf.shape[1]
    for r in range(1, SUBLANES):
        sbuf[r - 1] = ubuf[r:r + ns, :]
    off = halo - (CONV_WIDTH - 1)
    for r0 in range(0, ts, CONV_RC):
        acc = jnp.broadcast_to(b_ref[...], (CONV_RC, ubuf.shape[1]))
        for k in range(CONV_WIDTH):
            r = (off + k) % SUBLANES
            lo = off + k - r + r0
            win = ubuf[lo:lo + CONV_RC, :] if r == 0 else sbuf[r - 1, lo:lo + CONV_RC, :]
            acc = acc + w_ref[k:k + 1, :] * win
        cbuf[r0:r0 + CONV_RC, :] = acc
    ubuf[0:halo, :] = ubuf[ts:ts + halo, :]

    a = cbuf[...]
    mu = jnp.mean(a, axis=-1, keepdims=True)
    ac = a - mu
    var = jnp.mean(ac * ac, axis=-1, keepdims=True)
    z = ac * lax.rsqrt(var + LN_EPS) * lg_ref[...] + lb_ref[...]
    z = z * _sigmoid(z)
    y_ref[0] = jnp.dot(z.astype(BF16), wo_ref[...], preferred_element_type=F32).astype(BF16)


def _conv_branch(u3, w, b, lg, lb, wo):
    B, S, C = u3.shape
    D = wo.shape[1]
    ts = CONV_TS
    const = lambda bi, si: (0, 0)
    return pl.pallas_call(
        _conv_kernel,
        out_shape=jax.ShapeDtypeStruct((B, S, D), BF16),
        grid=(B, S // ts),
        in_specs=[pl.BlockSpec((1, ts, C), lambda bi, si: (bi, si, 0)),
                  pl.BlockSpec(w.shape, const),
                  pl.BlockSpec((1, C), const),
                  pl.BlockSpec((1, C), const),
                  pl.BlockSpec((1, C), const),
                  pl.BlockSpec(wo.shape, const)],
        out_specs=pl.BlockSpec((1, ts, D), lambda bi, si: (bi, si, 0)),
        scratch_shapes=[pltpu.VMEM((ts + CONV_HALO, C), F32),
                        pltpu.VMEM((SUBLANES - 1, ts + CONV_HALO - SUBLANES, C), F32),
                        pltpu.VMEM((ts, C), F32)],
        compiler_params=_params("arbitrary", "arbitrary"),
        name="conv",
    )(u3, w, b, lg, lb, wo)


def _mlstm_kernel(*refs, n_riders):
    (qk_ref, v_ref, o_ref, ifc_ref, ifr_ref, cw_ref, cb_ref, bifc_ref, bifr_ref, ng_ref, wo_ref) = refs[:11]
    rider_in = refs[11:11 + n_riders]
    y_ref = refs[11 + n_riders]
    rider_out = refs[12 + n_riders:12 + 2 * n_riders]
    qkbuf, cn_ref, m_ref, hbuf = refs[12 + 2 * n_riders:]

    for src, dst in zip(rider_in, rider_out):
        dst[...] = src[...].astype(BF16)

    @pl.when(pl.program_id(1) == 0)
    def _():
        qkbuf[:, 0:SUBLANES, :] = jnp.zeros((qkbuf.shape[0], SUBLANES, qkbuf.shape[2]), F32)
        cn_ref[...] = jnp.zeros(cn_ref.shape, F32)
        m_ref[...] = jnp.zeros(m_ref.shape, F32)

    nb, L, mi = hbuf.shape
    dh = mi // M_HEADS
    halo = SUBLANES
    off = halo - (QK_CONV_WIDTH - 1)
    rows = lax.broadcasted_iota(I32, (L, L), 0)
    cols = lax.broadcasted_iota(I32, (L, L), 1)
    causal = cols <= rows
    lower = causal.astype(F32)
    upper = (rows <= cols).astype(F32)
    lane = lax.broadcasted_iota(I32, (L, dh), 1)
    ones_col = jnp.where(lane == 0, 1.0, 0.0).astype(BF16)
    scale = dh ** -0.5

    seqs = []
    for b in range(nb):
        qkbuf[b, halo:halo + L, :] = qk_ref[b]
        y = jnp.broadcast_to(cb_ref[...], (L, qkbuf.shape[2]))
        for k in range(QK_CONV_WIDTH):
            y = y + cw_ref[k:k + 1, :] * qkbuf[b, off + k:off + k + L, :]
        y = y * _sigmoid(y)
        qkbuf[b, 0:halo, :] = qkbuf[b, L:L + halo, :]
        ifr = ifr_ref[b] + bifr_ref[...]
        ifc = ifc_ref[b] + bifc_ref[...]
        bcum_c = jnp.dot(lower, _log_sigmoid(ifc), preferred_element_type=F32,
                         precision=lax.Precision.HIGHEST)
        bcum_r = jnp.dot(_log_sigmoid(ifr), upper, preferred_element_type=F32,
                         precision=lax.Precision.HIGHEST)
        seqs.append((y, ifr, bcum_c, bcum_r))

    probs = [(b, hd) for b in range(nb) for hd in range(M_HEADS)]
    st = {}
    for p in probs:
        b, hd = p
        y, ifr, bcum_c, bcum_r = seqs[b]
        c0 = hd * dh
        qb = (y[:, c0:c0 + dh] * scale).astype(BF16)
        kt = y[:, mi + c0:mi + c0 + dh].T
        v = v_ref[b, :, c0:c0 + dh]
        bc = bcum_c[:, M_HEADS + hd:M_HEADS + hd + 1]
        br = bcum_r[M_HEADS + hd:M_HEADS + hd + 1, :]
        li = ifr[hd:hd + 1, :]
        m_prev = m_ref[b, hd, 0:1, 0:1]
        dmat = jnp.where(causal, bc - br + li, -jnp.inf)
        st[p] = dict(qb=qb, kt=kt, v=v, bc=bc, br=br, li=li, m_prev=m_prev, dmat=dmat)
    for p in probs:
        s = st[p]
        s["inter"] = s["bc"] + s["m_prev"]
        s["m_t"] = jnp.maximum(jnp.max(s["dmat"], axis=-1, keepdims=True), s["inter"])
    for p in probs:
        s = st[p]
        s["qk"] = jnp.dot(s["qb"], s["kt"].astype(BF16), preferred_element_type=F32)
    for p in probs:
        b, hd = p
        s = st[p]
        s["cn"] = cn_ref[b, hd]
        s["qcn"] = jnp.dot(s["qb"], s["cn"].astype(BF16), preferred_element_type=F32)
    for p in probs:
        s = st[p]
        s["wts"] = jnp.exp(s["dmat"] - s["m_t"])
        s["s_inter"] = jnp.exp(s["inter"] - s["m_t"])
    for p in probs:
        s = st[p]
        s["s_mat"] = s["qk"] * s["wts"]
    for p in probs:
        s = st[p]
        s["sv"] = jnp.dot(s["s_mat"].astype(BF16), s["v"], preferred_element_type=F32)
    for p in probs:
        s = st[p]
        s["rowsum"] = jnp.sum(s["s_mat"], axis=-1, keepdims=True)
    for p in probs:
        s = st[p]
        s["num"] = s["sv"] + s["s_inter"] * s["qcn"][:, 0:dh]
        s["den"] = s["rowsum"] + s["s_inter"] * s["qcn"][:, dh:dh + 1]
    for p in probs:
        b, hd = p
        s = st[p]
        b_last = s["br"][:, L - 1:L]
        a = b_last - s["br"] + s["li"]
        m_new = jnp.maximum(b_last + s["m_prev"], jnp.max(a, axis=-1, keepdims=True))
        wk = jnp.exp(a - m_new)
        sc = jnp.exp(b_last + s["m_prev"] - m_new)
        v_ext = jnp.concatenate([s["v"], ones_col], axis=1)
        cn_ref[b, hd] = sc * s["cn"] + jnp.dot((s["kt"] * wk).astype(BF16), v_ext, preferred_element_type=F32)
        m_ref[b, hd] = jnp.broadcast_to(m_new, m_ref.shape[2:])
    for p in probs:
        s = st[p]
        s["hh"] = s["num"] / jnp.maximum(jnp.abs(s["den"]), jnp.exp(-s["m_t"]))
        s["mu"] = jnp.mean(s["hh"], axis=-1, keepdims=True)
    for p in probs:
        s = st[p]
        s["hc"] = s["hh"] - s["mu"]
        s["var"] = jnp.mean(s["hc"] * s["hc"], axis=-1, keepdims=True)
    for p in probs:
        b, hd = p
        s = st[p]
        c0 = hd * dh
        hn = s["hc"] * lax.rsqrt(s["var"] + LN_EPS) * ng_ref[:, c0:c0 + dh]
        hbuf[b, :, c0:c0 + dh] = hn * _sigmoid(o_ref[b, :, c0:c0 + dh])
    for b in range(nb):
        y = jnp.dot(hbuf[b].astype(BF16), wo_ref[...], preferred_element_type=F32)
        y_ref[b] = y.astype(BF16)


def _mlstm_branch(qk3, v3, o3, ifc3, ifr3, cw, cb, bifc, bifr, ng, wo, riders):
    B, S, C2 = qk3.shape
    mi = v3.shape[2]
    dh = mi // M_HEADS
    D = wo.shape[1]
    L = MLSTM_L
    nb = MLSTM_SEQS
    ns = S // L
    n_steps = (B // nb) * ns
    const = lambda bi, ci: (0, 0)
    tile = lambda bi, ci: (bi, ci, 0)
    slab = lambda bi, ci: (bi * ns + ci, 0, 0)
    slabs = [r.reshape(n_steps, -1, r.shape[-1]) for r in riders]
    outs = pl.pallas_call(
        functools.partial(_mlstm_kernel, n_riders=len(riders)),
        out_shape=[jax.ShapeDtypeStruct((B, S, D), BF16)]
        + [jax.ShapeDtypeStruct(s.shape, BF16) for s in slabs],
        grid=(B // nb, ns),
        in_specs=[pl.BlockSpec((nb, L, C2), tile),
                  pl.BlockSpec((nb, L, mi), tile),
                  pl.BlockSpec((nb, L, mi), tile),
                  pl.BlockSpec((nb, L, LANES), tile),
                  pl.BlockSpec((nb, SUBLANES, L), lambda bi, ci: (bi, 0, ci)),
                  pl.BlockSpec(cw.shape, const),
                  pl.BlockSpec((1, C2), const),
                  pl.BlockSpec((1, LANES), const),
                  pl.BlockSpec((SUBLANES, 1), const),
                  pl.BlockSpec((1, mi), const),
                  pl.BlockSpec(wo.shape, const)]
        + [pl.BlockSpec((1,) + s.shape[1:], slab) for s in slabs],
        out_specs=[pl.BlockSpec((nb, L, D), tile)]
        + [pl.BlockSpec((1,) + s.shape[1:], slab) for s in slabs],
        scratch_shapes=[pltpu.VMEM((nb, L + SUBLANES, C2), F32),
                        pltpu.VMEM((nb, M_HEADS, dh, 2 * dh), F32),
                        pltpu.VMEM((nb, M_HEADS, SUBLANES, LANES), F32),
                        pltpu.VMEM((nb, L, mi), F32)],
        compiler_params=_params("arbitrary", "arbitrary"),
        name="mlstm",
    )(qk3, v3, o3, ifc3, ifr3, cw, cb, bifc, bifr, ng, wo, *slabs)
    return outs[0], [o.reshape(r.shape) for o, r in zip(outs[1:], riders)]


def _merge_kernel(x_ref, ya_ref, yb_ref, sga_ref, sgb_ref, mod_ref, g2_ref, wo_ref, wr_ref, br_ref,
                  x1_ref, h2_ref, ri_ref, rf_ref, cnt_ref, run_ref):
    ts = MERGE_SUB
    subs = [pl.ds(r0, ts) for r0 in range(0, x_ref.shape[0], ts)]

    @pl.when(pl.program_id(0) == 0)
    def _():
        run_ref[...] = jnp.zeros(run_ref.shape, F32)

    gate1 = mod_ref[0, 2:3, :]
    shift2 = mod_ref[0, 3:4, :]
    scale2 = mod_ref[0, 4:5, :]
    lane = lax.broadcasted_iota(I32, (ts, LANES), 1).astype(F32)
    neg = -jnp.inf
    rows = lax.broadcasted_iota(I32, (ts, ts), 0)
    cols = lax.broadcasted_iota(I32, (ts, ts), 1)
    strict = jnp.where(cols < rows, 1.0, 0.0).astype(BF16)

    def first_argmax(vals):
        mx = jnp.max(vals, axis=-1, keepdims=True)
        idx = jnp.min(jnp.where(vals == mx, lane, float(LANES)), axis=-1, keepdims=True)
        return mx, idx

    h2s = []
    for sl in subs:
        merged = (sga_ref[sl, :].astype(F32) * ya_ref[sl, :].astype(F32)
                  + sgb_ref[sl, :].astype(F32) * yb_ref[sl, :].astype(F32))
        mix = jnp.dot(merged.astype(BF16), wo_ref[...], preferred_element_type=F32)
        x1 = x_ref[sl, :] + gate1 * mix
        x1_ref[sl, :] = x1
        ms = jnp.mean(x1 * x1, axis=-1, keepdims=True)
        h2 = x1 * lax.rsqrt(ms + RMS_EPS) * g2_ref[...]
        h2 = h2 * (1.0 + scale2) + shift2
        h2_ref[sl, :] = _pack_bf16_pairs(h2)
        h2s.append(h2.astype(BF16))

    run = run_ref[0:1, :]
    for sl, h2b in zip(subs, h2s):
        logits = jnp.dot(h2b, wr_ref[...], preferred_element_type=F32) + br_ref[...]
        lg = jnp.where(lane < N_GROUPS, logits, neg)
        gmax, gsel = first_argmax(lg)
        p_g = 1.0 / jnp.sum(jnp.exp(lg - gmax), axis=-1, keepdims=True)
        lo = N_GROUPS + gsel * E_PER_GROUP
        le = jnp.where((lane >= lo) & (lane < lo + E_PER_GROUP), logits, neg)
        l1, i1 = first_argmax(le)
        l2, i2 = first_argmax(jnp.where(lane == i1, neg, le))
        r = jnp.exp(l2 - l1)
        w1 = p_g / (1.0 + r)
        w2 = p_g * r / (1.0 + r)
        e1 = i1 - N_GROUPS
        e2 = i2 - N_GROUPS

        onehot = jnp.where((lane == e1) | (lane == e2), 1.0, 0.0)
        before = jnp.dot(strict, onehot.astype(BF16), preferred_element_type=F32) + run
        rank1 = jnp.sum(jnp.where(lane == e1, before, 0.0), axis=-1, keepdims=True)
        rank2 = jnp.sum(jnp.where(lane == e2, before, 0.0), axis=-1, keepdims=True)
        run = run + jnp.sum(onehot, axis=0, keepdims=True)

        codes = jnp.where(lane == 0, e1 * float(RANK_RADIX) + rank1,
                          jnp.where(lane == 1, e2 * float(RANK_RADIX) + rank2, 0.0))
        ri_ref[:, sl] = codes.T[0:SUBLANES, :].astype(I32)
        rf_ref[sl, :] = jnp.where(lane == 0, w1, jnp.where(lane == 1, w2, 0.0))
    run_ref[...] = jnp.broadcast_to(run, run_ref.shape)
    cnt_ref[...] = jnp.broadcast_to(run, cnt_ref.shape).astype(I32)


def _merge(x2, ya, yb, sga, sgb, mod3, g2, wo, wr, br, seq):
    T, D = x2.shape
    tm = MERGE_TM
    per_b = seq // tm
    row = lambda i: (i, 0)
    const = lambda i: (0, 0)
    return pl.pallas_call(
        _merge_kernel,
        out_shape=[jax.ShapeDtypeStruct((T, D), F32),
                   jax.ShapeDtypeStruct((T, D // 2), I32),
                   jax.ShapeDtypeStruct((SUBLANES, T), I32),
                   jax.ShapeDtypeStruct((T, LANES), F32),
                   jax.ShapeDtypeStruct((SUBLANES, LANES), I32)],
        grid=(T // tm,),
        in_specs=[pl.BlockSpec((tm, D), row),
                  pl.BlockSpec((tm, D), row),
                  pl.BlockSpec((tm, D), row),
                  pl.BlockSpec((tm, D), row),
                  pl.BlockSpec((tm, D), row),
                  pl.BlockSpec((1, 6, D), lambda i: (i // per_b, 0, 0)),
                  pl.BlockSpec((1, D), const),
                  pl.BlockSpec(wo.shape, const),
                  pl.BlockSpec(wr.shape, const),
                  pl.BlockSpec((1, LANES), const)],
        out_specs=[pl.BlockSpec((tm, D), row),
                   pl.BlockSpec((tm, D // 2), row),
                   pl.BlockSpec((SUBLANES, tm), lambda i: (0, i)),
                   pl.BlockSpec((tm, LANES), row),
                   pl.BlockSpec((SUBLANES, LANES), const)],
        scratch_shapes=[pltpu.VMEM((SUBLANES, LANES), F32)],
        compiler_params=_params("arbitrary"),
        name="merge",
    )(x2, ya, yb, sga, sgb, mod3, g2, wo, wr, br)


def _sc_workers():
    info = plsc.get_sparse_core_info()
    mesh = plsc.VectorSubcoreMesh(core_axis_name="core", subcore_axis_name="subcore")
    params = pltpu.CompilerParams()
    if "needs_layout_passes" in pltpu.CompilerParams.__dataclass_fields__:
        params = dataclasses.replace(params, needs_layout_passes=False)
    return info, mesh, params


def _rows_from_codes(code_v, base_v, idx_v, lanes):
    for j in range(code_v.shape[0] // lanes):
        c = code_v[pl.ds(j * lanes, lanes)]
        expert = lax.shift_right_logical(c, RANK_BITS)
        idx_v[pl.ds(j * lanes, lanes)] = plsc.load_gather(base_v, [expert]) + (c & (RANK_RADIX - 1))


def _two_slot_loop(n_chunks, start, finish):
    start(0, 0)

    @pl.loop(0, n_chunks, step=2)
    def _(c):
        start(c + 1, 1)
        finish(c, 0)

        @pl.when(c + 2 < n_chunks)
        def _():
            start(c + 2, 0)

        finish(c + 1, 1)


def _sc_dispatch(h2, code0, code1, base, n_rows):
    T, D = h2.shape
    info, mesh, params = _sc_workers()
    n_workers = info.num_cores * info.num_subcores
    w = SC_WINDOW_BYTES // (D * h2.dtype.itemsize)
    per_w = T // n_workers
    n_chunks = per_w // w
    assert per_w * n_workers == T and n_chunks * w == per_w and n_chunks % 2 == 0

    @functools.partial(
        pl.kernel, out_type=jax.ShapeDtypeStruct((n_rows, D), h2.dtype), mesh=mesh, compiler_params=params,
        scratch_types=[pltpu.VMEM((N_EXPERTS,), I32), pltpu.VMEM((w,), I32), pltpu.VMEM((w,), I32),
                       pltpu.VMEM((w,), I32), pltpu.VMEM((w, D), h2.dtype), pltpu.VMEM((w, D), h2.dtype),
                       pltpu.SemaphoreType.DMA, pltpu.SemaphoreType.DMA])
    def scatter(h_hbm, c0_hbm, c1_hbm, b_hbm, xs_hbm, base_v, code_v, i0_v, i1_v, rows0, rows1, sem0, sem1):
        wid = lax.axis_index("subcore") * info.num_cores + lax.axis_index("core")
        w0 = wid * per_w
        pltpu.sync_copy(b_hbm, base_v)
        rows = (rows0, rows1)
        sems = (sem0, sem1)

        def start(c, slot):
            pltpu.async_copy(h_hbm.at[pl.ds(w0 + c * w, w)], rows[slot], sems[slot])

        def finish(c, slot):
            pltpu.sync_copy(c0_hbm.at[pl.ds(w0 + c * w, w)], code_v)
            _rows_from_codes(code_v, base_v, i0_v, info.num_lanes)
            pltpu.sync_copy(c1_hbm.at[pl.ds(w0 + c * w, w)], code_v)
            _rows_from_codes(code_v, base_v, i1_v, info.num_lanes)
            pltpu.make_async_copy(h_hbm.at[pl.ds(w0 + c * w, w)], rows[slot], sems[slot]).wait()
            pltpu.sync_copy(rows[slot], xs_hbm.at[i0_v])
            pltpu.sync_copy(rows[slot], xs_hbm.at[i1_v])

        _two_slot_loop(n_chunks, start, finish)

    return scatter(h2, code0, code1, base)


def _sc_collect(ys, codes, base):
    n = codes.shape[0]
    D = ys.shape[1]
    info, mesh, params = _sc_workers()
    n_workers = info.num_cores * info.num_subcores
    w = SC_WINDOW_BYTES // (D * ys.dtype.itemsize)
    per_w = n // n_workers
    n_chunks = per_w // w
    assert per_w * n_workers == n and n_chunks * w == per_w and n_chunks % 2 == 0

    @functools.partial(
        pl.kernel, out_type=jax.ShapeDtypeStruct((n, D), ys.dtype), mesh=mesh, compiler_params=params,
        scratch_types=[pltpu.VMEM((N_EXPERTS,), I32), pltpu.VMEM((w,), I32), pltpu.VMEM((w,), I32),
                       pltpu.VMEM((w,), I32), pltpu.VMEM((w, D), ys.dtype), pltpu.VMEM((w, D), ys.dtype),
                       pltpu.SemaphoreType.DMA, pltpu.SemaphoreType.DMA])
    def gather(ys_hbm, c_hbm, b_hbm, yk_hbm, base_v, code_v, i0_v, i1_v, rows0, rows1, sem0, sem1):
        wid = lax.axis_index("subcore") * info.num_cores + lax.axis_index("core")
        w0 = wid * per_w
        pltpu.sync_copy(b_hbm, base_v)
        idx = (i0_v, i1_v)
        rows = (rows0, rows1)
        sems = (sem0, sem1)

        def start(c, slot):
            pltpu.sync_copy(c_hbm.at[pl.ds(w0 + c * w, w)], code_v)
            _rows_from_codes(code_v, base_v, idx[slot], info.num_lanes)
            pltpu.async_copy(ys_hbm.at[idx[slot]], rows[slot], sems[slot])

        def finish(c, slot):
            pltpu.make_async_copy(ys_hbm.at[idx[slot]], rows[slot], sems[slot]).wait()
            pltpu.sync_copy(rows[slot], yk_hbm.at[pl.ds(w0 + c * w, w)])

        _two_slot_loop(n_chunks, start, finish)

    return gather(ys, codes, base)


def _schedule_kernel(cnt_ref, te_ref, tb_ref, base_ref, nt_ref):
    tm = EXPERT_TM

    def expert(e, t0):
        n = (cnt_ref[e] + tm - 1) // tm
        base_ref[e] = t0 * tm

        def tile(t, c):
            te_ref[t] = e
            tb_ref[t] = t
            return c

        lax.fori_loop(t0, t0 + n, tile, 0)
        return t0 + n

    nt = lax.fori_loop(0, N_EXPERTS, expert, 0)
    nt_ref[0] = nt
    last = te_ref[nt - 1]

    def idle(t, c):
        te_ref[t] = last
        tb_ref[t] = nt - 1
        return c

    lax.fori_loop(nt, te_ref.shape[0], idle, 0)


def _schedule(counts, max_tiles):
    smem = pl.BlockSpec(memory_space=pltpu.SMEM)
    return pl.pallas_call(
        _schedule_kernel,
        out_shape=[jax.ShapeDtypeStruct((max_tiles,), I32),
                   jax.ShapeDtypeStruct((max_tiles,), I32),
                   jax.ShapeDtypeStruct((N_EXPERTS,), I32),
                   jax.ShapeDtypeStruct((1,), I32)],
        in_specs=[smem],
        out_specs=[smem, smem, smem, smem],
        name="schedule",
    )(counts)


def _expert_kernel(te_ref, tb_ref, nt_ref, xs_ref, wg_ref, wu_ref, wd_ref, ys_ref):
    @pl.when(pl.program_id(0) < nt_ref[0])
    def _():
        xb = _unpack_bf16_pairs(xs_ref[...])
        g = jnp.dot(xb, wg_ref[0], preferred_element_type=F32)
        u = jnp.dot(xb, wu_ref[0], preferred_element_type=F32)
        act = (g * _sigmoid(g)) * u
        ys_ref[...] = jnp.dot(act.astype(BF16), wd_ref[0], preferred_element_type=F32)


def _experts(tile_e, tile_b, n_tiles, xs, wg, wu, wd, max_tiles):
    P = xs.shape[0]
    D, de = wg.shape[1:]
    tm = EXPERT_TM
    wmap = lambda j, te, tb, nt: (te[j], 0, 0)
    rmap = lambda j, te, tb, nt: (tb[j], 0)
    return pl.pallas_call(
        _expert_kernel,
        out_shape=jax.ShapeDtypeStruct((P, D), F32),
        grid_spec=pltpu.PrefetchScalarGridSpec(
            num_scalar_prefetch=3,
            grid=(max_tiles,),
            in_specs=[pl.BlockSpec((tm, xs.shape[1]), rmap),
                      pl.BlockSpec((1, D, de), wmap),
                      pl.BlockSpec((1, D, de), wmap),
                      pl.BlockSpec((1, de, D), wmap)],
            out_specs=pl.BlockSpec((tm, D), rmap)),
        compiler_params=_params("arbitrary"),
        name="experts",
    )(tile_e, tile_b, n_tiles, xs, wg, wu, wd)


def _combine_kernel(x1_ref, rf_ref, mod_ref, gf_ref, y0_ref, y1_ref, out_ref, *, final_norm):
    gate2 = mod_ref[0, 5:6, :]
    w = rf_ref[...]
    moe = w[:, 0:1] * y0_ref[...] + w[:, 1:2] * y1_ref[...]
    x2 = x1_ref[...] + gate2 * moe
    if final_norm:
        ms = jnp.mean(x2 * x2, axis=-1, keepdims=True)
        x2 = x2 * lax.rsqrt(ms + RMS_EPS) * gf_ref[...]
    out_ref[...] = x2


def _combine(x1, rf, mod3, gf, yk, seq, final_norm):
    T, D = x1.shape
    tc = COMBINE_TM
    per_b = seq // tc
    n_blk = T // tc
    return pl.pallas_call(
        functools.partial(_combine_kernel, final_norm=final_norm),
        out_shape=jax.ShapeDtypeStruct((T, D), F32),
        grid=(n_blk,),
        in_specs=[pl.BlockSpec((tc, D), lambda i: (i, 0)),
                  pl.BlockSpec((tc, LANES), lambda i: (i, 0)),
                  pl.BlockSpec((1, 6, D), lambda i: (i // per_b, 0, 0)),
                  pl.BlockSpec((1, D), lambda i: (0, 0)),
                  pl.BlockSpec((tc, D), lambda i: (i, 0)),
                  pl.BlockSpec((tc, D), lambda i: (n_blk + i, 0))],
        out_specs=pl.BlockSpec((tc, D), lambda i: (i, 0)),
        compiler_params=_params("arbitrary"),
        name="combine",
    )(x1, rf, mod3, gf, yk, yk)


def _layer(x2, c, seq, layer, w_ada, b_ada, g_norm1, w_in, b_if, conv_dw_w, conv_dw_b, conv_ln_g, conv_ln_b,
           w_conv_out, qk_conv_w, qk_conv_b, m_norm_g, w_m_out, w_out, g_norm2, w_rg, b_rg,
           w_re, b_re, w_e_gate, w_e_up, w_e_down):
    T, D = x2.shape
    B = T // seq
    dc = D // 2
    nif = 2 * M_HEADS

    if_lo = 6 * dc
    full = lambda w: (w, [(0, w.shape[2])])
    mod, (w_main, w_gates, w_out_b, w_conv_out_b, w_m_out_b) = _ada(
        c, w_ada, b_ada, layer,
        [(w_in, [(0, if_lo), (if_lo + nif, w_in.shape[2])]), full(w_out), full(w_conv_out), full(w_m_out)])
    mod3 = mod.reshape(B, 6, D)
    w_if = w_in[layer, :, if_lo:if_lo + nif]
    w_if_pad = jnp.pad(w_if, ((0, 0), (0, LANES - nif))).astype(BF16)
    w_ift = w_if.T.astype(BF16)
    u, qk, v, o, sga, sgb, ifc, ifr = _inproj(x2, mod3, g_norm1.reshape(1, D), w_main, w_gates,
                                              w_if_pad, w_ift, seq)

    ya = _conv_branch(u.reshape(B, seq, dc), conv_dw_w, conv_dw_b.reshape(1, dc),
                      conv_ln_g.reshape(1, dc), conv_ln_b.reshape(1, dc), w_conv_out_b)
    bifc = jnp.pad(b_if, (0, LANES - nif)).reshape(1, LANES)
    bifr = b_if.reshape(nif, 1)
    yb, (wg_b, wu_b, wd_b) = _mlstm_branch(
        qk.reshape(B, seq, 2 * dc), v.reshape(B, seq, dc), o.reshape(B, seq, dc),
        ifc.reshape(B, seq, LANES), ifr, qk_conv_w, qk_conv_b.reshape(1, 2 * dc), bifc, bifr,
        m_norm_g.reshape(1, dc), w_m_out_b, riders=(w_e_gate, w_e_up, w_e_down))

    n_r = N_GROUPS + N_EXPERTS
    w_r = jnp.pad(jnp.concatenate([w_rg, w_re], axis=1), ((0, 0), (0, LANES - n_r))).astype(BF16)
    b_r = jnp.pad(jnp.concatenate([b_rg, b_re]), (0, LANES - n_r)).reshape(1, LANES)
    x1, h2, ri, rf, cnt = _merge(x2, ya.reshape(T, D), yb.reshape(T, D), sga, sgb, mod3,
                                 g_norm2.reshape(1, D), w_out_b, w_r, b_r, seq)

    tm = EXPERT_TM
    max_tiles = (T * TOP_K) // tm + N_EXPERTS
    tile_e, tile_b, base, n_tiles = _schedule(cnt[0, :N_EXPERTS], max_tiles)

    code0 = ri[0]
    code1 = ri[1]
    xs = _sc_dispatch(h2, code0, code1, base, max_tiles * tm)
    ys = _experts(tile_e, tile_b, n_tiles, xs, wg_b, wu_b, wd_b, max_tiles)
    return x1, rf, mod3, ys, code0, code1, base


def kernel(x, c, w_ada, b_ada, g_norm1, w_in, b_if, conv_dw_w, conv_dw_b, conv_ln_g, conv_ln_b,
           w_conv_out, qk_conv_w, qk_conv_b, m_norm_g, w_m_out, w_out, g_norm2, w_rg, b_rg,
           w_re, b_re, w_e_gate, w_e_up, w_e_down, g_final):
    B, S, D = x.shape
    depth = w_ada.shape[0]
    x2 = x.reshape(B * S, D)
    for l in range(depth):
        x1, rf, mod3, ys, code0, code1, base = _layer(
            x2, c, S, l, w_ada[l], b_ada[l], g_norm1[l], w_in, b_if[l], conv_dw_w[l], conv_dw_b[l],
            conv_ln_g[l], conv_ln_b[l], w_conv_out, qk_conv_w[l], qk_conv_b[l], m_norm_g[l],
            w_m_out, w_out, g_norm2[l], w_rg[l], b_rg[l], w_re[l], b_re[l],
            w_e_gate[l], w_e_up[l], w_e_down[l])
        yk = _sc_collect(ys, jnp.concatenate([code0, code1]), base)
        x2 = _combine(x1, rf, mod3, g_final.reshape(1, D), yk, S, final_norm=l == depth - 1)
    return x2.reshape(B, S, D)
```

```python
import dataclasses
import functools

import jax
import jax.numpy as jnp
from jax import lax
from jax.experimental import pallas as pl
from jax.experimental.pallas import tpu as pltpu
from jax.experimental.pallas import tpu_sc as plsc

F32 = jnp.float32
BF16 = jnp.bfloat16
I32 = jnp.int32

M_HEADS = 4
CONV_WIDTH = 31
QK_CONV_WIDTH = 4
N_GROUPS = 4
E_PER_GROUP = 8
N_EXPERTS = N_GROUPS * E_PER_GROUP
TOP_K = 2
RMS_EPS = 1e-6
LN_EPS = 1e-5

LANES = 128
SUBLANES = 8
VMEM_LIMIT = 56 * 1024 * 1024

ADA_TN = 768
INPROJ_TM = 512
INPROJ_SUB = 256
CONV_TS = 512
CONV_HALO = 32
CONV_RC = 256
MLSTM_L = 128
MLSTM_SEQS = 4
MERGE_TM = 512
MERGE_SUB = 256
EXPERT_TM = 512
SC_WINDOW_BYTES = 128 * 1024
COMBINE_TM = 256
RANK_BITS = 16
RANK_RADIX = 1 << RANK_BITS
assert EXPERT_TM & (EXPERT_TM - 1) == 0


def _sigmoid(v):
    return 1.0 / (1.0 + jnp.exp(-v))


def _log_sigmoid(v):
    return -(jnp.maximum(-v, 0.0) + jnp.log1p(jnp.exp(-jnp.abs(v))))


def _pack_bf16_pairs(v):
    n = v.shape[1] // 2
    bits = lax.bitcast_convert_type(v.astype(BF16).astype(F32), jnp.uint32)
    word = bits[:, n:] | (bits[:, :n] >> 16)
    return lax.bitcast_convert_type(word, I32)


def _unpack_bf16_pairs(w):
    bits = lax.bitcast_convert_type(w, jnp.uint32)
    lo = lax.bitcast_convert_type(bits << 16, F32)
    hi = lax.bitcast_convert_type(bits & jnp.uint32(0xFFFF0000), F32)
    return jnp.concatenate([lo, hi], axis=1).astype(BF16)


def _params(*sem):
    return pltpu.CompilerParams(dimension_semantics=sem, vmem_limit_bytes=VMEM_LIMIT)


def _ada_kernel(*refs, cuts):
    c_ref, w_ref, b_ref = refs[:3]
    srcs = refs[3:3 + len(cuts)]
    o_ref = refs[3 + len(cuts)]
    dsts = iter(refs[4 + len(cuts):])
    c = c_ref[...]
    s = c * _sigmoid(c)
    o_ref[...] = jnp.dot(s, w_ref[...], preferred_element_type=F32,
                         precision=lax.Precision.HIGHEST) + b_ref[...]
    for src, ranges in zip(srcs, cuts):
        for lo, hi in ranges:
            next(dsts)[...] = src[:, lo:hi].astype(BF16)


def _ada(c, w_ada, b_ada, layer, riders):
    B, D = c.shape
    N = w_ada.shape[1]
    n_steps = N // ADA_TN
    slab = lambda j: (j, 0)
    outs = pl.pallas_call(
        functools.partial(_ada_kernel, cuts=tuple(tuple(r) for _, r in riders)),
        out_shape=[jax.ShapeDtypeStruct((B, N), F32)]
        + [jax.ShapeDtypeStruct((w.shape[1], hi - lo), BF16) for w, r in riders for lo, hi in r],
        grid=(n_steps,),
        in_specs=[pl.BlockSpec((B, D), lambda j: (0, 0)),
                  pl.BlockSpec((D, ADA_TN), lambda j: (0, j)),
                  pl.BlockSpec((1, ADA_TN), lambda j: (0, j))]
        + [pl.BlockSpec((None, w.shape[1] // n_steps, w.shape[2]), lambda j: (layer, j, 0))
           for w, _ in riders],
        out_specs=[pl.BlockSpec((B, ADA_TN), lambda j: (0, j))]
        + [pl.BlockSpec((w.shape[1] // n_steps, hi - lo), slab) for w, r in riders for lo, hi in r],
        compiler_params=_params("arbitrary"),
        name="ada",
    )(c, w_ada, b_ada.reshape(1, N), *[w for w, _ in riders])
    return outs[0], outs[1:]


def _inproj_kernel(x_ref, mod_ref, g_ref, wm_ref, wgt_ref, wif_ref, wift_ref,
                   u_ref, qk_ref, v_ref, o_ref, sga_ref, sgb_ref, ifc_ref, ifr_ref):
    shift = mod_ref[0, 0:1, :]
    scale = mod_ref[0, 1:2, :]
    dc = u_ref.shape[1]
    d = sga_ref.shape[1]
    ts = INPROJ_SUB
    subs = [pl.ds(r0, ts) for r0 in range(0, x_ref.shape[0], ts)]

    hbs = []
    for sl in subs:
        x = x_ref[sl, :]
        ms = jnp.mean(x * x, axis=-1, keepdims=True)
        h = x * lax.rsqrt(ms + RMS_EPS) * g_ref[...]
        h = h * (1.0 + scale) + shift
        hbs.append(h.astype(BF16))

    for sl, hb in zip(subs, hbs):
        def seg(lo, hi):
            return jnp.dot(hb, wm_ref[:, lo:hi], preferred_element_type=F32)

        u_ref[sl, :] = seg(0, dc) * _sigmoid(seg(dc, 2 * dc))
        qk_ref[sl, :] = seg(2 * dc, 4 * dc)
        v_ref[sl, :] = seg(4 * dc, 5 * dc).astype(BF16)
        o_ref[sl, :] = seg(5 * dc, 6 * dc)
        sga_ref[sl, :] = _sigmoid(jnp.dot(hb, wgt_ref[:, 0:d], preferred_element_type=F32)).astype(BF16)
        sgb_ref[sl, :] = _sigmoid(jnp.dot(hb, wgt_ref[:, d:2 * d], preferred_element_type=F32)).astype(BF16)
        ifc_ref[sl, :] = jnp.dot(hb, wif_ref[...], preferred_element_type=F32)
        ifr_ref[0, :, sl] = lax.dot_general(wift_ref[...], hb, (((1,), (1,)), ((), ())),
                                            preferred_element_type=F32)


def _inproj(x2, mod3, g1, w_main, w_gates, w_if, w_ift, seq):
    T, D = x2.shape
    tm = INPROJ_TM
    dc = D // 2
    per_b = seq // tm
    row = lambda i: (i, 0)
    const = lambda i: (0, 0)
    return pl.pallas_call(
        _inproj_kernel,
        out_shape=[jax.ShapeDtypeStruct((T, dc), F32),
                   jax.ShapeDtypeStruct((T, 2 * dc), F32),
                   jax.ShapeDtypeStruct((T, dc), BF16),
                   jax.ShapeDtypeStruct((T, dc), F32),
                   jax.ShapeDtypeStruct((T, D), BF16),
                   jax.ShapeDtypeStruct((T, D), BF16),
                   jax.ShapeDtypeStruct((T, LANES), F32),
                   jax.ShapeDtypeStruct((T // seq, SUBLANES, seq), F32)],
        grid=(T // tm,),
        in_specs=[pl.BlockSpec((tm, D), row),
                  pl.BlockSpec((1, 6, D), lambda i: (i // per_b, 0, 0)),
                  pl.BlockSpec((1, D), const),
                  pl.BlockSpec(w_main.shape, const),
                  pl.BlockSpec(w_gates.shape, const),
                  pl.BlockSpec(w_if.shape, const),
                  pl.BlockSpec(w_ift.shape, const)],
        out_specs=[pl.BlockSpec((tm, dc), row),
                   pl.BlockSpec((tm, 2 * dc), row),
                   pl.BlockSpec((tm, dc), row),
                   pl.BlockSpec((tm, dc), row),
                   pl.BlockSpec((tm, D), row),
                   pl.BlockSpec((tm, D), row),
                   pl.BlockSpec((tm, LANES), row),
                   pl.BlockSpec((1, SUBLANES, tm), lambda i: (i // per_b, 0, i % per_b))],
        compiler_params=_params("arbitrary"),
        name="inproj",
    )(x2, mod3, g1, w_main, w_gates, w_if, w_ift)


def _conv_kernel(u_ref, w_ref, b_ref, lg_ref, lb_ref, wo_ref, y_ref, ubuf, sbuf, cbuf):
    ts = u_ref.shape[1]
    halo = CONV_HALO

    @pl.when(pl.program_id(1) == 0)
    def _():
        ubuf[0:halo, :] = jnp.zeros((halo, ubuf.shape[1]), F32)

    ubuf[halo:halo + ts, :] = u_ref[0]
    ns = sbuf.shape[1]
    for r in range(1, SUBLANES):
        sbuf[r - 1] = ubuf[r:r + ns, :]
    off = halo - (CONV_WIDTH - 1)
    for r0 in range(0, ts, CONV_RC):
        acc = jnp.broadcast_to(b_ref[...], (CONV_RC, ubuf.shape[1]))
        for k in range(CONV_WIDTH):
            r = (off + k) % SUBLANES
            lo = off + k - r + r0
            win = ubuf[lo:lo + CONV_RC, :] if r == 0 else sbuf[r - 1, lo:lo + CONV_RC, :]
            acc = acc + w_ref[k:k + 1, :] * win
        cbuf[r0:r0 + CONV_RC, :] = acc
    ubuf[0:halo, :] = ubuf[ts:ts + halo, :]

    a = cbuf[...]
    mu = jnp.mean(a, axis=-1, keepdims=True)
    ac = a - mu
    var = jnp.mean(ac * ac, axis=-1, keepdims=True)
    z = ac * lax.rsqrt(var + LN_EPS) * lg_ref[...] + lb_ref[...]
    z = z * _sigmoid(z)
    y_ref[0] = jnp.dot(z.astype(BF16), wo_ref[...], preferred_element_type=F32).astype(BF16)


def _conv_branch(u3, w, b, lg, lb, wo):
    B, S, C = u3.shape
    D = wo.shape[1]
    ts = CONV_TS
    const = lambda bi, si: (0, 0)
    return pl.pallas_call(
        _conv_kernel,
        out_shape=jax.ShapeDtypeStruct((B, S, D), BF16),
        grid=(B, S // ts),
        in_specs=[pl.BlockSpec((1, ts, C), lambda bi, si: (bi, si, 0)),
                  pl.BlockSpec(w.shape, const),
                  pl.BlockSpec((1, C), const),
                  pl.BlockSpec((1, C), const),
                  pl.BlockSpec((1, C), const),
                  pl.BlockSpec(wo.shape, const)],
        out_specs=pl.BlockSpec((1, ts, D), lambda bi, si: (bi, si, 0)),
        scratch_shapes=[pltpu.VMEM((ts + CONV_HALO, C), F32),
                        pltpu.VMEM((SUBLANES - 1, ts + CONV_HALO - SUBLANES, C), F32),
                        pltpu.VMEM((ts, C), F32)],
        compiler_params=_params("arbitrary", "arbitrary"),
        name="conv",
    )(u3, w, b, lg, lb, wo)


def _mlstm_kernel(*refs, n_riders):
    (qk_ref, v_ref, o_ref, ifc_ref, ifr_ref, cw_ref, cb_ref, bifc_ref, bifr_ref, ng_ref, wo_ref) = refs[:11]
    rider_in = refs[11:11 + n_riders]
    y_ref = refs[11 + n_riders]
    rider_out = refs[12 + n_riders:12 + 2 * n_riders]
    qkbuf, cn_ref, m_ref, hbuf = refs[12 + 2 * n_riders:]

    for src, dst in zip(rider_in, rider_out):
        dst[...] = src[...].astype(BF16)

    @pl.when(pl.program_id(1) == 0)
    def _():
        qkbuf[:, 0:SUBLANES, :] = jnp.zeros((qkbuf.shape[0], SUBLANES, qkbuf.shape[2]), F32)
        cn_ref[...] = jnp.zeros(cn_ref.shape, F32)
        m_ref[...] = jnp.zeros(m_ref.shape, F32)

    nb, L, mi = hbuf.shape
    dh = mi // M_HEADS
    halo = SUBLANES
    off = halo - (QK_CONV_WIDTH - 1)
    rows = lax.broadcasted_iota(I32, (L, L), 0)
    cols = lax.broadcasted_iota(I32, (L, L), 1)
    causal = cols <= rows
    lower = causal.astype(F32)
    upper = (rows <= cols).astype(F32)
    lane = lax.broadcasted_iota(I32, (L, dh), 1)
    ones_col = jnp.where(lane == 0, 1.0, 0.0).astype(BF16)
    scale = dh ** -0.5

    seqs = []
    for b in range(nb):
        qkbuf[b, halo:halo + L, :] = qk_ref[b]
        y = jnp.broadcast_to(cb_ref[...], (L, qkbuf.shape[2]))
        for k in range(QK_CONV_WIDTH):
            y = y + cw_ref[k:k + 1, :] * qkbuf[b, off + k:off + k + L, :]
        y = y * _sigmoid(y)
        qkbuf[b, 0:halo, :] = qkbuf[b, L:L + halo, :]
        ifr = ifr_ref[b] + bifr_ref[...]
        ifc = ifc_ref[b] + bifc_ref[...]
        bcum_c = jnp.dot(lower, _log_sigmoid(ifc), preferred_element_type=F32,
                         precision=lax.Precision.HIGHEST)
        bcum_r = jnp.dot(_log_sigmoid(ifr), upper, preferred_element_type=F32,
                         precision=lax.Precision.HIGHEST)
        seqs.append((y, ifr, bcum_c, bcum_r))

    probs = [(b, hd) for b in range(nb) for hd in range(M_HEADS)]
    st = {}
    for p in probs:
        b, hd = p
        y, ifr, bcum_c, bcum_r = seqs[b]
        c0 = hd * dh
        qb = (y[:, c0:c0 + dh] * scale).astype(BF16)
        kt = y[:, mi + c0:mi + c0 + dh].T
        v = v_ref[b, :, c0:c0 + dh]
        bc = bcum_c[:, M_HEADS + hd:M_HEADS + hd + 1]
        br = bcum_r[M_HEADS + hd:M_HEADS + hd + 1, :]
        li = ifr[hd:hd + 1, :]
        m_prev = m_ref[b, hd, 0:1, 0:1]
        dmat = jnp.where(causal, bc - br + li, -jnp.inf)
        st[p] = dict(qb=qb, kt=kt, v=v, bc=bc, br=br, li=li, m_prev=m_prev, dmat=dmat)
    for p in probs:
        s = st[p]
        s["inter"] = s["bc"] + s["m_prev"]
        s["m_t"] = jnp.maximum(jnp.max(s["dmat"], axis=-1, keepdims=True), s["inter"])
    for p in probs:
        s = st[p]
        s["qk"] = jnp.dot(s["qb"], s["kt"].astype(BF16), preferred_element_type=F32)
    for p in probs:
        b, hd = p
        s = st[p]
        s["cn"] = cn_ref[b, hd]
        s["qcn"] = jnp.dot(s["qb"], s["cn"].astype(BF16), preferred_element_type=F32)
    for p in probs:
        s = st[p]
        s["wts"] = jnp.exp(s["dmat"] - s["m_t"])
        s["s_inter"] = jnp.exp(s["inter"] - s["m_t"])
    for p in probs:
        s = st[p]
        s["s_mat"] = s["qk"] * s["wts"]
    for p in probs:
        s = st[p]
        s["sv"] = jnp.dot(s["s_mat"].astype(BF16), s["v"], preferred_element_type=F32)
    for p in probs:
        s = st[p]
        s["rowsum"] = jnp.sum(s["s_mat"], axis=-1, keepdims=True)
    for p in probs:
        s = st[p]
        s["num"] = s["sv"] + s["s_inter"] * s["qcn"][:, 0:dh]
        s["den"] = s["rowsum"] + s["s_inter"] * s["qcn"][:, dh:dh + 1]
    for p in probs:
        b, hd = p
        s = st[p]
        b_last = s["br"][:, L - 1:L]
        a = b_last - s["br"] + s["li"]
        m_new = jnp.maximum(b_last + s["m_prev"], jnp.max(a, axis=-1, keepdims=True))
        wk = jnp.exp(a - m_new)
        sc = jnp.exp(b_last + s["m_prev"] - m_new)
        v_ext = jnp.concatenate([s["v"], ones_col], axis=1)
        cn_ref[b, hd] = sc * s["cn"] + jnp.dot((s["kt"] * wk).astype(BF16), v_ext, preferred_element_type=F32)
        m_ref[b, hd] = jnp.broadcast_to(m_new, m_ref.shape[2:])
    for p in probs:
        s = st[p]
        s["hh"] = s["num"] / jnp.maximum(jnp.abs(s["den"]), jnp.exp(-s["m_t"]))
        s["mu"] = jnp.mean(s["hh"], axis=-1, keepdims=True)
    for p in probs:
        s = st[p]
        s["hc"] = s["hh"] - s["mu"]
        s["var"] = jnp.mean(s["hc"] * s["hc"], axis=-1, keepdims=True)
    for p in probs:
        b, hd = p
        s = st[p]
        c0 = hd * dh
        hn = s["hc"] * lax.rsqrt(s["var"] + LN_EPS) * ng_ref[:, c0:c0 + dh]
        hbuf[b, :, c0:c0 + dh] = hn * _sigmoid(o_ref[b, :, c0:c0 + dh])
    for b in range(nb):
        y = jnp.dot(hbuf[b].astype(BF16), wo_ref[...], preferred_element_type=F32)
        y_ref[b] = y.astype(BF16)


def _mlstm_branch(qk3, v3, o3, ifc3, ifr3, cw, cb, bifc, bifr, ng, wo, riders):
    B, S, C2 = qk3.shape
    mi = v3.shape[2]
    dh = mi // M_HEADS
    D = wo.shape[1]
    L = MLSTM_L
    nb = MLSTM_SEQS
    ns = S // L
    n_steps = (B // nb) * ns
    const = lambda bi, ci: (0, 0)
    tile = lambda bi, ci: (bi, ci, 0)
    slab = lambda bi, ci: (bi * ns + ci, 0, 0)
    slabs = [r.reshape(n_steps, -1, r.shape[-1]) for r in riders]
    outs = pl.pallas_call(
        functools.partial(_mlstm_kernel, n_riders=len(riders)),
        out_shape=[jax.ShapeDtypeStruct((B, S, D), BF16)]
        + [jax.ShapeDtypeStruct(s.shape, BF16) for s in slabs],
        grid=(B // nb, ns),
        in_specs=[pl.BlockSpec((nb, L, C2), tile),
                  pl.BlockSpec((nb, L, mi), tile),
                  pl.BlockSpec((nb, L, mi), tile),
                  pl.BlockSpec((nb, L, LANES), tile),
                  pl.BlockSpec((nb, SUBLANES, L), lambda bi, ci: (bi, 0, ci)),
                  pl.BlockSpec(cw.shape, const),
                  pl.BlockSpec((1, C2), const),
                  pl.BlockSpec((1, LANES), const),
                  pl.BlockSpec((SUBLANES, 1), const),
                  pl.BlockSpec((1, mi), const),
                  pl.BlockSpec(wo.shape, const)]
        + [pl.BlockSpec((1,) + s.shape[1:], slab) for s in slabs],
        out_specs=[pl.BlockSpec((nb, L, D), tile)]
        + [pl.BlockSpec((1,) + s.shape[1:], slab) for s in slabs],
        scratch_shapes=[pltpu.VMEM((nb, L + SUBLANES, C2), F32),
                        pltpu.VMEM((nb, M_HEADS, dh, 2 * dh), F32),
                        pltpu.VMEM((nb, M_HEADS, SUBLANES, LANES), F32),
                        pltpu.VMEM((nb, L, mi), F32)],
        compiler_params=_params("arbitrary", "arbitrary"),
        name="mlstm",
    )(qk3, v3, o3, ifc3, ifr3, cw, cb, bifc, bifr, ng, wo, *slabs)
    return outs[0], [o.reshape(r.shape) for o, r in zip(outs[1:], riders)]


def _merge_kernel(x_ref, ya_ref, yb_ref, sga_ref, sgb_ref, mod_ref, g2_ref, wo_ref, wr_ref, br_ref,
                  x1_ref, h2_ref, ri_ref, rf_ref, cnt_ref, run_ref):
    ts = MERGE_SUB
    subs = [pl.ds(r0, ts) for r0 in range(0, x_ref.shape[0], ts)]

    @pl.when(pl.program_id(0) == 0)
    def _():
        run_ref[...] = jnp.zeros(run_ref.shape, F32)

    gate1 = mod_ref[0, 2:3, :]
    shift2 = mod_ref[0, 3:4, :]
    scale2 = mod_ref[0, 4:5, :]
    lane = lax.broadcasted_iota(I32, (ts, LANES), 1).astype(F32)
    neg = -jnp.inf
    rows = lax.broadcasted_iota(I32, (ts, ts), 0)
    cols = lax.broadcasted_iota(I32, (ts, ts), 1)
    strict = jnp.where(cols < rows, 1.0, 0.0).astype(BF16)

    def first_argmax(vals):
        mx = jnp.max(vals, axis=-1, keepdims=True)
        idx = jnp.min(jnp.where(vals == mx, lane, float(LANES)), axis=-1, keepdims=True)
        return mx, idx

    h2s = []
    for sl in subs:
        merged = (sga_ref[sl, :].astype(F32) * ya_ref[sl, :].astype(F32)
                  + sgb_ref[sl, :].astype(F32) * yb_ref[sl, :].astype(F32))
        mix = jnp.dot(merged.astype(BF16), wo_ref[...], preferred_element_type=F32)
        x1 = x_ref[sl, :] + gate1 * mix
        x1_ref[sl, :] = x1
        ms = jnp.mean(x1 * x1, axis=-1, keepdims=True)
        h2 = x1 * lax.rsqrt(ms + RMS_EPS) * g2_ref[...]
        h2 = h2 * (1.0 + scale2) + shift2
        h2_ref[sl, :] = _pack_bf16_pairs(h2)
        h2s.append(h2.astype(BF16))

    run = run_ref[0:1, :]
    for sl, h2b in zip(subs, h2s):
        logits = jnp.dot(h2b, wr_ref[...], preferred_element_type=F32) + br_ref[...]
        lg = jnp.where(lane < N_GROUPS, logits, neg)
        gmax, gsel = first_argmax(lg)
        p_g = 1.0 / jnp.sum(jnp.exp(lg - gmax), axis=-1, keepdims=True)
        lo = N_GROUPS + gsel * E_PER_GROUP
        le = jnp.where((lane >= lo) & (lane < lo + E_PER_GROUP), logits, neg)
        l1, i1 = first_argmax(le)
        l2, i2 = first_argmax(jnp.where(lane == i1, neg, le))
        r = jnp.exp(l2 - l1)
        w1 = p_g / (1.0 + r)
        w2 = p_g * r / (1.0 + r)
        e1 = i1 - N_GROUPS
        e2 = i2 - N_GROUPS

        onehot = jnp.where((lane == e1) | (lane == e2), 1.0, 0.0)
        before = jnp.dot(strict, onehot.astype(BF16), preferred_element_type=F32) + run
        rank1 = jnp.sum(jnp.where(lane == e1, before, 0.0), axis=-1, keepdims=True)
        rank2 = jnp.sum(jnp.where(lane == e2, before, 0.0), axis=-1, keepdims=True)
        run = run + jnp.sum(onehot, axis=0, keepdims=True)

        codes = jnp.where(lane == 0, e1 * float(RANK_RADIX) + rank1,
                          jnp.where(lane == 1, e2 * float(RANK_RADIX) + rank2, 0.0))
        ri_ref[:, sl] = codes.T[0:SUBLANES, :].astype(I32)
        rf_ref[sl, :] = jnp.where(lane == 0, w1, jnp.where(lane == 1, w2, 0.0))
    run_ref[...] = jnp.broadcast_to(run, run_ref.shape)
    cnt_ref[...] = jnp.broadcast_to(run, cnt_ref.shape).astype(I32)


def _merge(x2, ya, yb, sga, sgb, mod3, g2, wo, wr, br, seq):
    T, D = x2.shape
    tm = MERGE_TM
    per_b = seq // tm
    row = lambda i: (i, 0)
    const = lambda i: (0, 0)
    return pl.pallas_call(
        _merge_kernel,
        out_shape=[jax.ShapeDtypeStruct((T, D), F32),
                   jax.ShapeDtypeStruct((T, D // 2), I32),
                   jax.ShapeDtypeStruct((SUBLANES, T), I32),
                   jax.ShapeDtypeStruct((T, LANES), F32),
                   jax.ShapeDtypeStruct((SUBLANES, LANES), I32)],
        grid=(T // tm,),
        in_specs=[pl.BlockSpec((tm, D), row),
                  pl.BlockSpec((tm, D), row),
                  pl.BlockSpec((tm, D), row),
                  pl.BlockSpec((tm, D), row),
                  pl.BlockSpec((tm, D), row),
                  pl.BlockSpec((1, 6, D), lambda i: (i // per_b, 0, 0)),
                  pl.BlockSpec((1, D), const),
                  pl.BlockSpec(wo.shape, const),
                  pl.BlockSpec(wr.shape, const),
                  pl.BlockSpec((1, LANES), const)],
        out_specs=[pl.BlockSpec((tm, D), row),
                   pl.BlockSpec((tm, D // 2), row),
                   pl.BlockSpec((SUBLANES, tm), lambda i: (0, i)),
                   pl.BlockSpec((tm, LANES), row),
                   pl.BlockSpec((SUBLANES, LANES), const)],
        scratch_shapes=[pltpu.VMEM((SUBLANES, LANES), F32)],
        compiler_params=_params("arbitrary"),
        name="merge",
    )(x2, ya, yb, sga, sgb, mod3, g2, wo, wr, br)


def _sc_workers():
    info = plsc.get_sparse_core_info()
    mesh = plsc.VectorSubcoreMesh(core_axis_name="core", subcore_axis_name="subcore")
    params = pltpu.CompilerParams()
    if "needs_layout_passes" in pltpu.CompilerParams.__dataclass_fields__:
        params = dataclasses.replace(params, needs_layout_passes=False)
    return info, mesh, params


def _rows_from_codes(code_v, base_v, idx_v, lanes):
    for j in range(code_v.shape[0] // lanes):
        c = code_v[pl.ds(j * lanes, lanes)]
        expert = lax.shift_right_logical(c, RANK_BITS)
        idx_v[pl.ds(j * lanes, lanes)] = plsc.load_gather(base_v, [expert]) + (c & (RANK_RADIX - 1))


def _two_slot_loop(n_chunks, start, finish):
    start(0, 0)

    @pl.loop(0, n_chunks, step=2)
    def _(c):
        start(c + 1, 1)
        finish(c, 0)

        @pl.when(c + 2 < n_chunks)
        def _():
            start(c + 2, 0)

        finish(c + 1, 1)


def _sc_dispatch(h2, code0, code1, base, n_rows):
    T, D = h2.shape
    info, mesh, params = _sc_workers()
    n_workers = info.num_cores * info.num_subcores
    w = SC_WINDOW_BYTES // (D * h2.dtype.itemsize)
    per_w = T // n_workers
    n_chunks = per_w // w
    assert per_w * n_workers == T and n_chunks * w == per_w and n_chunks % 2 == 0

    @functools.partial(
        pl.kernel, out_type=jax.ShapeDtypeStruct((n_rows, D), h2.dtype), mesh=mesh, compiler_params=params,
        scratch_types=[pltpu.VMEM((N_EXPERTS,), I32), pltpu.VMEM((w,), I32), pltpu.VMEM((w,), I32),
                       pltpu.VMEM((w,), I32), pltpu.VMEM((w, D), h2.dtype), pltpu.VMEM((w, D), h2.dtype),
                       pltpu.SemaphoreType.DMA, pltpu.SemaphoreType.DMA])
    def scatter(h_hbm, c0_hbm, c1_hbm, b_hbm, xs_hbm, base_v, code_v, i0_v, i1_v, rows0, rows1, sem0, sem1):
        wid = lax.axis_index("subcore") * info.num_cores + lax.axis_index("core")
        w0 = wid * per_w
        pltpu.sync_copy(b_hbm, base_v)
        rows = (rows0, rows1)
        sems = (sem0, sem1)

        def start(c, slot):
            pltpu.async_copy(h_hbm.at[pl.ds(w0 + c * w, w)], rows[slot], sems[slot])

        def finish(c, slot):
            pltpu.sync_copy(c0_hbm.at[pl.ds(w0 + c * w, w)], code_v)
            _rows_from_codes(code_v, base_v, i0_v, info.num_lanes)
            pltpu.sync_copy(c1_hbm.at[pl.ds(w0 + c * w, w)], code_v)
            _rows_from_codes(code_v, base_v, i1_v, info.num_lanes)
            pltpu.make_async_copy(h_hbm.at[pl.ds(w0 + c * w, w)], rows[slot], sems[slot]).wait()
            pltpu.sync_copy(rows[slot], xs_hbm.at[i0_v])
            pltpu.sync_copy(rows[slot], xs_hbm.at[i1_v])

        _two_slot_loop(n_chunks, start, finish)

    return scatter(h2, code0, code1, base)


def _sc_collect(ys, codes, base):
    n = codes.shape[0]
    D = ys.shape[1]
    info, mesh, params = _sc_workers()
    n_workers = info.num_cores * info.num_subcores
    w = SC_WINDOW_BYTES // (D * ys.dtype.itemsize)
    per_w = n // n_workers
    n_chunks = per_w // w
    assert per_w * n_workers == n and n_chunks * w == per_w and n_chunks % 2 == 0

    @functools.partial(
        pl.kernel, out_type=jax.ShapeDtypeStruct((n, D), ys.dtype), mesh=mesh, compiler_params=params,
        scratch_types=[pltpu.VMEM((N_EXPERTS,), I32), pltpu.VMEM((w,), I32), pltpu.VMEM((w,), I32),
                       pltpu.VMEM((w,), I32), pltpu.VMEM((w, D), ys.dtype), pltpu.VMEM((w, D), ys.dtype),
                       pltpu.SemaphoreType.DMA, pltpu.SemaphoreType.DMA])
    def gather(ys_hbm, c_hbm, b_hbm, yk_hbm, base_v, code_v, i0_v, i1_v, rows0, rows1, sem0, sem1):
        wid = lax.axis_index("subcore") * info.num_cores + lax.axis_index("core")
        w0 = wid * per_w
        pltpu.sync_copy(b_hbm, base_v)
        idx = (i0_v, i1_v)
        rows = (rows0, rows1)
        sems = (sem0, sem1)

        def start(c, slot):
            pltpu.sync_copy(c_hbm.at[pl.ds(w0 + c * w, w)], code_v)
            _rows_from_codes(code_v, base_v, idx[slot], info.num_lanes)
            pltpu.async_copy(ys_hbm.at[idx[slot]], rows[slot], sems[slot])

        def finish(c, slot):
            pltpu.make_async_copy(ys_hbm.at[idx[slot]], rows[slot], sems[slot]).wait()
            pltpu.sync_copy(rows[slot], yk_hbm.at[pl.ds(w0 + c * w, w)])

        _two_slot_loop(n_chunks, start, finish)

    return gather(ys, codes, base)


def _schedule_kernel(cnt_ref, te_ref, tb_ref, base_ref, nt_ref):
    tm = EXPERT_TM

    def expert(e, t0):
        n = (cnt_ref[e] + tm - 1) // tm
        base_ref[e] = t0 * tm

        def tile(t, c):
            te_ref[t] = e
            tb_ref[t] = t
            return c

        lax.fori_loop(t0, t0 + n, tile, 0)
        return t0 + n

    nt = lax.fori_loop(0, N_EXPERTS, expert, 0)
    nt_ref[0] = nt
    last = te_ref[nt - 1]

    def idle(t, c):
        te_ref[t] = last
        tb_ref[t] = nt - 1
        return c

    lax.fori_loop(nt, te_ref.shape[0], idle, 0)


def _schedule(counts, max_tiles):
    smem = pl.BlockSpec(memory_space=pltpu.SMEM)
    return pl.pallas_call(
        _schedule_kernel,
        out_shape=[jax.ShapeDtypeStruct((max_tiles,), I32),
                   jax.ShapeDtypeStruct((max_tiles,), I32),
                   jax.ShapeDtypeStruct((N_EXPERTS,), I32),
                   jax.ShapeDtypeStruct((1,), I32)],
        in_specs=[smem],
        out_specs=[smem, smem, smem, smem],
        name="schedule",
    )(counts)


def _expert_kernel(te_ref, tb_ref, nt_ref, xs_ref, wg_ref, wu_ref, wd_ref, ys_ref):
    @pl.when(pl.program_id(0) < nt_ref[0])
    def _():
        xb = _unpack_bf16_pairs(xs_ref[...])
        g = jnp.dot(xb, wg_ref[0], preferred_element_type=F32)
        u = jnp.dot(xb, wu_ref[0], preferred_element_type=F32)
        act = (g * _sigmoid(g)) * u
        ys_ref[...] = jnp.dot(act.astype(BF16), wd_ref[0], preferred_element_type=F32)


def _experts(tile_e, tile_b, n_tiles, xs, wg, wu, wd, max_tiles):
    P = xs.shape[0]
    D, de = wg.shape[1:]
    tm = EXPERT_TM
    wmap = lambda j, te, tb, nt: (te[j], 0, 0)
    rmap = lambda j, te, tb, nt: (tb[j], 0)
    return pl.pallas_call(
        _expert_kernel,
        out_shape=jax.ShapeDtypeStruct((P, D), F32),
        grid_spec=pltpu.PrefetchScalarGridSpec(
            num_scalar_prefetch=3,
            grid=(max_tiles,),
            in_specs=[pl.BlockSpec((tm, xs.shape[1]), rmap),
                      pl.BlockSpec((1, D, de), wmap),
                      pl.BlockSpec((1, D, de), wmap),
                      pl.BlockSpec((1, de, D), wmap)],
            out_specs=pl.BlockSpec((tm, D), rmap)),
        compiler_params=_params("arbitrary"),
        name="experts",
    )(tile_e, tile_b, n_tiles, xs, wg, wu, wd)


def _combine_kernel(x1_ref, rf_ref, mod_ref, gf_ref, y0_ref, y1_ref, out_ref, *, final_norm):
    gate2 = mod_ref[0, 5:6, :]
    w = rf_ref[...]
    moe = w[:, 0:1] * y0_ref[...] + w[:, 1:2] * y1_ref[...]
    x2 = x1_ref[...] + gate2 * moe
    if final_norm:
        ms = jnp.mean(x2 * x2, axis=-1, keepdims=True)
        x2 = x2 * lax.rsqrt(ms + RMS_EPS) * gf_ref[...]
    out_ref[...] = x2


def _combine(x1, rf, mod3, gf, yk, seq, final_norm):
    T, D = x1.shape
    tc = COMBINE_TM
    per_b = seq // tc
    n_blk = T // tc
    return pl.pallas_call(
        functools.partial(_combine_kernel, final_norm=final_norm),
        out_shape=jax.ShapeDtypeStruct((T, D), F32),
        grid=(n_blk,),
        in_specs=[pl.BlockSpec((tc, D), lambda i: (i, 0)),
                  pl.BlockSpec((tc, LANES), lambda i: (i, 0)),
                  pl.BlockSpec((1, 6, D), lambda i: (i // per_b, 0, 0)),
                  pl.BlockSpec((1, D), lambda i: (0, 0)),
                  pl.BlockSpec((tc, D), lambda i: (i, 0)),
                  pl.BlockSpec((tc, D), lambda i: (n_blk + i, 0))],
        out_specs=pl.BlockSpec((tc, D), lambda i: (i, 0)),
        compiler_params=_params("arbitrary"),
        name="combine",
    )(x1, rf, mod3, gf, yk, yk)


def _layer(x2, c, seq, layer, w_ada, b_ada, g_norm1, w_in, b_if, conv_dw_w, conv_dw_b, conv_ln_g, conv_ln_b,
           w_conv_out, qk_conv_w, qk_conv_b, m_norm_g, w_m_out, w_out, g_norm2, w_rg, b_rg,
           w_re, b_re, w_e_gate, w_e_up, w_e_down):
    T, D = x2.shape
    B = T // seq
    dc = D // 2
    nif = 2 * M_HEADS

    if_lo = 6 * dc
    full = lambda w: (w, [(0, w.shape[2])])
    mod, (w_main, w_gates, w_out_b, w_conv_out_b, w_m_out_b) = _ada(
        c, w_ada, b_ada, layer,
        [(w_in, [(0, if_lo), (if_lo + nif, w_in.shape[2])]), full(w_out), full(w_conv_out), full(w_m_out)])
    mod3 = mod.reshape(B, 6, D)
    w_if = w_in[layer, :, if_lo:if_lo + nif]
    w_if_pad = jnp.pad(w_if, ((0, 0), (0, LANES - nif))).astype(BF16)
    w_ift = w_if.T.astype(BF16)
    u, qk, v, o, sga, sgb, ifc, ifr = _inproj(x2, mod3, g_norm1.reshape(1, D), w_main, w_gates,
                                              w_if_pad, w_ift, seq)

    ya = _conv_branch(u.reshape(B, seq, dc), conv_dw_w, conv_dw_b.reshape(1, dc),
                      conv_ln_g.reshape(1, dc), conv_ln_b.reshape(1, dc), w_conv_out_b)
    bifc = jnp.pad(b_if, (0, LANES - nif)).reshape(1, LANES)
    bifr = b_if.reshape(nif, 1)
    yb, (wg_b, wu_b, wd_b) = _mlstm_branch(
        qk.reshape(B, seq, 2 * dc), v.reshape(B, seq, dc), o.reshape(B, seq, dc),
        ifc.reshape(B, seq, LANES), ifr, qk_conv_w, qk_conv_b.reshape(1, 2 * dc), bifc, bifr,
        m_norm_g.reshape(1, dc), w_m_out_b, riders=(w_e_gate, w_e_up, w_e_down))

    n_r = N_GROUPS + N_EXPERTS
    w_r = jnp.pad(jnp.concatenate([w_rg, w_re], axis=1), ((0, 0), (0, LANES - n_r))).astype(BF16)
    b_r = jnp.pad(jnp.concatenate([b_rg, b_re]), (0, LANES - n_r)).reshape(1, LANES)
    x1, h2, ri, rf, cnt = _merge(x2, ya.reshape(T, D), yb.reshape(T, D), sga, sgb, mod3,
                                 g_norm2.reshape(1, D), w_out_b, w_r, b_r, seq)

    tm = EXPERT_TM
    max_tiles = (T * TOP_K) // tm + N_EXPERTS
    tile_e, tile_b, base, n_tiles = _schedule(cnt[0, :N_EXPERTS], max_tiles)

    code0 = ri[0]
    code1 = ri[1]
    xs = _sc_dispatch(h2, code0, code1, base, max_tiles * tm)
    ys = _experts(tile_e, tile_b, n_tiles, xs, wg_b, wu_b, wd_b, max_tiles)
    return x1, rf, mod3, ys, code0, code1, base


def kernel(x, c, w_ada, b_ada, g_norm1, w_in, b_if, conv_dw_w, conv_dw_b, conv_ln_g, conv_ln_b,
           w_conv_out, qk_conv_w, qk_conv_b, m_norm_g, w_m_out, w_out, g_norm2, w_rg, b_rg,
           w_re, b_re, w_e_gate, w_e_up, w_e_down, g_final):
    B, S, D = x.shape
    depth = w_ada.shape[0]
    x2 = x.reshape(B * S, D)
    for l in range(depth):
        x1, rf, mod3, ys, code0, code1, base = _layer(
            x2, c, S, l, w_ada[l], b_ada[l], g_norm1[l], w_in, b_if[l], conv_dw_w[l], conv_dw_b[l],
            conv_ln_g[l], conv_ln_b[l], w_conv_out, qk_conv_w[l], qk_conv_b[l], m_norm_g[l],
            w_m_out, w_out, g_norm2[l], w_rg[l], b_rg[l], w_re[l], b_re[l],
            w_e_gate[l], w_e_up[l], w_e_down[l])
        yk = _sc_collect(ys, jnp.concatenate([code0, code1]), base)
        x2 = _combine(x1, rf, mod3, g_final.reshape(1, D), yk, S, final_norm=l == depth - 1)
    return x2.reshape(B, S, D)
```

```python
import dataclasses
import functools

import jax
import jax.numpy as jnp
from jax import lax
from jax.experimental import pallas as pl
from jax.experimental.pallas import tpu as pltpu
from jax.experimental.pallas import tpu_sc as plsc

F32 = jnp.float32
BF16 = jnp.bfloat16
I32 = jnp.int32

M_HEADS = 4
CONV_WIDTH = 31
QK_CONV_WIDTH = 4
N_GROUPS = 4
E_PER_GROUP = 8
N_EXPERTS = N_GROUPS * E_PER_GROUP
TOP_K = 2
RMS_EPS = 1e-6
LN_EPS = 1e-5

LANES = 128
SUBLANES = 8
VMEM_LIMIT = 56 * 1024 * 1024

ADA_TN = 768
INPROJ_TM = 512
INPROJ_SUB = 256
CONV_TS = 512
CONV_HALO = 32
CONV_RC = 256
MLSTM_L = 128
MLSTM_SEQS = 8
MERGE_TM = 512
MERGE_SUB = 256
EXPERT_TM = 512
SC_WINDOW_BYTES = 128 * 1024
COMBINE_TM = 256
RANK_BITS = 16
RANK_RADIX = 1 << RANK_BITS
assert EXPERT_TM & (EXPERT_TM - 1) == 0


def _sigmoid(v):
    return 1.0 / (1.0 + jnp.exp(-v))


def _log_sigmoid(v):
    return -(jnp.maximum(-v, 0.0) + jnp.log1p(jnp.exp(-jnp.abs(v))))


def _pack_bf16_pairs(v):
    n = v.shape[1] // 2
    bits = lax.bitcast_convert_type(v.astype(BF16).astype(F32), jnp.uint32)
    word = bits[:, n:] | (bits[:, :n] >> 16)
    return lax.bitcast_convert_type(word, I32)


def _unpack_bf16_pairs(w):
    bits = lax.bitcast_convert_type(w, jnp.uint32)
    lo = lax.bitcast_convert_type(bits << 16, F32)
    hi = lax.bitcast_convert_type(bits & jnp.uint32(0xFFFF0000), F32)
    return jnp.concatenate([lo, hi], axis=1).astype(BF16)


def _split_bf16(v):
    hi = v.astype(BF16)
    r1 = v - hi.astype(F32)
    mid = r1.astype(BF16)
    lo = (r1 - mid.astype(F32)).astype(BF16)
    return hi, mid, lo


def _params(*sem):
    return pltpu.CompilerParams(dimension_semantics=sem, vmem_limit_bytes=VMEM_LIMIT)


def _ada_kernel(*refs, cuts):
    c_ref, w_ref, b_ref = refs[:3]
    srcs = refs[3:3 + len(cuts)]
    o_ref = refs[3 + len(cuts)]
    dsts = iter(refs[4 + len(cuts):])
    c = c_ref[...]
    s = c * _sigmoid(c)
    o_ref[...] = jnp.dot(s, w_ref[...], preferred_element_type=F32,
                         precision=lax.Precision.HIGHEST) + b_ref[...]
    for src, ranges in zip(srcs, cuts):
        for lo, hi in ranges:
            next(dsts)[...] = src[:, lo:hi].astype(BF16)


def _ada(c, w_ada, b_ada, layer, riders):
    B, D = c.shape
    N = w_ada.shape[1]
    n_steps = N // ADA_TN
    slab = lambda j: (j, 0)
    outs = pl.pallas_call(
        functools.partial(_ada_kernel, cuts=tuple(tuple(r) for _, r in riders)),
        out_shape=[jax.ShapeDtypeStruct((B, N), F32)]
        + [jax.ShapeDtypeStruct((w.shape[1], hi - lo), BF16) for w, r in riders for lo, hi in r],
        grid=(n_steps,),
        in_specs=[pl.BlockSpec((B, D), lambda j: (0, 0)),
                  pl.BlockSpec((D, ADA_TN), lambda j: (0, j)),
                  pl.BlockSpec((1, ADA_TN), lambda j: (0, j))]
        + [pl.BlockSpec((None, w.shape[1] // n_steps, w.shape[2]), lambda j: (layer, j, 0))
           for w, _ in riders],
        out_specs=[pl.BlockSpec((B, ADA_TN), lambda j: (0, j))]
        + [pl.BlockSpec((w.shape[1] // n_steps, hi - lo), slab) for w, r in riders for lo, hi in r],
        compiler_params=_params("arbitrary"),
        name="ada",
    )(c, w_ada, b_ada.reshape(1, N), *[w for w, _ in riders])
    return outs[0], outs[1:]


def _inproj_kernel(x_ref, mod_ref, g_ref, wm_ref, wgt_ref, wif_ref, wift_ref,
                   u_ref, qk_ref, v_ref, o_ref, sga_ref, sgb_ref, ifc_ref, ifr_ref):
    shift = mod_ref[0, 0:1, :]
    scale = mod_ref[0, 1:2, :]
    dc = u_ref.shape[1]
    d = sga_ref.shape[1]
    ts = INPROJ_SUB
    subs = [pl.ds(r0, ts) for r0 in range(0, x_ref.shape[0], ts)]

    hbs = []
    for sl in subs:
        x = x_ref[sl, :]
        ms = jnp.mean(x * x, axis=-1, keepdims=True)
        h = x * lax.rsqrt(ms + RMS_EPS) * g_ref[...]
        h = h * (1.0 + scale) + shift
        hbs.append(h.astype(BF16))

    for sl, hb in zip(subs, hbs):
        def seg(lo, hi):
            return jnp.dot(hb, wm_ref[:, lo:hi], preferred_element_type=F32)

        u_ref[sl, :] = seg(0, dc) * _sigmoid(seg(dc, 2 * dc))
        qk_ref[sl, :] = seg(2 * dc, 4 * dc)
        v_ref[sl, :] = seg(4 * dc, 5 * dc).astype(BF16)
        o_ref[sl, :] = seg(5 * dc, 6 * dc)
        sga_ref[sl, :] = _sigmoid(jnp.dot(hb, wgt_ref[:, 0:d], preferred_element_type=F32)).astype(BF16)
        sgb_ref[sl, :] = _sigmoid(jnp.dot(hb, wgt_ref[:, d:2 * d], preferred_element_type=F32)).astype(BF16)
        ifc_ref[sl, :] = jnp.dot(hb, wif_ref[...], preferred_element_type=F32)
        ifr_ref[0, :, sl] = lax.dot_general(wift_ref[...], hb, (((1,), (1,)), ((), ())),
                                            preferred_element_type=F32)


def _inproj(x2, mod3, g1, w_main, w_gates, w_if, w_ift, seq):
    T, D = x2.shape
    tm = INPROJ_TM
    dc = D // 2
    per_b = seq // tm
    row = lambda i: (i, 0)
    const = lambda i: (0, 0)
    return pl.pallas_call(
        _inproj_kernel,
        out_shape=[jax.ShapeDtypeStruct((T, dc), F32),
                   jax.ShapeDtypeStruct((T, 2 * dc), F32),
                   jax.ShapeDtypeStruct((T, dc), BF16),
                   jax.ShapeDtypeStruct((T, dc), F32),
                   jax.ShapeDtypeStruct((T, D), BF16),
                   jax.ShapeDtypeStruct((T, D), BF16),
                   jax.ShapeDtypeStruct((T, LANES), F32),
                   jax.ShapeDtypeStruct((T // seq, SUBLANES, seq), F32)],
        grid=(T // tm,),
        in_specs=[pl.BlockSpec((tm, D), row),
                  pl.BlockSpec((1, 6, D), lambda i: (i // per_b, 0, 0)),
                  pl.BlockSpec((1, D), const),
                  pl.BlockSpec(w_main.shape, const),
                  pl.BlockSpec(w_gates.shape, const),
                  pl.BlockSpec(w_if.shape, const),
                  pl.BlockSpec(w_ift.shape, const)],
        out_specs=[pl.BlockSpec((tm, dc), row),
                   pl.BlockSpec((tm, 2 * dc), row),
                   pl.BlockSpec((tm, dc), row),
                   pl.BlockSpec((tm, dc), row),
                   pl.BlockSpec((tm, D), row),
                   pl.BlockSpec((tm, D), row),
                   pl.BlockSpec((tm, LANES), row),
                   pl.BlockSpec((1, SUBLANES, tm), lambda i: (i // per_b, 0, i % per_b))],
        compiler_params=_params("arbitrary"),
        name="inproj",
    )(x2, mod3, g1, w_main, w_gates, w_if, w_ift)


def _conv_kernel(*refs, n_riders):
    u_ref, w_ref, b_ref, lg_ref, lb_ref, wo_ref = refs[:6]
    rider_in = refs[6:6 + n_riders]
    y_ref = refs[6 + n_riders]
    rider_out = refs[7 + n_riders:7 + 2 * n_riders]
    ubuf, sbuf, cbuf = refs[7 + 2 * n_riders:]
    ts = u_ref.shape[1]
    halo = CONV_HALO

    for src, dst in zip(rider_in, rider_out):
        dst[...] = src[...].astype(BF16)

    @pl.when(pl.program_id(1) == 0)
    def _():
        ubuf[0:halo, :] = jnp.zeros((halo, ubuf.shape[1]), F32)

    ubuf[halo:halo + ts, :] = u_ref[0]
    ns = sbuf.shape[1]
    for r in range(1, SUBLANES):
        sbuf[r - 1] = ubuf[r:r + ns, :]
    off = halo - (CONV_WIDTH - 1)
    for r0 in range(0, ts, CONV_RC):
        acc = jnp.broadcast_to(b_ref[...], (CONV_RC, ubuf.shape[1]))
        for k in range(CONV_WIDTH):
            r = (off + k) % SUBLANES
            lo = off + k - r + r0
            win = ubuf[lo:lo + CONV_RC, :] if r == 0 else sbuf[r - 1, lo:lo + CONV_RC, :]
            acc = acc + w_ref[k:k + 1, :] * win
        cbuf[r0:r0 + CONV_RC, :] = acc
    ubuf[0:halo, :] = ubuf[ts:ts + halo, :]

    a = cbuf[...]
    mu = jnp.mean(a, axis=-1, keepdims=True)
    ac = a - mu
    var = jnp.mean(ac * ac, axis=-1, keepdims=True)
    z = ac * lax.rsqrt(var + LN_EPS) * lg_ref[...] + lb_ref[...]
    z = z * _sigmoid(z)
    y_ref[0] = jnp.dot(z.astype(BF16), wo_ref[...], preferred_element_type=F32).astype(BF16)


def _conv_branch(u3, w, b, lg, lb, wo, riders):
    B, S, C = u3.shape
    D = wo.shape[1]
    ts = CONV_TS
    ns = S // ts
    const = lambda bi, si: (0, 0)
    slab = lambda bi, si: (bi * ns + si, 0, 0)
    slabs = [r.reshape(B * ns, -1, r.shape[-1]) for r in riders]
    outs = pl.pallas_call(
        functools.partial(_conv_kernel, n_riders=len(riders)),
        out_shape=[jax.ShapeDtypeStruct((B, S, D), BF16)]
        + [jax.ShapeDtypeStruct(s.shape, BF16) for s in slabs],
        grid=(B, ns),
        in_specs=[pl.BlockSpec((1, ts, C), lambda bi, si: (bi, si, 0)),
                  pl.BlockSpec(w.shape, const),
                  pl.BlockSpec((1, C), const),
                  pl.BlockSpec((1, C), const),
                  pl.BlockSpec((1, C), const),
                  pl.BlockSpec(wo.shape, const)]
        + [pl.BlockSpec((1,) + s.shape[1:], slab) for s in slabs],
        out_specs=[pl.BlockSpec((1, ts, D), lambda bi, si: (bi, si, 0))]
        + [pl.BlockSpec((1,) + s.shape[1:], slab) for s in slabs],
        scratch_shapes=[pltpu.VMEM((ts + CONV_HALO, C), F32),
                        pltpu.VMEM((SUBLANES - 1, ts + CONV_HALO - SUBLANES, C), F32),
                        pltpu.VMEM((ts, C), F32)],
        compiler_params=_params("arbitrary", "arbitrary"),
        name="conv",
    )(u3, w, b, lg, lb, wo, *slabs)
    return outs[0], [o.reshape(r.shape) for o, r in zip(outs[1:], riders)]


def _mlstm_kernel(qk_ref, v_ref, o_ref, ifc_ref, ifr_ref, cw_ref, cb_ref, bifc_ref, bifr_ref, ng_ref,
                  wo_ref, y_ref, qkbuf, cn_ref, m_ref, hbuf):
    @pl.when(pl.program_id(1) == 0)
    def _():
        qkbuf[:, 0:SUBLANES, :] = jnp.zeros((qkbuf.shape[0], SUBLANES, qkbuf.shape[2]), F32)
        cn_ref[...] = jnp.zeros(cn_ref.shape, F32)
        m_ref[...] = jnp.zeros(m_ref.shape, F32)

    nb, L, mi = hbuf.shape
    dh = mi // M_HEADS
    halo = SUBLANES
    off = halo - (QK_CONV_WIDTH - 1)
    rows = lax.broadcasted_iota(I32, (L, L), 0)
    cols = lax.broadcasted_iota(I32, (L, L), 1)
    causal = cols <= rows
    lower = jnp.where(causal, 1.0, 0.0).astype(BF16)
    upper = jnp.where(rows <= cols, 1.0, 0.0).astype(BF16)
    lane = lax.broadcasted_iota(I32, (L, dh), 1)
    ones_col = jnp.where(lane == 0, 1.0, 0.0).astype(BF16)
    scale = dh ** -0.5

    seqs = []
    for b in range(nb):
        qkbuf[b, halo:halo + L, :] = qk_ref[b]
        y = jnp.broadcast_to(cb_ref[...], (L, qkbuf.shape[2]))
        for k in range(QK_CONV_WIDTH):
            y = y + cw_ref[k:k + 1, :] * qkbuf[b, off + k:off + k + L, :]
        y = y * _sigmoid(y)
        qkbuf[b, 0:halo, :] = qkbuf[b, L:L + halo, :]
        ifr = ifr_ref[b] + bifr_ref[...]
        ifc = ifc_ref[b] + bifc_ref[...]
        bcum_c = sum(jnp.dot(lower, p, preferred_element_type=F32) for p in _split_bf16(_log_sigmoid(ifc)))
        bcum_r = sum(jnp.dot(p, upper, preferred_element_type=F32) for p in _split_bf16(_log_sigmoid(ifr)))
        seqs.append((y, ifr, bcum_c, bcum_r))

    probs = [(b, hd) for b in range(nb) for hd in range(M_HEADS)]
    st = {}
    for p in probs:
        b, hd = p
        y, ifr, bcum_c, bcum_r = seqs[b]
        c0 = hd * dh
        qb = (y[:, c0:c0 + dh] * scale).astype(BF16)
        kt = y[:, mi + c0:mi + c0 + dh].T
        v = v_ref[b, :, c0:c0 + dh]
        bc = bcum_c[:, M_HEADS + hd:M_HEADS + hd + 1]
        br = bcum_r[M_HEADS + hd:M_HEADS + hd + 1, :]
        li = ifr[hd:hd + 1, :]
        m_prev = m_ref[b, hd, 0:1, 0:1]
        dmat = jnp.where(causal, bc - br + li, -jnp.inf)
        st[p] = dict(qb=qb, kt=kt, v=v, bc=bc, br=br, li=li, m_prev=m_prev, dmat=dmat)
    for p in probs:
        s = st[p]
        s["inter"] = s["bc"] + s["m_prev"]
        s["m_t"] = jnp.maximum(jnp.max(s["dmat"], axis=-1, keepdims=True), s["inter"])
    for p in probs:
        s = st[p]
        s["qk"] = jnp.dot(s["qb"], s["kt"].astype(BF16), preferred_element_type=F32)
    for p in probs:
        b, hd = p
        s = st[p]
        s["cn"] = cn_ref[b, hd]
        s["qcn"] = jnp.dot(s["qb"], s["cn"].astype(BF16), preferred_element_type=F32)
    for p in probs:
        s = st[p]
        s["wts"] = jnp.exp(s["dmat"] - s["m_t"])
        s["s_inter"] = jnp.exp(s["inter"] - s["m_t"])
    for p in probs:
        s = st[p]
        s["s_mat"] = s["qk"] * s["wts"]
    for p in probs:
        s = st[p]
        s["sv"] = jnp.dot(s["s_mat"].astype(BF16), s["v"], preferred_element_type=F32)
    for p in probs:
        s = st[p]
        s["rowsum"] = jnp.sum(s["s_mat"], axis=-1, keepdims=True)
    for p in probs:
        s = st[p]
        s["num"] = s["sv"] + s["s_inter"] * s["qcn"][:, 0:dh]
        s["den"] = s["rowsum"] + s["s_inter"] * s["qcn"][:, dh:dh + 1]
    for p in probs:
        b, hd = p
        s = st[p]
        b_last = s["br"][:, L - 1:L]
        a = b_last - s["br"] + s["li"]
        m_new = jnp.maximum(b_last + s["m_prev"], jnp.max(a, axis=-1, keepdims=True))
        wk = jnp.exp(a - m_new)
        sc = jnp.exp(b_last + s["m_prev"] - m_new)
        v_ext = jnp.concatenate([s["v"], ones_col], axis=1)
        cn_ref[b, hd] = sc * s["cn"] + jnp.dot((s["kt"] * wk).astype(BF16), v_ext, preferred_element_type=F32)
        m_ref[b, hd] = jnp.broadcast_to(m_new, m_ref.shape[2:])
    for p in probs:
        s = st[p]
        s["hh"] = s["num"] / jnp.maximum(jnp.abs(s["den"]), jnp.exp(-s["m_t"]))
        s["mu"] = jnp.mean(s["hh"], axis=-1, keepdims=True)
    for p in probs:
        s = st[p]
        s["hc"] = s["hh"] - s["mu"]
        s["var"] = jnp.mean(s["hc"] * s["hc"], axis=-1, keepdims=True)
    for p in probs:
        b, hd = p
        s = st[p]
        c0 = hd * dh
        hn = s["hc"] * lax.rsqrt(s["var"] + LN_EPS) * ng_ref[:, c0:c0 + dh]
        hbuf[b, :, c0:c0 + dh] = hn * _sigmoid(o_ref[b, :, c0:c0 + dh])
    for b in range(nb):
        y = jnp.dot(hbuf[b].astype(BF16), wo_ref[...], preferred_element_type=F32)
        y_ref[b] = y.astype(BF16)


def _mlstm_branch(qk3, v3, o3, ifc3, ifr3, cw, cb, bifc, bifr, ng, wo):
    B, S, C2 = qk3.shape
    mi = v3.shape[2]
    dh = mi // M_HEADS
    D = wo.shape[1]
    L = MLSTM_L
    nb = MLSTM_SEQS
    const = lambda bi, ci: (0, 0)
    tile = lambda bi, ci: (bi, ci, 0)
    return pl.pallas_call(
        _mlstm_kernel,
        out_shape=jax.ShapeDtypeStruct((B, S, D), BF16),
        grid=(B // nb, S // L),
        in_specs=[pl.BlockSpec((nb, L, C2), tile),
                  pl.BlockSpec((nb, L, mi), tile),
                  pl.BlockSpec((nb, L, mi), tile),
                  pl.BlockSpec((nb, L, LANES), tile),
                  pl.BlockSpec((nb, SUBLANES, L), lambda bi, ci: (bi, 0, ci)),
                  pl.BlockSpec(cw.shape, const),
                  pl.BlockSpec((1, C2), const),
                  pl.BlockSpec((1, LANES), const),
                  pl.BlockSpec((SUBLANES, 1), const),
                  pl.BlockSpec((1, mi), const),
                  pl.BlockSpec(wo.shape, const)],
        out_specs=pl.BlockSpec((nb, L, D), tile),
        scratch_shapes=[pltpu.VMEM((nb, L + SUBLANES, C2), F32),
                        pltpu.VMEM((nb, M_HEADS, dh, 2 * dh), F32),
                        pltpu.VMEM((nb, M_HEADS, SUBLANES, LANES), F32),
                        pltpu.VMEM((nb, L, mi), F32)],
        compiler_params=_params("arbitrary", "arbitrary"),
        name="mlstm",
    )(qk3, v3, o3, ifc3, ifr3, cw, cb, bifc, bifr, ng, wo)


def _merge_kernel(x_ref, ya_ref, yb_ref, sga_ref, sgb_ref, mod_ref, g2_ref, wo_ref, wr_ref, br_ref,
                  x1_ref, h2_ref, ri_ref, rf_ref, cnt_ref, run_ref):
    ts = MERGE_SUB
    subs = [pl.ds(r0, ts) for r0 in range(0, x_ref.shape[0], ts)]

    @pl.when(pl.program_id(0) == 0)
    def _():
        run_ref[...] = jnp.zeros(run_ref.shape, F32)

    gate1 = mod_ref[0, 2:3, :]
    shift2 = mod_ref[0, 3:4, :]
    scale2 = mod_ref[0, 4:5, :]
    lane = lax.broadcasted_iota(I32, (ts, LANES), 1).astype(F32)
    neg = -jnp.inf
    rows = lax.broadcasted_iota(I32, (ts, ts), 0)
    cols = lax.broadcasted_iota(I32, (ts, ts), 1)
    strict = jnp.where(cols < rows, 1.0, 0.0).astype(BF16)

    def first_argmax(vals):
        mx = jnp.max(vals, axis=-1, keepdims=True)
        idx = jnp.min(jnp.where(vals == mx, lane, float(LANES)), axis=-1, keepdims=True)
        return mx, idx

    h2s = []
    for sl in subs:
        merged = (sga_ref[sl, :].astype(F32) * ya_ref[sl, :].astype(F32)
                  + sgb_ref[sl, :].astype(F32) * yb_ref[sl, :].astype(F32))
        mix = jnp.dot(merged.astype(BF16), wo_ref[...], preferred_element_type=F32)
        x1 = x_ref[sl, :] + gate1 * mix
        x1_ref[sl, :] = x1
        ms = jnp.mean(x1 * x1, axis=-1, keepdims=True)
        h2 = x1 * lax.rsqrt(ms + RMS_EPS) * g2_ref[...]
        h2 = h2 * (1.0 + scale2) + shift2
        h2_ref[sl, :] = _pack_bf16_pairs(h2)
        h2s.append(h2.astype(BF16))

    run = run_ref[0:1, :]
    for sl, h2b in zip(subs, h2s):
        logits = jnp.dot(h2b, wr_ref[...], preferred_element_type=F32) + br_ref[...]
        lg = jnp.where(lane < N_GROUPS, logits, neg)
        gmax, gsel = first_argmax(lg)
        p_g = 1.0 / jnp.sum(jnp.exp(lg - gmax), axis=-1, keepdims=True)
        lo = N_GROUPS + gsel * E_PER_GROUP
        le = jnp.where((lane >= lo) & (lane < lo + E_PER_GROUP), logits, neg)
        l1, i1 = first_argmax(le)
        l2, i2 = first_argmax(jnp.where(lane == i1, neg, le))
        r = jnp.exp(l2 - l1)
        w1 = p_g / (1.0 + r)
        w2 = p_g * r / (1.0 + r)
        e1 = i1 - N_GROUPS
        e2 = i2 - N_GROUPS

        onehot = jnp.where((lane == e1) | (lane == e2), 1.0, 0.0)
        before = jnp.dot(strict, onehot.astype(BF16), preferred_element_type=F32) + run
        rank1 = jnp.sum(jnp.where(lane == e1, before, 0.0), axis=-1, keepdims=True)
        rank2 = jnp.sum(jnp.where(lane == e2, before, 0.0), axis=-1, keepdims=True)
        run = run + jnp.sum(onehot, axis=0, keepdims=True)

        codes = jnp.where(lane == 0, e1 * float(RANK_RADIX) + rank1,
                          jnp.where(lane == 1, e2 * float(RANK_RADIX) + rank2, 0.0))
        ri_ref[:, sl] = codes.T[0:SUBLANES, :].astype(I32)
        rf_ref[sl, :] = jnp.where(lane == 0, w1, jnp.where(lane == 1, w2, 0.0))
    run_ref[...] = jnp.broadcast_to(run, run_ref.shape)
    cnt_ref[...] = jnp.broadcast_to(run, cnt_ref.shape).astype(I32)


def _merge(x2, ya, yb, sga, sgb, mod3, g2, wo, wr, br, seq):
    T, D = x2.shape
    tm = MERGE_TM
    per_b = seq // tm
    row = lambda i: (i, 0)
    const = lambda i: (0, 0)
    return pl.pallas_call(
        _merge_kernel,
        out_shape=[jax.ShapeDtypeStruct((T, D), F32),
                   jax.ShapeDtypeStruct((T, D // 2), I32),
                   jax.ShapeDtypeStruct((SUBLANES, T), I32),
                   jax.ShapeDtypeStruct((T, LANES), F32),
                   jax.ShapeDtypeStruct((SUBLANES, LANES), I32)],
        grid=(T // tm,),
        in_specs=[pl.BlockSpec((tm, D), row),
                  pl.BlockSpec((tm, D), row),
                  pl.BlockSpec((tm, D), row),
                  pl.BlockSpec((tm, D), row),
                  pl.BlockSpec((tm, D), row),
                  pl.BlockSpec((1, 6, D), lambda i: (i // per_b, 0, 0)),
                  pl.BlockSpec((1, D), const),
                  pl.BlockSpec(wo.shape, const),
                  pl.BlockSpec(wr.shape, const),
                  pl.BlockSpec((1, LANES), const)],
        out_specs=[pl.BlockSpec((tm, D), row),
                   pl.BlockSpec((tm, D // 2), row),
                   pl.BlockSpec((SUBLANES, tm), lambda i: (0, i)),
                   pl.BlockSpec((tm, LANES), row),
                   pl.BlockSpec((SUBLANES, LANES), const)],
        scratch_shapes=[pltpu.VMEM((SUBLANES, LANES), F32)],
        compiler_params=_params("arbitrary"),
        name="merge",
    )(x2, ya, yb, sga, sgb, mod3, g2, wo, wr, br)


def _sc_workers():
    info = plsc.get_sparse_core_info()
    mesh = plsc.VectorSubcoreMesh(core_axis_name="core", subcore_axis_name="subcore")
    params = pltpu.CompilerParams()
    if "needs_layout_passes" in pltpu.CompilerParams.__dataclass_fields__:
        params = dataclasses.replace(params, needs_layout_passes=False)
    return info, mesh, params


def _rows_from_codes(code_v, base_v, idx_v, lanes):
    for j in range(code_v.shape[0] // lanes):
        c = code_v[pl.ds(j * lanes, lanes)]
        expert = lax.shift_right_logical(c, RANK_BITS)
        idx_v[pl.ds(j * lanes, lanes)] = plsc.load_gather(base_v, [expert]) + (c & (RANK_RADIX - 1))


def _two_slot_loop(n_chunks, start, finish):
    start(0, 0)

    @pl.loop(0, n_chunks, step=2)
    def _(c):
        start(c + 1, 1)
        finish(c, 0)

        @pl.when(c + 2 < n_chunks)
        def _():
            start(c + 2, 0)

        finish(c + 1, 1)


def _sc_dispatch(h2, code0, code1, base, n_rows):
    T, D = h2.shape
    info, mesh, params = _sc_workers()
    n_workers = info.num_cores * info.num_subcores
    w = SC_WINDOW_BYTES // (D * h2.dtype.itemsize)
    per_w = T // n_workers
    n_chunks = per_w // w
    assert per_w * n_workers == T and n_chunks * w == per_w and n_chunks % 2 == 0

    @functools.partial(
        pl.kernel, out_type=jax.ShapeDtypeStruct((n_rows, D), h2.dtype), mesh=mesh, compiler_params=params,
        scratch_types=[pltpu.VMEM((N_EXPERTS,), I32), pltpu.VMEM((w,), I32), pltpu.VMEM((w,), I32),
                       pltpu.VMEM((w,), I32), pltpu.VMEM((w, D), h2.dtype), pltpu.VMEM((w, D), h2.dtype),
                       pltpu.SemaphoreType.DMA, pltpu.SemaphoreType.DMA])
    def scatter(h_hbm, c0_hbm, c1_hbm, b_hbm, xs_hbm, base_v, code_v, i0_v, i1_v, rows0, rows1, sem0, sem1):
        wid = lax.axis_index("subcore") * info.num_cores + lax.axis_index("core")
        w0 = wid * per_w
        pltpu.sync_copy(b_hbm, base_v)
        rows = (rows0, rows1)
        sems = (sem0, sem1)

        def start(c, slot):
            pltpu.async_copy(h_hbm.at[pl.ds(w0 + c * w, w)], rows[slot], sems[slot])

        def finish(c, slot):
            pltpu.sync_copy(c0_hbm.at[pl.ds(w0 + c * w, w)], code_v)
            _rows_from_codes(code_v, base_v, i0_v, info.num_lanes)
            pltpu.sync_copy(c1_hbm.at[pl.ds(w0 + c * w, w)], code_v)
            _rows_from_codes(code_v, base_v, i1_v, info.num_lanes)
            pltpu.make_async_copy(h_hbm.at[pl.ds(w0 + c * w, w)], rows[slot], sems[slot]).wait()
            pltpu.sync_copy(rows[slot], xs_hbm.at[i0_v])
            pltpu.sync_copy(rows[slot], xs_hbm.at[i1_v])

        _two_slot_loop(n_chunks, start, finish)

    return scatter(h2, code0, code1, base)


def _sc_collect(ys, codes, base):
    n = codes.shape[0]
    D = ys.shape[1]
    info, mesh, params = _sc_workers()
    n_workers = info.num_cores * info.num_subcores
    w = SC_WINDOW_BYTES // (D * ys.dtype.itemsize)
    per_w = n // n_workers
    n_chunks = per_w // w
    assert per_w * n_workers == n and n_chunks * w == per_w and n_chunks % 2 == 0

    @functools.partial(
        pl.kernel, out_type=jax.ShapeDtypeStruct((n, D), ys.dtype), mesh=mesh, compiler_params=params,
        scratch_types=[pltpu.VMEM((N_EXPERTS,), I32), pltpu.VMEM((w,), I32), pltpu.VMEM((w,), I32),
                       pltpu.VMEM((w,), I32), pltpu.VMEM((w, D), ys.dtype), pltpu.VMEM((w, D), ys.dtype),
                       pltpu.SemaphoreType.DMA, pltpu.SemaphoreType.DMA])
    def gather(ys_hbm, c_hbm, b_hbm, yk_hbm, base_v, code_v, i0_v, i1_v, rows0, rows1, sem0, sem1):
        wid = lax.axis_index("subcore") * info.num_cores + lax.axis_index("core")
        w0 = wid * per_w
        pltpu.sync_copy(b_hbm, base_v)
        idx = (i0_v, i1_v)
        rows = (rows0, rows1)
        sems = (sem0, sem1)

        def start(c, slot):
            pltpu.sync_copy(c_hbm.at[pl.ds(w0 + c * w, w)], code_v)
            _rows_from_codes(code_v, base_v, idx[slot], info.num_lanes)
            pltpu.async_copy(ys_hbm.at[idx[slot]], rows[slot], sems[slot])

        def finish(c, slot):
            pltpu.make_async_copy(ys_hbm.at[idx[slot]], rows[slot], sems[slot]).wait()
            pltpu.sync_copy(rows[slot], yk_hbm.at[pl.ds(w0 + c * w, w)])

        _two_slot_loop(n_chunks, start, finish)

    return gather(ys, codes, base)


def _schedule_kernel(cnt_ref, te_ref, tb_ref, base_ref, nt_ref):
    tm = EXPERT_TM

    def expert(e, t0):
        n = (cnt_ref[e] + tm - 1) // tm
        base_ref[e] = t0 * tm

        def tile(t, c):
            te_ref[t] = e
            tb_ref[t] = t
            return c

        lax.fori_loop(t0, t0 + n, tile, 0)
        return t0 + n

    nt = lax.fori_loop(0, N_EXPERTS, expert, 0)
    nt_ref[0] = nt
    last = te_ref[nt - 1]

    def idle(t, c):
        te_ref[t] = last
        tb_ref[t] = nt - 1
        return c

    lax.fori_loop(nt, te_ref.shape[0], idle, 0)


def _schedule(counts, max_tiles):
    smem = pl.BlockSpec(memory_space=pltpu.SMEM)
    return pl.pallas_call(
        _schedule_kernel,
        out_shape=[jax.ShapeDtypeStruct((max_tiles,), I32),
                   jax.ShapeDtypeStruct((max_tiles,), I32),
                   jax.ShapeDtypeStruct((N_EXPERTS,), I32),
                   jax.ShapeDtypeStruct((1,), I32)],
        in_specs=[smem],
        out_specs=[smem, smem, smem, smem],
        name="schedule",
    )(counts)


def _expert_kernel(te_ref, tb_ref, nt_ref, xs_ref, wg_ref, wu_ref, wd_ref, ys_ref):
    @pl.when(pl.program_id(0) < nt_ref[0])
    def _():
        xb = _unpack_bf16_pairs(xs_ref[...])
        g = jnp.dot(xb, wg_ref[0], preferred_element_type=F32)
        u = jnp.dot(xb, wu_ref[0], preferred_element_type=F32)
        act = (g * _sigmoid(g)) * u
        ys_ref[...] = jnp.dot(act.astype(BF16), wd_ref[0], preferred_element_type=F32)


def _experts(tile_e, tile_b, n_tiles, xs, wg, wu, wd, max_tiles):
    P = xs.shape[0]
    D, de = wg.shape[1:]
    tm = EXPERT_TM
    wmap = lambda j, te, tb, nt: (te[j], 0, 0)
    rmap = lambda j, te, tb, nt: (tb[j], 0)
    return pl.pallas_call(
        _expert_kernel,
        out_shape=jax.ShapeDtypeStruct((P, D), F32),
        grid_spec=pltpu.PrefetchScalarGridSpec(
            num_scalar_prefetch=3,
            grid=(max_tiles,),
            in_specs=[pl.BlockSpec((tm, xs.shape[1]), rmap),
                      pl.BlockSpec((1, D, de), wmap),
                      pl.BlockSpec((1, D, de), wmap),
                      pl.BlockSpec((1, de, D), wmap)],
            out_specs=pl.BlockSpec((tm, D), rmap)),
        compiler_params=_params("arbitrary"),
        name="experts",
    )(tile_e, tile_b, n_tiles, xs, wg, wu, wd)


def _combine_kernel(x1_ref, rf_ref, mod_ref, gf_ref, y0_ref, y1_ref, out_ref, *, final_norm):
    gate2 = mod_ref[0, 5:6, :]
    w = rf_ref[...]
    moe = w[:, 0:1] * y0_ref[...] + w[:, 1:2] * y1_ref[...]
    x2 = x1_ref[...] + gate2 * moe
    if final_norm:
        ms = jnp.mean(x2 * x2, axis=-1, keepdims=True)
        x2 = x2 * lax.rsqrt(ms + RMS_EPS) * gf_ref[...]
    out_ref[...] = x2


def _combine(x1, rf, mod3, gf, yk, seq, final_norm):
    T, D = x1.shape
    tc = COMBINE_TM
    per_b = seq // tc
    n_blk = T // tc
    return pl.pallas_call(
        functools.partial(_combine_kernel, final_norm=final_norm),
        out_shape=jax.ShapeDtypeStruct((T, D), F32),
        grid=(n_blk,),
        in_specs=[pl.BlockSpec((tc, D), lambda i: (i, 0)),
                  pl.BlockSpec((tc, LANES), lambda i: (i, 0)),
                  pl.BlockSpec((1, 6, D), lambda i: (i // per_b, 0, 0)),
                  pl.BlockSpec((1, D), lambda i: (0, 0)),
                  pl.BlockSpec((tc, D), lambda i: (i, 0)),
                  pl.BlockSpec((tc, D), lambda i: (n_blk + i, 0))],
        out_specs=pl.BlockSpec((tc, D), lambda i: (i, 0)),
        compiler_params=_params("arbitrary"),
        name="combine",
    )(x1, rf, mod3, gf, yk, yk)


def _layer(x2, c, seq, layer, w_ada, b_ada, g_norm1, w_in, b_if, conv_dw_w, conv_dw_b, conv_ln_g, conv_ln_b,
           w_conv_out, qk_conv_w, qk_conv_b, m_norm_g, w_m_out, w_out, g_norm2, w_rg, b_rg,
           w_re, b_re, w_e_gate, w_e_up, w_e_down):
    T, D = x2.shape
    B = T // seq
    dc = D // 2
    nif = 2 * M_HEADS

    if_lo = 6 * dc
    full = lambda w: (w, [(0, w.shape[2])])
    mod, (w_main, w_gates, w_out_b, w_conv_out_b, w_m_out_b) = _ada(
        c, w_ada, b_ada, layer,
        [(w_in, [(0, if_lo), (if_lo + nif, w_in.shape[2])]), full(w_out), full(w_conv_out), full(w_m_out)])
    mod3 = mod.reshape(B, 6, D)
    w_if = w_in[layer, :, if_lo:if_lo + nif]
    w_if_pad = jnp.pad(w_if, ((0, 0), (0, LANES - nif))).astype(BF16)
    w_ift = w_if.T.astype(BF16)
    u, qk, v, o, sga, sgb, ifc, ifr = _inproj(x2, mod3, g_norm1.reshape(1, D), w_main, w_gates,
                                              w_if_pad, w_ift, seq)

    ya, (wg_b, wu_b, wd_b) = _conv_branch(
        u.reshape(B, seq, dc), conv_dw_w, conv_dw_b.reshape(1, dc), conv_ln_g.reshape(1, dc),
        conv_ln_b.reshape(1, dc), w_conv_out_b, riders=(w_e_gate, w_e_up, w_e_down))
    bifc = jnp.pad(b_if, (0, LANES - nif)).reshape(1, LANES)
    bifr = b_if.reshape(nif, 1)
    yb = _mlstm_branch(
        qk.reshape(B, seq, 2 * dc), v.reshape(B, seq, dc), o.reshape(B, seq, dc),
        ifc.reshape(B, seq, LANES), ifr, qk_conv_w, qk_conv_b.reshape(1, 2 * dc), bifc, bifr,
        m_norm_g.reshape(1, dc), w_m_out_b)

    n_r = N_GROUPS + N_EXPERTS
    w_r = jnp.pad(jnp.concatenate([w_rg, w_re], axis=1), ((0, 0), (0, LANES - n_r))).astype(BF16)
    b_r = jnp.pad(jnp.concatenate([b_rg, b_re]), (0, LANES - n_r)).reshape(1, LANES)
    x1, h2, ri, rf, cnt = _merge(x2, ya.reshape(T, D), yb.reshape(T, D), sga, sgb, mod3,
                                 g_norm2.reshape(1, D), w_out_b, w_r, b_r, seq)

    tm = EXPERT_TM
    max_tiles = (T * TOP_K) // tm + N_EXPERTS
    tile_e, tile_b, base, n_tiles = _schedule(cnt[0, :N_EXPERTS], max_tiles)

    code0 = ri[0]
    code1 = ri[1]
    xs = _sc_dispatch(h2, code0, code1, base, max_tiles * tm)
    ys = _experts(tile_e, tile_b, n_tiles, xs, wg_b, wu_b, wd_b, max_tiles)
    return x1, rf, mod3, ys, code0, code1, base


def kernel(x, c, w_ada, b_ada, g_norm1, w_in, b_if, conv_dw_w, conv_dw_b, conv_ln_g, conv_ln_b,
           w_conv_out, qk_conv_w, qk_conv_b, m_norm_g, w_m_out, w_out, g_norm2, w_rg, b_rg,
           w_re, b_re, w_e_gate, w_e_up, w_e_down, g_final):
    B, S, D = x.shape
    depth = w_ada.shape[0]
    x2 = x.reshape(B * S, D)
    for l in range(depth):
        x1, rf, mod3, ys, code0, code1, base = _layer(
            x2, c, S, l, w_ada[l], b_ada[l], g_norm1[l], w_in, b_if[l], conv_dw_w[l], conv_dw_b[l],
            conv_ln_g[l], conv_ln_b[l], w_conv_out, qk_conv_w[l], qk_conv_b[l], m_norm_g[l],
            w_m_out, w_out, g_norm2[l], w_rg[l], b_rg[l], w_re[l], b_re[l],
            w_e_gate[l], w_e_up[l], w_e_down[l])
        yk = _sc_collect(ys, jnp.concatenate([code0, code1]), base)
        x2 = _combine(x1, rf, mod3, g_final.reshape(1, D), yk, S, final_norm=l == depth - 1)
    return x2.reshape(B, S, D)
```

```python
import dataclasses
import functools

import jax
import jax.numpy as jnp
from jax import lax
from jax.experimental import pallas as pl
from jax.experimental.pallas import tpu as pltpu
from jax.experimental.pallas import tpu_sc as plsc

F32 = jnp.float32
BF16 = jnp.bfloat16
I32 = jnp.int32

M_HEADS = 4
CONV_WIDTH = 31
QK_CONV_WIDTH = 4
N_GROUPS = 4
E_PER_GROUP = 8
N_EXPERTS = N_GROUPS * E_PER_GROUP
TOP_K = 2
RMS_EPS = 1e-6
LN_EPS = 1e-5

LANES = 128
SUBLANES = 8
VMEM_LIMIT = 56 * 1024 * 1024

ADA_TN = 768
INPROJ_TM = 512
INPROJ_SUB = 256
CONV_TS = 512
CONV_HALO = 32
CONV_RC = 256
MLSTM_L = 128
MLSTM_SEQS = 8
MERGE_TM = 512
MERGE_SUB = 256
EXPERT_TM = 512
SC_WINDOW_BYTES = 128 * 1024
COMBINE_TM = 256
RANK_BITS = 16
RANK_RADIX = 1 << RANK_BITS
assert EXPERT_TM & (EXPERT_TM - 1) == 0


def _sigmoid(v):
    return 1.0 / (1.0 + jnp.exp(-v))


def _log_sigmoid(v):
    return -(jnp.maximum(-v, 0.0) + jnp.log1p(jnp.exp(-jnp.abs(v))))


def _pack_bf16_pairs(v):
    n = v.shape[1] // 2
    bits = lax.bitcast_convert_type(v.astype(BF16).astype(F32), jnp.uint32)
    word = bits[:, n:] | (bits[:, :n] >> 16)
    return lax.bitcast_convert_type(word, I32)


def _unpack_bf16_pairs(w):
    bits = lax.bitcast_convert_type(w, jnp.uint32)
    lo = lax.bitcast_convert_type(bits << 16, F32)
    hi = lax.bitcast_convert_type(bits & jnp.uint32(0xFFFF0000), F32)
    return jnp.concatenate([lo, hi], axis=1).astype(BF16)


def _split_bf16(v):
    hi = v.astype(BF16)
    r1 = v - hi.astype(F32)
    mid = r1.astype(BF16)
    lo = (r1 - mid.astype(F32)).astype(BF16)
    return hi, mid, lo


def _params(*sem):
    return pltpu.CompilerParams(dimension_semantics=sem, vmem_limit_bytes=VMEM_LIMIT)


def _ada_kernel(*refs, cuts):
    c_ref, w_ref, b_ref = refs[:3]
    srcs = refs[3:3 + len(cuts)]
    o_ref = refs[3 + len(cuts)]
    dsts = iter(refs[4 + len(cuts):])
    c = c_ref[...]
    s = c * _sigmoid(c)
    o_ref[...] = jnp.dot(s, w_ref[...], preferred_element_type=F32,
                         precision=lax.Precision.HIGHEST) + b_ref[...]
    for src, ranges in zip(srcs, cuts):
        for lo, hi in ranges:
            next(dsts)[...] = src[:, lo:hi].astype(BF16)


def _ada(c, w_ada, b_ada, layer, riders):
    B, D = c.shape
    N = w_ada.shape[1]
    n_steps = N // ADA_TN
    slab = lambda j: (j, 0)
    outs = pl.pallas_call(
        functools.partial(_ada_kernel, cuts=tuple(tuple(r) for _, r in riders)),
        out_shape=[jax.ShapeDtypeStruct((B, N), F32)]
        + [jax.ShapeDtypeStruct((w.shape[1], hi - lo), BF16) for w, r in riders for lo, hi in r],
        grid=(n_steps,),
        in_specs=[pl.BlockSpec((B, D), lambda j: (0, 0)),
                  pl.BlockSpec((D, ADA_TN), lambda j: (0, j)),
                  pl.BlockSpec((1, ADA_TN), lambda j: (0, j))]
        + [pl.BlockSpec((None, w.shape[1] // n_steps, w.shape[2]), lambda j: (layer, j, 0))
           for w, _ in riders],
        out_specs=[pl.BlockSpec((B, ADA_TN), lambda j: (0, j))]
        + [pl.BlockSpec((w.shape[1] // n_steps, hi - lo), slab) for w, r in riders for lo, hi in r],
        compiler_params=_params("arbitrary"),
        name="ada",
    )(c, w_ada, b_ada.reshape(1, N), *[w for w, _ in riders])
    return outs[0], outs[1:]


def _inproj_kernel(*refs, n_riders):
    x_ref, mod_ref, g_ref, wm_ref, wgt_ref, wif_ref, wift_ref = refs[:7]
    rider_in = refs[7:7 + n_riders]
    u_ref, qk_ref, v_ref, o_ref, sga_ref, sgb_ref, ifc_ref, ifr_ref = refs[7 + n_riders:15 + n_riders]
    rider_out = refs[15 + n_riders:]

    for src, dst in zip(rider_in, rider_out):
        dst[...] = src[...].astype(BF16)

    shift = mod_ref[0, 0:1, :]
    scale = mod_ref[0, 1:2, :]
    dc = u_ref.shape[1]
    d = sga_ref.shape[1]
    ts = INPROJ_SUB
    subs = [pl.ds(r0, ts) for r0 in range(0, x_ref.shape[0], ts)]

    hbs = []
    for sl in subs:
        x = x_ref[sl, :]
        ms = jnp.mean(x * x, axis=-1, keepdims=True)
        h = x * lax.rsqrt(ms + RMS_EPS) * g_ref[...]
        h = h * (1.0 + scale) + shift
        hbs.append(h.astype(BF16))

    for sl, hb in zip(subs, hbs):
        def seg(lo, hi):
            return jnp.dot(hb, wm_ref[:, lo:hi], preferred_element_type=F32)

        u_ref[sl, :] = seg(0, dc) * _sigmoid(seg(dc, 2 * dc))
        qk_ref[sl, :] = seg(2 * dc, 4 * dc)
        v_ref[sl, :] = seg(4 * dc, 5 * dc).astype(BF16)
        o_ref[sl, :] = seg(5 * dc, 6 * dc)
        sga_ref[sl, :] = _sigmoid(jnp.dot(hb, wgt_ref[:, 0:d], preferred_element_type=F32)).astype(BF16)
        sgb_ref[sl, :] = _sigmoid(jnp.dot(hb, wgt_ref[:, d:2 * d], preferred_element_type=F32)).astype(BF16)
        ifc_ref[sl, :] = jnp.dot(hb, wif_ref[...], preferred_element_type=F32)
        ifr_ref[0, :, sl] = lax.dot_general(wift_ref[...], hb, (((1,), (1,)), ((), ())),
                                            preferred_element_type=F32)


def _inproj(x2, mod3, g1, w_main, w_gates, w_if, w_ift, seq, riders):
    T, D = x2.shape
    tm = INPROJ_TM
    dc = D // 2
    per_b = seq // tm
    row = lambda i: (i, 0)
    const = lambda i: (0, 0)
    slab = lambda i: (i, 0, 0)
    slabs = [r.reshape(T // tm, -1, r.shape[-1]) for r in riders]
    once = pl.Buffered(1)
    outs = pl.pallas_call(
        functools.partial(_inproj_kernel, n_riders=len(riders)),
        out_shape=[jax.ShapeDtypeStruct((T, dc), F32),
                   jax.ShapeDtypeStruct((T, 2 * dc), F32),
                   jax.ShapeDtypeStruct((T, dc), BF16),
                   jax.ShapeDtypeStruct((T, dc), F32),
                   jax.ShapeDtypeStruct((T, D), BF16),
                   jax.ShapeDtypeStruct((T, D), BF16),
                   jax.ShapeDtypeStruct((T, LANES), F32),
                   jax.ShapeDtypeStruct((T // seq, SUBLANES, seq), F32)]
        + [jax.ShapeDtypeStruct(s.shape, BF16) for s in slabs],
        grid=(T // tm,),
        in_specs=[pl.BlockSpec((tm, D), row),
                  pl.BlockSpec((1, 6, D), lambda i: (i // per_b, 0, 0)),
                  pl.BlockSpec((1, D), const),
                  pl.BlockSpec(w_main.shape, const, pipeline_mode=once),
                  pl.BlockSpec(w_gates.shape, const, pipeline_mode=once),
                  pl.BlockSpec(w_if.shape, const),
                  pl.BlockSpec(w_ift.shape, const)]
        + [pl.BlockSpec((1,) + s.shape[1:], slab) for s in slabs],
        out_specs=[pl.BlockSpec((tm, dc), row),
                   pl.BlockSpec((tm, 2 * dc), row),
                   pl.BlockSpec((tm, dc), row),
                   pl.BlockSpec((tm, dc), row),
                   pl.BlockSpec((tm, D), row),
                   pl.BlockSpec((tm, D), row),
                   pl.BlockSpec((tm, LANES), row),
                   pl.BlockSpec((1, SUBLANES, tm), lambda i: (i // per_b, 0, i % per_b))]
        + [pl.BlockSpec((1,) + s.shape[1:], slab) for s in slabs],
        compiler_params=_params("arbitrary"),
        name="inproj",
    )(x2, mod3, g1, w_main, w_gates, w_if, w_ift, *slabs)
    return outs[:8], [o.reshape(r.shape) for o, r in zip(outs[8:], riders)]


def _conv_kernel(u_ref, w_ref, b_ref, lg_ref, lb_ref, wo_ref, y_ref, ubuf, sbuf, cbuf):
    ts = u_ref.shape[1]
    halo = CONV_HALO

    @pl.when(pl.program_id(1) == 0)
    def _():
        ubuf[0:halo, :] = jnp.zeros((halo, ubuf.shape[1]), F32)

    ubuf[halo:halo + ts, :] = u_ref[0]
    ns = sbuf.shape[1]
    for r in range(1, SUBLANES):
        sbuf[r - 1] = ubuf[r:r + ns, :]
    off = halo - (CONV_WIDTH - 1)
    for r0 in range(0, ts, CONV_RC):
        acc = jnp.broadcast_to(b_ref[...], (CONV_RC, ubuf.shape[1]))
        for k in range(CONV_WIDTH):
            r = (off + k) % SUBLANES
            lo = off + k - r + r0
            win = ubuf[lo:lo + CONV_RC, :] if r == 0 else sbuf[r - 1, lo:lo + CONV_RC, :]
            acc = acc + w_ref[k:k + 1, :] * win
        cbuf[r0:r0 + CONV_RC, :] = acc
    ubuf[0:halo, :] = ubuf[ts:ts + halo, :]

    a = cbuf[...]
    mu = jnp.mean(a, axis=-1, keepdims=True)
    ac = a - mu
    var = jnp.mean(ac * ac, axis=-1, keepdims=True)
    z = ac * lax.rsqrt(var + LN_EPS) * lg_ref[...] + lb_ref[...]
    z = z * _sigmoid(z)
    y_ref[0] = jnp.dot(z.astype(BF16), wo_ref[...], preferred_element_type=F32).astype(BF16)


def _conv_branch(u3, w, b, lg, lb, wo):
    B, S, C = u3.shape
    D = wo.shape[1]
    ts = CONV_TS
    const = lambda bi, si: (0, 0)
    return pl.pallas_call(
        _conv_kernel,
        out_shape=jax.ShapeDtypeStruct((B, S, D), BF16),
        grid=(B, S // ts),
        in_specs=[pl.BlockSpec((1, ts, C), lambda bi, si: (bi, si, 0)),
                  pl.BlockSpec(w.shape, const),
                  pl.BlockSpec((1, C), const),
                  pl.BlockSpec((1, C), const),
                  pl.BlockSpec((1, C), const),
                  pl.BlockSpec(wo.shape, const)],
        out_specs=pl.BlockSpec((1, ts, D), lambda bi, si: (bi, si, 0)),
        scratch_shapes=[pltpu.VMEM((ts + CONV_HALO, C), F32),
                        pltpu.VMEM((SUBLANES - 1, ts + CONV_HALO - SUBLANES, C), F32),
                        pltpu.VMEM((ts, C), F32)],
        compiler_params=_params("arbitrary", "arbitrary"),
        name="conv",
    )(u3, w, b, lg, lb, wo)


def _mlstm_kernel(qk_ref, v_ref, o_ref, ifc_ref, ifr_ref, cw_ref, cb_ref, bifc_ref, bifr_ref, ng_ref,
                  wo_ref, y_ref, qkbuf, cn_ref, m_ref, hbuf):
    @pl.when(pl.program_id(1) == 0)
    def _():
        qkbuf[:, 0:SUBLANES, :] = jnp.zeros((qkbuf.shape[0], SUBLANES, qkbuf.shape[2]), F32)
        cn_ref[...] = jnp.zeros(cn_ref.shape, F32)
        m_ref[...] = jnp.zeros(m_ref.shape, F32)

    nb, L, mi = hbuf.shape
    dh = mi // M_HEADS
    halo = SUBLANES
    off = halo - (QK_CONV_WIDTH - 1)
    rows = lax.broadcasted_iota(I32, (L, L), 0)
    cols = lax.broadcasted_iota(I32, (L, L), 1)
    causal = cols <= rows
    lower = jnp.where(causal, 1.0, 0.0).astype(BF16)
    upper = jnp.where(rows <= cols, 1.0, 0.0).astype(BF16)
    lane = lax.broadcasted_iota(I32, (L, dh), 1)
    ones_col = jnp.where(lane == 0, 1.0, 0.0).astype(BF16)
    scale = dh ** -0.5

    seqs = []
    for b in range(nb):
        qkbuf[b, halo:halo + L, :] = qk_ref[b]
        y = jnp.broadcast_to(cb_ref[...], (L, qkbuf.shape[2]))
        for k in range(QK_CONV_WIDTH):
            y = y + cw_ref[k:k + 1, :] * qkbuf[b, off + k:off + k + L, :]
        y = y * _sigmoid(y)
        qkbuf[b, 0:halo, :] = qkbuf[b, L:L + halo, :]
        ifr = ifr_ref[b] + bifr_ref[...]
        ifc = ifc_ref[b] + bifc_ref[...]
        bcum_c = sum(jnp.dot(lower, p, preferred_element_type=F32) for p in _split_bf16(_log_sigmoid(ifc)))
        bcum_r = sum(jnp.dot(p, upper, preferred_element_type=F32) for p in _split_bf16(_log_sigmoid(ifr)))
        seqs.append((y, ifr, bcum_c, bcum_r))

    probs = [(b, hd) for b in range(nb) for hd in range(M_HEADS)]
    st = {}
    for p in probs:
        b, hd = p
        y, ifr, bcum_c, bcum_r = seqs[b]
        c0 = hd * dh
        qb = (y[:, c0:c0 + dh] * scale).astype(BF16)
        kt = y[:, mi + c0:mi + c0 + dh].T
        v = v_ref[b, :, c0:c0 + dh]
        bc = bcum_c[:, M_HEADS + hd:M_HEADS + hd + 1]
        br = bcum_r[M_HEADS + hd:M_HEADS + hd + 1, :]
        li = ifr[hd:hd + 1, :]
        m_prev = m_ref[b, hd, 0:1, 0:1]
        dmat = jnp.where(causal, bc - br + li, -jnp.inf)
        st[p] = dict(qb=qb, kt=kt, v=v, bc=bc, br=br, li=li, m_prev=m_prev, dmat=dmat)
    for p in probs:
        s = st[p]
        s["inter"] = s["bc"] + s["m_prev"]
        s["m_t"] = jnp.maximum(jnp.max(s["dmat"], axis=-1, keepdims=True), s["inter"])
    for p in probs:
        s = st[p]
        s["qk"] = jnp.dot(s["qb"], s["kt"].astype(BF16), preferred_element_type=F32)
    for p in probs:
        b, hd = p
        s = st[p]
        s["cn"] = cn_ref[b, hd]
        s["qcn"] = jnp.dot(s["qb"], s["cn"].astype(BF16), preferred_element_type=F32)
    for p in probs:
        s = st[p]
        s["wts"] = jnp.exp(s["dmat"] - s["m_t"])
        s["s_inter"] = jnp.exp(s["inter"] - s["m_t"])
    for p in probs:
        s = st[p]
        s["s_mat"] = s["qk"] * s["wts"]
    for p in probs:
        s = st[p]
        s["sv"] = jnp.dot(s["s_mat"].astype(BF16), s["v"], preferred_element_type=F32)
    for p in probs:
        s = st[p]
        s["rowsum"] = jnp.sum(s["s_mat"], axis=-1, keepdims=True)
    for p in probs:
        s = st[p]
        s["num"] = s["sv"] + s["s_inter"] * s["qcn"][:, 0:dh]
        s["den"] = s["rowsum"] + s["s_inter"] * s["qcn"][:, dh:dh + 1]
    for p in probs:
        b, hd = p
        s = st[p]
        b_last = s["br"][:, L - 1:L]
        a = b_last - s["br"] + s["li"]
        m_new = jnp.maximum(b_last + s["m_prev"], jnp.max(a, axis=-1, keepdims=True))
        wk = jnp.exp(a - m_new)
        sc = jnp.exp(b_last + s["m_prev"] - m_new)
        v_ext = jnp.concatenate([s["v"], ones_col], axis=1)
        cn_ref[b, hd] = sc * s["cn"] + jnp.dot((s["kt"] * wk).astype(BF16), v_ext, preferred_element_type=F32)
        m_ref[b, hd] = jnp.broadcast_to(m_new, m_ref.shape[2:])
    for p in probs:
        s = st[p]
        s["hh"] = s["num"] / jnp.maximum(jnp.abs(s["den"]), jnp.exp(-s["m_t"]))
        s["mu"] = jnp.mean(s["hh"], axis=-1, keepdims=True)
    for p in probs:
        s = st[p]
        s["hc"] = s["hh"] - s["mu"]
        s["var"] = jnp.mean(s["hc"] * s["hc"], axis=-1, keepdims=True)
    for p in probs:
        b, hd = p
        s = st[p]
        c0 = hd * dh
        hn = s["hc"] * lax.rsqrt(s["var"] + LN_EPS) * ng_ref[:, c0:c0 + dh]
        hbuf[b, :, c0:c0 + dh] = hn * _sigmoid(o_ref[b, :, c0:c0 + dh])
    for b in range(nb):
        y = jnp.dot(hbuf[b].astype(BF16), wo_ref[...], preferred_element_type=F32)
        y_ref[b] = y.astype(BF16)


def _mlstm_branch(qk3, v3, o3, ifc3, ifr3, cw, cb, bifc, bifr, ng, wo):
    B, S, C2 = qk3.shape
    mi = v3.shape[2]
    dh = mi // M_HEADS
    D = wo.shape[1]
    L = MLSTM_L
    nb = MLSTM_SEQS
    const = lambda bi, ci: (0, 0)
    tile = lambda bi, ci: (bi, ci, 0)
    return pl.pallas_call(
        _mlstm_kernel,
        out_shape=jax.ShapeDtypeStruct((B, S, D), BF16),
        grid=(B // nb, S // L),
        in_specs=[pl.BlockSpec((nb, L, C2), tile),
                  pl.BlockSpec((nb, L, mi), tile),
                  pl.BlockSpec((nb, L, mi), tile),
                  pl.BlockSpec((nb, L, LANES), tile),
                  pl.BlockSpec((nb, SUBLANES, L), lambda bi, ci: (bi, 0, ci)),
                  pl.BlockSpec(cw.shape, const),
                  pl.BlockSpec((1, C2), const),
                  pl.BlockSpec((1, LANES), const),
                  pl.BlockSpec((SUBLANES, 1), const),
                  pl.BlockSpec((1, mi), const),
                  pl.BlockSpec(wo.shape, const)],
        out_specs=pl.BlockSpec((nb, L, D), tile),
        scratch_shapes=[pltpu.VMEM((nb, L + SUBLANES, C2), F32),
                        pltpu.VMEM((nb, M_HEADS, dh, 2 * dh), F32),
                        pltpu.VMEM((nb, M_HEADS, SUBLANES, LANES), F32),
                        pltpu.VMEM((nb, L, mi), F32)],
        compiler_params=_params("arbitrary", "arbitrary"),
        name="mlstm",
    )(qk3, v3, o3, ifc3, ifr3, cw, cb, bifc, bifr, ng, wo)


def _merge_kernel(x_ref, ya_ref, yb_ref, sga_ref, sgb_ref, mod_ref, g2_ref, wo_ref, wr_ref, br_ref,
                  x1_ref, h2_ref, ri_ref, rf_ref, cnt_ref, run_ref):
    ts = MERGE_SUB
    subs = [pl.ds(r0, ts) for r0 in range(0, x_ref.shape[0], ts)]

    @pl.when(pl.program_id(0) == 0)
    def _():
        run_ref[...] = jnp.zeros(run_ref.shape, F32)

    gate1 = mod_ref[0, 2:3, :]
    shift2 = mod_ref[0, 3:4, :]
    scale2 = mod_ref[0, 4:5, :]
    lane = lax.broadcasted_iota(I32, (ts, LANES), 1).astype(F32)
    neg = -jnp.inf
    rows = lax.broadcasted_iota(I32, (ts, ts), 0)
    cols = lax.broadcasted_iota(I32, (ts, ts), 1)
    strict = jnp.where(cols < rows, 1.0, 0.0).astype(BF16)

    def first_argmax(vals):
        mx = jnp.max(vals, axis=-1, keepdims=True)
        idx = jnp.min(jnp.where(vals == mx, lane, float(LANES)), axis=-1, keepdims=True)
        return mx, idx

    h2s = []
    for sl in subs:
        merged = (sga_ref[sl, :].astype(F32) * ya_ref[sl, :].astype(F32)
                  + sgb_ref[sl, :].astype(F32) * yb_ref[sl, :].astype(F32))
        mix = jnp.dot(merged.astype(BF16), wo_ref[...], preferred_element_type=F32)
        x1 = x_ref[sl, :] + gate1 * mix
        x1_ref[sl, :] = x1
        ms = jnp.mean(x1 * x1, axis=-1, keepdims=True)
        h2 = x1 * lax.rsqrt(ms + RMS_EPS) * g2_ref[...]
        h2 = h2 * (1.0 + scale2) + shift2
        h2_ref[sl, :] = _pack_bf16_pairs(h2)
        h2s.append(h2.astype(BF16))

    run = run_ref[0:1, :]
    for sl, h2b in zip(subs, h2s):
        logits = jnp.dot(h2b, wr_ref[...], preferred_element_type=F32) + br_ref[...]
        lg = jnp.where(lane < N_GROUPS, logits, neg)
        gmax, gsel = first_argmax(lg)
        p_g = 1.0 / jnp.sum(jnp.exp(lg - gmax), axis=-1, keepdims=True)
        lo = N_GROUPS + gsel * E_PER_GROUP
        le = jnp.where((lane >= lo) & (lane < lo + E_PER_GROUP), logits, neg)
        l1, i1 = first_argmax(le)
        l2, i2 = first_argmax(jnp.where(lane == i1, neg, le))
        r = jnp.exp(l2 - l1)
        w1 = p_g / (1.0 + r)
        w2 = p_g * r / (1.0 + r)
        e1 = i1 - N_GROUPS
        e2 = i2 - N_GROUPS

        onehot = jnp.where((lane == e1) | (lane == e2), 1.0, 0.0)
        before = jnp.dot(strict, onehot.astype(BF16), preferred_element_type=F32) + run
        rank1 = jnp.sum(jnp.where(lane == e1, before, 0.0), axis=-1, keepdims=True)
        rank2 = jnp.sum(jnp.where(lane == e2, before, 0.0), axis=-1, keepdims=True)
        run = run + jnp.sum(onehot, axis=0, keepdims=True)

        codes = jnp.where(lane == 0, e1 * float(RANK_RADIX) + rank1,
                          jnp.where(lane == 1, e2 * float(RANK_RADIX) + rank2, 0.0))
        ri_ref[:, sl] = codes.T[0:SUBLANES, :].astype(I32)
        rf_ref[sl, :] = jnp.where(lane == 0, w1, jnp.where(lane == 1, w2, 0.0))
    run_ref[...] = jnp.broadcast_to(run, run_ref.shape)
    cnt_ref[...] = jnp.broadcast_to(run, cnt_ref.shape).astype(I32)


def _merge(x2, ya, yb, sga, sgb, mod3, g2, wo, wr, br, seq):
    T, D = x2.shape
    tm = MERGE_TM
    per_b = seq // tm
    row = lambda i: (i, 0)
    const = lambda i: (0, 0)
    return pl.pallas_call(
        _merge_kernel,
        out_shape=[jax.ShapeDtypeStruct((T, D), F32),
                   jax.ShapeDtypeStruct((T, D // 2), I32),
                   jax.ShapeDtypeStruct((SUBLANES, T), I32),
                   jax.ShapeDtypeStruct((T, LANES), F32),
                   jax.ShapeDtypeStruct((SUBLANES, LANES), I32)],
        grid=(T // tm,),
        in_specs=[pl.BlockSpec((tm, D), row),
                  pl.BlockSpec((tm, D), row),
                  pl.BlockSpec((tm, D), row),
                  pl.BlockSpec((tm, D), row),
                  pl.BlockSpec((tm, D), row),
                  pl.BlockSpec((1, 6, D), lambda i: (i // per_b, 0, 0)),
                  pl.BlockSpec((1, D), const),
                  pl.BlockSpec(wo.shape, const),
                  pl.BlockSpec(wr.shape, const),
                  pl.BlockSpec((1, LANES), const)],
        out_specs=[pl.BlockSpec((tm, D), row),
                   pl.BlockSpec((tm, D // 2), row),
                   pl.BlockSpec((SUBLANES, tm), lambda i: (0, i)),
                   pl.BlockSpec((tm, LANES), row),
                   pl.BlockSpec((SUBLANES, LANES), const)],
        scratch_shapes=[pltpu.VMEM((SUBLANES, LANES), F32)],
        compiler_params=_params("arbitrary"),
        name="merge",
    )(x2, ya, yb, sga, sgb, mod3, g2, wo, wr, br)


def _sc_workers():
    info = plsc.get_sparse_core_info()
    mesh = plsc.VectorSubcoreMesh(core_axis_name="core", subcore_axis_name="subcore")
    params = pltpu.CompilerParams()
    if "needs_layout_passes" in pltpu.CompilerParams.__dataclass_fields__:
        params = dataclasses.replace(params, needs_layout_passes=False)
    return info, mesh, params


def _rows_from_codes(code_v, base_v, idx_v, lanes):
    for j in range(code_v.shape[0] // lanes):
        c = code_v[pl.ds(j * lanes, lanes)]
        expert = lax.shift_right_logical(c, RANK_BITS)
        idx_v[pl.ds(j * lanes, lanes)] = plsc.load_gather(base_v, [expert]) + (c & (RANK_RADIX - 1))


def _two_slot_loop(n_chunks, start, finish):
    start(0, 0)

    @pl.loop(0, n_chunks, step=2)
    def _(c):
        start(c + 1, 1)
        finish(c, 0)

        @pl.when(c + 2 < n_chunks)
        def _():
            start(c + 2, 0)

        finish(c + 1, 1)


def _sc_dispatch(h2, code0, code1, base, n_rows):
    T, D = h2.shape
    info, mesh, params = _sc_workers()
    n_workers = info.num_cores * info.num_subcores
    w = SC_WINDOW_BYTES // (D * h2.dtype.itemsize)
    per_w = T // n_workers
    n_chunks = per_w // w
    assert per_w * n_workers == T and n_chunks * w == per_w and n_chunks % 2 == 0

    @functools.partial(
        pl.kernel, out_type=jax.ShapeDtypeStruct((n_rows, D), h2.dtype), mesh=mesh, compiler_params=params,
        scratch_types=[pltpu.VMEM((N_EXPERTS,), I32), pltpu.VMEM((w,), I32), pltpu.VMEM((w,), I32),
                       pltpu.VMEM((w,), I32), pltpu.VMEM((w, D), h2.dtype), pltpu.VMEM((w, D), h2.dtype),
                       pltpu.SemaphoreType.DMA, pltpu.SemaphoreType.DMA])
    def scatter(h_hbm, c0_hbm, c1_hbm, b_hbm, xs_hbm, base_v, code_v, i0_v, i1_v, rows0, rows1, sem0, sem1):
        wid = lax.axis_index("subcore") * info.num_cores + lax.axis_index("core")
        w0 = wid * per_w
        pltpu.sync_copy(b_hbm, base_v)
        rows = (rows0, rows1)
        sems = (sem0, sem1)

        def start(c, slot):
            pltpu.async_copy(h_hbm.at[pl.ds(w0 + c * w, w)], rows[slot], sems[slot])

        def finish(c, slot):
            pltpu.sync_copy(c0_hbm.at[pl.ds(w0 + c * w, w)], code_v)
            _rows_from_codes(code_v, base_v, i0_v, info.num_lanes)
            pltpu.sync_copy(c1_hbm.at[pl.ds(w0 + c * w, w)], code_v)
            _rows_from_codes(code_v, base_v, i1_v, info.num_lanes)
            pltpu.make_async_copy(h_hbm.at[pl.ds(w0 + c * w, w)], rows[slot], sems[slot]).wait()
            pltpu.sync_copy(rows[slot], xs_hbm.at[i0_v])
            pltpu.sync_copy(rows[slot], xs_hbm.at[i1_v])

        _two_slot_loop(n_chunks, start, finish)

    return scatter(h2, code0, code1, base)


def _sc_collect(ys, codes, base):
    n = codes.shape[0]
    D = ys.shape[1]
    info, mesh, params = _sc_workers()
    n_workers = info.num_cores * info.num_subcores
    w = SC_WINDOW_BYTES // (D * ys.dtype.itemsize)
    per_w = n // n_workers
    n_chunks = per_w // w
    assert per_w * n_workers == n and n_chunks * w == per_w and n_chunks % 2 == 0

    @functools.partial(
        pl.kernel, out_type=jax.ShapeDtypeStruct((n, D), ys.dtype), mesh=mesh, compiler_params=params,
        scratch_types=[pltpu.VMEM((N_EXPERTS,), I32), pltpu.VMEM((w,), I32), pltpu.VMEM((w,), I32),
                       pltpu.VMEM((w,), I32), pltpu.VMEM((w, D), ys.dtype), pltpu.VMEM((w, D), ys.dtype),
                       pltpu.SemaphoreType.DMA, pltpu.SemaphoreType.DMA])
    def gather(ys_hbm, c_hbm, b_hbm, yk_hbm, base_v, code_v, i0_v, i1_v, rows0, rows1, sem0, sem1):
        wid = lax.axis_index("subcore") * info.num_cores + lax.axis_index("core")
        w0 = wid * per_w
        pltpu.sync_copy(b_hbm, base_v)
        idx = (i0_v, i1_v)
        rows = (rows0, rows1)
        sems = (sem0, sem1)

        def start(c, slot):
            pltpu.sync_copy(c_hbm.at[pl.ds(w0 + c * w, w)], code_v)
            _rows_from_codes(code_v, base_v, idx[slot], info.num_lanes)
            pltpu.async_copy(ys_hbm.at[idx[slot]], rows[slot], sems[slot])

        def finish(c, slot):
            pltpu.make_async_copy(ys_hbm.at[idx[slot]], rows[slot], sems[slot]).wait()
            pltpu.sync_copy(rows[slot], yk_hbm.at[pl.ds(w0 + c * w, w)])

        _two_slot_loop(n_chunks, start, finish)

    return gather(ys, codes, base)


def _schedule_kernel(cnt_ref, te_ref, tb_ref, base_ref, nt_ref):
    tm = EXPERT_TM

    def expert(e, t0):
        n = (cnt_ref[e] + tm - 1) // tm
        base_ref[e] = t0 * tm

        def tile(t, c):
            te_ref[t] = e
            tb_ref[t] = t
            return c

        lax.fori_loop(t0, t0 + n, tile, 0)
        return t0 + n

    nt = lax.fori_loop(0, N_EXPERTS, expert, 0)
    nt_ref[0] = nt
    last = te_ref[nt - 1]

    def idle(t, c):
        te_ref[t] = last
        tb_ref[t] = nt - 1
        return c

    lax.fori_loop(nt, te_ref.shape[0], idle, 0)


def _schedule(counts, max_tiles):
    smem = pl.BlockSpec(memory_space=pltpu.SMEM)
    return pl.pallas_call(
        _schedule_kernel,
        out_shape=[jax.ShapeDtypeStruct((max_tiles,), I32),
                   jax.ShapeDtypeStruct((max_tiles,), I32),
                   jax.ShapeDtypeStruct((N_EXPERTS,), I32),
                   jax.ShapeDtypeStruct((1,), I32)],
        in_specs=[smem],
        out_specs=[smem, smem, smem, smem],
        name="schedule",
    )(counts)


def _expert_kernel(te_ref, tb_ref, nt_ref, xs_ref, wg_ref, wu_ref, wd_ref, ys_ref):
    @pl.when(pl.program_id(0) < nt_ref[0])
    def _():
        xb = _unpack_bf16_pairs(xs_ref[...])
        g = jnp.dot(xb, wg_ref[0], preferred_element_type=F32)
        u = jnp.dot(xb, wu_ref[0], preferred_element_type=F32)
        act = (g * _sigmoid(g)) * u
        ys_ref[...] = jnp.dot(act.astype(BF16), wd_ref[0], preferred_element_type=F32)


def _experts(tile_e, tile_b, n_tiles, xs, wg, wu, wd, max_tiles):
    P = xs.shape[0]
    D, de = wg.shape[1:]
    tm = EXPERT_TM
    wmap = lambda j, te, tb, nt: (te[j], 0, 0)
    rmap = lambda j, te, tb, nt: (tb[j], 0)
    return pl.pallas_call(
        _expert_kernel,
        out_shape=jax.ShapeDtypeStruct((P, D), F32),
        grid_spec=pltpu.PrefetchScalarGridSpec(
            num_scalar_prefetch=3,
            grid=(max_tiles,),
            in_specs=[pl.BlockSpec((tm, xs.shape[1]), rmap),
                      pl.BlockSpec((1, D, de), wmap),
                      pl.BlockSpec((1, D, de), wmap),
                      pl.BlockSpec((1, de, D), wmap)],
            out_specs=pl.BlockSpec((tm, D), rmap)),
        compiler_params=_params("arbitrary"),
        name="experts",
    )(tile_e, tile_b, n_tiles, xs, wg, wu, wd)


def _combine_kernel(x1_ref, rf_ref, mod_ref, gf_ref, y0_ref, y1_ref, out_ref, *, final_norm):
    gate2 = mod_ref[0, 5:6, :]
    w = rf_ref[...]
    moe = w[:, 0:1] * y0_ref[...] + w[:, 1:2] * y1_ref[...]
    x2 = x1_ref[...] + gate2 * moe
    if final_norm:
        ms = jnp.mean(x2 * x2, axis=-1, keepdims=True)
        x2 = x2 * lax.rsqrt(ms + RMS_EPS) * gf_ref[...]
    out_ref[...] = x2


def _combine(x1, rf, mod3, gf, yk, seq, final_norm):
    T, D = x1.shape
    tc = COMBINE_TM
    per_b = seq // tc
    n_blk = T // tc
    return pl.pallas_call(
        functools.partial(_combine_kernel, final_norm=final_norm),
        out_shape=jax.ShapeDtypeStruct((T, D), F32),
        grid=(n_blk,),
        in_specs=[pl.BlockSpec((tc, D), lambda i: (i, 0)),
                  pl.BlockSpec((tc, LANES), lambda i: (i, 0)),
                  pl.BlockSpec((1, 6, D), lambda i: (i // per_b, 0, 0)),
                  pl.BlockSpec((1, D), lambda i: (0, 0)),
                  pl.BlockSpec((tc, D), lambda i: (i, 0)),
                  pl.BlockSpec((tc, D), lambda i: (n_blk + i, 0))],
        out_specs=pl.BlockSpec((tc, D), lambda i: (i, 0)),
        compiler_params=_params("arbitrary"),
        name="combine",
    )(x1, rf, mod3, gf, yk, yk)


def _layer(x2, c, seq, layer, w_ada, b_ada, g_norm1, w_in, b_if, conv_dw_w, conv_dw_b, conv_ln_g, conv_ln_b,
           w_conv_out, qk_conv_w, qk_conv_b, m_norm_g, w_m_out, w_out, g_norm2, w_rg, b_rg,
           w_re, b_re, w_e_gate, w_e_up, w_e_down):
    T, D = x2.shape
    B = T // seq
    dc = D // 2
    nif = 2 * M_HEADS

    if_lo = 6 * dc
    full = lambda w: (w, [(0, w.shape[2])])
    mod, (w_main, w_gates, w_out_b, w_conv_out_b, w_m_out_b) = _ada(
        c, w_ada, b_ada, layer,
        [(w_in, [(0, if_lo), (if_lo + nif, w_in.shape[2])]), full(w_out), full(w_conv_out), full(w_m_out)])
    mod3 = mod.reshape(B, 6, D)
    w_if = w_in[layer, :, if_lo:if_lo + nif]
    w_if_pad = jnp.pad(w_if, ((0, 0), (0, LANES - nif))).astype(BF16)
    w_ift = w_if.T.astype(BF16)
    (u, qk, v, o, sga, sgb, ifc, ifr), (wg_b, wu_b, wd_b) = _inproj(
        x2, mod3, g_norm1.reshape(1, D), w_main, w_gates, w_if_pad, w_ift, seq,
        riders=(w_e_gate, w_e_up, w_e_down))

    ya = _conv_branch(
        u.reshape(B, seq, dc), conv_dw_w, conv_dw_b.reshape(1, dc), conv_ln_g.reshape(1, dc),
        conv_ln_b.reshape(1, dc), w_conv_out_b)
    bifc = jnp.pad(b_if, (0, LANES - nif)).reshape(1, LANES)
    bifr = b_if.reshape(nif, 1)
    yb = _mlstm_branch(
        qk.reshape(B, seq, 2 * dc), v.reshape(B, seq, dc), o.reshape(B, seq, dc),
        ifc.reshape(B, seq, LANES), ifr, qk_conv_w, qk_conv_b.reshape(1, 2 * dc), bifc, bifr,
        m_norm_g.reshape(1, dc), w_m_out_b)

    n_r = N_GROUPS + N_EXPERTS
    w_r = jnp.pad(jnp.concatenate([w_rg, w_re], axis=1), ((0, 0), (0, LANES - n_r))).astype(BF16)
    b_r = jnp.pad(jnp.concatenate([b_rg, b_re]), (0, LANES - n_r)).reshape(1, LANES)
    x1, h2, ri, rf, cnt = _merge(x2, ya.reshape(T, D), yb.reshape(T, D), sga, sgb, mod3,
                                 g_norm2.reshape(1, D), w_out_b, w_r, b_r, seq)

    tm = EXPERT_TM
    max_tiles = (T * TOP_K) // tm + N_EXPERTS
    tile_e, tile_b, base, n_tiles = _schedule(cnt[0, :N_EXPERTS], max_tiles)

    code0 = ri[0]
    code1 = ri[1]
    xs = _sc_dispatch(h2, code0, code1, base, max_tiles * tm)
    ys = _experts(tile_e, tile_b, n_tiles, xs, wg_b, wu_b, wd_b, max_tiles)
    return x1, rf, mod3, ys, code0, code1, base


def kernel(x, c, w_ada, b_ada, g_norm1, w_in, b_if, conv_dw_w, conv_dw_b, conv_ln_g, conv_ln_b,
           w_conv_out, qk_conv_w, qk_conv_b, m_norm_g, w_m_out, w_out, g_norm2, w_rg, b_rg,
           w_re, b_re, w_e_gate, w_e_up, w_e_down, g_final):
    B, S, D = x.shape
    depth = w_ada.shape[0]
    x2 = x.reshape(B * S, D)
    for l in range(depth):
        x1, rf, mod3, ys, code0, code1, base = _layer(
            x2, c, S, l, w_ada[l], b_ada[l], g_norm1[l], w_in, b_if[l], conv_dw_w[l], conv_dw_b[l],
            conv_ln_g[l], conv_ln_b[l], w_conv_out, qk_conv_w[l], qk_conv_b[l], m_norm_g[l],
            w_m_out, w_out, g_norm2[l], w_rg[l], b_rg[l], w_re[l], b_re[l],
            w_e_gate[l], w_e_up[l], w_e_down[l])
        yk = _sc_collect(ys, jnp.concatenate([code0, code1]), base)
        x2 = _combine(x1, rf, mod3, g_final.reshape(1, D), yk, S, final_norm=l == depth - 1)
    return x2.reshape(B, S, D)
```

```python
import dataclasses
import functools

import jax
import jax.numpy as jnp
from jax import lax
from jax.experimental import pallas as pl
from jax.experimental.pallas import tpu as pltpu
from jax.experimental.pallas import tpu_sc as plsc

F32 = jnp.float32
BF16 = jnp.bfloat16
I32 = jnp.int32

M_HEADS = 4
CONV_WIDTH = 31
QK_CONV_WIDTH = 4
N_GROUPS = 4
E_PER_GROUP = 8
N_EXPERTS = N_GROUPS * E_PER_GROUP
TOP_K = 2
RMS_EPS = 1e-6
LN_EPS = 1e-5

LANES = 128
SUBLANES = 8
VMEM_LIMIT = 56 * 1024 * 1024

ADA_TN = 768
INPROJ_TM = 512
INPROJ_SUB = 256
CONV_TS = 512
CONV_HALO = 32
CONV_RC = 256
MLSTM_L = 128
MLSTM_SEQS = 8
MERGE_TM = 512
MERGE_SUB = 256
EXPERT_TM = 512
SC_WINDOW_BYTES = 128 * 1024
COMBINE_TM = 256
RANK_BITS = 16
RANK_RADIX = 1 << RANK_BITS
assert EXPERT_TM & (EXPERT_TM - 1) == 0


def _sigmoid(v):
    return 1.0 / (1.0 + jnp.exp(-v))


def _log_sigmoid(v):
    return -(jnp.maximum(-v, 0.0) + jnp.log1p(jnp.exp(-jnp.abs(v))))


def _pack_bf16_pairs(v):
    n = v.shape[1] // 2
    bits = lax.bitcast_convert_type(v.astype(BF16).astype(F32), jnp.uint32)
    word = bits[:, n:] | (bits[:, :n] >> 16)
    return lax.bitcast_convert_type(word, I32)


def _unpack_pairs_f32(w):
    bits = lax.bitcast_convert_type(w, jnp.uint32)
    lo = lax.bitcast_convert_type(bits << 16, F32)
    hi = lax.bitcast_convert_type(bits & jnp.uint32(0xFFFF0000), F32)
    return jnp.concatenate([lo, hi], axis=1)


def _unpack_bf16_pairs(w):
    return _unpack_pairs_f32(w).astype(BF16)


def _split_bf16(v):
    hi = v.astype(BF16)
    r1 = v - hi.astype(F32)
    mid = r1.astype(BF16)
    lo = (r1 - mid.astype(F32)).astype(BF16)
    return hi, mid, lo


def _params(*sem):
    return pltpu.CompilerParams(dimension_semantics=sem, vmem_limit_bytes=VMEM_LIMIT)


def _ada_kernel(*refs, cuts):
    c_ref, w_ref, b_ref = refs[:3]
    srcs = refs[3:3 + len(cuts)]
    o_ref = refs[3 + len(cuts)]
    dsts = iter(refs[4 + len(cuts):])
    c = c_ref[...]
    s = c * _sigmoid(c)
    o_ref[...] = jnp.dot(s, w_ref[...], preferred_element_type=F32,
                         precision=lax.Precision.HIGHEST) + b_ref[...]
    for src, ranges in zip(srcs, cuts):
        for lo, hi in ranges:
            next(dsts)[...] = src[:, lo:hi].astype(BF16)


def _ada(c, w_ada, b_ada, layer, riders):
    B, D = c.shape
    N = w_ada.shape[1]
    n_steps = N // ADA_TN
    slab = lambda j: (j, 0)
    outs = pl.pallas_call(
        functools.partial(_ada_kernel, cuts=tuple(tuple(r) for _, r in riders)),
        out_shape=[jax.ShapeDtypeStruct((B, N), F32)]
        + [jax.ShapeDtypeStruct((w.shape[1], hi - lo), BF16) for w, r in riders for lo, hi in r],
        grid=(n_steps,),
        in_specs=[pl.BlockSpec((B, D), lambda j: (0, 0)),
                  pl.BlockSpec((D, ADA_TN), lambda j: (0, j)),
                  pl.BlockSpec((1, ADA_TN), lambda j: (0, j))]
        + [pl.BlockSpec((None, w.shape[1] // n_steps, w.shape[2]), lambda j: (layer, j, 0))
           for w, _ in riders],
        out_specs=[pl.BlockSpec((B, ADA_TN), lambda j: (0, j))]
        + [pl.BlockSpec((w.shape[1] // n_steps, hi - lo), slab) for w, r in riders for lo, hi in r],
        compiler_params=_params("arbitrary"),
        name="ada",
    )(c, w_ada, b_ada.reshape(1, N), *[w for w, _ in riders])
    return outs[0], outs[1:]


def _inproj_kernel(*refs, n_riders):
    x_ref, mod_ref, g_ref, wm_ref, wgt_ref, wif_ref, wift_ref = refs[:7]
    rider_in = refs[7:7 + n_riders]
    u_ref, qk_ref, v_ref, o_ref, sga_ref, sgb_ref, ifc_ref, ifr_ref = refs[7 + n_riders:15 + n_riders]
    rider_out = refs[15 + n_riders:]

    for src, dst in zip(rider_in, rider_out):
        dst[...] = src[...].astype(BF16)

    shift = mod_ref[0, 0:1, :]
    scale = mod_ref[0, 1:2, :]
    dc = u_ref.shape[1]
    d = sga_ref.shape[1]
    ts = INPROJ_SUB
    subs = [pl.ds(r0, ts) for r0 in range(0, x_ref.shape[0], ts)]

    hbs = []
    for sl in subs:
        x = x_ref[sl, :]
        ms = jnp.mean(x * x, axis=-1, keepdims=True)
        h = x * lax.rsqrt(ms + RMS_EPS) * g_ref[...]
        h = h * (1.0 + scale) + shift
        hbs.append(h.astype(BF16))

    for sl, hb in zip(subs, hbs):
        def seg(lo, hi):
            return jnp.dot(hb, wm_ref[:, lo:hi], preferred_element_type=F32)

        u_ref[sl, :] = seg(0, dc) * _sigmoid(seg(dc, 2 * dc))
        qk_ref[sl, :] = seg(2 * dc, 4 * dc)
        v_ref[sl, :] = seg(4 * dc, 5 * dc).astype(BF16)
        o_ref[sl, :] = seg(5 * dc, 6 * dc)
        sga_ref[sl, :] = _sigmoid(jnp.dot(hb, wgt_ref[:, 0:d], preferred_element_type=F32)).astype(BF16)
        sgb_ref[sl, :] = _sigmoid(jnp.dot(hb, wgt_ref[:, d:2 * d], preferred_element_type=F32)).astype(BF16)
        ifc_ref[sl, :] = jnp.dot(hb, wif_ref[...], preferred_element_type=F32)
        ifr_ref[0, :, sl] = lax.dot_general(wift_ref[...], hb, (((1,), (1,)), ((), ())),
                                            preferred_element_type=F32)


def _inproj(x2, mod3, g1, w_main, w_gates, w_if, w_ift, seq, riders):
    T, D = x2.shape
    tm = INPROJ_TM
    dc = D // 2
    per_b = seq // tm
    row = lambda i: (i, 0)
    const = lambda i: (0, 0)
    slab = lambda i: (i, 0, 0)
    slabs = [r.reshape(T // tm, -1, r.shape[-1]) for r in riders]
    once = pl.Buffered(1)
    outs = pl.pallas_call(
        functools.partial(_inproj_kernel, n_riders=len(riders)),
        out_shape=[jax.ShapeDtypeStruct((T, dc), F32),
                   jax.ShapeDtypeStruct((T, 2 * dc), F32),
                   jax.ShapeDtypeStruct((T, dc), BF16),
                   jax.ShapeDtypeStruct((T, dc), F32),
                   jax.ShapeDtypeStruct((T, D), BF16),
                   jax.ShapeDtypeStruct((T, D), BF16),
                   jax.ShapeDtypeStruct((T, LANES), F32),
                   jax.ShapeDtypeStruct((T // seq, SUBLANES, seq), F32)]
        + [jax.ShapeDtypeStruct(s.shape, BF16) for s in slabs],
        grid=(T // tm,),
        in_specs=[pl.BlockSpec((tm, D), row),
                  pl.BlockSpec((1, 6, D), lambda i: (i // per_b, 0, 0)),
                  pl.BlockSpec((1, D), const),
                  pl.BlockSpec(w_main.shape, const, pipeline_mode=once),
                  pl.BlockSpec(w_gates.shape, const, pipeline_mode=once),
                  pl.BlockSpec(w_if.shape, const),
                  pl.BlockSpec(w_ift.shape, const)]
        + [pl.BlockSpec((1,) + s.shape[1:], slab) for s in slabs],
        out_specs=[pl.BlockSpec((tm, dc), row),
                   pl.BlockSpec((tm, 2 * dc), row),
                   pl.BlockSpec((tm, dc), row),
                   pl.BlockSpec((tm, dc), row),
                   pl.BlockSpec((tm, D), row),
                   pl.BlockSpec((tm, D), row),
                   pl.BlockSpec((tm, LANES), row),
                   pl.BlockSpec((1, SUBLANES, tm), lambda i: (i // per_b, 0, i % per_b))]
        + [pl.BlockSpec((1,) + s.shape[1:], slab) for s in slabs],
        compiler_params=_params("arbitrary"),
        name="inproj",
    )(x2, mod3, g1, w_main, w_gates, w_if, w_ift, *slabs)
    return outs[:8], [o.reshape(r.shape) for o, r in zip(outs[8:], riders)]


def _conv_kernel(u_ref, w_ref, b_ref, lg_ref, lb_ref, wo_ref, y_ref, ubuf, sbuf, cbuf):
    ts = u_ref.shape[1]
    halo = CONV_HALO

    @pl.when(pl.program_id(1) == 0)
    def _():
        ubuf[0:halo, :] = jnp.zeros((halo, ubuf.shape[1]), F32)

    ubuf[halo:halo + ts, :] = u_ref[0]
    ns = sbuf.shape[1]
    for r in range(1, SUBLANES):
        sbuf[r - 1] = ubuf[r:r + ns, :]
    off = halo - (CONV_WIDTH - 1)
    for r0 in range(0, ts, CONV_RC):
        acc = jnp.broadcast_to(b_ref[...], (CONV_RC, ubuf.shape[1]))
        for k in range(CONV_WIDTH):
            r = (off + k) % SUBLANES
            lo = off + k - r + r0
            win = ubuf[lo:lo + CONV_RC, :] if r == 0 else sbuf[r - 1, lo:lo + CONV_RC, :]
            acc = acc + w_ref[k:k + 1, :] * win
        cbuf[r0:r0 + CONV_RC, :] = acc
    ubuf[0:halo, :] = ubuf[ts:ts + halo, :]

    a = cbuf[...]
    mu = jnp.mean(a, axis=-1, keepdims=True)
    ac = a - mu
    var = jnp.mean(ac * ac, axis=-1, keepdims=True)
    z = ac * lax.rsqrt(var + LN_EPS) * lg_ref[...] + lb_ref[...]
    z = z * _sigmoid(z)
    y_ref[0] = jnp.dot(z.astype(BF16), wo_ref[...], preferred_element_type=F32).astype(BF16)


def _conv_branch(u3, w, b, lg, lb, wo):
    B, S, C = u3.shape
    D = wo.shape[1]
    ts = CONV_TS
    const = lambda bi, si: (0, 0)
    return pl.pallas_call(
        _conv_kernel,
        out_shape=jax.ShapeDtypeStruct((B, S, D), BF16),
        grid=(B, S // ts),
        in_specs=[pl.BlockSpec((1, ts, C), lambda bi, si: (bi, si, 0)),
                  pl.BlockSpec(w.shape, const),
                  pl.BlockSpec((1, C), const),
                  pl.BlockSpec((1, C), const),
                  pl.BlockSpec((1, C), const),
                  pl.BlockSpec(wo.shape, const)],
        out_specs=pl.BlockSpec((1, ts, D), lambda bi, si: (bi, si, 0)),
        scratch_shapes=[pltpu.VMEM((ts + CONV_HALO, C), F32),
                        pltpu.VMEM((SUBLANES - 1, ts + CONV_HALO - SUBLANES, C), F32),
                        pltpu.VMEM((ts, C), F32)],
        compiler_params=_params("arbitrary", "arbitrary"),
        name="conv",
    )(u3, w, b, lg, lb, wo)


def _mlstm_kernel(qk_ref, v_ref, o_ref, ifc_ref, ifr_ref, cw_ref, cb_ref, bifc_ref, bifr_ref, ng_ref,
                  wo_ref, y_ref, qkbuf, cn_ref, m_ref, hbuf):
    @pl.when(pl.program_id(1) == 0)
    def _():
        qkbuf[:, 0:SUBLANES, :] = jnp.zeros((qkbuf.shape[0], SUBLANES, qkbuf.shape[2]), F32)
        cn_ref[...] = jnp.zeros(cn_ref.shape, F32)
        m_ref[...] = jnp.zeros(m_ref.shape, F32)

    nb, L, mi = hbuf.shape
    dh = mi // M_HEADS
    halo = SUBLANES
    off = halo - (QK_CONV_WIDTH - 1)
    rows = lax.broadcasted_iota(I32, (L, L), 0)
    cols = lax.broadcasted_iota(I32, (L, L), 1)
    causal = cols <= rows
    lower = jnp.where(causal, 1.0, 0.0).astype(BF16)
    upper = jnp.where(rows <= cols, 1.0, 0.0).astype(BF16)
    lane = lax.broadcasted_iota(I32, (L, dh), 1)
    ones_col = jnp.where(lane == 0, 1.0, 0.0).astype(BF16)
    scale = dh ** -0.5

    seqs = []
    for b in range(nb):
        qkbuf[b, halo:halo + L, :] = qk_ref[b]
        y = jnp.broadcast_to(cb_ref[...], (L, qkbuf.shape[2]))
        for k in range(QK_CONV_WIDTH):
            y = y + cw_ref[k:k + 1, :] * qkbuf[b, off + k:off + k + L, :]
        y = y * _sigmoid(y)
        qkbuf[b, 0:halo, :] = qkbuf[b, L:L + halo, :]
        ifr = ifr_ref[b] + bifr_ref[...]
        ifc = ifc_ref[b] + bifc_ref[...]
        bcum_c = sum(jnp.dot(lower, p, preferred_element_type=F32) for p in _split_bf16(_log_sigmoid(ifc)))
        bcum_r = sum(jnp.dot(p, upper, preferred_element_type=F32) for p in _split_bf16(_log_sigmoid(ifr)))
        seqs.append((y, ifr, bcum_c, bcum_r))

    probs = [(b, hd) for b in range(nb) for hd in range(M_HEADS)]
    st = {}
    for p in probs:
        b, hd = p
        y, ifr, bcum_c, bcum_r = seqs[b]
        c0 = hd * dh
        qb = (y[:, c0:c0 + dh] * scale).astype(BF16)
        kt = y[:, mi + c0:mi + c0 + dh].T
        v = v_ref[b, :, c0:c0 + dh]
        bc = bcum_c[:, M_HEADS + hd:M_HEADS + hd + 1]
        br = bcum_r[M_HEADS + hd:M_HEADS + hd + 1, :]
        li = ifr[hd:hd + 1, :]
        m_prev = m_ref[b, hd, 0:1, 0:1]
        dmat = jnp.where(causal, bc - br + li, -jnp.inf)
        st[p] = dict(qb=qb, kt=kt, v=v, bc=bc, br=br, li=li, m_prev=m_prev, dmat=dmat)
    for p in probs:
        s = st[p]
        s["inter"] = s["bc"] + s["m_prev"]
        s["m_t"] = jnp.maximum(jnp.max(s["dmat"], axis=-1, keepdims=True), s["inter"])
    for p in probs:
        s = st[p]
        s["qk"] = jnp.dot(s["qb"], s["kt"].astype(BF16), preferred_element_type=F32)
    for p in probs:
        b, hd = p
        s = st[p]
        s["cn"] = cn_ref[b, hd]
        s["qcn"] = jnp.dot(s["qb"], s["cn"].astype(BF16), preferred_element_type=F32)
    for p in probs:
        s = st[p]
        s["wts"] = jnp.exp(s["dmat"] - s["m_t"])
        s["s_inter"] = jnp.exp(s["inter"] - s["m_t"])
    for p in probs:
        s = st[p]
        s["s_mat"] = s["qk"] * s["wts"]
    for p in probs:
        s = st[p]
        s["sv"] = jnp.dot(s["s_mat"].astype(BF16), s["v"], preferred_element_type=F32)
    for p in probs:
        s = st[p]
        s["rowsum"] = jnp.sum(s["s_mat"], axis=-1, keepdims=True)
    for p in probs:
        s = st[p]
        s["num"] = s["sv"] + s["s_inter"] * s["qcn"][:, 0:dh]
        s["den"] = s["rowsum"] + s["s_inter"] * s["qcn"][:, dh:dh + 1]
    for p in probs:
        b, hd = p
        s = st[p]
        b_last = s["br"][:, L - 1:L]
        a = b_last - s["br"] + s["li"]
        m_new = jnp.maximum(b_last + s["m_prev"], jnp.max(a, axis=-1, keepdims=True))
        wk = jnp.exp(a - m_new)
        sc = jnp.exp(b_last + s["m_prev"] - m_new)
        v_ext = jnp.concatenate([s["v"], ones_col], axis=1)
        cn_ref[b, hd] = sc * s["cn"] + jnp.dot((s["kt"] * wk).astype(BF16), v_ext, preferred_element_type=F32)
        m_ref[b, hd] = jnp.broadcast_to(m_new, m_ref.shape[2:])
    for p in probs:
        s = st[p]
        s["hh"] = s["num"] / jnp.maximum(jnp.abs(s["den"]), jnp.exp(-s["m_t"]))
        s["mu"] = jnp.mean(s["hh"], axis=-1, keepdims=True)
    for p in probs:
        s = st[p]
        s["hc"] = s["hh"] - s["mu"]
        s["var"] = jnp.mean(s["hc"] * s["hc"], axis=-1, keepdims=True)
    for p in probs:
        b, hd = p
        s = st[p]
        c0 = hd * dh
        hn = s["hc"] * lax.rsqrt(s["var"] + LN_EPS) * ng_ref[:, c0:c0 + dh]
        hbuf[b, :, c0:c0 + dh] = hn * _sigmoid(o_ref[b, :, c0:c0 + dh])
    for b in range(nb):
        y = jnp.dot(hbuf[b].astype(BF16), wo_ref[...], preferred_element_type=F32)
        y_ref[b] = y.astype(BF16)


def _mlstm_branch(qk3, v3, o3, ifc3, ifr3, cw, cb, bifc, bifr, ng, wo):
    B, S, C2 = qk3.shape
    mi = v3.shape[2]
    dh = mi // M_HEADS
    D = wo.shape[1]
    L = MLSTM_L
    nb = MLSTM_SEQS
    const = lambda bi, ci: (0, 0)
    tile = lambda bi, ci: (bi, ci, 0)
    return pl.pallas_call(
        _mlstm_kernel,
        out_shape=jax.ShapeDtypeStruct((B, S, D), BF16),
        grid=(B // nb, S // L),
        in_specs=[pl.BlockSpec((nb, L, C2), tile),
                  pl.BlockSpec((nb, L, mi), tile),
                  pl.BlockSpec((nb, L, mi), tile),
                  pl.BlockSpec((nb, L, LANES), tile),
                  pl.BlockSpec((nb, SUBLANES, L), lambda bi, ci: (bi, 0, ci)),
                  pl.BlockSpec(cw.shape, const),
                  pl.BlockSpec((1, C2), const),
                  pl.BlockSpec((1, LANES), const),
                  pl.BlockSpec((SUBLANES, 1), const),
                  pl.BlockSpec((1, mi), const),
                  pl.BlockSpec(wo.shape, const)],
        out_specs=pl.BlockSpec((nb, L, D), tile),
        scratch_shapes=[pltpu.VMEM((nb, L + SUBLANES, C2), F32),
                        pltpu.VMEM((nb, M_HEADS, dh, 2 * dh), F32),
                        pltpu.VMEM((nb, M_HEADS, SUBLANES, LANES), F32),
                        pltpu.VMEM((nb, L, mi), F32)],
        compiler_params=_params("arbitrary", "arbitrary"),
        name="mlstm",
    )(qk3, v3, o3, ifc3, ifr3, cw, cb, bifc, bifr, ng, wo)


def _merge_kernel(x_ref, ya_ref, yb_ref, sga_ref, sgb_ref, mod_ref, g2_ref, wo_ref, wr_ref, br_ref,
                  x1_ref, h2_ref, ri_ref, rf_ref, cnt_ref, run_ref):
    ts = MERGE_SUB
    subs = [pl.ds(r0, ts) for r0 in range(0, x_ref.shape[0], ts)]

    @pl.when(pl.program_id(0) == 0)
    def _():
        run_ref[...] = jnp.zeros(run_ref.shape, F32)

    gate1 = mod_ref[0, 2:3, :]
    shift2 = mod_ref[0, 3:4, :]
    scale2 = mod_ref[0, 4:5, :]
    lane = lax.broadcasted_iota(I32, (ts, LANES), 1).astype(F32)
    neg = -jnp.inf
    rows = lax.broadcasted_iota(I32, (ts, ts), 0)
    cols = lax.broadcasted_iota(I32, (ts, ts), 1)
    strict = jnp.where(cols < rows, 1.0, 0.0).astype(BF16)

    def first_argmax(vals):
        mx = jnp.max(vals, axis=-1, keepdims=True)
        idx = jnp.min(jnp.where(vals == mx, lane, float(LANES)), axis=-1, keepdims=True)
        return mx, idx

    h2s = []
    for sl in subs:
        merged = (sga_ref[sl, :].astype(F32) * ya_ref[sl, :].astype(F32)
                  + sgb_ref[sl, :].astype(F32) * yb_ref[sl, :].astype(F32))
        mix = jnp.dot(merged.astype(BF16), wo_ref[...], preferred_element_type=F32)
        x1 = x_ref[sl, :] + gate1 * mix
        x1_ref[sl, :] = x1
        ms = jnp.mean(x1 * x1, axis=-1, keepdims=True)
        h2 = x1 * lax.rsqrt(ms + RMS_EPS) * g2_ref[...]
        h2 = h2 * (1.0 + scale2) + shift2
        h2_ref[sl, :] = _pack_bf16_pairs(h2)
        h2s.append(h2.astype(BF16))

    run = run_ref[0:1, :]
    for sl, h2b in zip(subs, h2s):
        logits = jnp.dot(h2b, wr_ref[...], preferred_element_type=F32) + br_ref[...]
        lg = jnp.where(lane < N_GROUPS, logits, neg)
        gmax, gsel = first_argmax(lg)
        p_g = 1.0 / jnp.sum(jnp.exp(lg - gmax), axis=-1, keepdims=True)
        lo = N_GROUPS + gsel * E_PER_GROUP
        le = jnp.where((lane >= lo) & (lane < lo + E_PER_GROUP), logits, neg)
        l1, i1 = first_argmax(le)
        l2, i2 = first_argmax(jnp.where(lane == i1, neg, le))
        r = jnp.exp(l2 - l1)
        w1 = p_g / (1.0 + r)
        w2 = p_g * r / (1.0 + r)
        e1 = i1 - N_GROUPS
        e2 = i2 - N_GROUPS

        onehot = jnp.where((lane == e1) | (lane == e2), 1.0, 0.0)
        before = jnp.dot(strict, onehot.astype(BF16), preferred_element_type=F32) + run
        rank1 = jnp.sum(jnp.where(lane == e1, before, 0.0), axis=-1, keepdims=True)
        rank2 = jnp.sum(jnp.where(lane == e2, before, 0.0), axis=-1, keepdims=True)
        run = run + jnp.sum(onehot, axis=0, keepdims=True)

        codes = jnp.where(lane == 0, e1 * float(RANK_RADIX) + rank1,
                          jnp.where(lane == 1, e2 * float(RANK_RADIX) + rank2, 0.0))
        ri_ref[:, sl] = codes.T[0:SUBLANES, :].astype(I32)
        rf_ref[sl, :] = jnp.where(lane == 0, w1, jnp.where(lane == 1, w2, 0.0))
    run_ref[...] = jnp.broadcast_to(run, run_ref.shape)
    cnt_ref[...] = jnp.broadcast_to(run, cnt_ref.shape).astype(I32)


def _merge(x2, ya, yb, sga, sgb, mod3, g2, wo, wr, br, seq):
    T, D = x2.shape
    tm = MERGE_TM
    per_b = seq // tm
    row = lambda i: (i, 0)
    const = lambda i: (0, 0)
    return pl.pallas_call(
        _merge_kernel,
        out_shape=[jax.ShapeDtypeStruct((T, D), F32),
                   jax.ShapeDtypeStruct((T, D // 2), I32),
                   jax.ShapeDtypeStruct((SUBLANES, T), I32),
                   jax.ShapeDtypeStruct((T, LANES), F32),
                   jax.ShapeDtypeStruct((SUBLANES, LANES), I32)],
        grid=(T // tm,),
        in_specs=[pl.BlockSpec((tm, D), row),
                  pl.BlockSpec((tm, D), row),
                  pl.BlockSpec((tm, D), row),
                  pl.BlockSpec((tm, D), row),
                  pl.BlockSpec((tm, D), row),
                  pl.BlockSpec((1, 6, D), lambda i: (i // per_b, 0, 0)),
                  pl.BlockSpec((1, D), const),
                  pl.BlockSpec(wo.shape, const),
                  pl.BlockSpec(wr.shape, const),
                  pl.BlockSpec((1, LANES), const)],
        out_specs=[pl.BlockSpec((tm, D), row),
                   pl.BlockSpec((tm, D // 2), row),
                   pl.BlockSpec((SUBLANES, tm), lambda i: (0, i)),
                   pl.BlockSpec((tm, LANES), row),
                   pl.BlockSpec((SUBLANES, LANES), const)],
        scratch_shapes=[pltpu.VMEM((SUBLANES, LANES), F32)],
        compiler_params=_params("arbitrary"),
        name="merge",
    )(x2, ya, yb, sga, sgb, mod3, g2, wo, wr, br)


def _sc_workers():
    info = plsc.get_sparse_core_info()
    mesh = plsc.VectorSubcoreMesh(core_axis_name="core", subcore_axis_name="subcore")
    params = pltpu.CompilerParams()
    if "needs_layout_passes" in pltpu.CompilerParams.__dataclass_fields__:
        params = dataclasses.replace(params, needs_layout_passes=False)
    return info, mesh, params


def _rows_from_codes(code_v, base_v, idx_v, lanes):
    for j in range(code_v.shape[0] // lanes):
        c = code_v[pl.ds(j * lanes, lanes)]
        expert = lax.shift_right_logical(c, RANK_BITS)
        idx_v[pl.ds(j * lanes, lanes)] = plsc.load_gather(base_v, [expert]) + (c & (RANK_RADIX - 1))


def _two_slot_loop(n_chunks, start, finish):
    start(0, 0)

    @pl.loop(0, n_chunks, step=2)
    def _(c):
        start(c + 1, 1)
        finish(c, 0)

        @pl.when(c + 2 < n_chunks)
        def _():
            start(c + 2, 0)

        finish(c + 1, 1)


def _sc_dispatch(h2, code0, code1, base, n_rows):
    T, D = h2.shape
    info, mesh, params = _sc_workers()
    n_workers = info.num_cores * info.num_subcores
    w = SC_WINDOW_BYTES // (D * h2.dtype.itemsize)
    per_w = T // n_workers
    n_chunks = per_w // w
    assert per_w * n_workers == T and n_chunks * w == per_w and n_chunks % 2 == 0

    @functools.partial(
        pl.kernel, out_type=jax.ShapeDtypeStruct((n_rows, D), h2.dtype), mesh=mesh, compiler_params=params,
        scratch_types=[pltpu.VMEM((N_EXPERTS,), I32), pltpu.VMEM((w,), I32), pltpu.VMEM((w,), I32),
                       pltpu.VMEM((w,), I32), pltpu.VMEM((w, D), h2.dtype), pltpu.VMEM((w, D), h2.dtype),
                       pltpu.SemaphoreType.DMA, pltpu.SemaphoreType.DMA])
    def scatter(h_hbm, c0_hbm, c1_hbm, b_hbm, xs_hbm, base_v, code_v, i0_v, i1_v, rows0, rows1, sem0, sem1):
        wid = lax.axis_index("subcore") * info.num_cores + lax.axis_index("core")
        w0 = wid * per_w
        pltpu.sync_copy(b_hbm, base_v)
        rows = (rows0, rows1)
        sems = (sem0, sem1)

        def start(c, slot):
            pltpu.async_copy(h_hbm.at[pl.ds(w0 + c * w, w)], rows[slot], sems[slot])

        def finish(c, slot):
            pltpu.sync_copy(c0_hbm.at[pl.ds(w0 + c * w, w)], code_v)
            _rows_from_codes(code_v, base_v, i0_v, info.num_lanes)
            pltpu.sync_copy(c1_hbm.at[pl.ds(w0 + c * w, w)], code_v)
            _rows_from_codes(code_v, base_v, i1_v, info.num_lanes)
            pltpu.make_async_copy(h_hbm.at[pl.ds(w0 + c * w, w)], rows[slot], sems[slot]).wait()
            pltpu.sync_copy(rows[slot], xs_hbm.at[i0_v])
            pltpu.sync_copy(rows[slot], xs_hbm.at[i1_v])

        _two_slot_loop(n_chunks, start, finish)

    return scatter(h2, code0, code1, base)


def _sc_collect(ys, codes, base):
    n = codes.shape[0]
    D = ys.shape[1]
    info, mesh, params = _sc_workers()
    n_workers = info.num_cores * info.num_subcores
    w = SC_WINDOW_BYTES // (D * ys.dtype.itemsize)
    per_w = n // n_workers
    n_chunks = per_w // w
    assert per_w * n_workers == n and n_chunks * w == per_w and n_chunks % 2 == 0

    @functools.partial(
        pl.kernel, out_type=jax.ShapeDtypeStruct((n, D), ys.dtype), mesh=mesh, compiler_params=params,
        scratch_types=[pltpu.VMEM((N_EXPERTS,), I32), pltpu.VMEM((w,), I32), pltpu.VMEM((w,), I32),
                       pltpu.VMEM((w,), I32), pltpu.VMEM((w, D), ys.dtype), pltpu.VMEM((w, D), ys.dtype),
                       pltpu.SemaphoreType.DMA, pltpu.SemaphoreType.DMA])
    def gather(ys_hbm, c_hbm, b_hbm, yk_hbm, base_v, code_v, i0_v, i1_v, rows0, rows1, sem0, sem1):
        wid = lax.axis_index("subcore") * info.num_cores + lax.axis_index("core")
        w0 = wid * per_w
        pltpu.sync_copy(b_hbm, base_v)
        idx = (i0_v, i1_v)
        rows = (rows0, rows1)
        sems = (sem0, sem1)

        def start(c, slot):
            pltpu.sync_copy(c_hbm.at[pl.ds(w0 + c * w, w)], code_v)
            _rows_from_codes(code_v, base_v, idx[slot], info.num_lanes)
            pltpu.async_copy(ys_hbm.at[idx[slot]], rows[slot], sems[slot])

        def finish(c, slot):
            pltpu.make_async_copy(ys_hbm.at[idx[slot]], rows[slot], sems[slot]).wait()
            pltpu.sync_copy(rows[slot], yk_hbm.at[pl.ds(w0 + c * w, w)])

        _two_slot_loop(n_chunks, start, finish)

    return gather(ys, codes, base)


def _schedule_kernel(cnt_ref, te_ref, tb_ref, base_ref, nt_ref):
    tm = EXPERT_TM

    def expert(e, t0):
        n = (cnt_ref[e] + tm - 1) // tm
        base_ref[e] = t0 * tm

        def tile(t, c):
            te_ref[t] = e
            tb_ref[t] = t
            return c

        lax.fori_loop(t0, t0 + n, tile, 0)
        return t0 + n

    nt = lax.fori_loop(0, N_EXPERTS, expert, 0)
    nt_ref[0] = nt
    last = te_ref[nt - 1]

    def idle(t, c):
        te_ref[t] = last
        tb_ref[t] = nt - 1
        return c

    lax.fori_loop(nt, te_ref.shape[0], idle, 0)


def _schedule(counts, max_tiles):
    smem = pl.BlockSpec(memory_space=pltpu.SMEM)
    return pl.pallas_call(
        _schedule_kernel,
        out_shape=[jax.ShapeDtypeStruct((max_tiles,), I32),
                   jax.ShapeDtypeStruct((max_tiles,), I32),
                   jax.ShapeDtypeStruct((N_EXPERTS,), I32),
                   jax.ShapeDtypeStruct((1,), I32)],
        in_specs=[smem],
        out_specs=[smem, smem, smem, smem],
        name="schedule",
    )(counts)


def _expert_kernel(te_ref, tb_ref, nt_ref, xs_ref, wg_ref, wu_ref, wd_ref, ys_ref):
    @pl.when(pl.program_id(0) < nt_ref[0])
    def _():
        xb = _unpack_bf16_pairs(xs_ref[...])
        g = jnp.dot(xb, wg_ref[0], preferred_element_type=F32)
        u = jnp.dot(xb, wu_ref[0], preferred_element_type=F32)
        act = (g * _sigmoid(g)) * u
        ys_ref[...] = _pack_bf16_pairs(jnp.dot(act.astype(BF16), wd_ref[0], preferred_element_type=F32))


def _experts(tile_e, tile_b, n_tiles, xs, wg, wu, wd, max_tiles):
    P = xs.shape[0]
    D, de = wg.shape[1:]
    tm = EXPERT_TM
    wmap = lambda j, te, tb, nt: (te[j], 0, 0)
    rmap = lambda j, te, tb, nt: (tb[j], 0)
    return pl.pallas_call(
        _expert_kernel,
        out_shape=jax.ShapeDtypeStruct(xs.shape, I32),
        grid_spec=pltpu.PrefetchScalarGridSpec(
            num_scalar_prefetch=3,
            grid=(max_tiles,),
            in_specs=[pl.BlockSpec((tm, xs.shape[1]), rmap),
                      pl.BlockSpec((1, D, de), wmap),
                      pl.BlockSpec((1, D, de), wmap),
                      pl.BlockSpec((1, de, D), wmap)],
            out_specs=pl.BlockSpec((tm, xs.shape[1]), rmap)),
        compiler_params=_params("arbitrary"),
        name="experts",
    )(tile_e, tile_b, n_tiles, xs, wg, wu, wd)


def _combine_kernel(x1_ref, rf_ref, mod_ref, gf_ref, y0_ref, y1_ref, out_ref, *, final_norm):
    gate2 = mod_ref[0, 5:6, :]
    w = rf_ref[...]
    moe = w[:, 0:1] * _unpack_pairs_f32(y0_ref[...]) + w[:, 1:2] * _unpack_pairs_f32(y1_ref[...])
    x2 = x1_ref[...] + gate2 * moe
    if final_norm:
        ms = jnp.mean(x2 * x2, axis=-1, keepdims=True)
        x2 = x2 * lax.rsqrt(ms + RMS_EPS) * gf_ref[...]
    out_ref[...] = x2


def _combine(x1, rf, mod3, gf, yk, seq, final_norm):
    T, D = x1.shape
    tc = COMBINE_TM
    per_b = seq // tc
    n_blk = T // tc
    return pl.pallas_call(
        functools.partial(_combine_kernel, final_norm=final_norm),
        out_shape=jax.ShapeDtypeStruct((T, D), F32),
        grid=(n_blk,),
        in_specs=[pl.BlockSpec((tc, D), lambda i: (i, 0)),
                  pl.BlockSpec((tc, LANES), lambda i: (i, 0)),
                  pl.BlockSpec((1, 6, D), lambda i: (i // per_b, 0, 0)),
                  pl.BlockSpec((1, D), lambda i: (0, 0)),
                  pl.BlockSpec((tc, yk.shape[1]), lambda i: (i, 0)),
                  pl.BlockSpec((tc, yk.shape[1]), lambda i: (n_blk + i, 0))],
        out_specs=pl.BlockSpec((tc, D), lambda i: (i, 0)),
        compiler_params=_params("arbitrary"),
        name="combine",
    )(x1, rf, mod3, gf, yk, yk)


def _layer(x2, c, seq, layer, w_ada, b_ada, g_norm1, w_in, b_if, conv_dw_w, conv_dw_b, conv_ln_g, conv_ln_b,
           w_conv_out, qk_conv_w, qk_conv_b, m_norm_g, w_m_out, w_out, g_norm2, w_rg, b_rg,
           w_re, b_re, w_e_gate, w_e_up, w_e_down):
    T, D = x2.shape
    B = T // seq
    dc = D // 2
    nif = 2 * M_HEADS

    if_lo = 6 * dc
    full = lambda w: (w, [(0, w.shape[2])])
    mod, (w_main, w_gates, w_out_b, w_conv_out_b, w_m_out_b) = _ada(
        c, w_ada, b_ada, layer,
        [(w_in, [(0, if_lo), (if_lo + nif, w_in.shape[2])]), full(w_out), full(w_conv_out), full(w_m_out)])
    mod3 = mod.reshape(B, 6, D)
    w_if = w_in[layer, :, if_lo:if_lo + nif]
    w_if_pad = jnp.pad(w_if, ((0, 0), (0, LANES - nif))).astype(BF16)
    w_ift = w_if.T.astype(BF16)
    (u, qk, v, o, sga, sgb, ifc, ifr), (wg_b, wu_b, wd_b) = _inproj(
        x2, mod3, g_norm1.reshape(1, D), w_main, w_gates, w_if_pad, w_ift, seq,
        riders=(w_e_gate, w_e_up, w_e_down))

    ya = _conv_branch(
        u.reshape(B, seq, dc), conv_dw_w, conv_dw_b.reshape(1, dc), conv_ln_g.reshape(1, dc),
        conv_ln_b.reshape(1, dc), w_conv_out_b)
    bifc = jnp.pad(b_if, (0, LANES - nif)).reshape(1, LANES)
    bifr = b_if.reshape(nif, 1)
    yb = _mlstm_branch(
        qk.reshape(B, seq, 2 * dc), v.reshape(B, seq, dc), o.reshape(B, seq, dc),
        ifc.reshape(B, seq, LANES), ifr, qk_conv_w, qk_conv_b.reshape(1, 2 * dc), bifc, bifr,
        m_norm_g.reshape(1, dc), w_m_out_b)

    n_r = N_GROUPS + N_EXPERTS
    w_r = jnp.pad(jnp.concatenate([w_rg, w_re], axis=1), ((0, 0), (0, LANES - n_r))).astype(BF16)
    b_r = jnp.pad(jnp.concatenate([b_rg, b_re]), (0, LANES - n_r)).reshape(1, LANES)
    x1, h2, ri, rf, cnt = _merge(x2, ya.reshape(T, D), yb.reshape(T, D), sga, sgb, mod3,
                                 g_norm2.reshape(1, D), w_out_b, w_r, b_r, seq)

    tm = EXPERT_TM
    max_tiles = (T * TOP_K) // tm + N_EXPERTS
    tile_e, tile_b, base, n_tiles = _schedule(cnt[0, :N_EXPERTS], max_tiles)

    code0 = ri[0]
    code1 = ri[1]
    xs = _sc_dispatch(h2, code0, code1, base, max_tiles * tm)
    ys = _experts(tile_e, tile_b, n_tiles, xs, wg_b, wu_b, wd_b, max_tiles)
    return x1, rf, mod3, ys, code0, code1, base


def kernel(x, c, w_ada, b_ada, g_norm1, w_in, b_if, conv_dw_w, conv_dw_b, conv_ln_g, conv_ln_b,
           w_conv_out, qk_conv_w, qk_conv_b, m_norm_g, w_m_out, w_out, g_norm2, w_rg, b_rg,
           w_re, b_re, w_e_gate, w_e_up, w_e_down, g_final):
    B, S, D = x.shape
    depth = w_ada.shape[0]
    x2 = x.reshape(B * S, D)
    for l in range(depth):
        x1, rf, mod3, ys, code0, code1, base = _layer(
            x2, c, S, l, w_ada[l], b_ada[l], g_norm1[l], w_in, b_if[l], conv_dw_w[l], conv_dw_b[l],
            conv_ln_g[l], conv_ln_b[l], w_conv_out, qk_conv_w[l], qk_conv_b[l], m_norm_g[l],
            w_m_out, w_out, g_norm2[l], w_rg[l], b_rg[l], w_re[l], b_re[l],
            w_e_gate[l], w_e_up[l], w_e_down[l])
        yk = _sc_collect(ys, jnp.concatenate([code0, code1]), base)
        x2 = _combine(x1, rf, mod3, g_final.reshape(1, D), yk, S, final_norm=l == depth - 1)
    return x2.reshape(B, S, D)
```

```python
import dataclasses
import functools

import jax
import jax.numpy as jnp
from jax import lax
from jax.experimental import pallas as pl
from jax.experimental.pallas import tpu as pltpu
from jax.experimental.pallas import tpu_sc as plsc

F32 = jnp.float32
BF16 = jnp.bfloat16
I32 = jnp.int32

M_HEADS = 4
CONV_WIDTH = 31
QK_CONV_WIDTH = 4
N_GROUPS = 4
E_PER_GROUP = 8
N_EXPERTS = N_GROUPS * E_PER_GROUP
TOP_K = 2
RMS_EPS = 1e-6
LN_EPS = 1e-5

LANES = 128
SUBLANES = 8
VMEM_LIMIT = 56 * 1024 * 1024

ADA_TN = 768
INPROJ_TM = 512
INPROJ_SUB = 256
CONV_TS = 512
CONV_HALO = 32
CONV_RC = 256
MLSTM_L = 128
MLSTM_SEQS = 8
MERGE_TM = 512
MERGE_SUB = 256
EXPERT_TM = 512
SC_WINDOW_BYTES = 128 * 1024
COMBINE_TM = 1024
RANK_BITS = 16
RANK_RADIX = 1 << RANK_BITS
assert EXPERT_TM & (EXPERT_TM - 1) == 0


def _sigmoid(v):
    return 1.0 / (1.0 + jnp.exp(-v))


def _log_sigmoid(v):
    return -(jnp.maximum(-v, 0.0) + jnp.log1p(jnp.exp(-jnp.abs(v))))


def _pack_bf16_pairs(v):
    n = v.shape[1] // 2
    bits = lax.bitcast_convert_type(v.astype(BF16).astype(F32), jnp.uint32)
    word = bits[:, n:] | (bits[:, :n] >> 16)
    return lax.bitcast_convert_type(word, I32)


def _unpack_pairs_f32(w):
    bits = lax.bitcast_convert_type(w, jnp.uint32)
    lo = lax.bitcast_convert_type(bits << 16, F32)
    hi = lax.bitcast_convert_type(bits & jnp.uint32(0xFFFF0000), F32)
    return jnp.concatenate([lo, hi], axis=1)


def _unpack_bf16_pairs(w):
    return _unpack_pairs_f32(w).astype(BF16)


def _split_bf16(v):
    hi = v.astype(BF16)
    r1 = v - hi.astype(F32)
    mid = r1.astype(BF16)
    lo = (r1 - mid.astype(F32)).astype(BF16)
    return hi, mid, lo


def _params(*sem):
    return pltpu.CompilerParams(dimension_semantics=sem, vmem_limit_bytes=VMEM_LIMIT)


def _ada_kernel(*refs, cuts):
    c_ref, w_ref, b_ref = refs[:3]
    srcs = refs[3:3 + len(cuts)]
    o_ref = refs[3 + len(cuts)]
    dsts = iter(refs[4 + len(cuts):])
    c = c_ref[...]
    s = c * _sigmoid(c)
    o_ref[...] = jnp.dot(s, w_ref[...], preferred_element_type=F32,
                         precision=lax.Precision.HIGHEST) + b_ref[...]
    for src, ranges in zip(srcs, cuts):
        for lo, hi in ranges:
            next(dsts)[...] = src[:, lo:hi].astype(BF16)


def _ada(c, w_ada, b_ada, layer, riders):
    B, D = c.shape
    N = w_ada.shape[1]
    n_steps = N // ADA_TN
    slab = lambda j: (j, 0)
    outs = pl.pallas_call(
        functools.partial(_ada_kernel, cuts=tuple(tuple(r) for _, r in riders)),
        out_shape=[jax.ShapeDtypeStruct((B, N), F32)]
        + [jax.ShapeDtypeStruct((w.shape[1], hi - lo), BF16) for w, r in riders for lo, hi in r],
        grid=(n_steps,),
        in_specs=[pl.BlockSpec((B, D), lambda j: (0, 0)),
                  pl.BlockSpec((D, ADA_TN), lambda j: (0, j)),
                  pl.BlockSpec((1, ADA_TN), lambda j: (0, j))]
        + [pl.BlockSpec((None, w.shape[1] // n_steps, w.shape[2]), lambda j: (layer, j, 0))
           for w, _ in riders],
        out_specs=[pl.BlockSpec((B, ADA_TN), lambda j: (0, j))]
        + [pl.BlockSpec((w.shape[1] // n_steps, hi - lo), slab) for w, r in riders for lo, hi in r],
        compiler_params=_params("arbitrary"),
        name="ada",
    )(c, w_ada, b_ada.reshape(1, N), *[w for w, _ in riders])
    return outs[0], outs[1:]


def _inproj_kernel(*refs, n_riders):
    x_ref, mod_ref, g_ref, wm_ref, wgt_ref, wif_ref, wift_ref = refs[:7]
    rider_in = refs[7:7 + n_riders]
    u_ref, qk_ref, v_ref, o_ref, sga_ref, sgb_ref, ifc_ref, ifr_ref = refs[7 + n_riders:15 + n_riders]
    rider_out = refs[15 + n_riders:]

    for src, dst in zip(rider_in, rider_out):
        dst[...] = src[...].astype(BF16)

    shift = mod_ref[0, 0:1, :]
    scale = mod_ref[0, 1:2, :]
    dc = u_ref.shape[1]
    d = sga_ref.shape[1]
    ts = INPROJ_SUB
    subs = [pl.ds(r0, ts) for r0 in range(0, x_ref.shape[0], ts)]

    hbs = []
    for sl in subs:
        x = x_ref[sl, :]
        ms = jnp.mean(x * x, axis=-1, keepdims=True)
        h = x * lax.rsqrt(ms + RMS_EPS) * g_ref[...]
        h = h * (1.0 + scale) + shift
        hbs.append(h.astype(BF16))

    for sl, hb in zip(subs, hbs):
        def seg(lo, hi):
            return jnp.dot(hb, wm_ref[:, lo:hi], preferred_element_type=F32)

        u_ref[sl, :] = seg(0, dc) * _sigmoid(seg(dc, 2 * dc))
        qk_ref[sl, :] = seg(2 * dc, 4 * dc)
        v_ref[sl, :] = seg(4 * dc, 5 * dc).astype(BF16)
        o_ref[sl, :] = seg(5 * dc, 6 * dc)
        sga_ref[sl, :] = _sigmoid(jnp.dot(hb, wgt_ref[:, 0:d], preferred_element_type=F32)).astype(BF16)
        sgb_ref[sl, :] = _sigmoid(jnp.dot(hb, wgt_ref[:, d:2 * d], preferred_element_type=F32)).astype(BF16)
        ifc_ref[sl, :] = jnp.dot(hb, wif_ref[...], preferred_element_type=F32)
        ifr_ref[0, :, sl] = lax.dot_general(wift_ref[...], hb, (((1,), (1,)), ((), ())),
                                            preferred_element_type=F32)


def _inproj(x2, mod3, g1, w_main, w_gates, w_if, w_ift, seq, riders):
    T, D = x2.shape
    tm = INPROJ_TM
    dc = D // 2
    per_b = seq // tm
    row = lambda i: (i, 0)
    const = lambda i: (0, 0)
    slab = lambda i: (i, 0, 0)
    slabs = [r.reshape(T // tm, -1, r.shape[-1]) for r in riders]
    once = pl.Buffered(1)
    outs = pl.pallas_call(
        functools.partial(_inproj_kernel, n_riders=len(riders)),
        out_shape=[jax.ShapeDtypeStruct((T, dc), F32),
                   jax.ShapeDtypeStruct((T, 2 * dc), F32),
                   jax.ShapeDtypeStruct((T, dc), BF16),
                   jax.ShapeDtypeStruct((T, dc), F32),
                   jax.ShapeDtypeStruct((T, D), BF16),
                   jax.ShapeDtypeStruct((T, D), BF16),
                   jax.ShapeDtypeStruct((T, LANES), F32),
                   jax.ShapeDtypeStruct((T // seq, SUBLANES, seq), F32)]
        + [jax.ShapeDtypeStruct(s.shape, BF16) for s in slabs],
        grid=(T // tm,),
        in_specs=[pl.BlockSpec((tm, D), row),
                  pl.BlockSpec((1, 6, D), lambda i: (i // per_b, 0, 0)),
                  pl.BlockSpec((1, D), const),
                  pl.BlockSpec(w_main.shape, const, pipeline_mode=once),
                  pl.BlockSpec(w_gates.shape, const, pipeline_mode=once),
                  pl.BlockSpec(w_if.shape, const),
                  pl.BlockSpec(w_ift.shape, const)]
        + [pl.BlockSpec((1,) + s.shape[1:], slab) for s in slabs],
        out_specs=[pl.BlockSpec((tm, dc), row),
                   pl.BlockSpec((tm, 2 * dc), row),
                   pl.BlockSpec((tm, dc), row),
                   pl.BlockSpec((tm, dc), row),
                   pl.BlockSpec((tm, D), row),
                   pl.BlockSpec((tm, D), row),
                   pl.BlockSpec((tm, LANES), row),
                   pl.BlockSpec((1, SUBLANES, tm), lambda i: (i // per_b, 0, i % per_b))]
        + [pl.BlockSpec((1,) + s.shape[1:], slab) for s in slabs],
        compiler_params=_params("arbitrary"),
        name="inproj",
    )(x2, mod3, g1, w_main, w_gates, w_if, w_ift, *slabs)
    return outs[:8], [o.reshape(r.shape) for o, r in zip(outs[8:], riders)]


def _conv_kernel(u_ref, w_ref, b_ref, lg_ref, lb_ref, wo_ref, y_ref, ubuf, sbuf, cbuf):
    ts = u_ref.shape[1]
    halo = CONV_HALO

    @pl.when(pl.program_id(1) == 0)
    def _():
        ubuf[0:halo, :] = jnp.zeros((halo, ubuf.shape[1]), F32)

    ubuf[halo:halo + ts, :] = u_ref[0]
    ns = sbuf.shape[1]
    for r in range(1, SUBLANES):
        sbuf[r - 1] = ubuf[r:r + ns, :]
    off = halo - (CONV_WIDTH - 1)
    for r0 in range(0, ts, CONV_RC):
        acc = jnp.broadcast_to(b_ref[...], (CONV_RC, ubuf.shape[1]))
        for k in range(CONV_WIDTH):
            r = (off + k) % SUBLANES
            lo = off + k - r + r0
            win = ubuf[lo:lo + CONV_RC, :] if r == 0 else sbuf[r - 1, lo:lo + CONV_RC, :]
            acc = acc + w_ref[k:k + 1, :] * win
        cbuf[r0:r0 + CONV_RC, :] = acc
    ubuf[0:halo, :] = ubuf[ts:ts + halo, :]

    a = cbuf[...]
    mu = jnp.mean(a, axis=-1, keepdims=True)
    ac = a - mu
    var = jnp.mean(ac * ac, axis=-1, keepdims=True)
    z = ac * lax.rsqrt(var + LN_EPS) * lg_ref[...] + lb_ref[...]
    z = z * _sigmoid(z)
    y_ref[0] = jnp.dot(z.astype(BF16), wo_ref[...], preferred_element_type=F32).astype(BF16)


def _conv_branch(u3, w, b, lg, lb, wo):
    B, S, C = u3.shape
    D = wo.shape[1]
    ts = CONV_TS
    const = lambda bi, si: (0, 0)
    return pl.pallas_call(
        _conv_kernel,
        out_shape=jax.ShapeDtypeStruct((B, S, D), BF16),
        grid=(B, S // ts),
        in_specs=[pl.BlockSpec((1, ts, C), lambda bi, si: (bi, si, 0)),
                  pl.BlockSpec(w.shape, const),
                  pl.BlockSpec((1, C), const),
                  pl.BlockSpec((1, C), const),
                  pl.BlockSpec((1, C), const),
                  pl.BlockSpec(wo.shape, const)],
        out_specs=pl.BlockSpec((1, ts, D), lambda bi, si: (bi, si, 0)),
        scratch_shapes=[pltpu.VMEM((ts + CONV_HALO, C), F32),
                        pltpu.VMEM((SUBLANES - 1, ts + CONV_HALO - SUBLANES, C), F32),
                        pltpu.VMEM((ts, C), F32)],
        compiler_params=_params("arbitrary", "arbitrary"),
        name="conv",
    )(u3, w, b, lg, lb, wo)


def _mlstm_kernel(qk_ref, v_ref, o_ref, ifc_ref, ifr_ref, cw_ref, cb_ref, bifc_ref, bifr_ref, ng_ref,
                  wo_ref, y_ref, qkbuf, cn_ref, m_ref, hbuf):
    @pl.when(pl.program_id(1) == 0)
    def _():
        qkbuf[:, 0:SUBLANES, :] = jnp.zeros((qkbuf.shape[0], SUBLANES, qkbuf.shape[2]), F32)
        cn_ref[...] = jnp.zeros(cn_ref.shape, F32)
        m_ref[...] = jnp.zeros(m_ref.shape, F32)

    nb, L, mi = hbuf.shape
    dh = mi // M_HEADS
    halo = SUBLANES
    off = halo - (QK_CONV_WIDTH - 1)
    rows = lax.broadcasted_iota(I32, (L, L), 0)
    cols = lax.broadcasted_iota(I32, (L, L), 1)
    causal = cols <= rows
    lower = jnp.where(causal, 1.0, 0.0).astype(BF16)
    upper = jnp.where(rows <= cols, 1.0, 0.0).astype(BF16)
    lane = lax.broadcasted_iota(I32, (L, dh), 1)
    ones_col = jnp.where(lane == 0, 1.0, 0.0).astype(BF16)
    scale = dh ** -0.5

    seqs = []
    for b in range(nb):
        qkbuf[b, halo:halo + L, :] = qk_ref[b]
        y = jnp.broadcast_to(cb_ref[...], (L, qkbuf.shape[2]))
        for k in range(QK_CONV_WIDTH):
            y = y + cw_ref[k:k + 1, :] * qkbuf[b, off + k:off + k + L, :]
        y = y * _sigmoid(y)
        qkbuf[b, 0:halo, :] = qkbuf[b, L:L + halo, :]
        ifr = ifr_ref[b] + bifr_ref[...]
        ifc = ifc_ref[b] + bifc_ref[...]
        bcum_c = sum(jnp.dot(lower, p, preferred_element_type=F32) for p in _split_bf16(_log_sigmoid(ifc)))
        bcum_r = sum(jnp.dot(p, upper, preferred_element_type=F32) for p in _split_bf16(_log_sigmoid(ifr)))
        seqs.append((y, ifr, bcum_c, bcum_r))

    probs = [(b, hd) for b in range(nb) for hd in range(M_HEADS)]
    st = {}
    for p in probs:
        b, hd = p
        y, ifr, bcum_c, bcum_r = seqs[b]
        c0 = hd * dh
        qb = (y[:, c0:c0 + dh] * scale).astype(BF16)
        kt = y[:, mi + c0:mi + c0 + dh].T
        v = v_ref[b, :, c0:c0 + dh]
        bc = bcum_c[:, M_HEADS + hd:M_HEADS + hd + 1]
        br = bcum_r[M_HEADS + hd:M_HEADS + hd + 1, :]
        li = ifr[hd:hd + 1, :]
        m_prev = m_ref[b, hd, 0:1, 0:1]
        dmat = jnp.where(causal, bc - br + li, -jnp.inf)
        st[p] = dict(qb=qb, kt=kt, v=v, bc=bc, br=br, li=li, m_prev=m_prev, dmat=dmat)
    for p in probs:
        s = st[p]
        s["inter"] = s["bc"] + s["m_prev"]
        s["m_t"] = jnp.maximum(jnp.max(s["dmat"], axis=-1, keepdims=True), s["inter"])
    for p in probs:
        s = st[p]
        s["qk"] = jnp.dot(s["qb"], s["kt"].astype(BF16), preferred_element_type=F32)
    for p in probs:
        b, hd = p
        s = st[p]
        s["cn"] = cn_ref[b, hd]
        s["qcn"] = jnp.dot(s["qb"], s["cn"].astype(BF16), preferred_element_type=F32)
    for p in probs:
        s = st[p]
        s["wts"] = jnp.exp(s["dmat"] - s["m_t"])
        s["s_inter"] = jnp.exp(s["inter"] - s["m_t"])
    for p in probs:
        s = st[p]
        s["s_mat"] = s["qk"] * s["wts"]
    for p in probs:
        s = st[p]
        s["sv"] = jnp.dot(s["s_mat"].astype(BF16), s["v"], preferred_element_type=F32)
    for p in probs:
        s = st[p]
        s["rowsum"] = jnp.sum(s["s_mat"], axis=-1, keepdims=True)
    for p in probs:
        s = st[p]
        s["num"] = s["sv"] + s["s_inter"] * s["qcn"][:, 0:dh]
        s["den"] = s["rowsum"] + s["s_inter"] * s["qcn"][:, dh:dh + 1]
    for p in probs:
        b, hd = p
        s = st[p]
        b_last = s["br"][:, L - 1:L]
        a = b_last - s["br"] + s["li"]
        m_new = jnp.maximum(b_last + s["m_prev"], jnp.max(a, axis=-1, keepdims=True))
        wk = jnp.exp(a - m_new)
        sc = jnp.exp(b_last + s["m_prev"] - m_new)
        v_ext = jnp.concatenate([s["v"], ones_col], axis=1)
        cn_ref[b, hd] = sc * s["cn"] + jnp.dot((s["kt"] * wk).astype(BF16), v_ext, preferred_element_type=F32)
        m_ref[b, hd] = jnp.broadcast_to(m_new, m_ref.shape[2:])
    for p in probs:
        s = st[p]
        s["hh"] = s["num"] / jnp.maximum(jnp.abs(s["den"]), jnp.exp(-s["m_t"]))
        s["mu"] = jnp.mean(s["hh"], axis=-1, keepdims=True)
    for p in probs:
        s = st[p]
        s["hc"] = s["hh"] - s["mu"]
        s["var"] = jnp.mean(s["hc"] * s["hc"], axis=-1, keepdims=True)
    for p in probs:
        b, hd = p
        s = st[p]
        c0 = hd * dh
        hn = s["hc"] * lax.rsqrt(s["var"] + LN_EPS) * ng_ref[:, c0:c0 + dh]
        hbuf[b, :, c0:c0 + dh] = hn * _sigmoid(o_ref[b, :, c0:c0 + dh])
    for b in range(nb):
        y = jnp.dot(hbuf[b].astype(BF16), wo_ref[...], preferred_element_type=F32)
        y_ref[b] = y.astype(BF16)


def _mlstm_branch(qk3, v3, o3, ifc3, ifr3, cw, cb, bifc, bifr, ng, wo):
    B, S, C2 = qk3.shape
    mi = v3.shape[2]
    dh = mi // M_HEADS
    D = wo.shape[1]
    L = MLSTM_L
    nb = MLSTM_SEQS
    const = lambda bi, ci: (0, 0)
    tile = lambda bi, ci: (bi, ci, 0)
    return pl.pallas_call(
        _mlstm_kernel,
        out_shape=jax.ShapeDtypeStruct((B, S, D), BF16),
        grid=(B // nb, S // L),
        in_specs=[pl.BlockSpec((nb, L, C2), tile),
                  pl.BlockSpec((nb, L, mi), tile),
                  pl.BlockSpec((nb, L, mi), tile),
                  pl.BlockSpec((nb, L, LANES), tile),
                  pl.BlockSpec((nb, SUBLANES, L), lambda bi, ci: (bi, 0, ci)),
                  pl.BlockSpec(cw.shape, const),
                  pl.BlockSpec((1, C2), const),
                  pl.BlockSpec((1, LANES), const),
                  pl.BlockSpec((SUBLANES, 1), const),
                  pl.BlockSpec((1, mi), const),
                  pl.BlockSpec(wo.shape, const)],
        out_specs=pl.BlockSpec((nb, L, D), tile),
        scratch_shapes=[pltpu.VMEM((nb, L + SUBLANES, C2), F32),
                        pltpu.VMEM((nb, M_HEADS, dh, 2 * dh), F32),
                        pltpu.VMEM((nb, M_HEADS, SUBLANES, LANES), F32),
                        pltpu.VMEM((nb, L, mi), F32)],
        compiler_params=_params("arbitrary", "arbitrary"),
        name="mlstm",
    )(qk3, v3, o3, ifc3, ifr3, cw, cb, bifc, bifr, ng, wo)


def _merge_kernel(x_ref, ya_ref, yb_ref, sga_ref, sgb_ref, mod_ref, g2_ref, wo_ref, wr_ref, br_ref,
                  x1_ref, h2_ref, ri_ref, rf_ref, cnt_ref, run_ref):
    ts = MERGE_SUB
    subs = [pl.ds(r0, ts) for r0 in range(0, x_ref.shape[0], ts)]

    @pl.when(pl.program_id(0) == 0)
    def _():
        run_ref[...] = jnp.zeros(run_ref.shape, F32)

    gate1 = mod_ref[0, 2:3, :]
    shift2 = mod_ref[0, 3:4, :]
    scale2 = mod_ref[0, 4:5, :]
    lane = lax.broadcasted_iota(I32, (ts, LANES), 1).astype(F32)
    neg = -jnp.inf
    rows = lax.broadcasted_iota(I32, (ts, ts), 0)
    cols = lax.broadcasted_iota(I32, (ts, ts), 1)
    strict = jnp.where(cols < rows, 1.0, 0.0).astype(BF16)

    def first_argmax(vals):
        mx = jnp.max(vals, axis=-1, keepdims=True)
        idx = jnp.min(jnp.where(vals == mx, lane, float(LANES)), axis=-1, keepdims=True)
        return mx, idx

    h2s = []
    for sl in subs:
        merged = (sga_ref[sl, :].astype(F32) * ya_ref[sl, :].astype(F32)
                  + sgb_ref[sl, :].astype(F32) * yb_ref[sl, :].astype(F32))
        mix = jnp.dot(merged.astype(BF16), wo_ref[...], preferred_element_type=F32)
        x1 = x_ref[sl, :] + gate1 * mix
        x1_ref[sl, :] = x1
        ms = jnp.mean(x1 * x1, axis=-1, keepdims=True)
        h2 = x1 * lax.rsqrt(ms + RMS_EPS) * g2_ref[...]
        h2 = h2 * (1.0 + scale2) + shift2
        h2_ref[sl, :] = _pack_bf16_pairs(h2)
        h2s.append(h2.astype(BF16))

    run = run_ref[0:1, :]
    for sl, h2b in zip(subs, h2s):
        logits = jnp.dot(h2b, wr_ref[...], preferred_element_type=F32) + br_ref[...]
        lg = jnp.where(lane < N_GROUPS, logits, neg)
        gmax, gsel = first_argmax(lg)
        p_g = 1.0 / jnp.sum(jnp.exp(lg - gmax), axis=-1, keepdims=True)
        lo = N_GROUPS + gsel * E_PER_GROUP
        le = jnp.where((lane >= lo) & (lane < lo + E_PER_GROUP), logits, neg)
        l1, i1 = first_argmax(le)
        l2, i2 = first_argmax(jnp.where(lane == i1, neg, le))
        r = jnp.exp(l2 - l1)
        w1 = p_g / (1.0 + r)
        w2 = p_g * r / (1.0 + r)
        e1 = i1 - N_GROUPS
        e2 = i2 - N_GROUPS

        onehot = jnp.where((lane == e1) | (lane == e2), 1.0, 0.0)
        before = jnp.dot(strict, onehot.astype(BF16), preferred_element_type=F32) + run
        rank1 = jnp.sum(jnp.where(lane == e1, before, 0.0), axis=-1, keepdims=True)
        rank2 = jnp.sum(jnp.where(lane == e2, before, 0.0), axis=-1, keepdims=True)
        run = run + jnp.sum(onehot, axis=0, keepdims=True)

        codes = jnp.where(lane == 0, e1 * float(RANK_RADIX) + rank1,
                          jnp.where(lane == 1, e2 * float(RANK_RADIX) + rank2, 0.0))
        ri_ref[:, sl] = codes.T[0:SUBLANES, :].astype(I32)
        rf_ref[sl, :] = jnp.where(lane == 0, w1, jnp.where(lane == 1, w2, 0.0))
    run_ref[...] = jnp.broadcast_to(run, run_ref.shape)
    cnt_ref[...] = jnp.broadcast_to(run, cnt_ref.shape).astype(I32)


def _merge(x2, ya, yb, sga, sgb, mod3, g2, wo, wr, br, seq):
    T, D = x2.shape
    tm = MERGE_TM
    per_b = seq // tm
    row = lambda i: (i, 0)
    const = lambda i: (0, 0)
    return pl.pallas_call(
        _merge_kernel,
        out_shape=[jax.ShapeDtypeStruct((T, D), F32),
                   jax.ShapeDtypeStruct((T, D // 2), I32),
                   jax.ShapeDtypeStruct((SUBLANES, T), I32),
                   jax.ShapeDtypeStruct((T, LANES), F32),
                   jax.ShapeDtypeStruct((SUBLANES, LANES), I32)],
        grid=(T // tm,),
        in_specs=[pl.BlockSpec((tm, D), row),
                  pl.BlockSpec((tm, D), row),
                  pl.BlockSpec((tm, D), row),
                  pl.BlockSpec((tm, D), row),
                  pl.BlockSpec((tm, D), row),
                  pl.BlockSpec((1, 6, D), lambda i: (i // per_b, 0, 0)),
                  pl.BlockSpec((1, D), const),
                  pl.BlockSpec(wo.shape, const),
                  pl.BlockSpec(wr.shape, const),
                  pl.BlockSpec((1, LANES), const)],
        out_specs=[pl.BlockSpec((tm, D), row),
                   pl.BlockSpec((tm, D // 2), row),
                   pl.BlockSpec((SUBLANES, tm), lambda i: (0, i)),
                   pl.BlockSpec((tm, LANES), row),
                   pl.BlockSpec((SUBLANES, LANES), const)],
        scratch_shapes=[pltpu.VMEM((SUBLANES, LANES), F32)],
        compiler_params=_params("arbitrary"),
        name="merge",
    )(x2, ya, yb, sga, sgb, mod3, g2, wo, wr, br)


def _sc_workers():
    info = plsc.get_sparse_core_info()
    mesh = plsc.VectorSubcoreMesh(core_axis_name="core", subcore_axis_name="subcore")
    params = pltpu.CompilerParams()
    if "needs_layout_passes" in pltpu.CompilerParams.__dataclass_fields__:
        params = dataclasses.replace(params, needs_layout_passes=False)
    return info, mesh, params


def _rows_from_codes(code_v, base_v, idx_v, lanes):
    for j in range(code_v.shape[0] // lanes):
        c = code_v[pl.ds(j * lanes, lanes)]
        expert = lax.shift_right_logical(c, RANK_BITS)
        idx_v[pl.ds(j * lanes, lanes)] = plsc.load_gather(base_v, [expert]) + (c & (RANK_RADIX - 1))


def _two_slot_loop(n_chunks, start, finish):
    start(0, 0)

    @pl.loop(0, n_chunks, step=2)
    def _(c):
        start(c + 1, 1)
        finish(c, 0)

        @pl.when(c + 2 < n_chunks)
        def _():
            start(c + 2, 0)

        finish(c + 1, 1)


def _sc_dispatch(h2, code0, code1, base, n_rows):
    T, D = h2.shape
    info, mesh, params = _sc_workers()
    n_workers = info.num_cores * info.num_subcores
    w = SC_WINDOW_BYTES // (D * h2.dtype.itemsize)
    per_w = T // n_workers
    n_chunks = per_w // w
    assert per_w * n_workers == T and n_chunks * w == per_w and n_chunks % 2 == 0

    @functools.partial(
        pl.kernel, out_type=jax.ShapeDtypeStruct((n_rows, D), h2.dtype), mesh=mesh, compiler_params=params,
        scratch_types=[pltpu.VMEM((N_EXPERTS,), I32), pltpu.VMEM((w,), I32), pltpu.VMEM((w,), I32),
                       pltpu.VMEM((w,), I32), pltpu.VMEM((w, D), h2.dtype), pltpu.VMEM((w, D), h2.dtype),
                       pltpu.SemaphoreType.DMA, pltpu.SemaphoreType.DMA])
    def scatter(h_hbm, c0_hbm, c1_hbm, b_hbm, xs_hbm, base_v, code_v, i0_v, i1_v, rows0, rows1, sem0, sem1):
        wid = lax.axis_index("subcore") * info.num_cores + lax.axis_index("core")
        w0 = wid * per_w
        pltpu.sync_copy(b_hbm, base_v)
        rows = (rows0, rows1)
        sems = (sem0, sem1)

        def start(c, slot):
            pltpu.async_copy(h_hbm.at[pl.ds(w0 + c * w, w)], rows[slot], sems[slot])

        def finish(c, slot):
            pltpu.sync_copy(c0_hbm.at[pl.ds(w0 + c * w, w)], code_v)
            _rows_from_codes(code_v, base_v, i0_v, info.num_lanes)
            pltpu.sync_copy(c1_hbm.at[pl.ds(w0 + c * w, w)], code_v)
            _rows_from_codes(code_v, base_v, i1_v, info.num_lanes)
            pltpu.make_async_copy(h_hbm.at[pl.ds(w0 + c * w, w)], rows[slot], sems[slot]).wait()
            pltpu.sync_copy(rows[slot], xs_hbm.at[i0_v])
            pltpu.sync_copy(rows[slot], xs_hbm.at[i1_v])

        _two_slot_loop(n_chunks, start, finish)

    return scatter(h2, code0, code1, base)


def _sc_collect(ys, codes, base):
    n = codes.shape[0]
    D = ys.shape[1]
    info, mesh, params = _sc_workers()
    n_workers = info.num_cores * info.num_subcores
    w = SC_WINDOW_BYTES // (D * ys.dtype.itemsize)
    per_w = n // n_workers
    n_chunks = per_w // w
    assert per_w * n_workers == n and n_chunks * w == per_w and n_chunks % 2 == 0

    @functools.partial(
        pl.kernel, out_type=jax.ShapeDtypeStruct((n, D), ys.dtype), mesh=mesh, compiler_params=params,
        scratch_types=[pltpu.VMEM((N_EXPERTS,), I32), pltpu.VMEM((w,), I32), pltpu.VMEM((w,), I32),
                       pltpu.VMEM((w,), I32), pltpu.VMEM((w, D), ys.dtype), pltpu.VMEM((w, D), ys.dtype),
                       pltpu.SemaphoreType.DMA, pltpu.SemaphoreType.DMA])
    def gather(ys_hbm, c_hbm, b_hbm, yk_hbm, base_v, code_v, i0_v, i1_v, rows0, rows1, sem0, sem1):
        wid = lax.axis_index("subcore") * info.num_cores + lax.axis_index("core")
        w0 = wid * per_w
        pltpu.sync_copy(b_hbm, base_v)
        idx = (i0_v, i1_v)
        rows = (rows0, rows1)
        sems = (sem0, sem1)

        def start(c, slot):
            pltpu.sync_copy(c_hbm.at[pl.ds(w0 + c * w, w)], code_v)
            _rows_from_codes(code_v, base_v, idx[slot], info.num_lanes)
            pltpu.async_copy(ys_hbm.at[idx[slot]], rows[slot], sems[slot])

        def finish(c, slot):
            pltpu.make_async_copy(ys_hbm.at[idx[slot]], rows[slot], sems[slot]).wait()
            pltpu.sync_copy(rows[slot], yk_hbm.at[pl.ds(w0 + c * w, w)])

        _two_slot_loop(n_chunks, start, finish)

    return gather(ys, codes, base)


def _schedule_kernel(cnt_ref, te_ref, tb_ref, base_ref, nt_ref):
    tm = EXPERT_TM

    def expert(e, t0):
        n = (cnt_ref[e] + tm - 1) // tm
        base_ref[e] = t0 * tm

        def tile(t, c):
            te_ref[t] = e
            tb_ref[t] = t
            return c

        lax.fori_loop(t0, t0 + n, tile, 0)
        return t0 + n

    nt = lax.fori_loop(0, N_EXPERTS, expert, 0)
    nt_ref[0] = nt
    last = te_ref[nt - 1]

    def idle(t, c):
        te_ref[t] = last
        tb_ref[t] = nt - 1
        return c

    lax.fori_loop(nt, te_ref.shape[0], idle, 0)


def _schedule(counts, max_tiles):
    smem = pl.BlockSpec(memory_space=pltpu.SMEM)
    return pl.pallas_call(
        _schedule_kernel,
        out_shape=[jax.ShapeDtypeStruct((max_tiles,), I32),
                   jax.ShapeDtypeStruct((max_tiles,), I32),
                   jax.ShapeDtypeStruct((N_EXPERTS,), I32),
                   jax.ShapeDtypeStruct((1,), I32)],
        in_specs=[smem],
        out_specs=[smem, smem, smem, smem],
        name="schedule",
    )(counts)


def _expert_kernel(te_ref, tb_ref, nt_ref, xs_ref, wg_ref, wu_ref, wd_ref, ys_ref):
    @pl.when(pl.program_id(0) < nt_ref[0])
    def _():
        xb = _unpack_bf16_pairs(xs_ref[...])
        g = jnp.dot(xb, wg_ref[0], preferred_element_type=F32)
        u = jnp.dot(xb, wu_ref[0], preferred_element_type=F32)
        act = (g * _sigmoid(g)) * u
        ys_ref[...] = _pack_bf16_pairs(jnp.dot(act.astype(BF16), wd_ref[0], preferred_element_type=F32))


def _experts(tile_e, tile_b, n_tiles, xs, wg, wu, wd, max_tiles):
    P = xs.shape[0]
    D, de = wg.shape[1:]
    tm = EXPERT_TM
    wmap = lambda j, te, tb, nt: (te[j], 0, 0)
    rmap = lambda j, te, tb, nt: (tb[j], 0)
    return pl.pallas_call(
        _expert_kernel,
        out_shape=jax.ShapeDtypeStruct(xs.shape, I32),
        grid_spec=pltpu.PrefetchScalarGridSpec(
            num_scalar_prefetch=3,
            grid=(max_tiles,),
            in_specs=[pl.BlockSpec((tm, xs.shape[1]), rmap),
                      pl.BlockSpec((1, D, de), wmap),
                      pl.BlockSpec((1, D, de), wmap),
                      pl.BlockSpec((1, de, D), wmap)],
            out_specs=pl.BlockSpec((tm, xs.shape[1]), rmap)),
        compiler_params=_params("arbitrary"),
        name="experts",
    )(tile_e, tile_b, n_tiles, xs, wg, wu, wd)


def _combine_kernel(x1_ref, rf_ref, mod_ref, gf_ref, y0_ref, y1_ref, out_ref, *, final_norm):
    gate2 = mod_ref[0, 5:6, :]
    w = rf_ref[...]
    moe = w[:, 0:1] * _unpack_pairs_f32(y0_ref[...]) + w[:, 1:2] * _unpack_pairs_f32(y1_ref[...])
    x2 = x1_ref[...] + gate2 * moe
    if final_norm:
        ms = jnp.mean(x2 * x2, axis=-1, keepdims=True)
        x2 = x2 * lax.rsqrt(ms + RMS_EPS) * gf_ref[...]
    out_ref[...] = x2


def _combine(x1, rf, mod3, gf, yk, seq, final_norm):
    T, D = x1.shape
    tc = COMBINE_TM
    per_b = seq // tc
    n_blk = T // tc
    return pl.pallas_call(
        functools.partial(_combine_kernel, final_norm=final_norm),
        out_shape=jax.ShapeDtypeStruct((T, D), F32),
        grid=(n_blk,),
        in_specs=[pl.BlockSpec((tc, D), lambda i: (i, 0)),
                  pl.BlockSpec((tc, LANES), lambda i: (i, 0)),
                  pl.BlockSpec((1, 6, D), lambda i: (i // per_b, 0, 0)),
                  pl.BlockSpec((1, D), lambda i: (0, 0)),
                  pl.BlockSpec((tc, yk.shape[1]), lambda i: (i, 0)),
                  pl.BlockSpec((tc, yk.shape[1]), lambda i: (n_blk + i, 0))],
        out_specs=pl.BlockSpec((tc, D), lambda i: (i, 0)),
        compiler_params=_params("arbitrary"),
        name="combine",
    )(x1, rf, mod3, gf, yk, yk)


def _layer(x2, c, seq, layer, w_ada, b_ada, g_norm1, w_in, b_if, conv_dw_w, conv_dw_b, conv_ln_g, conv_ln_b,
           w_conv_out, qk_conv_w, qk_conv_b, m_norm_g, w_m_out, w_out, g_norm2, w_rg, b_rg,
           w_re, b_re, w_e_gate, w_e_up, w_e_down):
    T, D = x2.shape
    B = T // seq
    dc = D // 2
    nif = 2 * M_HEADS

    if_lo = 6 * dc
    w_main = w_in[layer, :, :if_lo].astype(BF16)
    w_gates = w_in[layer, :, if_lo + nif:].astype(BF16)
    full = lambda w: (w, [(0, w.shape[2])])
    mod, (w_out_b, w_conv_out_b, w_m_out_b) = _ada(
        c, w_ada, b_ada, layer, [full(w_out), full(w_conv_out), full(w_m_out)])
    mod3 = mod.reshape(B, 6, D)
    w_if = w_in[layer, :, if_lo:if_lo + nif]
    w_if_pad = jnp.pad(w_if, ((0, 0), (0, LANES - nif))).astype(BF16)
    w_ift = w_if.T.astype(BF16)
    (u, qk, v, o, sga, sgb, ifc, ifr), (wg_b, wu_b, wd_b) = _inproj(
        x2, mod3, g_norm1.reshape(1, D), w_main, w_gates, w_if_pad, w_ift, seq,
        riders=(w_e_gate, w_e_up, w_e_down))

    ya = _conv_branch(
        u.reshape(B, seq, dc), conv_dw_w, conv_dw_b.reshape(1, dc), conv_ln_g.reshape(1, dc),
        conv_ln_b.reshape(1, dc), w_conv_out_b)
    bifc = jnp.pad(b_if, (0, LANES - nif)).reshape(1, LANES)
    bifr = b_if.reshape(nif, 1)
    yb = _mlstm_branch(
        qk.reshape(B, seq, 2 * dc), v.reshape(B, seq, dc), o.reshape(B, seq, dc),
        ifc.reshape(B, seq, LANES), ifr, qk_conv_w, qk_conv_b.reshape(1, 2 * dc), bifc, bifr,
        m_norm_g.reshape(1, dc), w_m_out_b)

    n_r = N_GROUPS + N_EXPERTS
    w_r = jnp.pad(jnp.concatenate([w_rg, w_re], axis=1), ((0, 0), (0, LANES - n_r))).astype(BF16)
    b_r = jnp.pad(jnp.concatenate([b_rg, b_re]), (0, LANES - n_r)).reshape(1, LANES)
    x1, h2, ri, rf, cnt = _merge(x2, ya.reshape(T, D), yb.reshape(T, D), sga, sgb, mod3,
                                 g_norm2.reshape(1, D), w_out_b, w_r, b_r, seq)

    tm = EXPERT_TM
    max_tiles = (T * TOP_K) // tm + N_EXPERTS
    tile_e, tile_b, base, n_tiles = _schedule(cnt[0, :N_EXPERTS], max_tiles)

    code0 = ri[0]
    code1 = ri[1]
    xs = _sc_dispatch(h2, code0, code1, base, max_tiles * tm)
    ys = _experts(tile_e, tile_b, n_tiles, xs, wg_b, wu_b, wd_b, max_tiles)
    return x1, rf, mod3, ys, code0, code1, base


def kernel(x, c, w_ada, b_ada, g_norm1, w_in, b_if, conv_dw_w, conv_dw_b, conv_ln_g, conv_ln_b,
           w_conv_out, qk_conv_w, qk_conv_b, m_norm_g, w_m_out, w_out, g_norm2, w_rg, b_rg,
           w_re, b_re, w_e_gate, w_e_up, w_e_down, g_final):
    B, S, D = x.shape
    depth = w_ada.shape[0]
    x2 = x.reshape(B * S, D)
    for l in range(depth):
        x1, rf, mod3, ys, code0, code1, base = _layer(
            x2, c, S, l, w_ada[l], b_ada[l], g_norm1[l], w_in, b_if[l], conv_dw_w[l], conv_dw_b[l],
            conv_ln_g[l], conv_ln_b[l], w_conv_out, qk_conv_w[l], qk_conv_b[l], m_norm_g[l],
            w_m_out, w_out, g_norm2[l], w_rg[l], b_rg[l], w_re[l], b_re[l],
            w_e_gate[l], w_e_up[l], w_e_down[l])
        yk = _sc_collect(ys, jnp.concatenate([code0, code1]), base)
        x2 = _combine(x1, rf, mod3, g_final.reshape(1, D), yk, S, final_norm=l == depth - 1)
    return x2.reshape(B, S, D)
```

```python
import dataclasses
import functools

import jax
import jax.numpy as jnp
from jax import lax
from jax.experimental import pallas as pl
from jax.experimental.pallas import tpu as pltpu
from jax.experimental.pallas import tpu_sc as plsc

F32 = jnp.float32
BF16 = jnp.bfloat16
I32 = jnp.int32

M_HEADS = 4
CONV_WIDTH = 31
QK_CONV_WIDTH = 4
N_GROUPS = 4
E_PER_GROUP = 8
N_EXPERTS = N_GROUPS * E_PER_GROUP
TOP_K = 2
RMS_EPS = 1e-6
LN_EPS = 1e-5

LANES = 128
SUBLANES = 8
VMEM_LIMIT = 56 * 1024 * 1024

ADA_TN = 768
INPROJ_TM = 512
INPROJ_SUB = 256
CONV_TS = 512
CONV_HALO = 32
CONV_RC = 256
MLSTM_L = 128
MLSTM_SEQS = 8
MERGE_TM = 512
MERGE_SUB = 256
EXPERT_TM = 512
SC_WINDOW_BYTES = 128 * 1024
COMBINE_TM = 1024
RANK_BITS = 16
RANK_RADIX = 1 << RANK_BITS
assert EXPERT_TM & (EXPERT_TM - 1) == 0


def _sigmoid(v):
    return 1.0 / (1.0 + jnp.exp(-v))


def _log_sigmoid(v):
    return -(jnp.maximum(-v, 0.0) + jnp.log1p(jnp.exp(-jnp.abs(v))))


def _pack_bf16_pairs(v):
    n = v.shape[1] // 2
    bits = lax.bitcast_convert_type(v.astype(BF16).astype(F32), jnp.uint32)
    word = bits[:, n:] | (bits[:, :n] >> 16)
    return lax.bitcast_convert_type(word, I32)


def _unpack_pairs_f32(w):
    bits = lax.bitcast_convert_type(w, jnp.uint32)
    lo = lax.bitcast_convert_type(bits << 16, F32)
    hi = lax.bitcast_convert_type(bits & jnp.uint32(0xFFFF0000), F32)
    return jnp.concatenate([lo, hi], axis=1)


def _unpack_bf16_pairs(w):
    return _unpack_pairs_f32(w).astype(BF16)


def _split_bf16(v):
    hi = v.astype(BF16)
    r1 = v - hi.astype(F32)
    mid = r1.astype(BF16)
    lo = (r1 - mid.astype(F32)).astype(BF16)
    return hi, mid, lo


def _params(*sem):
    return pltpu.CompilerParams(dimension_semantics=sem, vmem_limit_bytes=VMEM_LIMIT)


def _ada_kernel(*refs, cuts):
    c_ref, w_ref, b_ref = refs[:3]
    srcs = refs[3:3 + len(cuts)]
    o_ref = refs[3 + len(cuts)]
    dsts = iter(refs[4 + len(cuts):])
    c = c_ref[...]
    s = c * _sigmoid(c)
    o_ref[...] = jnp.dot(s, w_ref[...], preferred_element_type=F32,
                         precision=lax.Precision.HIGHEST) + b_ref[...]
    for src, ranges in zip(srcs, cuts):
        for lo, hi in ranges:
            next(dsts)[...] = src[:, lo:hi].astype(BF16)


def _ada(c, w_ada, b_ada, layer, riders):
    B, D = c.shape
    N = w_ada.shape[1]
    n_steps = N // ADA_TN
    slab = lambda j: (j, 0)
    outs = pl.pallas_call(
        functools.partial(_ada_kernel, cuts=tuple(tuple(r) for _, r in riders)),
        out_shape=[jax.ShapeDtypeStruct((B, N), F32)]
        + [jax.ShapeDtypeStruct((w.shape[1], hi - lo), BF16) for w, r in riders for lo, hi in r],
        grid=(n_steps,),
        in_specs=[pl.BlockSpec((B, D), lambda j: (0, 0)),
                  pl.BlockSpec((D, ADA_TN), lambda j: (0, j)),
                  pl.BlockSpec((1, ADA_TN), lambda j: (0, j))]
        + [pl.BlockSpec((None, w.shape[1] // n_steps, w.shape[2]), lambda j: (layer, j, 0))
           for w, _ in riders],
        out_specs=[pl.BlockSpec((B, ADA_TN), lambda j: (0, j))]
        + [pl.BlockSpec((w.shape[1] // n_steps, hi - lo), slab) for w, r in riders for lo, hi in r],
        compiler_params=_params("arbitrary"),
        name="ada",
    )(c, w_ada, b_ada.reshape(1, N), *[w for w, _ in riders])
    return outs[0], outs[1:]


def _inproj_kernel(*refs, n_riders):
    x_ref, mod_ref, g_ref, wm_ref, wgt_ref, wif_ref, wift_ref = refs[:7]
    rider_in = refs[7:7 + n_riders]
    u_ref, qk_ref, v_ref, o_ref, sga_ref, sgb_ref, ifc_ref, ifr_ref = refs[7 + n_riders:15 + n_riders]
    rider_out = refs[15 + n_riders:]

    for src, dst in zip(rider_in, rider_out):
        dst[...] = src[...].astype(BF16)

    shift = mod_ref[0, 0:1, :]
    scale = mod_ref[0, 1:2, :]
    dc = u_ref.shape[1]
    d = sga_ref.shape[1]
    ts = INPROJ_SUB
    subs = [pl.ds(r0, ts) for r0 in range(0, x_ref.shape[0], ts)]

    hbs = []
    for sl in subs:
        x = x_ref[sl, :]
        ms = jnp.mean(x * x, axis=-1, keepdims=True)
        h = x * lax.rsqrt(ms + RMS_EPS) * g_ref[...]
        h = h * (1.0 + scale) + shift
        hbs.append(h.astype(BF16))

    def times_t(a, wt):
        return lax.dot_general(a, wt, (((1,), (1,)), ((), ())), preferred_element_type=F32)

    for sl, hb in zip(subs, hbs):
        def seg(lo, hi):
            return times_t(hb, wm_ref[lo:hi, :])

        u_ref[sl, :] = seg(0, dc) * _sigmoid(seg(dc, 2 * dc))
        qk_ref[sl, :] = seg(2 * dc, 4 * dc)
        v_ref[sl, :] = seg(4 * dc, 5 * dc).astype(BF16)
        o_ref[sl, :] = seg(5 * dc, 6 * dc)
        sga_ref[sl, :] = _sigmoid(times_t(hb, wgt_ref[0:d, :])).astype(BF16)
        sgb_ref[sl, :] = _sigmoid(times_t(hb, wgt_ref[d:2 * d, :])).astype(BF16)
        ifc_ref[sl, :] = jnp.dot(hb, wif_ref[...], preferred_element_type=F32)
        ifr_ref[0, :, sl] = lax.dot_general(wift_ref[...], hb, (((1,), (1,)), ((), ())),
                                            preferred_element_type=F32)


def _inproj(x2, mod3, g1, w_main, w_gates, w_if, w_ift, seq, riders):
    T, D = x2.shape
    tm = INPROJ_TM
    dc = D // 2
    per_b = seq // tm
    row = lambda i: (i, 0)
    const = lambda i: (0, 0)
    slab = lambda i: (i, 0, 0)
    slabs = [r.reshape(T // tm, -1, r.shape[-1]) for r in riders]
    once = pl.Buffered(1)
    outs = pl.pallas_call(
        functools.partial(_inproj_kernel, n_riders=len(riders)),
        out_shape=[jax.ShapeDtypeStruct((T, dc), F32),
                   jax.ShapeDtypeStruct((T, 2 * dc), F32),
                   jax.ShapeDtypeStruct((T, dc), BF16),
                   jax.ShapeDtypeStruct((T, dc), F32),
                   jax.ShapeDtypeStruct((T, D), BF16),
                   jax.ShapeDtypeStruct((T, D), BF16),
                   jax.ShapeDtypeStruct((T, LANES), F32),
                   jax.ShapeDtypeStruct((T // seq, SUBLANES, seq), F32)]
        + [jax.ShapeDtypeStruct(s.shape, BF16) for s in slabs],
        grid=(T // tm,),
        in_specs=[pl.BlockSpec((tm, D), row),
                  pl.BlockSpec((1, 6, D), lambda i: (i // per_b, 0, 0)),
                  pl.BlockSpec((1, D), const),
                  pl.BlockSpec(w_main.shape, const, pipeline_mode=once),
                  pl.BlockSpec(w_gates.shape, const, pipeline_mode=once),
                  pl.BlockSpec(w_if.shape, const),
                  pl.BlockSpec(w_ift.shape, const)]
        + [pl.BlockSpec((1,) + s.shape[1:], slab) for s in slabs],
        out_specs=[pl.BlockSpec((tm, dc), row),
                   pl.BlockSpec((tm, 2 * dc), row),
                   pl.BlockSpec((tm, dc), row),
                   pl.BlockSpec((tm, dc), row),
                   pl.BlockSpec((tm, D), row),
                   pl.BlockSpec((tm, D), row),
                   pl.BlockSpec((tm, LANES), row),
                   pl.BlockSpec((1, SUBLANES, tm), lambda i: (i // per_b, 0, i % per_b))]
        + [pl.BlockSpec((1,) + s.shape[1:], slab) for s in slabs],
        compiler_params=_params("arbitrary"),
        name="inproj",
    )(x2, mod3, g1, w_main, w_gates, w_if, w_ift, *slabs)
    return outs[:8], [o.reshape(r.shape) for o, r in zip(outs[8:], riders)]


def _conv_kernel(u_ref, w_ref, b_ref, lg_ref, lb_ref, wo_ref, y_ref, ubuf, sbuf, cbuf):
    ts = u_ref.shape[1]
    halo = CONV_HALO

    @pl.when(pl.program_id(1) == 0)
    def _():
        ubuf[0:halo, :] = jnp.zeros((halo, ubuf.shape[1]), F32)

    ubuf[halo:halo + ts, :] = u_ref[0]
    ns = sbuf.shape[1]
    for r in range(1, SUBLANES):
        sbuf[r - 1] = ubuf[r:r + ns, :]
    off = halo - (CONV_WIDTH - 1)
    for r0 in range(0, ts, CONV_RC):
        acc = jnp.broadcast_to(b_ref[...], (CONV_RC, ubuf.shape[1]))
        for k in range(CONV_WIDTH):
            r = (off + k) % SUBLANES
            lo = off + k - r + r0
            win = ubuf[lo:lo + CONV_RC, :] if r == 0 else sbuf[r - 1, lo:lo + CONV_RC, :]
            acc = acc + w_ref[k:k + 1, :] * win
        cbuf[r0:r0 + CONV_RC, :] = acc
    ubuf[0:halo, :] = ubuf[ts:ts + halo, :]

    a = cbuf[...]
    mu = jnp.mean(a, axis=-1, keepdims=True)
    ac = a - mu
    var = jnp.mean(ac * ac, axis=-1, keepdims=True)
    z = ac * lax.rsqrt(var + LN_EPS) * lg_ref[...] + lb_ref[...]
    z = z * _sigmoid(z)
    y_ref[0] = jnp.dot(z.astype(BF16), wo_ref[...], preferred_element_type=F32).astype(BF16)


def _conv_branch(u3, w, b, lg, lb, wo):
    B, S, C = u3.shape
    D = wo.shape[1]
    ts = CONV_TS
    const = lambda bi, si: (0, 0)
    return pl.pallas_call(
        _conv_kernel,
        out_shape=jax.ShapeDtypeStruct((B, S, D), BF16),
        grid=(B, S // ts),
        in_specs=[pl.BlockSpec((1, ts, C), lambda bi, si: (bi, si, 0)),
                  pl.BlockSpec(w.shape, const),
                  pl.BlockSpec((1, C), const),
                  pl.BlockSpec((1, C), const),
                  pl.BlockSpec((1, C), const),
                  pl.BlockSpec(wo.shape, const)],
        out_specs=pl.BlockSpec((1, ts, D), lambda bi, si: (bi, si, 0)),
        scratch_shapes=[pltpu.VMEM((ts + CONV_HALO, C), F32),
                        pltpu.VMEM((SUBLANES - 1, ts + CONV_HALO - SUBLANES, C), F32),
                        pltpu.VMEM((ts, C), F32)],
        compiler_params=_params("arbitrary", "arbitrary"),
        name="conv",
    )(u3, w, b, lg, lb, wo)


def _mlstm_kernel(qk_ref, v_ref, o_ref, ifc_ref, ifr_ref, cw_ref, cb_ref, bifc_ref, bifr_ref, ng_ref,
                  wo_ref, y_ref, qkbuf, cn_ref, m_ref, hbuf):
    @pl.when(pl.program_id(1) == 0)
    def _():
        qkbuf[:, 0:SUBLANES, :] = jnp.zeros((qkbuf.shape[0], SUBLANES, qkbuf.shape[2]), F32)
        cn_ref[...] = jnp.zeros(cn_ref.shape, F32)
        m_ref[...] = jnp.zeros(m_ref.shape, F32)

    nb, L, mi = hbuf.shape
    dh = mi // M_HEADS
    halo = SUBLANES
    off = halo - (QK_CONV_WIDTH - 1)
    rows = lax.broadcasted_iota(I32, (L, L), 0)
    cols = lax.broadcasted_iota(I32, (L, L), 1)
    causal = cols <= rows
    lower = jnp.where(causal, 1.0, 0.0).astype(BF16)
    upper = jnp.where(rows <= cols, 1.0, 0.0).astype(BF16)
    lane = lax.broadcasted_iota(I32, (L, dh), 1)
    ones_col = jnp.where(lane == 0, 1.0, 0.0).astype(BF16)
    scale = dh ** -0.5

    seqs = []
    for b in range(nb):
        qkbuf[b, halo:halo + L, :] = qk_ref[b]
        y = jnp.broadcast_to(cb_ref[...], (L, qkbuf.shape[2]))
        for k in range(QK_CONV_WIDTH):
            y = y + cw_ref[k:k + 1, :] * qkbuf[b, off + k:off + k + L, :]
        y = y * _sigmoid(y)
        qkbuf[b, 0:halo, :] = qkbuf[b, L:L + halo, :]
        ifr = ifr_ref[b] + bifr_ref[...]
        ifc = ifc_ref[b] + bifc_ref[...]
        bcum_c = sum(jnp.dot(lower, p, preferred_element_type=F32) for p in _split_bf16(_log_sigmoid(ifc)))
        bcum_r = sum(jnp.dot(p, upper, preferred_element_type=F32) for p in _split_bf16(_log_sigmoid(ifr)))
        seqs.append((y, ifr, bcum_c, bcum_r))

    probs = [(b, hd) for b in range(nb) for hd in range(M_HEADS)]
    st = {}
    for p in probs:
        b, hd = p
        y, ifr, bcum_c, bcum_r = seqs[b]
        c0 = hd * dh
        qb = (y[:, c0:c0 + dh] * scale).astype(BF16)
        kt = y[:, mi + c0:mi + c0 + dh].T
        v = v_ref[b, :, c0:c0 + dh]
        bc = bcum_c[:, M_HEADS + hd:M_HEADS + hd + 1]
        br = bcum_r[M_HEADS + hd:M_HEADS + hd + 1, :]
        li = ifr[hd:hd + 1, :]
        m_prev = m_ref[b, hd, 0:1, 0:1]
        dmat = jnp.where(causal, bc - br + li, -jnp.inf)
        st[p] = dict(qb=qb, kt=kt, v=v, bc=bc, br=br, li=li, m_prev=m_prev, dmat=dmat)
    for p in probs:
        s = st[p]
        s["inter"] = s["bc"] + s["m_prev"]
        s["m_t"] = jnp.maximum(jnp.max(s["dmat"], axis=-1, keepdims=True), s["inter"])
    for p in probs:
        s = st[p]
        s["qk"] = jnp.dot(s["qb"], s["kt"].astype(BF16), preferred_element_type=F32)
    for p in probs:
        b, hd = p
        s = st[p]
        s["cn"] = cn_ref[b, hd]
        s["qcn"] = jnp.dot(s["qb"], s["cn"].astype(BF16), preferred_element_type=F32)
    for p in probs:
        s = st[p]
        s["wts"] = jnp.exp(s["dmat"] - s["m_t"])
        s["s_inter"] = jnp.exp(s["inter"] - s["m_t"])
    for p in probs:
        s = st[p]
        s["s_mat"] = s["qk"] * s["wts"]
    for p in probs:
        s = st[p]
        s["sv"] = jnp.dot(s["s_mat"].astype(BF16), s["v"], preferred_element_type=F32)
    for p in probs:
        s = st[p]
        s["rowsum"] = jnp.sum(s["s_mat"], axis=-1, keepdims=True)
    for p in probs:
        s = st[p]
        s["num"] = s["sv"] + s["s_inter"] * s["qcn"][:, 0:dh]
        s["den"] = s["rowsum"] + s["s_inter"] * s["qcn"][:, dh:dh + 1]
    for p in probs:
        b, hd = p
        s = st[p]
        b_last = s["br"][:, L - 1:L]
        a = b_last - s["br"] + s["li"]
        m_new = jnp.maximum(b_last + s["m_prev"], jnp.max(a, axis=-1, keepdims=True))
        wk = jnp.exp(a - m_new)
        sc = jnp.exp(b_last + s["m_prev"] - m_new)
        v_ext = jnp.concatenate([s["v"], ones_col], axis=1)
        cn_ref[b, hd] = sc * s["cn"] + jnp.dot((s["kt"] * wk).astype(BF16), v_ext, preferred_element_type=F32)
        m_ref[b, hd] = jnp.broadcast_to(m_new, m_ref.shape[2:])
    for p in probs:
        s = st[p]
        s["hh"] = s["num"] / jnp.maximum(jnp.abs(s["den"]), jnp.exp(-s["m_t"]))
        s["mu"] = jnp.mean(s["hh"], axis=-1, keepdims=True)
    for p in probs:
        s = st[p]
        s["hc"] = s["hh"] - s["mu"]
        s["var"] = jnp.mean(s["hc"] * s["hc"], axis=-1, keepdims=True)
    for p in probs:
        b, hd = p
        s = st[p]
        c0 = hd * dh
        hn = s["hc"] * lax.rsqrt(s["var"] + LN_EPS) * ng_ref[:, c0:c0 + dh]
        hbuf[b, :, c0:c0 + dh] = hn * _sigmoid(o_ref[b, :, c0:c0 + dh])
    for b in range(nb):
        y = jnp.dot(hbuf[b].astype(BF16), wo_ref[...], preferred_element_type=F32)
        y_ref[b] = y.astype(BF16)


def _mlstm_branch(qk3, v3, o3, ifc3, ifr3, cw, cb, bifc, bifr, ng, wo):
    B, S, C2 = qk3.shape
    mi = v3.shape[2]
    dh = mi // M_HEADS
    D = wo.shape[1]
    L = MLSTM_L
    nb = MLSTM_SEQS
    const = lambda bi, ci: (0, 0)
    tile = lambda bi, ci: (bi, ci, 0)
    return pl.pallas_call(
        _mlstm_kernel,
        out_shape=jax.ShapeDtypeStruct((B, S, D), BF16),
        grid=(B // nb, S // L),
        in_specs=[pl.BlockSpec((nb, L, C2), tile),
                  pl.BlockSpec((nb, L, mi), tile),
                  pl.BlockSpec((nb, L, mi), tile),
                  pl.BlockSpec((nb, L, LANES), tile),
                  pl.BlockSpec((nb, SUBLANES, L), lambda bi, ci: (bi, 0, ci)),
                  pl.BlockSpec(cw.shape, const),
                  pl.BlockSpec((1, C2), const),
                  pl.BlockSpec((1, LANES), const),
                  pl.BlockSpec((SUBLANES, 1), const),
                  pl.BlockSpec((1, mi), const),
                  pl.BlockSpec(wo.shape, const)],
        out_specs=pl.BlockSpec((nb, L, D), tile),
        scratch_shapes=[pltpu.VMEM((nb, L + SUBLANES, C2), F32),
                        pltpu.VMEM((nb, M_HEADS, dh, 2 * dh), F32),
                        pltpu.VMEM((nb, M_HEADS, SUBLANES, LANES), F32),
                        pltpu.VMEM((nb, L, mi), F32)],
        compiler_params=_params("arbitrary", "arbitrary"),
        name="mlstm",
    )(qk3, v3, o3, ifc3, ifr3, cw, cb, bifc, bifr, ng, wo)


def _merge_kernel(x_ref, ya_ref, yb_ref, sga_ref, sgb_ref, mod_ref, g2_ref, wo_ref, wr_ref, br_ref,
                  x1_ref, h2_ref, ri_ref, rf_ref, cnt_ref, run_ref):
    ts = MERGE_SUB
    subs = [pl.ds(r0, ts) for r0 in range(0, x_ref.shape[0], ts)]

    @pl.when(pl.program_id(0) == 0)
    def _():
        run_ref[...] = jnp.zeros(run_ref.shape, F32)

    gate1 = mod_ref[0, 2:3, :]
    shift2 = mod_ref[0, 3:4, :]
    scale2 = mod_ref[0, 4:5, :]
    lane = lax.broadcasted_iota(I32, (ts, LANES), 1).astype(F32)
    neg = -jnp.inf
    rows = lax.broadcasted_iota(I32, (ts, ts), 0)
    cols = lax.broadcasted_iota(I32, (ts, ts), 1)
    strict = jnp.where(cols < rows, 1.0, 0.0).astype(BF16)

    def first_argmax(vals):
        mx = jnp.max(vals, axis=-1, keepdims=True)
        idx = jnp.min(jnp.where(vals == mx, lane, float(LANES)), axis=-1, keepdims=True)
        return mx, idx

    h2s = []
    for sl in subs:
        merged = (sga_ref[sl, :].astype(F32) * ya_ref[sl, :].astype(F32)
                  + sgb_ref[sl, :].astype(F32) * yb_ref[sl, :].astype(F32))
        mix = jnp.dot(merged.astype(BF16), wo_ref[...], preferred_element_type=F32)
        x1 = x_ref[sl, :] + gate1 * mix
        x1_ref[sl, :] = x1
        ms = jnp.mean(x1 * x1, axis=-1, keepdims=True)
        h2 = x1 * lax.rsqrt(ms + RMS_EPS) * g2_ref[...]
        h2 = h2 * (1.0 + scale2) + shift2
        h2_ref[sl, :] = _pack_bf16_pairs(h2)
        h2s.append(h2.astype(BF16))

    run = run_ref[0:1, :]
    for sl, h2b in zip(subs, h2s):
        logits = jnp.dot(h2b, wr_ref[...], preferred_element_type=F32) + br_ref[...]
        lg = jnp.where(lane < N_GROUPS, logits, neg)
        gmax, gsel = first_argmax(lg)
        p_g = 1.0 / jnp.sum(jnp.exp(lg - gmax), axis=-1, keepdims=True)
        lo = N_GROUPS + gsel * E_PER_GROUP
        le = jnp.where((lane >= lo) & (lane < lo + E_PER_GROUP), logits, neg)
        l1, i1 = first_argmax(le)
        l2, i2 = first_argmax(jnp.where(lane == i1, neg, le))
        r = jnp.exp(l2 - l1)
        w1 = p_g / (1.0 + r)
        w2 = p_g * r / (1.0 + r)
        e1 = i1 - N_GROUPS
        e2 = i2 - N_GROUPS

        onehot = jnp.where((lane == e1) | (lane == e2), 1.0, 0.0)
        before = jnp.dot(strict, onehot.astype(BF16), preferred_element_type=F32) + run
        rank1 = jnp.sum(jnp.where(lane == e1, before, 0.0), axis=-1, keepdims=True)
        rank2 = jnp.sum(jnp.where(lane == e2, before, 0.0), axis=-1, keepdims=True)
        run = run + jnp.sum(onehot, axis=0, keepdims=True)

        codes = jnp.where(lane == 0, e1 * float(RANK_RADIX) + rank1,
                          jnp.where(lane == 1, e2 * float(RANK_RADIX) + rank2, 0.0))
        ri_ref[:, sl] = codes.T[0:SUBLANES, :].astype(I32)
        rf_ref[sl, :] = jnp.where(lane == 0, w1, jnp.where(lane == 1, w2, 0.0))
    run_ref[...] = jnp.broadcast_to(run, run_ref.shape)
    cnt_ref[...] = jnp.broadcast_to(run, cnt_ref.shape).astype(I32)


def _merge(x2, ya, yb, sga, sgb, mod3, g2, wo, wr, br, seq):
    T, D = x2.shape
    tm = MERGE_TM
    per_b = seq // tm
    row = lambda i: (i, 0)
    const = lambda i: (0, 0)
    return pl.pallas_call(
        _merge_kernel,
        out_shape=[jax.ShapeDtypeStruct((T, D), F32),
                   jax.ShapeDtypeStruct((T, D // 2), I32),
                   jax.ShapeDtypeStruct((SUBLANES, T), I32),
                   jax.ShapeDtypeStruct((T, LANES), F32),
                   jax.ShapeDtypeStruct((SUBLANES, LANES), I32)],
        grid=(T // tm,),
        in_specs=[pl.BlockSpec((tm, D), row),
                  pl.BlockSpec((tm, D), row),
                  pl.BlockSpec((tm, D), row),
                  pl.BlockSpec((tm, D), row),
                  pl.BlockSpec((tm, D), row),
                  pl.BlockSpec((1, 6, D), lambda i: (i // per_b, 0, 0)),
                  pl.BlockSpec((1, D), const),
                  pl.BlockSpec(wo.shape, const),
                  pl.BlockSpec(wr.shape, const),
                  pl.BlockSpec((1, LANES), const)],
        out_specs=[pl.BlockSpec((tm, D), row),
                   pl.BlockSpec((tm, D // 2), row),
                   pl.BlockSpec((SUBLANES, tm), lambda i: (0, i)),
                   pl.BlockSpec((tm, LANES), row),
                   pl.BlockSpec((SUBLANES, LANES), const)],
        scratch_shapes=[pltpu.VMEM((SUBLANES, LANES), F32)],
        compiler_params=_params("arbitrary"),
        name="merge",
    )(x2, ya, yb, sga, sgb, mod3, g2, wo, wr, br)


def _sc_workers():
    info = plsc.get_sparse_core_info()
    mesh = plsc.VectorSubcoreMesh(core_axis_name="core", subcore_axis_name="subcore")
    params = pltpu.CompilerParams()
    if "needs_layout_passes" in pltpu.CompilerParams.__dataclass_fields__:
        params = dataclasses.replace(params, needs_layout_passes=False)
    return info, mesh, params


def _rows_from_codes(code_v, base_v, idx_v, lanes):
    for j in range(code_v.shape[0] // lanes):
        c = code_v[pl.ds(j * lanes, lanes)]
        expert = lax.shift_right_logical(c, RANK_BITS)
        idx_v[pl.ds(j * lanes, lanes)] = plsc.load_gather(base_v, [expert]) + (c & (RANK_RADIX - 1))


def _two_slot_loop(n_chunks, start, finish):
    start(0, 0)

    @pl.loop(0, n_chunks, step=2)
    def _(c):
        start(c + 1, 1)
        finish(c, 0)

        @pl.when(c + 2 < n_chunks)
        def _():
            start(c + 2, 0)

        finish(c + 1, 1)


def _sc_dispatch(h2, code0, code1, base, n_rows):
    T, D = h2.shape
    info, mesh, params = _sc_workers()
    n_workers = info.num_cores * info.num_subcores
    w = SC_WINDOW_BYTES // (D * h2.dtype.itemsize)
    per_w = T // n_workers
    n_chunks = per_w // w
    assert per_w * n_workers == T and n_chunks * w == per_w and n_chunks % 2 == 0

    @functools.partial(
        pl.kernel, out_type=jax.ShapeDtypeStruct((n_rows, D), h2.dtype), mesh=mesh, compiler_params=params,
        scratch_types=[pltpu.VMEM((N_EXPERTS,), I32), pltpu.VMEM((w,), I32), pltpu.VMEM((w,), I32),
                       pltpu.VMEM((w,), I32), pltpu.VMEM((w, D), h2.dtype), pltpu.VMEM((w, D), h2.dtype),
                       pltpu.SemaphoreType.DMA, pltpu.SemaphoreType.DMA])
    def scatter(h_hbm, c0_hbm, c1_hbm, b_hbm, xs_hbm, base_v, code_v, i0_v, i1_v, rows0, rows1, sem0, sem1):
        wid = lax.axis_index("subcore") * info.num_cores + lax.axis_index("core")
        w0 = wid * per_w
        pltpu.sync_copy(b_hbm, base_v)
        rows = (rows0, rows1)
        sems = (sem0, sem1)

        def start(c, slot):
            pltpu.async_copy(h_hbm.at[pl.ds(w0 + c * w, w)], rows[slot], sems[slot])

        def finish(c, slot):
            pltpu.sync_copy(c0_hbm.at[pl.ds(w0 + c * w, w)], code_v)
            _rows_from_codes(code_v, base_v, i0_v, info.num_lanes)
            pltpu.sync_copy(c1_hbm.at[pl.ds(w0 + c * w, w)], code_v)
            _rows_from_codes(code_v, base_v, i1_v, info.num_lanes)
            pltpu.make_async_copy(h_hbm.at[pl.ds(w0 + c * w, w)], rows[slot], sems[slot]).wait()
            pltpu.sync_copy(rows[slot], xs_hbm.at[i0_v])
            pltpu.sync_copy(rows[slot], xs_hbm.at[i1_v])

        _two_slot_loop(n_chunks, start, finish)

    return scatter(h2, code0, code1, base)


def _sc_collect(ys, codes, base):
    n = codes.shape[0]
    D = ys.shape[1]
    info, mesh, params = _sc_workers()
    n_workers = info.num_cores * info.num_subcores
    w = SC_WINDOW_BYTES // (D * ys.dtype.itemsize)
    per_w = n // n_workers
    n_chunks = per_w // w
    assert per_w * n_workers == n and n_chunks * w == per_w and n_chunks % 2 == 0

    @functools.partial(
        pl.kernel, out_type=jax.ShapeDtypeStruct((n, D), ys.dtype), mesh=mesh, compiler_params=params,
        scratch_types=[pltpu.VMEM((N_EXPERTS,), I32), pltpu.VMEM((w,), I32), pltpu.VMEM((w,), I32),
                       pltpu.VMEM((w,), I32), pltpu.VMEM((w, D), ys.dtype), pltpu.VMEM((w, D), ys.dtype),
                       pltpu.SemaphoreType.DMA, pltpu.SemaphoreType.DMA])
    def gather(ys_hbm, c_hbm, b_hbm, yk_hbm, base_v, code_v, i0_v, i1_v, rows0, rows1, sem0, sem1):
        wid = lax.axis_index("subcore") * info.num_cores + lax.axis_index("core")
        w0 = wid * per_w
        pltpu.sync_copy(b_hbm, base_v)
        idx = (i0_v, i1_v)
        rows = (rows0, rows1)
        sems = (sem0, sem1)

        def start(c, slot):
            pltpu.sync_copy(c_hbm.at[pl.ds(w0 + c * w, w)], code_v)
            _rows_from_codes(code_v, base_v, idx[slot], info.num_lanes)
            pltpu.async_copy(ys_hbm.at[idx[slot]], rows[slot], sems[slot])

        def finish(c, slot):
            pltpu.make_async_copy(ys_hbm.at[idx[slot]], rows[slot], sems[slot]).wait()
            pltpu.sync_copy(rows[slot], yk_hbm.at[pl.ds(w0 + c * w, w)])

        _two_slot_loop(n_chunks, start, finish)

    return gather(ys, codes, base)


def _schedule_kernel(cnt_ref, te_ref, tb_ref, base_ref, nt_ref):
    tm = EXPERT_TM

    def expert(e, t0):
        n = (cnt_ref[e] + tm - 1) // tm
        base_ref[e] = t0 * tm

        def tile(t, c):
            te_ref[t] = e
            tb_ref[t] = t
            return c

        lax.fori_loop(t0, t0 + n, tile, 0)
        return t0 + n

    nt = lax.fori_loop(0, N_EXPERTS, expert, 0)
    nt_ref[0] = nt
    last = te_ref[nt - 1]

    def idle(t, c):
        te_ref[t] = last
        tb_ref[t] = nt - 1
        return c

    lax.fori_loop(nt, te_ref.shape[0], idle, 0)


def _schedule(counts, max_tiles):
    smem = pl.BlockSpec(memory_space=pltpu.SMEM)
    return pl.pallas_call(
        _schedule_kernel,
        out_shape=[jax.ShapeDtypeStruct((max_tiles,), I32),
                   jax.ShapeDtypeStruct((max_tiles,), I32),
                   jax.ShapeDtypeStruct((N_EXPERTS,), I32),
                   jax.ShapeDtypeStruct((1,), I32)],
        in_specs=[smem],
        out_specs=[smem, smem, smem, smem],
        name="schedule",
    )(counts)


def _expert_kernel(te_ref, tb_ref, nt_ref, xs_ref, wg_ref, wu_ref, wd_ref, ys_ref):
    @pl.when(pl.program_id(0) < nt_ref[0])
    def _():
        xb = _unpack_bf16_pairs(xs_ref[...])
        g = jnp.dot(xb, wg_ref[0], preferred_element_type=F32)
        u = jnp.dot(xb, wu_ref[0], preferred_element_type=F32)
        act = (g * _sigmoid(g)) * u
        ys_ref[...] = _pack_bf16_pairs(jnp.dot(act.astype(BF16), wd_ref[0], preferred_element_type=F32))


def _experts(tile_e, tile_b, n_tiles, xs, wg, wu, wd, max_tiles):
    P = xs.shape[0]
    D, de = wg.shape[1:]
    tm = EXPERT_TM
    wmap = lambda j, te, tb, nt: (te[j], 0, 0)
    rmap = lambda j, te, tb, nt: (tb[j], 0)
    return pl.pallas_call(
        _expert_kernel,
        out_shape=jax.ShapeDtypeStruct(xs.shape, I32),
        grid_spec=pltpu.PrefetchScalarGridSpec(
            num_scalar_prefetch=3,
            grid=(max_tiles,),
            in_specs=[pl.BlockSpec((tm, xs.shape[1]), rmap),
                      pl.BlockSpec((1, D, de), wmap),
                      pl.BlockSpec((1, D, de), wmap),
                      pl.BlockSpec((1, de, D), wmap)],
            out_specs=pl.BlockSpec((tm, xs.shape[1]), rmap)),
        compiler_params=_params("arbitrary"),
        name="experts",
    )(tile_e, tile_b, n_tiles, xs, wg, wu, wd)


def _combine_kernel(x1_ref, rf_ref, mod_ref, gf_ref, y0_ref, y1_ref, out_ref, *, final_norm):
    gate2 = mod_ref[0, 5:6, :]
    w = rf_ref[...]
    moe = w[:, 0:1] * _unpack_pairs_f32(y0_ref[...]) + w[:, 1:2] * _unpack_pairs_f32(y1_ref[...])
    x2 = x1_ref[...] + gate2 * moe
    if final_norm:
        ms = jnp.mean(x2 * x2, axis=-1, keepdims=True)
        x2 = x2 * lax.rsqrt(ms + RMS_EPS) * gf_ref[...]
    out_ref[...] = x2


def _combine(x1, rf, mod3, gf, yk, seq, final_norm):
    T, D = x1.shape
    tc = COMBINE_TM
    per_b = seq // tc
    n_blk = T // tc
    return pl.pallas_call(
        functools.partial(_combine_kernel, final_norm=final_norm),
        out_shape=jax.ShapeDtypeStruct((T, D), F32),
        grid=(n_blk,),
        in_specs=[pl.BlockSpec((tc, D), lambda i: (i, 0)),
                  pl.BlockSpec((tc, LANES), lambda i: (i, 0)),
                  pl.BlockSpec((1, 6, D), lambda i: (i // per_b, 0, 0)),
                  pl.BlockSpec((1, D), lambda i: (0, 0)),
                  pl.BlockSpec((tc, yk.shape[1]), lambda i: (i, 0)),
                  pl.BlockSpec((tc, yk.shape[1]), lambda i: (n_blk + i, 0))],
        out_specs=pl.BlockSpec((tc, D), lambda i: (i, 0)),
        compiler_params=_params("arbitrary"),
        name="combine",
    )(x1, rf, mod3, gf, yk, yk)


def _layer(x2, c, seq, layer, w_ada, b_ada, g_norm1, w_in, b_if, conv_dw_w, conv_dw_b, conv_ln_g, conv_ln_b,
           w_conv_out, qk_conv_w, qk_conv_b, m_norm_g, w_m_out, w_out, g_norm2, w_rg, b_rg,
           w_re, b_re, w_e_gate, w_e_up, w_e_down):
    T, D = x2.shape
    B = T // seq
    dc = D // 2
    nif = 2 * M_HEADS

    if_lo = 6 * dc
    w_in_t = jnp.swapaxes(w_in, 1, 2)
    w_main = w_in_t[layer, :if_lo, :].astype(BF16)
    w_gates = w_in_t[layer, if_lo + nif:, :].astype(BF16)
    full = lambda w: (w, [(0, w.shape[2])])
    mod, (w_out_b, w_conv_out_b, w_m_out_b) = _ada(
        c, w_ada, b_ada, layer, [full(w_out), full(w_conv_out), full(w_m_out)])
    mod3 = mod.reshape(B, 6, D)
    w_if = w_in[layer, :, if_lo:if_lo + nif]
    w_if_pad = jnp.pad(w_if, ((0, 0), (0, LANES - nif))).astype(BF16)
    w_ift = w_if.T.astype(BF16)
    (u, qk, v, o, sga, sgb, ifc, ifr), (wg_b, wu_b, wd_b) = _inproj(
        x2, mod3, g_norm1.reshape(1, D), w_main, w_gates, w_if_pad, w_ift, seq,
        riders=(w_e_gate, w_e_up, w_e_down))

    ya = _conv_branch(
        u.reshape(B, seq, dc), conv_dw_w, conv_dw_b.reshape(1, dc), conv_ln_g.reshape(1, dc),
        conv_ln_b.reshape(1, dc), w_conv_out_b)
    bifc = jnp.pad(b_if, (0, LANES - nif)).reshape(1, LANES)
    bifr = b_if.reshape(nif, 1)
    yb = _mlstm_branch(
        qk.reshape(B, seq, 2 * dc), v.reshape(B, seq, dc), o.reshape(B, seq, dc),
        ifc.reshape(B, seq, LANES), ifr, qk_conv_w, qk_conv_b.reshape(1, 2 * dc), bifc, bifr,
        m_norm_g.reshape(1, dc), w_m_out_b)

    n_r = N_GROUPS + N_EXPERTS
    w_r = jnp.pad(jnp.concatenate([w_rg, w_re], axis=1), ((0, 0), (0, LANES - n_r))).astype(BF16)
    b_r = jnp.pad(jnp.concatenate([b_rg, b_re]), (0, LANES - n_r)).reshape(1, LANES)
    x1, h2, ri, rf, cnt = _merge(x2, ya.reshape(T, D), yb.reshape(T, D), sga, sgb, mod3,
                                 g_norm2.reshape(1, D), w_out_b, w_r, b_r, seq)

    tm = EXPERT_TM
    max_tiles = (T * TOP_K) // tm + N_EXPERTS
    tile_e, tile_b, base, n_tiles = _schedule(cnt[0, :N_EXPERTS], max_tiles)

    code0 = ri[0]
    code1 = ri[1]
    xs = _sc_dispatch(h2, code0, code1, base, max_tiles * tm)
    ys = _experts(tile_e, tile_b, n_tiles, xs, wg_b, wu_b, wd_b, max_tiles)
    return x1, rf, mod3, ys, code0, code1, base


def kernel(x, c, w_ada, b_ada, g_norm1, w_in, b_if, conv_dw_w, conv_dw_b, conv_ln_g, conv_ln_b,
           w_conv_out, qk_conv_w, qk_conv_b, m_norm_g, w_m_out, w_out, g_norm2, w_rg, b_rg,
           w_re, b_re, w_e_gate, w_e_up, w_e_down, g_final):
    B, S, D = x.shape
    depth = w_ada.shape[0]
    x2 = x.reshape(B * S, D)
    for l in range(depth):
        x1, rf, mod3, ys, code0, code1, base = _layer(
            x2, c, S, l, w_ada[l], b_ada[l], g_norm1[l], w_in, b_if[l], conv_dw_w[l], conv_dw_b[l],
            conv_ln_g[l], conv_ln_b[l], w_conv_out, qk_conv_w[l], qk_conv_b[l], m_norm_g[l],
            w_m_out, w_out, g_norm2[l], w_rg[l], b_rg[l], w_re[l], b_re[l],
            w_e_gate[l], w_e_up[l], w_e_down[l])
        yk = _sc_collect(ys, jnp.concatenate([code0, code1]), base)
        x2 = _combine(x1, rf, mod3, g_final.reshape(1, D), yk, S, final_norm=l == depth - 1)
    return x2.reshape(B, S, D)
```

```python
import dataclasses
import functools

import jax
import jax.numpy as jnp
from jax import lax
from jax.experimental import pallas as pl
from jax.experimental.pallas import tpu as pltpu
from jax.experimental.pallas import tpu_sc as plsc

F32 = jnp.float32
BF16 = jnp.bfloat16
I32 = jnp.int32

M_HEADS = 4
CONV_WIDTH = 31
QK_CONV_WIDTH = 4
N_GROUPS = 4
E_PER_GROUP = 8
N_EXPERTS = N_GROUPS * E_PER_GROUP
TOP_K = 2
RMS_EPS = 1e-6
LN_EPS = 1e-5

LANES = 128
SUBLANES = 8
VMEM_LIMIT = 56 * 1024 * 1024

ADA_TN = 768
INPROJ_TM = 512
INPROJ_SUB = 256
CONV_TS = 1024
CONV_HALO = 32
CONV_RC = 256
MLSTM_L = 128
MLSTM_SEQS = 8
MERGE_TM = 1024
MERGE_SUB = 256
EXPERT_TM = 512
SC_WINDOW_BYTES = 128 * 1024
COMBINE_TM = 1024
RANK_BITS = 16
RANK_RADIX = 1 << RANK_BITS
assert EXPERT_TM & (EXPERT_TM - 1) == 0


def _sigmoid(v):
    return 1.0 / (1.0 + jnp.exp(-v))


def _log_sigmoid(v):
    return -(jnp.maximum(-v, 0.0) + jnp.log1p(jnp.exp(-jnp.abs(v))))


def _pack_bf16_pairs(v):
    n = v.shape[1] // 2
    bits = lax.bitcast_convert_type(v.astype(BF16).astype(F32), jnp.uint32)
    word = bits[:, n:] | (bits[:, :n] >> 16)
    return lax.bitcast_convert_type(word, I32)


def _unpack_pairs_f32(w):
    bits = lax.bitcast_convert_type(w, jnp.uint32)
    lo = lax.bitcast_convert_type(bits << 16, F32)
    hi = lax.bitcast_convert_type(bits & jnp.uint32(0xFFFF0000), F32)
    return jnp.concatenate([lo, hi], axis=1)


def _unpack_bf16_pairs(w):
    return _unpack_pairs_f32(w).astype(BF16)


def _split_bf16(v):
    hi = v.astype(BF16)
    r1 = v - hi.astype(F32)
    mid = r1.astype(BF16)
    lo = (r1 - mid.astype(F32)).astype(BF16)
    return hi, mid, lo


def _params(*sem):
    return pltpu.CompilerParams(dimension_semantics=sem, vmem_limit_bytes=VMEM_LIMIT)


def _ada_kernel(*refs, cuts):
    c_ref, w_ref, b_ref = refs[:3]
    srcs = refs[3:3 + len(cuts)]
    o_ref = refs[3 + len(cuts)]
    dsts = iter(refs[4 + len(cuts):])
    c = c_ref[...]
    s = c * _sigmoid(c)
    o_ref[...] = jnp.dot(s, w_ref[...], preferred_element_type=F32,
                         precision=lax.Precision.HIGHEST) + b_ref[...]
    for src, ranges in zip(srcs, cuts):
        for lo, hi in ranges:
            next(dsts)[...] = src[:, lo:hi].astype(BF16)


def _ada(c, w_ada, b_ada, layer, riders):
    B, D = c.shape
    N = w_ada.shape[1]
    n_steps = N // ADA_TN
    slab = lambda j: (j, 0)
    outs = pl.pallas_call(
        functools.partial(_ada_kernel, cuts=tuple(tuple(r) for _, r in riders)),
        out_shape=[jax.ShapeDtypeStruct((B, N), F32)]
        + [jax.ShapeDtypeStruct((w.shape[1], hi - lo), BF16) for w, r in riders for lo, hi in r],
        grid=(n_steps,),
        in_specs=[pl.BlockSpec((B, D), lambda j: (0, 0)),
                  pl.BlockSpec((D, ADA_TN), lambda j: (0, j)),
                  pl.BlockSpec((1, ADA_TN), lambda j: (0, j))]
        + [pl.BlockSpec((None, w.shape[1] // n_steps, w.shape[2]), lambda j: (layer, j, 0))
           for w, _ in riders],
        out_specs=[pl.BlockSpec((B, ADA_TN), lambda j: (0, j))]
        + [pl.BlockSpec((w.shape[1] // n_steps, hi - lo), slab) for w, r in riders for lo, hi in r],
        compiler_params=_params("arbitrary"),
        name="ada",
    )(c, w_ada, b_ada.reshape(1, N), *[w for w, _ in riders])
    return outs[0], outs[1:]


def _inproj_kernel(*refs, n_riders):
    x_ref, mod_ref, g_ref, wm_ref, wgt_ref, wif_ref, wift_ref = refs[:7]
    rider_in = refs[7:7 + n_riders]
    u_ref, qk_ref, v_ref, o_ref, sga_ref, sgb_ref, ifc_ref, ifr_ref = refs[7 + n_riders:15 + n_riders]
    rider_out = refs[15 + n_riders:]

    for src, dst in zip(rider_in, rider_out):
        dst[...] = src[...].astype(BF16)

    shift = mod_ref[0, 0:1, :]
    scale = mod_ref[0, 1:2, :]
    dc = u_ref.shape[1]
    d = sga_ref.shape[1]
    ts = INPROJ_SUB
    subs = [pl.ds(r0, ts) for r0 in range(0, x_ref.shape[0], ts)]

    hbs = []
    for sl in subs:
        x = x_ref[sl, :]
        ms = jnp.mean(x * x, axis=-1, keepdims=True)
        h = x * lax.rsqrt(ms + RMS_EPS) * g_ref[...]
        h = h * (1.0 + scale) + shift
        hbs.append(h.astype(BF16))

    def times_t(a, wt):
        return lax.dot_general(a, wt, (((1,), (1,)), ((), ())), preferred_element_type=F32)

    for sl, hb in zip(subs, hbs):
        def seg(lo, hi):
            return times_t(hb, wm_ref[lo:hi, :])

        u_ref[sl, :] = seg(0, dc) * _sigmoid(seg(dc, 2 * dc))
        qk_ref[sl, :] = seg(2 * dc, 4 * dc)
        v_ref[sl, :] = seg(4 * dc, 5 * dc).astype(BF16)
        o_ref[sl, :] = seg(5 * dc, 6 * dc)
        sga_ref[sl, :] = _sigmoid(times_t(hb, wgt_ref[0:d, :])).astype(BF16)
        sgb_ref[sl, :] = _sigmoid(times_t(hb, wgt_ref[d:2 * d, :])).astype(BF16)
        ifc_ref[sl, :] = jnp.dot(hb, wif_ref[...], preferred_element_type=F32)
        ifr_ref[0, :, sl] = lax.dot_general(wift_ref[...], hb, (((1,), (1,)), ((), ())),
                                            preferred_element_type=F32)


def _inproj(x2, mod3, g1, w_main, w_gates, w_if, w_ift, seq, riders):
    T, D = x2.shape
    tm = INPROJ_TM
    dc = D // 2
    per_b = seq // tm
    row = lambda i: (i, 0)
    const = lambda i: (0, 0)
    slab = lambda i: (i, 0, 0)
    slabs = [r.reshape(T // tm, -1, r.shape[-1]) for r in riders]
    once = pl.Buffered(1)
    outs = pl.pallas_call(
        functools.partial(_inproj_kernel, n_riders=len(riders)),
        out_shape=[jax.ShapeDtypeStruct((T, dc), F32),
                   jax.ShapeDtypeStruct((T, 2 * dc), F32),
                   jax.ShapeDtypeStruct((T, dc), BF16),
                   jax.ShapeDtypeStruct((T, dc), F32),
                   jax.ShapeDtypeStruct((T, D), BF16),
                   jax.ShapeDtypeStruct((T, D), BF16),
                   jax.ShapeDtypeStruct((T, LANES), F32),
                   jax.ShapeDtypeStruct((T // seq, SUBLANES, seq), F32)]
        + [jax.ShapeDtypeStruct(s.shape, BF16) for s in slabs],
        grid=(T // tm,),
        in_specs=[pl.BlockSpec((tm, D), row),
                  pl.BlockSpec((1, 6, D), lambda i: (i // per_b, 0, 0)),
                  pl.BlockSpec((1, D), const),
                  pl.BlockSpec(w_main.shape, const, pipeline_mode=once),
                  pl.BlockSpec(w_gates.shape, const, pipeline_mode=once),
                  pl.BlockSpec(w_if.shape, const),
                  pl.BlockSpec(w_ift.shape, const)]
        + [pl.BlockSpec((1,) + s.shape[1:], slab) for s in slabs],
        out_specs=[pl.BlockSpec((tm, dc), row),
                   pl.BlockSpec((tm, 2 * dc), row),
                   pl.BlockSpec((tm, dc), row),
                   pl.BlockSpec((tm, dc), row),
                   pl.BlockSpec((tm, D), row),
                   pl.BlockSpec((tm, D), row),
                   pl.BlockSpec((tm, LANES), row),
                   pl.BlockSpec((1, SUBLANES, tm), lambda i: (i // per_b, 0, i % per_b))]
        + [pl.BlockSpec((1,) + s.shape[1:], slab) for s in slabs],
        compiler_params=_params("arbitrary"),
        name="inproj",
    )(x2, mod3, g1, w_main, w_gates, w_if, w_ift, *slabs)
    return outs[:8], [o.reshape(r.shape) for o, r in zip(outs[8:], riders)]


def _conv_kernel(u_ref, w_ref, b_ref, lg_ref, lb_ref, wo_ref, y_ref, ubuf, sbuf, cbuf):
    ts = u_ref.shape[1]
    halo = CONV_HALO

    @pl.when(pl.program_id(1) == 0)
    def _():
        ubuf[0:halo, :] = jnp.zeros((halo, ubuf.shape[1]), F32)

    ubuf[halo:halo + ts, :] = u_ref[0]
    ns = sbuf.shape[1]
    for r in range(1, SUBLANES):
        sbuf[r - 1] = ubuf[r:r + ns, :]
    off = halo - (CONV_WIDTH - 1)
    for r0 in range(0, ts, CONV_RC):
        acc = jnp.broadcast_to(b_ref[...], (CONV_RC, ubuf.shape[1]))
        for k in range(CONV_WIDTH):
            r = (off + k) % SUBLANES
            lo = off + k - r + r0
            win = ubuf[lo:lo + CONV_RC, :] if r == 0 else sbuf[r - 1, lo:lo + CONV_RC, :]
            acc = acc + w_ref[k:k + 1, :] * win
        cbuf[r0:r0 + CONV_RC, :] = acc
    ubuf[0:halo, :] = ubuf[ts:ts + halo, :]

    a = cbuf[...]
    mu = jnp.mean(a, axis=-1, keepdims=True)
    ac = a - mu
    var = jnp.mean(ac * ac, axis=-1, keepdims=True)
    z = ac * lax.rsqrt(var + LN_EPS) * lg_ref[...] + lb_ref[...]
    z = z * _sigmoid(z)
    y_ref[0] = jnp.dot(z.astype(BF16), wo_ref[...], preferred_element_type=F32).astype(BF16)


def _conv_branch(u3, w, b, lg, lb, wo):
    B, S, C = u3.shape
    D = wo.shape[1]
    ts = CONV_TS
    const = lambda bi, si: (0, 0)
    return pl.pallas_call(
        _conv_kernel,
        out_shape=jax.ShapeDtypeStruct((B, S, D), BF16),
        grid=(B, S // ts),
        in_specs=[pl.BlockSpec((1, ts, C), lambda bi, si: (bi, si, 0)),
                  pl.BlockSpec(w.shape, const),
                  pl.BlockSpec((1, C), const),
                  pl.BlockSpec((1, C), const),
                  pl.BlockSpec((1, C), const),
                  pl.BlockSpec(wo.shape, const)],
        out_specs=pl.BlockSpec((1, ts, D), lambda bi, si: (bi, si, 0)),
        scratch_shapes=[pltpu.VMEM((ts + CONV_HALO, C), F32),
                        pltpu.VMEM((SUBLANES - 1, ts + CONV_HALO - SUBLANES, C), F32),
                        pltpu.VMEM((ts, C), F32)],
        compiler_params=_params("arbitrary", "arbitrary"),
        name="conv",
    )(u3, w, b, lg, lb, wo)


def _mlstm_kernel(qk_ref, v_ref, o_ref, ifc_ref, ifr_ref, cw_ref, cb_ref, bifc_ref, bifr_ref, ng_ref,
                  wo_ref, y_ref, qkbuf, cn_ref, m_ref, hbuf):
    @pl.when(pl.program_id(1) == 0)
    def _():
        qkbuf[:, 0:SUBLANES, :] = jnp.zeros((qkbuf.shape[0], SUBLANES, qkbuf.shape[2]), F32)
        cn_ref[...] = jnp.zeros(cn_ref.shape, F32)
        m_ref[...] = jnp.zeros(m_ref.shape, F32)

    nb, L, mi = hbuf.shape
    dh = mi // M_HEADS
    halo = SUBLANES
    off = halo - (QK_CONV_WIDTH - 1)
    rows = lax.broadcasted_iota(I32, (L, L), 0)
    cols = lax.broadcasted_iota(I32, (L, L), 1)
    causal = cols <= rows
    lower = jnp.where(causal, 1.0, 0.0).astype(BF16)
    upper = jnp.where(rows <= cols, 1.0, 0.0).astype(BF16)
    lane = lax.broadcasted_iota(I32, (L, dh), 1)
    ones_col = jnp.where(lane == 0, 1.0, 0.0).astype(BF16)
    scale = dh ** -0.5

    seqs = []
    for b in range(nb):
        qkbuf[b, halo:halo + L, :] = qk_ref[b]
        y = jnp.broadcast_to(cb_ref[...], (L, qkbuf.shape[2]))
        for k in range(QK_CONV_WIDTH):
            y = y + cw_ref[k:k + 1, :] * qkbuf[b, off + k:off + k + L, :]
        y = y * _sigmoid(y)
        qkbuf[b, 0:halo, :] = qkbuf[b, L:L + halo, :]
        ifr = ifr_ref[b] + bifr_ref[...]
        ifc = ifc_ref[b] + bifc_ref[...]
        bcum_c = sum(jnp.dot(lower, p, preferred_element_type=F32) for p in _split_bf16(_log_sigmoid(ifc)))
        bcum_r = sum(jnp.dot(p, upper, preferred_element_type=F32) for p in _split_bf16(_log_sigmoid(ifr)))
        seqs.append((y, ifr, bcum_c, bcum_r))

    probs = [(b, hd) for b in range(nb) for hd in range(M_HEADS)]
    st = {}
    for p in probs:
        b, hd = p
        y, ifr, bcum_c, bcum_r = seqs[b]
        c0 = hd * dh
        qb = (y[:, c0:c0 + dh] * scale).astype(BF16)
        kt = y[:, mi + c0:mi + c0 + dh].T
        v = v_ref[b, :, c0:c0 + dh]
        bc = bcum_c[:, M_HEADS + hd:M_HEADS + hd + 1]
        br = bcum_r[M_HEADS + hd:M_HEADS + hd + 1, :]
        li = ifr[hd:hd + 1, :]
        m_prev = m_ref[b, hd, 0:1, 0:1]
        dmat = jnp.where(causal, bc - br + li, -jnp.inf)
        st[p] = dict(qb=qb, kt=kt, v=v, bc=bc, br=br, li=li, m_prev=m_prev, dmat=dmat)
    for p in probs:
        s = st[p]
        s["inter"] = s["bc"] + s["m_prev"]
        s["m_t"] = jnp.maximum(jnp.max(s["dmat"], axis=-1, keepdims=True), s["inter"])
    for p in probs:
        s = st[p]
        s["qk"] = jnp.dot(s["qb"], s["kt"].astype(BF16), preferred_element_type=F32)
    for p in probs:
        b, hd = p
        s = st[p]
        s["cn"] = cn_ref[b, hd]
        s["qcn"] = jnp.dot(s["qb"], s["cn"].astype(BF16), preferred_element_type=F32)
    for p in probs:
        s = st[p]
        s["wts"] = jnp.exp(s["dmat"] - s["m_t"])
        s["s_inter"] = jnp.exp(s["inter"] - s["m_t"])
    for p in probs:
        s = st[p]
        s["s_mat"] = s["qk"] * s["wts"]
    for p in probs:
        s = st[p]
        s["sv"] = jnp.dot(s["s_mat"].astype(BF16), s["v"], preferred_element_type=F32)
    for p in probs:
        s = st[p]
        s["rowsum"] = jnp.sum(s["s_mat"], axis=-1, keepdims=True)
    for p in probs:
        s = st[p]
        s["num"] = s["sv"] + s["s_inter"] * s["qcn"][:, 0:dh]
        s["den"] = s["rowsum"] + s["s_inter"] * s["qcn"][:, dh:dh + 1]
    for p in probs:
        b, hd = p
        s = st[p]
        b_last = s["br"][:, L - 1:L]
        a = b_last - s["br"] + s["li"]
        m_new = jnp.maximum(b_last + s["m_prev"], jnp.max(a, axis=-1, keepdims=True))
        wk = jnp.exp(a - m_new)
        sc = jnp.exp(b_last + s["m_prev"] - m_new)
        v_ext = jnp.concatenate([s["v"], ones_col], axis=1)
        cn_ref[b, hd] = sc * s["cn"] + jnp.dot((s["kt"] * wk).astype(BF16), v_ext, preferred_element_type=F32)
        m_ref[b, hd] = jnp.broadcast_to(m_new, m_ref.shape[2:])
    for p in probs:
        s = st[p]
        s["hh"] = s["num"] / jnp.maximum(jnp.abs(s["den"]), jnp.exp(-s["m_t"]))
        s["mu"] = jnp.mean(s["hh"], axis=-1, keepdims=True)
    for p in probs:
        s = st[p]
        s["hc"] = s["hh"] - s["mu"]
        s["var"] = jnp.mean(s["hc"] * s["hc"], axis=-1, keepdims=True)
    for p in probs:
        b, hd = p
        s = st[p]
        c0 = hd * dh
        hn = s["hc"] * lax.rsqrt(s["var"] + LN_EPS) * ng_ref[:, c0:c0 + dh]
        hbuf[b, :, c0:c0 + dh] = hn * _sigmoid(o_ref[b, :, c0:c0 + dh])
    for b in range(nb):
        y = jnp.dot(hbuf[b].astype(BF16), wo_ref[...], preferred_element_type=F32)
        y_ref[b] = y.astype(BF16)


def _mlstm_branch(qk3, v3, o3, ifc3, ifr3, cw, cb, bifc, bifr, ng, wo):
    B, S, C2 = qk3.shape
    mi = v3.shape[2]
    dh = mi // M_HEADS
    D = wo.shape[1]
    L = MLSTM_L
    nb = MLSTM_SEQS
    const = lambda bi, ci: (0, 0)
    tile = lambda bi, ci: (bi, ci, 0)
    return pl.pallas_call(
        _mlstm_kernel,
        out_shape=jax.ShapeDtypeStruct((B, S, D), BF16),
        grid=(B // nb, S // L),
        in_specs=[pl.BlockSpec((nb, L, C2), tile),
                  pl.BlockSpec((nb, L, mi), tile),
                  pl.BlockSpec((nb, L, mi), tile),
                  pl.BlockSpec((nb, L, LANES), tile),
                  pl.BlockSpec((nb, SUBLANES, L), lambda bi, ci: (bi, 0, ci)),
                  pl.BlockSpec(cw.shape, const),
                  pl.BlockSpec((1, C2), const),
                  pl.BlockSpec((1, LANES), const),
                  pl.BlockSpec((SUBLANES, 1), const),
                  pl.BlockSpec((1, mi), const),
                  pl.BlockSpec(wo.shape, const)],
        out_specs=pl.BlockSpec((nb, L, D), tile),
        scratch_shapes=[pltpu.VMEM((nb, L + SUBLANES, C2), F32),
                        pltpu.VMEM((nb, M_HEADS, dh, 2 * dh), F32),
                        pltpu.VMEM((nb, M_HEADS, SUBLANES, LANES), F32),
                        pltpu.VMEM((nb, L, mi), F32)],
        compiler_params=_params("arbitrary", "arbitrary"),
        name="mlstm",
    )(qk3, v3, o3, ifc3, ifr3, cw, cb, bifc, bifr, ng, wo)


def _merge_kernel(x_ref, ya_ref, yb_ref, sga_ref, sgb_ref, mod_ref, g2_ref, wo_ref, wr_ref, br_ref,
                  x1_ref, h2_ref, ri_ref, rf_ref, cnt_ref, run_ref):
    ts = MERGE_SUB
    subs = [pl.ds(r0, ts) for r0 in range(0, x_ref.shape[0], ts)]

    @pl.when(pl.program_id(0) == 0)
    def _():
        run_ref[...] = jnp.zeros(run_ref.shape, F32)

    gate1 = mod_ref[0, 2:3, :]
    shift2 = mod_ref[0, 3:4, :]
    scale2 = mod_ref[0, 4:5, :]
    lane = lax.broadcasted_iota(I32, (ts, LANES), 1).astype(F32)
    neg = -jnp.inf
    rows = lax.broadcasted_iota(I32, (ts, ts), 0)
    cols = lax.broadcasted_iota(I32, (ts, ts), 1)
    strict = jnp.where(cols < rows, 1.0, 0.0).astype(BF16)

    def first_argmax(vals):
        mx = jnp.max(vals, axis=-1, keepdims=True)
        idx = jnp.min(jnp.where(vals == mx, lane, float(LANES)), axis=-1, keepdims=True)
        return mx, idx

    h2s = []
    for sl in subs:
        merged = (sga_ref[sl, :].astype(F32) * ya_ref[sl, :].astype(F32)
                  + sgb_ref[sl, :].astype(F32) * yb_ref[sl, :].astype(F32))
        mix = jnp.dot(merged.astype(BF16), wo_ref[...], preferred_element_type=F32)
        x1 = x_ref[sl, :] + gate1 * mix
        x1_ref[sl, :] = x1
        ms = jnp.mean(x1 * x1, axis=-1, keepdims=True)
        h2 = x1 * lax.rsqrt(ms + RMS_EPS) * g2_ref[...]
        h2 = h2 * (1.0 + scale2) + shift2
        h2_ref[sl, :] = _pack_bf16_pairs(h2)
        h2s.append(h2.astype(BF16))

    run = run_ref[0:1, :]
    for sl, h2b in zip(subs, h2s):
        logits = jnp.dot(h2b, wr_ref[...], preferred_element_type=F32) + br_ref[...]
        lg = jnp.where(lane < N_GROUPS, logits, neg)
        gmax, gsel = first_argmax(lg)
        p_g = 1.0 / jnp.sum(jnp.exp(lg - gmax), axis=-1, keepdims=True)
        lo = N_GROUPS + gsel * E_PER_GROUP
        le = jnp.where((lane >= lo) & (lane < lo + E_PER_GROUP), logits, neg)
        l1, i1 = first_argmax(le)
        l2, i2 = first_argmax(jnp.where(lane == i1, neg, le))
        r = jnp.exp(l2 - l1)
        w1 = p_g / (1.0 + r)
        w2 = p_g * r / (1.0 + r)
        e1 = i1 - N_GROUPS
        e2 = i2 - N_GROUPS

        onehot = jnp.where((lane == e1) | (lane == e2), 1.0, 0.0)
        before = jnp.dot(strict, onehot.astype(BF16), preferred_element_type=F32) + run
        rank1 = jnp.sum(jnp.where(lane == e1, before, 0.0), axis=-1, keepdims=True)
        rank2 = jnp.sum(jnp.where(lane == e2, before, 0.0), axis=-1, keepdims=True)
        run = run + jnp.sum(onehot, axis=0, keepdims=True)

        codes = jnp.where(lane == 0, e1 * float(RANK_RADIX) + rank1,
                          jnp.where(lane == 1, e2 * float(RANK_RADIX) + rank2, 0.0))
        ri_ref[:, sl] = codes.T[0:SUBLANES, :].astype(I32)
        rf_ref[sl, :] = jnp.where(lane == 0, w1, jnp.where(lane == 1, w2, 0.0))
    run_ref[...] = jnp.broadcast_to(run, run_ref.shape)
    cnt_ref[...] = jnp.broadcast_to(run, cnt_ref.shape).astype(I32)


def _merge(x2, ya, yb, sga, sgb, mod3, g2, wo, wr, br, seq):
    T, D = x2.shape
    tm = MERGE_TM
    per_b = seq // tm
    row = lambda i: (i, 0)
    const = lambda i: (0, 0)
    return pl.pallas_call(
        _merge_kernel,
        out_shape=[jax.ShapeDtypeStruct((T, D), F32),
                   jax.ShapeDtypeStruct((T, D // 2), I32),
                   jax.ShapeDtypeStruct((SUBLANES, T), I32),
                   jax.ShapeDtypeStruct((T, LANES), F32),
                   jax.ShapeDtypeStruct((SUBLANES, LANES), I32)],
        grid=(T // tm,),
        in_specs=[pl.BlockSpec((tm, D), row),
                  pl.BlockSpec((tm, D), row),
                  pl.BlockSpec((tm, D), row),
                  pl.BlockSpec((tm, D), row),
                  pl.BlockSpec((tm, D), row),
                  pl.BlockSpec((1, 6, D), lambda i: (i // per_b, 0, 0)),
                  pl.BlockSpec((1, D), const),
                  pl.BlockSpec(wo.shape, const),
                  pl.BlockSpec(wr.shape, const),
                  pl.BlockSpec((1, LANES), const)],
        out_specs=[pl.BlockSpec((tm, D), row),
                   pl.BlockSpec((tm, D // 2), row),
                   pl.BlockSpec((SUBLANES, tm), lambda i: (0, i)),
                   pl.BlockSpec((tm, LANES), row),
                   pl.BlockSpec((SUBLANES, LANES), const)],
        scratch_shapes=[pltpu.VMEM((SUBLANES, LANES), F32)],
        compiler_params=_params("arbitrary"),
        name="merge",
    )(x2, ya, yb, sga, sgb, mod3, g2, wo, wr, br)


def _sc_workers():
    info = plsc.get_sparse_core_info()
    mesh = plsc.VectorSubcoreMesh(core_axis_name="core", subcore_axis_name="subcore")
    params = pltpu.CompilerParams()
    if "needs_layout_passes" in pltpu.CompilerParams.__dataclass_fields__:
        params = dataclasses.replace(params, needs_layout_passes=False)
    return info, mesh, params


def _rows_from_codes(code_v, base_v, idx_v, lanes):
    for j in range(code_v.shape[0] // lanes):
        c = code_v[pl.ds(j * lanes, lanes)]
        expert = lax.shift_right_logical(c, RANK_BITS)
        idx_v[pl.ds(j * lanes, lanes)] = plsc.load_gather(base_v, [expert]) + (c & (RANK_RADIX - 1))


def _two_slot_loop(n_chunks, start, finish):
    start(0, 0)

    @pl.loop(0, n_chunks, step=2)
    def _(c):
        start(c + 1, 1)
        finish(c, 0)

        @pl.when(c + 2 < n_chunks)
        def _():
            start(c + 2, 0)

        finish(c + 1, 1)


def _sc_dispatch(h2, code0, code1, base, n_rows):
    T, D = h2.shape
    info, mesh, params = _sc_workers()
    n_workers = info.num_cores * info.num_subcores
    w = SC_WINDOW_BYTES // (D * h2.dtype.itemsize)
    per_w = T // n_workers
    n_chunks = per_w // w
    assert per_w * n_workers == T and n_chunks * w == per_w and n_chunks % 2 == 0

    @functools.partial(
        pl.kernel, out_type=jax.ShapeDtypeStruct((n_rows, D), h2.dtype), mesh=mesh, compiler_params=params,
        scratch_types=[pltpu.VMEM((N_EXPERTS,), I32), pltpu.VMEM((w,), I32), pltpu.VMEM((w,), I32),
                       pltpu.VMEM((w,), I32), pltpu.VMEM((w, D), h2.dtype), pltpu.VMEM((w, D), h2.dtype),
                       pltpu.SemaphoreType.DMA, pltpu.SemaphoreType.DMA])
    def scatter(h_hbm, c0_hbm, c1_hbm, b_hbm, xs_hbm, base_v, code_v, i0_v, i1_v, rows0, rows1, sem0, sem1):
        wid = lax.axis_index("subcore") * info.num_cores + lax.axis_index("core")
        w0 = wid * per_w
        pltpu.sync_copy(b_hbm, base_v)
        rows = (rows0, rows1)
        sems = (sem0, sem1)

        def start(c, slot):
            pltpu.async_copy(h_hbm.at[pl.ds(w0 + c * w, w)], rows[slot], sems[slot])

        def finish(c, slot):
            pltpu.sync_copy(c0_hbm.at[pl.ds(w0 + c * w, w)], code_v)
            _rows_from_codes(code_v, base_v, i0_v, info.num_lanes)
            pltpu.sync_copy(c1_hbm.at[pl.ds(w0 + c * w, w)], code_v)
            _rows_from_codes(code_v, base_v, i1_v, info.num_lanes)
            pltpu.make_async_copy(h_hbm.at[pl.ds(w0 + c * w, w)], rows[slot], sems[slot]).wait()
            pltpu.sync_copy(rows[slot], xs_hbm.at[i0_v])
            pltpu.sync_copy(rows[slot], xs_hbm.at[i1_v])

        _two_slot_loop(n_chunks, start, finish)

    return scatter(h2, code0, code1, base)


def _sc_collect(ys, codes, base):
    n = codes.shape[0]
    D = ys.shape[1]
    info, mesh, params = _sc_workers()
    n_workers = info.num_cores * info.num_subcores
    w = SC_WINDOW_BYTES // (D * ys.dtype.itemsize)
    per_w = n // n_workers
    n_chunks = per_w // w
    assert per_w * n_workers == n and n_chunks * w == per_w and n_chunks % 2 == 0

    @functools.partial(
        pl.kernel, out_type=jax.ShapeDtypeStruct((n, D), ys.dtype), mesh=mesh, compiler_params=params,
        scratch_types=[pltpu.VMEM((N_EXPERTS,), I32), pltpu.VMEM((w,), I32), pltpu.VMEM((w,), I32),
                       pltpu.VMEM((w,), I32), pltpu.VMEM((w, D), ys.dtype), pltpu.VMEM((w, D), ys.dtype),
                       pltpu.SemaphoreType.DMA, pltpu.SemaphoreType.DMA])
    def gather(ys_hbm, c_hbm, b_hbm, yk_hbm, base_v, code_v, i0_v, i1_v, rows0, rows1, sem0, sem1):
        wid = lax.axis_index("subcore") * info.num_cores + lax.axis_index("core")
        w0 = wid * per_w
        pltpu.sync_copy(b_hbm, base_v)
        idx = (i0_v, i1_v)
        rows = (rows0, rows1)
        sems = (sem0, sem1)

        def start(c, slot):
            pltpu.sync_copy(c_hbm.at[pl.ds(w0 + c * w, w)], code_v)
            _rows_from_codes(code_v, base_v, idx[slot], info.num_lanes)
            pltpu.async_copy(ys_hbm.at[idx[slot]], rows[slot], sems[slot])

        def finish(c, slot):
            pltpu.make_async_copy(ys_hbm.at[idx[slot]], rows[slot], sems[slot]).wait()
            pltpu.sync_copy(rows[slot], yk_hbm.at[pl.ds(w0 + c * w, w)])

        _two_slot_loop(n_chunks, start, finish)

    return gather(ys, codes, base)


def _schedule_kernel(cnt_ref, te_ref, tb_ref, base_ref, nt_ref):
    tm = EXPERT_TM

    def expert(e, t0):
        n = (cnt_ref[e] + tm - 1) // tm
        base_ref[e] = t0 * tm

        def tile(t, c):
            te_ref[t] = e
            tb_ref[t] = t
            return c

        lax.fori_loop(t0, t0 + n, tile, 0)
        return t0 + n

    nt = lax.fori_loop(0, N_EXPERTS, expert, 0)
    nt_ref[0] = nt
    last = te_ref[nt - 1]

    def idle(t, c):
        te_ref[t] = last
        tb_ref[t] = nt - 1
        return c

    lax.fori_loop(nt, te_ref.shape[0], idle, 0)


def _schedule(counts, max_tiles):
    smem = pl.BlockSpec(memory_space=pltpu.SMEM)
    return pl.pallas_call(
        _schedule_kernel,
        out_shape=[jax.ShapeDtypeStruct((max_tiles,), I32),
                   jax.ShapeDtypeStruct((max_tiles,), I32),
                   jax.ShapeDtypeStruct((N_EXPERTS,), I32),
                   jax.ShapeDtypeStruct((1,), I32)],
        in_specs=[smem],
        out_specs=[smem, smem, smem, smem],
        name="schedule",
    )(counts)


def _expert_kernel(te_ref, tb_ref, nt_ref, xs_ref, wg_ref, wu_ref, wd_ref, ys_ref):
    @pl.when(pl.program_id(0) < nt_ref[0])
    def _():
        xb = _unpack_bf16_pairs(xs_ref[...])
        g = jnp.dot(xb, wg_ref[0], preferred_element_type=F32)
        u = jnp.dot(xb, wu_ref[0], preferred_element_type=F32)
        act = (g * _sigmoid(g)) * u
        ys_ref[...] = _pack_bf16_pairs(jnp.dot(act.astype(BF16), wd_ref[0], preferred_element_type=F32))


def _experts(tile_e, tile_b, n_tiles, xs, wg, wu, wd, max_tiles):
    P = xs.shape[0]
    D, de = wg.shape[1:]
    tm = EXPERT_TM
    wmap = lambda j, te, tb, nt: (te[j], 0, 0)
    rmap = lambda j, te, tb, nt: (tb[j], 0)
    return pl.pallas_call(
        _expert_kernel,
        out_shape=jax.ShapeDtypeStruct(xs.shape, I32),
        grid_spec=pltpu.PrefetchScalarGridSpec(
            num_scalar_prefetch=3,
            grid=(max_tiles,),
            in_specs=[pl.BlockSpec((tm, xs.shape[1]), rmap),
                      pl.BlockSpec((1, D, de), wmap),
                      pl.BlockSpec((1, D, de), wmap),
                      pl.BlockSpec((1, de, D), wmap)],
            out_specs=pl.BlockSpec((tm, xs.shape[1]), rmap)),
        compiler_params=_params("arbitrary"),
        name="experts",
    )(tile_e, tile_b, n_tiles, xs, wg, wu, wd)


def _combine_kernel(x1_ref, rf_ref, mod_ref, gf_ref, y0_ref, y1_ref, out_ref, *, final_norm):
    gate2 = mod_ref[0, 5:6, :]
    w = rf_ref[...]
    moe = w[:, 0:1] * _unpack_pairs_f32(y0_ref[...]) + w[:, 1:2] * _unpack_pairs_f32(y1_ref[...])
    x2 = x1_ref[...] + gate2 * moe
    if final_norm:
        ms = jnp.mean(x2 * x2, axis=-1, keepdims=True)
        x2 = x2 * lax.rsqrt(ms + RMS_EPS) * gf_ref[...]
    out_ref[...] = x2


def _combine(x1, rf, mod3, gf, yk, seq, final_norm):
    T, D = x1.shape
    tc = COMBINE_TM
    per_b = seq // tc
    n_blk = T // tc
    return pl.pallas_call(
        functools.partial(_combine_kernel, final_norm=final_norm),
        out_shape=jax.ShapeDtypeStruct((T, D), F32),
        grid=(n_blk,),
        in_specs=[pl.BlockSpec((tc, D), lambda i: (i, 0)),
                  pl.BlockSpec((tc, LANES), lambda i: (i, 0)),
                  pl.BlockSpec((1, 6, D), lambda i: (i // per_b, 0, 0)),
                  pl.BlockSpec((1, D), lambda i: (0, 0)),
                  pl.BlockSpec((tc, yk.shape[1]), lambda i: (i, 0)),
                  pl.BlockSpec((tc, yk.shape[1]), lambda i: (n_blk + i, 0))],
        out_specs=pl.BlockSpec((tc, D), lambda i: (i, 0)),
        compiler_params=_params("arbitrary"),
        name="combine",
    )(x1, rf, mod3, gf, yk, yk)


def _layer(x2, c, seq, layer, w_ada, b_ada, g_norm1, w_in, b_if, conv_dw_w, conv_dw_b, conv_ln_g, conv_ln_b,
           w_conv_out, qk_conv_w, qk_conv_b, m_norm_g, w_m_out, w_out, g_norm2, w_rg, b_rg,
           w_re, b_re, w_e_gate, w_e_up, w_e_down):
    T, D = x2.shape
    B = T // seq
    dc = D // 2
    nif = 2 * M_HEADS

    if_lo = 6 * dc
    w_in_t = jnp.swapaxes(w_in, 1, 2)
    w_main = w_in_t[layer, :if_lo, :].astype(BF16)
    w_gates = w_in_t[layer, if_lo + nif:, :].astype(BF16)
    full = lambda w: (w, [(0, w.shape[2])])
    mod, (w_out_b, w_conv_out_b, w_m_out_b) = _ada(
        c, w_ada, b_ada, layer, [full(w_out), full(w_conv_out), full(w_m_out)])
    mod3 = mod.reshape(B, 6, D)
    w_if = w_in[layer, :, if_lo:if_lo + nif]
    w_if_pad = jnp.pad(w_if, ((0, 0), (0, LANES - nif))).astype(BF16)
    w_ift = w_if.T.astype(BF16)
    (u, qk, v, o, sga, sgb, ifc, ifr), (wg_b, wu_b, wd_b) = _inproj(
        x2, mod3, g_norm1.reshape(1, D), w_main, w_gates, w_if_pad, w_ift, seq,
        riders=(w_e_gate, w_e_up, w_e_down))

    ya = _conv_branch(
        u.reshape(B, seq, dc), conv_dw_w, conv_dw_b.reshape(1, dc), conv_ln_g.reshape(1, dc),
        conv_ln_b.reshape(1, dc), w_conv_out_b)
    bifc = jnp.pad(b_if, (0, LANES - nif)).reshape(1, LANES)
    bifr = b_if.reshape(nif, 1)
    yb = _mlstm_branch(
        qk.reshape(B, seq, 2 * dc), v.reshape(B, seq, dc), o.reshape(B, seq, dc),
        ifc.reshape(B, seq, LANES), ifr, qk_conv_w, qk_conv_b.reshape(1, 2 * dc), bifc, bifr,
        m_norm_g.reshape(1, dc), w_m_out_b)

    n_r = N_GROUPS + N_EXPERTS
    w_r = jnp.pad(jnp.concatenate([w_rg, w_re], axis=1), ((0, 0), (0, LANES - n_r))).astype(BF16)
    b_r = jnp.pad(jnp.concatenate([b_rg, b_re]), (0, LANES - n_r)).reshape(1, LANES)
    x1, h2, ri, rf, cnt = _merge(x2, ya.reshape(T, D), yb.reshape(T, D), sga, sgb, mod3,
                                 g_norm2.reshape(1, D), w_out_b, w_r, b_r, seq)

    tm = EXPERT_TM
    max_tiles = (T * TOP_K) // tm + N_EXPERTS
    tile_e, tile_b, base, n_tiles = _schedule(cnt[0, :N_EXPERTS], max_tiles)

    code0 = ri[0]
    code1 = ri[1]
    xs = _sc_dispatch(h2, code0, code1, base, max_tiles * tm)
    ys = _experts(tile_e, tile_b, n_tiles, xs, wg_b, wu_b, wd_b, max_tiles)
    return x1, rf, mod3, ys, code0, code1, base


def kernel(x, c, w_ada, b_ada, g_norm1, w_in, b_if, conv_dw_w, conv_dw_b, conv_ln_g, conv_ln_b,
           w_conv_out, qk_conv_w, qk_conv_b, m_norm_g, w_m_out, w_out, g_norm2, w_rg, b_rg,
           w_re, b_re, w_e_gate, w_e_up, w_e_down, g_final):
    B, S, D = x.shape
    depth = w_ada.shape[0]
    x2 = x.reshape(B * S, D)
    for l in range(depth):
        x1, rf, mod3, ys, code0, code1, base = _layer(
            x2, c, S, l, w_ada[l], b_ada[l], g_norm1[l], w_in, b_if[l], conv_dw_w[l], conv_dw_b[l],
            conv_ln_g[l], conv_ln_b[l], w_conv_out, qk_conv_w[l], qk_conv_b[l], m_norm_g[l],
            w_m_out, w_out, g_norm2[l], w_rg[l], b_rg[l], w_re[l], b_re[l],
            w_e_gate[l], w_e_up[l], w_e_down[l])
        yk = _sc_collect(ys, jnp.concatenate([code0, code1]), base)
        x2 = _combine(x1, rf, mod3, g_final.reshape(1, D), yk, S, final_norm=l == depth - 1)
    return x2.reshape(B, S, D)
```

```python
import dataclasses
import functools

import jax
import jax.numpy as jnp
from jax import lax
from jax.experimental import pallas as pl
from jax.experimental.pallas import tpu as pltpu
from jax.experimental.pallas import tpu_sc as plsc

F32 = jnp.float32
BF16 = jnp.bfloat16
I32 = jnp.int32

M_HEADS = 4
CONV_WIDTH = 31
QK_CONV_WIDTH = 4
N_GROUPS = 4
E_PER_GROUP = 8
N_EXPERTS = N_GROUPS * E_PER_GROUP
TOP_K = 2
RMS_EPS = 1e-6
LN_EPS = 1e-5

LANES = 128
SUBLANES = 8
VMEM_LIMIT = 56 * 1024 * 1024

ADA_TN = 768
INPROJ_TM = 512
INPROJ_SUB = 256
CONV_TS = 1024
CONV_HALO = 32
CONV_RC = 256
MLSTM_L = 128
MLSTM_SEQS = 8
MERGE_TM = 1024
MERGE_SUB = 512
EXPERT_TM = 512
SC_WINDOW_BYTES = 128 * 1024
COMBINE_TM = 1024
RANK_BITS = 16
RANK_RADIX = 1 << RANK_BITS
assert EXPERT_TM & (EXPERT_TM - 1) == 0


def _sigmoid(v):
    return 1.0 / (1.0 + jnp.exp(-v))


def _log_sigmoid(v):
    return -(jnp.maximum(-v, 0.0) + jnp.log1p(jnp.exp(-jnp.abs(v))))


def _pack_bf16_pairs(v):
    n = v.shape[1] // 2
    bits = lax.bitcast_convert_type(v.astype(BF16).astype(F32), jnp.uint32)
    word = bits[:, n:] | (bits[:, :n] >> 16)
    return lax.bitcast_convert_type(word, I32)


def _unpack_pairs_f32(w):
    bits = lax.bitcast_convert_type(w, jnp.uint32)
    lo = lax.bitcast_convert_type(bits << 16, F32)
    hi = lax.bitcast_convert_type(bits & jnp.uint32(0xFFFF0000), F32)
    return jnp.concatenate([lo, hi], axis=1)


def _unpack_bf16_pairs(w):
    return _unpack_pairs_f32(w).astype(BF16)


def _split_bf16(v):
    hi = v.astype(BF16)
    r1 = v - hi.astype(F32)
    mid = r1.astype(BF16)
    lo = (r1 - mid.astype(F32)).astype(BF16)
    return hi, mid, lo


def _params(*sem):
    return pltpu.CompilerParams(dimension_semantics=sem, vmem_limit_bytes=VMEM_LIMIT)


def _ada_kernel(*refs, cuts):
    c_ref, w_ref, b_ref = refs[:3]
    srcs = refs[3:3 + len(cuts)]
    o_ref = refs[3 + len(cuts)]
    dsts = iter(refs[4 + len(cuts):])
    c = c_ref[...]
    s = c * _sigmoid(c)
    o_ref[...] = jnp.dot(s, w_ref[...], preferred_element_type=F32,
                         precision=lax.Precision.HIGHEST) + b_ref[...]
    for src, ranges in zip(srcs, cuts):
        for lo, hi in ranges:
            next(dsts)[...] = src[:, lo:hi].astype(BF16)


def _ada(c, w_ada, b_ada, layer, riders):
    B, D = c.shape
    N = w_ada.shape[1]
    n_steps = N // ADA_TN
    slab = lambda j: (j, 0)
    outs = pl.pallas_call(
        functools.partial(_ada_kernel, cuts=tuple(tuple(r) for _, r in riders)),
        out_shape=[jax.ShapeDtypeStruct((B, N), F32)]
        + [jax.ShapeDtypeStruct((w.shape[1], hi - lo), BF16) for w, r in riders for lo, hi in r],
        grid=(n_steps,),
        in_specs=[pl.BlockSpec((B, D), lambda j: (0, 0)),
                  pl.BlockSpec((D, ADA_TN), lambda j: (0, j)),
                  pl.BlockSpec((1, ADA_TN), lambda j: (0, j))]
        + [pl.BlockSpec((None, w.shape[1] // n_steps, w.shape[2]), lambda j: (layer, j, 0))
           for w, _ in riders],
        out_specs=[pl.BlockSpec((B, ADA_TN), lambda j: (0, j))]
        + [pl.BlockSpec((w.shape[1] // n_steps, hi - lo), slab) for w, r in riders for lo, hi in r],
        compiler_params=_params("arbitrary"),
        name="ada",
    )(c, w_ada, b_ada.reshape(1, N), *[w for w, _ in riders])
    return outs[0], outs[1:]


def _inproj_kernel(*refs, n_riders):
    x_ref, mod_ref, g_ref, wm_ref, wgt_ref, wif_ref, wift_ref = refs[:7]
    rider_in = refs[7:7 + n_riders]
    u_ref, qk_ref, v_ref, o_ref, sga_ref, sgb_ref, ifc_ref, ifr_ref = refs[7 + n_riders:15 + n_riders]
    rider_out = refs[15 + n_riders:]

    for src, dst in zip(rider_in, rider_out):
        dst[...] = src[...].astype(BF16)

    shift = mod_ref[0, 0:1, :]
    scale = mod_ref[0, 1:2, :]
    dc = u_ref.shape[1]
    d = sga_ref.shape[1]
    ts = INPROJ_SUB
    subs = [pl.ds(r0, ts) for r0 in range(0, x_ref.shape[0], ts)]

    hbs = []
    for sl in subs:
        x = x_ref[sl, :]
        ms = jnp.mean(x * x, axis=-1, keepdims=True)
        h = x * lax.rsqrt(ms + RMS_EPS) * g_ref[...]
        h = h * (1.0 + scale) + shift
        hbs.append(h.astype(BF16))

    def times_t(a, wt):
        return lax.dot_general(a, wt, (((1,), (1,)), ((), ())), preferred_element_type=F32)

    for sl, hb in zip(subs, hbs):
        def seg(lo, hi):
            return times_t(hb, wm_ref[lo:hi, :])

        u_ref[sl, :] = seg(0, dc) * _sigmoid(seg(dc, 2 * dc))
        qk_ref[sl, :] = seg(2 * dc, 4 * dc)
        v_ref[sl, :] = seg(4 * dc, 5 * dc).astype(BF16)
        o_ref[sl, :] = seg(5 * dc, 6 * dc)
        sga_ref[sl, :] = _sigmoid(times_t(hb, wgt_ref[0:d, :])).astype(BF16)
        sgb_ref[sl, :] = _sigmoid(times_t(hb, wgt_ref[d:2 * d, :])).astype(BF16)
        ifc_ref[sl, :] = jnp.dot(hb, wif_ref[...], preferred_element_type=F32)
        ifr_ref[0, :, sl] = lax.dot_general(wift_ref[...], hb, (((1,), (1,)), ((), ())),
                                            preferred_element_type=F32)


def _inproj(x2, mod3, g1, w_main, w_gates, w_if, w_ift, seq, riders):
    T, D = x2.shape
    tm = INPROJ_TM
    dc = D // 2
    per_b = seq // tm
    row = lambda i: (i, 0)
    const = lambda i: (0, 0)
    slab = lambda i: (i, 0, 0)
    slabs = [r.reshape(T // tm, -1, r.shape[-1]) for r in riders]
    once = pl.Buffered(1)
    outs = pl.pallas_call(
        functools.partial(_inproj_kernel, n_riders=len(riders)),
        out_shape=[jax.ShapeDtypeStruct((T, dc), F32),
                   jax.ShapeDtypeStruct((T, 2 * dc), F32),
                   jax.ShapeDtypeStruct((T, dc), BF16),
                   jax.ShapeDtypeStruct((T, dc), F32),
                   jax.ShapeDtypeStruct((T, D), BF16),
                   jax.ShapeDtypeStruct((T, D), BF16),
                   jax.ShapeDtypeStruct((T, LANES), F32),
                   jax.ShapeDtypeStruct((T // seq, SUBLANES, seq), F32)]
        + [jax.ShapeDtypeStruct(s.shape, BF16) for s in slabs],
        grid=(T // tm,),
        in_specs=[pl.BlockSpec((tm, D), row),
                  pl.BlockSpec((1, 6, D), lambda i: (i // per_b, 0, 0)),
                  pl.BlockSpec((1, D), const),
                  pl.BlockSpec(w_main.shape, const, pipeline_mode=once),
                  pl.BlockSpec(w_gates.shape, const, pipeline_mode=once),
                  pl.BlockSpec(w_if.shape, const),
                  pl.BlockSpec(w_ift.shape, const)]
        + [pl.BlockSpec((1,) + s.shape[1:], slab) for s in slabs],
        out_specs=[pl.BlockSpec((tm, dc), row),
                   pl.BlockSpec((tm, 2 * dc), row),
                   pl.BlockSpec((tm, dc), row),
                   pl.BlockSpec((tm, dc), row),
                   pl.BlockSpec((tm, D), row),
                   pl.BlockSpec((tm, D), row),
                   pl.BlockSpec((tm, LANES), row),
                   pl.BlockSpec((1, SUBLANES, tm), lambda i: (i // per_b, 0, i % per_b))]
        + [pl.BlockSpec((1,) + s.shape[1:], slab) for s in slabs],
        compiler_params=_params("arbitrary"),
        name="inproj",
    )(x2, mod3, g1, w_main, w_gates, w_if, w_ift, *slabs)
    return outs[:8], [o.reshape(r.shape) for o, r in zip(outs[8:], riders)]


def _conv_kernel(u_ref, w_ref, b_ref, lg_ref, lb_ref, wo_ref, y_ref, ubuf, sbuf, cbuf):
    ts = u_ref.shape[1]
    halo = CONV_HALO

    @pl.when(pl.program_id(1) == 0)
    def _():
        ubuf[0:halo, :] = jnp.zeros((halo, ubuf.shape[1]), F32)

    ubuf[halo:halo + ts, :] = u_ref[0]
    ns = sbuf.shape[1]
    for r in range(1, SUBLANES):
        sbuf[r - 1] = ubuf[r:r + ns, :]
    off = halo - (CONV_WIDTH - 1)
    for r0 in range(0, ts, CONV_RC):
        acc = jnp.broadcast_to(b_ref[...], (CONV_RC, ubuf.shape[1]))
        for k in range(CONV_WIDTH):
            r = (off + k) % SUBLANES
            lo = off + k - r + r0
            win = ubuf[lo:lo + CONV_RC, :] if r == 0 else sbuf[r - 1, lo:lo + CONV_RC, :]
            acc = acc + w_ref[k:k + 1, :] * win
        cbuf[r0:r0 + CONV_RC, :] = acc
    ubuf[0:halo, :] = ubuf[ts:ts + halo, :]

    a = cbuf[...]
    mu = jnp.mean(a, axis=-1, keepdims=True)
    ac = a - mu
    var = jnp.mean(ac * ac, axis=-1, keepdims=True)
    z = ac * lax.rsqrt(var + LN_EPS) * lg_ref[...] + lb_ref[...]
    z = z * _sigmoid(z)
    y_ref[0] = jnp.dot(z.astype(BF16), wo_ref[...], preferred_element_type=F32).astype(BF16)


def _conv_branch(u3, w, b, lg, lb, wo):
    B, S, C = u3.shape
    D = wo.shape[1]
    ts = CONV_TS
    const = lambda bi, si: (0, 0)
    return pl.pallas_call(
        _conv_kernel,
        out_shape=jax.ShapeDtypeStruct((B, S, D), BF16),
        grid=(B, S // ts),
        in_specs=[pl.BlockSpec((1, ts, C), lambda bi, si: (bi, si, 0)),
                  pl.BlockSpec(w.shape, const),
                  pl.BlockSpec((1, C), const),
                  pl.BlockSpec((1, C), const),
                  pl.BlockSpec((1, C), const),
                  pl.BlockSpec(wo.shape, const)],
        out_specs=pl.BlockSpec((1, ts, D), lambda bi, si: (bi, si, 0)),
        scratch_shapes=[pltpu.VMEM((ts + CONV_HALO, C), F32),
                        pltpu.VMEM((SUBLANES - 1, ts + CONV_HALO - SUBLANES, C), F32),
                        pltpu.VMEM((ts, C), F32)],
        compiler_params=_params("arbitrary", "arbitrary"),
        name="conv",
    )(u3, w, b, lg, lb, wo)


def _mlstm_kernel(qk_ref, v_ref, o_ref, ifc_ref, ifr_ref, cw_ref, cb_ref, bifc_ref, bifr_ref, ng_ref,
                  wo_ref, y_ref, qkbuf, cn_ref, m_ref, hbuf):
    @pl.when(pl.program_id(1) == 0)
    def _():
        qkbuf[:, 0:SUBLANES, :] = jnp.zeros((qkbuf.shape[0], SUBLANES, qkbuf.shape[2]), F32)
        cn_ref[...] = jnp.zeros(cn_ref.shape, F32)
        m_ref[...] = jnp.zeros(m_ref.shape, F32)

    nb, L, mi = hbuf.shape
    dh = mi // M_HEADS
    halo = SUBLANES
    off = halo - (QK_CONV_WIDTH - 1)
    rows = lax.broadcasted_iota(I32, (L, L), 0)
    cols = lax.broadcasted_iota(I32, (L, L), 1)
    causal = cols <= rows
    lower = jnp.where(causal, 1.0, 0.0).astype(BF16)
    upper = jnp.where(rows <= cols, 1.0, 0.0).astype(BF16)
    lane = lax.broadcasted_iota(I32, (L, dh), 1)
    ones_col = jnp.where(lane == 0, 1.0, 0.0).astype(BF16)
    scale = dh ** -0.5

    seqs = []
    for b in range(nb):
        qkbuf[b, halo:halo + L, :] = qk_ref[b]
        y = jnp.broadcast_to(cb_ref[...], (L, qkbuf.shape[2]))
        for k in range(QK_CONV_WIDTH):
            y = y + cw_ref[k:k + 1, :] * qkbuf[b, off + k:off + k + L, :]
        y = y * _sigmoid(y)
        qkbuf[b, 0:halo, :] = qkbuf[b, L:L + halo, :]
        ifr = ifr_ref[b] + bifr_ref[...]
        ifc = ifc_ref[b] + bifc_ref[...]
        bcum_c = sum(jnp.dot(lower, p, preferred_element_type=F32) for p in _split_bf16(_log_sigmoid(ifc)))
        bcum_r = sum(jnp.dot(p, upper, preferred_element_type=F32) for p in _split_bf16(_log_sigmoid(ifr)))
        seqs.append((y, ifr, bcum_c, bcum_r))

    probs = [(b, hd) for b in range(nb) for hd in range(M_HEADS)]
    st = {}
    for p in probs:
        b, hd = p
        y, ifr, bcum_c, bcum_r = seqs[b]
        c0 = hd * dh
        qb = (y[:, c0:c0 + dh] * scale).astype(BF16)
        kt = y[:, mi + c0:mi + c0 + dh].T
        v = v_ref[b, :, c0:c0 + dh]
        bc = bcum_c[:, M_HEADS + hd:M_HEADS + hd + 1]
        br = bcum_r[M_HEADS + hd:M_HEADS + hd + 1, :]
        li = ifr[hd:hd + 1, :]
        m_prev = m_ref[b, hd, 0:1, 0:1]
        dmat = jnp.where(causal, bc - br + li, -jnp.inf)
        st[p] = dict(qb=qb, kt=kt, v=v, bc=bc, br=br, li=li, m_prev=m_prev, dmat=dmat)
    for p in probs:
        s = st[p]
        s["inter"] = s["bc"] + s["m_prev"]
        s["m_t"] = jnp.maximum(jnp.max(s["dmat"], axis=-1, keepdims=True), s["inter"])
    for p in probs:
        s = st[p]
        s["qk"] = jnp.dot(s["qb"], s["kt"].astype(BF16), preferred_element_type=F32)
    for p in probs:
        b, hd = p
        s = st[p]
        s["cn"] = cn_ref[b, hd]
        s["qcn"] = jnp.dot(s["qb"], s["cn"].astype(BF16), preferred_element_type=F32)
    for p in probs:
        s = st[p]
        s["wts"] = jnp.exp(s["dmat"] - s["m_t"])
        s["s_inter"] = jnp.exp(s["inter"] - s["m_t"])
    for p in probs:
        s = st[p]
        s["s_mat"] = s["qk"] * s["wts"]
    for p in probs:
        s = st[p]
        s["sv"] = jnp.dot(s["s_mat"].astype(BF16), s["v"], preferred_element_type=F32)
    for p in probs:
        s = st[p]
        s["rowsum"] = jnp.sum(s["s_mat"], axis=-1, keepdims=True)
    for p in probs:
        s = st[p]
        s["num"] = s["sv"] + s["s_inter"] * s["qcn"][:, 0:dh]
        s["den"] = s["rowsum"] + s["s_inter"] * s["qcn"][:, dh:dh + 1]
    for p in probs:
        b, hd = p
        s = st[p]
        b_last = s["br"][:, L - 1:L]
        a = b_last - s["br"] + s["li"]
        m_new = jnp.maximum(b_last + s["m_prev"], jnp.max(a, axis=-1, keepdims=True))
        wk = jnp.exp(a - m_new)
        sc = jnp.exp(b_last + s["m_prev"] - m_new)
        v_ext = jnp.concatenate([s["v"], ones_col], axis=1)
        cn_ref[b, hd] = sc * s["cn"] + jnp.dot((s["kt"] * wk).astype(BF16), v_ext, preferred_element_type=F32)
        m_ref[b, hd] = jnp.broadcast_to(m_new, m_ref.shape[2:])
    for p in probs:
        s = st[p]
        s["hh"] = s["num"] / jnp.maximum(jnp.abs(s["den"]), jnp.exp(-s["m_t"]))
        s["mu"] = jnp.mean(s["hh"], axis=-1, keepdims=True)
    for p in probs:
        s = st[p]
        s["hc"] = s["hh"] - s["mu"]
        s["var"] = jnp.mean(s["hc"] * s["hc"], axis=-1, keepdims=True)
    for p in probs:
        b, hd = p
        s = st[p]
        c0 = hd * dh
        hn = s["hc"] * lax.rsqrt(s["var"] + LN_EPS) * ng_ref[:, c0:c0 + dh]
        hbuf[b, :, c0:c0 + dh] = hn * _sigmoid(o_ref[b, :, c0:c0 + dh])
    for b in range(nb):
        y = jnp.dot(hbuf[b].astype(BF16), wo_ref[...], preferred_element_type=F32)
        y_ref[b] = y.astype(BF16)


def _mlstm_branch(qk3, v3, o3, ifc3, ifr3, cw, cb, bifc, bifr, ng, wo):
    B, S, C2 = qk3.shape
    mi = v3.shape[2]
    dh = mi // M_HEADS
    D = wo.shape[1]
    L = MLSTM_L
    nb = MLSTM_SEQS
    const = lambda bi, ci: (0, 0)
    tile = lambda bi, ci: (bi, ci, 0)
    return pl.pallas_call(
        _mlstm_kernel,
        out_shape=jax.ShapeDtypeStruct((B, S, D), BF16),
        grid=(B // nb, S // L),
        in_specs=[pl.BlockSpec((nb, L, C2), tile),
                  pl.BlockSpec((nb, L, mi), tile),
                  pl.BlockSpec((nb, L, mi), tile),
                  pl.BlockSpec((nb, L, LANES), tile),
                  pl.BlockSpec((nb, SUBLANES, L), lambda bi, ci: (bi, 0, ci)),
                  pl.BlockSpec(cw.shape, const),
                  pl.BlockSpec((1, C2), const),
                  pl.BlockSpec((1, LANES), const),
                  pl.BlockSpec((SUBLANES, 1), const),
                  pl.BlockSpec((1, mi), const),
                  pl.BlockSpec(wo.shape, const)],
        out_specs=pl.BlockSpec((nb, L, D), tile),
        scratch_shapes=[pltpu.VMEM((nb, L + SUBLANES, C2), F32),
                        pltpu.VMEM((nb, M_HEADS, dh, 2 * dh), F32),
                        pltpu.VMEM((nb, M_HEADS, SUBLANES, LANES), F32),
                        pltpu.VMEM((nb, L, mi), F32)],
        compiler_params=_params("arbitrary", "arbitrary"),
        name="mlstm",
    )(qk3, v3, o3, ifc3, ifr3, cw, cb, bifc, bifr, ng, wo)


def _merge_kernel(x_ref, ya_ref, yb_ref, sga_ref, sgb_ref, mod_ref, g2_ref, wo_ref, wr_ref, br_ref,
                  x1_ref, h2_ref, ri_ref, rf_ref, cnt_ref, run_ref):
    ts = MERGE_SUB
    subs = [pl.ds(r0, ts) for r0 in range(0, x_ref.shape[0], ts)]

    @pl.when(pl.program_id(0) == 0)
    def _():
        run_ref[...] = jnp.zeros(run_ref.shape, F32)

    gate1 = mod_ref[0, 2:3, :]
    shift2 = mod_ref[0, 3:4, :]
    scale2 = mod_ref[0, 4:5, :]
    lane = lax.broadcasted_iota(I32, (ts, LANES), 1).astype(F32)
    neg = -jnp.inf
    rows = lax.broadcasted_iota(I32, (ts, ts), 0)
    cols = lax.broadcasted_iota(I32, (ts, ts), 1)
    strict = jnp.where(cols < rows, 1.0, 0.0).astype(BF16)

    def first_argmax(vals):
        mx = jnp.max(vals, axis=-1, keepdims=True)
        idx = jnp.min(jnp.where(vals == mx, lane, float(LANES)), axis=-1, keepdims=True)
        return mx, idx

    h2s = []
    for sl in subs:
        merged = (sga_ref[sl, :].astype(F32) * ya_ref[sl, :].astype(F32)
                  + sgb_ref[sl, :].astype(F32) * yb_ref[sl, :].astype(F32))
        mix = jnp.dot(merged.astype(BF16), wo_ref[...], preferred_element_type=F32)
        x1 = x_ref[sl, :] + gate1 * mix
        x1_ref[sl, :] = x1
        ms = jnp.mean(x1 * x1, axis=-1, keepdims=True)
        h2 = x1 * lax.rsqrt(ms + RMS_EPS) * g2_ref[...]
        h2 = h2 * (1.0 + scale2) + shift2
        h2_ref[sl, :] = _pack_bf16_pairs(h2)
        h2s.append(h2.astype(BF16))

    run = run_ref[0:1, :]
    for sl, h2b in zip(subs, h2s):
        logits = jnp.dot(h2b, wr_ref[...], preferred_element_type=F32) + br_ref[...]
        lg = jnp.where(lane < N_GROUPS, logits, neg)
        gmax, gsel = first_argmax(lg)
        p_g = 1.0 / jnp.sum(jnp.exp(lg - gmax), axis=-1, keepdims=True)
        lo = N_GROUPS + gsel * E_PER_GROUP
        le = jnp.where((lane >= lo) & (lane < lo + E_PER_GROUP), logits, neg)
        l1, i1 = first_argmax(le)
        l2, i2 = first_argmax(jnp.where(lane == i1, neg, le))
        r = jnp.exp(l2 - l1)
        w1 = p_g / (1.0 + r)
        w2 = p_g * r / (1.0 + r)
        e1 = i1 - N_GROUPS
        e2 = i2 - N_GROUPS

        onehot = jnp.where((lane == e1) | (lane == e2), 1.0, 0.0)
        before = jnp.dot(strict, onehot.astype(BF16), preferred_element_type=F32) + run
        rank1 = jnp.sum(jnp.where(lane == e1, before, 0.0), axis=-1, keepdims=True)
        rank2 = jnp.sum(jnp.where(lane == e2, before, 0.0), axis=-1, keepdims=True)
        run = run + jnp.sum(onehot, axis=0, keepdims=True)

        codes = jnp.where(lane == 0, e1 * float(RANK_RADIX) + rank1,
                          jnp.where(lane == 1, e2 * float(RANK_RADIX) + rank2, 0.0))
        ri_ref[:, sl] = codes.T[0:SUBLANES, :].astype(I32)
        rf_ref[sl, :] = jnp.where(lane == 0, w1, jnp.where(lane == 1, w2, 0.0))
    run_ref[...] = jnp.broadcast_to(run, run_ref.shape)
    cnt_ref[...] = jnp.broadcast_to(run, cnt_ref.shape).astype(I32)


def _merge(x2, ya, yb, sga, sgb, mod3, g2, wo, wr, br, seq):
    T, D = x2.shape
    tm = MERGE_TM
    per_b = seq // tm
    row = lambda i: (i, 0)
    const = lambda i: (0, 0)
    return pl.pallas_call(
        _merge_kernel,
        out_shape=[jax.ShapeDtypeStruct((T, D), F32),
                   jax.ShapeDtypeStruct((T, D // 2), I32),
                   jax.ShapeDtypeStruct((SUBLANES, T), I32),
                   jax.ShapeDtypeStruct((T, LANES), F32),
                   jax.ShapeDtypeStruct((SUBLANES, LANES), I32)],
        grid=(T // tm,),
        in_specs=[pl.BlockSpec((tm, D), row),
                  pl.BlockSpec((tm, D), row),
                  pl.BlockSpec((tm, D), row),
                  pl.BlockSpec((tm, D), row),
                  pl.BlockSpec((tm, D), row),
                  pl.BlockSpec((1, 6, D), lambda i: (i // per_b, 0, 0)),
                  pl.BlockSpec((1, D), const),
                  pl.BlockSpec(wo.shape, const),
                  pl.BlockSpec(wr.shape, const),
                  pl.BlockSpec((1, LANES), const)],
        out_specs=[pl.BlockSpec((tm, D), row),
                   pl.BlockSpec((tm, D // 2), row),
                   pl.BlockSpec((SUBLANES, tm), lambda i: (0, i)),
                   pl.BlockSpec((tm, LANES), row),
                   pl.BlockSpec((SUBLANES, LANES), const)],
        scratch_shapes=[pltpu.VMEM((SUBLANES, LANES), F32)],
        compiler_params=_params("arbitrary"),
        name="merge",
    )(x2, ya, yb, sga, sgb, mod3, g2, wo, wr, br)


def _sc_workers():
    info = plsc.get_sparse_core_info()
    mesh = plsc.VectorSubcoreMesh(core_axis_name="core", subcore_axis_name="subcore")
    params = pltpu.CompilerParams()
    if "needs_layout_passes" in pltpu.CompilerParams.__dataclass_fields__:
        params = dataclasses.replace(params, needs_layout_passes=False)
    return info, mesh, params


def _rows_from_codes(code_v, base_v, idx_v, lanes):
    for j in range(code_v.shape[0] // lanes):
        c = code_v[pl.ds(j * lanes, lanes)]
        expert = lax.shift_right_logical(c, RANK_BITS)
        idx_v[pl.ds(j * lanes, lanes)] = plsc.load_gather(base_v, [expert]) + (c & (RANK_RADIX - 1))


def _two_slot_loop(n_chunks, start, finish):
    start(0, 0)

    @pl.loop(0, n_chunks, step=2)
    def _(c):
        start(c + 1, 1)
        finish(c, 0)

        @pl.when(c + 2 < n_chunks)
        def _():
            start(c + 2, 0)

        finish(c + 1, 1)


def _sc_dispatch(h2, code0, code1, base, n_rows):
    T, D = h2.shape
    info, mesh, params = _sc_workers()
    n_workers = info.num_cores * info.num_subcores
    w = SC_WINDOW_BYTES // (D * h2.dtype.itemsize)
    per_w = T // n_workers
    n_chunks = per_w // w
    assert per_w * n_workers == T and n_chunks * w == per_w and n_chunks % 2 == 0

    @functools.partial(
        pl.kernel, out_type=jax.ShapeDtypeStruct((n_rows, D), h2.dtype), mesh=mesh, compiler_params=params,
        scratch_types=[pltpu.VMEM((N_EXPERTS,), I32), pltpu.VMEM((w,), I32), pltpu.VMEM((w,), I32),
                       pltpu.VMEM((w,), I32), pltpu.VMEM((w, D), h2.dtype), pltpu.VMEM((w, D), h2.dtype),
                       pltpu.SemaphoreType.DMA, pltpu.SemaphoreType.DMA])
    def scatter(h_hbm, c0_hbm, c1_hbm, b_hbm, xs_hbm, base_v, code_v, i0_v, i1_v, rows0, rows1, sem0, sem1):
        wid = lax.axis_index("subcore") * info.num_cores + lax.axis_index("core")
        w0 = wid * per_w
        pltpu.sync_copy(b_hbm, base_v)
        rows = (rows0, rows1)
        sems = (sem0, sem1)

        def start(c, slot):
            pltpu.async_copy(h_hbm.at[pl.ds(w0 + c * w, w)], rows[slot], sems[slot])

        def finish(c, slot):
            pltpu.sync_copy(c0_hbm.at[pl.ds(w0 + c * w, w)], code_v)
            _rows_from_codes(code_v, base_v, i0_v, info.num_lanes)
            pltpu.sync_copy(c1_hbm.at[pl.ds(w0 + c * w, w)], code_v)
            _rows_from_codes(code_v, base_v, i1_v, info.num_lanes)
            pltpu.make_async_copy(h_hbm.at[pl.ds(w0 + c * w, w)], rows[slot], sems[slot]).wait()
            pltpu.sync_copy(rows[slot], xs_hbm.at[i0_v])
            pltpu.sync_copy(rows[slot], xs_hbm.at[i1_v])

        _two_slot_loop(n_chunks, start, finish)

    return scatter(h2, code0, code1, base)


def _sc_collect(ys, codes, base):
    n = codes.shape[0]
    D = ys.shape[1]
    info, mesh, params = _sc_workers()
    n_workers = info.num_cores * info.num_subcores
    w = SC_WINDOW_BYTES // (D * ys.dtype.itemsize)
    per_w = n // n_workers
    n_chunks = per_w // w
    assert per_w * n_workers == n and n_chunks * w == per_w and n_chunks % 2 == 0

    @functools.partial(
        pl.kernel, out_type=jax.ShapeDtypeStruct((n, D), ys.dtype), mesh=mesh, compiler_params=params,
        scratch_types=[pltpu.VMEM((N_EXPERTS,), I32), pltpu.VMEM((w,), I32), pltpu.VMEM((w,), I32),
                       pltpu.VMEM((w,), I32), pltpu.VMEM((w, D), ys.dtype), pltpu.VMEM((w, D), ys.dtype),
                       pltpu.SemaphoreType.DMA, pltpu.SemaphoreType.DMA])
    def gather(ys_hbm, c_hbm, b_hbm, yk_hbm, base_v, code_v, i0_v, i1_v, rows0, rows1, sem0, sem1):
        wid = lax.axis_index("subcore") * info.num_cores + lax.axis_index("core")
        w0 = wid * per_w
        pltpu.sync_copy(b_hbm, base_v)
        idx = (i0_v, i1_v)
        rows = (rows0, rows1)
        sems = (sem0, sem1)

        def start(c, slot):
            pltpu.sync_copy(c_hbm.at[pl.ds(w0 + c * w, w)], code_v)
            _rows_from_codes(code_v, base_v, idx[slot], info.num_lanes)
            pltpu.async_copy(ys_hbm.at[idx[slot]], rows[slot], sems[slot])

        def finish(c, slot):
            pltpu.make_async_copy(ys_hbm.at[idx[slot]], rows[slot], sems[slot]).wait()
            pltpu.sync_copy(rows[slot], yk_hbm.at[pl.ds(w0 + c * w, w)])

        _two_slot_loop(n_chunks, start, finish)

    return gather(ys, codes, base)


def _schedule_kernel(cnt_ref, te_ref, tb_ref, base_ref, nt_ref):
    tm = EXPERT_TM

    def expert(e, t0):
        n = (cnt_ref[e] + tm - 1) // tm
        base_ref[e] = t0 * tm

        def tile(t, c):
            te_ref[t] = e
            tb_ref[t] = t
            return c

        lax.fori_loop(t0, t0 + n, tile, 0)
        return t0 + n

    nt = lax.fori_loop(0, N_EXPERTS, expert, 0)
    nt_ref[0] = nt
    last = te_ref[nt - 1]

    def idle(t, c):
        te_ref[t] = last
        tb_ref[t] = nt - 1
        return c

    lax.fori_loop(nt, te_ref.shape[0], idle, 0)


def _schedule(counts, max_tiles):
    smem = pl.BlockSpec(memory_space=pltpu.SMEM)
    return pl.pallas_call(
        _schedule_kernel,
        out_shape=[jax.ShapeDtypeStruct((max_tiles,), I32),
                   jax.ShapeDtypeStruct((max_tiles,), I32),
                   jax.ShapeDtypeStruct((N_EXPERTS,), I32),
                   jax.ShapeDtypeStruct((1,), I32)],
        in_specs=[smem],
        out_specs=[smem, smem, smem, smem],
        name="schedule",
    )(counts)


def _expert_kernel(te_ref, tb_ref, nt_ref, xs_ref, wg_ref, wu_ref, wd_ref, ys_ref):
    @pl.when(pl.program_id(0) < nt_ref[0])
    def _():
        xb = _unpack_bf16_pairs(xs_ref[...])
        g = jnp.dot(xb, wg_ref[0], preferred_element_type=F32)
        u = jnp.dot(xb, wu_ref[0], preferred_element_type=F32)
        act = (g * _sigmoid(g)) * u
        ys_ref[...] = _pack_bf16_pairs(jnp.dot(act.astype(BF16), wd_ref[0], preferred_element_type=F32))


def _experts(tile_e, tile_b, n_tiles, xs, wg, wu, wd, max_tiles):
    P = xs.shape[0]
    D, de = wg.shape[1:]
    tm = EXPERT_TM
    wmap = lambda j, te, tb, nt: (te[j], 0, 0)
    rmap = lambda j, te, tb, nt: (tb[j], 0)
    return pl.pallas_call(
        _expert_kernel,
        out_shape=jax.ShapeDtypeStruct(xs.shape, I32),
        grid_spec=pltpu.PrefetchScalarGridSpec(
            num_scalar_prefetch=3,
            grid=(max_tiles,),
            in_specs=[pl.BlockSpec((tm, xs.shape[1]), rmap),
                      pl.BlockSpec((1, D, de), wmap),
                      pl.BlockSpec((1, D, de), wmap),
                      pl.BlockSpec((1, de, D), wmap)],
            out_specs=pl.BlockSpec((tm, xs.shape[1]), rmap)),
        compiler_params=_params("arbitrary"),
        name="experts",
    )(tile_e, tile_b, n_tiles, xs, wg, wu, wd)


def _combine_kernel(x1_ref, rf_ref, mod_ref, gf_ref, y0_ref, y1_ref, out_ref, *, final_norm):
    gate2 = mod_ref[0, 5:6, :]
    w = rf_ref[...]
    moe = w[:, 0:1] * _unpack_pairs_f32(y0_ref[...]) + w[:, 1:2] * _unpack_pairs_f32(y1_ref[...])
    x2 = x1_ref[...] + gate2 * moe
    if final_norm:
        ms = jnp.mean(x2 * x2, axis=-1, keepdims=True)
        x2 = x2 * lax.rsqrt(ms + RMS_EPS) * gf_ref[...]
    out_ref[...] = x2


def _combine(x1, rf, mod3, gf, yk, seq, final_norm):
    T, D = x1.shape
    tc = COMBINE_TM
    per_b = seq // tc
    n_blk = T // tc
    return pl.pallas_call(
        functools.partial(_combine_kernel, final_norm=final_norm),
        out_shape=jax.ShapeDtypeStruct((T, D), F32),
        grid=(n_blk,),
        in_specs=[pl.BlockSpec((tc, D), lambda i: (i, 0)),
                  pl.BlockSpec((tc, LANES), lambda i: (i, 0)),
                  pl.BlockSpec((1, 6, D), lambda i: (i // per_b, 0, 0)),
                  pl.BlockSpec((1, D), lambda i: (0, 0)),
                  pl.BlockSpec((tc, yk.shape[1]), lambda i: (i, 0)),
                  pl.BlockSpec((tc, yk.shape[1]), lambda i: (n_blk + i, 0))],
        out_specs=pl.BlockSpec((tc, D), lambda i: (i, 0)),
        compiler_params=_params("arbitrary"),
        name="combine",
    )(x1, rf, mod3, gf, yk, yk)


def _layer(x2, c, seq, layer, w_ada, b_ada, g_norm1, w_in, b_if, conv_dw_w, conv_dw_b, conv_ln_g, conv_ln_b,
           w_conv_out, qk_conv_w, qk_conv_b, m_norm_g, w_m_out, w_out, g_norm2, w_rg, b_rg,
           w_re, b_re, w_e_gate, w_e_up, w_e_down):
    T, D = x2.shape
    B = T // seq
    dc = D // 2
    nif = 2 * M_HEADS

    if_lo = 6 * dc
    w_in_t = jnp.swapaxes(w_in, 1, 2)
    w_main = w_in_t[layer, :if_lo, :].astype(BF16)
    w_gates = w_in_t[layer, if_lo + nif:, :].astype(BF16)
    full = lambda w: (w, [(0, w.shape[2])])
    mod, (w_out_b, w_conv_out_b, w_m_out_b) = _ada(
        c, w_ada, b_ada, layer, [full(w_out), full(w_conv_out), full(w_m_out)])
    mod3 = mod.reshape(B, 6, D)
    w_if = w_in[layer, :, if_lo:if_lo + nif]
    w_if_pad = jnp.pad(w_if, ((0, 0), (0, LANES - nif))).astype(BF16)
    w_ift = w_if.T.astype(BF16)
    (u, qk, v, o, sga, sgb, ifc, ifr), (wg_b, wu_b, wd_b) = _inproj(
        x2, mod3, g_norm1.reshape(1, D), w_main, w_gates, w_if_pad, w_ift, seq,
        riders=(w_e_gate, w_e_up, w_e_down))

    ya = _conv_branch(
        u.reshape(B, seq, dc), conv_dw_w, conv_dw_b.reshape(1, dc), conv_ln_g.reshape(1, dc),
        conv_ln_b.reshape(1, dc), w_conv_out_b)
    bifc = jnp.pad(b_if, (0, LANES - nif)).reshape(1, LANES)
    bifr = b_if.reshape(nif, 1)
    yb = _mlstm_branch(
        qk.reshape(B, seq, 2 * dc), v.reshape(B, seq, dc), o.reshape(B, seq, dc),
        ifc.reshape(B, seq, LANES), ifr, qk_conv_w, qk_conv_b.reshape(1, 2 * dc), bifc, bifr,
        m_norm_g.reshape(1, dc), w_m_out_b)

    n_r = N_GROUPS + N_EXPERTS
    w_r = jnp.pad(jnp.concatenate([w_rg, w_re], axis=1), ((0, 0), (0, LANES - n_r))).astype(BF16)
    b_r = jnp.pad(jnp.concatenate([b_rg, b_re]), (0, LANES - n_r)).reshape(1, LANES)
    x1, h2, ri, rf, cnt = _merge(x2, ya.reshape(T, D), yb.reshape(T, D), sga, sgb, mod3,
                                 g_norm2.reshape(1, D), w_out_b, w_r, b_r, seq)

    tm = EXPERT_TM
    max_tiles = (T * TOP_K) // tm + N_EXPERTS
    tile_e, tile_b, base, n_tiles = _schedule(cnt[0, :N_EXPERTS], max_tiles)

    code0 = ri[0]
    code1 = ri[1]
    xs = _sc_dispatch(h2, code0, code1, base, max_tiles * tm)
    ys = _experts(tile_e, tile_b, n_tiles, xs, wg_b, wu_b, wd_b, max_tiles)
    return x1, rf, mod3, ys, code0, code1, base


def kernel(x, c, w_ada, b_ada, g_norm1, w_in, b_if, conv_dw_w, conv_dw_b, conv_ln_g, conv_ln_b,
           w_conv_out, qk_conv_w, qk_conv_b, m_norm_g, w_m_out, w_out, g_norm2, w_rg, b_rg,
           w_re, b_re, w_e_gate, w_e_up, w_e_down, g_final):
    B, S, D = x.shape
    depth = w_ada.shape[0]
    x2 = x.reshape(B * S, D)
    for l in range(depth):
        x1, rf, mod3, ys, code0, code1, base = _layer(
            x2, c, S, l, w_ada[l], b_ada[l], g_norm1[l], w_in, b_if[l], conv_dw_w[l], conv_dw_b[l],
            conv_ln_g[l], conv_ln_b[l], w_conv_out, qk_conv_w[l], qk_conv_b[l], m_norm_g[l],
            w_m_out, w_out, g_norm2[l], w_rg[l], b_rg[l], w_re[l], b_re[l],
            w_e_gate[l], w_e_up[l], w_e_down[l])
        yk = _sc_collect(ys, jnp.concatenate([code0, code1]), base)
        x2 = _combine(x1, rf, mod3, g_final.reshape(1, D), yk, S, final_norm=l == depth - 1)
    return x2.reshape(B, S, D)
```

```python
import dataclasses
import functools

import jax
import jax.numpy as jnp
from jax import lax
from jax.experimental import pallas as pl
from jax.experimental.pallas import tpu as pltpu
from jax.experimental.pallas import tpu_sc as plsc

F32 = jnp.float32
BF16 = jnp.bfloat16
I32 = jnp.int32

M_HEADS = 4
CONV_WIDTH = 31
QK_CONV_WIDTH = 4
N_GROUPS = 4
E_PER_GROUP = 8
N_EXPERTS = N_GROUPS * E_PER_GROUP
TOP_K = 2
RMS_EPS = 1e-6
LN_EPS = 1e-5

LANES = 128
SUBLANES = 8
VMEM_LIMIT = 56 * 1024 * 1024

ADA_TN = 768
INPROJ_TM = 512
INPROJ_SUB = 256
CONV_TS = 1024
CONV_HALO = 32
CONV_RC = 512
MLSTM_L = 128
MLSTM_SEQS = 8
MERGE_TM = 1024
MERGE_SUB = 512
EXPERT_TM = 512
SC_WINDOW_BYTES = 128 * 1024
COMBINE_TM = 1024
RANK_BITS = 16
RANK_RADIX = 1 << RANK_BITS
assert EXPERT_TM & (EXPERT_TM - 1) == 0


def _sigmoid(v):
    return 1.0 / (1.0 + jnp.exp(-v))


def _log_sigmoid(v):
    return -(jnp.maximum(-v, 0.0) + jnp.log1p(jnp.exp(-jnp.abs(v))))


def _pack_bf16_pairs(v):
    n = v.shape[1] // 2
    bits = lax.bitcast_convert_type(v.astype(BF16).astype(F32), jnp.uint32)
    word = bits[:, n:] | (bits[:, :n] >> 16)
    return lax.bitcast_convert_type(word, I32)


def _unpack_pairs_f32(w):
    bits = lax.bitcast_convert_type(w, jnp.uint32)
    lo = lax.bitcast_convert_type(bits << 16, F32)
    hi = lax.bitcast_convert_type(bits & jnp.uint32(0xFFFF0000), F32)
    return jnp.concatenate([lo, hi], axis=1)


def _unpack_bf16_pairs(w):
    return _unpack_pairs_f32(w).astype(BF16)


def _split_bf16(v):
    hi = v.astype(BF16)
    r1 = v - hi.astype(F32)
    mid = r1.astype(BF16)
    lo = (r1 - mid.astype(F32)).astype(BF16)
    return hi, mid, lo


def _params(*sem):
    return pltpu.CompilerParams(dimension_semantics=sem, vmem_limit_bytes=VMEM_LIMIT)


def _ada_kernel(*refs, cuts):
    c_ref, w_ref, b_ref = refs[:3]
    srcs = refs[3:3 + len(cuts)]
    o_ref = refs[3 + len(cuts)]
    dsts = iter(refs[4 + len(cuts):])
    c = c_ref[...]
    s = c * _sigmoid(c)
    o_ref[...] = jnp.dot(s, w_ref[...], preferred_element_type=F32,
                         precision=lax.Precision.HIGHEST) + b_ref[...]
    for src, ranges in zip(srcs, cuts):
        for lo, hi in ranges:
            next(dsts)[...] = src[:, lo:hi].astype(BF16)


def _ada(c, w_ada, b_ada, layer, riders):
    B, D = c.shape
    N = w_ada.shape[1]
    n_steps = N // ADA_TN
    slab = lambda j: (j, 0)
    outs = pl.pallas_call(
        functools.partial(_ada_kernel, cuts=tuple(tuple(r) for _, r in riders)),
        out_shape=[jax.ShapeDtypeStruct((B, N), F32)]
        + [jax.ShapeDtypeStruct((w.shape[1], hi - lo), BF16) for w, r in riders for lo, hi in r],
        grid=(n_steps,),
        in_specs=[pl.BlockSpec((B, D), lambda j: (0, 0)),
                  pl.BlockSpec((D, ADA_TN), lambda j: (0, j)),
                  pl.BlockSpec((1, ADA_TN), lambda j: (0, j))]
        + [pl.BlockSpec((None, w.shape[1] // n_steps, w.shape[2]), lambda j: (layer, j, 0))
           for w, _ in riders],
        out_specs=[pl.BlockSpec((B, ADA_TN), lambda j: (0, j))]
        + [pl.BlockSpec((w.shape[1] // n_steps, hi - lo), slab) for w, r in riders for lo, hi in r],
        compiler_params=_params("arbitrary"),
        name="ada",
    )(c, w_ada, b_ada.reshape(1, N), *[w for w, _ in riders])
    return outs[0], outs[1:]


def _inproj_kernel(*refs, n_riders):
    x_ref, mod_ref, g_ref, wm_ref, wgt_ref, wif_ref, wift_ref = refs[:7]
    rider_in = refs[7:7 + n_riders]
    u_ref, qk_ref, v_ref, o_ref, sga_ref, sgb_ref, ifc_ref, ifr_ref = refs[7 + n_riders:15 + n_riders]
    rider_out = refs[15 + n_riders:]

    for src, dst in zip(rider_in, rider_out):
        dst[...] = src[...].astype(BF16)

    shift = mod_ref[0, 0:1, :]
    scale = mod_ref[0, 1:2, :]
    dc = u_ref.shape[1]
    d = sga_ref.shape[1]
    ts = INPROJ_SUB
    subs = [pl.ds(r0, ts) for r0 in range(0, x_ref.shape[0], ts)]

    hbs = []
    for sl in subs:
        x = x_ref[sl, :]
        ms = jnp.mean(x * x, axis=-1, keepdims=True)
        h = x * lax.rsqrt(ms + RMS_EPS) * g_ref[...]
        h = h * (1.0 + scale) + shift
        hbs.append(h.astype(BF16))

    def times_t(a, wt):
        return lax.dot_general(a, wt, (((1,), (1,)), ((), ())), preferred_element_type=F32)

    for sl, hb in zip(subs, hbs):
        def seg(lo, hi):
            return times_t(hb, wm_ref[lo:hi, :])

        u_ref[sl, :] = seg(0, dc) * _sigmoid(seg(dc, 2 * dc))
        qk_ref[sl, :] = seg(2 * dc, 4 * dc)
        v_ref[sl, :] = seg(4 * dc, 5 * dc).astype(BF16)
        o_ref[sl, :] = seg(5 * dc, 6 * dc)
        sga_ref[sl, :] = _sigmoid(times_t(hb, wgt_ref[0:d, :])).astype(BF16)
        sgb_ref[sl, :] = _sigmoid(times_t(hb, wgt_ref[d:2 * d, :])).astype(BF16)
        ifc_ref[sl, :] = jnp.dot(hb, wif_ref[...], preferred_element_type=F32)
        ifr_ref[0, :, sl] = lax.dot_general(wift_ref[...], hb, (((1,), (1,)), ((), ())),
                                            preferred_element_type=F32)


def _inproj(x2, mod3, g1, w_main, w_gates, w_if, w_ift, seq, riders):
    T, D = x2.shape
    tm = INPROJ_TM
    dc = D // 2
    per_b = seq // tm
    row = lambda i: (i, 0)
    const = lambda i: (0, 0)
    slab = lambda i: (i, 0, 0)
    slabs = [r.reshape(T // tm, -1, r.shape[-1]) for r in riders]
    once = pl.Buffered(1)
    outs = pl.pallas_call(
        functools.partial(_inproj_kernel, n_riders=len(riders)),
        out_shape=[jax.ShapeDtypeStruct((T, dc), F32),
                   jax.ShapeDtypeStruct((T, 2 * dc), F32),
                   jax.ShapeDtypeStruct((T, dc), BF16),
                   jax.ShapeDtypeStruct((T, dc), F32),
                   jax.ShapeDtypeStruct((T, D), BF16),
                   jax.ShapeDtypeStruct((T, D), BF16),
                   jax.ShapeDtypeStruct((T, LANES), F32),
                   jax.ShapeDtypeStruct((T // seq, SUBLANES, seq), F32)]
        + [jax.ShapeDtypeStruct(s.shape, BF16) for s in slabs],
        grid=(T // tm,),
        in_specs=[pl.BlockSpec((tm, D), row),
                  pl.BlockSpec((1, 6, D), lambda i: (i // per_b, 0, 0)),
                  pl.BlockSpec((1, D), const),
                  pl.BlockSpec(w_main.shape, const, pipeline_mode=once),
                  pl.BlockSpec(w_gates.shape, const, pipeline_mode=once),
                  pl.BlockSpec(w_if.shape, const),
                  pl.BlockSpec(w_ift.shape, const)]
        + [pl.BlockSpec((1,) + s.shape[1:], slab) for s in slabs],
        out_specs=[pl.BlockSpec((tm, dc), row),
                   pl.BlockSpec((tm, 2 * dc), row),
                   pl.BlockSpec((tm, dc), row),
                   pl.BlockSpec((tm, dc), row),
                   pl.BlockSpec((tm, D), row),
                   pl.BlockSpec((tm, D), row),
                   pl.BlockSpec((tm, LANES), row),
                   pl.BlockSpec((1, SUBLANES, tm), lambda i: (i // per_b, 0, i % per_b))]
        + [pl.BlockSpec((1,) + s.shape[1:], slab) for s in slabs],
        compiler_params=_params("arbitrary"),
        name="inproj",
    )(x2, mod3, g1, w_main, w_gates, w_if, w_ift, *slabs)
    return outs[:8], [o.reshape(r.shape) for o, r in zip(outs[8:], riders)]


def _conv_kernel(u_ref, w_ref, b_ref, lg_ref, lb_ref, wo_ref, y_ref, ubuf, sbuf, cbuf):
    ts = u_ref.shape[1]
    halo = CONV_HALO

    @pl.when(pl.program_id(1) == 0)
    def _():
        ubuf[0:halo, :] = jnp.zeros((halo, ubuf.shape[1]), F32)

    ubuf[halo:halo + ts, :] = u_ref[0]
    ns = sbuf.shape[1]
    for r in range(1, SUBLANES):
        sbuf[r - 1] = ubuf[r:r + ns, :]
    off = halo - (CONV_WIDTH - 1)
    for r0 in range(0, ts, CONV_RC):
        acc = jnp.broadcast_to(b_ref[...], (CONV_RC, ubuf.shape[1]))
        for k in range(CONV_WIDTH):
            r = (off + k) % SUBLANES
            lo = off + k - r + r0
            win = ubuf[lo:lo + CONV_RC, :] if r == 0 else sbuf[r - 1, lo:lo + CONV_RC, :]
            acc = acc + w_ref[k:k + 1, :] * win
        cbuf[r0:r0 + CONV_RC, :] = acc
    ubuf[0:halo, :] = ubuf[ts:ts + halo, :]

    a = cbuf[...]
    mu = jnp.mean(a, axis=-1, keepdims=True)
    ac = a - mu
    var = jnp.mean(ac * ac, axis=-1, keepdims=True)
    z = ac * lax.rsqrt(var + LN_EPS) * lg_ref[...] + lb_ref[...]
    z = z * _sigmoid(z)
    y_ref[0] = jnp.dot(z.astype(BF16), wo_ref[...], preferred_element_type=F32).astype(BF16)


def _conv_branch(u3, w, b, lg, lb, wo):
    B, S, C = u3.shape
    D = wo.shape[1]
    ts = CONV_TS
    const = lambda bi, si: (0, 0)
    return pl.pallas_call(
        _conv_kernel,
        out_shape=jax.ShapeDtypeStruct((B, S, D), BF16),
        grid=(B, S // ts),
        in_specs=[pl.BlockSpec((1, ts, C), lambda bi, si: (bi, si, 0)),
                  pl.BlockSpec(w.shape, const),
                  pl.BlockSpec((1, C), const),
                  pl.BlockSpec((1, C), const),
                  pl.BlockSpec((1, C), const),
                  pl.BlockSpec(wo.shape, const)],
        out_specs=pl.BlockSpec((1, ts, D), lambda bi, si: (bi, si, 0)),
        scratch_shapes=[pltpu.VMEM((ts + CONV_HALO, C), F32),
                        pltpu.VMEM((SUBLANES - 1, ts + CONV_HALO - SUBLANES, C), F32),
                        pltpu.VMEM((ts, C), F32)],
        compiler_params=_params("arbitrary", "arbitrary"),
        name="conv",
    )(u3, w, b, lg, lb, wo)


def _mlstm_kernel(qk_ref, v_ref, o_ref, ifc_ref, ifr_ref, cw_ref, cb_ref, bifc_ref, bifr_ref, ng_ref,
                  wo_ref, y_ref, qkbuf, cn_ref, m_ref, hbuf):
    @pl.when(pl.program_id(1) == 0)
    def _():
        qkbuf[:, 0:SUBLANES, :] = jnp.zeros((qkbuf.shape[0], SUBLANES, qkbuf.shape[2]), F32)
        cn_ref[...] = jnp.zeros(cn_ref.shape, F32)
        m_ref[...] = jnp.zeros(m_ref.shape, F32)

    nb, L, mi = hbuf.shape
    dh = mi // M_HEADS
    halo = SUBLANES
    off = halo - (QK_CONV_WIDTH - 1)
    rows = lax.broadcasted_iota(I32, (L, L), 0)
    cols = lax.broadcasted_iota(I32, (L, L), 1)
    causal = cols <= rows
    lower = jnp.where(causal, 1.0, 0.0).astype(BF16)
    upper = jnp.where(rows <= cols, 1.0, 0.0).astype(BF16)
    lane = lax.broadcasted_iota(I32, (L, dh), 1)
    ones_col = jnp.where(lane == 0, 1.0, 0.0).astype(BF16)
    scale = dh ** -0.5

    seqs = []
    for b in range(nb):
        qkbuf[b, halo:halo + L, :] = qk_ref[b]
        y = jnp.broadcast_to(cb_ref[...], (L, qkbuf.shape[2]))
        for k in range(QK_CONV_WIDTH):
            y = y + cw_ref[k:k + 1, :] * qkbuf[b, off + k:off + k + L, :]
        y = y * _sigmoid(y)
        qkbuf[b, 0:halo, :] = qkbuf[b, L:L + halo, :]
        ifr = ifr_ref[b] + bifr_ref[...]
        ifc = ifc_ref[b] + bifc_ref[...]
        bcum_c = sum(jnp.dot(lower, p, preferred_element_type=F32) for p in _split_bf16(_log_sigmoid(ifc)))
        bcum_r = sum(jnp.dot(p, upper, preferred_element_type=F32) for p in _split_bf16(_log_sigmoid(ifr)))
        seqs.append((y, ifr, bcum_c, bcum_r))

    probs = [(b, hd) for b in range(nb) for hd in range(M_HEADS)]
    st = {}
    for p in probs:
        b, hd = p
        y, ifr, bcum_c, bcum_r = seqs[b]
        c0 = hd * dh
        qb = (y[:, c0:c0 + dh] * scale).astype(BF16)
        kt = y[:, mi + c0:mi + c0 + dh].T
        v = v_ref[b, :, c0:c0 + dh]
        bc = bcum_c[:, M_HEADS + hd:M_HEADS + hd + 1]
        br = bcum_r[M_HEADS + hd:M_HEADS + hd + 1, :]
        li = ifr[hd:hd + 1, :]
        m_prev = m_ref[b, hd, 0:1, 0:1]
        dmat = jnp.where(causal, bc - br + li, -jnp.inf)
        st[p] = dict(qb=qb, kt=kt, v=v, bc=bc, br=br, li=li, m_prev=m_prev, dmat=dmat)
    for p in probs:
        s = st[p]
        s["inter"] = s["bc"] + s["m_prev"]
        s["m_t"] = jnp.maximum(jnp.max(s["dmat"], axis=-1, keepdims=True), s["inter"])
    for p in probs:
        s = st[p]
        s["qk"] = jnp.dot(s["qb"], s["kt"].astype(BF16), preferred_element_type=F32)
    for p in probs:
        b, hd = p
        s = st[p]
        s["cn"] = cn_ref[b, hd]
        s["qcn"] = jnp.dot(s["qb"], s["cn"].astype(BF16), preferred_element_type=F32)
    for p in probs:
        s = st[p]
        s["wts"] = jnp.exp(s["dmat"] - s["m_t"])
        s["s_inter"] = jnp.exp(s["inter"] - s["m_t"])
    for p in probs:
        s = st[p]
        s["s_mat"] = s["qk"] * s["wts"]
    for p in probs:
        s = st[p]
        s["sv"] = jnp.dot(s["s_mat"].astype(BF16), s["v"], preferred_element_type=F32)
    for p in probs:
        s = st[p]
        s["rowsum"] = jnp.sum(s["s_mat"], axis=-1, keepdims=True)
    for p in probs:
        s = st[p]
        s["num"] = s["sv"] + s["s_inter"] * s["qcn"][:, 0:dh]
        s["den"] = s["rowsum"] + s["s_inter"] * s["qcn"][:, dh:dh + 1]
    for p in probs:
        b, hd = p
        s = st[p]
        b_last = s["br"][:, L - 1:L]
        a = b_last - s["br"] + s["li"]
        m_new = jnp.maximum(b_last + s["m_prev"], jnp.max(a, axis=-1, keepdims=True))
        wk = jnp.exp(a - m_new)
        sc = jnp.exp(b_last + s["m_prev"] - m_new)
        v_ext = jnp.concatenate([s["v"], ones_col], axis=1)
        cn_ref[b, hd] = sc * s["cn"] + jnp.dot((s["kt"] * wk).astype(BF16), v_ext, preferred_element_type=F32)
        m_ref[b, hd] = jnp.broadcast_to(m_new, m_ref.shape[2:])
    for p in probs:
        s = st[p]
        s["hh"] = s["num"] / jnp.maximum(jnp.abs(s["den"]), jnp.exp(-s["m_t"]))
        s["mu"] = jnp.mean(s["hh"], axis=-1, keepdims=True)
    for p in probs:
        s = st[p]
        s["hc"] = s["hh"] - s["mu"]
        s["var"] = jnp.mean(s["hc"] * s["hc"], axis=-1, keepdims=True)
    for p in probs:
        b, hd = p
        s = st[p]
        c0 = hd * dh
        hn = s["hc"] * lax.rsqrt(s["var"] + LN_EPS) * ng_ref[:, c0:c0 + dh]
        hbuf[b, :, c0:c0 + dh] = hn * _sigmoid(o_ref[b, :, c0:c0 + dh])
    for b in range(nb):
        y = jnp.dot(hbuf[b].astype(BF16), wo_ref[...], preferred_element_type=F32)
        y_ref[b] = y.astype(BF16)


def _mlstm_branch(qk3, v3, o3, ifc3, ifr3, cw, cb, bifc, bifr, ng, wo):
    B, S, C2 = qk3.shape
    mi = v3.shape[2]
    dh = mi // M_HEADS
    D = wo.shape[1]
    L = MLSTM_L
    nb = MLSTM_SEQS
    const = lambda bi, ci: (0, 0)
    tile = lambda bi, ci: (bi, ci, 0)
    return pl.pallas_call(
        _mlstm_kernel,
        out_shape=jax.ShapeDtypeStruct((B, S, D), BF16),
        grid=(B // nb, S // L),
        in_specs=[pl.BlockSpec((nb, L, C2), tile),
                  pl.BlockSpec((nb, L, mi), tile),
                  pl.BlockSpec((nb, L, mi), tile),
                  pl.BlockSpec((nb, L, LANES), tile),
                  pl.BlockSpec((nb, SUBLANES, L), lambda bi, ci: (bi, 0, ci)),
                  pl.BlockSpec(cw.shape, const),
                  pl.BlockSpec((1, C2), const),
                  pl.BlockSpec((1, LANES), const),
                  pl.BlockSpec((SUBLANES, 1), const),
                  pl.BlockSpec((1, mi), const),
                  pl.BlockSpec(wo.shape, const)],
        out_specs=pl.BlockSpec((nb, L, D), tile),
        scratch_shapes=[pltpu.VMEM((nb, L + SUBLANES, C2), F32),
                        pltpu.VMEM((nb, M_HEADS, dh, 2 * dh), F32),
                        pltpu.VMEM((nb, M_HEADS, SUBLANES, LANES), F32),
                        pltpu.VMEM((nb, L, mi), F32)],
        compiler_params=_params("arbitrary", "arbitrary"),
        name="mlstm",
    )(qk3, v3, o3, ifc3, ifr3, cw, cb, bifc, bifr, ng, wo)


def _merge_kernel(x_ref, ya_ref, yb_ref, sga_ref, sgb_ref, mod_ref, g2_ref, wo_ref, wr_ref, br_ref,
                  x1_ref, h2_ref, ri_ref, rf_ref, cnt_ref, run_ref):
    ts = MERGE_SUB
    subs = [pl.ds(r0, ts) for r0 in range(0, x_ref.shape[0], ts)]

    @pl.when(pl.program_id(0) == 0)
    def _():
        run_ref[...] = jnp.zeros(run_ref.shape, F32)

    gate1 = mod_ref[0, 2:3, :]
    shift2 = mod_ref[0, 3:4, :]
    scale2 = mod_ref[0, 4:5, :]
    lane = lax.broadcasted_iota(I32, (ts, LANES), 1).astype(F32)
    neg = -jnp.inf
    rows = lax.broadcasted_iota(I32, (ts, ts), 0)
    cols = lax.broadcasted_iota(I32, (ts, ts), 1)
    strict = jnp.where(cols < rows, 1.0, 0.0).astype(BF16)

    def first_argmax(vals):
        mx = jnp.max(vals, axis=-1, keepdims=True)
        idx = jnp.min(jnp.where(vals == mx, lane, float(LANES)), axis=-1, keepdims=True)
        return mx, idx

    h2s = []
    for sl in subs:
        merged = (sga_ref[sl, :].astype(F32) * ya_ref[sl, :].astype(F32)
                  + sgb_ref[sl, :].astype(F32) * yb_ref[sl, :].astype(F32))
        mix = jnp.dot(merged.astype(BF16), wo_ref[...], preferred_element_type=F32)
        x1 = x_ref[sl, :] + gate1 * mix
        x1_ref[sl, :] = x1
        ms = jnp.mean(x1 * x1, axis=-1, keepdims=True)
        h2 = x1 * lax.rsqrt(ms + RMS_EPS) * g2_ref[...]
        h2 = h2 * (1.0 + scale2) + shift2
        h2_ref[sl, :] = _pack_bf16_pairs(h2)
        h2s.append(h2.astype(BF16))

    run = run_ref[0:1, :]
    for sl, h2b in zip(subs, h2s):
        logits = jnp.dot(h2b, wr_ref[...], preferred_element_type=F32) + br_ref[...]
        lg = jnp.where(lane < N_GROUPS, logits, neg)
        gmax, gsel = first_argmax(lg)
        p_g = 1.0 / jnp.sum(jnp.exp(lg - gmax), axis=-1, keepdims=True)
        lo = N_GROUPS + gsel * E_PER_GROUP
        le = jnp.where((lane >= lo) & (lane < lo + E_PER_GROUP), logits, neg)
        l1, i1 = first_argmax(le)
        l2, i2 = first_argmax(jnp.where(lane == i1, neg, le))
        r = jnp.exp(l2 - l1)
        w1 = p_g / (1.0 + r)
        w2 = p_g * r / (1.0 + r)
        e1 = i1 - N_GROUPS
        e2 = i2 - N_GROUPS

        onehot = jnp.where((lane == e1) | (lane == e2), 1.0, 0.0)
        before = jnp.dot(strict, onehot.astype(BF16), preferred_element_type=F32) + run
        rank1 = jnp.sum(jnp.where(lane == e1, before, 0.0), axis=-1, keepdims=True)
        rank2 = jnp.sum(jnp.where(lane == e2, before, 0.0), axis=-1, keepdims=True)
        run = run + jnp.sum(onehot, axis=0, keepdims=True)

        codes = jnp.where(lane == 0, e1 * float(RANK_RADIX) + rank1,
                          jnp.where(lane == 1, e2 * float(RANK_RADIX) + rank2, 0.0))
        ri_ref[:, sl] = codes.T[0:SUBLANES, :].astype(I32)
        rf_ref[sl, :] = jnp.where(lane == 0, w1, jnp.where(lane == 1, w2, 0.0))
    run_ref[...] = jnp.broadcast_to(run, run_ref.shape)
    cnt_ref[...] = jnp.broadcast_to(run, cnt_ref.shape).astype(I32)


def _merge(x2, ya, yb, sga, sgb, mod3, g2, wo, wr, br, seq):
    T, D = x2.shape
    tm = MERGE_TM
    per_b = seq // tm
    row = lambda i: (i, 0)
    const = lambda i: (0, 0)
    return pl.pallas_call(
        _merge_kernel,
        out_shape=[jax.ShapeDtypeStruct((T, D), F32),
                   jax.ShapeDtypeStruct((T, D // 2), I32),
                   jax.ShapeDtypeStruct((SUBLANES, T), I32),
                   jax.ShapeDtypeStruct((T, LANES), F32),
                   jax.ShapeDtypeStruct((SUBLANES, LANES), I32)],
        grid=(T // tm,),
        in_specs=[pl.BlockSpec((tm, D), row),
                  pl.BlockSpec((tm, D), row),
                  pl.BlockSpec((tm, D), row),
                  pl.BlockSpec((tm, D), row),
                  pl.BlockSpec((tm, D), row),
                  pl.BlockSpec((1, 6, D), lambda i: (i // per_b, 0, 0)),
                  pl.BlockSpec((1, D), const),
                  pl.BlockSpec(wo.shape, const),
                  pl.BlockSpec(wr.shape, const),
                  pl.BlockSpec((1, LANES), const)],
        out_specs=[pl.BlockSpec((tm, D), row),
                   pl.BlockSpec((tm, D // 2), row),
                   pl.BlockSpec((SUBLANES, tm), lambda i: (0, i)),
                   pl.BlockSpec((tm, LANES), row),
                   pl.BlockSpec((SUBLANES, LANES), const)],
        scratch_shapes=[pltpu.VMEM((SUBLANES, LANES), F32)],
        compiler_params=_params("arbitrary"),
        name="merge",
    )(x2, ya, yb, sga, sgb, mod3, g2, wo, wr, br)


def _sc_workers():
    info = plsc.get_sparse_core_info()
    mesh = plsc.VectorSubcoreMesh(core_axis_name="core", subcore_axis_name="subcore")
    params = pltpu.CompilerParams()
    if "needs_layout_passes" in pltpu.CompilerParams.__dataclass_fields__:
        params = dataclasses.replace(params, needs_layout_passes=False)
    return info, mesh, params


def _rows_from_codes(code_v, base_v, idx_v, lanes):
    for j in range(code_v.shape[0] // lanes):
        c = code_v[pl.ds(j * lanes, lanes)]
        expert = lax.shift_right_logical(c, RANK_BITS)
        idx_v[pl.ds(j * lanes, lanes)] = plsc.load_gather(base_v, [expert]) + (c & (RANK_RADIX - 1))


def _two_slot_loop(n_chunks, start, finish):
    start(0, 0)

    @pl.loop(0, n_chunks, step=2)
    def _(c):
        start(c + 1, 1)
        finish(c, 0)

        @pl.when(c + 2 < n_chunks)
        def _():
            start(c + 2, 0)

        finish(c + 1, 1)


def _sc_dispatch(h2, code0, code1, base, n_rows):
    T, D = h2.shape
    info, mesh, params = _sc_workers()
    n_workers = info.num_cores * info.num_subcores
    w = SC_WINDOW_BYTES // (D * h2.dtype.itemsize)
    per_w = T // n_workers
    n_chunks = per_w // w
    assert per_w * n_workers == T and n_chunks * w == per_w and n_chunks % 2 == 0

    @functools.partial(
        pl.kernel, out_type=jax.ShapeDtypeStruct((n_rows, D), h2.dtype), mesh=mesh, compiler_params=params,
        scratch_types=[pltpu.VMEM((N_EXPERTS,), I32), pltpu.VMEM((w,), I32), pltpu.VMEM((w,), I32),
                       pltpu.VMEM((w,), I32), pltpu.VMEM((w, D), h2.dtype), pltpu.VMEM((w, D), h2.dtype),
                       pltpu.SemaphoreType.DMA, pltpu.SemaphoreType.DMA])
    def scatter(h_hbm, c0_hbm, c1_hbm, b_hbm, xs_hbm, base_v, code_v, i0_v, i1_v, rows0, rows1, sem0, sem1):
        wid = lax.axis_index("subcore") * info.num_cores + lax.axis_index("core")
        w0 = wid * per_w
        pltpu.sync_copy(b_hbm, base_v)
        rows = (rows0, rows1)
        sems = (sem0, sem1)

        def start(c, slot):
            pltpu.async_copy(h_hbm.at[pl.ds(w0 + c * w, w)], rows[slot], sems[slot])

        def finish(c, slot):
            pltpu.sync_copy(c0_hbm.at[pl.ds(w0 + c * w, w)], code_v)
            _rows_from_codes(code_v, base_v, i0_v, info.num_lanes)
            pltpu.sync_copy(c1_hbm.at[pl.ds(w0 + c * w, w)], code_v)
            _rows_from_codes(code_v, base_v, i1_v, info.num_lanes)
            pltpu.make_async_copy(h_hbm.at[pl.ds(w0 + c * w, w)], rows[slot], sems[slot]).wait()
            pltpu.sync_copy(rows[slot], xs_hbm.at[i0_v])
            pltpu.sync_copy(rows[slot], xs_hbm.at[i1_v])

        _two_slot_loop(n_chunks, start, finish)

    return scatter(h2, code0, code1, base)


def _sc_collect(ys, codes, base):
    n = codes.shape[0]
    D = ys.shape[1]
    info, mesh, params = _sc_workers()
    n_workers = info.num_cores * info.num_subcores
    w = SC_WINDOW_BYTES // (D * ys.dtype.itemsize)
    per_w = n // n_workers
    n_chunks = per_w // w
    assert per_w * n_workers == n and n_chunks * w == per_w and n_chunks % 2 == 0

    @functools.partial(
        pl.kernel, out_type=jax.ShapeDtypeStruct((n, D), ys.dtype), mesh=mesh, compiler_params=params,
        scratch_types=[pltpu.VMEM((N_EXPERTS,), I32), pltpu.VMEM((w,), I32), pltpu.VMEM((w,), I32),
                       pltpu.VMEM((w,), I32), pltpu.VMEM((w, D), ys.dtype), pltpu.VMEM((w, D), ys.dtype),
                       pltpu.SemaphoreType.DMA, pltpu.SemaphoreType.DMA])
    def gather(ys_hbm, c_hbm, b_hbm, yk_hbm, base_v, code_v, i0_v, i1_v, rows0, rows1, sem0, sem1):
        wid = lax.axis_index("subcore") * info.num_cores + lax.axis_index("core")
        w0 = wid * per_w
        pltpu.sync_copy(b_hbm, base_v)
        idx = (i0_v, i1_v)
        rows = (rows0, rows1)
        sems = (sem0, sem1)

        def start(c, slot):
            pltpu.sync_copy(c_hbm.at[pl.ds(w0 + c * w, w)], code_v)
            _rows_from_codes(code_v, base_v, idx[slot], info.num_lanes)
            pltpu.async_copy(ys_hbm.at[idx[slot]], rows[slot], sems[slot])

        def finish(c, slot):
            pltpu.make_async_copy(ys_hbm.at[idx[slot]], rows[slot], sems[slot]).wait()
            pltpu.sync_copy(rows[slot], yk_hbm.at[pl.ds(w0 + c * w, w)])

        _two_slot_loop(n_chunks, start, finish)

    return gather(ys, codes, base)


def _schedule_kernel(cnt_ref, te_ref, tb_ref, base_ref, nt_ref):
    tm = EXPERT_TM

    def expert(e, t0):
        n = (cnt_ref[e] + tm - 1) // tm
        base_ref[e] = t0 * tm

        def tile(t, c):
            te_ref[t] = e
            tb_ref[t] = t
            return c

        lax.fori_loop(t0, t0 + n, tile, 0)
        return t0 + n

    nt = lax.fori_loop(0, N_EXPERTS, expert, 0)
    nt_ref[0] = nt
    last = te_ref[nt - 1]

    def idle(t, c):
        te_ref[t] = last
        tb_ref[t] = nt - 1
        return c

    lax.fori_loop(nt, te_ref.shape[0], idle, 0)


def _schedule(counts, max_tiles):
    smem = pl.BlockSpec(memory_space=pltpu.SMEM)
    return pl.pallas_call(
        _schedule_kernel,
        out_shape=[jax.ShapeDtypeStruct((max_tiles,), I32),
                   jax.ShapeDtypeStruct((max_tiles,), I32),
                   jax.ShapeDtypeStruct((N_EXPERTS,), I32),
                   jax.ShapeDtypeStruct((1,), I32)],
        in_specs=[smem],
        out_specs=[smem, smem, smem, smem],
        name="schedule",
    )(counts)


def _expert_kernel(te_ref, tb_ref, nt_ref, xs_ref, wg_ref, wu_ref, wd_ref, ys_ref):
    @pl.when(pl.program_id(0) < nt_ref[0])
    def _():
        xb = _unpack_bf16_pairs(xs_ref[...])
        g = jnp.dot(xb, wg_ref[0], preferred_element_type=F32)
        u = jnp.dot(xb, wu_ref[0], preferred_element_type=F32)
        act = (g * _sigmoid(g)) * u
        ys_ref[...] = _pack_bf16_pairs(jnp.dot(act.astype(BF16), wd_ref[0], preferred_element_type=F32))


def _experts(tile_e, tile_b, n_tiles, xs, wg, wu, wd, max_tiles):
    P = xs.shape[0]
    D, de = wg.shape[1:]
    tm = EXPERT_TM
    wmap = lambda j, te, tb, nt: (te[j], 0, 0)
    rmap = lambda j, te, tb, nt: (tb[j], 0)
    return pl.pallas_call(
        _expert_kernel,
        out_shape=jax.ShapeDtypeStruct(xs.shape, I32),
        grid_spec=pltpu.PrefetchScalarGridSpec(
            num_scalar_prefetch=3,
            grid=(max_tiles,),
            in_specs=[pl.BlockSpec((tm, xs.shape[1]), rmap),
                      pl.BlockSpec((1, D, de), wmap),
                      pl.BlockSpec((1, D, de), wmap),
                      pl.BlockSpec((1, de, D), wmap)],
            out_specs=pl.BlockSpec((tm, xs.shape[1]), rmap)),
        compiler_params=_params("arbitrary"),
        name="experts",
    )(tile_e, tile_b, n_tiles, xs, wg, wu, wd)


def _combine_kernel(x1_ref, rf_ref, mod_ref, gf_ref, y0_ref, y1_ref, out_ref, *, final_norm):
    gate2 = mod_ref[0, 5:6, :]
    w = rf_ref[...]
    moe = w[:, 0:1] * _unpack_pairs_f32(y0_ref[...]) + w[:, 1:2] * _unpack_pairs_f32(y1_ref[...])
    x2 = x1_ref[...] + gate2 * moe
    if final_norm:
        ms = jnp.mean(x2 * x2, axis=-1, keepdims=True)
        x2 = x2 * lax.rsqrt(ms + RMS_EPS) * gf_ref[...]
    out_ref[...] = x2


def _combine(x1, rf, mod3, gf, yk, seq, final_norm):
    T, D = x1.shape
    tc = COMBINE_TM
    per_b = seq // tc
    n_blk = T // tc
    return pl.pallas_call(
        functools.partial(_combine_kernel, final_norm=final_norm),
        out_shape=jax.ShapeDtypeStruct((T, D), F32),
        grid=(n_blk,),
        in_specs=[pl.BlockSpec((tc, D), lambda i: (i, 0)),
                  pl.BlockSpec((tc, LANES), lambda i: (i, 0)),
                  pl.BlockSpec((1, 6, D), lambda i: (i // per_b, 0, 0)),
                  pl.BlockSpec((1, D), lambda i: (0, 0)),
                  pl.BlockSpec((tc, yk.shape[1]), lambda i: (i, 0)),
                  pl.BlockSpec((tc, yk.shape[1]), lambda i: (n_blk + i, 0))],
        out_specs=pl.BlockSpec((tc, D), lambda i: (i, 0)),
        compiler_params=_params("arbitrary"),
        name="combine",
    )(x1, rf, mod3, gf, yk, yk)


def _layer(x2, c, seq, layer, w_ada, b_ada, g_norm1, w_in, b_if, conv_dw_w, conv_dw_b, conv_ln_g, conv_ln_b,
           w_conv_out, qk_conv_w, qk_conv_b, m_norm_g, w_m_out, w_out, g_norm2, w_rg, b_rg,
           w_re, b_re, w_e_gate, w_e_up, w_e_down):
    T, D = x2.shape
    B = T // seq
    dc = D // 2
    nif = 2 * M_HEADS

    if_lo = 6 * dc
    w_in_t = jnp.swapaxes(w_in, 1, 2)
    w_main = w_in_t[layer, :if_lo, :].astype(BF16)
    w_gates = w_in_t[layer, if_lo + nif:, :].astype(BF16)
    full = lambda w: (w, [(0, w.shape[2])])
    mod, (w_out_b, w_conv_out_b, w_m_out_b) = _ada(
        c, w_ada, b_ada, layer, [full(w_out), full(w_conv_out), full(w_m_out)])
    mod3 = mod.reshape(B, 6, D)
    w_if = w_in[layer, :, if_lo:if_lo + nif]
    w_if_pad = jnp.pad(w_if, ((0, 0), (0, LANES - nif))).astype(BF16)
    w_ift = w_if.T.astype(BF16)
    (u, qk, v, o, sga, sgb, ifc, ifr), (wg_b, wu_b, wd_b) = _inproj(
        x2, mod3, g_norm1.reshape(1, D), w_main, w_gates, w_if_pad, w_ift, seq,
        riders=(w_e_gate, w_e_up, w_e_down))

    ya = _conv_branch(
        u.reshape(B, seq, dc), conv_dw_w, conv_dw_b.reshape(1, dc), conv_ln_g.reshape(1, dc),
        conv_ln_b.reshape(1, dc), w_conv_out_b)
    bifc = jnp.pad(b_if, (0, LANES - nif)).reshape(1, LANES)
    bifr = b_if.reshape(nif, 1)
    yb = _mlstm_branch(
        qk.reshape(B, seq, 2 * dc), v.reshape(B, seq, dc), o.reshape(B, seq, dc),
        ifc.reshape(B, seq, LANES), ifr, qk_conv_w, qk_conv_b.reshape(1, 2 * dc), bifc, bifr,
        m_norm_g.reshape(1, dc), w_m_out_b)

    n_r = N_GROUPS + N_EXPERTS
    w_r = jnp.pad(jnp.concatenate([w_rg, w_re], axis=1), ((0, 0), (0, LANES - n_r))).astype(BF16)
    b_r = jnp.pad(jnp.concatenate([b_rg, b_re]), (0, LANES - n_r)).reshape(1, LANES)
    x1, h2, ri, rf, cnt = _merge(x2, ya.reshape(T, D), yb.reshape(T, D), sga, sgb, mod3,
                                 g_norm2.reshape(1, D), w_out_b, w_r, b_r, seq)

    tm = EXPERT_TM
    max_tiles = (T * TOP_K) // tm + N_EXPERTS
    tile_e, tile_b, base, n_tiles = _schedule(cnt[0, :N_EXPERTS], max_tiles)

    code0 = ri[0]
    code1 = ri[1]
    xs = _sc_dispatch(h2, code0, code1, base, max_tiles * tm)
    ys = _experts(tile_e, tile_b, n_tiles, xs, wg_b, wu_b, wd_b, max_tiles)
    return x1, rf, mod3, ys, code0, code1, base


def kernel(x, c, w_ada, b_ada, g_norm1, w_in, b_if, conv_dw_w, conv_dw_b, conv_ln_g, conv_ln_b,
           w_conv_out, qk_conv_w, qk_conv_b, m_norm_g, w_m_out, w_out, g_norm2, w_rg, b_rg,
           w_re, b_re, w_e_gate, w_e_up, w_e_down, g_final):
    B, S, D = x.shape
    depth = w_ada.shape[0]
    x2 = x.reshape(B * S, D)
    for l in range(depth):
        x1, rf, mod3, ys, code0, code1, base = _layer(
            x2, c, S, l, w_ada[l], b_ada[l], g_norm1[l], w_in, b_if[l], conv_dw_w[l], conv_dw_b[l],
            conv_ln_g[l], conv_ln_b[l], w_conv_out, qk_conv_w[l], qk_conv_b[l], m_norm_g[l],
            w_m_out, w_out, g_norm2[l], w_rg[l], b_rg[l], w_re[l], b_re[l],
            w_e_gate[l], w_e_up[l], w_e_down[l])
        yk = _sc_collect(ys, jnp.concatenate([code0, code1]), base)
        x2 = _combine(x1, rf, mod3, g_final.reshape(1, D), yk, S, final_norm=l == depth - 1)
    return x2.reshape(B, S, D)
```

```python
import dataclasses
import functools

import jax
import jax.numpy as jnp
from jax import lax
from jax.experimental import pallas as pl
from jax.experimental.pallas import tpu as pltpu
from jax.experimental.pallas import tpu_sc as plsc

F32 = jnp.float32
BF16 = jnp.bfloat16
I32 = jnp.int32

M_HEADS = 4
CONV_WIDTH = 31
QK_CONV_WIDTH = 4
N_GROUPS = 4
E_PER_GROUP = 8
N_EXPERTS = N_GROUPS * E_PER_GROUP
TOP_K = 2
RMS_EPS = 1e-6
LN_EPS = 1e-5

LANES = 128
SUBLANES = 8
VMEM_LIMIT = 56 * 1024 * 1024

ADA_TN = 768
INPROJ_TM = 512
INPROJ_SUB = 256
CONV_TS = 1024
CONV_HALO = 32
CONV_RC = 512
MLSTM_L = 128
MLSTM_SEQS = 8
MERGE_TM = 1024
MERGE_SUB = 512
EXPERT_TM = 512
SC_WINDOW_BYTES = 128 * 1024
COMBINE_TM = 1024
RANK_BITS = 16
RANK_RADIX = 1 << RANK_BITS
assert EXPERT_TM & (EXPERT_TM - 1) == 0


def _sigmoid(v):
    return 1.0 / (1.0 + jnp.exp(-v))


def _log_sigmoid(v):
    return -(jnp.maximum(-v, 0.0) + jnp.log1p(jnp.exp(-jnp.abs(v))))


def _pack_bf16_pairs(v):
    n = v.shape[1] // 2
    bits = lax.bitcast_convert_type(v.astype(BF16).astype(F32), jnp.uint32)
    word = bits[:, n:] | (bits[:, :n] >> 16)
    return lax.bitcast_convert_type(word, I32)


def _unpack_pairs_f32(w):
    bits = lax.bitcast_convert_type(w, jnp.uint32)
    lo = lax.bitcast_convert_type(bits << 16, F32)
    hi = lax.bitcast_convert_type(bits & jnp.uint32(0xFFFF0000), F32)
    return jnp.concatenate([lo, hi], axis=1)


def _unpack_bf16_pairs(w):
    return _unpack_pairs_f32(w).astype(BF16)


def _split_bf16(v):
    hi = v.astype(BF16)
    r1 = v - hi.astype(F32)
    mid = r1.astype(BF16)
    lo = (r1 - mid.astype(F32)).astype(BF16)
    return hi, mid, lo


def _params(*sem):
    return pltpu.CompilerParams(dimension_semantics=sem, vmem_limit_bytes=VMEM_LIMIT)


def _ada_kernel(*refs, cuts):
    c_ref, w_ref, b_ref = refs[:3]
    srcs = refs[3:3 + len(cuts)]
    o_ref = refs[3 + len(cuts)]
    dsts = iter(refs[4 + len(cuts):])
    c = c_ref[...]
    s = c * _sigmoid(c)
    o_ref[...] = jnp.dot(s, w_ref[...], preferred_element_type=F32,
                         precision=lax.Precision.HIGHEST) + b_ref[...]
    for src, ranges in zip(srcs, cuts):
        for lo, hi in ranges:
            next(dsts)[...] = src[:, lo:hi].astype(BF16)


def _ada(c, w_ada, b_ada, layer, riders):
    B, D = c.shape
    N = w_ada.shape[1]
    n_steps = N // ADA_TN
    slab = lambda j: (j, 0)
    outs = pl.pallas_call(
        functools.partial(_ada_kernel, cuts=tuple(tuple(r) for _, r in riders)),
        out_shape=[jax.ShapeDtypeStruct((B, N), F32)]
        + [jax.ShapeDtypeStruct((w.shape[1], hi - lo), BF16) for w, r in riders for lo, hi in r],
        grid=(n_steps,),
        in_specs=[pl.BlockSpec((B, D), lambda j: (0, 0)),
                  pl.BlockSpec((D, ADA_TN), lambda j: (0, j)),
                  pl.BlockSpec((1, ADA_TN), lambda j: (0, j))]
        + [pl.BlockSpec((None, w.shape[1] // n_steps, w.shape[2]), lambda j: (layer, j, 0))
           for w, _ in riders],
        out_specs=[pl.BlockSpec((B, ADA_TN), lambda j: (0, j))]
        + [pl.BlockSpec((w.shape[1] // n_steps, hi - lo), slab) for w, r in riders for lo, hi in r],
        compiler_params=_params("arbitrary"),
        name="ada",
    )(c, w_ada, b_ada.reshape(1, N), *[w for w, _ in riders])
    return outs[0], outs[1:]


def _inproj_kernel(*refs, n_riders):
    x_ref, mod_ref, g_ref, wm_ref, wgt_ref, wif_ref, wift_ref = refs[:7]
    rider_in = refs[7:7 + n_riders]
    u_ref, qk_ref, v_ref, o_ref, sga_ref, sgb_ref, ifc_ref, ifr_ref = refs[7 + n_riders:15 + n_riders]
    rider_out = refs[15 + n_riders:]

    for src, dst in zip(rider_in, rider_out):
        dst[...] = src[...].astype(BF16)

    shift = mod_ref[0, 0:1, :]
    scale = mod_ref[0, 1:2, :]
    dc = u_ref.shape[1]
    d = sga_ref.shape[1]
    ts = INPROJ_SUB
    subs = [pl.ds(r0, ts) for r0 in range(0, x_ref.shape[0], ts)]

    hbs = []
    for sl in subs:
        x = x_ref[sl, :]
        ms = jnp.mean(x * x, axis=-1, keepdims=True)
        h = x * lax.rsqrt(ms + RMS_EPS) * g_ref[...]
        h = h * (1.0 + scale) + shift
        hbs.append(h.astype(BF16))

    def times_t(a, wt):
        return lax.dot_general(a, wt, (((1,), (1,)), ((), ())), preferred_element_type=F32)

    for sl, hb in zip(subs, hbs):
        def seg(lo, hi):
            return times_t(hb, wm_ref[lo:hi, :])

        u_ref[sl, :] = seg(0, dc) * _sigmoid(seg(dc, 2 * dc))
        qk_ref[sl, :] = seg(2 * dc, 4 * dc)
        v_ref[sl, :] = seg(4 * dc, 5 * dc).astype(BF16)
        o_ref[sl, :] = seg(5 * dc, 6 * dc)
        sga_ref[sl, :] = _sigmoid(times_t(hb, wgt_ref[0:d, :])).astype(BF16)
        sgb_ref[sl, :] = _sigmoid(times_t(hb, wgt_ref[d:2 * d, :])).astype(BF16)
        ifc_ref[sl, :] = jnp.dot(hb, wif_ref[...], preferred_element_type=F32)
        ifr_ref[0, :, sl] = lax.dot_general(wift_ref[...], hb, (((1,), (1,)), ((), ())),
                                            preferred_element_type=F32)


def _inproj(x2, mod3, g1, w_main, w_gates, w_if, w_ift, seq, riders):
    T, D = x2.shape
    tm = INPROJ_TM
    dc = D // 2
    per_b = seq // tm
    row = lambda i: (i, 0)
    const = lambda i: (0, 0)
    slab = lambda i: (i, 0, 0)
    slabs = [r.reshape(T // tm, -1, r.shape[-1]) for r in riders]
    once = pl.Buffered(1)
    outs = pl.pallas_call(
        functools.partial(_inproj_kernel, n_riders=len(riders)),
        out_shape=[jax.ShapeDtypeStruct((T, dc), F32),
                   jax.ShapeDtypeStruct((T, 2 * dc), F32),
                   jax.ShapeDtypeStruct((T, dc), BF16),
                   jax.ShapeDtypeStruct((T, dc), F32),
                   jax.ShapeDtypeStruct((T, D), BF16),
                   jax.ShapeDtypeStruct((T, D), BF16),
                   jax.ShapeDtypeStruct((T, LANES), F32),
                   jax.ShapeDtypeStruct((T // seq, SUBLANES, seq), F32)]
        + [jax.ShapeDtypeStruct(s.shape, BF16) for s in slabs],
        grid=(T // tm,),
        in_specs=[pl.BlockSpec((tm, D), row),
                  pl.BlockSpec((1, 6, D), lambda i: (i // per_b, 0, 0)),
                  pl.BlockSpec((1, D), const),
                  pl.BlockSpec(w_main.shape, const, pipeline_mode=once),
                  pl.BlockSpec(w_gates.shape, const, pipeline_mode=once),
                  pl.BlockSpec(w_if.shape, const),
                  pl.BlockSpec(w_ift.shape, const)]
        + [pl.BlockSpec((1,) + s.shape[1:], slab) for s in slabs],
        out_specs=[pl.BlockSpec((tm, dc), row),
                   pl.BlockSpec((tm, 2 * dc), row),
                   pl.BlockSpec((tm, dc), row),
                   pl.BlockSpec((tm, dc), row),
                   pl.BlockSpec((tm, D), row),
                   pl.BlockSpec((tm, D), row),
                   pl.BlockSpec((tm, LANES), row),
                   pl.BlockSpec((1, SUBLANES, tm), lambda i: (i // per_b, 0, i % per_b))]
        + [pl.BlockSpec((1,) + s.shape[1:], slab) for s in slabs],
        compiler_params=_params("arbitrary"),
        name="inproj",
    )(x2, mod3, g1, w_main, w_gates, w_if, w_ift, *slabs)
    return outs[:8], [o.reshape(r.shape) for o, r in zip(outs[8:], riders)]


def _conv_kernel(u_ref, w_ref, b_ref, lg_ref, lb_ref, wo_ref, y_ref, ubuf, sbuf, cbuf):
    ts = u_ref.shape[1]
    halo = CONV_HALO

    @pl.when(pl.program_id(1) == 0)
    def _():
        ubuf[0:halo, :] = jnp.zeros((halo, ubuf.shape[1]), F32)

    ubuf[halo:halo + ts, :] = u_ref[0]
    ns = sbuf.shape[1]
    for r in range(1, SUBLANES):
        sbuf[r - 1] = ubuf[r:r + ns, :]
    off = halo - (CONV_WIDTH - 1)
    for r0 in range(0, ts, CONV_RC):
        acc = jnp.broadcast_to(b_ref[...], (CONV_RC, ubuf.shape[1]))
        for k in range(CONV_WIDTH):
            r = (off + k) % SUBLANES
            lo = off + k - r + r0
            win = ubuf[lo:lo + CONV_RC, :] if r == 0 else sbuf[r - 1, lo:lo + CONV_RC, :]
            acc = acc + w_ref[k:k + 1, :] * win
        cbuf[r0:r0 + CONV_RC, :] = acc
    ubuf[0:halo, :] = ubuf[ts:ts + halo, :]

    a = cbuf[...]
    mu = jnp.mean(a, axis=-1, keepdims=True)
    ac = a - mu
    var = jnp.mean(ac * ac, axis=-1, keepdims=True)
    z = ac * lax.rsqrt(var + LN_EPS) * lg_ref[...] + lb_ref[...]
    z = z * _sigmoid(z)
    y_ref[0] = jnp.dot(z.astype(BF16), wo_ref[...], preferred_element_type=F32).astype(BF16)


def _conv_branch(u3, w, b, lg, lb, wo):
    B, S, C = u3.shape
    D = wo.shape[1]
    ts = CONV_TS
    const = lambda bi, si: (0, 0)
    return pl.pallas_call(
        _conv_kernel,
        out_shape=jax.ShapeDtypeStruct((B, S, D), BF16),
        grid=(B, S // ts),
        in_specs=[pl.BlockSpec((1, ts, C), lambda bi, si: (bi, si, 0)),
                  pl.BlockSpec(w.shape, const),
                  pl.BlockSpec((1, C), const),
                  pl.BlockSpec((1, C), const),
                  pl.BlockSpec((1, C), const),
                  pl.BlockSpec(wo.shape, const)],
        out_specs=pl.BlockSpec((1, ts, D), lambda bi, si: (bi, si, 0)),
        scratch_shapes=[pltpu.VMEM((ts + CONV_HALO, C), F32),
                        pltpu.VMEM((SUBLANES - 1, ts + CONV_HALO - SUBLANES, C), F32),
                        pltpu.VMEM((ts, C), F32)],
        compiler_params=_params("arbitrary", "arbitrary"),
        name="conv",
    )(u3, w, b, lg, lb, wo)


def _mlstm_kernel(qk_ref, v_ref, o_ref, ifc_ref, ifr_ref, cw_ref, cb_ref, bifc_ref, bifr_ref, ng_ref,
                  wo_ref, y_ref, qkbuf, cn_ref, m_ref, hbuf):
    @pl.when(pl.program_id(1) == 0)
    def _():
        qkbuf[:, 0:SUBLANES, :] = jnp.zeros((qkbuf.shape[0], SUBLANES, qkbuf.shape[2]), F32)
        cn_ref[...] = jnp.zeros(cn_ref.shape, F32)
        m_ref[...] = jnp.zeros(m_ref.shape, F32)

    nb, L, mi = hbuf.shape
    dh = mi // M_HEADS
    halo = SUBLANES
    off = halo - (QK_CONV_WIDTH - 1)
    rows = lax.broadcasted_iota(I32, (L, L), 0)
    cols = lax.broadcasted_iota(I32, (L, L), 1)
    causal = cols <= rows
    lower = jnp.where(causal, 1.0, 0.0).astype(BF16)
    upper = jnp.where(rows <= cols, 1.0, 0.0).astype(BF16)
    lane = lax.broadcasted_iota(I32, (L, dh), 1)
    ones_col = jnp.where(lane == 0, 1.0, 0.0).astype(BF16)
    scale = dh ** -0.5

    seqs = []
    for b in range(nb):
        qkbuf[b, halo:halo + L, :] = qk_ref[b]
        y = jnp.broadcast_to(cb_ref[...], (L, qkbuf.shape[2]))
        for k in range(QK_CONV_WIDTH):
            y = y + cw_ref[k:k + 1, :] * qkbuf[b, off + k:off + k + L, :]
        y = y * _sigmoid(y)
        qkbuf[b, 0:halo, :] = qkbuf[b, L:L + halo, :]
        ifr = ifr_ref[b] + bifr_ref[...]
        ifc = ifc_ref[b] + bifc_ref[...]
        bcum_c = sum(jnp.dot(lower, p, preferred_element_type=F32) for p in _split_bf16(_log_sigmoid(ifc)))
        bcum_r = sum(jnp.dot(p, upper, preferred_element_type=F32) for p in _split_bf16(_log_sigmoid(ifr)))
        seqs.append((y, ifr, bcum_c, bcum_r))

    probs = [(b, hd) for b in range(nb) for hd in range(M_HEADS)]
    st = {}
    for p in probs:
        b, hd = p
        y, ifr, bcum_c, bcum_r = seqs[b]
        c0 = hd * dh
        qb = (y[:, c0:c0 + dh] * scale).astype(BF16)
        kt = y[:, mi + c0:mi + c0 + dh].T
        v = v_ref[b, :, c0:c0 + dh]
        bc = bcum_c[:, M_HEADS + hd:M_HEADS + hd + 1]
        br = bcum_r[M_HEADS + hd:M_HEADS + hd + 1, :]
        li = ifr[hd:hd + 1, :]
        m_prev = m_ref[b, hd, 0:1, 0:1]
        dmat = jnp.where(causal, bc - br + li, -jnp.inf)
        st[p] = dict(qb=qb, kt=kt, v=v, bc=bc, br=br, li=li, m_prev=m_prev, dmat=dmat)
    for p in probs:
        s = st[p]
        s["inter"] = s["bc"] + s["m_prev"]
        s["m_t"] = jnp.maximum(jnp.max(s["dmat"], axis=-1, keepdims=True), s["inter"])
    for p in probs:
        s = st[p]
        s["qk"] = jnp.dot(s["qb"], s["kt"].astype(BF16), preferred_element_type=F32)
    for p in probs:
        b, hd = p
        s = st[p]
        s["cn"] = cn_ref[b, hd]
        s["qcn"] = jnp.dot(s["qb"], s["cn"].astype(BF16), preferred_element_type=F32)
    for p in probs:
        s = st[p]
        s["wts"] = jnp.exp(s["dmat"] - s["m_t"])
        s["s_inter"] = jnp.exp(s["inter"] - s["m_t"])
    for p in probs:
        s = st[p]
        s["s_mat"] = s["qk"] * s["wts"]
    for p in probs:
        s = st[p]
        s["sv"] = jnp.dot(s["s_mat"].astype(BF16), s["v"], preferred_element_type=F32)
    for p in probs:
        s = st[p]
        s["rowsum"] = jnp.sum(s["s_mat"], axis=-1, keepdims=True)
    for p in probs:
        s = st[p]
        s["num"] = s["sv"] + s["s_inter"] * s["qcn"][:, 0:dh]
        s["den"] = s["rowsum"] + s["s_inter"] * s["qcn"][:, dh:dh + 1]
    for p in probs:
        b, hd = p
        s = st[p]
        b_last = s["br"][:, L - 1:L]
        a = b_last - s["br"] + s["li"]
        m_new = jnp.maximum(b_last + s["m_prev"], jnp.max(a, axis=-1, keepdims=True))
        wk = jnp.exp(a - m_new)
        sc = jnp.exp(b_last + s["m_prev"] - m_new)
        v_ext = jnp.concatenate([s["v"], ones_col], axis=1)
        cn_ref[b, hd] = sc * s["cn"] + jnp.dot((s["kt"] * wk).astype(BF16), v_ext, preferred_element_type=F32)
        m_ref[b, hd] = jnp.broadcast_to(m_new, m_ref.shape[2:])
    for p in probs:
        s = st[p]
        s["hh"] = s["num"] / jnp.maximum(jnp.abs(s["den"]), jnp.exp(-s["m_t"]))
        s["mu"] = jnp.mean(s["hh"], axis=-1, keepdims=True)
    for p in probs:
        s = st[p]
        s["hc"] = s["hh"] - s["mu"]
        s["var"] = jnp.mean(s["hc"] * s["hc"], axis=-1, keepdims=True)
    for p in probs:
        b, hd = p
        s = st[p]
        c0 = hd * dh
        hn = s["hc"] * lax.rsqrt(s["var"] + LN_EPS) * ng_ref[:, c0:c0 + dh]
        hbuf[b, :, c0:c0 + dh] = hn * _sigmoid(o_ref[b, :, c0:c0 + dh])
    for b in range(nb):
        y = jnp.dot(hbuf[b].astype(BF16), wo_ref[...], preferred_element_type=F32)
        y_ref[b] = y.astype(BF16)


def _mlstm_branch(qk3, v3, o3, ifc3, ifr3, cw, cb, bifc, bifr, ng, wo):
    B, S, C2 = qk3.shape
    mi = v3.shape[2]
    dh = mi // M_HEADS
    D = wo.shape[1]
    L = MLSTM_L
    nb = MLSTM_SEQS
    const = lambda bi, ci: (0, 0)
    tile = lambda bi, ci: (bi, ci, 0)
    return pl.pallas_call(
        _mlstm_kernel,
        out_shape=jax.ShapeDtypeStruct((B, S, D), BF16),
        grid=(B // nb, S // L),
        in_specs=[pl.BlockSpec((nb, L, C2), tile),
                  pl.BlockSpec((nb, L, mi), tile),
                  pl.BlockSpec((nb, L, mi), tile),
                  pl.BlockSpec((nb, L, LANES), tile),
                  pl.BlockSpec((nb, SUBLANES, L), lambda bi, ci: (bi, 0, ci)),
                  pl.BlockSpec(cw.shape, const),
                  pl.BlockSpec((1, C2), const),
                  pl.BlockSpec((1, LANES), const),
                  pl.BlockSpec((SUBLANES, 1), const),
                  pl.BlockSpec((1, mi), const),
                  pl.BlockSpec(wo.shape, const)],
        out_specs=pl.BlockSpec((nb, L, D), tile),
        scratch_shapes=[pltpu.VMEM((nb, L + SUBLANES, C2), F32),
                        pltpu.VMEM((nb, M_HEADS, dh, 2 * dh), F32),
                        pltpu.VMEM((nb, M_HEADS, SUBLANES, LANES), F32),
                        pltpu.VMEM((nb, L, mi), F32)],
        compiler_params=_params("arbitrary", "arbitrary"),
        name="mlstm",
    )(qk3, v3, o3, ifc3, ifr3, cw, cb, bifc, bifr, ng, wo)


def _merge_kernel(x_ref, ya_ref, yb_ref, sga_ref, sgb_ref, mod_ref, g2_ref, wo_ref, wr_ref, br_ref,
                  x1_ref, h2_ref, ri_ref, rf_ref, cnt_ref, run_ref):
    ts = MERGE_SUB
    subs = [pl.ds(r0, ts) for r0 in range(0, x_ref.shape[0], ts)]

    @pl.when(pl.program_id(0) == 0)
    def _():
        run_ref[...] = jnp.zeros(run_ref.shape, F32)

    gate1 = mod_ref[0, 2:3, :]
    shift2 = mod_ref[0, 3:4, :]
    scale2 = mod_ref[0, 4:5, :]
    lane = lax.broadcasted_iota(I32, (ts, LANES), 1).astype(F32)
    neg = -jnp.inf
    rows = lax.broadcasted_iota(I32, (ts, ts), 0)
    cols = lax.broadcasted_iota(I32, (ts, ts), 1)
    strict = jnp.where(cols < rows, 1.0, 0.0).astype(BF16)

    def first_argmax(vals):
        mx = jnp.max(vals, axis=-1, keepdims=True)
        idx = jnp.min(jnp.where(vals == mx, lane, float(LANES)), axis=-1, keepdims=True)
        return mx, idx

    h2s = []
    for sl in subs:
        merged = (sga_ref[sl, :].astype(F32) * ya_ref[sl, :].astype(F32)
                  + sgb_ref[sl, :].astype(F32) * yb_ref[sl, :].astype(F32))
        mix = jnp.dot(merged.astype(BF16), wo_ref[...], preferred_element_type=F32)
        x1 = x_ref[sl, :] + gate1 * mix
        x1_ref[sl, :] = x1
        ms = jnp.mean(x1 * x1, axis=-1, keepdims=True)
        h2 = x1 * lax.rsqrt(ms + RMS_EPS) * g2_ref[...]
        h2 = h2 * (1.0 + scale2) + shift2
        h2_ref[sl, :] = _pack_bf16_pairs(h2)
        h2s.append(h2.astype(BF16))

    run = run_ref[0:1, :]
    for sl, h2b in zip(subs, h2s):
        logits = jnp.dot(h2b, wr_ref[...], preferred_element_type=F32) + br_ref[...]
        lg = jnp.where(lane < N_GROUPS, logits, neg)
        gmax, gsel = first_argmax(lg)
        p_g = 1.0 / jnp.sum(jnp.exp(lg - gmax), axis=-1, keepdims=True)
        lo = N_GROUPS + gsel * E_PER_GROUP
        le = jnp.where((lane >= lo) & (lane < lo + E_PER_GROUP), logits, neg)
        l1, i1 = first_argmax(le)
        l2, i2 = first_argmax(jnp.where(lane == i1, neg, le))
        r = jnp.exp(l2 - l1)
        w1 = p_g / (1.0 + r)
        w2 = p_g * r / (1.0 + r)
        e1 = i1 - N_GROUPS
        e2 = i2 - N_GROUPS

        onehot = jnp.where((lane == e1) | (lane == e2), 1.0, 0.0)
        before = jnp.dot(strict, onehot.astype(BF16), preferred_element_type=F32) + run
        rank1 = jnp.sum(jnp.where(lane == e1, before, 0.0), axis=-1, keepdims=True)
        rank2 = jnp.sum(jnp.where(lane == e2, before, 0.0), axis=-1, keepdims=True)
        run = run + jnp.sum(onehot, axis=0, keepdims=True)

        codes = jnp.where(lane == 0, e1 * float(RANK_RADIX) + rank1,
                          jnp.where(lane == 1, e2 * float(RANK_RADIX) + rank2, 0.0))
        ri_ref[:, sl] = codes.T[0:SUBLANES, :].astype(I32)
        rf_ref[sl, :] = jnp.where(lane == 0, w1, jnp.where(lane == 1, w2, 0.0))
    run_ref[...] = jnp.broadcast_to(run, run_ref.shape)
    cnt_ref[...] = jnp.broadcast_to(run, cnt_ref.shape).astype(I32)


def _merge(x2, ya, yb, sga, sgb, mod3, g2, wo, wr, br, seq):
    T, D = x2.shape
    tm = MERGE_TM
    per_b = seq // tm
    row = lambda i: (i, 0)
    const = lambda i: (0, 0)
    return pl.pallas_call(
        _merge_kernel,
        out_shape=[jax.ShapeDtypeStruct((T, D), F32),
                   jax.ShapeDtypeStruct((T, D // 2), I32),
                   jax.ShapeDtypeStruct((SUBLANES, T), I32),
                   jax.ShapeDtypeStruct((T, LANES), F32),
                   jax.ShapeDtypeStruct((SUBLANES, LANES), I32)],
        grid=(T // tm,),
        in_specs=[pl.BlockSpec((tm, D), row),
                  pl.BlockSpec((tm, D), row),
                  pl.BlockSpec((tm, D), row),
                  pl.BlockSpec((tm, D), row),
                  pl.BlockSpec((tm, D), row),
                  pl.BlockSpec((1, 6, D), lambda i: (i // per_b, 0, 0)),
                  pl.BlockSpec((1, D), const),
                  pl.BlockSpec(wo.shape, const),
                  pl.BlockSpec(wr.shape, const),
                  pl.BlockSpec((1, LANES), const)],
        out_specs=[pl.BlockSpec((tm, D), row),
                   pl.BlockSpec((tm, D // 2), row),
                   pl.BlockSpec((SUBLANES, tm), lambda i: (0, i)),
                   pl.BlockSpec((tm, LANES), row),
                   pl.BlockSpec((SUBLANES, LANES), const)],
        scratch_shapes=[pltpu.VMEM((SUBLANES, LANES), F32)],
        compiler_params=_params("arbitrary"),
        name="merge",
    )(x2, ya, yb, sga, sgb, mod3, g2, wo, wr, br)


def _sc_workers():
    info = plsc.get_sparse_core_info()
    mesh = plsc.VectorSubcoreMesh(core_axis_name="core", subcore_axis_name="subcore")
    params = pltpu.CompilerParams()
    if "needs_layout_passes" in pltpu.CompilerParams.__dataclass_fields__:
        params = dataclasses.replace(params, needs_layout_passes=False)
    return info, mesh, params


def _rows_from_codes(code_v, base_v, idx_v, lanes):
    for j in range(code_v.shape[0] // lanes):
        c = code_v[pl.ds(j * lanes, lanes)]
        expert = lax.shift_right_logical(c, RANK_BITS)
        idx_v[pl.ds(j * lanes, lanes)] = plsc.load_gather(base_v, [expert]) + (c & (RANK_RADIX - 1))


def _two_slot_loop(n_chunks, start, finish):
    start(0, 0)

    @pl.loop(0, n_chunks, step=2)
    def _(c):
        start(c + 1, 1)
        finish(c, 0)

        @pl.when(c + 2 < n_chunks)
        def _():
            start(c + 2, 0)

        finish(c + 1, 1)


def _sc_dispatch(h2, code0, code1, base, n_rows):
    T, D = h2.shape
    info, mesh, params = _sc_workers()
    n_workers = info.num_cores * info.num_subcores
    w = SC_WINDOW_BYTES // (D * h2.dtype.itemsize)
    per_w = T // n_workers
    n_chunks = per_w // w
    assert per_w * n_workers == T and n_chunks * w == per_w and n_chunks % 2 == 0

    @functools.partial(
        pl.kernel, out_type=jax.ShapeDtypeStruct((n_rows, D), h2.dtype), mesh=mesh, compiler_params=params,
        scratch_types=[pltpu.VMEM((N_EXPERTS,), I32), pltpu.VMEM((w,), I32), pltpu.VMEM((w,), I32),
                       pltpu.VMEM((w,), I32), pltpu.VMEM((w, D), h2.dtype), pltpu.VMEM((w, D), h2.dtype),
                       pltpu.SemaphoreType.DMA, pltpu.SemaphoreType.DMA])
    def scatter(h_hbm, c0_hbm, c1_hbm, b_hbm, xs_hbm, base_v, code_v, i0_v, i1_v, rows0, rows1, sem0, sem1):
        wid = lax.axis_index("subcore") * info.num_cores + lax.axis_index("core")
        w0 = wid * per_w
        pltpu.sync_copy(b_hbm, base_v)
        rows = (rows0, rows1)
        sems = (sem0, sem1)

        def start(c, slot):
            pltpu.async_copy(h_hbm.at[pl.ds(w0 + c * w, w)], rows[slot], sems[slot])

        def finish(c, slot):
            pltpu.sync_copy(c0_hbm.at[pl.ds(w0 + c * w, w)], code_v)
            _rows_from_codes(code_v, base_v, i0_v, info.num_lanes)
            pltpu.sync_copy(c1_hbm.at[pl.ds(w0 + c * w, w)], code_v)
            _rows_from_codes(code_v, base_v, i1_v, info.num_lanes)
            pltpu.make_async_copy(h_hbm.at[pl.ds(w0 + c * w, w)], rows[slot], sems[slot]).wait()
            pltpu.sync_copy(rows[slot], xs_hbm.at[i0_v])
            pltpu.sync_copy(rows[slot], xs_hbm.at[i1_v])

        _two_slot_loop(n_chunks, start, finish)

    return scatter(h2, code0, code1, base)


def _sc_collect(ys, codes, base):
    n = codes.shape[0]
    D = ys.shape[1]
    info, mesh, params = _sc_workers()
    n_workers = info.num_cores * info.num_subcores
    w = SC_WINDOW_BYTES // (D * ys.dtype.itemsize)
    per_w = n // n_workers
    n_chunks = per_w // w
    assert per_w * n_workers == n and n_chunks * w == per_w and n_chunks % 2 == 0

    @functools.partial(
        pl.kernel, out_type=jax.ShapeDtypeStruct((n, D), ys.dtype), mesh=mesh, compiler_params=params,
        scratch_types=[pltpu.VMEM((N_EXPERTS,), I32), pltpu.VMEM((w,), I32), pltpu.VMEM((w,), I32),
                       pltpu.VMEM((w,), I32), pltpu.VMEM((w, D), ys.dtype), pltpu.VMEM((w, D), ys.dtype),
                       pltpu.SemaphoreType.DMA, pltpu.SemaphoreType.DMA])
    def gather(ys_hbm, c_hbm, b_hbm, yk_hbm, base_v, code_v, i0_v, i1_v, rows0, rows1, sem0, sem1):
        wid = lax.axis_index("subcore") * info.num_cores + lax.axis_index("core")
        w0 = wid * per_w
        pltpu.sync_copy(b_hbm, base_v)
        idx = (i0_v, i1_v)
        rows = (rows0, rows1)
        sems = (sem0, sem1)

        def start(c, slot):
            pltpu.sync_copy(c_hbm.at[pl.ds(w0 + c * w, w)], code_v)
            _rows_from_codes(code_v, base_v, idx[slot], info.num_lanes)
            pltpu.async_copy(ys_hbm.at[idx[slot]], rows[slot], sems[slot])

        def finish(c, slot):
            pltpu.make_async_copy(ys_hbm.at[idx[slot]], rows[slot], sems[slot]).wait()
            pltpu.sync_copy(rows[slot], yk_hbm.at[pl.ds(w0 + c * w, w)])

        _two_slot_loop(n_chunks, start, finish)

    return gather(ys, codes, base)


def _schedule_kernel(cnt_ref, te_ref, tb_ref, base_ref, nt_ref, tr_ref):
    tm = EXPERT_TM

    def expert(e, t0):
        n = (cnt_ref[e] + tm - 1) // tm
        base_ref[e] = t0 * tm

        def tile(t, c):
            te_ref[t] = e
            tb_ref[t] = t
            tr_ref[t] = jnp.minimum(cnt_ref[e] - (t - t0) * tm, tm)
            return c

        lax.fori_loop(t0, t0 + n, tile, 0)
        return t0 + n

    nt = lax.fori_loop(0, N_EXPERTS, expert, 0)
    nt_ref[0] = nt
    last = te_ref[nt - 1]

    def idle(t, c):
        te_ref[t] = last
        tb_ref[t] = nt - 1
        tr_ref[t] = 0
        return c

    lax.fori_loop(nt, te_ref.shape[0], idle, 0)


def _schedule(counts, max_tiles):
    smem = pl.BlockSpec(memory_space=pltpu.SMEM)
    return pl.pallas_call(
        _schedule_kernel,
        out_shape=[jax.ShapeDtypeStruct((max_tiles,), I32),
                   jax.ShapeDtypeStruct((max_tiles,), I32),
                   jax.ShapeDtypeStruct((N_EXPERTS,), I32),
                   jax.ShapeDtypeStruct((1,), I32),
                   jax.ShapeDtypeStruct((max_tiles,), I32)],
        in_specs=[smem],
        out_specs=[smem, smem, smem, smem, smem],
        name="schedule",
    )(counts)


def _expert_kernel(te_ref, tb_ref, tr_ref, xs_ref, wg_ref, wu_ref, wd_ref, ys_ref):
    rows = tr_ref[pl.program_id(0)]
    half = xs_ref.shape[0] // 2

    for r0 in (0, half):
        @pl.when(rows > r0)
        def _():
            xb = _unpack_bf16_pairs(xs_ref[r0:r0 + half, :])
            g = jnp.dot(xb, wg_ref[0], preferred_element_type=F32)
            u = jnp.dot(xb, wu_ref[0], preferred_element_type=F32)
            act = (g * _sigmoid(g)) * u
            ys_ref[r0:r0 + half, :] = _pack_bf16_pairs(
                jnp.dot(act.astype(BF16), wd_ref[0], preferred_element_type=F32))


def _experts(tile_e, tile_b, tile_rows, xs, wg, wu, wd, max_tiles):
    P = xs.shape[0]
    D, de = wg.shape[1:]
    tm = EXPERT_TM
    wmap = lambda j, te, tb, nt: (te[j], 0, 0)
    rmap = lambda j, te, tb, nt: (tb[j], 0)
    return pl.pallas_call(
        _expert_kernel,
        out_shape=jax.ShapeDtypeStruct(xs.shape, I32),
        grid_spec=pltpu.PrefetchScalarGridSpec(
            num_scalar_prefetch=3,
            grid=(max_tiles,),
            in_specs=[pl.BlockSpec((tm, xs.shape[1]), rmap),
                      pl.BlockSpec((1, D, de), wmap),
                      pl.BlockSpec((1, D, de), wmap),
                      pl.BlockSpec((1, de, D), wmap)],
            out_specs=pl.BlockSpec((tm, xs.shape[1]), rmap)),
        compiler_params=_params("arbitrary"),
        name="experts",
    )(tile_e, tile_b, tile_rows, xs, wg, wu, wd)


def _combine_kernel(x1_ref, rf_ref, mod_ref, gf_ref, y0_ref, y1_ref, out_ref, *, final_norm):
    gate2 = mod_ref[0, 5:6, :]
    w = rf_ref[...]
    moe = w[:, 0:1] * _unpack_pairs_f32(y0_ref[...]) + w[:, 1:2] * _unpack_pairs_f32(y1_ref[...])
    x2 = x1_ref[...] + gate2 * moe
    if final_norm:
        ms = jnp.mean(x2 * x2, axis=-1, keepdims=True)
        x2 = x2 * lax.rsqrt(ms + RMS_EPS) * gf_ref[...]
    out_ref[...] = x2


def _combine(x1, rf, mod3, gf, yk, seq, final_norm):
    T, D = x1.shape
    tc = COMBINE_TM
    per_b = seq // tc
    n_blk = T // tc
    return pl.pallas_call(
        functools.partial(_combine_kernel, final_norm=final_norm),
        out_shape=jax.ShapeDtypeStruct((T, D), F32),
        grid=(n_blk,),
        in_specs=[pl.BlockSpec((tc, D), lambda i: (i, 0)),
                  pl.BlockSpec((tc, LANES), lambda i: (i, 0)),
                  pl.BlockSpec((1, 6, D), lambda i: (i // per_b, 0, 0)),
                  pl.BlockSpec((1, D), lambda i: (0, 0)),
                  pl.BlockSpec((tc, yk.shape[1]), lambda i: (i, 0)),
                  pl.BlockSpec((tc, yk.shape[1]), lambda i: (n_blk + i, 0))],
        out_specs=pl.BlockSpec((tc, D), lambda i: (i, 0)),
        compiler_params=_params("arbitrary"),
        name="combine",
    )(x1, rf, mod3, gf, yk, yk)


def _layer(x2, c, seq, layer, w_ada, b_ada, g_norm1, w_in, b_if, conv_dw_w, conv_dw_b, conv_ln_g, conv_ln_b,
           w_conv_out, qk_conv_w, qk_conv_b, m_norm_g, w_m_out, w_out, g_norm2, w_rg, b_rg,
           w_re, b_re, w_e_gate, w_e_up, w_e_down):
    T, D = x2.shape
    B = T // seq
    dc = D // 2
    nif = 2 * M_HEADS

    if_lo = 6 * dc
    w_in_t = jnp.swapaxes(w_in, 1, 2)
    w_main = w_in_t[layer, :if_lo, :].astype(BF16)
    w_gates = w_in_t[layer, if_lo + nif:, :].astype(BF16)
    full = lambda w: (w, [(0, w.shape[2])])
    mod, (w_out_b, w_conv_out_b, w_m_out_b) = _ada(
        c, w_ada, b_ada, layer, [full(w_out), full(w_conv_out), full(w_m_out)])
    mod3 = mod.reshape(B, 6, D)
    w_if = w_in[layer, :, if_lo:if_lo + nif]
    w_if_pad = jnp.pad(w_if, ((0, 0), (0, LANES - nif))).astype(BF16)
    w_ift = w_if.T.astype(BF16)
    (u, qk, v, o, sga, sgb, ifc, ifr), (wg_b, wu_b, wd_b) = _inproj(
        x2, mod3, g_norm1.reshape(1, D), w_main, w_gates, w_if_pad, w_ift, seq,
        riders=(w_e_gate, w_e_up, w_e_down))

    ya = _conv_branch(
        u.reshape(B, seq, dc), conv_dw_w, conv_dw_b.reshape(1, dc), conv_ln_g.reshape(1, dc),
        conv_ln_b.reshape(1, dc), w_conv_out_b)
    bifc = jnp.pad(b_if, (0, LANES - nif)).reshape(1, LANES)
    bifr = b_if.reshape(nif, 1)
    yb = _mlstm_branch(
        qk.reshape(B, seq, 2 * dc), v.reshape(B, seq, dc), o.reshape(B, seq, dc),
        ifc.reshape(B, seq, LANES), ifr, qk_conv_w, qk_conv_b.reshape(1, 2 * dc), bifc, bifr,
        m_norm_g.reshape(1, dc), w_m_out_b)

    n_r = N_GROUPS + N_EXPERTS
    w_r = jnp.pad(jnp.concatenate([w_rg, w_re], axis=1), ((0, 0), (0, LANES - n_r))).astype(BF16)
    b_r = jnp.pad(jnp.concatenate([b_rg, b_re]), (0, LANES - n_r)).reshape(1, LANES)
    x1, h2, ri, rf, cnt = _merge(x2, ya.reshape(T, D), yb.reshape(T, D), sga, sgb, mod3,
                                 g_norm2.reshape(1, D), w_out_b, w_r, b_r, seq)

    tm = EXPERT_TM
    max_tiles = (T * TOP_K) // tm + N_EXPERTS
    tile_e, tile_b, base, _, tile_rows = _schedule(cnt[0, :N_EXPERTS], max_tiles)

    code0 = ri[0]
    code1 = ri[1]
    xs = _sc_dispatch(h2, code0, code1, base, max_tiles * tm)
    ys = _experts(tile_e, tile_b, tile_rows, xs, wg_b, wu_b, wd_b, max_tiles)
    return x1, rf, mod3, ys, code0, code1, base


def kernel(x, c, w_ada, b_ada, g_norm1, w_in, b_if, conv_dw_w, conv_dw_b, conv_ln_g, conv_ln_b,
           w_conv_out, qk_conv_w, qk_conv_b, m_norm_g, w_m_out, w_out, g_norm2, w_rg, b_rg,
           w_re, b_re, w_e_gate, w_e_up, w_e_down, g_final):
    B, S, D = x.shape
    depth = w_ada.shape[0]
    x2 = x.reshape(B * S, D)
    for l in range(depth):
        x1, rf, mod3, ys, code0, code1, base = _layer(
            x2, c, S, l, w_ada[l], b_ada[l], g_norm1[l], w_in, b_if[l], conv_dw_w[l], conv_dw_b[l],
            conv_ln_g[l], conv_ln_b[l], w_conv_out, qk_conv_w[l], qk_conv_b[l], m_norm_g[l],
            w_m_out, w_out, g_norm2[l], w_rg[l], b_rg[l], w_re[l], b_re[l],
            w_e_gate[l], w_e_up[l], w_e_down[l])
        yk = _sc_collect(ys, jnp.concatenate([code0, code1]), base)
        x2 = _combine(x1, rf, mod3, g_final.reshape(1, D), yk, S, final_norm=l == depth - 1)
    return x2.reshape(B, S, D)
```
